```python
import math
import jax, jax.numpy as jnp
from jax import lax
import numpy as np

D_MODEL = 1024
BATCH = 8
SEQ = 4096
DEPTH = 1

N_META = 16
SB_HEADS = 8
SB_HEAD_DIM = 64
SB_WIDTH = SB_HEADS * SB_HEAD_DIM
CONV_WIDTH = D_MODEL - SB_WIDTH
CONV_KERNEL = 31
MIX_WIDTH = SB_WIDTH + CONV_WIDTH
IN_COLS = 3 * SB_WIDTH + 2 * CONV_WIDTH
Q_BLOCK = 128
N_EXPERTS = 32
TOP_K = 4
D_EXPERT = D_MODEL
SWIGLU_LIMIT = 7.0
SWIGLU_ALPHA = 1.702
EXPERT_BLOCK = 256
EPS = 1e-6

kernel_name = "hymba_stickbreak_conformer_moe"


def rms_norm(x, g):
    xf = x.astype(jnp.float32)
    y = xf * lax.rsqrt(jnp.mean(xf * xf, axis=-1, keepdims=True) + EPS)
    return (y * g.astype(jnp.float32)).astype(x.dtype)


def layer_norm(x, g, b):
    xf = x.astype(jnp.float32)
    mu = jnp.mean(xf, axis=-1, keepdims=True)
    var = jnp.mean(jnp.square(xf - mu), axis=-1, keepdims=True)
    y = (xf - mu) * lax.rsqrt(var + EPS)
    return (y * g.astype(jnp.float32) + b.astype(jnp.float32)).astype(x.dtype)


def stick_breaking_block(q_blk, k, v, q_pos, k_pos):
    z = jnp.einsum("bhqd,bhkd->bhqk", q_blk, k).astype(jnp.float32) * (1.0 / math.sqrt(SB_HEAD_DIM))
    mask = k_pos[None, :] < q_pos[:, None]
    log_beta = jax.nn.log_sigmoid(z)
    log_keep = jnp.where(mask, jax.nn.log_sigmoid(-z), 0.0)
    later = lax.cumsum(log_keep, axis=3, reverse=True) - log_keep
    a = jnp.where(mask, jnp.exp(log_beta + later), 0.0)
    return jnp.einsum("bhqk,bhkd->bhqd", a.astype(v.dtype), v)


def stick_breaking_attention(q, k, v):
    b, h, l, dh = q.shape
    meta_pos = jnp.arange(N_META)
    meta_out = stick_breaking_block(q[:, :, :N_META], k[:, :, :N_META], v[:, :, :N_META], meta_pos, meta_pos)
    n_blocks = (l - N_META) // Q_BLOCK
    k_pos = jnp.arange(l)

    def one_block(i):
        start = N_META + i * Q_BLOCK
        q_blk = lax.dynamic_slice_in_dim(q, start, Q_BLOCK, axis=2)
        q_pos = start + jnp.arange(Q_BLOCK)
        return stick_breaking_block(q_blk, k, v, q_pos, k_pos)

    out = lax.map(one_block, jnp.arange(n_blocks))
    out = jnp.moveaxis(out, 0, 2).reshape(b, h, n_blocks * Q_BLOCK, dh)
    return jnp.concatenate([meta_out, out], axis=2)


def conformer_conv(u, conv_w, conv_b, ln_g, ln_b):
    val, gate = jnp.split(u, 2, axis=-1)
    hcv = val * jax.nn.sigmoid(gate)
    hcv = lax.conv_general_dilated(
        hcv, conv_w.astype(hcv.dtype)[:, None, :],
        window_strides=(1,), padding=[(CONV_KERNEL - 1, 0)],
        dimension_numbers=("NWC", "WIO", "NWC"),
        feature_group_count=CONV_WIDTH) + conv_b
    hcv = layer_norm(hcv, ln_g, ln_b)
    return jax.nn.silu(hcv)


def moe_ffn(h, w_router, b_router, w_up, b_up, w_down, b_down):
    t, d = h.shape
    logits = (h @ w_router + b_router).astype(jnp.float32)
    top_logits, top_idx = lax.top_k(logits, TOP_K)
    gates = jax.nn.softmax(top_logits, axis=-1)
    n_assign = t * TOP_K
    expert_of = top_idx.reshape(-1).astype(jnp.int32)
    token_of = jnp.arange(n_assign, dtype=jnp.int32) // TOP_K
    gate_of = gates.reshape(-1)
    order = jnp.argsort(expert_of)
    e_sorted = expert_of[order]
    counts = jnp.zeros((N_EXPERTS,), jnp.int32).at[expert_of].add(1)
    padded = (counts + EXPERT_BLOCK - 1) // EXPERT_BLOCK * EXPERT_BLOCK
    start = jnp.cumsum(counts) - counts
    pad_end = jnp.cumsum(padded)
    pad_start = pad_end - padded
    dest = pad_start[e_sorted] + jnp.arange(n_assign, dtype=jnp.int32) - start[e_sorted]
    n_blocks = -(-n_assign // EXPERT_BLOCK) + N_EXPERTS
    n_rows = n_blocks * EXPERT_BLOCK
    row_token = jnp.full((n_rows,), t, jnp.int32).at[dest].set(token_of[order])
    row_gate = jnp.zeros((n_rows,), jnp.float32).at[dest].set(gate_of[order])
    block_expert = jnp.minimum(
        jnp.searchsorted(pad_end, jnp.arange(n_blocks, dtype=jnp.int32) * EXPERT_BLOCK, side="right"),
        N_EXPERTS - 1)
    h_pad = jnp.concatenate([h, jnp.zeros((1, d), h.dtype)], axis=0)
    x_rows = h_pad[row_token].reshape(n_blocks, EXPERT_BLOCK, d)

    def expert_block(args):
        xb, e = args
        up = xb @ w_up[e] + b_up[e]
        glu = jnp.minimum(up[:, :D_EXPERT], SWIGLU_LIMIT)
        lin = jnp.clip(up[:, D_EXPERT:], -SWIGLU_LIMIT, SWIGLU_LIMIT)
        act = glu * jax.nn.sigmoid(SWIGLU_ALPHA * glu) * (lin + 1.0)
        return act @ w_down[e] + b_down[e]

    y_rows = lax.map(expert_block, (x_rows, block_expert)).reshape(n_rows, d)
    y = jnp.zeros((t + 1, d), jnp.float32).at[row_token].add(y_rows.astype(jnp.float32) * row_gate[:, None])
    return y[:t].astype(h.dtype)


def hybrid_layer(x, norm1_g, w_in, q_norm_g, k_norm_g, conv_w, conv_b, conv_ln_g, conv_ln_b,
                 attn_out_g, w_out, norm2_g, w_router, b_router, w_up, b_up, w_down, b_down):
    b, l, d = x.shape
    n = rms_norm(x, norm1_g)
    proj = n @ w_in
    q, k, v, u = jnp.split(proj, [SB_WIDTH, 2 * SB_WIDTH, 3 * SB_WIDTH], axis=-1)
    q = rms_norm(q.reshape(b, l, SB_HEADS, SB_HEAD_DIM), q_norm_g).transpose(0, 2, 1, 3)
    k = rms_norm(k.reshape(b, l, SB_HEADS, SB_HEAD_DIM), k_norm_g).transpose(0, 2, 1, 3)
    v = v.reshape(b, l, SB_HEADS, SB_HEAD_DIM).transpose(0, 2, 1, 3)
    sb = stick_breaking_attention(q, k, v).transpose(0, 2, 1, 3)
    sb = rms_norm(sb, attn_out_g).reshape(b, l, SB_WIDTH)
    cv = conformer_conv(u, conv_w, conv_b, conv_ln_g, conv_ln_b)
    x = x + jnp.concatenate([sb, cv], axis=-1) @ w_out
    hn = rms_norm(x, norm2_g).reshape(b * l, d)
    return x + moe_ffn(hn, w_router, b_router, w_up, b_up, w_down, b_down).reshape(b, l, d)


def setup_inputs(seed: int = 0) -> dict:
    key = jax.random.key(seed)
    ks = jax.random.split(key, 20)
    f32 = jnp.float32
    nrm = lambda k, shape, scale: jax.random.normal(k, shape, f32) * scale
    return {
        "x": nrm(ks[0], (BATCH, SEQ, D_MODEL), 1.0),
        "meta_tokens": nrm(ks[1], (N_META, D_MODEL), 1.0),
        "norm1_g": 1.0 + nrm(ks[2], (DEPTH, D_MODEL), 0.01),
        "w_in": nrm(ks[3], (DEPTH, D_MODEL, IN_COLS), D_MODEL ** -0.5),
        "q_norm_g": 1.0 + nrm(ks[4], (DEPTH, SB_HEAD_DIM), 0.01),
        "k_norm_g": 1.0 + nrm(ks[5], (DEPTH, SB_HEAD_DIM), 0.01),
        "conv_w": nrm(ks[6], (DEPTH, CONV_KERNEL, CONV_WIDTH), CONV_KERNEL ** -0.5),
        "conv_b": nrm(ks[7], (DEPTH, CONV_WIDTH), 0.01),
        "conv_ln_g": 1.0 + nrm(ks[8], (DEPTH, CONV_WIDTH), 0.01),
        "conv_ln_b": nrm(ks[9], (DEPTH, CONV_WIDTH), 0.01),
        "attn_out_g": 1.0 + nrm(ks[10], (DEPTH, SB_HEADS, SB_HEAD_DIM), 0.01),
        "w_out": nrm(ks[11], (DEPTH, MIX_WIDTH, D_MODEL), MIX_WIDTH ** -0.5),
        "norm2_g": 1.0 + nrm(ks[12], (DEPTH, D_MODEL), 0.01),
        "w_router": nrm(ks[13], (DEPTH, D_MODEL, N_EXPERTS), D_MODEL ** -0.5),
        "b_router": nrm(ks[14], (DEPTH, N_EXPERTS), 0.01),
        "w_up": nrm(ks[15], (DEPTH, N_EXPERTS, D_MODEL, 2 * D_EXPERT), D_MODEL ** -0.5),
        "b_up": nrm(ks[16], (DEPTH, N_EXPERTS, 2 * D_EXPERT), 0.01),
        "w_down": nrm(ks[17], (DEPTH, N_EXPERTS, D_EXPERT, D_MODEL), D_EXPERT ** -0.5),
        "b_down": nrm(ks[18], (DEPTH, N_EXPERTS, D_MODEL), 0.01),
    }


def reference(x, meta_tokens, norm1_g, w_in, q_norm_g, k_norm_g, conv_w, conv_b, conv_ln_g,
              conv_ln_b, attn_out_g, w_out, norm2_g, w_router, b_router, w_up, b_up, w_down, b_down):
    b = x.shape[0]
    meta = jnp.broadcast_to(meta_tokens.astype(x.dtype)[None], (b, N_META, x.shape[-1]))
    h = jnp.concatenate([meta, x], axis=1)
    for layer in range(DEPTH):
        h = hybrid_layer(h, norm1_g[layer], w_in[layer], q_norm_g[layer], k_norm_g[layer],
                         conv_w[layer], conv_b[layer], conv_ln_g[layer], conv_ln_b[layer],
                         attn_out_g[layer], w_out[layer], norm2_g[layer], w_router[layer],
                         b_router[layer], w_up[layer], b_up[layer], w_down[layer], b_down[layer])
    return h[:, N_META:]
```

```python
import functools

import jax
import jax.numpy as jnp
from jax import lax
from jax.experimental import pallas as pl
from jax.experimental.pallas import tpu as pltpu

N_META = 16
SB_HEADS = 8
SB_HEAD_DIM = 64
SB_WIDTH = SB_HEADS * SB_HEAD_DIM
CONV_KERNEL = 31
N_EXPERTS = 32
TOP_K = 4
SWIGLU_LIMIT = 7.0
SWIGLU_ALPHA = 1.702
EPS = 1e-6

LANES = 128
SUBLANES = 8
HALO = 32
EXPERT_ROWS = 256
VMEM_LIMIT = 56 * 1024 * 1024

F32 = jnp.float32
BF16 = jnp.bfloat16


def _params(n_axes, vmem=VMEM_LIMIT):
    return pltpu.CompilerParams(dimension_semantics=("arbitrary",) * n_axes, vmem_limit_bytes=vmem)


def _inproj_kernel(x_ref, g1_ref, w_ref, qg_ref, kg_ref, q_ref, k_ref, v_ref, h_ref):
    x = x_ref[...]
    ms = jnp.mean(x * x, axis=-1, keepdims=True)
    n = (x * lax.rsqrt(ms + EPS) * g1_ref[...]).astype(BF16)
    lo = lax.broadcasted_iota(jnp.int32, (1, LANES), 1) < SB_HEAD_DIM

    def proj(c0, c1):
        return jnp.dot(n, w_ref[:, c0:c1], preferred_element_type=F32)

    def head_norm(acc, g_ref, out_ref, scale):
        for c in range(SB_WIDTH // LANES):
            sl = slice(c * LANES, (c + 1) * LANES)
            a = acc[:, sl]
            sq = a * a
            s_lo = jnp.sum(jnp.where(lo, sq, 0.0), axis=-1, keepdims=True)
            s_hi = jnp.sum(jnp.where(lo, 0.0, sq), axis=-1, keepdims=True)
            r = lax.rsqrt(jnp.where(lo, s_lo, s_hi) * (1.0 / SB_HEAD_DIM) + EPS)
            out_ref[:, sl] = (a * r * (g_ref[:, sl] * scale)).astype(BF16)

    head_norm(proj(0, SB_WIDTH), qg_ref, q_ref, SB_HEAD_DIM ** -0.5)
    head_norm(proj(SB_WIDTH, 2 * SB_WIDTH), kg_ref, k_ref, 1.0)
    v_ref[...] = proj(2 * SB_WIDTH, 3 * SB_WIDTH).astype(BF16)
    cw = (w_ref.shape[1] - 3 * SB_WIDTH) // 2
    val = proj(3 * SB_WIDTH, 3 * SB_WIDTH + cw)
    gate = proj(3 * SB_WIDTH + cw, 3 * SB_WIDTH + 2 * cw)
    h_ref[...] = val * jax.nn.sigmoid(gate)


def _inproj(x2, g1, w_in_bf, qg, kg, tm):
    t, d = x2.shape
    cols = w_in_bf.shape[1]
    cw = (cols - 3 * SB_WIDTH) // 2
    row = lambda i: (i, 0)
    fixed = lambda i: (0, 0)
    return pl.pallas_call(
        _inproj_kernel,
        grid=(t // tm,),
        in_specs=[
            pl.BlockSpec((tm, d), row),
            pl.BlockSpec((1, d), fixed),
            pl.BlockSpec((d, cols), fixed),
            pl.BlockSpec((1, SB_WIDTH), fixed),
            pl.BlockSpec((1, SB_WIDTH), fixed),
        ],
        out_specs=[
            pl.BlockSpec((tm, SB_WIDTH), row),
            pl.BlockSpec((tm, SB_WIDTH), row),
            pl.BlockSpec((tm, SB_WIDTH), row),
            pl.BlockSpec((tm, cw), row),
        ],
        out_shape=[
            jax.ShapeDtypeStruct((t, SB_WIDTH), BF16),
            jax.ShapeDtypeStruct((t, SB_WIDTH), BF16),
            jax.ShapeDtypeStruct((t, SB_WIDTH), BF16),
            jax.ShapeDtypeStruct((t, cw), F32),
        ],
        compiler_params=_params(1),
        name="inproj",
    )(x2, g1, w_in_bf, qg, kg)


def _attn_kernel(q_ref, k_ref, v_ref, km_ref, vm_ref, g_ref, o_ref, *, tq):
    i = pl.program_id(2)
    lane = lax.broadcasted_iota(jnp.int32, (1, LANES), 1)
    lo = lane < SB_HEAD_DIM
    q2 = q_ref[0]
    zero_bf = jnp.zeros((), BF16)
    qs = (jnp.where(lo, q2, zero_bf), jnp.where(lo, zero_bf, q2))
    row = lax.broadcasted_iota(jnp.int32, (tq, tq), 0)
    col = lax.broadcasted_iota(jnp.int32, (tq, tq), 1)
    suffix = (row > col).astype(BF16)
    causal = col < row

    def block(kb, vb, carry, mask, suffix_m):
        acc, r0, r1 = carry
        rs = [r0, r1]
        for h in range(2):
            z = lax.dot_general(qs[h], kb, (((1,), (1,)), ((), ())), preferred_element_type=F32)
            l1p = jnp.log(1.0 + jnp.exp(-jnp.abs(z)))
            log_beta = jnp.minimum(z, 0.0) - l1p
            log_keep = log_beta - z
            if mask is not None:
                log_keep = jnp.where(mask, log_keep, 0.0)
            later = jnp.dot(log_keep.astype(BF16), suffix_m, preferred_element_type=F32)
            a = jnp.exp(log_beta + later + rs[h])
            if mask is not None:
                a = jnp.where(mask, a, 0.0)
            vh = jnp.where(lo, vb, zero_bf) if h == 0 else jnp.where(lo, zero_bf, vb)
            acc = acc + jnp.dot(a.astype(BF16), vh, preferred_element_type=F32)
            rs[h] = rs[h] + jnp.sum(log_keep, axis=-1, keepdims=True)
        return acc, rs[0], rs[1]

    def kv_block(j):
        start = pl.multiple_of(j * tq, tq)
        return k_ref[0, pl.ds(start, tq), :], v_ref[0, pl.ds(start, tq), :]

    carry = (jnp.zeros((tq, LANES), F32), jnp.zeros((tq, 1), F32), jnp.zeros((tq, 1), F32))
    kb, vb = kv_block(i)
    carry = block(kb, vb, carry, causal, suffix)

    def body(n, c):
        kb, vb = kv_block(i - 1 - n)
        return block(kb, vb, c, None, suffix)

    carry = lax.fori_loop(0, i, body, carry)
    meta_mask = lane < N_META
    acc, _, _ = block(km_ref[...], vm_ref[...], carry, meta_mask, suffix[:LANES, :LANES])

    sq = acc * acc
    s_lo = jnp.sum(jnp.where(lo, sq, 0.0), axis=-1, keepdims=True)
    s_hi = jnp.sum(jnp.where(lo, 0.0, sq), axis=-1, keepdims=True)
    r = lax.rsqrt(jnp.where(lo, s_lo, s_hi) * (1.0 / SB_HEAD_DIM) + EPS)
    o_ref[0] = (acc * r * g_ref[...]).astype(BF16)


def _attention(q, k, v, km, vm, og, tq):
    b, s, _ = q.shape
    n_pairs = SB_WIDTH // LANES
    return pl.pallas_call(
        functools.partial(_attn_kernel, tq=tq),
        grid=(b, n_pairs, s // tq),
        in_specs=[
            pl.BlockSpec((1, tq, LANES), lambda b, p, i: (b, i, p)),
            pl.BlockSpec((1, s, LANES), lambda b, p, i: (b, 0, p)),
            pl.BlockSpec((1, s, LANES), lambda b, p, i: (b, 0, p)),
            pl.BlockSpec((LANES, LANES), lambda b, p, i: (0, p)),
            pl.BlockSpec((LANES, LANES), lambda b, p, i: (0, p)),
            pl.BlockSpec((1, LANES), lambda b, p, i: (0, p)),
        ],
        out_specs=pl.BlockSpec((1, tq, LANES), lambda b, p, i: (b, i, p)),
        out_shape=jax.ShapeDtypeStruct((b, s, SB_WIDTH), BF16),
        compiler_params=_params(3),
        name="attention",
    )(q, k, v, km, vm, og)


def _mix_kernel(h_ref, halo_ref, mhalo_ref, sb_ref, x_ref, cw_ref, cb_ref, lg_ref, lb_ref, wo_ref,
                g2_ref, wr_ref, br_ref,
                xn_ref, hn_ref, eid_ref, gate_ref, rank_ref, cnt_ref,
                win_ref, cwin_ref, cv_ref, carry_ref, *, tm, chunk):
    b = pl.program_id(0)
    i = pl.program_id(1)

    @pl.when(jnp.logical_and(b == 0, i == 0))
    def _():
        carry_ref[...] = jnp.zeros_like(carry_ref)

    @pl.when(i == 0)
    def _():
        win_ref[0:HALO, :] = mhalo_ref[...]

    @pl.when(i > 0)
    def _():
        win_ref[0:HALO, :] = halo_ref[0]

    win_ref[HALO:, :] = h_ref[0]

    first_tap = HALO - (CONV_KERNEL - 1)

    def conv_chunk(c, _):
        r0 = pl.multiple_of(c * chunk, chunk)
        cwin_ref[...] = win_ref[pl.ds(r0, chunk + HALO), :]
        acc = jnp.broadcast_to(cb_ref[...], (chunk, cb_ref.shape[1]))
        for j in range(CONV_KERNEL):
            acc = acc + cw_ref[j:j + 1, :] * cwin_ref[first_tap + j:first_tap + j + chunk, :]
        mu = jnp.mean(acc, axis=-1, keepdims=True)
        cen = acc - mu
        var = jnp.mean(cen * cen, axis=-1, keepdims=True)
        y = cen * lax.rsqrt(var + EPS) * lg_ref[...] + lb_ref[...]
        cv_ref[pl.ds(r0, chunk), :] = (y * jax.nn.sigmoid(y)).astype(BF16)
        return 0

    lax.fori_loop(0, tm // chunk, conv_chunk, 0)

    mixed = jnp.dot(sb_ref[0], wo_ref[0:SB_WIDTH, :], preferred_element_type=F32)
    mixed = mixed + jnp.dot(cv_ref[...], wo_ref[SB_WIDTH:, :], preferred_element_type=F32)
    xn = x_ref[0] + mixed
    xn_ref[0] = xn
    ms = jnp.mean(xn * xn, axis=-1, keepdims=True)
    hn = xn * lax.rsqrt(ms + EPS) * g2_ref[...]
    hn_ref[0] = hn

    logits = lax.dot_general(wr_ref[...], hn, (((1,), (1,)), ((), ())),
                             precision=lax.Precision.HIGHEST, preferred_element_type=F32) + br_ref[...]
    eidx = lax.broadcasted_iota(jnp.int32, logits.shape, 0)
    vals, idxs = [], []
    for _ in range(TOP_K):
        m = jnp.max(logits, axis=0, keepdims=True)
        sel = jnp.min(jnp.where(logits == m, eidx, N_EXPERTS), axis=0, keepdims=True)
        vals.append(m)
        idxs.append(sel)
        logits = jnp.where(eidx == sel, -jnp.inf, logits)
    exps = [jnp.exp(v - vals[0]) for v in vals]
    denom = exps[0] + exps[1] + exps[2] + exps[3]
    gate_ref[...] = jnp.concatenate([e / denom for e in exps], axis=0)
    eid_ref[...] = jnp.concatenate(idxs, axis=0)

    onehots = [(eidx == s).astype(F32) for s in idxs]
    chosen = onehots[0] + onehots[1] + onehots[2] + onehots[3]
    tr = lax.broadcasted_iota(jnp.int32, (tm, tm), 0)
    tc = lax.broadcasted_iota(jnp.int32, (tm, tm), 1)
    before = (tr < tc).astype(BF16)
    prefix = jnp.dot(chosen.astype(BF16), before, preferred_element_type=F32) + carry_ref[...]
    rank_ref[...] = jnp.concatenate(
        [jnp.sum(o * prefix, axis=0, keepdims=True) for o in onehots], axis=0).astype(jnp.int32)
    carry_ref[...] = carry_ref[...] + jnp.sum(chosen, axis=1, keepdims=True)
    cnt_ref[...] = jnp.broadcast_to(carry_ref[...], cnt_ref.shape)


def _mix(hcv, mhalo, sb, x, conv_w, conv_b, ln_g, ln_b, w_out_bf, g2, wr_t, br, tm, chunk):
    b, s, d = x.shape
    cw = hcv.shape[-1]
    t = b * s
    per = s // tm
    tile = lambda b, i: (b, i, 0)
    fixed = lambda b, i: (0, 0)
    tok = lambda b, i: (0, b * per + i)
    return pl.pallas_call(
        functools.partial(_mix_kernel, tm=tm, chunk=chunk),
        grid=(b, per),
        in_specs=[
            pl.BlockSpec((1, tm, cw), tile),
            pl.BlockSpec((1, HALO, cw), lambda b, i: (b, jnp.maximum(i * (tm // HALO) - 1, 0), 0)),
            pl.BlockSpec((HALO, cw), fixed),
            pl.BlockSpec((1, tm, SB_WIDTH), tile),
            pl.BlockSpec((1, tm, d), tile),
            pl.BlockSpec((CONV_KERNEL, cw), fixed),
            pl.BlockSpec((1, cw), fixed),
            pl.BlockSpec((1, cw), fixed),
            pl.BlockSpec((1, cw), fixed),
            pl.BlockSpec((SB_WIDTH + cw, d), fixed),
            pl.BlockSpec((1, d), fixed),
            pl.BlockSpec((N_EXPERTS, d), fixed),
            pl.BlockSpec((N_EXPERTS, 1), fixed),
        ],
        out_specs=[
            pl.BlockSpec((1, tm, d), tile),
            pl.BlockSpec((1, tm, d), tile),
            pl.BlockSpec((TOP_K, tm), tok),
            pl.BlockSpec((TOP_K, tm), tok),
            pl.BlockSpec((TOP_K, tm), tok),
            pl.BlockSpec((N_EXPERTS, LANES), fixed),
        ],
        out_shape=[
            jax.ShapeDtypeStruct((b, s, d), F32),
            jax.ShapeDtypeStruct((b, s, d), F32),
            jax.ShapeDtypeStruct((TOP_K, t), jnp.int32),
            jax.ShapeDtypeStruct((TOP_K, t), F32),
            jax.ShapeDtypeStruct((TOP_K, t), jnp.int32),
            jax.ShapeDtypeStruct((N_EXPERTS, LANES), F32),
        ],
        scratch_shapes=[
            pltpu.VMEM((tm + HALO, cw), F32),
            pltpu.VMEM((chunk + HALO, cw), F32),
            pltpu.VMEM((tm, cw), BF16),
            pltpu.VMEM((N_EXPERTS, 1), F32),
        ],
        compiler_params=_params(2),
        name="mix_router",
    )(hcv, hcv, mhalo, sb, x, conv_w, conv_b, ln_g, ln_b, w_out_bf, g2, wr_t, br)


def _wait_rows(rows_ref, n, sem):
    pltpu.make_async_copy(rows_ref.at[pl.ds(0, n)], rows_ref.at[pl.ds(0, n)], sem).wait()


def _dispatch_kernel(padlo_ref, padlen_ref, nused_ref, pos_ref, hn_ref, rows_ref, zeros_ref, sem, zsem,
                     *, tm, n_blocks, n_pad):
    i = pl.program_id(0)
    base = i * tm

    @pl.when(i == 0)
    def _():
        zeros_ref[...] = jnp.zeros_like(zeros_ref)
        for e in range(N_EXPERTS):
            lo, ln = padlo_ref[e], padlen_ref[e]
            head = (-lo) & (SUBLANES - 1)
            for r in range(SUBLANES - 1):
                @pl.when(r < head)
                def _():
                    pltpu.make_async_copy(zeros_ref.at[pl.ds(0, 1)], rows_ref.at[pl.ds(lo + r, 1)], zsem).start()

            lo8, ln8 = lo + head, ln - head
            for bit in range(SUBLANES.bit_length() - 1, EXPERT_ROWS.bit_length() - 1):
                size = 1 << bit

                @pl.when((ln8 >> bit) & 1 == 1)
                def _():
                    start = pl.multiple_of(lo8 + ((ln8 >> (bit + 1)) << (bit + 1)), SUBLANES)
                    pltpu.make_async_copy(zeros_ref.at[pl.ds(0, size)],
                                          rows_ref.at[pl.ds(start, size)], zsem).start()

        def tail(j, _):
            start = pl.multiple_of(j * EXPERT_ROWS, EXPERT_ROWS)
            pltpu.make_async_copy(zeros_ref, rows_ref.at[pl.ds(start, EXPERT_ROWS)], zsem).start()
            return 0

        lax.fori_loop(nused_ref[0], n_blocks, tail, 0)

    def issue(t, _):
        for k in range(TOP_K):
            pltpu.make_async_copy(hn_ref.at[pl.ds(base + t, 1)],
                                  rows_ref.at[pl.ds(pos_ref[k, t], 1)], sem).start()
        return 0

    lax.fori_loop(0, tm, issue, 0)
    _wait_rows(rows_ref, TOP_K * tm, sem)

    @pl.when(i == pl.num_programs(0) - 1)
    def _():
        _wait_rows(rows_ref, n_pad, zsem)


def _dispatch(pad_lo, pad_len, n_used, pos, hn, n_blocks, tm):
    t, d = hn.shape
    n_rows = n_blocks * EXPERT_ROWS
    grid_spec = pltpu.PrefetchScalarGridSpec(
        num_scalar_prefetch=3,
        grid=(t // tm,),
        in_specs=[
            pl.BlockSpec((TOP_K, tm), lambda i, *_: (0, i), memory_space=pltpu.SMEM),
            pl.BlockSpec(memory_space=pl.ANY),
        ],
        out_specs=pl.BlockSpec(memory_space=pl.ANY),
        scratch_shapes=[pltpu.VMEM((EXPERT_ROWS, d), hn.dtype), pltpu.SemaphoreType.DMA(()),
                        pltpu.SemaphoreType.DMA(())],
    )
    return pl.pallas_call(
        functools.partial(_dispatch_kernel, tm=tm, n_blocks=n_blocks, n_pad=n_rows - t * TOP_K),
        grid_spec=grid_spec,
        out_shape=jax.ShapeDtypeStruct((n_rows, d), hn.dtype),
        compiler_params=_params(1),
        name="dispatch",
    )(pad_lo, pad_len, n_used, pos, hn)


def _expert_kernel(be_ref, nused_ref, x_ref, wu_ref, bu_ref, wd_ref, bd_ref, y_ref,
                   wu_bf, wd_bf, *, cast_rows):
    j = pl.program_id(0)
    de = wd_ref.shape[1]

    @pl.when(j >= nused_ref[0])
    def _():
        y_ref[...] = jnp.zeros_like(y_ref)

    @pl.when(j < nused_ref[0])
    def _():
        prev = be_ref[jnp.maximum(j - 1, 0)]

        @pl.when(jnp.logical_or(j == 0, be_ref[j] != prev))
        def _():
            def cast(c, _):
                r0 = pl.multiple_of(c * cast_rows, cast_rows)
                wu_bf[pl.ds(r0, cast_rows), :] = wu_ref[0, pl.ds(r0, cast_rows), :].astype(BF16)
                wd_bf[pl.ds(r0, cast_rows), :] = wd_ref[0, pl.ds(r0, cast_rows), :].astype(BF16)
                return 0

            lax.fori_loop(0, wu_ref.shape[1] // cast_rows, cast, 0)

        up = jnp.dot(x_ref[...].astype(BF16), wu_bf[...], preferred_element_type=F32) + bu_ref[0]
        glu = jnp.minimum(up[:, :de], SWIGLU_LIMIT)
        lin = jnp.clip(up[:, de:], -SWIGLU_LIMIT, SWIGLU_LIMIT)
        act = glu * jax.nn.sigmoid(SWIGLU_ALPHA * glu) * (lin + 1.0)
        y_ref[...] = jnp.dot(act.astype(BF16), wd_bf[...], preferred_element_type=F32) + bd_ref[0]


def _experts(block_expert, n_used, x_rows, w_up, b_up, w_down, b_down):
    n_rows, d = x_rows.shape
    n_blocks = n_rows // EXPERT_ROWS
    ne, _, up_cols = w_up.shape
    de = w_down.shape[1]
    blk = lambda j, be, nu: (jnp.minimum(j, nu[0] - 1), 0)
    wsel = lambda j, be, nu: (be[jnp.minimum(j, nu[0] - 1)], 0, 0)
    grid_spec = pltpu.PrefetchScalarGridSpec(
        num_scalar_prefetch=2,
        grid=(n_blocks,),
        in_specs=[
            pl.BlockSpec((EXPERT_ROWS, d), blk),
            pl.BlockSpec((1, d, up_cols), wsel),
            pl.BlockSpec((1, 1, up_cols), wsel),
            pl.BlockSpec((1, de, d), wsel),
            pl.BlockSpec((1, 1, d), wsel),
        ],
        out_specs=pl.BlockSpec((EXPERT_ROWS, d), lambda j, be, nu: (j, 0)),
        scratch_shapes=[pltpu.VMEM((d, up_cols), BF16), pltpu.VMEM((de, d), BF16)],
    )
    return pl.pallas_call(
        functools.partial(_expert_kernel, cast_rows=64),
        grid_spec=grid_spec,
        out_shape=jax.ShapeDtypeStruct((n_rows, d), F32),
        compiler_params=_params(1),
        name="experts",
    )(block_expert, n_used, x_rows, w_up, b_up.reshape(ne, 1, up_cols),
      w_down, b_down.reshape(ne, 1, d))


def _combine_kernel(pos_ref, pos_next_ref, gate_ref, xn_ref, rows_ref, o_ref, buf, sems, *, tm):
    i = pl.program_id(0)
    n = pl.num_programs(0)
    slot = lax.rem(i, 2)

    def gather(p_ref, s):
        def issue(t, _):
            for k in range(TOP_K):
                pltpu.make_async_copy(rows_ref.at[pl.ds(p_ref[k, t], 1)],
                                      buf.at[s, pl.ds(k * tm + t, 1)], sems.at[s]).start()
            return 0

        lax.fori_loop(0, tm, issue, 0)

    @pl.when(i == 0)
    def _():
        gather(pos_ref, 0)

    @pl.when(i + 1 < n)
    def _():
        gather(pos_next_ref, 1 - slot)

    pltpu.make_async_copy(rows_ref.at[pl.ds(0, TOP_K * tm)], buf.at[slot], sems.at[slot]).wait()
    acc = xn_ref[...]
    for k in range(TOP_K):
        acc = acc + gate_ref[:, k:k + 1] * buf[slot, pl.ds(k * tm, tm), :]
    o_ref[...] = acc


def _combine(pos, gates_t, xn, y_rows, tm):
    t, d = xn.shape
    n = t // tm
    return pl.pallas_call(
        functools.partial(_combine_kernel, tm=tm),
        grid=(n,),
        in_specs=[
            pl.BlockSpec((TOP_K, tm), lambda i: (0, i), memory_space=pltpu.SMEM),
            pl.BlockSpec((TOP_K, tm), lambda i: (0, jnp.minimum(i + 1, n - 1)), memory_space=pltpu.SMEM),
            pl.BlockSpec((tm, TOP_K), lambda i: (i, 0)),
            pl.BlockSpec((tm, d), lambda i: (i, 0)),
            pl.BlockSpec(memory_space=pl.ANY),
        ],
        out_specs=pl.BlockSpec((tm, d), lambda i: (i, 0)),
        out_shape=jax.ShapeDtypeStruct((t, d), F32),
        scratch_shapes=[pltpu.VMEM((2, TOP_K * tm, d), F32), pltpu.SemaphoreType.DMA((2,))],
        compiler_params=_params(1),
        name="combine",
    )(pos, pos, gates_t, xn, y_rows)


def _tile(n, want):
    t = min(n, want)
    assert n % t == 0, (n, t)
    return t


def kernel(x, meta_tokens, norm1_g, w_in, q_norm_g, k_norm_g, conv_w, conv_b, conv_ln_g, conv_ln_b,
           attn_out_g, w_out, norm2_g, w_router, b_router, w_up, b_up, w_down, b_down):
    assert norm1_g.shape[0] == 1, "single layer: meta-token rows are only keys/values and conv context"
    b, s, d = x.shape
    t = b * s
    cw = conv_w.shape[-1]

    g1 = norm1_g[0][None, :]
    w_in_bf = w_in[0].astype(BF16)
    qg = jnp.tile(q_norm_g[0], SB_HEADS)[None, :]
    kg = jnp.tile(k_norm_g[0], SB_HEADS)[None, :]

    q, k, v, hcv = _inproj(x.reshape(t, d), g1, w_in_bf, qg, kg, _tile(t, 512))
    _, km, vm, hm = _inproj(meta_tokens, g1, w_in_bf, qg, kg, N_META)
    pad = ((0, LANES - N_META), (0, 0))
    sb = _attention(q.reshape(b, s, SB_WIDTH), k.reshape(b, s, SB_WIDTH), v.reshape(b, s, SB_WIDTH),
                    jnp.pad(km, pad), jnp.pad(vm, pad), attn_out_g[0].reshape(1, SB_WIDTH), _tile(s, 256))

    mhalo = jnp.concatenate([jnp.zeros((HALO - N_META, cw), F32), hm], axis=0)
    tm = _tile(s, 512)
    xn, hn, eid, gates, rank, counts = _mix(
        hcv.reshape(b, s, cw), mhalo, sb, x, conv_w[0], conv_b[0][None, :], conv_ln_g[0][None, :],
        conv_ln_b[0][None, :], w_out[0].astype(BF16), norm2_g[0][None, :], w_router[0].T,
        b_router[0][:, None], tm, _tile(tm, 64))

    counts = counts[:, 0].astype(jnp.int32)
    padded = (counts + EXPERT_ROWS - 1) // EXPERT_ROWS * EXPERT_ROWS
    pad_end = jnp.cumsum(padded)
    pad_start = pad_end - padded
    n_blocks = t * TOP_K // EXPERT_ROWS + N_EXPERTS
    block_row = jnp.arange(n_blocks, dtype=jnp.int32) * EXPERT_ROWS
    block_expert = jnp.minimum(jnp.sum(block_row[:, None] >= pad_end[None, :], axis=1), N_EXPERTS - 1).astype(jnp.int32)
    n_used = (pad_end[-1:] // EXPERT_ROWS).astype(jnp.int32)
    onehot = eid[:, :, None] == jnp.arange(N_EXPERTS, dtype=jnp.int32)
    pos = rank + jnp.sum(jnp.where(onehot, pad_start, 0), axis=-1)

    x_rows = _dispatch(pad_start + counts, padded - counts, n_used, pos, hn.reshape(t, d), n_blocks,
                       _tile(t, 512))
    y_rows = _experts(block_expert, n_used, x_rows, w_up[0], b_up[0], w_down[0], b_down[0])
    out = _combine(pos, gates.T, xn.reshape(t, d), y_rows, _tile(t, 256))
    return out.reshape(b, s, d)
```

```python
import functools

import jax
import jax.numpy as jnp
from jax import lax
from jax.experimental import pallas as pl
from jax.experimental.pallas import tpu as pltpu

N_META = 16
SB_HEADS = 8
SB_HEAD_DIM = 64
SB_WIDTH = SB_HEADS * SB_HEAD_DIM
CONV_KERNEL = 31
N_EXPERTS = 32
TOP_K = 4
SWIGLU_LIMIT = 7.0
SWIGLU_ALPHA = 1.702
EPS = 1e-6

LANES = 128
SUBLANES = 8
HALO = 32
EXPERT_ROWS = 256
VMEM_LIMIT = 56 * 1024 * 1024

F32 = jnp.float32
BF16 = jnp.bfloat16


def _params(n_axes, vmem=VMEM_LIMIT):
    return pltpu.CompilerParams(dimension_semantics=("arbitrary",) * n_axes, vmem_limit_bytes=vmem)


def _inproj_kernel(x_ref, g1_ref, w_ref, qg_ref, kg_ref, q_ref, k_ref, v_ref, h_ref):
    x = x_ref[...]
    ms = jnp.mean(x * x, axis=-1, keepdims=True)
    n = (x * lax.rsqrt(ms + EPS) * g1_ref[...]).astype(BF16)
    lo = lax.broadcasted_iota(jnp.int32, (1, LANES), 1) < SB_HEAD_DIM

    def proj(c0, c1):
        return jnp.dot(n, w_ref[:, c0:c1], preferred_element_type=F32)

    def head_norm(acc, g_ref, out_ref, scale):
        for c in range(SB_WIDTH // LANES):
            sl = slice(c * LANES, (c + 1) * LANES)
            a = acc[:, sl]
            sq = a * a
            s_lo = jnp.sum(jnp.where(lo, sq, 0.0), axis=-1, keepdims=True)
            s_hi = jnp.sum(jnp.where(lo, 0.0, sq), axis=-1, keepdims=True)
            r = lax.rsqrt(jnp.where(lo, s_lo, s_hi) * (1.0 / SB_HEAD_DIM) + EPS)
            out_ref[:, sl] = (a * r * (g_ref[:, sl] * scale)).astype(BF16)

    head_norm(proj(0, SB_WIDTH), qg_ref, q_ref, SB_HEAD_DIM ** -0.5)
    head_norm(proj(SB_WIDTH, 2 * SB_WIDTH), kg_ref, k_ref, 1.0)
    v_ref[...] = proj(2 * SB_WIDTH, 3 * SB_WIDTH).astype(BF16)
    cw = (w_ref.shape[1] - 3 * SB_WIDTH) // 2
    val = proj(3 * SB_WIDTH, 3 * SB_WIDTH + cw)
    gate = proj(3 * SB_WIDTH + cw, 3 * SB_WIDTH + 2 * cw)
    h_ref[...] = val * jax.nn.sigmoid(gate)


def _inproj(x2, g1, w_in_bf, qg, kg, tm):
    t, d = x2.shape
    cols = w_in_bf.shape[1]
    cw = (cols - 3 * SB_WIDTH) // 2
    row = lambda i: (i, 0)
    fixed = lambda i: (0, 0)
    return pl.pallas_call(
        _inproj_kernel,
        grid=(t // tm,),
        in_specs=[
            pl.BlockSpec((tm, d), row),
            pl.BlockSpec((1, d), fixed),
            pl.BlockSpec((d, cols), fixed),
            pl.BlockSpec((1, SB_WIDTH), fixed),
            pl.BlockSpec((1, SB_WIDTH), fixed),
        ],
        out_specs=[
            pl.BlockSpec((tm, SB_WIDTH), row),
            pl.BlockSpec((tm, SB_WIDTH), row),
            pl.BlockSpec((tm, SB_WIDTH), row),
            pl.BlockSpec((tm, cw), row),
        ],
        out_shape=[
            jax.ShapeDtypeStruct((t, SB_WIDTH), BF16),
            jax.ShapeDtypeStruct((t, SB_WIDTH), BF16),
            jax.ShapeDtypeStruct((t, SB_WIDTH), BF16),
            jax.ShapeDtypeStruct((t, cw), F32),
        ],
        compiler_params=_params(1),
        name="inproj",
    )(x2, g1, w_in_bf, qg, kg)


def _attn_kernel(q_ref, k_ref, v_ref, km_ref, vm_ref, g_ref, o_ref, *, tq):
    i = pl.program_id(2)
    lane = lax.broadcasted_iota(jnp.int32, (1, LANES), 1)
    lo = lane < SB_HEAD_DIM
    q2 = q_ref[0]
    zero_bf = jnp.zeros((), BF16)
    qs = (jnp.where(lo, q2, zero_bf), jnp.where(lo, zero_bf, q2))
    row = lax.broadcasted_iota(jnp.int32, (tq, tq), 0)
    col = lax.broadcasted_iota(jnp.int32, (tq, tq), 1)
    suffix = (row > col).astype(BF16)
    causal = col < row

    def block(kb, vb, carry, mask, suffix_m):
        acc, r0, r1 = carry
        rs = [r0, r1]
        for h in range(2):
            z = lax.dot_general(qs[h], kb, (((1,), (1,)), ((), ())), preferred_element_type=F32)
            l1p = jnp.log(1.0 + jnp.exp(-jnp.abs(z)))
            log_beta = jnp.minimum(z, 0.0) - l1p
            log_keep = log_beta - z
            if mask is not None:
                log_keep = jnp.where(mask, log_keep, 0.0)
            later = jnp.dot(log_keep.astype(BF16), suffix_m, preferred_element_type=F32)
            a = jnp.exp(log_beta + later + rs[h])
            if mask is not None:
                a = jnp.where(mask, a, 0.0)
            vh = jnp.where(lo, vb, zero_bf) if h == 0 else jnp.where(lo, zero_bf, vb)
            acc = acc + jnp.dot(a.astype(BF16), vh, preferred_element_type=F32)
            rs[h] = rs[h] + jnp.sum(log_keep, axis=-1, keepdims=True)
        return acc, rs[0], rs[1]

    def kv_block(j):
        start = pl.multiple_of(j * tq, tq)
        return k_ref[0, pl.ds(start, tq), :], v_ref[0, pl.ds(start, tq), :]

    carry = (jnp.zeros((tq, LANES), F32), jnp.zeros((tq, 1), F32), jnp.zeros((tq, 1), F32))
    kb, vb = kv_block(i)
    carry = block(kb, vb, carry, causal, suffix)

    def body(n, c):
        kb, vb = kv_block(i - 1 - n)
        return block(kb, vb, c, None, suffix)

    carry = lax.fori_loop(0, i, body, carry)
    meta_mask = lane < N_META
    acc, _, _ = block(km_ref[...], vm_ref[...], carry, meta_mask, suffix[:LANES, :LANES])

    sq = acc * acc
    s_lo = jnp.sum(jnp.where(lo, sq, 0.0), axis=-1, keepdims=True)
    s_hi = jnp.sum(jnp.where(lo, 0.0, sq), axis=-1, keepdims=True)
    r = lax.rsqrt(jnp.where(lo, s_lo, s_hi) * (1.0 / SB_HEAD_DIM) + EPS)
    o_ref[0] = (acc * r * g_ref[...]).astype(BF16)


def _attention(q, k, v, km, vm, og, tq):
    b, s, _ = q.shape
    n_pairs = SB_WIDTH // LANES
    return pl.pallas_call(
        functools.partial(_attn_kernel, tq=tq),
        grid=(b, n_pairs, s // tq),
        in_specs=[
            pl.BlockSpec((1, tq, LANES), lambda b, p, i: (b, i, p)),
            pl.BlockSpec((1, s, LANES), lambda b, p, i: (b, 0, p)),
            pl.BlockSpec((1, s, LANES), lambda b, p, i: (b, 0, p)),
            pl.BlockSpec((LANES, LANES), lambda b, p, i: (0, p)),
            pl.BlockSpec((LANES, LANES), lambda b, p, i: (0, p)),
            pl.BlockSpec((1, LANES), lambda b, p, i: (0, p)),
        ],
        out_specs=pl.BlockSpec((1, tq, LANES), lambda b, p, i: (b, i, p)),
        out_shape=jax.ShapeDtypeStruct((b, s, SB_WIDTH), BF16),
        compiler_params=_params(3),
        name="attention",
    )(q, k, v, km, vm, og)


def _mix_kernel(h_ref, halo_ref, mhalo_ref, sb_ref, x_ref, cw_ref, cb_ref, lg_ref, lb_ref, wo_ref,
                g2_ref, wr_ref, br_ref,
                xn_ref, hn_ref, eid_ref, gate_ref, rank_ref, cnt_ref,
                win_ref, cwin_ref, cv_ref, carry_ref, *, tm, chunk):
    b = pl.program_id(0)
    i = pl.program_id(1)

    @pl.when(jnp.logical_and(b == 0, i == 0))
    def _():
        carry_ref[...] = jnp.zeros_like(carry_ref)

    @pl.when(i == 0)
    def _():
        win_ref[0:HALO, :] = mhalo_ref[...]

    @pl.when(i > 0)
    def _():
        win_ref[0:HALO, :] = halo_ref[0]

    win_ref[HALO:, :] = h_ref[0]

    first_tap = HALO - (CONV_KERNEL - 1)

    def conv_chunk(c, _):
        r0 = pl.multiple_of(c * chunk, chunk)
        cwin_ref[...] = win_ref[pl.ds(r0, chunk + HALO), :]
        acc = jnp.broadcast_to(cb_ref[...], (chunk, cb_ref.shape[1]))
        for j in range(CONV_KERNEL):
            acc = acc + cw_ref[j:j + 1, :] * cwin_ref[first_tap + j:first_tap + j + chunk, :]
        mu = jnp.mean(acc, axis=-1, keepdims=True)
        cen = acc - mu
        var = jnp.mean(cen * cen, axis=-1, keepdims=True)
        y = cen * lax.rsqrt(var + EPS) * lg_ref[...] + lb_ref[...]
        cv_ref[pl.ds(r0, chunk), :] = (y * jax.nn.sigmoid(y)).astype(BF16)
        return 0

    lax.fori_loop(0, tm // chunk, conv_chunk, 0)

    mixed = jnp.dot(sb_ref[0], wo_ref[0:SB_WIDTH, :], preferred_element_type=F32)
    mixed = mixed + jnp.dot(cv_ref[...], wo_ref[SB_WIDTH:, :], preferred_element_type=F32)
    xn = x_ref[0] + mixed
    xn_ref[0] = xn
    ms = jnp.mean(xn * xn, axis=-1, keepdims=True)
    hn = xn * lax.rsqrt(ms + EPS) * g2_ref[...]
    hn_ref[0] = hn

    logits = lax.dot_general(wr_ref[...], hn, (((1,), (1,)), ((), ())),
                             precision=lax.Precision.HIGHEST, preferred_element_type=F32) + br_ref[...]
    eidx = lax.broadcasted_iota(jnp.int32, logits.shape, 0)
    vals, idxs = [], []
    for _ in range(TOP_K):
        m = jnp.max(logits, axis=0, keepdims=True)
        sel = jnp.min(jnp.where(logits == m, eidx, N_EXPERTS), axis=0, keepdims=True)
        vals.append(m)
        idxs.append(sel)
        logits = jnp.where(eidx == sel, -jnp.inf, logits)
    exps = [jnp.exp(v - vals[0]) for v in vals]
    denom = exps[0] + exps[1] + exps[2] + exps[3]
    gate_ref[...] = jnp.concatenate([e / denom for e in exps], axis=0)
    eid_ref[...] = jnp.concatenate(idxs, axis=0)

    onehots = [(eidx == s).astype(F32) for s in idxs]
    chosen = onehots[0] + onehots[1] + onehots[2] + onehots[3]
    tr = lax.broadcasted_iota(jnp.int32, (tm, tm), 0)
    tc = lax.broadcasted_iota(jnp.int32, (tm, tm), 1)
    before = (tr < tc).astype(BF16)
    prefix = jnp.dot(chosen.astype(BF16), before, preferred_element_type=F32) + carry_ref[...]
    rank_ref[...] = jnp.concatenate(
        [jnp.sum(o * prefix, axis=0, keepdims=True) for o in onehots], axis=0).astype(jnp.int32)
    carry_ref[...] = carry_ref[...] + jnp.sum(chosen, axis=1, keepdims=True)
    cnt_ref[...] = jnp.broadcast_to(carry_ref[...], cnt_ref.shape)


def _mix(hcv, mhalo, sb, x, conv_w, conv_b, ln_g, ln_b, w_out_bf, g2, wr_t, br, tm, chunk):
    b, s, d = x.shape
    cw = hcv.shape[-1]
    t = b * s
    per = s // tm
    tile = lambda b, i: (b, i, 0)
    fixed = lambda b, i: (0, 0)
    tok = lambda b, i: (0, b * per + i)
    return pl.pallas_call(
        functools.partial(_mix_kernel, tm=tm, chunk=chunk),
        grid=(b, per),
        in_specs=[
            pl.BlockSpec((1, tm, cw), tile),
            pl.BlockSpec((1, HALO, cw), lambda b, i: (b, jnp.maximum(i * (tm // HALO) - 1, 0), 0)),
            pl.BlockSpec((HALO, cw), fixed),
            pl.BlockSpec((1, tm, SB_WIDTH), tile),
            pl.BlockSpec((1, tm, d), tile),
            pl.BlockSpec((CONV_KERNEL, cw), fixed),
            pl.BlockSpec((1, cw), fixed),
            pl.BlockSpec((1, cw), fixed),
            pl.BlockSpec((1, cw), fixed),
            pl.BlockSpec((SB_WIDTH + cw, d), fixed),
            pl.BlockSpec((1, d), fixed),
            pl.BlockSpec((N_EXPERTS, d), fixed),
            pl.BlockSpec((N_EXPERTS, 1), fixed),
        ],
        out_specs=[
            pl.BlockSpec((1, tm, d), tile),
            pl.BlockSpec((1, tm, d), tile),
            pl.BlockSpec((TOP_K, tm), tok),
            pl.BlockSpec((TOP_K, tm), tok),
            pl.BlockSpec((TOP_K, tm), tok),
            pl.BlockSpec((N_EXPERTS, LANES), fixed),
        ],
        out_shape=[
            jax.ShapeDtypeStruct((b, s, d), F32),
            jax.ShapeDtypeStruct((b, s, d), F32),
            jax.ShapeDtypeStruct((TOP_K, t), jnp.int32),
            jax.ShapeDtypeStruct((TOP_K, t), F32),
            jax.ShapeDtypeStruct((TOP_K, t), jnp.int32),
            jax.ShapeDtypeStruct((N_EXPERTS, LANES), F32),
        ],
        scratch_shapes=[
            pltpu.VMEM((tm + HALO, cw), F32),
            pltpu.VMEM((chunk + HALO, cw), F32),
            pltpu.VMEM((tm, cw), BF16),
            pltpu.VMEM((N_EXPERTS, 1), F32),
        ],
        compiler_params=_params(2),
        name="mix_router",
    )(hcv, hcv, mhalo, sb, x, conv_w, conv_b, ln_g, ln_b, w_out_bf, g2, wr_t, br)


def _wait_rows(rows_ref, n, sem):
    pltpu.make_async_copy(rows_ref.at[pl.ds(0, n)], rows_ref.at[pl.ds(0, n)], sem).wait()


def _dispatch_kernel(padlo_ref, padlen_ref, nused_ref, pos_ref, hn_ref, rows_ref, zeros_ref, sem, zsem,
                     *, tm, n_blocks, n_pad):
    i = pl.program_id(0)

    @pl.when(i == 0)
    def _():
        zeros_ref[...] = jnp.zeros_like(zeros_ref)
        for e in range(N_EXPERTS):
            lo, ln = padlo_ref[e], padlen_ref[e]
            head = (-lo) & (SUBLANES - 1)
            for r in range(SUBLANES - 1):
                @pl.when(r < head)
                def _():
                    pltpu.make_async_copy(zeros_ref.at[pl.ds(0, 1)], rows_ref.at[pl.ds(lo + r, 1)], zsem).start()

            lo8, ln8 = lo + head, ln - head
            for bit in range(SUBLANES.bit_length() - 1, EXPERT_ROWS.bit_length() - 1):
                size = 1 << bit

                @pl.when((ln8 >> bit) & 1 == 1)
                def _():
                    start = pl.multiple_of(lo8 + ((ln8 >> (bit + 1)) << (bit + 1)), SUBLANES)
                    pltpu.make_async_copy(zeros_ref.at[pl.ds(0, size)],
                                          rows_ref.at[pl.ds(start, size)], zsem).start()

        def tail(j, _):
            start = pl.multiple_of(j * EXPERT_ROWS, EXPERT_ROWS)
            pltpu.make_async_copy(zeros_ref, rows_ref.at[pl.ds(start, EXPERT_ROWS)], zsem).start()
            return 0

        lax.fori_loop(nused_ref[0], n_blocks, tail, 0)

    def issue(t, _):
        for k in range(TOP_K):
            pltpu.make_async_copy(hn_ref.at[pl.ds(t, 1)],
                                  rows_ref.at[pl.ds(pos_ref[k, t], 1)], sem).start()
        return 0

    lax.fori_loop(0, tm, issue, 0)
    _wait_rows(rows_ref, TOP_K * tm, sem)

    @pl.when(i == pl.num_programs(0) - 1)
    def _():
        _wait_rows(rows_ref, n_pad, zsem)


def _dispatch(pad_lo, pad_len, n_used, pos, hn, n_blocks, tm):
    t, d = hn.shape
    n_rows = n_blocks * EXPERT_ROWS
    grid_spec = pltpu.PrefetchScalarGridSpec(
        num_scalar_prefetch=3,
        grid=(t // tm,),
        in_specs=[
            pl.BlockSpec((TOP_K, tm), lambda i, *_: (0, i), memory_space=pltpu.SMEM),
            pl.BlockSpec((tm, d), lambda i, *_: (i, 0)),
        ],
        out_specs=pl.BlockSpec(memory_space=pl.ANY),
        scratch_shapes=[pltpu.VMEM((EXPERT_ROWS, d), hn.dtype), pltpu.SemaphoreType.DMA(()),
                        pltpu.SemaphoreType.DMA(())],
    )
    return pl.pallas_call(
        functools.partial(_dispatch_kernel, tm=tm, n_blocks=n_blocks, n_pad=n_rows - t * TOP_K),
        grid_spec=grid_spec,
        out_shape=jax.ShapeDtypeStruct((n_rows, d), hn.dtype),
        compiler_params=_params(1),
        name="dispatch",
    )(pad_lo, pad_len, n_used, pos, hn)


def _expert_kernel(be_ref, nused_ref, x_ref, wu_ref, bu_ref, wd_ref, bd_ref, y_ref,
                   wu_bf, wd_bf, *, cast_rows):
    j = pl.program_id(0)
    de = wd_ref.shape[1]

    @pl.when(j >= nused_ref[0])
    def _():
        y_ref[...] = jnp.zeros_like(y_ref)

    @pl.when(j < nused_ref[0])
    def _():
        prev = be_ref[jnp.maximum(j - 1, 0)]

        @pl.when(jnp.logical_or(j == 0, be_ref[j] != prev))
        def _():
            def cast(c, _):
                r0 = pl.multiple_of(c * cast_rows, cast_rows)
                wu_bf[pl.ds(r0, cast_rows), :] = wu_ref[0, pl.ds(r0, cast_rows), :].astype(BF16)
                wd_bf[pl.ds(r0, cast_rows), :] = wd_ref[0, pl.ds(r0, cast_rows), :].astype(BF16)
                return 0

            lax.fori_loop(0, wu_ref.shape[1] // cast_rows, cast, 0)

        up = jnp.dot(x_ref[...].astype(BF16), wu_bf[...], preferred_element_type=F32) + bu_ref[0]
        glu = jnp.minimum(up[:, :de], SWIGLU_LIMIT)
        lin = jnp.clip(up[:, de:], -SWIGLU_LIMIT, SWIGLU_LIMIT)
        act = glu * jax.nn.sigmoid(SWIGLU_ALPHA * glu) * (lin + 1.0)
        y_ref[...] = jnp.dot(act.astype(BF16), wd_bf[...], preferred_element_type=F32) + bd_ref[0]


def _experts(block_expert, n_used, x_rows, w_up, b_up, w_down, b_down):
    n_rows, d = x_rows.shape
    n_blocks = n_rows // EXPERT_ROWS
    ne, _, up_cols = w_up.shape
    de = w_down.shape[1]
    blk = lambda j, be, nu: (jnp.minimum(j, nu[0] - 1), 0)
    wsel = lambda j, be, nu: (be[jnp.minimum(j, nu[0] - 1)], 0, 0)
    grid_spec = pltpu.PrefetchScalarGridSpec(
        num_scalar_prefetch=2,
        grid=(n_blocks,),
        in_specs=[
            pl.BlockSpec((EXPERT_ROWS, d), blk),
            pl.BlockSpec((1, d, up_cols), wsel),
            pl.BlockSpec((1, 1, up_cols), wsel),
            pl.BlockSpec((1, de, d), wsel),
            pl.BlockSpec((1, 1, d), wsel),
        ],
        out_specs=pl.BlockSpec((EXPERT_ROWS, d), lambda j, be, nu: (j, 0)),
        scratch_shapes=[pltpu.VMEM((d, up_cols), BF16), pltpu.VMEM((de, d), BF16)],
    )
    return pl.pallas_call(
        functools.partial(_expert_kernel, cast_rows=64),
        grid_spec=grid_spec,
        out_shape=jax.ShapeDtypeStruct((n_rows, d), F32),
        compiler_params=_params(1),
        name="experts",
    )(block_expert, n_used, x_rows, w_up, b_up.reshape(ne, 1, up_cols),
      w_down, b_down.reshape(ne, 1, d))


def _combine_kernel(pos_ref, pos_next_ref, gate_ref, xn_ref, rows_ref, o_ref, buf, sems, *, tm):
    i = pl.program_id(0)
    n = pl.num_programs(0)
    slot = lax.rem(i, 2)

    def gather(p_ref, s):
        def issue(t, _):
            for k in range(TOP_K):
                pltpu.make_async_copy(rows_ref.at[pl.ds(p_ref[k, t], 1)],
                                      buf.at[s, pl.ds(k * tm + t, 1)], sems.at[s]).start()
            return 0

        lax.fori_loop(0, tm, issue, 0)

    @pl.when(i == 0)
    def _():
        gather(pos_ref, 0)

    @pl.when(i + 1 < n)
    def _():
        gather(pos_next_ref, 1 - slot)

    pltpu.make_async_copy(rows_ref.at[pl.ds(0, TOP_K * tm)], buf.at[slot], sems.at[slot]).wait()
    acc = xn_ref[...]
    for k in range(TOP_K):
        acc = acc + gate_ref[:, k:k + 1] * buf[slot, pl.ds(k * tm, tm), :]
    o_ref[...] = acc


def _combine(pos, gates_t, xn, y_rows, tm):
    t, d = xn.shape
    n = t // tm
    return pl.pallas_call(
        functools.partial(_combine_kernel, tm=tm),
        grid=(n,),
        in_specs=[
            pl.BlockSpec((TOP_K, tm), lambda i: (0, i), memory_space=pltpu.SMEM),
            pl.BlockSpec((TOP_K, tm), lambda i: (0, jnp.minimum(i + 1, n - 1)), memory_space=pltpu.SMEM),
            pl.BlockSpec((tm, TOP_K), lambda i: (i, 0)),
            pl.BlockSpec((tm, d), lambda i: (i, 0)),
            pl.BlockSpec(memory_space=pl.ANY),
        ],
        out_specs=pl.BlockSpec((tm, d), lambda i: (i, 0)),
        out_shape=jax.ShapeDtypeStruct((t, d), F32),
        scratch_shapes=[pltpu.VMEM((2, TOP_K * tm, d), F32), pltpu.SemaphoreType.DMA((2,))],
        compiler_params=_params(1),
        name="combine",
    )(pos, pos, gates_t, xn, y_rows)


def _tile(n, want):
    t = min(n, want)
    assert n % t == 0, (n, t)
    return t


def kernel(x, meta_tokens, norm1_g, w_in, q_norm_g, k_norm_g, conv_w, conv_b, conv_ln_g, conv_ln_b,
           attn_out_g, w_out, norm2_g, w_router, b_router, w_up, b_up, w_down, b_down):
    assert norm1_g.shape[0] == 1, "single layer: meta-token rows are only keys/values and conv context"
    b, s, d = x.shape
    t = b * s
    cw = conv_w.shape[-1]

    g1 = norm1_g[0][None, :]
    w_in_bf = w_in[0].astype(BF16)
    qg = jnp.tile(q_norm_g[0], SB_HEADS)[None, :]
    kg = jnp.tile(k_norm_g[0], SB_HEADS)[None, :]

    q, k, v, hcv = _inproj(x.reshape(t, d), g1, w_in_bf, qg, kg, _tile(t, 512))
    _, km, vm, hm = _inproj(meta_tokens, g1, w_in_bf, qg, kg, N_META)
    pad = ((0, LANES - N_META), (0, 0))
    sb = _attention(q.reshape(b, s, SB_WIDTH), k.reshape(b, s, SB_WIDTH), v.reshape(b, s, SB_WIDTH),
                    jnp.pad(km, pad), jnp.pad(vm, pad), attn_out_g[0].reshape(1, SB_WIDTH), _tile(s, 256))

    mhalo = jnp.concatenate([jnp.zeros((HALO - N_META, cw), F32), hm], axis=0)
    tm = _tile(s, 512)
    xn, hn, eid, gates, rank, counts = _mix(
        hcv.reshape(b, s, cw), mhalo, sb, x, conv_w[0], conv_b[0][None, :], conv_ln_g[0][None, :],
        conv_ln_b[0][None, :], w_out[0].astype(BF16), norm2_g[0][None, :], w_router[0].T,
        b_router[0][:, None], tm, _tile(tm, 64))

    counts = counts[:, 0].astype(jnp.int32)
    padded = (counts + EXPERT_ROWS - 1) // EXPERT_ROWS * EXPERT_ROWS
    pad_end = jnp.cumsum(padded)
    pad_start = pad_end - padded
    n_blocks = t * TOP_K // EXPERT_ROWS + N_EXPERTS
    block_row = jnp.arange(n_blocks, dtype=jnp.int32) * EXPERT_ROWS
    block_expert = jnp.minimum(jnp.sum(block_row[:, None] >= pad_end[None, :], axis=1), N_EXPERTS - 1).astype(jnp.int32)
    n_used = (pad_end[-1:] // EXPERT_ROWS).astype(jnp.int32)
    onehot = eid[:, :, None] == jnp.arange(N_EXPERTS, dtype=jnp.int32)
    pos = rank + jnp.sum(jnp.where(onehot, pad_start, 0), axis=-1)

    x_rows = _dispatch(pad_start + counts, padded - counts, n_used, pos, hn.reshape(t, d), n_blocks,
                       _tile(t, 512))
    y_rows = _experts(block_expert, n_used, x_rows, w_up[0], b_up[0], w_down[0], b_down[0])
    out = _combine(pos, gates.T, xn.reshape(t, d), y_rows, _tile(t, 256))
    return out.reshape(b, s, d)
```

```python
import functools

import jax
import jax.numpy as jnp
from jax import lax
from jax.experimental import pallas as pl
from jax.experimental.pallas import tpu as pltpu

N_META = 16
SB_HEADS = 8
SB_HEAD_DIM = 64
SB_WIDTH = SB_HEADS * SB_HEAD_DIM
CONV_KERNEL = 31
N_EXPERTS = 32
TOP_K = 4
SWIGLU_LIMIT = 7.0
SWIGLU_ALPHA = 1.702
EPS = 1e-6
F32_EXP_UNDERFLOW = -104.0

LANES = 128
SUBLANES = 8
HALO = 32
EXPERT_ROWS = 256
VMEM_LIMIT = 56 * 1024 * 1024

F32 = jnp.float32
BF16 = jnp.bfloat16


def _params(n_axes, vmem=VMEM_LIMIT):
    return pltpu.CompilerParams(dimension_semantics=("arbitrary",) * n_axes, vmem_limit_bytes=vmem)


def _inproj_kernel(x_ref, g1_ref, w_ref, qg_ref, kg_ref, q_ref, k_ref, v_ref, h_ref):
    x = x_ref[...]
    ms = jnp.mean(x * x, axis=-1, keepdims=True)
    n = (x * lax.rsqrt(ms + EPS) * g1_ref[...]).astype(BF16)
    lo = lax.broadcasted_iota(jnp.int32, (1, LANES), 1) < SB_HEAD_DIM

    def proj(c0, c1):
        return jnp.dot(n, w_ref[:, c0:c1], preferred_element_type=F32)

    def head_norm(acc, g_ref, out_ref, scale):
        for c in range(SB_WIDTH // LANES):
            sl = slice(c * LANES, (c + 1) * LANES)
            a = acc[:, sl]
            sq = a * a
            s_lo = jnp.sum(jnp.where(lo, sq, 0.0), axis=-1, keepdims=True)
            s_hi = jnp.sum(jnp.where(lo, 0.0, sq), axis=-1, keepdims=True)
            r = lax.rsqrt(jnp.where(lo, s_lo, s_hi) * (1.0 / SB_HEAD_DIM) + EPS)
            out_ref[:, sl] = (a * r * (g_ref[:, sl] * scale)).astype(BF16)

    head_norm(proj(0, SB_WIDTH), qg_ref, q_ref, SB_HEAD_DIM ** -0.5)
    head_norm(proj(SB_WIDTH, 2 * SB_WIDTH), kg_ref, k_ref, 1.0)
    v_ref[...] = proj(2 * SB_WIDTH, 3 * SB_WIDTH).astype(BF16)
    cw = (w_ref.shape[1] - 3 * SB_WIDTH) // 2
    val = proj(3 * SB_WIDTH, 3 * SB_WIDTH + cw)
    gate = proj(3 * SB_WIDTH + cw, 3 * SB_WIDTH + 2 * cw)
    h_ref[...] = val * jax.nn.sigmoid(gate)


def _inproj(x2, g1, w_in_bf, qg, kg, tm):
    t, d = x2.shape
    cols = w_in_bf.shape[1]
    cw = (cols - 3 * SB_WIDTH) // 2
    row = lambda i: (i, 0)
    fixed = lambda i: (0, 0)
    return pl.pallas_call(
        _inproj_kernel,
        grid=(t // tm,),
        in_specs=[
            pl.BlockSpec((tm, d), row),
            pl.BlockSpec((1, d), fixed),
            pl.BlockSpec((d, cols), fixed),
            pl.BlockSpec((1, SB_WIDTH), fixed),
            pl.BlockSpec((1, SB_WIDTH), fixed),
        ],
        out_specs=[
            pl.BlockSpec((tm, SB_WIDTH), row),
            pl.BlockSpec((tm, SB_WIDTH), row),
            pl.BlockSpec((tm, SB_WIDTH), row),
            pl.BlockSpec((tm, cw), row),
        ],
        out_shape=[
            jax.ShapeDtypeStruct((t, SB_WIDTH), BF16),
            jax.ShapeDtypeStruct((t, SB_WIDTH), BF16),
            jax.ShapeDtypeStruct((t, SB_WIDTH), BF16),
            jax.ShapeDtypeStruct((t, cw), F32),
        ],
        compiler_params=_params(1),
        name="inproj",
    )(x2, g1, w_in_bf, qg, kg)


def _attn_kernel(q_ref, k_ref, v_ref, km_ref, vm_ref, g_ref, o_ref, *, tq):
    i = pl.program_id(2)
    lane = lax.broadcasted_iota(jnp.int32, (1, LANES), 1)
    lo = lane < SB_HEAD_DIM
    q2 = q_ref[0]
    zero_bf = jnp.zeros((), BF16)
    qs = (jnp.where(lo, q2, zero_bf), jnp.where(lo, zero_bf, q2))
    row = lax.broadcasted_iota(jnp.int32, (tq, tq), 0)
    col = lax.broadcasted_iota(jnp.int32, (tq, tq), 1)
    suffix = (row > col).astype(BF16)
    causal = col < row

    def block(kb, vb, carry, mask, suffix_m):
        acc, r0, r1 = carry
        rs = [r0, r1]
        for h in range(2):
            z = lax.dot_general(qs[h], kb, (((1,), (1,)), ((), ())), preferred_element_type=F32)
            l1p = jnp.log(1.0 + jnp.exp(-jnp.abs(z)))
            log_beta = jnp.minimum(z, 0.0) - l1p
            log_keep = log_beta - z
            if mask is not None:
                log_keep = jnp.where(mask, log_keep, 0.0)
            later = jnp.dot(log_keep.astype(BF16), suffix_m, preferred_element_type=F32)
            a = jnp.exp(log_beta + later + rs[h])
            if mask is not None:
                a = jnp.where(mask, a, 0.0)
            vh = jnp.where(lo, vb, zero_bf) if h == 0 else jnp.where(lo, zero_bf, vb)
            acc = acc + jnp.dot(a.astype(BF16), vh, preferred_element_type=F32)
            rs[h] = rs[h] + jnp.sum(log_keep, axis=-1, keepdims=True)
        return acc, rs[0], rs[1]

    def kv_block(j):
        start = pl.multiple_of(j * tq, tq)
        return k_ref[0, pl.ds(start, tq), :], v_ref[0, pl.ds(start, tq), :]

    def live(r0, r1):
        return jnp.max(jnp.maximum(r0, r1)) > F32_EXP_UNDERFLOW

    carry = (jnp.zeros((tq, LANES), F32), jnp.zeros((tq, 1), F32), jnp.zeros((tq, 1), F32))
    kb, vb = kv_block(i)
    acc, r0, r1 = block(kb, vb, carry, causal, suffix)

    def cond(c):
        return jnp.logical_and(c[0] < i, c[4])

    def body(c):
        n = c[0]
        kb, vb = kv_block(i - 1 - n)
        acc, r0, r1 = block(kb, vb, c[1:4], None, suffix)
        return n + 1, acc, r0, r1, live(r0, r1)

    _, acc, r0, r1, alive = lax.while_loop(cond, body, (jnp.int32(0), acc, r0, r1, live(r0, r1)))

    def meta_block(acc):
        return block(km_ref[...], vm_ref[...], (acc, r0, r1), lane < N_META, suffix[:LANES, :LANES])[0]

    acc = lax.cond(alive, meta_block, lambda acc: acc, acc)

    sq = acc * acc
    s_lo = jnp.sum(jnp.where(lo, sq, 0.0), axis=-1, keepdims=True)
    s_hi = jnp.sum(jnp.where(lo, 0.0, sq), axis=-1, keepdims=True)
    r = lax.rsqrt(jnp.where(lo, s_lo, s_hi) * (1.0 / SB_HEAD_DIM) + EPS)
    o_ref[0] = (acc * r * g_ref[...]).astype(BF16)


def _attention(q, k, v, km, vm, og, tq):
    b, s, _ = q.shape
    n_pairs = SB_WIDTH // LANES
    return pl.pallas_call(
        functools.partial(_attn_kernel, tq=tq),
        grid=(b, n_pairs, s // tq),
        in_specs=[
            pl.BlockSpec((1, tq, LANES), lambda b, p, i: (b, i, p)),
            pl.BlockSpec((1, s, LANES), lambda b, p, i: (b, 0, p)),
            pl.BlockSpec((1, s, LANES), lambda b, p, i: (b, 0, p)),
            pl.BlockSpec((LANES, LANES), lambda b, p, i: (0, p)),
            pl.BlockSpec((LANES, LANES), lambda b, p, i: (0, p)),
            pl.BlockSpec((1, LANES), lambda b, p, i: (0, p)),
        ],
        out_specs=pl.BlockSpec((1, tq, LANES), lambda b, p, i: (b, i, p)),
        out_shape=jax.ShapeDtypeStruct((b, s, SB_WIDTH), BF16),
        compiler_params=_params(3),
        name="attention",
    )(q, k, v, km, vm, og)


def _mix_kernel(h_ref, halo_ref, mhalo_ref, sb_ref, x_ref, cw_ref, cb_ref, lg_ref, lb_ref, wo_ref,
                g2_ref, wr_ref, br_ref,
                xn_ref, hn_ref, eid_ref, gate_ref, rank_ref, cnt_ref,
                win_ref, cwin_ref, cv_ref, carry_ref, *, tm, chunk):
    b = pl.program_id(0)
    i = pl.program_id(1)

    @pl.when(jnp.logical_and(b == 0, i == 0))
    def _():
        carry_ref[...] = jnp.zeros_like(carry_ref)

    @pl.when(i == 0)
    def _():
        win_ref[0:HALO, :] = mhalo_ref[...]

    @pl.when(i > 0)
    def _():
        win_ref[0:HALO, :] = halo_ref[0]

    win_ref[HALO:, :] = h_ref[0]

    first_tap = HALO - (CONV_KERNEL - 1)

    def conv_chunk(c, _):
        r0 = pl.multiple_of(c * chunk, chunk)
        cwin_ref[...] = win_ref[pl.ds(r0, chunk + HALO), :]
        acc = jnp.broadcast_to(cb_ref[...], (chunk, cb_ref.shape[1]))
        for j in range(CONV_KERNEL):
            acc = acc + cw_ref[j:j + 1, :] * cwin_ref[first_tap + j:first_tap + j + chunk, :]
        mu = jnp.mean(acc, axis=-1, keepdims=True)
        cen = acc - mu
        var = jnp.mean(cen * cen, axis=-1, keepdims=True)
        y = cen * lax.rsqrt(var + EPS) * lg_ref[...] + lb_ref[...]
        cv_ref[pl.ds(r0, chunk), :] = (y * jax.nn.sigmoid(y)).astype(BF16)
        return 0

    lax.fori_loop(0, tm // chunk, conv_chunk, 0)

    mixed = jnp.dot(sb_ref[0], wo_ref[0:SB_WIDTH, :], preferred_element_type=F32)
    mixed = mixed + jnp.dot(cv_ref[...], wo_ref[SB_WIDTH:, :], preferred_element_type=F32)
    xn = x_ref[0] + mixed
    xn_ref[0] = xn
    ms = jnp.mean(xn * xn, axis=-1, keepdims=True)
    hn = xn * lax.rsqrt(ms + EPS) * g2_ref[...]
    hn_ref[0] = hn

    logits = lax.dot_general(wr_ref[...], hn, (((1,), (1,)), ((), ())),
                             precision=lax.Precision.HIGHEST, preferred_element_type=F32) + br_ref[...]
    eidx = lax.broadcasted_iota(jnp.int32, logits.shape, 0)
    vals, idxs = [], []
    for _ in range(TOP_K):
        m = jnp.max(logits, axis=0, keepdims=True)
        sel = jnp.min(jnp.where(logits == m, eidx, N_EXPERTS), axis=0, keepdims=True)
        vals.append(m)
        idxs.append(sel)
        logits = jnp.where(eidx == sel, -jnp.inf, logits)
    exps = [jnp.exp(v - vals[0]) for v in vals]
    denom = exps[0] + exps[1] + exps[2] + exps[3]
    gate_ref[...] = jnp.concatenate([e / denom for e in exps], axis=0)
    eid_ref[...] = jnp.concatenate(idxs, axis=0)

    onehots = [(eidx == s).astype(F32) for s in idxs]
    chosen = onehots[0] + onehots[1] + onehots[2] + onehots[3]
    tr = lax.broadcasted_iota(jnp.int32, (tm, tm), 0)
    tc = lax.broadcasted_iota(jnp.int32, (tm, tm), 1)
    before = (tr < tc).astype(BF16)
    prefix = jnp.dot(chosen.astype(BF16), before, preferred_element_type=F32) + carry_ref[...]
    rank_ref[...] = jnp.concatenate(
        [jnp.sum(o * prefix, axis=0, keepdims=True) for o in onehots], axis=0).astype(jnp.int32)
    carry_ref[...] = carry_ref[...] + jnp.sum(chosen, axis=1, keepdims=True)
    cnt_ref[...] = jnp.broadcast_to(carry_ref[...], cnt_ref.shape)


def _mix(hcv, mhalo, sb, x, conv_w, conv_b, ln_g, ln_b, w_out_bf, g2, wr_t, br, tm, chunk):
    b, s, d = x.shape
    cw = hcv.shape[-1]
    t = b * s
    per = s // tm
    tile = lambda b, i: (b, i, 0)
    fixed = lambda b, i: (0, 0)
    tok = lambda b, i: (0, b * per + i)
    return pl.pallas_call(
        functools.partial(_mix_kernel, tm=tm, chunk=chunk),
        grid=(b, per),
        in_specs=[
            pl.BlockSpec((1, tm, cw), tile),
            pl.BlockSpec((1, HALO, cw), lambda b, i: (b, jnp.maximum(i * (tm // HALO) - 1, 0), 0)),
            pl.BlockSpec((HALO, cw), fixed),
            pl.BlockSpec((1, tm, SB_WIDTH), tile),
            pl.BlockSpec((1, tm, d), tile),
            pl.BlockSpec((CONV_KERNEL, cw), fixed),
            pl.BlockSpec((1, cw), fixed),
            pl.BlockSpec((1, cw), fixed),
            pl.BlockSpec((1, cw), fixed),
            pl.BlockSpec((SB_WIDTH + cw, d), fixed),
            pl.BlockSpec((1, d), fixed),
            pl.BlockSpec((N_EXPERTS, d), fixed),
            pl.BlockSpec((N_EXPERTS, 1), fixed),
        ],
        out_specs=[
            pl.BlockSpec((1, tm, d), tile),
            pl.BlockSpec((1, tm, d), tile),
            pl.BlockSpec((TOP_K, tm), tok),
            pl.BlockSpec((TOP_K, tm), tok),
            pl.BlockSpec((TOP_K, tm), tok),
            pl.BlockSpec((N_EXPERTS, LANES), fixed),
        ],
        out_shape=[
            jax.ShapeDtypeStruct((b, s, d), F32),
            jax.ShapeDtypeStruct((b, s, d), F32),
            jax.ShapeDtypeStruct((TOP_K, t), jnp.int32),
            jax.ShapeDtypeStruct((TOP_K, t), F32),
            jax.ShapeDtypeStruct((TOP_K, t), jnp.int32),
            jax.ShapeDtypeStruct((N_EXPERTS, LANES), F32),
        ],
        scratch_shapes=[
            pltpu.VMEM((tm + HALO, cw), F32),
            pltpu.VMEM((chunk + HALO, cw), F32),
            pltpu.VMEM((tm, cw), BF16),
            pltpu.VMEM((N_EXPERTS, 1), F32),
        ],
        compiler_params=_params(2),
        name="mix_router",
    )(hcv, hcv, mhalo, sb, x, conv_w, conv_b, ln_g, ln_b, w_out_bf, g2, wr_t, br)


def _wait_rows(rows_ref, n, sem):
    pltpu.make_async_copy(rows_ref.at[pl.ds(0, n)], rows_ref.at[pl.ds(0, n)], sem).wait()


def _dispatch_kernel(padlo_ref, padlen_ref, nused_ref, pos_ref, hn_ref, rows_ref, zeros_ref, sem, zsem,
                     *, tm, n_blocks, n_pad):
    i = pl.program_id(0)

    @pl.when(i == 0)
    def _():
        zeros_ref[...] = jnp.zeros_like(zeros_ref)
        for e in range(N_EXPERTS):
            lo, ln = padlo_ref[e], padlen_ref[e]
            head = (-lo) & (SUBLANES - 1)
            for r in range(SUBLANES - 1):
                @pl.when(r < head)
                def _():
                    pltpu.make_async_copy(zeros_ref.at[pl.ds(0, 1)], rows_ref.at[pl.ds(lo + r, 1)], zsem).start()

            lo8, ln8 = lo + head, ln - head
            for bit in range(SUBLANES.bit_length() - 1, EXPERT_ROWS.bit_length() - 1):
                size = 1 << bit

                @pl.when((ln8 >> bit) & 1 == 1)
                def _():
                    start = pl.multiple_of(lo8 + ((ln8 >> (bit + 1)) << (bit + 1)), SUBLANES)
                    pltpu.make_async_copy(zeros_ref.at[pl.ds(0, size)],
                                          rows_ref.at[pl.ds(start, size)], zsem).start()

        def tail(j, _):
            start = pl.multiple_of(j * EXPERT_ROWS, EXPERT_ROWS)
            pltpu.make_async_copy(zeros_ref, rows_ref.at[pl.ds(start, EXPERT_ROWS)], zsem).start()
            return 0

        lax.fori_loop(nused_ref[0], n_blocks, tail, 0)

    def issue(t, _):
        for k in range(TOP_K):
            pltpu.make_async_copy(hn_ref.at[pl.ds(t, 1)],
                                  rows_ref.at[pl.ds(pos_ref[k, t], 1)], sem).start()
        return 0

    lax.fori_loop(0, tm, issue, 0)
    _wait_rows(rows_ref, TOP_K * tm, sem)

    @pl.when(i == pl.num_programs(0) - 1)
    def _():
        _wait_rows(rows_ref, n_pad, zsem)


def _dispatch(pad_lo, pad_len, n_used, pos, hn, n_blocks, tm):
    t, d = hn.shape
    n_rows = n_blocks * EXPERT_ROWS
    grid_spec = pltpu.PrefetchScalarGridSpec(
        num_scalar_prefetch=3,
        grid=(t // tm,),
        in_specs=[
            pl.BlockSpec((TOP_K, tm), lambda i, *_: (0, i), memory_space=pltpu.SMEM),
            pl.BlockSpec((tm, d), lambda i, *_: (i, 0)),
        ],
        out_specs=pl.BlockSpec(memory_space=pl.ANY),
        scratch_shapes=[pltpu.VMEM((EXPERT_ROWS, d), hn.dtype), pltpu.SemaphoreType.DMA(()),
                        pltpu.SemaphoreType.DMA(())],
    )
    return pl.pallas_call(
        functools.partial(_dispatch_kernel, tm=tm, n_blocks=n_blocks, n_pad=n_rows - t * TOP_K),
        grid_spec=grid_spec,
        out_shape=jax.ShapeDtypeStruct((n_rows, d), hn.dtype),
        compiler_params=_params(1),
        name="dispatch",
    )(pad_lo, pad_len, n_used, pos, hn)


def _expert_kernel(be_ref, nused_ref, x_ref, wu_ref, bu_ref, wd_ref, bd_ref, y_ref,
                   wu_bf, wd_bf, *, cast_rows):
    j = pl.program_id(0)
    de = wd_ref.shape[1]

    @pl.when(j >= nused_ref[0])
    def _():
        y_ref[...] = jnp.zeros_like(y_ref)

    @pl.when(j < nused_ref[0])
    def _():
        prev = be_ref[jnp.maximum(j - 1, 0)]

        @pl.when(jnp.logical_or(j == 0, be_ref[j] != prev))
        def _():
            def cast(c, _):
                r0 = pl.multiple_of(c * cast_rows, cast_rows)
                wu_bf[pl.ds(r0, cast_rows), :] = wu_ref[0, pl.ds(r0, cast_rows), :].astype(BF16)
                wd_bf[pl.ds(r0, cast_rows), :] = wd_ref[0, pl.ds(r0, cast_rows), :].astype(BF16)
                return 0

            lax.fori_loop(0, wu_ref.shape[1] // cast_rows, cast, 0)

        up = jnp.dot(x_ref[...].astype(BF16), wu_bf[...], preferred_element_type=F32) + bu_ref[0]
        glu = jnp.minimum(up[:, :de], SWIGLU_LIMIT)
        lin = jnp.clip(up[:, de:], -SWIGLU_LIMIT, SWIGLU_LIMIT)
        act = glu * jax.nn.sigmoid(SWIGLU_ALPHA * glu) * (lin + 1.0)
        y_ref[...] = jnp.dot(act.astype(BF16), wd_bf[...], preferred_element_type=F32) + bd_ref[0]


def _experts(block_expert, n_used, x_rows, w_up, b_up, w_down, b_down):
    n_rows, d = x_rows.shape
    n_blocks = n_rows // EXPERT_ROWS
    ne, _, up_cols = w_up.shape
    de = w_down.shape[1]
    blk = lambda j, be, nu: (jnp.minimum(j, nu[0] - 1), 0)
    wsel = lambda j, be, nu: (be[jnp.minimum(j, nu[0] - 1)], 0, 0)
    grid_spec = pltpu.PrefetchScalarGridSpec(
        num_scalar_prefetch=2,
        grid=(n_blocks,),
        in_specs=[
            pl.BlockSpec((EXPERT_ROWS, d), blk),
            pl.BlockSpec((1, d, up_cols), wsel),
            pl.BlockSpec((1, 1, up_cols), wsel),
            pl.BlockSpec((1, de, d), wsel),
            pl.BlockSpec((1, 1, d), wsel),
        ],
        out_specs=pl.BlockSpec((EXPERT_ROWS, d), lambda j, be, nu: (j, 0)),
        scratch_shapes=[pltpu.VMEM((d, up_cols), BF16), pltpu.VMEM((de, d), BF16)],
    )
    return pl.pallas_call(
        functools.partial(_expert_kernel, cast_rows=64),
        grid_spec=grid_spec,
        out_shape=jax.ShapeDtypeStruct((n_rows, d), F32),
        compiler_params=_params(1),
        name="experts",
    )(block_expert, n_used, x_rows, w_up, b_up.reshape(ne, 1, up_cols),
      w_down, b_down.reshape(ne, 1, d))


def _combine_kernel(pos_ref, pos_next_ref, gate_ref, xn_ref, rows_ref, o_ref, buf, sems, *, tm):
    i = pl.program_id(0)
    n = pl.num_programs(0)
    slot = lax.rem(i, 2)

    def gather(p_ref, s):
        def issue(t, _):
            for k in range(TOP_K):
                pltpu.make_async_copy(rows_ref.at[pl.ds(p_ref[k, t], 1)],
                                      buf.at[s, pl.ds(k * tm + t, 1)], sems.at[s]).start()
            return 0

        lax.fori_loop(0, tm, issue, 0)

    @pl.when(i == 0)
    def _():
        gather(pos_ref, 0)

    @pl.when(i + 1 < n)
    def _():
        gather(pos_next_ref, 1 - slot)

    pltpu.make_async_copy(rows_ref.at[pl.ds(0, TOP_K * tm)], buf.at[slot], sems.at[slot]).wait()
    acc = xn_ref[...]
    for k in range(TOP_K):
        acc = acc + gate_ref[:, k:k + 1] * buf[slot, pl.ds(k * tm, tm), :]
    o_ref[...] = acc


def _combine(pos, gates_t, xn, y_rows, tm):
    t, d = xn.shape
    n = t // tm
    return pl.pallas_call(
        functools.partial(_combine_kernel, tm=tm),
        grid=(n,),
        in_specs=[
            pl.BlockSpec((TOP_K, tm), lambda i: (0, i), memory_space=pltpu.SMEM),
            pl.BlockSpec((TOP_K, tm), lambda i: (0, jnp.minimum(i + 1, n - 1)), memory_space=pltpu.SMEM),
            pl.BlockSpec((tm, TOP_K), lambda i: (i, 0)),
            pl.BlockSpec((tm, d), lambda i: (i, 0)),
            pl.BlockSpec(memory_space=pl.ANY),
        ],
        out_specs=pl.BlockSpec((tm, d), lambda i: (i, 0)),
        out_shape=jax.ShapeDtypeStruct((t, d), F32),
        scratch_shapes=[pltpu.VMEM((2, TOP_K * tm, d), F32), pltpu.SemaphoreType.DMA((2,))],
        compiler_params=_params(1),
        name="combine",
    )(pos, pos, gates_t, xn, y_rows)


def _tile(n, want):
    t = min(n, want)
    assert n % t == 0, (n, t)
    return t


def kernel(x, meta_tokens, norm1_g, w_in, q_norm_g, k_norm_g, conv_w, conv_b, conv_ln_g, conv_ln_b,
           attn_out_g, w_out, norm2_g, w_router, b_router, w_up, b_up, w_down, b_down):
    assert norm1_g.shape[0] == 1, "single layer: meta-token rows are only keys/values and conv context"
    b, s, d = x.shape
    t = b * s
    cw = conv_w.shape[-1]

    g1 = norm1_g[0][None, :]
    w_in_bf = w_in[0].astype(BF16)
    qg = jnp.tile(q_norm_g[0], SB_HEADS)[None, :]
    kg = jnp.tile(k_norm_g[0], SB_HEADS)[None, :]

    q, k, v, hcv = _inproj(x.reshape(t, d), g1, w_in_bf, qg, kg, _tile(t, 512))
    _, km, vm, hm = _inproj(meta_tokens, g1, w_in_bf, qg, kg, N_META)
    pad = ((0, LANES - N_META), (0, 0))
    sb = _attention(q.reshape(b, s, SB_WIDTH), k.reshape(b, s, SB_WIDTH), v.reshape(b, s, SB_WIDTH),
                    jnp.pad(km, pad), jnp.pad(vm, pad), attn_out_g[0].reshape(1, SB_WIDTH), _tile(s, 256))

    mhalo = jnp.concatenate([jnp.zeros((HALO - N_META, cw), F32), hm], axis=0)
    tm = _tile(s, 512)
    xn, hn, eid, gates, rank, counts = _mix(
        hcv.reshape(b, s, cw), mhalo, sb, x, conv_w[0], conv_b[0][None, :], conv_ln_g[0][None, :],
        conv_ln_b[0][None, :], w_out[0].astype(BF16), norm2_g[0][None, :], w_router[0].T,
        b_router[0][:, None], tm, _tile(tm, 64))

    counts = counts[:, 0].astype(jnp.int32)
    padded = (counts + EXPERT_ROWS - 1) // EXPERT_ROWS * EXPERT_ROWS
    pad_end = jnp.cumsum(padded)
    pad_start = pad_end - padded
    n_blocks = t * TOP_K // EXPERT_ROWS + N_EXPERTS
    block_row = jnp.arange(n_blocks, dtype=jnp.int32) * EXPERT_ROWS
    block_expert = jnp.minimum(jnp.sum(block_row[:, None] >= pad_end[None, :], axis=1), N_EXPERTS - 1).astype(jnp.int32)
    n_used = (pad_end[-1:] // EXPERT_ROWS).astype(jnp.int32)
    onehot = eid[:, :, None] == jnp.arange(N_EXPERTS, dtype=jnp.int32)
    pos = rank + jnp.sum(jnp.where(onehot, pad_start, 0), axis=-1)

    x_rows = _dispatch(pad_start + counts, padded - counts, n_used, pos, hn.reshape(t, d), n_blocks,
                       _tile(t, 512))
    y_rows = _experts(block_expert, n_used, x_rows, w_up[0], b_up[0], w_down[0], b_down[0])
    out = _combine(pos, gates.T, xn.reshape(t, d), y_rows, _tile(t, 256))
    return out.reshape(b, s, d)
```

```python
import functools

import jax
import jax.numpy as jnp
from jax import lax
from jax.experimental import pallas as pl
from jax.experimental.pallas import tpu as pltpu

N_META = 16
SB_HEADS = 8
SB_HEAD_DIM = 64
SB_WIDTH = SB_HEADS * SB_HEAD_DIM
CONV_KERNEL = 31
N_EXPERTS = 32
TOP_K = 4
SWIGLU_LIMIT = 7.0
SWIGLU_ALPHA = 1.702
EPS = 1e-6
F32_EXP_UNDERFLOW = -104.0

LANES = 128
SUBLANES = 8
HALO = 32
EXPERT_ROWS = 512
VMEM_LIMIT = 56 * 1024 * 1024

F32 = jnp.float32
BF16 = jnp.bfloat16


def _params(n_axes, vmem=VMEM_LIMIT):
    return pltpu.CompilerParams(dimension_semantics=("arbitrary",) * n_axes, vmem_limit_bytes=vmem)


def _inproj_kernel(x_ref, g1_ref, w_ref, qg_ref, kg_ref, q_ref, k_ref, v_ref, h_ref):
    x = x_ref[...]
    ms = jnp.mean(x * x, axis=-1, keepdims=True)
    n = (x * lax.rsqrt(ms + EPS) * g1_ref[...]).astype(BF16)
    lo = lax.broadcasted_iota(jnp.int32, (1, LANES), 1) < SB_HEAD_DIM

    def proj(c0, c1):
        return jnp.dot(n, w_ref[:, c0:c1], preferred_element_type=F32)

    def head_norm(acc, g_ref, out_ref, scale):
        for c in range(SB_WIDTH // LANES):
            sl = slice(c * LANES, (c + 1) * LANES)
            a = acc[:, sl]
            sq = a * a
            s_lo = jnp.sum(jnp.where(lo, sq, 0.0), axis=-1, keepdims=True)
            s_hi = jnp.sum(jnp.where(lo, 0.0, sq), axis=-1, keepdims=True)
            r = lax.rsqrt(jnp.where(lo, s_lo, s_hi) * (1.0 / SB_HEAD_DIM) + EPS)
            out_ref[:, sl] = (a * r * (g_ref[:, sl] * scale)).astype(BF16)

    head_norm(proj(0, SB_WIDTH), qg_ref, q_ref, SB_HEAD_DIM ** -0.5)
    head_norm(proj(SB_WIDTH, 2 * SB_WIDTH), kg_ref, k_ref, 1.0)
    v_ref[...] = proj(2 * SB_WIDTH, 3 * SB_WIDTH).astype(BF16)
    cw = (w_ref.shape[1] - 3 * SB_WIDTH) // 2
    val = proj(3 * SB_WIDTH, 3 * SB_WIDTH + cw)
    gate = proj(3 * SB_WIDTH + cw, 3 * SB_WIDTH + 2 * cw)
    h_ref[...] = val * jax.nn.sigmoid(gate)


def _inproj(x2, g1, w_in_bf, qg, kg, tm):
    t, d = x2.shape
    cols = w_in_bf.shape[1]
    cw = (cols - 3 * SB_WIDTH) // 2
    row = lambda i: (i, 0)
    fixed = lambda i: (0, 0)
    return pl.pallas_call(
        _inproj_kernel,
        grid=(t // tm,),
        in_specs=[
            pl.BlockSpec((tm, d), row),
            pl.BlockSpec((1, d), fixed),
            pl.BlockSpec((d, cols), fixed),
            pl.BlockSpec((1, SB_WIDTH), fixed),
            pl.BlockSpec((1, SB_WIDTH), fixed),
        ],
        out_specs=[
            pl.BlockSpec((tm, SB_WIDTH), row),
            pl.BlockSpec((tm, SB_WIDTH), row),
            pl.BlockSpec((tm, SB_WIDTH), row),
            pl.BlockSpec((tm, cw), row),
        ],
        out_shape=[
            jax.ShapeDtypeStruct((t, SB_WIDTH), BF16),
            jax.ShapeDtypeStruct((t, SB_WIDTH), BF16),
            jax.ShapeDtypeStruct((t, SB_WIDTH), BF16),
            jax.ShapeDtypeStruct((t, cw), F32),
        ],
        compiler_params=_params(1),
        name="inproj",
    )(x2, g1, w_in_bf, qg, kg)


def _attn_kernel(q_ref, k_ref, v_ref, km_ref, vm_ref, g_ref, o_ref, *, tq):
    i = pl.program_id(2)
    lane = lax.broadcasted_iota(jnp.int32, (1, LANES), 1)
    lo = lane < SB_HEAD_DIM
    q2 = q_ref[0]
    zero_bf = jnp.zeros((), BF16)
    qs = (jnp.where(lo, q2, zero_bf), jnp.where(lo, zero_bf, q2))
    row = lax.broadcasted_iota(jnp.int32, (tq, tq), 0)
    col = lax.broadcasted_iota(jnp.int32, (tq, tq), 1)
    suffix = (row > col).astype(BF16)
    causal = col < row

    def block(kb, vb, carry, mask, suffix_m):
        acc, r0, r1 = carry
        rs = [r0, r1]
        for h in range(2):
            z = lax.dot_general(qs[h], kb, (((1,), (1,)), ((), ())), preferred_element_type=F32)
            l1p = jnp.log(1.0 + jnp.exp(-jnp.abs(z)))
            log_beta = jnp.minimum(z, 0.0) - l1p
            log_keep = log_beta - z
            if mask is not None:
                log_keep = jnp.where(mask, log_keep, 0.0)
            later = jnp.dot(log_keep.astype(BF16), suffix_m, preferred_element_type=F32)
            a = jnp.exp(log_beta + later + rs[h])
            if mask is not None:
                a = jnp.where(mask, a, 0.0)
            vh = jnp.where(lo, vb, zero_bf) if h == 0 else jnp.where(lo, zero_bf, vb)
            acc = acc + jnp.dot(a.astype(BF16), vh, preferred_element_type=F32)
            rs[h] = rs[h] + jnp.sum(log_keep, axis=-1, keepdims=True)
        return acc, rs[0], rs[1]

    def kv_block(j):
        start = pl.multiple_of(j * tq, tq)
        return k_ref[0, pl.ds(start, tq), :], v_ref[0, pl.ds(start, tq), :]

    def live(r0, r1):
        return jnp.max(jnp.maximum(r0, r1)) > F32_EXP_UNDERFLOW

    carry = (jnp.zeros((tq, LANES), F32), jnp.zeros((tq, 1), F32), jnp.zeros((tq, 1), F32))
    kb, vb = kv_block(i)
    acc, r0, r1 = block(kb, vb, carry, causal, suffix)

    def cond(c):
        return jnp.logical_and(c[0] < i, c[4])

    def body(c):
        n = c[0]
        kb, vb = kv_block(i - 1 - n)
        acc, r0, r1 = block(kb, vb, c[1:4], None, suffix)
        return n + 1, acc, r0, r1, live(r0, r1)

    _, acc, r0, r1, alive = lax.while_loop(cond, body, (jnp.int32(0), acc, r0, r1, live(r0, r1)))

    def meta_block(acc):
        return block(km_ref[...], vm_ref[...], (acc, r0, r1), lane < N_META, suffix[:LANES, :LANES])[0]

    acc = lax.cond(alive, meta_block, lambda acc: acc, acc)

    sq = acc * acc
    s_lo = jnp.sum(jnp.where(lo, sq, 0.0), axis=-1, keepdims=True)
    s_hi = jnp.sum(jnp.where(lo, 0.0, sq), axis=-1, keepdims=True)
    r = lax.rsqrt(jnp.where(lo, s_lo, s_hi) * (1.0 / SB_HEAD_DIM) + EPS)
    o_ref[0] = (acc * r * g_ref[...]).astype(BF16)


def _attention(q, k, v, km, vm, og, tq):
    b, s, _ = q.shape
    n_pairs = SB_WIDTH // LANES
    return pl.pallas_call(
        functools.partial(_attn_kernel, tq=tq),
        grid=(b, n_pairs, s // tq),
        in_specs=[
            pl.BlockSpec((1, tq, LANES), lambda b, p, i: (b, i, p)),
            pl.BlockSpec((1, s, LANES), lambda b, p, i: (b, 0, p)),
            pl.BlockSpec((1, s, LANES), lambda b, p, i: (b, 0, p)),
            pl.BlockSpec((LANES, LANES), lambda b, p, i: (0, p)),
            pl.BlockSpec((LANES, LANES), lambda b, p, i: (0, p)),
            pl.BlockSpec((1, LANES), lambda b, p, i: (0, p)),
        ],
        out_specs=pl.BlockSpec((1, tq, LANES), lambda b, p, i: (b, i, p)),
        out_shape=jax.ShapeDtypeStruct((b, s, SB_WIDTH), BF16),
        compiler_params=_params(3),
        name="attention",
    )(q, k, v, km, vm, og)


def _mix_kernel(h_ref, halo_ref, mhalo_ref, sb_ref, x_ref, cw_ref, cb_ref, lg_ref, lb_ref, wo_ref,
                g2_ref, wr_ref, br_ref,
                xn_ref, hn_ref, eid_ref, gate_ref, rank_ref, cnt_ref,
                win_ref, cwin_ref, cv_ref, carry_ref, *, tm, chunk):
    b = pl.program_id(0)
    i = pl.program_id(1)

    @pl.when(jnp.logical_and(b == 0, i == 0))
    def _():
        carry_ref[...] = jnp.zeros_like(carry_ref)

    @pl.when(i == 0)
    def _():
        win_ref[0:HALO, :] = mhalo_ref[...]

    @pl.when(i > 0)
    def _():
        win_ref[0:HALO, :] = halo_ref[0]

    win_ref[HALO:, :] = h_ref[0]

    first_tap = HALO - (CONV_KERNEL - 1)

    def conv_chunk(c, _):
        r0 = pl.multiple_of(c * chunk, chunk)
        cwin_ref[...] = win_ref[pl.ds(r0, chunk + HALO), :]
        acc = jnp.broadcast_to(cb_ref[...], (chunk, cb_ref.shape[1]))
        for j in range(CONV_KERNEL):
            acc = acc + cw_ref[j:j + 1, :] * cwin_ref[first_tap + j:first_tap + j + chunk, :]
        mu = jnp.mean(acc, axis=-1, keepdims=True)
        cen = acc - mu
        var = jnp.mean(cen * cen, axis=-1, keepdims=True)
        y = cen * lax.rsqrt(var + EPS) * lg_ref[...] + lb_ref[...]
        cv_ref[pl.ds(r0, chunk), :] = (y * jax.nn.sigmoid(y)).astype(BF16)
        return 0

    lax.fori_loop(0, tm // chunk, conv_chunk, 0)

    mixed = jnp.dot(sb_ref[0], wo_ref[0:SB_WIDTH, :], preferred_element_type=F32)
    mixed = mixed + jnp.dot(cv_ref[...], wo_ref[SB_WIDTH:, :], preferred_element_type=F32)
    xn = x_ref[0] + mixed
    xn_ref[0] = xn
    ms = jnp.mean(xn * xn, axis=-1, keepdims=True)
    hn = xn * lax.rsqrt(ms + EPS) * g2_ref[...]
    hn_ref[0] = hn

    logits = lax.dot_general(wr_ref[...], hn, (((1,), (1,)), ((), ())),
                             precision=lax.Precision.HIGHEST, preferred_element_type=F32) + br_ref[...]
    eidx = lax.broadcasted_iota(jnp.int32, logits.shape, 0)
    vals, idxs = [], []
    for _ in range(TOP_K):
        m = jnp.max(logits, axis=0, keepdims=True)
        sel = jnp.min(jnp.where(logits == m, eidx, N_EXPERTS), axis=0, keepdims=True)
        vals.append(m)
        idxs.append(sel)
        logits = jnp.where(eidx == sel, -jnp.inf, logits)
    exps = [jnp.exp(v - vals[0]) for v in vals]
    denom = exps[0] + exps[1] + exps[2] + exps[3]
    gate_ref[...] = jnp.concatenate([e / denom for e in exps], axis=0)
    eid_ref[...] = jnp.concatenate(idxs, axis=0)

    onehots = [(eidx == s).astype(F32) for s in idxs]
    chosen = onehots[0] + onehots[1] + onehots[2] + onehots[3]
    tr = lax.broadcasted_iota(jnp.int32, (tm, tm), 0)
    tc = lax.broadcasted_iota(jnp.int32, (tm, tm), 1)
    before = (tr < tc).astype(BF16)
    prefix = jnp.dot(chosen.astype(BF16), before, preferred_element_type=F32) + carry_ref[...]
    rank_ref[...] = jnp.concatenate(
        [jnp.sum(o * prefix, axis=0, keepdims=True) for o in onehots], axis=0).astype(jnp.int32)
    carry_ref[...] = carry_ref[...] + jnp.sum(chosen, axis=1, keepdims=True)
    cnt_ref[...] = jnp.broadcast_to(carry_ref[...], cnt_ref.shape)


def _mix(hcv, mhalo, sb, x, conv_w, conv_b, ln_g, ln_b, w_out_bf, g2, wr_t, br, tm, chunk):
    b, s, d = x.shape
    cw = hcv.shape[-1]
    t = b * s
    per = s // tm
    tile = lambda b, i: (b, i, 0)
    fixed = lambda b, i: (0, 0)
    tok = lambda b, i: (0, b * per + i)
    return pl.pallas_call(
        functools.partial(_mix_kernel, tm=tm, chunk=chunk),
        grid=(b, per),
        in_specs=[
            pl.BlockSpec((1, tm, cw), tile),
            pl.BlockSpec((1, HALO, cw), lambda b, i: (b, jnp.maximum(i * (tm // HALO) - 1, 0), 0)),
            pl.BlockSpec((HALO, cw), fixed),
            pl.BlockSpec((1, tm, SB_WIDTH), tile),
            pl.BlockSpec((1, tm, d), tile),
            pl.BlockSpec((CONV_KERNEL, cw), fixed),
            pl.BlockSpec((1, cw), fixed),
            pl.BlockSpec((1, cw), fixed),
            pl.BlockSpec((1, cw), fixed),
            pl.BlockSpec((SB_WIDTH + cw, d), fixed),
            pl.BlockSpec((1, d), fixed),
            pl.BlockSpec((N_EXPERTS, d), fixed),
            pl.BlockSpec((N_EXPERTS, 1), fixed),
        ],
        out_specs=[
            pl.BlockSpec((1, tm, d), tile),
            pl.BlockSpec((1, tm, d), tile),
            pl.BlockSpec((TOP_K, tm), tok),
            pl.BlockSpec((TOP_K, tm), tok),
            pl.BlockSpec((TOP_K, tm), tok),
            pl.BlockSpec((N_EXPERTS, LANES), fixed),
        ],
        out_shape=[
            jax.ShapeDtypeStruct((b, s, d), F32),
            jax.ShapeDtypeStruct((b, s, d), F32),
            jax.ShapeDtypeStruct((TOP_K, t), jnp.int32),
            jax.ShapeDtypeStruct((TOP_K, t), F32),
            jax.ShapeDtypeStruct((TOP_K, t), jnp.int32),
            jax.ShapeDtypeStruct((N_EXPERTS, LANES), F32),
        ],
        scratch_shapes=[
            pltpu.VMEM((tm + HALO, cw), F32),
            pltpu.VMEM((chunk + HALO, cw), F32),
            pltpu.VMEM((tm, cw), BF16),
            pltpu.VMEM((N_EXPERTS, 1), F32),
        ],
        compiler_params=_params(2),
        name="mix_router",
    )(hcv, hcv, mhalo, sb, x, conv_w, conv_b, ln_g, ln_b, w_out_bf, g2, wr_t, br)


def _wait_rows(rows_ref, n, sem):
    pltpu.make_async_copy(rows_ref.at[pl.ds(0, n)], rows_ref.at[pl.ds(0, n)], sem).wait()


def _dispatch_kernel(padlo_ref, padlen_ref, nused_ref, pos_ref, hn_ref, rows_ref, zeros_ref, sem, zsem,
                     *, tm, n_blocks, n_pad):
    i = pl.program_id(0)

    @pl.when(i == 0)
    def _():
        zeros_ref[...] = jnp.zeros_like(zeros_ref)
        for e in range(N_EXPERTS):
            lo, ln = padlo_ref[e], padlen_ref[e]
            head = (-lo) & (SUBLANES - 1)
            for r in range(SUBLANES - 1):
                @pl.when(r < head)
                def _():
                    pltpu.make_async_copy(zeros_ref.at[pl.ds(0, 1)], rows_ref.at[pl.ds(lo + r, 1)], zsem).start()

            lo8, ln8 = lo + head, ln - head
            for bit in range(SUBLANES.bit_length() - 1, EXPERT_ROWS.bit_length() - 1):
                size = 1 << bit

                @pl.when((ln8 >> bit) & 1 == 1)
                def _():
                    start = pl.multiple_of(lo8 + ((ln8 >> (bit + 1)) << (bit + 1)), SUBLANES)
                    pltpu.make_async_copy(zeros_ref.at[pl.ds(0, size)],
                                          rows_ref.at[pl.ds(start, size)], zsem).start()

        def tail(j, _):
            start = pl.multiple_of(j * EXPERT_ROWS, EXPERT_ROWS)
            pltpu.make_async_copy(zeros_ref, rows_ref.at[pl.ds(start, EXPERT_ROWS)], zsem).start()
            return 0

        lax.fori_loop(nused_ref[0], n_blocks, tail, 0)

    def issue(t, _):
        for k in range(TOP_K):
            pltpu.make_async_copy(hn_ref.at[pl.ds(t, 1)],
                                  rows_ref.at[pl.ds(pos_ref[k, t], 1)], sem).start()
        return 0

    lax.fori_loop(0, tm, issue, 0)
    _wait_rows(rows_ref, TOP_K * tm, sem)

    @pl.when(i == pl.num_programs(0) - 1)
    def _():
        _wait_rows(rows_ref, n_pad, zsem)


def _dispatch(pad_lo, pad_len, n_used, pos, hn, n_blocks, tm):
    t, d = hn.shape
    n_rows = n_blocks * EXPERT_ROWS
    grid_spec = pltpu.PrefetchScalarGridSpec(
        num_scalar_prefetch=3,
        grid=(t // tm,),
        in_specs=[
            pl.BlockSpec((TOP_K, tm), lambda i, *_: (0, i), memory_space=pltpu.SMEM),
            pl.BlockSpec((tm, d), lambda i, *_: (i, 0)),
        ],
        out_specs=pl.BlockSpec(memory_space=pl.ANY),
        scratch_shapes=[pltpu.VMEM((EXPERT_ROWS, d), hn.dtype), pltpu.SemaphoreType.DMA(()),
                        pltpu.SemaphoreType.DMA(())],
    )
    return pl.pallas_call(
        functools.partial(_dispatch_kernel, tm=tm, n_blocks=n_blocks, n_pad=n_rows - t * TOP_K),
        grid_spec=grid_spec,
        out_shape=jax.ShapeDtypeStruct((n_rows, d), hn.dtype),
        compiler_params=_params(1),
        name="dispatch",
    )(pad_lo, pad_len, n_used, pos, hn)


def _expert_kernel(be_ref, nused_ref, x_ref, wu_ref, bu_ref, wd_ref, bd_ref, y_ref,
                   wu_bf, wd_bf, *, cast_rows):
    j = pl.program_id(0)
    de = wd_ref.shape[1]

    @pl.when(j >= nused_ref[0])
    def _():
        y_ref[...] = jnp.zeros_like(y_ref)

    @pl.when(j < nused_ref[0])
    def _():
        prev = be_ref[jnp.maximum(j - 1, 0)]

        @pl.when(jnp.logical_or(j == 0, be_ref[j] != prev))
        def _():
            def cast(c, _):
                r0 = pl.multiple_of(c * cast_rows, cast_rows)
                wu_bf[pl.ds(r0, cast_rows), :] = wu_ref[0, pl.ds(r0, cast_rows), :].astype(BF16)
                wd_bf[pl.ds(r0, cast_rows), :] = wd_ref[0, pl.ds(r0, cast_rows), :].astype(BF16)
                return 0

            lax.fori_loop(0, wu_ref.shape[1] // cast_rows, cast, 0)

        up = jnp.dot(x_ref[...].astype(BF16), wu_bf[...], preferred_element_type=F32) + bu_ref[0]
        glu = jnp.minimum(up[:, :de], SWIGLU_LIMIT)
        lin = jnp.clip(up[:, de:], -SWIGLU_LIMIT, SWIGLU_LIMIT)
        act = glu * jax.nn.sigmoid(SWIGLU_ALPHA * glu) * (lin + 1.0)
        y_ref[...] = jnp.dot(act.astype(BF16), wd_bf[...], preferred_element_type=F32) + bd_ref[0]


def _experts(block_expert, n_used, x_rows, w_up, b_up, w_down, b_down):
    n_rows, d = x_rows.shape
    n_blocks = n_rows // EXPERT_ROWS
    ne, _, up_cols = w_up.shape
    de = w_down.shape[1]
    blk = lambda j, be, nu: (jnp.minimum(j, nu[0] - 1), 0)
    wsel = lambda j, be, nu: (be[jnp.minimum(j, nu[0] - 1)], 0, 0)
    grid_spec = pltpu.PrefetchScalarGridSpec(
        num_scalar_prefetch=2,
        grid=(n_blocks,),
        in_specs=[
            pl.BlockSpec((EXPERT_ROWS, d), blk),
            pl.BlockSpec((1, d, up_cols), wsel),
            pl.BlockSpec((1, 1, up_cols), wsel),
            pl.BlockSpec((1, de, d), wsel),
            pl.BlockSpec((1, 1, d), wsel),
        ],
        out_specs=pl.BlockSpec((EXPERT_ROWS, d), lambda j, be, nu: (j, 0)),
        scratch_shapes=[pltpu.VMEM((d, up_cols), BF16), pltpu.VMEM((de, d), BF16)],
    )
    return pl.pallas_call(
        functools.partial(_expert_kernel, cast_rows=64),
        grid_spec=grid_spec,
        out_shape=jax.ShapeDtypeStruct((n_rows, d), F32),
        compiler_params=_params(1),
        name="experts",
    )(block_expert, n_used, x_rows, w_up, b_up.reshape(ne, 1, up_cols),
      w_down, b_down.reshape(ne, 1, d))


def _combine_kernel(pos_ref, pos_next_ref, gate_ref, xn_ref, rows_ref, o_ref, buf, sems, *, tm):
    i = pl.program_id(0)
    n = pl.num_programs(0)
    slot = lax.rem(i, 2)

    def gather(p_ref, s):
        def issue(t, _):
            for k in range(TOP_K):
                pltpu.make_async_copy(rows_ref.at[pl.ds(p_ref[k, t], 1)],
                                      buf.at[s, pl.ds(k * tm + t, 1)], sems.at[s]).start()
            return 0

        lax.fori_loop(0, tm, issue, 0)

    @pl.when(i == 0)
    def _():
        gather(pos_ref, 0)

    @pl.when(i + 1 < n)
    def _():
        gather(pos_next_ref, 1 - slot)

    pltpu.make_async_copy(rows_ref.at[pl.ds(0, TOP_K * tm)], buf.at[slot], sems.at[slot]).wait()
    acc = xn_ref[...]
    for k in range(TOP_K):
        acc = acc + gate_ref[:, k:k + 1] * buf[slot, pl.ds(k * tm, tm), :]
    o_ref[...] = acc


def _combine(pos, gates_t, xn, y_rows, tm):
    t, d = xn.shape
    n = t // tm
    return pl.pallas_call(
        functools.partial(_combine_kernel, tm=tm),
        grid=(n,),
        in_specs=[
            pl.BlockSpec((TOP_K, tm), lambda i: (0, i), memory_space=pltpu.SMEM),
            pl.BlockSpec((TOP_K, tm), lambda i: (0, jnp.minimum(i + 1, n - 1)), memory_space=pltpu.SMEM),
            pl.BlockSpec((tm, TOP_K), lambda i: (i, 0)),
            pl.BlockSpec((tm, d), lambda i: (i, 0)),
            pl.BlockSpec(memory_space=pl.ANY),
        ],
        out_specs=pl.BlockSpec((tm, d), lambda i: (i, 0)),
        out_shape=jax.ShapeDtypeStruct((t, d), F32),
        scratch_shapes=[pltpu.VMEM((2, TOP_K * tm, d), F32), pltpu.SemaphoreType.DMA((2,))],
        compiler_params=_params(1),
        name="combine",
    )(pos, pos, gates_t, xn, y_rows)


def _tile(n, want):
    t = min(n, want)
    assert n % t == 0, (n, t)
    return t


def kernel(x, meta_tokens, norm1_g, w_in, q_norm_g, k_norm_g, conv_w, conv_b, conv_ln_g, conv_ln_b,
           attn_out_g, w_out, norm2_g, w_router, b_router, w_up, b_up, w_down, b_down):
    assert norm1_g.shape[0] == 1, "single layer: meta-token rows are only keys/values and conv context"
    b, s, d = x.shape
    t = b * s
    cw = conv_w.shape[-1]

    g1 = norm1_g[0][None, :]
    w_in_bf = w_in[0].astype(BF16)
    qg = jnp.tile(q_norm_g[0], SB_HEADS)[None, :]
    kg = jnp.tile(k_norm_g[0], SB_HEADS)[None, :]

    q, k, v, hcv = _inproj(x.reshape(t, d), g1, w_in_bf, qg, kg, _tile(t, 512))
    _, km, vm, hm = _inproj(meta_tokens, g1, w_in_bf, qg, kg, N_META)
    pad = ((0, LANES - N_META), (0, 0))
    sb = _attention(q.reshape(b, s, SB_WIDTH), k.reshape(b, s, SB_WIDTH), v.reshape(b, s, SB_WIDTH),
                    jnp.pad(km, pad), jnp.pad(vm, pad), attn_out_g[0].reshape(1, SB_WIDTH), _tile(s, 256))

    mhalo = jnp.concatenate([jnp.zeros((HALO - N_META, cw), F32), hm], axis=0)
    tm = _tile(s, 512)
    xn, hn, eid, gates, rank, counts = _mix(
        hcv.reshape(b, s, cw), mhalo, sb, x, conv_w[0], conv_b[0][None, :], conv_ln_g[0][None, :],
        conv_ln_b[0][None, :], w_out[0].astype(BF16), norm2_g[0][None, :], w_router[0].T,
        b_router[0][:, None], tm, _tile(tm, 64))

    counts = counts[:, 0].astype(jnp.int32)
    padded = (counts + EXPERT_ROWS - 1) // EXPERT_ROWS * EXPERT_ROWS
    pad_end = jnp.cumsum(padded)
    pad_start = pad_end - padded
    n_blocks = t * TOP_K // EXPERT_ROWS + N_EXPERTS
    block_row = jnp.arange(n_blocks, dtype=jnp.int32) * EXPERT_ROWS
    block_expert = jnp.minimum(jnp.sum(block_row[:, None] >= pad_end[None, :], axis=1), N_EXPERTS - 1).astype(jnp.int32)
    n_used = (pad_end[-1:] // EXPERT_ROWS).astype(jnp.int32)
    onehot = eid[:, :, None] == jnp.arange(N_EXPERTS, dtype=jnp.int32)
    pos = rank + jnp.sum(jnp.where(onehot, pad_start, 0), axis=-1)

    x_rows = _dispatch(pad_start + counts, padded - counts, n_used, pos, hn.reshape(t, d), n_blocks,
                       _tile(t, 512))
    y_rows = _experts(block_expert, n_used, x_rows, w_up[0], b_up[0], w_down[0], b_down[0])
    out = _combine(pos, gates.T, xn.reshape(t, d), y_rows, _tile(t, 256))
    return out.reshape(b, s, d)
```

```python
import functools

import jax
import jax.numpy as jnp
from jax import lax
from jax.experimental import pallas as pl
from jax.experimental.pallas import tpu as pltpu

N_META = 16
SB_HEADS = 8
SB_HEAD_DIM = 64
SB_WIDTH = SB_HEADS * SB_HEAD_DIM
CONV_KERNEL = 31
N_EXPERTS = 32
TOP_K = 4
SWIGLU_LIMIT = 7.0
SWIGLU_ALPHA = 1.702
EPS = 1e-6
F32_EXP_UNDERFLOW = -104.0

LANES = 128
SUBLANES = 8
MXU_K = 256
PIECE = 16
HALO = 32
EXPERT_ROWS = 512
VMEM_LIMIT = 56 * 1024 * 1024

F32 = jnp.float32
BF16 = jnp.bfloat16


def _params(n_axes, vmem=VMEM_LIMIT):
    return pltpu.CompilerParams(dimension_semantics=("arbitrary",) * n_axes, vmem_limit_bytes=vmem)


def _inproj_kernel(x_ref, g1_ref, w_ref, qg_ref, kg_ref, q_ref, k_ref, v_ref, h_ref):
    x = x_ref[...]
    ms = jnp.mean(x * x, axis=-1, keepdims=True)
    n = (x * lax.rsqrt(ms + EPS) * g1_ref[...]).astype(BF16)
    lo = lax.broadcasted_iota(jnp.int32, (1, LANES), 1) < SB_HEAD_DIM

    def proj(c0, c1):
        return jnp.dot(n, w_ref[:, c0:c1], preferred_element_type=F32)

    def head_norm(acc, g_ref, out_ref, scale):
        for c in range(SB_WIDTH // LANES):
            sl = slice(c * LANES, (c + 1) * LANES)
            a = acc[:, sl]
            sq = a * a
            s_lo = jnp.sum(jnp.where(lo, sq, 0.0), axis=-1, keepdims=True)
            s_hi = jnp.sum(jnp.where(lo, 0.0, sq), axis=-1, keepdims=True)
            r = lax.rsqrt(jnp.where(lo, s_lo, s_hi) * (1.0 / SB_HEAD_DIM) + EPS)
            out_ref[:, sl] = (a * r * (g_ref[:, sl] * scale)).astype(BF16)

    head_norm(proj(0, SB_WIDTH), qg_ref, q_ref, SB_HEAD_DIM ** -0.5)
    head_norm(proj(SB_WIDTH, 2 * SB_WIDTH), kg_ref, k_ref, 1.0)
    v_ref[...] = proj(2 * SB_WIDTH, 3 * SB_WIDTH).astype(BF16)
    cw = (w_ref.shape[1] - 3 * SB_WIDTH) // 2
    val = proj(3 * SB_WIDTH, 3 * SB_WIDTH + cw)
    gate = proj(3 * SB_WIDTH + cw, 3 * SB_WIDTH + 2 * cw)
    h_ref[...] = val * jax.nn.sigmoid(gate)


def _inproj(x2, g1, w_in_bf, qg, kg, tm):
    t, d = x2.shape
    cols = w_in_bf.shape[1]
    cw = (cols - 3 * SB_WIDTH) // 2
    row = lambda i: (i, 0)
    fixed = lambda i: (0, 0)
    return pl.pallas_call(
        _inproj_kernel,
        grid=(t // tm,),
        in_specs=[
            pl.BlockSpec((tm, d), row),
            pl.BlockSpec((1, d), fixed),
            pl.BlockSpec((d, cols), fixed),
            pl.BlockSpec((1, SB_WIDTH), fixed),
            pl.BlockSpec((1, SB_WIDTH), fixed),
        ],
        out_specs=[
            pl.BlockSpec((tm, SB_WIDTH), row),
            pl.BlockSpec((tm, SB_WIDTH), row),
            pl.BlockSpec((tm, SB_WIDTH), row),
            pl.BlockSpec((tm, cw), row),
        ],
        out_shape=[
            jax.ShapeDtypeStruct((t, SB_WIDTH), BF16),
            jax.ShapeDtypeStruct((t, SB_WIDTH), BF16),
            jax.ShapeDtypeStruct((t, SB_WIDTH), BF16),
            jax.ShapeDtypeStruct((t, cw), F32),
        ],
        compiler_params=_params(1),
        name="inproj",
    )(x2, g1, w_in_bf, qg, kg)


def _attn_kernel(q_ref, k_ref, v_ref, km_ref, vm_ref, g_ref, o_ref, *, tq):
    i = pl.program_id(2)
    lane = lax.broadcasted_iota(jnp.int32, (1, LANES), 1)
    lo = lane < SB_HEAD_DIM
    q2 = q_ref[0]
    zero_bf = jnp.zeros((), BF16)
    qs = (jnp.where(lo, q2, zero_bf), jnp.where(lo, zero_bf, q2))
    row = lax.broadcasted_iota(jnp.int32, (tq, tq), 0)
    col = lax.broadcasted_iota(jnp.int32, (tq, tq), 1)
    suffix = (row > col).astype(BF16)
    causal = col < row

    def block(kb, vb, carry, mask, suffix_m):
        acc, r0, r1 = carry
        rs = [r0, r1]
        for h in range(2):
            z = lax.dot_general(qs[h], kb, (((1,), (1,)), ((), ())), preferred_element_type=F32)
            l1p = jnp.log(1.0 + jnp.exp(-jnp.abs(z)))
            log_beta = jnp.minimum(z, 0.0) - l1p
            log_keep = log_beta - z
            if mask is not None:
                log_keep = jnp.where(mask, log_keep, 0.0)
            later = jnp.dot(log_keep.astype(BF16), suffix_m, preferred_element_type=F32)
            a = jnp.exp(log_beta + later + rs[h])
            if mask is not None:
                a = jnp.where(mask, a, 0.0)
            vh = jnp.where(lo, vb, zero_bf) if h == 0 else jnp.where(lo, zero_bf, vb)
            acc = acc + jnp.dot(a.astype(BF16), vh, preferred_element_type=F32)
            rs[h] = rs[h] + jnp.sum(log_keep, axis=-1, keepdims=True)
        return acc, rs[0], rs[1]

    def kv_block(j):
        start = pl.multiple_of(j * tq, tq)
        return k_ref[0, pl.ds(start, tq), :], v_ref[0, pl.ds(start, tq), :]

    def live(r0, r1):
        return jnp.max(jnp.maximum(r0, r1)) > F32_EXP_UNDERFLOW

    carry = (jnp.zeros((tq, LANES), F32), jnp.zeros((tq, 1), F32), jnp.zeros((tq, 1), F32))
    kb, vb = kv_block(i)
    acc, r0, r1 = block(kb, vb, carry, causal, suffix)

    def cond(c):
        return jnp.logical_and(c[0] < i, c[4])

    def body(c):
        n = c[0]
        kb, vb = kv_block(i - 1 - n)
        acc, r0, r1 = block(kb, vb, c[1:4], None, suffix)
        return n + 1, acc, r0, r1, live(r0, r1)

    _, acc, r0, r1, alive = lax.while_loop(cond, body, (jnp.int32(0), acc, r0, r1, live(r0, r1)))

    def meta_block(acc):
        return block(km_ref[...], vm_ref[...], (acc, r0, r1), lane < N_META, suffix[:LANES, :LANES])[0]

    acc = lax.cond(alive, meta_block, lambda acc: acc, acc)

    sq = acc * acc
    s_lo = jnp.sum(jnp.where(lo, sq, 0.0), axis=-1, keepdims=True)
    s_hi = jnp.sum(jnp.where(lo, 0.0, sq), axis=-1, keepdims=True)
    r = lax.rsqrt(jnp.where(lo, s_lo, s_hi) * (1.0 / SB_HEAD_DIM) + EPS)
    o_ref[0] = (acc * r * g_ref[...]).astype(BF16)


def _attention(q, k, v, km, vm, og, tq):
    b, s, _ = q.shape
    n_pairs = SB_WIDTH // LANES
    return pl.pallas_call(
        functools.partial(_attn_kernel, tq=tq),
        grid=(b, n_pairs, s // tq),
        in_specs=[
            pl.BlockSpec((1, tq, LANES), lambda b, p, i: (b, i, p)),
            pl.BlockSpec((1, s, LANES), lambda b, p, i: (b, 0, p)),
            pl.BlockSpec((1, s, LANES), lambda b, p, i: (b, 0, p)),
            pl.BlockSpec((LANES, LANES), lambda b, p, i: (0, p)),
            pl.BlockSpec((LANES, LANES), lambda b, p, i: (0, p)),
            pl.BlockSpec((1, LANES), lambda b, p, i: (0, p)),
        ],
        out_specs=pl.BlockSpec((1, tq, LANES), lambda b, p, i: (b, i, p)),
        out_shape=jax.ShapeDtypeStruct((b, s, SB_WIDTH), BF16),
        compiler_params=_params(3),
        name="attention",
    )(q, k, v, km, vm, og)


def _mix_kernel(h_ref, halo_ref, mhalo_ref, sb_ref, x_ref, cw_ref, cb_ref, lg_ref, lb_ref, wo_ref,
                g2_ref, wr_ref, br_ref,
                xn_ref, hn_ref, eid_ref, gate_ref, rank_ref, before_ref, cnt_ref,
                win_ref, cwin_ref, cv_ref, carry_ref, *, tm, chunk):
    b = pl.program_id(0)
    i = pl.program_id(1)

    @pl.when(jnp.logical_and(b == 0, i == 0))
    def _():
        carry_ref[...] = jnp.zeros_like(carry_ref)

    @pl.when(i == 0)
    def _():
        win_ref[0:HALO, :] = mhalo_ref[...]

    @pl.when(i > 0)
    def _():
        win_ref[0:HALO, :] = halo_ref[0]

    win_ref[HALO:, :] = h_ref[0]

    first_tap = HALO - (CONV_KERNEL - 1)

    def conv_chunk(c, _):
        r0 = pl.multiple_of(c * chunk, chunk)
        cwin_ref[...] = win_ref[pl.ds(r0, chunk + HALO), :]
        acc = jnp.broadcast_to(cb_ref[...], (chunk, cb_ref.shape[1]))
        for j in range(CONV_KERNEL):
            acc = acc + cw_ref[j:j + 1, :] * cwin_ref[first_tap + j:first_tap + j + chunk, :]
        mu = jnp.mean(acc, axis=-1, keepdims=True)
        cen = acc - mu
        var = jnp.mean(cen * cen, axis=-1, keepdims=True)
        y = cen * lax.rsqrt(var + EPS) * lg_ref[...] + lb_ref[...]
        cv_ref[pl.ds(r0, chunk), :] = (y * jax.nn.sigmoid(y)).astype(BF16)
        return 0

    lax.fori_loop(0, tm // chunk, conv_chunk, 0)

    mixed = jnp.dot(sb_ref[0], wo_ref[0:SB_WIDTH, :], preferred_element_type=F32)
    mixed = mixed + jnp.dot(cv_ref[...], wo_ref[SB_WIDTH:, :], preferred_element_type=F32)
    xn = x_ref[0] + mixed
    xn_ref[0] = xn
    ms = jnp.mean(xn * xn, axis=-1, keepdims=True)
    hn = xn * lax.rsqrt(ms + EPS) * g2_ref[...]
    hn_ref[0] = hn

    logits = lax.dot_general(wr_ref[...], hn, (((1,), (1,)), ((), ())),
                             precision=lax.Precision.HIGHEST, preferred_element_type=F32) + br_ref[...]
    eidx = lax.broadcasted_iota(jnp.int32, logits.shape, 0)
    vals, idxs = [], []
    for _ in range(TOP_K):
        m = jnp.max(logits, axis=0, keepdims=True)
        sel = jnp.min(jnp.where(logits == m, eidx, N_EXPERTS), axis=0, keepdims=True)
        vals.append(m)
        idxs.append(sel)
        logits = jnp.where(eidx == sel, -jnp.inf, logits)
    exps = [jnp.exp(v - vals[0]) for v in vals]
    denom = exps[0] + exps[1] + exps[2] + exps[3]
    gate_ref[...] = jnp.concatenate([e / denom for e in exps], axis=0)
    eid_ref[...] = jnp.concatenate(idxs, axis=0)

    onehots = [(eidx == s).astype(F32) for s in idxs]
    chosen = onehots[0] + onehots[1] + onehots[2] + onehots[3]
    tr = lax.broadcasted_iota(jnp.int32, (tm, tm), 0)
    tc = lax.broadcasted_iota(jnp.int32, (tm, tm), 1)
    before = (tr < tc).astype(BF16)
    prefix = jnp.dot(chosen.astype(BF16), before, preferred_element_type=F32) + carry_ref[...]
    rank_ref[...] = jnp.concatenate(
        [jnp.sum(o * prefix, axis=0, keepdims=True) for o in onehots], axis=0).astype(jnp.int32)
    before_ref[...] = prefix.astype(jnp.int32)
    carry_ref[...] = carry_ref[...] + jnp.sum(chosen, axis=1, keepdims=True)
    cnt_ref[...] = jnp.broadcast_to(carry_ref[...], cnt_ref.shape)


def _mix(hcv, mhalo, sb, x, conv_w, conv_b, ln_g, ln_b, w_out_bf, g2, wr_t, br, tm, chunk):
    b, s, d = x.shape
    cw = hcv.shape[-1]
    t = b * s
    per = s // tm
    tile = lambda b, i: (b, i, 0)
    fixed = lambda b, i: (0, 0)
    tok = lambda b, i: (0, b * per + i)
    return pl.pallas_call(
        functools.partial(_mix_kernel, tm=tm, chunk=chunk),
        grid=(b, per),
        in_specs=[
            pl.BlockSpec((1, tm, cw), tile),
            pl.BlockSpec((1, HALO, cw), lambda b, i: (b, jnp.maximum(i * (tm // HALO) - 1, 0), 0)),
            pl.BlockSpec((HALO, cw), fixed),
            pl.BlockSpec((1, tm, SB_WIDTH), tile),
            pl.BlockSpec((1, tm, d), tile),
            pl.BlockSpec((CONV_KERNEL, cw), fixed),
            pl.BlockSpec((1, cw), fixed),
            pl.BlockSpec((1, cw), fixed),
            pl.BlockSpec((1, cw), fixed),
            pl.BlockSpec((SB_WIDTH + cw, d), fixed),
            pl.BlockSpec((1, d), fixed),
            pl.BlockSpec((N_EXPERTS, d), fixed),
            pl.BlockSpec((N_EXPERTS, 1), fixed),
        ],
        out_specs=[
            pl.BlockSpec((1, tm, d), tile),
            pl.BlockSpec((1, tm, d), tile),
            pl.BlockSpec((TOP_K, tm), tok),
            pl.BlockSpec((TOP_K, tm), tok),
            pl.BlockSpec((TOP_K, tm), tok),
            pl.BlockSpec((N_EXPERTS, tm), tok),
            pl.BlockSpec((N_EXPERTS, LANES), fixed),
        ],
        out_shape=[
            jax.ShapeDtypeStruct((b, s, d), F32),
            jax.ShapeDtypeStruct((b, s, d), F32),
            jax.ShapeDtypeStruct((TOP_K, t), jnp.int32),
            jax.ShapeDtypeStruct((TOP_K, t), F32),
            jax.ShapeDtypeStruct((TOP_K, t), jnp.int32),
            jax.ShapeDtypeStruct((N_EXPERTS, t), jnp.int32),
            jax.ShapeDtypeStruct((N_EXPERTS, LANES), F32),
        ],
        scratch_shapes=[
            pltpu.VMEM((tm + HALO, cw), F32),
            pltpu.VMEM((chunk + HALO, cw), F32),
            pltpu.VMEM((tm, cw), BF16),
            pltpu.VMEM((N_EXPERTS, 1), F32),
        ],
        compiler_params=_params(2),
        name="mix_router",
    )(hcv, hcv, mhalo, sb, x, conv_w, conv_b, ln_g, ln_b, w_out_bf, g2, wr_t, br)


def _wait_rows(rows_ref, n, sem):
    pltpu.make_async_copy(rows_ref.at[pl.ds(0, n)], rows_ref.at[pl.ds(0, n)], sem).wait()


def _dispatch_kernel(padlo_ref, padlen_ref, nused_ref, pos_ref, hn_ref, rows_ref, zeros_ref, sem, zsem,
                     *, tm, n_blocks, n_pad):
    i = pl.program_id(0)

    @pl.when(i == 0)
    def _():
        zeros_ref[...] = jnp.zeros_like(zeros_ref)
        for e in range(N_EXPERTS):
            lo, ln = padlo_ref[e], padlen_ref[e]
            head = (-lo) & (SUBLANES - 1)
            for r in range(SUBLANES - 1):
                @pl.when(r < head)
                def _():
                    pltpu.make_async_copy(zeros_ref.at[pl.ds(0, 1)], rows_ref.at[pl.ds(lo + r, 1)], zsem).start()

            lo8, ln8 = lo + head, ln - head
            for bit in range(SUBLANES.bit_length() - 1, EXPERT_ROWS.bit_length() - 1):
                size = 1 << bit

                @pl.when((ln8 >> bit) & 1 == 1)
                def _():
                    start = pl.multiple_of(lo8 + ((ln8 >> (bit + 1)) << (bit + 1)), SUBLANES)
                    pltpu.make_async_copy(zeros_ref.at[pl.ds(0, size)],
                                          rows_ref.at[pl.ds(start, size)], zsem).start()

        def tail(j, _):
            start = pl.multiple_of(j * EXPERT_ROWS, EXPERT_ROWS)
            pltpu.make_async_copy(zeros_ref, rows_ref.at[pl.ds(start, EXPERT_ROWS)], zsem).start()
            return 0

        lax.fori_loop(nused_ref[0], n_blocks, tail, 0)

    def issue(t, _):
        for k in range(TOP_K):
            pltpu.make_async_copy(hn_ref.at[pl.ds(t, 1)],
                                  rows_ref.at[pl.ds(pos_ref[k, t], 1)], sem).start()
        return 0

    lax.fori_loop(0, tm, issue, 0)
    _wait_rows(rows_ref, TOP_K * tm, sem)

    @pl.when(i == pl.num_programs(0) - 1)
    def _():
        _wait_rows(rows_ref, n_pad, zsem)


def _dispatch(pad_lo, pad_len, n_used, pos, hn, n_blocks, tm):
    t, d = hn.shape
    n_rows = n_blocks * EXPERT_ROWS
    grid_spec = pltpu.PrefetchScalarGridSpec(
        num_scalar_prefetch=3,
        grid=(t // tm,),
        in_specs=[
            pl.BlockSpec((TOP_K, tm), lambda i, *_: (0, i), memory_space=pltpu.SMEM),
            pl.BlockSpec((tm, d), lambda i, *_: (i, 0)),
        ],
        out_specs=pl.BlockSpec(memory_space=pl.ANY),
        scratch_shapes=[pltpu.VMEM((EXPERT_ROWS, d), hn.dtype), pltpu.SemaphoreType.DMA(()),
                        pltpu.SemaphoreType.DMA(())],
    )
    return pl.pallas_call(
        functools.partial(_dispatch_kernel, tm=tm, n_blocks=n_blocks, n_pad=n_rows - t * TOP_K),
        grid_spec=grid_spec,
        out_shape=jax.ShapeDtypeStruct((n_rows, d), hn.dtype),
        compiler_params=_params(1),
        name="dispatch",
    )(pad_lo, pad_len, n_used, pos, hn)


def _expert_kernel(be_ref, nused_ref, x_ref, wu_ref, bu_ref, wd_ref, bd_ref, y_ref,
                   wu_bf, wd_bf, *, cast_rows):
    j = pl.program_id(0)
    de = wd_ref.shape[1]

    @pl.when(j >= nused_ref[0])
    def _():
        y_ref[...] = jnp.zeros_like(y_ref)

    @pl.when(j < nused_ref[0])
    def _():
        prev = be_ref[jnp.maximum(j - 1, 0)]

        @pl.when(jnp.logical_or(j == 0, be_ref[j] != prev))
        def _():
            def cast(c, _):
                r0 = pl.multiple_of(c * cast_rows, cast_rows)
                wu_bf[pl.ds(r0, cast_rows), :] = wu_ref[0, pl.ds(r0, cast_rows), :].astype(BF16)
                wd_bf[pl.ds(r0, cast_rows), :] = wd_ref[0, pl.ds(r0, cast_rows), :].astype(BF16)
                return 0

            lax.fori_loop(0, wu_ref.shape[1] // cast_rows, cast, 0)

        up = jnp.dot(x_ref[...].astype(BF16), wu_bf[...], preferred_element_type=F32) + bu_ref[0]
        glu = jnp.minimum(up[:, :de], SWIGLU_LIMIT)
        lin = jnp.clip(up[:, de:], -SWIGLU_LIMIT, SWIGLU_LIMIT)
        act = glu * jax.nn.sigmoid(SWIGLU_ALPHA * glu) * (lin + 1.0)
        y = jnp.dot(act.astype(BF16), wd_bf[...], preferred_element_type=F32) + bd_ref[0]
        y_ref[...] = _pack_halves(y)


def _pack_halves(y):
    half = y.shape[1] // 2
    bits = lambda v: lax.bitcast_convert_type(v.astype(BF16).astype(F32), jnp.uint32)
    return (bits(y[:, half:]) & jnp.uint32(0xFFFF0000)) | (bits(y[:, :half]) >> 16)


def _unpack_halves(w):
    lo = lax.bitcast_convert_type(w << 16, F32).astype(BF16)
    hi = lax.bitcast_convert_type(w & jnp.uint32(0xFFFF0000), F32).astype(BF16)
    return lo, hi


def _experts(block_expert, n_used, x_rows, w_up, b_up, w_down, b_down):
    n_rows, d = x_rows.shape
    n_blocks = n_rows // EXPERT_ROWS + 1
    ne, _, up_cols = w_up.shape
    de = w_down.shape[1]
    blk = lambda j, be, nu: (jnp.minimum(j, nu[0] - 1), 0)
    wsel = lambda j, be, nu: (be[jnp.minimum(j, nu[0] - 1)], 0, 0)
    grid_spec = pltpu.PrefetchScalarGridSpec(
        num_scalar_prefetch=2,
        grid=(n_blocks,),
        in_specs=[
            pl.BlockSpec((EXPERT_ROWS, d), blk),
            pl.BlockSpec((1, d, up_cols), wsel),
            pl.BlockSpec((1, 1, up_cols), wsel),
            pl.BlockSpec((1, de, d), wsel),
            pl.BlockSpec((1, 1, d), wsel),
        ],
        out_specs=pl.BlockSpec((EXPERT_ROWS, d // 2), lambda j, be, nu: (j, 0)),
        scratch_shapes=[pltpu.VMEM((d, up_cols), BF16), pltpu.VMEM((de, d), BF16)],
    )
    return pl.pallas_call(
        functools.partial(_expert_kernel, cast_rows=64),
        grid_spec=grid_spec,
        out_shape=jax.ShapeDtypeStruct((n_blocks * EXPERT_ROWS, d // 2), jnp.uint32),
        compiler_params=_params(1),
        name="experts",
    )(block_expert, n_used, x_rows, w_up, b_up.reshape(ne, 1, up_cols),
      w_down, b_down.reshape(ne, 1, d))


def _combine_kernel(rstart_ref, rcnt_ref, eid_ref, pos_ref, gate_ref, xn_ref, rows_ref, o_ref, buf, sems,
                    *, tm, cap):
    i = pl.program_id(0)
    n = pl.num_programs(0)
    slot = lax.rem(i, 2)
    half = o_ref.shape[1] // 2

    def windows(tile):
        out, off = [], jnp.int32(0)
        for e in range(N_EXPERTS):
            start = rstart_ref[tile * N_EXPERTS + e]
            cnt = rcnt_ref[tile * N_EXPERTS + e]
            first = (start >> 3) << 3
            pieces = jnp.where(cnt > 0, (start - first + cnt + PIECE - 1) // PIECE, 0)
            out.append((first, pieces, off))
            off = off + pieces * PIECE
        return out, off

    def piece_copy(first, off, p, s):
        src = rows_ref.at[pl.ds(pl.multiple_of(first + p * PIECE, SUBLANES), PIECE)]
        dst = buf.at[s, pl.ds(pl.multiple_of(off + p * PIECE, PIECE), PIECE)]
        return pltpu.make_async_copy(src, dst, sems.at[s])

    def fetch(tile, s, wait):
        for first, pieces, off in windows(tile)[0]:
            def body(p, _):
                c = piece_copy(first, off, p, s)
                c.wait() if wait else c.start()
                return 0

            lax.fori_loop(0, pieces, body, 0)

    @pl.when(i == 0)
    def _():
        buf[...] = jnp.zeros_like(buf)
        fetch(0, 0, False)

    @pl.when(i + 1 < n)
    def _():
        fetch(i + 1, 1 - slot, False)

    wins, total = windows(i)
    eid = eid_ref[...]
    staged = pos_ref[...]
    shift = jnp.zeros_like(staged)
    for e, (first, _, off) in enumerate(wins):
        shift = jnp.where(eid == e, off - first, shift)
    staged = staged + shift
    gates = gate_ref[...]

    fetch(i, slot, True)
    o_ref[...] = xn_ref[...]

    def chunk(c, _):
        c0 = pl.multiple_of(c * MXU_K, MXU_K)
        col = lax.broadcasted_iota(jnp.int32, (tm, MXU_K), 1) + c0
        sel = jnp.zeros((tm, MXU_K), F32)
        for k in range(TOP_K):
            sel = sel + jnp.where(staged[:, k:k + 1] == col, gates[:, k:k + 1], 0.0)
        sel = sel.astype(BF16)
        lo, hi = _unpack_halves(buf[slot, pl.ds(c0, MXU_K), :])
        o_ref[:, :half] += jnp.dot(sel, lo, preferred_element_type=F32)
        o_ref[:, half:] += jnp.dot(sel, hi, preferred_element_type=F32)
        return 0

    lax.fori_loop(0, (total + MXU_K - 1) // MXU_K, chunk, 0)


def _combine(rstart, rcnt, eid_t, pos_t, gates_t, xn, y_rows, tm):
    t, d = xn.shape
    n = t // tm
    cap = -(-(TOP_K * tm + 2 * PIECE * N_EXPERTS) // MXU_K) * MXU_K
    tok = lambda i, *_: (i, 0)
    grid_spec = pltpu.PrefetchScalarGridSpec(
        num_scalar_prefetch=2,
        grid=(n,),
        in_specs=[
            pl.BlockSpec((tm, TOP_K), tok),
            pl.BlockSpec((tm, TOP_K), tok),
            pl.BlockSpec((tm, TOP_K), tok),
            pl.BlockSpec((tm, d), tok),
            pl.BlockSpec(memory_space=pl.ANY),
        ],
        out_specs=pl.BlockSpec((tm, d), tok),
        scratch_shapes=[pltpu.VMEM((2, cap, d // 2), jnp.uint32), pltpu.SemaphoreType.DMA((2,))],
    )
    return pl.pallas_call(
        functools.partial(_combine_kernel, tm=tm, cap=cap),
        grid_spec=grid_spec,
        out_shape=jax.ShapeDtypeStruct((t, d), F32),
        compiler_params=_params(1),
        name="combine",
    )(rstart, rcnt, eid_t, pos_t, gates_t, xn, y_rows)


def _tile(n, want):
    t = min(n, want)
    assert n % t == 0, (n, t)
    return t


def kernel(x, meta_tokens, norm1_g, w_in, q_norm_g, k_norm_g, conv_w, conv_b, conv_ln_g, conv_ln_b,
           attn_out_g, w_out, norm2_g, w_router, b_router, w_up, b_up, w_down, b_down):
    assert norm1_g.shape[0] == 1, "single layer: meta-token rows are only keys/values and conv context"
    b, s, d = x.shape
    t = b * s
    cw = conv_w.shape[-1]

    g1 = norm1_g[0][None, :]
    w_in_bf = w_in[0].astype(BF16)
    qg = jnp.tile(q_norm_g[0], SB_HEADS)[None, :]
    kg = jnp.tile(k_norm_g[0], SB_HEADS)[None, :]

    q, k, v, hcv = _inproj(x.reshape(t, d), g1, w_in_bf, qg, kg, _tile(t, 512))
    _, km, vm, hm = _inproj(meta_tokens, g1, w_in_bf, qg, kg, N_META)
    pad = ((0, LANES - N_META), (0, 0))
    sb = _attention(q.reshape(b, s, SB_WIDTH), k.reshape(b, s, SB_WIDTH), v.reshape(b, s, SB_WIDTH),
                    jnp.pad(km, pad), jnp.pad(vm, pad), attn_out_g[0].reshape(1, SB_WIDTH), _tile(s, 256))

    mhalo = jnp.concatenate([jnp.zeros((HALO - N_META, cw), F32), hm], axis=0)
    tm = _tile(s, 512)
    xn, hn, eid, gates, rank, before, counts = _mix(
        hcv.reshape(b, s, cw), mhalo, sb, x, conv_w[0], conv_b[0][None, :], conv_ln_g[0][None, :],
        conv_ln_b[0][None, :], w_out[0].astype(BF16), norm2_g[0][None, :], w_router[0].T,
        b_router[0][:, None], tm, _tile(tm, 64))

    counts = counts[:, 0].astype(jnp.int32)
    padded = (counts + EXPERT_ROWS - 1) // EXPERT_ROWS * EXPERT_ROWS
    pad_end = jnp.cumsum(padded)
    pad_start = pad_end - padded
    n_blocks = t * TOP_K // EXPERT_ROWS + N_EXPERTS
    block_row = jnp.arange(n_blocks, dtype=jnp.int32) * EXPERT_ROWS
    block_expert = jnp.minimum(jnp.sum(block_row[:, None] >= pad_end[None, :], axis=1), N_EXPERTS - 1).astype(jnp.int32)
    n_used = (pad_end[-1:] // EXPERT_ROWS).astype(jnp.int32)
    onehot = eid[:, :, None] == jnp.arange(N_EXPERTS, dtype=jnp.int32)
    pos = rank + jnp.sum(jnp.where(onehot, pad_start, 0), axis=-1)

    x_rows = _dispatch(pad_start + counts, padded - counts, n_used, pos, hn.reshape(t, d), n_blocks,
                       _tile(t, 512))
    y_rows = _experts(block_expert, n_used, x_rows, w_up[0], b_up[0], w_down[0], b_down[0])
    tc = _tile(t, 256)
    run_start = pad_start[:, None] + before[:, ::tc]
    run_end = jnp.concatenate([run_start[:, 1:], (pad_start + counts)[:, None]], axis=1)
    out = _combine(run_start.T.reshape(-1), (run_end - run_start).T.reshape(-1), eid.T, pos.T, gates.T,
                   xn.reshape(t, d), y_rows, tc)
    return out.reshape(b, s, d)
```

```python
import functools

import jax
import jax.numpy as jnp
from jax import lax
from jax.experimental import pallas as pl
from jax.experimental.pallas import tpu as pltpu

N_META = 16
SB_HEADS = 8
SB_HEAD_DIM = 64
SB_WIDTH = SB_HEADS * SB_HEAD_DIM
CONV_KERNEL = 31
N_EXPERTS = 32
TOP_K = 4
SWIGLU_LIMIT = 7.0
SWIGLU_ALPHA = 1.702
EPS = 1e-6
F32_EXP_UNDERFLOW = -104.0

LANES = 128
SUBLANES = 8
PIECE = 16
COMBINE_CHUNK = 512
HALO = 32
EXPERT_ROWS = 512
VMEM_LIMIT = 56 * 1024 * 1024

F32 = jnp.float32
BF16 = jnp.bfloat16


def _params(n_axes, vmem=VMEM_LIMIT):
    return pltpu.CompilerParams(dimension_semantics=("arbitrary",) * n_axes, vmem_limit_bytes=vmem)


def _inproj_kernel(x_ref, g1_ref, w_ref, qg_ref, kg_ref, q_ref, k_ref, v_ref, h_ref):
    x = x_ref[...]
    ms = jnp.mean(x * x, axis=-1, keepdims=True)
    n = (x * lax.rsqrt(ms + EPS) * g1_ref[...]).astype(BF16)
    lo = lax.broadcasted_iota(jnp.int32, (1, LANES), 1) < SB_HEAD_DIM

    def proj(c0, c1):
        return jnp.dot(n, w_ref[:, c0:c1], preferred_element_type=F32)

    def head_norm(acc, g_ref, out_ref, scale):
        for c in range(SB_WIDTH // LANES):
            sl = slice(c * LANES, (c + 1) * LANES)
            a = acc[:, sl]
            sq = a * a
            s_lo = jnp.sum(jnp.where(lo, sq, 0.0), axis=-1, keepdims=True)
            s_hi = jnp.sum(jnp.where(lo, 0.0, sq), axis=-1, keepdims=True)
            r = lax.rsqrt(jnp.where(lo, s_lo, s_hi) * (1.0 / SB_HEAD_DIM) + EPS)
            out_ref[:, sl] = (a * r * (g_ref[:, sl] * scale)).astype(BF16)

    head_norm(proj(0, SB_WIDTH), qg_ref, q_ref, SB_HEAD_DIM ** -0.5)
    head_norm(proj(SB_WIDTH, 2 * SB_WIDTH), kg_ref, k_ref, 1.0)
    v_ref[...] = proj(2 * SB_WIDTH, 3 * SB_WIDTH).astype(BF16)
    cw = (w_ref.shape[1] - 3 * SB_WIDTH) // 2
    val = proj(3 * SB_WIDTH, 3 * SB_WIDTH + cw)
    gate = proj(3 * SB_WIDTH + cw, 3 * SB_WIDTH + 2 * cw)
    h_ref[...] = val * jax.nn.sigmoid(gate)


def _inproj(x2, g1, w_in_bf, qg, kg, tm):
    t, d = x2.shape
    cols = w_in_bf.shape[1]
    cw = (cols - 3 * SB_WIDTH) // 2
    row = lambda i: (i, 0)
    fixed = lambda i: (0, 0)
    return pl.pallas_call(
        _inproj_kernel,
        grid=(t // tm,),
        in_specs=[
            pl.BlockSpec((tm, d), row),
            pl.BlockSpec((1, d), fixed),
            pl.BlockSpec((d, cols), fixed),
            pl.BlockSpec((1, SB_WIDTH), fixed),
            pl.BlockSpec((1, SB_WIDTH), fixed),
        ],
        out_specs=[
            pl.BlockSpec((tm, SB_WIDTH), row),
            pl.BlockSpec((tm, SB_WIDTH), row),
            pl.BlockSpec((tm, SB_WIDTH), row),
            pl.BlockSpec((tm, cw), row),
        ],
        out_shape=[
            jax.ShapeDtypeStruct((t, SB_WIDTH), BF16),
            jax.ShapeDtypeStruct((t, SB_WIDTH), BF16),
            jax.ShapeDtypeStruct((t, SB_WIDTH), BF16),
            jax.ShapeDtypeStruct((t, cw), F32),
        ],
        compiler_params=_params(1),
        name="inproj",
    )(x2, g1, w_in_bf, qg, kg)


def _attn_kernel(q_ref, k_ref, v_ref, km_ref, vm_ref, g_ref, o_ref, *, tq):
    i = pl.program_id(2)
    lane = lax.broadcasted_iota(jnp.int32, (1, LANES), 1)
    lo = lane < SB_HEAD_DIM
    q2 = q_ref[0]
    zero_bf = jnp.zeros((), BF16)
    qs = (jnp.where(lo, q2, zero_bf), jnp.where(lo, zero_bf, q2))
    row = lax.broadcasted_iota(jnp.int32, (tq, tq), 0)
    col = lax.broadcasted_iota(jnp.int32, (tq, tq), 1)
    suffix = (row > col).astype(BF16)
    causal = col < row

    def block(kb, vb, carry, mask, suffix_m):
        acc, r0, r1 = carry
        rs = [r0, r1]
        for h in range(2):
            z = lax.dot_general(qs[h], kb, (((1,), (1,)), ((), ())), preferred_element_type=F32)
            l1p = jnp.log(1.0 + jnp.exp(-jnp.abs(z)))
            log_beta = jnp.minimum(z, 0.0) - l1p
            log_keep = log_beta - z
            if mask is not None:
                log_keep = jnp.where(mask, log_keep, 0.0)
            later = jnp.dot(log_keep.astype(BF16), suffix_m, preferred_element_type=F32)
            a = jnp.exp(log_beta + later + rs[h])
            if mask is not None:
                a = jnp.where(mask, a, 0.0)
            vh = jnp.where(lo, vb, zero_bf) if h == 0 else jnp.where(lo, zero_bf, vb)
            acc = acc + jnp.dot(a.astype(BF16), vh, preferred_element_type=F32)
            rs[h] = rs[h] + jnp.sum(log_keep, axis=-1, keepdims=True)
        return acc, rs[0], rs[1]

    def kv_block(j):
        start = pl.multiple_of(j * tq, tq)
        return k_ref[0, pl.ds(start, tq), :], v_ref[0, pl.ds(start, tq), :]

    def live(r0, r1):
        return jnp.max(jnp.maximum(r0, r1)) > F32_EXP_UNDERFLOW

    carry = (jnp.zeros((tq, LANES), F32), jnp.zeros((tq, 1), F32), jnp.zeros((tq, 1), F32))
    kb, vb = kv_block(i)
    acc, r0, r1 = block(kb, vb, carry, causal, suffix)

    def cond(c):
        return jnp.logical_and(c[0] < i, c[4])

    def body(c):
        n = c[0]
        kb, vb = kv_block(i - 1 - n)
        acc, r0, r1 = block(kb, vb, c[1:4], None, suffix)
        return n + 1, acc, r0, r1, live(r0, r1)

    _, acc, r0, r1, alive = lax.while_loop(cond, body, (jnp.int32(0), acc, r0, r1, live(r0, r1)))

    def meta_block(acc):
        return block(km_ref[...], vm_ref[...], (acc, r0, r1), lane < N_META, suffix[:LANES, :LANES])[0]

    acc = lax.cond(alive, meta_block, lambda acc: acc, acc)

    sq = acc * acc
    s_lo = jnp.sum(jnp.where(lo, sq, 0.0), axis=-1, keepdims=True)
    s_hi = jnp.sum(jnp.where(lo, 0.0, sq), axis=-1, keepdims=True)
    r = lax.rsqrt(jnp.where(lo, s_lo, s_hi) * (1.0 / SB_HEAD_DIM) + EPS)
    o_ref[0] = (acc * r * g_ref[...]).astype(BF16)


def _attention(q, k, v, km, vm, og, tq):
    b, s, _ = q.shape
    n_pairs = SB_WIDTH // LANES
    return pl.pallas_call(
        functools.partial(_attn_kernel, tq=tq),
        grid=(b, n_pairs, s // tq),
        in_specs=[
            pl.BlockSpec((1, tq, LANES), lambda b, p, i: (b, i, p)),
            pl.BlockSpec((1, s, LANES), lambda b, p, i: (b, 0, p)),
            pl.BlockSpec((1, s, LANES), lambda b, p, i: (b, 0, p)),
            pl.BlockSpec((LANES, LANES), lambda b, p, i: (0, p)),
            pl.BlockSpec((LANES, LANES), lambda b, p, i: (0, p)),
            pl.BlockSpec((1, LANES), lambda b, p, i: (0, p)),
        ],
        out_specs=pl.BlockSpec((1, tq, LANES), lambda b, p, i: (b, i, p)),
        out_shape=jax.ShapeDtypeStruct((b, s, SB_WIDTH), BF16),
        compiler_params=_params(3),
        name="attention",
    )(q, k, v, km, vm, og)


def _mix_kernel(h_ref, halo_ref, mhalo_ref, sb_ref, x_ref, cw_ref, cb_ref, lg_ref, lb_ref, wo_ref,
                g2_ref, wr_ref, br_ref,
                xn_ref, hn_ref, eid_ref, gate_ref, rank_ref, before_ref, cnt_ref,
                win_ref, cwin_ref, cv_ref, carry_ref, *, tm, chunk):
    b = pl.program_id(0)
    i = pl.program_id(1)

    @pl.when(jnp.logical_and(b == 0, i == 0))
    def _():
        carry_ref[...] = jnp.zeros_like(carry_ref)

    @pl.when(i == 0)
    def _():
        win_ref[0:HALO, :] = mhalo_ref[...]

    @pl.when(i > 0)
    def _():
        win_ref[0:HALO, :] = halo_ref[0]

    win_ref[HALO:, :] = h_ref[0]

    first_tap = HALO - (CONV_KERNEL - 1)

    def conv_chunk(c, _):
        r0 = pl.multiple_of(c * chunk, chunk)
        cwin_ref[...] = win_ref[pl.ds(r0, chunk + HALO), :]
        acc = jnp.broadcast_to(cb_ref[...], (chunk, cb_ref.shape[1]))
        for j in range(CONV_KERNEL):
            acc = acc + cw_ref[j:j + 1, :] * cwin_ref[first_tap + j:first_tap + j + chunk, :]
        mu = jnp.mean(acc, axis=-1, keepdims=True)
        cen = acc - mu
        var = jnp.mean(cen * cen, axis=-1, keepdims=True)
        y = cen * lax.rsqrt(var + EPS) * lg_ref[...] + lb_ref[...]
        cv_ref[pl.ds(r0, chunk), :] = (y * jax.nn.sigmoid(y)).astype(BF16)
        return 0

    lax.fori_loop(0, tm // chunk, conv_chunk, 0)

    mixed = jnp.dot(sb_ref[0], wo_ref[0:SB_WIDTH, :], preferred_element_type=F32)
    mixed = mixed + jnp.dot(cv_ref[...], wo_ref[SB_WIDTH:, :], preferred_element_type=F32)
    xn = x_ref[0] + mixed
    xn_ref[0] = xn
    ms = jnp.mean(xn * xn, axis=-1, keepdims=True)
    hn = xn * lax.rsqrt(ms + EPS) * g2_ref[...]
    hn_ref[0] = hn

    logits = lax.dot_general(wr_ref[...], hn, (((1,), (1,)), ((), ())),
                             precision=lax.Precision.HIGHEST, preferred_element_type=F32) + br_ref[...]
    eidx = lax.broadcasted_iota(jnp.int32, logits.shape, 0)
    vals, idxs = [], []
    for _ in range(TOP_K):
        m = jnp.max(logits, axis=0, keepdims=True)
        sel = jnp.min(jnp.where(logits == m, eidx, N_EXPERTS), axis=0, keepdims=True)
        vals.append(m)
        idxs.append(sel)
        logits = jnp.where(eidx == sel, -jnp.inf, logits)
    exps = [jnp.exp(v - vals[0]) for v in vals]
    denom = exps[0] + exps[1] + exps[2] + exps[3]
    gate_ref[...] = jnp.concatenate([e / denom for e in exps], axis=0)
    eid_ref[...] = jnp.concatenate(idxs, axis=0)

    onehots = [(eidx == s).astype(F32) for s in idxs]
    chosen = onehots[0] + onehots[1] + onehots[2] + onehots[3]
    tr = lax.broadcasted_iota(jnp.int32, (tm, tm), 0)
    tc = lax.broadcasted_iota(jnp.int32, (tm, tm), 1)
    before = (tr < tc).astype(BF16)
    prefix = jnp.dot(chosen.astype(BF16), before, preferred_element_type=F32) + carry_ref[...]
    rank_ref[...] = jnp.concatenate(
        [jnp.sum(o * prefix, axis=0, keepdims=True) for o in onehots], axis=0).astype(jnp.int32)
    before_ref[...] = prefix.astype(jnp.int32)
    carry_ref[...] = carry_ref[...] + jnp.sum(chosen, axis=1, keepdims=True)
    cnt_ref[...] = jnp.broadcast_to(carry_ref[...], cnt_ref.shape)


def _mix(hcv, mhalo, sb, x, conv_w, conv_b, ln_g, ln_b, w_out_bf, g2, wr_t, br, tm, chunk):
    b, s, d = x.shape
    cw = hcv.shape[-1]
    t = b * s
    per = s // tm
    tile = lambda b, i: (b, i, 0)
    fixed = lambda b, i: (0, 0)
    tok = lambda b, i: (0, b * per + i)
    return pl.pallas_call(
        functools.partial(_mix_kernel, tm=tm, chunk=chunk),
        grid=(b, per),
        in_specs=[
            pl.BlockSpec((1, tm, cw), tile),
            pl.BlockSpec((1, HALO, cw), lambda b, i: (b, jnp.maximum(i * (tm // HALO) - 1, 0), 0)),
            pl.BlockSpec((HALO, cw), fixed),
            pl.BlockSpec((1, tm, SB_WIDTH), tile),
            pl.BlockSpec((1, tm, d), tile),
            pl.BlockSpec((CONV_KERNEL, cw), fixed),
            pl.BlockSpec((1, cw), fixed),
            pl.BlockSpec((1, cw), fixed),
            pl.BlockSpec((1, cw), fixed),
            pl.BlockSpec((SB_WIDTH + cw, d), fixed),
            pl.BlockSpec((1, d), fixed),
            pl.BlockSpec((N_EXPERTS, d), fixed),
            pl.BlockSpec((N_EXPERTS, 1), fixed),
        ],
        out_specs=[
            pl.BlockSpec((1, tm, d), tile),
            pl.BlockSpec((1, tm, d), tile),
            pl.BlockSpec((TOP_K, tm), tok),
            pl.BlockSpec((TOP_K, tm), tok),
            pl.BlockSpec((TOP_K, tm), tok),
            pl.BlockSpec((N_EXPERTS, tm), tok),
            pl.BlockSpec((N_EXPERTS, LANES), fixed),
        ],
        out_shape=[
            jax.ShapeDtypeStruct((b, s, d), F32),
            jax.ShapeDtypeStruct((b, s, d), F32),
            jax.ShapeDtypeStruct((TOP_K, t), jnp.int32),
            jax.ShapeDtypeStruct((TOP_K, t), F32),
            jax.ShapeDtypeStruct((TOP_K, t), jnp.int32),
            jax.ShapeDtypeStruct((N_EXPERTS, t), jnp.int32),
            jax.ShapeDtypeStruct((N_EXPERTS, LANES), F32),
        ],
        scratch_shapes=[
            pltpu.VMEM((tm + HALO, cw), F32),
            pltpu.VMEM((chunk + HALO, cw), F32),
            pltpu.VMEM((tm, cw), BF16),
            pltpu.VMEM((N_EXPERTS, 1), F32),
        ],
        compiler_params=_params(2),
        name="mix_router",
    )(hcv, hcv, mhalo, sb, x, conv_w, conv_b, ln_g, ln_b, w_out_bf, g2, wr_t, br)


def _wait_rows(rows_ref, n, sem):
    pltpu.make_async_copy(rows_ref.at[pl.ds(0, n)], rows_ref.at[pl.ds(0, n)], sem).wait()


def _dispatch_kernel(padlo_ref, padlen_ref, nused_ref, pos_ref, hn_ref, rows_ref, zeros_ref, sem, zsem,
                     *, tm, n_blocks, n_pad):
    i = pl.program_id(0)

    @pl.when(i == 0)
    def _():
        zeros_ref[...] = jnp.zeros_like(zeros_ref)
        for e in range(N_EXPERTS):
            lo, ln = padlo_ref[e], padlen_ref[e]
            head = (-lo) & (SUBLANES - 1)
            for r in range(SUBLANES - 1):
                @pl.when(r < head)
                def _():
                    pltpu.make_async_copy(zeros_ref.at[pl.ds(0, 1)], rows_ref.at[pl.ds(lo + r, 1)], zsem).start()

            lo8, ln8 = lo + head, ln - head
            for bit in range(SUBLANES.bit_length() - 1, EXPERT_ROWS.bit_length() - 1):
                size = 1 << bit

                @pl.when((ln8 >> bit) & 1 == 1)
                def _():
                    start = pl.multiple_of(lo8 + ((ln8 >> (bit + 1)) << (bit + 1)), SUBLANES)
                    pltpu.make_async_copy(zeros_ref.at[pl.ds(0, size)],
                                          rows_ref.at[pl.ds(start, size)], zsem).start()

        def tail(j, _):
            start = pl.multiple_of(j * EXPERT_ROWS, EXPERT_ROWS)
            pltpu.make_async_copy(zeros_ref, rows_ref.at[pl.ds(start, EXPERT_ROWS)], zsem).start()
            return 0

        lax.fori_loop(nused_ref[0], n_blocks, tail, 0)

    def issue(t, _):
        for k in range(TOP_K):
            pltpu.make_async_copy(hn_ref.at[pl.ds(t, 1)],
                                  rows_ref.at[pl.ds(pos_ref[k, t], 1)], sem).start()
        return 0

    lax.fori_loop(0, tm, issue, 0)
    _wait_rows(rows_ref, TOP_K * tm, sem)

    @pl.when(i == pl.num_programs(0) - 1)
    def _():
        _wait_rows(rows_ref, n_pad, zsem)


def _dispatch(pad_lo, pad_len, n_used, pos, hn, n_blocks, tm):
    t, d = hn.shape
    n_rows = n_blocks * EXPERT_ROWS
    grid_spec = pltpu.PrefetchScalarGridSpec(
        num_scalar_prefetch=3,
        grid=(t // tm,),
        in_specs=[
            pl.BlockSpec((TOP_K, tm), lambda i, *_: (0, i), memory_space=pltpu.SMEM),
            pl.BlockSpec((tm, d), lambda i, *_: (i, 0)),
        ],
        out_specs=pl.BlockSpec(memory_space=pl.ANY),
        scratch_shapes=[pltpu.VMEM((EXPERT_ROWS, d), hn.dtype), pltpu.SemaphoreType.DMA(()),
                        pltpu.SemaphoreType.DMA(())],
    )
    return pl.pallas_call(
        functools.partial(_dispatch_kernel, tm=tm, n_blocks=n_blocks, n_pad=n_rows - t * TOP_K),
        grid_spec=grid_spec,
        out_shape=jax.ShapeDtypeStruct((n_rows, d), hn.dtype),
        compiler_params=_params(1),
        name="dispatch",
    )(pad_lo, pad_len, n_used, pos, hn)


def _expert_kernel(be_ref, nused_ref, x_ref, wu_ref, bu_ref, wd_ref, bd_ref, y_ref,
                   wu_bf, wd_bf, *, cast_rows):
    j = pl.program_id(0)
    de = wd_ref.shape[1]

    @pl.when(j >= nused_ref[0])
    def _():
        y_ref[...] = jnp.zeros_like(y_ref)

    @pl.when(j < nused_ref[0])
    def _():
        prev = be_ref[jnp.maximum(j - 1, 0)]

        @pl.when(jnp.logical_or(j == 0, be_ref[j] != prev))
        def _():
            def cast(c, _):
                r0 = pl.multiple_of(c * cast_rows, cast_rows)
                wu_bf[pl.ds(r0, cast_rows), :] = wu_ref[0, pl.ds(r0, cast_rows), :].astype(BF16)
                wd_bf[pl.ds(r0, cast_rows), :] = wd_ref[0, pl.ds(r0, cast_rows), :].astype(BF16)
                return 0

            lax.fori_loop(0, wu_ref.shape[1] // cast_rows, cast, 0)

        up = jnp.dot(x_ref[...].astype(BF16), wu_bf[...], preferred_element_type=F32) + bu_ref[0]
        glu = jnp.minimum(up[:, :de], SWIGLU_LIMIT)
        lin = jnp.clip(up[:, de:], -SWIGLU_LIMIT, SWIGLU_LIMIT)
        act = glu * jax.nn.sigmoid(SWIGLU_ALPHA * glu) * (lin + 1.0)
        y = jnp.dot(act.astype(BF16), wd_bf[...], preferred_element_type=F32) + bd_ref[0]
        y_ref[...] = _pack_halves(y)


def _pack_halves(y):
    half = y.shape[1] // 2
    bits = lambda v: lax.bitcast_convert_type(v.astype(BF16).astype(F32), jnp.uint32)
    return (bits(y[:, half:]) & jnp.uint32(0xFFFF0000)) | (bits(y[:, :half]) >> 16)


def _unpack_halves(w):
    lo = lax.bitcast_convert_type(w << 16, F32).astype(BF16)
    hi = lax.bitcast_convert_type(w & jnp.uint32(0xFFFF0000), F32).astype(BF16)
    return lo, hi


def _experts(block_expert, n_used, x_rows, w_up, b_up, w_down, b_down):
    n_rows, d = x_rows.shape
    n_blocks = n_rows // EXPERT_ROWS + 1
    ne, _, up_cols = w_up.shape
    de = w_down.shape[1]
    blk = lambda j, be, nu: (jnp.minimum(j, nu[0] - 1), 0)
    wsel = lambda j, be, nu: (be[jnp.minimum(j, nu[0] - 1)], 0, 0)
    grid_spec = pltpu.PrefetchScalarGridSpec(
        num_scalar_prefetch=2,
        grid=(n_blocks,),
        in_specs=[
            pl.BlockSpec((EXPERT_ROWS, d), blk),
            pl.BlockSpec((1, d, up_cols), wsel),
            pl.BlockSpec((1, 1, up_cols), wsel),
            pl.BlockSpec((1, de, d), wsel),
            pl.BlockSpec((1, 1, d), wsel),
        ],
        out_specs=pl.BlockSpec((EXPERT_ROWS, d // 2), lambda j, be, nu: (j, 0)),
        scratch_shapes=[pltpu.VMEM((d, up_cols), BF16), pltpu.VMEM((de, d), BF16)],
    )
    return pl.pallas_call(
        functools.partial(_expert_kernel, cast_rows=64),
        grid_spec=grid_spec,
        out_shape=jax.ShapeDtypeStruct((n_blocks * EXPERT_ROWS, d // 2), jnp.uint32),
        compiler_params=_params(1),
        name="experts",
    )(block_expert, n_used, x_rows, w_up, b_up.reshape(ne, 1, up_cols),
      w_down, b_down.reshape(ne, 1, d))


def _combine_kernel(rstart_ref, rcnt_ref, eid_ref, pos_ref, gate_ref, xn_ref, rows_ref, o_ref, buf, sems,
                    *, tm, cap):
    i = pl.program_id(0)
    n = pl.num_programs(0)
    slot = lax.rem(i, 2)
    half = o_ref.shape[1] // 2

    def windows(tile):
        out, off = [], jnp.int32(0)
        for e in range(N_EXPERTS):
            start = rstart_ref[tile * N_EXPERTS + e]
            cnt = rcnt_ref[tile * N_EXPERTS + e]
            first = (start >> 3) << 3
            pieces = jnp.where(cnt > 0, (start - first + cnt + PIECE - 1) // PIECE, 0)
            out.append((first, pieces, off))
            off = off + pieces * PIECE
        return out, off

    def fetch(tile, s):
        for first, pieces, off in windows(tile)[0]:
            def body(p, _):
                src = rows_ref.at[pl.ds(pl.multiple_of(first + p * PIECE, SUBLANES), PIECE)]
                dst = buf.at[s, pl.ds(pl.multiple_of(off + p * PIECE, PIECE), PIECE)]
                pltpu.make_async_copy(src, dst, sems.at[s]).start()
                return 0

            lax.fori_loop(0, pieces, body, 0)

    @pl.when(i == 0)
    def _():
        buf[...] = jnp.zeros_like(buf)
        fetch(0, 0)

    @pl.when(i + 1 < n)
    def _():
        fetch(i + 1, 1 - slot)

    wins, total = windows(i)
    eid = eid_ref[...]
    staged = pos_ref[...]
    shift = jnp.zeros_like(staged)
    for e, (first, _, off) in enumerate(wins):
        shift = jnp.where(eid == e, off - first, shift)
    staged = staged + shift
    gates = gate_ref[...]

    @pl.when(total > 0)
    def _():
        rows = pl.ds(0, pl.multiple_of(total, PIECE))
        pltpu.make_async_copy(rows_ref.at[rows], buf.at[slot, rows], sems.at[slot]).wait()

    o_ref[...] = xn_ref[...]

    def chunk(c, _):
        c0 = pl.multiple_of(c * COMBINE_CHUNK, COMBINE_CHUNK)
        col = lax.broadcasted_iota(jnp.int32, (tm, COMBINE_CHUNK), 1) + c0
        sel = jnp.zeros((tm, COMBINE_CHUNK), F32)
        for k in range(TOP_K):
            sel = sel + jnp.where(staged[:, k:k + 1] == col, gates[:, k:k + 1], 0.0)
        sel = sel.astype(BF16)
        lo, hi = _unpack_halves(buf[slot, pl.ds(c0, COMBINE_CHUNK), :])
        o_ref[:, :half] += jnp.dot(sel, lo, preferred_element_type=F32)
        o_ref[:, half:] += jnp.dot(sel, hi, preferred_element_type=F32)
        return 0

    lax.fori_loop(0, (total + COMBINE_CHUNK - 1) // COMBINE_CHUNK, chunk, 0)


def _combine(rstart, rcnt, eid_t, pos_t, gates_t, xn, y_rows, tm):
    t, d = xn.shape
    n = t // tm
    cap = -(-(TOP_K * tm + 2 * PIECE * N_EXPERTS) // COMBINE_CHUNK) * COMBINE_CHUNK
    tok = lambda i, *_: (i, 0)
    grid_spec = pltpu.PrefetchScalarGridSpec(
        num_scalar_prefetch=2,
        grid=(n,),
        in_specs=[
            pl.BlockSpec((tm, TOP_K), tok),
            pl.BlockSpec((tm, TOP_K), tok),
            pl.BlockSpec((tm, TOP_K), tok),
            pl.BlockSpec((tm, d), tok),
            pl.BlockSpec(memory_space=pl.ANY),
        ],
        out_specs=pl.BlockSpec((tm, d), tok),
        scratch_shapes=[pltpu.VMEM((2, cap, d // 2), jnp.uint32), pltpu.SemaphoreType.DMA((2,))],
    )
    return pl.pallas_call(
        functools.partial(_combine_kernel, tm=tm, cap=cap),
        grid_spec=grid_spec,
        out_shape=jax.ShapeDtypeStruct((t, d), F32),
        compiler_params=_params(1),
        name="combine",
    )(rstart, rcnt, eid_t, pos_t, gates_t, xn, y_rows)


def _tile(n, want):
    t = min(n, want)
    assert n % t == 0, (n, t)
    return t


def kernel(x, meta_tokens, norm1_g, w_in, q_norm_g, k_norm_g, conv_w, conv_b, conv_ln_g, conv_ln_b,
           attn_out_g, w_out, norm2_g, w_router, b_router, w_up, b_up, w_down, b_down):
    assert norm1_g.shape[0] == 1, "single layer: meta-token rows are only keys/values and conv context"
    b, s, d = x.shape
    t = b * s
    cw = conv_w.shape[-1]

    g1 = norm1_g[0][None, :]
    w_in_bf = w_in[0].astype(BF16)
    qg = jnp.tile(q_norm_g[0], SB_HEADS)[None, :]
    kg = jnp.tile(k_norm_g[0], SB_HEADS)[None, :]

    q, k, v, hcv = _inproj(x.reshape(t, d), g1, w_in_bf, qg, kg, _tile(t, 512))
    _, km, vm, hm = _inproj(meta_tokens, g1, w_in_bf, qg, kg, N_META)
    pad = ((0, LANES - N_META), (0, 0))
    sb = _attention(q.reshape(b, s, SB_WIDTH), k.reshape(b, s, SB_WIDTH), v.reshape(b, s, SB_WIDTH),
                    jnp.pad(km, pad), jnp.pad(vm, pad), attn_out_g[0].reshape(1, SB_WIDTH), _tile(s, 256))

    mhalo = jnp.concatenate([jnp.zeros((HALO - N_META, cw), F32), hm], axis=0)
    tm = _tile(s, 512)
    xn, hn, eid, gates, rank, before, counts = _mix(
        hcv.reshape(b, s, cw), mhalo, sb, x, conv_w[0], conv_b[0][None, :], conv_ln_g[0][None, :],
        conv_ln_b[0][None, :], w_out[0].astype(BF16), norm2_g[0][None, :], w_router[0].T,
        b_router[0][:, None], tm, _tile(tm, 64))

    counts = counts[:, 0].astype(jnp.int32)
    padded = (counts + EXPERT_ROWS - 1) // EXPERT_ROWS * EXPERT_ROWS
    pad_end = jnp.cumsum(padded)
    pad_start = pad_end - padded
    n_blocks = t * TOP_K // EXPERT_ROWS + N_EXPERTS
    block_row = jnp.arange(n_blocks, dtype=jnp.int32) * EXPERT_ROWS
    block_expert = jnp.minimum(jnp.sum(block_row[:, None] >= pad_end[None, :], axis=1), N_EXPERTS - 1).astype(jnp.int32)
    n_used = (pad_end[-1:] // EXPERT_ROWS).astype(jnp.int32)
    onehot = eid[:, :, None] == jnp.arange(N_EXPERTS, dtype=jnp.int32)
    pos = rank + jnp.sum(jnp.where(onehot, pad_start, 0), axis=-1)

    x_rows = _dispatch(pad_start + counts, padded - counts, n_used, pos, hn.reshape(t, d), n_blocks,
                       _tile(t, 512))
    y_rows = _experts(block_expert, n_used, x_rows, w_up[0], b_up[0], w_down[0], b_down[0])
    tc = _tile(t, 256)
    run_start = pad_start[:, None] + before[:, ::tc]
    run_end = jnp.concatenate([run_start[:, 1:], (pad_start + counts)[:, None]], axis=1)
    out = _combine(run_start.T.reshape(-1), (run_end - run_start).T.reshape(-1), eid.T, pos.T, gates.T,
                   xn.reshape(t, d), y_rows, tc)
    return out.reshape(b, s, d)
```

```python
import functools

import jax
import jax.numpy as jnp
from jax import lax
from jax.experimental import pallas as pl
from jax.experimental.pallas import tpu as pltpu

N_META = 16
SB_HEADS = 8
SB_HEAD_DIM = 64
SB_WIDTH = SB_HEADS * SB_HEAD_DIM
CONV_KERNEL = 31
N_EXPERTS = 32
TOP_K = 4
SWIGLU_LIMIT = 7.0
SWIGLU_ALPHA = 1.702
EPS = 1e-6
F32_EXP_UNDERFLOW = -104.0

LANES = 128
SUBLANES = 8
PIECE = 16
COMBINE_CHUNK = 512
HALO = 32
EXPERT_ROWS = 512
VMEM_LIMIT = 56 * 1024 * 1024

F32 = jnp.float32
BF16 = jnp.bfloat16


def _params(n_axes, vmem=VMEM_LIMIT):
    return pltpu.CompilerParams(dimension_semantics=("arbitrary",) * n_axes, vmem_limit_bytes=vmem)


def _inproj_kernel(x_ref, g1_ref, w_ref, qg_ref, kg_ref, q_ref, k_ref, v_ref, h_ref):
    x = x_ref[...]
    ms = jnp.mean(x * x, axis=-1, keepdims=True)
    n = (x * lax.rsqrt(ms + EPS) * g1_ref[...]).astype(BF16)
    lo = lax.broadcasted_iota(jnp.int32, (1, LANES), 1) < SB_HEAD_DIM

    def proj(c0, c1):
        return jnp.dot(n, w_ref[:, c0:c1], preferred_element_type=F32)

    def head_norm(acc, g_ref, out_ref, scale):
        for c in range(SB_WIDTH // LANES):
            sl = slice(c * LANES, (c + 1) * LANES)
            a = acc[:, sl]
            sq = a * a
            s_lo = jnp.sum(jnp.where(lo, sq, 0.0), axis=-1, keepdims=True)
            s_hi = jnp.sum(jnp.where(lo, 0.0, sq), axis=-1, keepdims=True)
            r = lax.rsqrt(jnp.where(lo, s_lo, s_hi) * (1.0 / SB_HEAD_DIM) + EPS)
            out_ref[:, sl] = (a * r * (g_ref[:, sl] * scale)).astype(BF16)

    head_norm(proj(0, SB_WIDTH), qg_ref, q_ref, SB_HEAD_DIM ** -0.5)
    head_norm(proj(SB_WIDTH, 2 * SB_WIDTH), kg_ref, k_ref, 1.0)
    v_ref[...] = proj(2 * SB_WIDTH, 3 * SB_WIDTH).astype(BF16)
    cw = (w_ref.shape[1] - 3 * SB_WIDTH) // 2
    val = proj(3 * SB_WIDTH, 3 * SB_WIDTH + cw)
    gate = proj(3 * SB_WIDTH + cw, 3 * SB_WIDTH + 2 * cw)
    h_ref[...] = val * jax.nn.sigmoid(gate)


def _inproj(x2, g1, w_in_bf, qg, kg, tm):
    t, d = x2.shape
    cols = w_in_bf.shape[1]
    cw = (cols - 3 * SB_WIDTH) // 2
    row = lambda i: (i, 0)
    fixed = lambda i: (0, 0)
    return pl.pallas_call(
        _inproj_kernel,
        grid=(t // tm,),
        in_specs=[
            pl.BlockSpec((tm, d), row),
            pl.BlockSpec((1, d), fixed),
            pl.BlockSpec((d, cols), fixed),
            pl.BlockSpec((1, SB_WIDTH), fixed),
            pl.BlockSpec((1, SB_WIDTH), fixed),
        ],
        out_specs=[
            pl.BlockSpec((tm, SB_WIDTH), row),
            pl.BlockSpec((tm, SB_WIDTH), row),
            pl.BlockSpec((tm, SB_WIDTH), row),
            pl.BlockSpec((tm, cw), row),
        ],
        out_shape=[
            jax.ShapeDtypeStruct((t, SB_WIDTH), BF16),
            jax.ShapeDtypeStruct((t, SB_WIDTH), BF16),
            jax.ShapeDtypeStruct((t, SB_WIDTH), BF16),
            jax.ShapeDtypeStruct((t, cw), F32),
        ],
        compiler_params=_params(1),
        name="inproj",
    )(x2, g1, w_in_bf, qg, kg)


def _attn_kernel(q_ref, k_ref, v_ref, km_ref, vm_ref, g_ref, o_ref, *, tq):
    i = pl.program_id(2)
    lane = lax.broadcasted_iota(jnp.int32, (1, LANES), 1)
    lo = lane < SB_HEAD_DIM
    q2 = q_ref[0]
    zero_bf = jnp.zeros((), BF16)
    qs = jnp.concatenate([jnp.where(lo, q2, zero_bf), jnp.where(lo, zero_bf, q2)], axis=0)
    row = lax.broadcasted_iota(jnp.int32, (2 * tq, tq), 0)
    col = lax.broadcasted_iota(jnp.int32, (2 * tq, tq), 1)
    causal = col < jnp.where(row >= tq, row - tq, row)
    srow = lax.broadcasted_iota(jnp.int32, (tq, tq), 0)
    scol = lax.broadcasted_iota(jnp.int32, (tq, tq), 1)
    neg_suffix = jnp.where(srow > scol, -1.0, 0.0).astype(BF16)

    def block(kb, vb, acc, r, mask, neg_suffix_m):
        z = lax.dot_general(qs, kb, (((1,), (1,)), ((), ())), preferred_element_type=F32)
        m = jnp.minimum(z, 0.0)
        p = jnp.maximum(z, 0.0)
        l1p = jnp.log(1.0 + jnp.exp(m - p))
        log_beta = m - l1p
        neg_keep = p + l1p
        if mask is not None:
            neg_keep = jnp.where(mask, neg_keep, 0.0)
        later = jnp.dot(neg_keep.astype(BF16), neg_suffix_m, preferred_element_type=F32)
        a = jnp.exp(log_beta + later + r)
        if mask is not None:
            a = jnp.where(mask, a, 0.0)
        res = jnp.dot(a.astype(BF16), vb, preferred_element_type=F32)
        acc = acc + jnp.where(lo, res[:tq], res[tq:])
        return acc, r - jnp.sum(neg_keep, axis=-1, keepdims=True)

    def kv_block(j):
        start = pl.multiple_of(j * tq, tq)
        return k_ref[0, pl.ds(start, tq), :], v_ref[0, pl.ds(start, tq), :]

    def live(r):
        return jnp.max(r) > F32_EXP_UNDERFLOW

    acc0 = jnp.zeros((tq, LANES), F32)
    r0 = jnp.zeros((2 * tq, 1), F32)

    def diagonal(_):
        return block(*kv_block(i), acc0, r0, causal, neg_suffix)

    def diagonal_and_previous(_):
        acc, r = block(*kv_block(i), acc0, r0, causal, neg_suffix)
        return block(*kv_block(i - 1), acc, r, None, neg_suffix)

    acc, r = lax.cond(i > 0, diagonal_and_previous, diagonal, 0)

    def cond(c):
        return jnp.logical_and(c[0] < i, c[3])

    def body(c):
        acc, r = block(*kv_block(i - 1 - c[0]), c[1], c[2], None, neg_suffix)
        return c[0] + 1, acc, r, live(r)

    _, acc, r, alive = lax.while_loop(cond, body, (jnp.int32(1), acc, r, live(r)))

    def meta_block(acc):
        return block(km_ref[...], vm_ref[...], acc, r, lane < N_META, neg_suffix[:LANES, :LANES])[0]

    acc = lax.cond(alive, meta_block, lambda acc: acc, acc)

    sq = acc * acc
    s_lo = jnp.sum(jnp.where(lo, sq, 0.0), axis=-1, keepdims=True)
    s_hi = jnp.sum(jnp.where(lo, 0.0, sq), axis=-1, keepdims=True)
    rn = lax.rsqrt(jnp.where(lo, s_lo, s_hi) * (1.0 / SB_HEAD_DIM) + EPS)
    o_ref[0] = (acc * rn * g_ref[...]).astype(BF16)


def _attention(q, k, v, km, vm, og, tq):
    b, s, _ = q.shape
    n_pairs = SB_WIDTH // LANES
    return pl.pallas_call(
        functools.partial(_attn_kernel, tq=tq),
        grid=(b, n_pairs, s // tq),
        in_specs=[
            pl.BlockSpec((1, tq, LANES), lambda b, p, i: (b, i, p)),
            pl.BlockSpec((1, s, LANES), lambda b, p, i: (b, 0, p)),
            pl.BlockSpec((1, s, LANES), lambda b, p, i: (b, 0, p)),
            pl.BlockSpec((LANES, LANES), lambda b, p, i: (0, p)),
            pl.BlockSpec((LANES, LANES), lambda b, p, i: (0, p)),
            pl.BlockSpec((1, LANES), lambda b, p, i: (0, p)),
        ],
        out_specs=pl.BlockSpec((1, tq, LANES), lambda b, p, i: (b, i, p)),
        out_shape=jax.ShapeDtypeStruct((b, s, SB_WIDTH), BF16),
        compiler_params=_params(3),
        name="attention",
    )(q, k, v, km, vm, og)


def _mix_kernel(h_ref, halo_ref, mhalo_ref, sb_ref, x_ref, cw_ref, cb_ref, lg_ref, lb_ref, wo_ref,
                g2_ref, wr_ref, br_ref,
                xn_ref, hn_ref, eid_ref, gate_ref, rank_ref, before_ref, cnt_ref,
                win_ref, cwin_ref, cv_ref, carry_ref, *, tm, chunk):
    b = pl.program_id(0)
    i = pl.program_id(1)

    @pl.when(jnp.logical_and(b == 0, i == 0))
    def _():
        carry_ref[...] = jnp.zeros_like(carry_ref)

    @pl.when(i == 0)
    def _():
        win_ref[0:HALO, :] = mhalo_ref[...]

    @pl.when(i > 0)
    def _():
        win_ref[0:HALO, :] = halo_ref[0]

    win_ref[HALO:, :] = h_ref[0]

    first_tap = HALO - (CONV_KERNEL - 1)

    def conv_chunk(c, _):
        r0 = pl.multiple_of(c * chunk, chunk)
        cwin_ref[...] = win_ref[pl.ds(r0, chunk + HALO), :]
        acc = jnp.broadcast_to(cb_ref[...], (chunk, cb_ref.shape[1]))
        for j in range(CONV_KERNEL):
            acc = acc + cw_ref[j:j + 1, :] * cwin_ref[first_tap + j:first_tap + j + chunk, :]
        mu = jnp.mean(acc, axis=-1, keepdims=True)
        cen = acc - mu
        var = jnp.mean(cen * cen, axis=-1, keepdims=True)
        y = cen * lax.rsqrt(var + EPS) * lg_ref[...] + lb_ref[...]
        cv_ref[pl.ds(r0, chunk), :] = (y * jax.nn.sigmoid(y)).astype(BF16)
        return 0

    lax.fori_loop(0, tm // chunk, conv_chunk, 0)

    mixed = jnp.dot(sb_ref[0], wo_ref[0:SB_WIDTH, :], preferred_element_type=F32)
    mixed = mixed + jnp.dot(cv_ref[...], wo_ref[SB_WIDTH:, :], preferred_element_type=F32)
    xn = x_ref[0] + mixed
    xn_ref[0] = xn
    ms = jnp.mean(xn * xn, axis=-1, keepdims=True)
    hn = xn * lax.rsqrt(ms + EPS) * g2_ref[...]
    hn_ref[0] = hn

    logits = lax.dot_general(wr_ref[...], hn, (((1,), (1,)), ((), ())),
                             precision=lax.Precision.HIGHEST, preferred_element_type=F32) + br_ref[...]
    eidx = lax.broadcasted_iota(jnp.int32, logits.shape, 0)
    vals, idxs = [], []
    for _ in range(TOP_K):
        m = jnp.max(logits, axis=0, keepdims=True)
        sel = jnp.min(jnp.where(logits == m, eidx, N_EXPERTS), axis=0, keepdims=True)
        vals.append(m)
        idxs.append(sel)
        logits = jnp.where(eidx == sel, -jnp.inf, logits)
    exps = [jnp.exp(v - vals[0]) for v in vals]
    denom = exps[0] + exps[1] + exps[2] + exps[3]
    gate_ref[...] = jnp.concatenate([e / denom for e in exps], axis=0)
    eid_ref[...] = jnp.concatenate(idxs, axis=0)

    onehots = [(eidx == s).astype(F32) for s in idxs]
    chosen = onehots[0] + onehots[1] + onehots[2] + onehots[3]
    tr = lax.broadcasted_iota(jnp.int32, (tm, tm), 0)
    tc = lax.broadcasted_iota(jnp.int32, (tm, tm), 1)
    before = (tr < tc).astype(BF16)
    prefix = jnp.dot(chosen.astype(BF16), before, preferred_element_type=F32) + carry_ref[...]
    rank_ref[...] = jnp.concatenate(
        [jnp.sum(o * prefix, axis=0, keepdims=True) for o in onehots], axis=0).astype(jnp.int32)
    before_ref[...] = prefix.astype(jnp.int32)
    carry_ref[...] = carry_ref[...] + jnp.sum(chosen, axis=1, keepdims=True)
    cnt_ref[...] = jnp.broadcast_to(carry_ref[...], cnt_ref.shape)


def _mix(hcv, mhalo, sb, x, conv_w, conv_b, ln_g, ln_b, w_out_bf, g2, wr_t, br, tm, chunk):
    b, s, d = x.shape
    cw = hcv.shape[-1]
    t = b * s
    per = s // tm
    tile = lambda b, i: (b, i, 0)
    fixed = lambda b, i: (0, 0)
    tok = lambda b, i: (0, b * per + i)
    return pl.pallas_call(
        functools.partial(_mix_kernel, tm=tm, chunk=chunk),
        grid=(b, per),
        in_specs=[
            pl.BlockSpec((1, tm, cw), tile),
            pl.BlockSpec((1, HALO, cw), lambda b, i: (b, jnp.maximum(i * (tm // HALO) - 1, 0), 0)),
            pl.BlockSpec((HALO, cw), fixed),
            pl.BlockSpec((1, tm, SB_WIDTH), tile),
            pl.BlockSpec((1, tm, d), tile),
            pl.BlockSpec((CONV_KERNEL, cw), fixed),
            pl.BlockSpec((1, cw), fixed),
            pl.BlockSpec((1, cw), fixed),
            pl.BlockSpec((1, cw), fixed),
            pl.BlockSpec((SB_WIDTH + cw, d), fixed),
            pl.BlockSpec((1, d), fixed),
            pl.BlockSpec((N_EXPERTS, d), fixed),
            pl.BlockSpec((N_EXPERTS, 1), fixed),
        ],
        out_specs=[
            pl.BlockSpec((1, tm, d), tile),
            pl.BlockSpec((1, tm, d), tile),
            pl.BlockSpec((TOP_K, tm), tok),
            pl.BlockSpec((TOP_K, tm), tok),
            pl.BlockSpec((TOP_K, tm), tok),
            pl.BlockSpec((N_EXPERTS, tm), tok),
            pl.BlockSpec((N_EXPERTS, LANES), fixed),
        ],
        out_shape=[
            jax.ShapeDtypeStruct((b, s, d), F32),
            jax.ShapeDtypeStruct((b, s, d), F32),
            jax.ShapeDtypeStruct((TOP_K, t), jnp.int32),
            jax.ShapeDtypeStruct((TOP_K, t), F32),
            jax.ShapeDtypeStruct((TOP_K, t), jnp.int32),
            jax.ShapeDtypeStruct((N_EXPERTS, t), jnp.int32),
            jax.ShapeDtypeStruct((N_EXPERTS, LANES), F32),
        ],
        scratch_shapes=[
            pltpu.VMEM((tm + HALO, cw), F32),
            pltpu.VMEM((chunk + HALO, cw), F32),
            pltpu.VMEM((tm, cw), BF16),
            pltpu.VMEM((N_EXPERTS, 1), F32),
        ],
        compiler_params=_params(2),
        name="mix_router",
    )(hcv, hcv, mhalo, sb, x, conv_w, conv_b, ln_g, ln_b, w_out_bf, g2, wr_t, br)


def _wait_rows(rows_ref, n, sem):
    pltpu.make_async_copy(rows_ref.at[pl.ds(0, n)], rows_ref.at[pl.ds(0, n)], sem).wait()


def _dispatch_kernel(padlo_ref, padlen_ref, nused_ref, pos_ref, hn_ref, rows_ref, zeros_ref, sem, zsem,
                     *, tm, n_blocks, n_pad):
    i = pl.program_id(0)

    @pl.when(i == 0)
    def _():
        zeros_ref[...] = jnp.zeros_like(zeros_ref)
        for e in range(N_EXPERTS):
            lo, ln = padlo_ref[e], padlen_ref[e]
            head = (-lo) & (SUBLANES - 1)
            for r in range(SUBLANES - 1):
                @pl.when(r < head)
                def _():
                    pltpu.make_async_copy(zeros_ref.at[pl.ds(0, 1)], rows_ref.at[pl.ds(lo + r, 1)], zsem).start()

            lo8, ln8 = lo + head, ln - head
            for bit in range(SUBLANES.bit_length() - 1, EXPERT_ROWS.bit_length() - 1):
                size = 1 << bit

                @pl.when((ln8 >> bit) & 1 == 1)
                def _():
                    start = pl.multiple_of(lo8 + ((ln8 >> (bit + 1)) << (bit + 1)), SUBLANES)
                    pltpu.make_async_copy(zeros_ref.at[pl.ds(0, size)],
                                          rows_ref.at[pl.ds(start, size)], zsem).start()

        def tail(j, _):
            start = pl.multiple_of(j * EXPERT_ROWS, EXPERT_ROWS)
            pltpu.make_async_copy(zeros_ref, rows_ref.at[pl.ds(start, EXPERT_ROWS)], zsem).start()
            return 0

        lax.fori_loop(nused_ref[0], n_blocks, tail, 0)

    def issue(t, _):
        for k in range(TOP_K):
            pltpu.make_async_copy(hn_ref.at[pl.ds(t, 1)],
                                  rows_ref.at[pl.ds(pos_ref[k, t], 1)], sem).start()
        return 0

    lax.fori_loop(0, tm, issue, 0)
    _wait_rows(rows_ref, TOP_K * tm, sem)

    @pl.when(i == pl.num_programs(0) - 1)
    def _():
        _wait_rows(rows_ref, n_pad, zsem)


def _dispatch(pad_lo, pad_len, n_used, pos, hn, n_blocks, tm):
    t, d = hn.shape
    n_rows = n_blocks * EXPERT_ROWS
    grid_spec = pltpu.PrefetchScalarGridSpec(
        num_scalar_prefetch=3,
        grid=(t // tm,),
        in_specs=[
            pl.BlockSpec((TOP_K, tm), lambda i, *_: (0, i), memory_space=pltpu.SMEM),
            pl.BlockSpec((tm, d), lambda i, *_: (i, 0)),
        ],
        out_specs=pl.BlockSpec(memory_space=pl.ANY),
        scratch_shapes=[pltpu.VMEM((EXPERT_ROWS, d), hn.dtype), pltpu.SemaphoreType.DMA(()),
                        pltpu.SemaphoreType.DMA(())],
    )
    return pl.pallas_call(
        functools.partial(_dispatch_kernel, tm=tm, n_blocks=n_blocks, n_pad=n_rows - t * TOP_K),
        grid_spec=grid_spec,
        out_shape=jax.ShapeDtypeStruct((n_rows, d), hn.dtype),
        compiler_params=_params(1),
        name="dispatch",
    )(pad_lo, pad_len, n_used, pos, hn)


def _expert_kernel(be_ref, nused_ref, x_ref, wu_ref, bu_ref, wd_ref, bd_ref, y_ref,
                   wu_bf, wd_bf, *, cast_rows):
    j = pl.program_id(0)
    de = wd_ref.shape[1]

    @pl.when(j >= nused_ref[0])
    def _():
        y_ref[...] = jnp.zeros_like(y_ref)

    @pl.when(j < nused_ref[0])
    def _():
        prev = be_ref[jnp.maximum(j - 1, 0)]

        @pl.when(jnp.logical_or(j == 0, be_ref[j] != prev))
        def _():
            def cast(c, _):
                r0 = pl.multiple_of(c * cast_rows, cast_rows)
                wu_bf[pl.ds(r0, cast_rows), :] = wu_ref[0, pl.ds(r0, cast_rows), :].astype(BF16)
                wd_bf[pl.ds(r0, cast_rows), :] = wd_ref[0, pl.ds(r0, cast_rows), :].astype(BF16)
                return 0

            lax.fori_loop(0, wu_ref.shape[1] // cast_rows, cast, 0)

        up = jnp.dot(x_ref[...].astype(BF16), wu_bf[...], preferred_element_type=F32) + bu_ref[0]
        glu = jnp.minimum(up[:, :de], SWIGLU_LIMIT)
        lin = jnp.clip(up[:, de:], -SWIGLU_LIMIT, SWIGLU_LIMIT)
        act = glu * jax.nn.sigmoid(SWIGLU_ALPHA * glu) * (lin + 1.0)
        y = jnp.dot(act.astype(BF16), wd_bf[...], preferred_element_type=F32) + bd_ref[0]
        y_ref[...] = _pack_halves(y)


def _pack_halves(y):
    half = y.shape[1] // 2
    bits = lambda v: lax.bitcast_convert_type(v.astype(BF16).astype(F32), jnp.uint32)
    return (bits(y[:, half:]) & jnp.uint32(0xFFFF0000)) | (bits(y[:, :half]) >> 16)


def _unpack_halves(w):
    lo = lax.bitcast_convert_type(w << 16, F32).astype(BF16)
    hi = lax.bitcast_convert_type(w & jnp.uint32(0xFFFF0000), F32).astype(BF16)
    return lo, hi


def _experts(block_expert, n_used, x_rows, w_up, b_up, w_down, b_down):
    n_rows, d = x_rows.shape
    n_blocks = n_rows // EXPERT_ROWS + 1
    ne, _, up_cols = w_up.shape
    de = w_down.shape[1]
    blk = lambda j, be, nu: (jnp.minimum(j, nu[0] - 1), 0)
    wsel = lambda j, be, nu: (be[jnp.minimum(j, nu[0] - 1)], 0, 0)
    grid_spec = pltpu.PrefetchScalarGridSpec(
        num_scalar_prefetch=2,
        grid=(n_blocks,),
        in_specs=[
            pl.BlockSpec((EXPERT_ROWS, d), blk),
            pl.BlockSpec((1, d, up_cols), wsel),
            pl.BlockSpec((1, 1, up_cols), wsel),
            pl.BlockSpec((1, de, d), wsel),
            pl.BlockSpec((1, 1, d), wsel),
        ],
        out_specs=pl.BlockSpec((EXPERT_ROWS, d // 2), lambda j, be, nu: (j, 0)),
        scratch_shapes=[pltpu.VMEM((d, up_cols), BF16), pltpu.VMEM((de, d), BF16)],
    )
    return pl.pallas_call(
        functools.partial(_expert_kernel, cast_rows=64),
        grid_spec=grid_spec,
        out_shape=jax.ShapeDtypeStruct((n_blocks * EXPERT_ROWS, d // 2), jnp.uint32),
        compiler_params=_params(1),
        name="experts",
    )(block_expert, n_used, x_rows, w_up, b_up.reshape(ne, 1, up_cols),
      w_down, b_down.reshape(ne, 1, d))


def _combine_kernel(rstart_ref, rcnt_ref, eid_ref, pos_ref, gate_ref, xn_ref, rows_ref, o_ref, buf, sems,
                    *, tm, cap):
    i = pl.program_id(0)
    n = pl.num_programs(0)
    slot = lax.rem(i, 2)
    half = o_ref.shape[1] // 2

    def windows(tile):
        out, off = [], jnp.int32(0)
        for e in range(N_EXPERTS):
            start = rstart_ref[tile * N_EXPERTS + e]
            cnt = rcnt_ref[tile * N_EXPERTS + e]
            first = (start >> 3) << 3
            pieces = jnp.where(cnt > 0, (start - first + cnt + PIECE - 1) // PIECE, 0)
            out.append((first, pieces, off))
            off = off + pieces * PIECE
        return out, off

    def fetch(tile, s):
        for first, pieces, off in windows(tile)[0]:
            def body(p, _):
                src = rows_ref.at[pl.ds(pl.multiple_of(first + p * PIECE, SUBLANES), PIECE)]
                dst = buf.at[s, pl.ds(pl.multiple_of(off + p * PIECE, PIECE), PIECE)]
                pltpu.make_async_copy(src, dst, sems.at[s]).start()
                return 0

            lax.fori_loop(0, pieces, body, 0)

    @pl.when(i == 0)
    def _():
        buf[...] = jnp.zeros_like(buf)
        fetch(0, 0)

    @pl.when(i + 1 < n)
    def _():
        fetch(i + 1, 1 - slot)

    wins, total = windows(i)
    eid = eid_ref[...]
    staged = pos_ref[...]
    shift = jnp.zeros_like(staged)
    for e, (first, _, off) in enumerate(wins):
        shift = jnp.where(eid == e, off - first, shift)
    staged = staged + shift
    gates = gate_ref[...]

    @pl.when(total > 0)
    def _():
        rows = pl.ds(0, pl.multiple_of(total, PIECE))
        pltpu.make_async_copy(rows_ref.at[rows], buf.at[slot, rows], sems.at[slot]).wait()

    o_ref[...] = xn_ref[...]

    def chunk(c, _):
        c0 = pl.multiple_of(c * COMBINE_CHUNK, COMBINE_CHUNK)
        col = lax.broadcasted_iota(jnp.int32, (tm, COMBINE_CHUNK), 1) + c0
        sel = jnp.zeros((tm, COMBINE_CHUNK), F32)
        for k in range(TOP_K):
            sel = sel + jnp.where(staged[:, k:k + 1] == col, gates[:, k:k + 1], 0.0)
        sel = sel.astype(BF16)
        lo, hi = _unpack_halves(buf[slot, pl.ds(c0, COMBINE_CHUNK), :])
        o_ref[:, :half] += jnp.dot(sel, lo, preferred_element_type=F32)
        o_ref[:, half:] += jnp.dot(sel, hi, preferred_element_type=F32)
        return 0

    lax.fori_loop(0, (total + COMBINE_CHUNK - 1) // COMBINE_CHUNK, chunk, 0)


def _combine(rstart, rcnt, eid_t, pos_t, gates_t, xn, y_rows, tm):
    t, d = xn.shape
    n = t // tm
    cap = -(-(TOP_K * tm + 2 * PIECE * N_EXPERTS) // COMBINE_CHUNK) * COMBINE_CHUNK
    tok = lambda i, *_: (i, 0)
    grid_spec = pltpu.PrefetchScalarGridSpec(
        num_scalar_prefetch=2,
        grid=(n,),
        in_specs=[
            pl.BlockSpec((tm, TOP_K), tok),
            pl.BlockSpec((tm, TOP_K), tok),
            pl.BlockSpec((tm, TOP_K), tok),
            pl.BlockSpec((tm, d), tok),
            pl.BlockSpec(memory_space=pl.ANY),
        ],
        out_specs=pl.BlockSpec((tm, d), tok),
        scratch_shapes=[pltpu.VMEM((2, cap, d // 2), jnp.uint32), pltpu.SemaphoreType.DMA((2,))],
    )
    return pl.pallas_call(
        functools.partial(_combine_kernel, tm=tm, cap=cap),
        grid_spec=grid_spec,
        out_shape=jax.ShapeDtypeStruct((t, d), F32),
        compiler_params=_params(1),
        name="combine",
    )(rstart, rcnt, eid_t, pos_t, gates_t, xn, y_rows)


def _tile(n, want):
    t = min(n, want)
    assert n % t == 0, (n, t)
    return t


def kernel(x, meta_tokens, norm1_g, w_in, q_norm_g, k_norm_g, conv_w, conv_b, conv_ln_g, conv_ln_b,
           attn_out_g, w_out, norm2_g, w_router, b_router, w_up, b_up, w_down, b_down):
    assert norm1_g.shape[0] == 1, "single layer: meta-token rows are only keys/values and conv context"
    b, s, d = x.shape
    t = b * s
    cw = conv_w.shape[-1]

    g1 = norm1_g[0][None, :]
    w_in_bf = w_in[0].astype(BF16)
    qg = jnp.tile(q_norm_g[0], SB_HEADS)[None, :]
    kg = jnp.tile(k_norm_g[0], SB_HEADS)[None, :]

    q, k, v, hcv = _inproj(x.reshape(t, d), g1, w_in_bf, qg, kg, _tile(t, 512))
    _, km, vm, hm = _inproj(meta_tokens, g1, w_in_bf, qg, kg, N_META)
    pad = ((0, LANES - N_META), (0, 0))
    sb = _attention(q.reshape(b, s, SB_WIDTH), k.reshape(b, s, SB_WIDTH), v.reshape(b, s, SB_WIDTH),
                    jnp.pad(km, pad), jnp.pad(vm, pad), attn_out_g[0].reshape(1, SB_WIDTH), _tile(s, 256))

    mhalo = jnp.concatenate([jnp.zeros((HALO - N_META, cw), F32), hm], axis=0)
    tm = _tile(s, 512)
    xn, hn, eid, gates, rank, before, counts = _mix(
        hcv.reshape(b, s, cw), mhalo, sb, x, conv_w[0], conv_b[0][None, :], conv_ln_g[0][None, :],
        conv_ln_b[0][None, :], w_out[0].astype(BF16), norm2_g[0][None, :], w_router[0].T,
        b_router[0][:, None], tm, _tile(tm, 64))

    counts = counts[:, 0].astype(jnp.int32)
    padded = (counts + EXPERT_ROWS - 1) // EXPERT_ROWS * EXPERT_ROWS
    pad_end = jnp.cumsum(padded)
    pad_start = pad_end - padded
    n_blocks = t * TOP_K // EXPERT_ROWS + N_EXPERTS
    block_row = jnp.arange(n_blocks, dtype=jnp.int32) * EXPERT_ROWS
    block_expert = jnp.minimum(jnp.sum(block_row[:, None] >= pad_end[None, :], axis=1), N_EXPERTS - 1).astype(jnp.int32)
    n_used = (pad_end[-1:] // EXPERT_ROWS).astype(jnp.int32)
    onehot = eid[:, :, None] == jnp.arange(N_EXPERTS, dtype=jnp.int32)
    pos = rank + jnp.sum(jnp.where(onehot, pad_start, 0), axis=-1)

    x_rows = _dispatch(pad_start + counts, padded - counts, n_used, pos, hn.reshape(t, d), n_blocks,
                       _tile(t, 512))
    y_rows = _experts(block_expert, n_used, x_rows, w_up[0], b_up[0], w_down[0], b_down[0])
    tc = _tile(t, 256)
    run_start = pad_start[:, None] + before[:, ::tc]
    run_end = jnp.concatenate([run_start[:, 1:], (pad_start + counts)[:, None]], axis=1)
    out = _combine(run_start.T.reshape(-1), (run_end - run_start).T.reshape(-1), eid.T, pos.T, gates.T,
                   xn.reshape(t, d), y_rows, tc)
    return out.reshape(b, s, d)
```

```python
import functools

import jax
import jax.numpy as jnp
from jax import lax
from jax.experimental import pallas as pl
from jax.experimental.pallas import tpu as pltpu

N_META = 16
SB_HEADS = 8
SB_HEAD_DIM = 64
SB_WIDTH = SB_HEADS * SB_HEAD_DIM
CONV_KERNEL = 31
N_EXPERTS = 32
TOP_K = 4
SWIGLU_LIMIT = 7.0
SWIGLU_ALPHA = 1.702
EPS = 1e-6
F32_EXP_UNDERFLOW = -104.0

LANES = 128
SUBLANES = 8
PIECE = 16
COMBINE_CHUNK = 512
HALO = 32
EXPERT_ROWS = 512
VMEM_LIMIT = 56 * 1024 * 1024

F32 = jnp.float32
BF16 = jnp.bfloat16


def _params(n_axes, vmem=VMEM_LIMIT):
    return pltpu.CompilerParams(dimension_semantics=("arbitrary",) * n_axes, vmem_limit_bytes=vmem)


def _inproj_kernel(x_ref, g1_ref, w_ref, qg_ref, kg_ref, q_ref, k_ref, v_ref, h_ref):
    x = x_ref[...]
    ms = jnp.mean(x * x, axis=-1, keepdims=True)
    n = (x * lax.rsqrt(ms + EPS) * g1_ref[...]).astype(BF16)
    lo = lax.broadcasted_iota(jnp.int32, (1, LANES), 1) < SB_HEAD_DIM

    def proj(c0, c1):
        return jnp.dot(n, w_ref[:, c0:c1], preferred_element_type=F32)

    def head_norm(acc, g_ref, out_ref, scale):
        for c in range(SB_WIDTH // LANES):
            sl = slice(c * LANES, (c + 1) * LANES)
            a = acc[:, sl]
            sq = a * a
            s_lo = jnp.sum(jnp.where(lo, sq, 0.0), axis=-1, keepdims=True)
            s_hi = jnp.sum(jnp.where(lo, 0.0, sq), axis=-1, keepdims=True)
            r = lax.rsqrt(jnp.where(lo, s_lo, s_hi) * (1.0 / SB_HEAD_DIM) + EPS)
            out_ref[:, sl] = (a * r * (g_ref[:, sl] * scale)).astype(BF16)

    head_norm(proj(0, SB_WIDTH), qg_ref, q_ref, SB_HEAD_DIM ** -0.5)
    head_norm(proj(SB_WIDTH, 2 * SB_WIDTH), kg_ref, k_ref, 1.0)
    v_ref[...] = proj(2 * SB_WIDTH, 3 * SB_WIDTH).astype(BF16)
    cw = (w_ref.shape[1] - 3 * SB_WIDTH) // 2
    val = proj(3 * SB_WIDTH, 3 * SB_WIDTH + cw)
    gate = proj(3 * SB_WIDTH + cw, 3 * SB_WIDTH + 2 * cw)
    h_ref[...] = val * jax.nn.sigmoid(gate)


def _inproj(x2, g1, w_in_bf, qg, kg, tm):
    t, d = x2.shape
    cols = w_in_bf.shape[1]
    cw = (cols - 3 * SB_WIDTH) // 2
    row = lambda i: (i, 0)
    fixed = lambda i: (0, 0)
    return pl.pallas_call(
        _inproj_kernel,
        grid=(t // tm,),
        in_specs=[
            pl.BlockSpec((tm, d), row),
            pl.BlockSpec((1, d), fixed),
            pl.BlockSpec((d, cols), fixed),
            pl.BlockSpec((1, SB_WIDTH), fixed),
            pl.BlockSpec((1, SB_WIDTH), fixed),
        ],
        out_specs=[
            pl.BlockSpec((tm, SB_WIDTH), row),
            pl.BlockSpec((tm, SB_WIDTH), row),
            pl.BlockSpec((tm, SB_WIDTH), row),
            pl.BlockSpec((tm, cw), row),
        ],
        out_shape=[
            jax.ShapeDtypeStruct((t, SB_WIDTH), BF16),
            jax.ShapeDtypeStruct((t, SB_WIDTH), BF16),
            jax.ShapeDtypeStruct((t, SB_WIDTH), BF16),
            jax.ShapeDtypeStruct((t, cw), F32),
        ],
        compiler_params=_params(1),
        name="inproj",
    )(x2, g1, w_in_bf, qg, kg)


def _attn_kernel(q_ref, k_ref, v_ref, km_ref, vm_ref, g_ref, o_ref, *, tq):
    i = pl.program_id(2)
    lane = lax.broadcasted_iota(jnp.int32, (1, LANES), 1)
    lo = lane < SB_HEAD_DIM
    q2 = q_ref[0]
    zero_bf = jnp.zeros((), BF16)
    qs = jnp.concatenate([jnp.where(lo, q2, zero_bf), jnp.where(lo, zero_bf, q2)], axis=0)
    row = lax.broadcasted_iota(jnp.int32, (2 * tq, tq), 0)
    col = lax.broadcasted_iota(jnp.int32, (2 * tq, tq), 1)
    causal = col < jnp.where(row >= tq, row - tq, row)
    srow = lax.broadcasted_iota(jnp.int32, (tq, tq), 0)
    scol = lax.broadcasted_iota(jnp.int32, (tq, tq), 1)
    neg_suffix = jnp.where(srow > scol, -1.0, 0.0).astype(BF16)

    def block(kb, vb, acc, r, mask, neg_suffix_m):
        z = lax.dot_general(qs, kb, (((1,), (1,)), ((), ())), preferred_element_type=F32)
        m = jnp.minimum(z, 0.0)
        p = jnp.maximum(z, 0.0)
        l1p = jnp.log(1.0 + jnp.exp(m - p))
        log_beta = m - l1p
        neg_keep = p + l1p
        if mask is not None:
            neg_keep = jnp.where(mask, neg_keep, 0.0)
        later = jnp.dot(neg_keep.astype(BF16), neg_suffix_m, preferred_element_type=F32)
        a = jnp.exp(log_beta + later + r)
        if mask is not None:
            a = jnp.where(mask, a, 0.0)
        res = jnp.dot(a.astype(BF16), vb, preferred_element_type=F32)
        acc = acc + jnp.where(lo, res[:tq], res[tq:])
        return acc, r - jnp.sum(neg_keep, axis=-1, keepdims=True)

    def kv_block(j):
        start = pl.multiple_of(j * tq, tq)
        return k_ref[0, pl.ds(start, tq), :], v_ref[0, pl.ds(start, tq), :]

    def live(r):
        return jnp.max(r) > F32_EXP_UNDERFLOW

    acc0 = jnp.zeros((tq, LANES), F32)
    r0 = jnp.zeros((2 * tq, 1), F32)

    def diagonal(_):
        return block(*kv_block(i), acc0, r0, causal, neg_suffix)

    def diagonal_and_previous(_):
        acc, r = block(*kv_block(i), acc0, r0, causal, neg_suffix)
        return block(*kv_block(i - 1), acc, r, None, neg_suffix)

    acc, r = lax.cond(i > 0, diagonal_and_previous, diagonal, 0)

    def cond(c):
        return jnp.logical_and(c[0] < i, c[3])

    def body(c):
        acc, r = block(*kv_block(i - 1 - c[0]), c[1], c[2], None, neg_suffix)
        return c[0] + 1, acc, r, live(r)

    _, acc, r, alive = lax.while_loop(cond, body, (jnp.int32(1), acc, r, live(r)))

    def meta_block(acc):
        return block(km_ref[...], vm_ref[...], acc, r, lane < N_META, neg_suffix[:LANES, :LANES])[0]

    acc = lax.cond(alive, meta_block, lambda acc: acc, acc)

    sq = acc * acc
    s_lo = jnp.sum(jnp.where(lo, sq, 0.0), axis=-1, keepdims=True)
    s_hi = jnp.sum(jnp.where(lo, 0.0, sq), axis=-1, keepdims=True)
    rn = lax.rsqrt(jnp.where(lo, s_lo, s_hi) * (1.0 / SB_HEAD_DIM) + EPS)
    o_ref[0] = (acc * rn * g_ref[...]).astype(BF16)


def _attention(q, k, v, km, vm, og, tq):
    b, s, _ = q.shape
    n_pairs = SB_WIDTH // LANES
    return pl.pallas_call(
        functools.partial(_attn_kernel, tq=tq),
        grid=(b, n_pairs, s // tq),
        in_specs=[
            pl.BlockSpec((1, tq, LANES), lambda b, p, i: (b, i, p)),
            pl.BlockSpec((1, s, LANES), lambda b, p, i: (b, 0, p)),
            pl.BlockSpec((1, s, LANES), lambda b, p, i: (b, 0, p)),
            pl.BlockSpec((LANES, LANES), lambda b, p, i: (0, p)),
            pl.BlockSpec((LANES, LANES), lambda b, p, i: (0, p)),
            pl.BlockSpec((1, LANES), lambda b, p, i: (0, p)),
        ],
        out_specs=pl.BlockSpec((1, tq, LANES), lambda b, p, i: (b, i, p)),
        out_shape=jax.ShapeDtypeStruct((b, s, SB_WIDTH), BF16),
        compiler_params=_params(3),
        name="attention",
    )(q, k, v, km, vm, og)


def _mix_kernel(h_ref, halo_ref, mhalo_ref, sb_ref, x_ref, cw_ref, cb_ref, lg_ref, lb_ref, wo_ref,
                g2_ref, wr_ref, br_ref,
                xn_ref, hn_ref, eid_ref, gate_ref, rank_ref, before_ref, cnt_ref,
                win_ref, conv_ref, cv_ref, carry_ref, *, tm, chunk):
    b = pl.program_id(0)
    i = pl.program_id(1)

    @pl.when(jnp.logical_and(b == 0, i == 0))
    def _():
        carry_ref[...] = jnp.zeros_like(carry_ref)

    @pl.when(i == 0)
    def _():
        win_ref[0:HALO, :] = mhalo_ref[...]

    @pl.when(i > 0)
    def _():
        win_ref[0:HALO, :] = halo_ref[0]

    win_ref[HALO:, :] = h_ref[0]

    first_tap = HALO - (CONV_KERNEL - 1)

    def conv_chunk(c, _):
        r0 = pl.multiple_of(c * chunk, chunk)
        rows = chunk + HALO
        for lt in range(cb_ref.shape[1] // LANES):
            sl = slice(lt * LANES, (lt + 1) * LANES)
            window = win_ref[pl.ds(r0, rows), sl]
            acc = jnp.broadcast_to(cb_ref[:, sl], (chunk, LANES))
            for b in range(SUBLANES):
                shifted = window if b == 0 else pltpu.roll(window, shift=rows - b, axis=0)
                for a in range(HALO // SUBLANES + 1):
                    j = SUBLANES * a + b - first_tap
                    if 0 <= j < CONV_KERNEL:
                        acc = acc + cw_ref[j:j + 1, sl] * shifted[SUBLANES * a:SUBLANES * a + chunk, :]
            conv_ref[:, sl] = acc
        acc = conv_ref[...]
        mu = jnp.mean(acc, axis=-1, keepdims=True)
        cen = acc - mu
        var = jnp.mean(cen * cen, axis=-1, keepdims=True)
        y = cen * lax.rsqrt(var + EPS) * lg_ref[...] + lb_ref[...]
        cv_ref[pl.ds(r0, chunk), :] = (y * jax.nn.sigmoid(y)).astype(BF16)
        return 0

    lax.fori_loop(0, tm // chunk, conv_chunk, 0)

    mixed = jnp.dot(sb_ref[0], wo_ref[0:SB_WIDTH, :], preferred_element_type=F32)
    mixed = mixed + jnp.dot(cv_ref[...], wo_ref[SB_WIDTH:, :], preferred_element_type=F32)
    xn = x_ref[0] + mixed
    xn_ref[0] = xn
    ms = jnp.mean(xn * xn, axis=-1, keepdims=True)
    hn = xn * lax.rsqrt(ms + EPS) * g2_ref[...]
    hn_ref[0] = hn

    logits = lax.dot_general(wr_ref[...], hn, (((1,), (1,)), ((), ())),
                             precision=lax.Precision.HIGHEST, preferred_element_type=F32) + br_ref[...]
    eidx = lax.broadcasted_iota(jnp.int32, logits.shape, 0)
    vals, idxs = [], []
    for _ in range(TOP_K):
        m = jnp.max(logits, axis=0, keepdims=True)
        sel = jnp.min(jnp.where(logits == m, eidx, N_EXPERTS), axis=0, keepdims=True)
        vals.append(m)
        idxs.append(sel)
        logits = jnp.where(eidx == sel, -jnp.inf, logits)
    exps = [jnp.exp(v - vals[0]) for v in vals]
    denom = exps[0] + exps[1] + exps[2] + exps[3]
    gate_ref[...] = jnp.concatenate([e / denom for e in exps], axis=0)
    eid_ref[...] = jnp.concatenate(idxs, axis=0)

    onehots = [(eidx == s).astype(F32) for s in idxs]
    chosen = onehots[0] + onehots[1] + onehots[2] + onehots[3]
    tr = lax.broadcasted_iota(jnp.int32, (tm, tm), 0)
    tc = lax.broadcasted_iota(jnp.int32, (tm, tm), 1)
    before = (tr < tc).astype(BF16)
    prefix = jnp.dot(chosen.astype(BF16), before, preferred_element_type=F32) + carry_ref[...]
    rank_ref[...] = jnp.concatenate(
        [jnp.sum(o * prefix, axis=0, keepdims=True) for o in onehots], axis=0).astype(jnp.int32)
    before_ref[...] = prefix.astype(jnp.int32)
    carry_ref[...] = carry_ref[...] + jnp.sum(chosen, axis=1, keepdims=True)
    cnt_ref[...] = jnp.broadcast_to(carry_ref[...], cnt_ref.shape)


def _mix(hcv, mhalo, sb, x, conv_w, conv_b, ln_g, ln_b, w_out_bf, g2, wr_t, br, tm, chunk):
    b, s, d = x.shape
    cw = hcv.shape[-1]
    t = b * s
    per = s // tm
    tile = lambda b, i: (b, i, 0)
    fixed = lambda b, i: (0, 0)
    tok = lambda b, i: (0, b * per + i)
    return pl.pallas_call(
        functools.partial(_mix_kernel, tm=tm, chunk=chunk),
        grid=(b, per),
        in_specs=[
            pl.BlockSpec((1, tm, cw), tile),
            pl.BlockSpec((1, HALO, cw), lambda b, i: (b, jnp.maximum(i * (tm // HALO) - 1, 0), 0)),
            pl.BlockSpec((HALO, cw), fixed),
            pl.BlockSpec((1, tm, SB_WIDTH), tile),
            pl.BlockSpec((1, tm, d), tile),
            pl.BlockSpec((CONV_KERNEL, cw), fixed),
            pl.BlockSpec((1, cw), fixed),
            pl.BlockSpec((1, cw), fixed),
            pl.BlockSpec((1, cw), fixed),
            pl.BlockSpec((SB_WIDTH + cw, d), fixed),
            pl.BlockSpec((1, d), fixed),
            pl.BlockSpec((N_EXPERTS, d), fixed),
            pl.BlockSpec((N_EXPERTS, 1), fixed),
        ],
        out_specs=[
            pl.BlockSpec((1, tm, d), tile),
            pl.BlockSpec((1, tm, d), tile),
            pl.BlockSpec((TOP_K, tm), tok),
            pl.BlockSpec((TOP_K, tm), tok),
            pl.BlockSpec((TOP_K, tm), tok),
            pl.BlockSpec((N_EXPERTS, tm), tok),
            pl.BlockSpec((N_EXPERTS, LANES), fixed),
        ],
        out_shape=[
            jax.ShapeDtypeStruct((b, s, d), F32),
            jax.ShapeDtypeStruct((b, s, d), F32),
            jax.ShapeDtypeStruct((TOP_K, t), jnp.int32),
            jax.ShapeDtypeStruct((TOP_K, t), F32),
            jax.ShapeDtypeStruct((TOP_K, t), jnp.int32),
            jax.ShapeDtypeStruct((N_EXPERTS, t), jnp.int32),
            jax.ShapeDtypeStruct((N_EXPERTS, LANES), F32),
        ],
        scratch_shapes=[
            pltpu.VMEM((tm + HALO, cw), F32),
            pltpu.VMEM((chunk, cw), F32),
            pltpu.VMEM((tm, cw), BF16),
            pltpu.VMEM((N_EXPERTS, 1), F32),
        ],
        compiler_params=_params(2),
        name="mix_router",
    )(hcv, hcv, mhalo, sb, x, conv_w, conv_b, ln_g, ln_b, w_out_bf, g2, wr_t, br)


def _wait_rows(rows_ref, n, sem):
    pltpu.make_async_copy(rows_ref.at[pl.ds(0, n)], rows_ref.at[pl.ds(0, n)], sem).wait()


def _dispatch_kernel(padlo_ref, padlen_ref, nused_ref, pos_ref, hn_ref, rows_ref, zeros_ref, sem, zsem,
                     *, tm, n_blocks, n_pad):
    i = pl.program_id(0)

    @pl.when(i == 0)
    def _():
        zeros_ref[...] = jnp.zeros_like(zeros_ref)
        for e in range(N_EXPERTS):
            lo, ln = padlo_ref[e], padlen_ref[e]
            head = (-lo) & (SUBLANES - 1)
            for r in range(SUBLANES - 1):
                @pl.when(r < head)
                def _():
                    pltpu.make_async_copy(zeros_ref.at[pl.ds(0, 1)], rows_ref.at[pl.ds(lo + r, 1)], zsem).start()

            lo8, ln8 = lo + head, ln - head
            for bit in range(SUBLANES.bit_length() - 1, EXPERT_ROWS.bit_length() - 1):
                size = 1 << bit

                @pl.when((ln8 >> bit) & 1 == 1)
                def _():
                    start = pl.multiple_of(lo8 + ((ln8 >> (bit + 1)) << (bit + 1)), SUBLANES)
                    pltpu.make_async_copy(zeros_ref.at[pl.ds(0, size)],
                                          rows_ref.at[pl.ds(start, size)], zsem).start()

        def tail(j, _):
            start = pl.multiple_of(j * EXPERT_ROWS, EXPERT_ROWS)
            pltpu.make_async_copy(zeros_ref, rows_ref.at[pl.ds(start, EXPERT_ROWS)], zsem).start()
            return 0

        lax.fori_loop(nused_ref[0], n_blocks, tail, 0)

    def issue(t, _):
        for k in range(TOP_K):
            pltpu.make_async_copy(hn_ref.at[pl.ds(t, 1)],
                                  rows_ref.at[pl.ds(pos_ref[k, t], 1)], sem).start()
        return 0

    lax.fori_loop(0, tm, issue, 0)
    _wait_rows(rows_ref, TOP_K * tm, sem)

    @pl.when(i == pl.num_programs(0) - 1)
    def _():
        _wait_rows(rows_ref, n_pad, zsem)


def _dispatch(pad_lo, pad_len, n_used, pos, hn, n_blocks, tm):
    t, d = hn.shape
    n_rows = n_blocks * EXPERT_ROWS
    grid_spec = pltpu.PrefetchScalarGridSpec(
        num_scalar_prefetch=3,
        grid=(t // tm,),
        in_specs=[
            pl.BlockSpec((TOP_K, tm), lambda i, *_: (0, i), memory_space=pltpu.SMEM),
            pl.BlockSpec((tm, d), lambda i, *_: (i, 0)),
        ],
        out_specs=pl.BlockSpec(memory_space=pl.ANY),
        scratch_shapes=[pltpu.VMEM((EXPERT_ROWS, d), hn.dtype), pltpu.SemaphoreType.DMA(()),
                        pltpu.SemaphoreType.DMA(())],
    )
    return pl.pallas_call(
        functools.partial(_dispatch_kernel, tm=tm, n_blocks=n_blocks, n_pad=n_rows - t * TOP_K),
        grid_spec=grid_spec,
        out_shape=jax.ShapeDtypeStruct((n_rows, d), hn.dtype),
        compiler_params=_params(1),
        name="dispatch",
    )(pad_lo, pad_len, n_used, pos, hn)


def _expert_kernel(be_ref, nused_ref, x_ref, wu_ref, bu_ref, wd_ref, bd_ref, y_ref,
                   wu_bf, wd_bf, *, cast_rows):
    j = pl.program_id(0)
    de = wd_ref.shape[1]

    @pl.when(j >= nused_ref[0])
    def _():
        y_ref[...] = jnp.zeros_like(y_ref)

    @pl.when(j < nused_ref[0])
    def _():
        prev = be_ref[jnp.maximum(j - 1, 0)]

        @pl.when(jnp.logical_or(j == 0, be_ref[j] != prev))
        def _():
            def cast(c, _):
                r0 = pl.multiple_of(c * cast_rows, cast_rows)
                wu_bf[pl.ds(r0, cast_rows), :] = wu_ref[0, pl.ds(r0, cast_rows), :].astype(BF16)
                wd_bf[pl.ds(r0, cast_rows), :] = wd_ref[0, pl.ds(r0, cast_rows), :].astype(BF16)
                return 0

            lax.fori_loop(0, wu_ref.shape[1] // cast_rows, cast, 0)

        up = jnp.dot(x_ref[...].astype(BF16), wu_bf[...], preferred_element_type=F32) + bu_ref[0]
        glu = jnp.minimum(up[:, :de], SWIGLU_LIMIT)
        lin = jnp.clip(up[:, de:], -SWIGLU_LIMIT, SWIGLU_LIMIT)
        act = glu * jax.nn.sigmoid(SWIGLU_ALPHA * glu) * (lin + 1.0)
        y = jnp.dot(act.astype(BF16), wd_bf[...], preferred_element_type=F32) + bd_ref[0]
        y_ref[...] = _pack_halves(y)


def _pack_halves(y):
    half = y.shape[1] // 2
    bits = lambda v: lax.bitcast_convert_type(v.astype(BF16).astype(F32), jnp.uint32)
    return (bits(y[:, half:]) & jnp.uint32(0xFFFF0000)) | (bits(y[:, :half]) >> 16)


def _unpack_halves(w):
    lo = lax.bitcast_convert_type(w << 16, F32).astype(BF16)
    hi = lax.bitcast_convert_type(w & jnp.uint32(0xFFFF0000), F32).astype(BF16)
    return lo, hi


def _experts(block_expert, n_used, x_rows, w_up, b_up, w_down, b_down):
    n_rows, d = x_rows.shape
    n_blocks = n_rows // EXPERT_ROWS + 1
    ne, _, up_cols = w_up.shape
    de = w_down.shape[1]
    blk = lambda j, be, nu: (jnp.minimum(j, nu[0] - 1), 0)
    wsel = lambda j, be, nu: (be[jnp.minimum(j, nu[0] - 1)], 0, 0)
    grid_spec = pltpu.PrefetchScalarGridSpec(
        num_scalar_prefetch=2,
        grid=(n_blocks,),
        in_specs=[
            pl.BlockSpec((EXPERT_ROWS, d), blk),
            pl.BlockSpec((1, d, up_cols), wsel),
            pl.BlockSpec((1, 1, up_cols), wsel),
            pl.BlockSpec((1, de, d), wsel),
            pl.BlockSpec((1, 1, d), wsel),
        ],
        out_specs=pl.BlockSpec((EXPERT_ROWS, d // 2), lambda j, be, nu: (j, 0)),
        scratch_shapes=[pltpu.VMEM((d, up_cols), BF16), pltpu.VMEM((de, d), BF16)],
    )
    return pl.pallas_call(
        functools.partial(_expert_kernel, cast_rows=64),
        grid_spec=grid_spec,
        out_shape=jax.ShapeDtypeStruct((n_blocks * EXPERT_ROWS, d // 2), jnp.uint32),
        compiler_params=_params(1),
        name="experts",
    )(block_expert, n_used, x_rows, w_up, b_up.reshape(ne, 1, up_cols),
      w_down, b_down.reshape(ne, 1, d))


def _combine_kernel(rstart_ref, rcnt_ref, eid_ref, pos_ref, gate_ref, xn_ref, rows_ref, o_ref, buf, sems,
                    *, tm, cap):
    i = pl.program_id(0)
    n = pl.num_programs(0)
    slot = lax.rem(i, 2)
    half = o_ref.shape[1] // 2

    def windows(tile):
        out, off = [], jnp.int32(0)
        for e in range(N_EXPERTS):
            start = rstart_ref[tile * N_EXPERTS + e]
            cnt = rcnt_ref[tile * N_EXPERTS + e]
            first = (start >> 3) << 3
            pieces = jnp.where(cnt > 0, (start - first + cnt + PIECE - 1) // PIECE, 0)
            out.append((first, pieces, off))
            off = off + pieces * PIECE
        return out, off

    def fetch(tile, s):
        for first, pieces, off in windows(tile)[0]:
            def body(p, _):
                src = rows_ref.at[pl.ds(pl.multiple_of(first + p * PIECE, SUBLANES), PIECE)]
                dst = buf.at[s, pl.ds(pl.multiple_of(off + p * PIECE, PIECE), PIECE)]
                pltpu.make_async_copy(src, dst, sems.at[s]).start()
                return 0

            lax.fori_loop(0, pieces, body, 0)

    @pl.when(i == 0)
    def _():
        buf[...] = jnp.zeros_like(buf)
        fetch(0, 0)

    @pl.when(i + 1 < n)
    def _():
        fetch(i + 1, 1 - slot)

    wins, total = windows(i)
    eid = eid_ref[...]
    staged = pos_ref[...]
    shift = jnp.zeros_like(staged)
    for e, (first, _, off) in enumerate(wins):
        shift = jnp.where(eid == e, off - first, shift)
    staged = staged + shift
    gates = gate_ref[...]

    @pl.when(total > 0)
    def _():
        rows = pl.ds(0, pl.multiple_of(total, PIECE))
        pltpu.make_async_copy(rows_ref.at[rows], buf.at[slot, rows], sems.at[slot]).wait()

    o_ref[...] = xn_ref[...]

    def chunk(c, _):
        c0 = pl.multiple_of(c * COMBINE_CHUNK, COMBINE_CHUNK)
        col = lax.broadcasted_iota(jnp.int32, (tm, COMBINE_CHUNK), 1) + c0
        sel = jnp.zeros((tm, COMBINE_CHUNK), F32)
        for k in range(TOP_K):
            sel = sel + jnp.where(staged[:, k:k + 1] == col, gates[:, k:k + 1], 0.0)
        sel = sel.astype(BF16)
        lo, hi = _unpack_halves(buf[slot, pl.ds(c0, COMBINE_CHUNK), :])
        o_ref[:, :half] += jnp.dot(sel, lo, preferred_element_type=F32)
        o_ref[:, half:] += jnp.dot(sel, hi, preferred_element_type=F32)
        return 0

    lax.fori_loop(0, (total + COMBINE_CHUNK - 1) // COMBINE_CHUNK, chunk, 0)


def _combine(rstart, rcnt, eid_t, pos_t, gates_t, xn, y_rows, tm):
    t, d = xn.shape
    n = t // tm
    cap = -(-(TOP_K * tm + 2 * PIECE * N_EXPERTS) // COMBINE_CHUNK) * COMBINE_CHUNK
    tok = lambda i, *_: (i, 0)
    grid_spec = pltpu.PrefetchScalarGridSpec(
        num_scalar_prefetch=2,
        grid=(n,),
        in_specs=[
            pl.BlockSpec((tm, TOP_K), tok),
            pl.BlockSpec((tm, TOP_K), tok),
            pl.BlockSpec((tm, TOP_K), tok),
            pl.BlockSpec((tm, d), tok),
            pl.BlockSpec(memory_space=pl.ANY),
        ],
        out_specs=pl.BlockSpec((tm, d), tok),
        scratch_shapes=[pltpu.VMEM((2, cap, d // 2), jnp.uint32), pltpu.SemaphoreType.DMA((2,))],
    )
    return pl.pallas_call(
        functools.partial(_combine_kernel, tm=tm, cap=cap),
        grid_spec=grid_spec,
        out_shape=jax.ShapeDtypeStruct((t, d), F32),
        compiler_params=_params(1),
        name="combine",
    )(rstart, rcnt, eid_t, pos_t, gates_t, xn, y_rows)


def _tile(n, want):
    t = min(n, want)
    assert n % t == 0, (n, t)
    return t


def kernel(x, meta_tokens, norm1_g, w_in, q_norm_g, k_norm_g, conv_w, conv_b, conv_ln_g, conv_ln_b,
           attn_out_g, w_out, norm2_g, w_router, b_router, w_up, b_up, w_down, b_down):
    assert norm1_g.shape[0] == 1, "single layer: meta-token rows are only keys/values and conv context"
    b, s, d = x.shape
    t = b * s
    cw = conv_w.shape[-1]

    g1 = norm1_g[0][None, :]
    w_in_bf = w_in[0].astype(BF16)
    qg = jnp.tile(q_norm_g[0], SB_HEADS)[None, :]
    kg = jnp.tile(k_norm_g[0], SB_HEADS)[None, :]

    q, k, v, hcv = _inproj(x.reshape(t, d), g1, w_in_bf, qg, kg, _tile(t, 512))
    _, km, vm, hm = _inproj(meta_tokens, g1, w_in_bf, qg, kg, N_META)
    pad = ((0, LANES - N_META), (0, 0))
    sb = _attention(q.reshape(b, s, SB_WIDTH), k.reshape(b, s, SB_WIDTH), v.reshape(b, s, SB_WIDTH),
                    jnp.pad(km, pad), jnp.pad(vm, pad), attn_out_g[0].reshape(1, SB_WIDTH), _tile(s, 256))

    mhalo = jnp.concatenate([jnp.zeros((HALO - N_META, cw), F32), hm], axis=0)
    tm = _tile(s, 512)
    xn, hn, eid, gates, rank, before, counts = _mix(
        hcv.reshape(b, s, cw), mhalo, sb, x, conv_w[0], conv_b[0][None, :], conv_ln_g[0][None, :],
        conv_ln_b[0][None, :], w_out[0].astype(BF16), norm2_g[0][None, :], w_router[0].T,
        b_router[0][:, None], tm, _tile(tm, 64))

    counts = counts[:, 0].astype(jnp.int32)
    padded = (counts + EXPERT_ROWS - 1) // EXPERT_ROWS * EXPERT_ROWS
    pad_end = jnp.cumsum(padded)
    pad_start = pad_end - padded
    n_blocks = t * TOP_K // EXPERT_ROWS + N_EXPERTS
    block_row = jnp.arange(n_blocks, dtype=jnp.int32) * EXPERT_ROWS
    block_expert = jnp.minimum(jnp.sum(block_row[:, None] >= pad_end[None, :], axis=1), N_EXPERTS - 1).astype(jnp.int32)
    n_used = (pad_end[-1:] // EXPERT_ROWS).astype(jnp.int32)
    onehot = eid[:, :, None] == jnp.arange(N_EXPERTS, dtype=jnp.int32)
    pos = rank + jnp.sum(jnp.where(onehot, pad_start, 0), axis=-1)

    x_rows = _dispatch(pad_start + counts, padded - counts, n_used, pos, hn.reshape(t, d), n_blocks,
                       _tile(t, 512))
    y_rows = _experts(block_expert, n_used, x_rows, w_up[0], b_up[0], w_down[0], b_down[0])
    tc = _tile(t, 256)
    run_start = pad_start[:, None] + before[:, ::tc]
    run_end = jnp.concatenate([run_start[:, 1:], (pad_start + counts)[:, None]], axis=1)
    out = _combine(run_start.T.reshape(-1), (run_end - run_start).T.reshape(-1), eid.T, pos.T, gates.T,
                   xn.reshape(t, d), y_rows, tc)
    return out.reshape(b, s, d)
```

```python
import functools

import jax
import jax.numpy as jnp
from jax import lax
from jax.experimental import pallas as pl
from jax.experimental.pallas import tpu as pltpu

N_META = 16
SB_HEADS = 8
SB_HEAD_DIM = 64
SB_WIDTH = SB_HEADS * SB_HEAD_DIM
CONV_KERNEL = 31
N_EXPERTS = 32
TOP_K = 4
SWIGLU_LIMIT = 7.0
SWIGLU_ALPHA = 1.702
EPS = 1e-6
F32_EXP_UNDERFLOW = -104.0

LANES = 128
SUBLANES = 8
PIECE = 16
COMBINE_CHUNK = 512
HALO = 32
EXPERT_ROWS = 512
VMEM_LIMIT = 56 * 1024 * 1024

F32 = jnp.float32
BF16 = jnp.bfloat16


def _params(n_axes, vmem=VMEM_LIMIT):
    return pltpu.CompilerParams(dimension_semantics=("arbitrary",) * n_axes, vmem_limit_bytes=vmem)


def _inproj_kernel(x_ref, g1_ref, w_ref, qg_ref, kg_ref, q_ref, k_ref, v_ref, h_ref):
    x = x_ref[...]
    ms = jnp.mean(x * x, axis=-1, keepdims=True)
    n = (x * lax.rsqrt(ms + EPS) * g1_ref[...]).astype(BF16)
    lo = lax.broadcasted_iota(jnp.int32, (1, LANES), 1) < SB_HEAD_DIM

    def proj(c0, c1):
        return jnp.dot(n, w_ref[:, c0:c1], preferred_element_type=F32)

    def head_norm(acc, g_ref, out_ref, scale):
        for c in range(SB_WIDTH // LANES):
            sl = slice(c * LANES, (c + 1) * LANES)
            a = acc[:, sl]
            sq = a * a
            s_lo = jnp.sum(jnp.where(lo, sq, 0.0), axis=-1, keepdims=True)
            s_hi = jnp.sum(jnp.where(lo, 0.0, sq), axis=-1, keepdims=True)
            r = lax.rsqrt(jnp.where(lo, s_lo, s_hi) * (1.0 / SB_HEAD_DIM) + EPS)
            out_ref[:, sl] = (a * r * (g_ref[:, sl] * scale)).astype(BF16)

    head_norm(proj(0, SB_WIDTH), qg_ref, q_ref, SB_HEAD_DIM ** -0.5)
    head_norm(proj(SB_WIDTH, 2 * SB_WIDTH), kg_ref, k_ref, 1.0)
    v_ref[...] = proj(2 * SB_WIDTH, 3 * SB_WIDTH).astype(BF16)
    cw = (w_ref.shape[1] - 3 * SB_WIDTH) // 2
    val = proj(3 * SB_WIDTH, 3 * SB_WIDTH + cw)
    gate = proj(3 * SB_WIDTH + cw, 3 * SB_WIDTH + 2 * cw)
    h_ref[...] = val * jax.nn.sigmoid(gate)


def _inproj(x2, g1, w_in_bf, qg, kg, tm):
    t, d = x2.shape
    cols = w_in_bf.shape[1]
    cw = (cols - 3 * SB_WIDTH) // 2
    row = lambda i: (i, 0)
    fixed = lambda i: (0, 0)
    return pl.pallas_call(
        _inproj_kernel,
        grid=(t // tm,),
        in_specs=[
            pl.BlockSpec((tm, d), row),
            pl.BlockSpec((1, d), fixed),
            pl.BlockSpec((d, cols), fixed),
            pl.BlockSpec((1, SB_WIDTH), fixed),
            pl.BlockSpec((1, SB_WIDTH), fixed),
        ],
        out_specs=[
            pl.BlockSpec((tm, SB_WIDTH), row),
            pl.BlockSpec((tm, SB_WIDTH), row),
            pl.BlockSpec((tm, SB_WIDTH), row),
            pl.BlockSpec((tm, cw), row),
        ],
        out_shape=[
            jax.ShapeDtypeStruct((t, SB_WIDTH), BF16),
            jax.ShapeDtypeStruct((t, SB_WIDTH), BF16),
            jax.ShapeDtypeStruct((t, SB_WIDTH), BF16),
            jax.ShapeDtypeStruct((t, cw), F32),
        ],
        compiler_params=_params(1),
        name="inproj",
    )(x2, g1, w_in_bf, qg, kg)


def _attn_kernel(q_ref, k_ref, v_ref, km_ref, vm_ref, g_ref, o_ref, *, tq):
    i = pl.program_id(2)
    lane = lax.broadcasted_iota(jnp.int32, (1, LANES), 1)
    lo = lane < SB_HEAD_DIM
    q2 = q_ref[0]
    zero_bf = jnp.zeros((), BF16)
    qs = jnp.concatenate([jnp.where(lo, q2, zero_bf), jnp.where(lo, zero_bf, q2)], axis=0)
    row = lax.broadcasted_iota(jnp.int32, (2 * tq, tq), 0)
    col = lax.broadcasted_iota(jnp.int32, (2 * tq, tq), 1)
    causal = col < jnp.where(row >= tq, row - tq, row)
    srow = lax.broadcasted_iota(jnp.int32, (tq, tq), 0)
    scol = lax.broadcasted_iota(jnp.int32, (tq, tq), 1)
    neg_suffix = jnp.where(srow > scol, -1.0, 0.0).astype(BF16)

    def block(kb, vb, acc, r, mask, neg_suffix_m):
        z = lax.dot_general(qs, kb, (((1,), (1,)), ((), ())), preferred_element_type=F32)
        m = jnp.minimum(z, 0.0)
        p = jnp.maximum(z, 0.0)
        l1p = jnp.log(1.0 + jnp.exp(m - p))
        log_beta = m - l1p
        neg_keep = p + l1p
        if mask is not None:
            neg_keep = jnp.where(mask, neg_keep, 0.0)
        later = jnp.dot(neg_keep.astype(BF16), neg_suffix_m, preferred_element_type=F32)
        a = jnp.exp(log_beta + later + r)
        if mask is not None:
            a = jnp.where(mask, a, 0.0)
        res = jnp.dot(a.astype(BF16), vb, preferred_element_type=F32)
        acc = acc + jnp.where(lo, res[:tq], res[tq:])
        return acc, r - jnp.sum(neg_keep, axis=-1, keepdims=True)

    def kv_block(j):
        start = pl.multiple_of(j * tq, tq)
        return k_ref[0, pl.ds(start, tq), :], v_ref[0, pl.ds(start, tq), :]

    def live(r):
        return jnp.max(r) > F32_EXP_UNDERFLOW

    acc0 = jnp.zeros((tq, LANES), F32)
    r0 = jnp.zeros((2 * tq, 1), F32)

    def diagonal(_):
        return block(*kv_block(i), acc0, r0, causal, neg_suffix)

    def diagonal_and_previous(_):
        acc, r = block(*kv_block(i), acc0, r0, causal, neg_suffix)
        return block(*kv_block(i - 1), acc, r, None, neg_suffix)

    acc, r = lax.cond(i > 0, diagonal_and_previous, diagonal, 0)

    def cond(c):
        return jnp.logical_and(c[0] < i, c[3])

    def body(c):
        acc, r = block(*kv_block(i - 1 - c[0]), c[1], c[2], None, neg_suffix)
        return c[0] + 1, acc, r, live(r)

    _, acc, r, alive = lax.while_loop(cond, body, (jnp.int32(1), acc, r, live(r)))

    def meta_block(acc):
        return block(km_ref[...], vm_ref[...], acc, r, lane < N_META, neg_suffix[:LANES, :LANES])[0]

    acc = lax.cond(alive, meta_block, lambda acc: acc, acc)

    sq = acc * acc
    s_lo = jnp.sum(jnp.where(lo, sq, 0.0), axis=-1, keepdims=True)
    s_hi = jnp.sum(jnp.where(lo, 0.0, sq), axis=-1, keepdims=True)
    rn = lax.rsqrt(jnp.where(lo, s_lo, s_hi) * (1.0 / SB_HEAD_DIM) + EPS)
    o_ref[0] = (acc * rn * g_ref[...]).astype(BF16)


def _attention(q, k, v, km, vm, og, tq):
    b, s, _ = q.shape
    n_pairs = SB_WIDTH // LANES
    return pl.pallas_call(
        functools.partial(_attn_kernel, tq=tq),
        grid=(b, n_pairs, s // tq),
        in_specs=[
            pl.BlockSpec((1, tq, LANES), lambda b, p, i: (b, i, p)),
            pl.BlockSpec((1, s, LANES), lambda b, p, i: (b, 0, p)),
            pl.BlockSpec((1, s, LANES), lambda b, p, i: (b, 0, p)),
            pl.BlockSpec((LANES, LANES), lambda b, p, i: (0, p)),
            pl.BlockSpec((LANES, LANES), lambda b, p, i: (0, p)),
            pl.BlockSpec((1, LANES), lambda b, p, i: (0, p)),
        ],
        out_specs=pl.BlockSpec((1, tq, LANES), lambda b, p, i: (b, i, p)),
        out_shape=jax.ShapeDtypeStruct((b, s, SB_WIDTH), BF16),
        compiler_params=_params(3),
        name="attention",
    )(q, k, v, km, vm, og)


def _mix_kernel(h_ref, halo_ref, mhalo_ref, sb_ref, x_ref, cw_ref, cb_ref, lg_ref, lb_ref, wo_ref,
                g2_ref, wr_ref, br_ref,
                xn_ref, hn_ref, eid_ref, gate_ref, rank_ref, before_ref, cnt_ref,
                win_ref, conv_ref, cv_ref, carry_ref, *, tm, chunk):
    b = pl.program_id(0)
    i = pl.program_id(1)

    @pl.when(jnp.logical_and(b == 0, i == 0))
    def _():
        carry_ref[...] = jnp.zeros_like(carry_ref)

    @pl.when(i == 0)
    def _():
        win_ref[0:HALO, :] = mhalo_ref[...]

    @pl.when(i > 0)
    def _():
        win_ref[0:HALO, :] = halo_ref[0]

    win_ref[HALO:, :] = h_ref[0]

    first_tap = HALO - (CONV_KERNEL - 1)

    def conv_chunk(c, _):
        r0 = pl.multiple_of(c * chunk, chunk)
        rows = chunk + HALO
        for lt in range(cb_ref.shape[1] // LANES):
            sl = slice(lt * LANES, (lt + 1) * LANES)
            window = win_ref[pl.ds(r0, rows), sl]
            acc = jnp.broadcast_to(cb_ref[:, sl], (chunk, LANES))
            for b in range(SUBLANES):
                shifted = window if b == 0 else pltpu.roll(window, shift=rows - b, axis=0)
                for a in range(HALO // SUBLANES + 1):
                    j = SUBLANES * a + b - first_tap
                    if 0 <= j < CONV_KERNEL:
                        acc = acc + cw_ref[j:j + 1, sl] * shifted[SUBLANES * a:SUBLANES * a + chunk, :]
            conv_ref[:, sl] = acc
        acc = conv_ref[...]
        mu = jnp.mean(acc, axis=-1, keepdims=True)
        cen = acc - mu
        var = jnp.mean(cen * cen, axis=-1, keepdims=True)
        y = cen * lax.rsqrt(var + EPS) * lg_ref[...] + lb_ref[...]
        cv_ref[pl.ds(r0, chunk), :] = (y * jax.nn.sigmoid(y)).astype(BF16)
        return 0

    lax.fori_loop(0, tm // chunk, conv_chunk, 0)

    mixed = jnp.dot(sb_ref[0], wo_ref[0:SB_WIDTH, :], preferred_element_type=F32)
    mixed = mixed + jnp.dot(cv_ref[...], wo_ref[SB_WIDTH:, :], preferred_element_type=F32)
    xn = x_ref[0] + mixed
    xn_ref[0] = xn
    ms = jnp.mean(xn * xn, axis=-1, keepdims=True)
    hn = xn * lax.rsqrt(ms + EPS) * g2_ref[...]
    hn_ref[0] = hn

    logits = lax.dot_general(wr_ref[...], hn, (((1,), (1,)), ((), ())),
                             precision=lax.Precision.HIGHEST, preferred_element_type=F32) + br_ref[...]
    eidx = lax.broadcasted_iota(jnp.int32, logits.shape, 0)
    vals, idxs = [], []
    for _ in range(TOP_K):
        m = jnp.max(logits, axis=0, keepdims=True)
        sel = jnp.min(jnp.where(logits == m, eidx, N_EXPERTS), axis=0, keepdims=True)
        vals.append(m)
        idxs.append(sel)
        logits = jnp.where(eidx == sel, -jnp.inf, logits)
    exps = [jnp.exp(v - vals[0]) for v in vals]
    denom = exps[0] + exps[1] + exps[2] + exps[3]
    gate_ref[...] = jnp.concatenate([e / denom for e in exps], axis=0)
    eid_ref[...] = jnp.concatenate(idxs, axis=0)

    onehots = [(eidx == s).astype(F32) for s in idxs]
    chosen = onehots[0] + onehots[1] + onehots[2] + onehots[3]
    tr = lax.broadcasted_iota(jnp.int32, (tm, tm), 0)
    tc = lax.broadcasted_iota(jnp.int32, (tm, tm), 1)
    before = (tr < tc).astype(BF16)
    prefix = jnp.dot(chosen.astype(BF16), before, preferred_element_type=F32) + carry_ref[...]
    rank_ref[...] = jnp.concatenate(
        [jnp.sum(o * prefix, axis=0, keepdims=True) for o in onehots], axis=0).astype(jnp.int32)
    before_ref[...] = prefix.astype(jnp.int32)
    carry_ref[...] = carry_ref[...] + jnp.sum(chosen, axis=1, keepdims=True)
    cnt_ref[...] = jnp.broadcast_to(carry_ref[...], cnt_ref.shape)


def _mix(hcv, mhalo, sb, x, conv_w, conv_b, ln_g, ln_b, w_out_bf, g2, wr_t, br, tm, chunk):
    b, s, d = x.shape
    cw = hcv.shape[-1]
    t = b * s
    per = s // tm
    tile = lambda b, i: (b, i, 0)
    fixed = lambda b, i: (0, 0)
    tok = lambda b, i: (0, b * per + i)
    return pl.pallas_call(
        functools.partial(_mix_kernel, tm=tm, chunk=chunk),
        grid=(b, per),
        in_specs=[
            pl.BlockSpec((1, tm, cw), tile),
            pl.BlockSpec((1, HALO, cw), lambda b, i: (b, jnp.maximum(i * (tm // HALO) - 1, 0), 0)),
            pl.BlockSpec((HALO, cw), fixed),
            pl.BlockSpec((1, tm, SB_WIDTH), tile),
            pl.BlockSpec((1, tm, d), tile),
            pl.BlockSpec((CONV_KERNEL, cw), fixed),
            pl.BlockSpec((1, cw), fixed),
            pl.BlockSpec((1, cw), fixed),
            pl.BlockSpec((1, cw), fixed),
            pl.BlockSpec((SB_WIDTH + cw, d), fixed),
            pl.BlockSpec((1, d), fixed),
            pl.BlockSpec((N_EXPERTS, d), fixed),
            pl.BlockSpec((N_EXPERTS, 1), fixed),
        ],
        out_specs=[
            pl.BlockSpec((1, tm, d), tile),
            pl.BlockSpec((1, tm, d), tile),
            pl.BlockSpec((TOP_K, tm), tok),
            pl.BlockSpec((TOP_K, tm), tok),
            pl.BlockSpec((TOP_K, tm), tok),
            pl.BlockSpec((N_EXPERTS, tm), tok),
            pl.BlockSpec((N_EXPERTS, LANES), fixed),
        ],
        out_shape=[
            jax.ShapeDtypeStruct((b, s, d), F32),
            jax.ShapeDtypeStruct((b, s, d), F32),
            jax.ShapeDtypeStruct((TOP_K, t), jnp.int32),
            jax.ShapeDtypeStruct((TOP_K, t), F32),
            jax.ShapeDtypeStruct((TOP_K, t), jnp.int32),
            jax.ShapeDtypeStruct((N_EXPERTS, t), jnp.int32),
            jax.ShapeDtypeStruct((N_EXPERTS, LANES), F32),
        ],
        scratch_shapes=[
            pltpu.VMEM((tm + HALO, cw), F32),
            pltpu.VMEM((chunk, cw), F32),
            pltpu.VMEM((tm, cw), BF16),
            pltpu.VMEM((N_EXPERTS, 1), F32),
        ],
        compiler_params=_params(2),
        name="mix_router",
    )(hcv, hcv, mhalo, sb, x, conv_w, conv_b, ln_g, ln_b, w_out_bf, g2, wr_t, br)


def _wait_rows(rows_ref, n, sem):
    pltpu.make_async_copy(rows_ref.at[pl.ds(0, n)], rows_ref.at[pl.ds(0, n)], sem).wait()


def _dispatch_kernel(padlo_ref, padlen_ref, nused_ref, pos_ref, hn_ref, rows_ref, zeros_ref, sem, zsem,
                     *, tm, n_blocks, n_pad):
    i = pl.program_id(0)

    @pl.when(i == 0)
    def _():
        zeros_ref[...] = jnp.zeros_like(zeros_ref)
        for e in range(N_EXPERTS):
            lo, ln = padlo_ref[e], padlen_ref[e]
            head = (-lo) & (SUBLANES - 1)
            for r in range(SUBLANES - 1):
                @pl.when(r < head)
                def _():
                    pltpu.make_async_copy(zeros_ref.at[pl.ds(0, 1)], rows_ref.at[pl.ds(lo + r, 1)], zsem).start()

            lo8, ln8 = lo + head, ln - head
            for bit in range(SUBLANES.bit_length() - 1, EXPERT_ROWS.bit_length() - 1):
                size = 1 << bit

                @pl.when((ln8 >> bit) & 1 == 1)
                def _():
                    start = pl.multiple_of(lo8 + ((ln8 >> (bit + 1)) << (bit + 1)), SUBLANES)
                    pltpu.make_async_copy(zeros_ref.at[pl.ds(0, size)],
                                          rows_ref.at[pl.ds(start, size)], zsem).start()

        def tail(j, _):
            start = pl.multiple_of(j * EXPERT_ROWS, EXPERT_ROWS)
            pltpu.make_async_copy(zeros_ref, rows_ref.at[pl.ds(start, EXPERT_ROWS)], zsem).start()
            return 0

        lax.fori_loop(nused_ref[0], n_blocks, tail, 0)

    def issue(t, _):
        for k in range(TOP_K):
            pltpu.make_async_copy(hn_ref.at[pl.ds(t, 1)],
                                  rows_ref.at[pl.ds(pos_ref[k, t], 1)], sem).start(priority=k % 2)
        return 0

    lax.fori_loop(0, tm, issue, 0)
    _wait_rows(rows_ref, TOP_K * tm, sem)

    @pl.when(i == pl.num_programs(0) - 1)
    def _():
        _wait_rows(rows_ref, n_pad, zsem)


def _dispatch(pad_lo, pad_len, n_used, pos, hn, n_blocks, tm):
    t, d = hn.shape
    n_rows = n_blocks * EXPERT_ROWS
    grid_spec = pltpu.PrefetchScalarGridSpec(
        num_scalar_prefetch=3,
        grid=(t // tm,),
        in_specs=[
            pl.BlockSpec((TOP_K, tm), lambda i, *_: (0, i), memory_space=pltpu.SMEM),
            pl.BlockSpec((tm, d), lambda i, *_: (i, 0)),
        ],
        out_specs=pl.BlockSpec(memory_space=pl.ANY),
        scratch_shapes=[pltpu.VMEM((EXPERT_ROWS, d), hn.dtype), pltpu.SemaphoreType.DMA(()),
                        pltpu.SemaphoreType.DMA(())],
    )
    return pl.pallas_call(
        functools.partial(_dispatch_kernel, tm=tm, n_blocks=n_blocks, n_pad=n_rows - t * TOP_K),
        grid_spec=grid_spec,
        out_shape=jax.ShapeDtypeStruct((n_rows, d), hn.dtype),
        compiler_params=_params(1),
        name="dispatch",
    )(pad_lo, pad_len, n_used, pos, hn)


def _expert_kernel(be_ref, nused_ref, x_ref, wu_ref, bu_ref, wd_ref, bd_ref, y_ref,
                   wu_bf, wd_bf, *, cast_rows):
    j = pl.program_id(0)
    de = wd_ref.shape[1]

    @pl.when(j >= nused_ref[0])
    def _():
        y_ref[...] = jnp.zeros_like(y_ref)

    @pl.when(j < nused_ref[0])
    def _():
        prev = be_ref[jnp.maximum(j - 1, 0)]

        @pl.when(jnp.logical_or(j == 0, be_ref[j] != prev))
        def _():
            def cast(c, _):
                r0 = pl.multiple_of(c * cast_rows, cast_rows)
                wu_bf[pl.ds(r0, cast_rows), :] = wu_ref[0, pl.ds(r0, cast_rows), :].astype(BF16)
                wd_bf[pl.ds(r0, cast_rows), :] = wd_ref[0, pl.ds(r0, cast_rows), :].astype(BF16)
                return 0

            lax.fori_loop(0, wu_ref.shape[1] // cast_rows, cast, 0)

        up = jnp.dot(x_ref[...].astype(BF16), wu_bf[...], preferred_element_type=F32) + bu_ref[0]
        glu = jnp.minimum(up[:, :de], SWIGLU_LIMIT)
        lin = jnp.clip(up[:, de:], -SWIGLU_LIMIT, SWIGLU_LIMIT)
        act = glu * jax.nn.sigmoid(SWIGLU_ALPHA * glu) * (lin + 1.0)
        y = jnp.dot(act.astype(BF16), wd_bf[...], preferred_element_type=F32) + bd_ref[0]
        y_ref[...] = _pack_halves(y)


def _pack_halves(y):
    half = y.shape[1] // 2
    bits = lambda v: lax.bitcast_convert_type(v.astype(BF16).astype(F32), jnp.uint32)
    return (bits(y[:, half:]) & jnp.uint32(0xFFFF0000)) | (bits(y[:, :half]) >> 16)


def _unpack_halves(w):
    lo = lax.bitcast_convert_type(w << 16, F32).astype(BF16)
    hi = lax.bitcast_convert_type(w & jnp.uint32(0xFFFF0000), F32).astype(BF16)
    return lo, hi


def _experts(block_expert, n_used, x_rows, w_up, b_up, w_down, b_down):
    n_rows, d = x_rows.shape
    n_blocks = n_rows // EXPERT_ROWS + 1
    ne, _, up_cols = w_up.shape
    de = w_down.shape[1]
    blk = lambda j, be, nu: (jnp.minimum(j, nu[0] - 1), 0)
    wsel = lambda j, be, nu: (be[jnp.minimum(j, nu[0] - 1)], 0, 0)
    grid_spec = pltpu.PrefetchScalarGridSpec(
        num_scalar_prefetch=2,
        grid=(n_blocks,),
        in_specs=[
            pl.BlockSpec((EXPERT_ROWS, d), blk),
            pl.BlockSpec((1, d, up_cols), wsel),
            pl.BlockSpec((1, 1, up_cols), wsel),
            pl.BlockSpec((1, de, d), wsel),
            pl.BlockSpec((1, 1, d), wsel),
        ],
        out_specs=pl.BlockSpec((EXPERT_ROWS, d // 2), lambda j, be, nu: (j, 0)),
        scratch_shapes=[pltpu.VMEM((d, up_cols), BF16), pltpu.VMEM((de, d), BF16)],
    )
    return pl.pallas_call(
        functools.partial(_expert_kernel, cast_rows=64),
        grid_spec=grid_spec,
        out_shape=jax.ShapeDtypeStruct((n_blocks * EXPERT_ROWS, d // 2), jnp.uint32),
        compiler_params=_params(1),
        name="experts",
    )(block_expert, n_used, x_rows, w_up, b_up.reshape(ne, 1, up_cols),
      w_down, b_down.reshape(ne, 1, d))


def _combine_kernel(rstart_ref, rcnt_ref, eid_ref, pos_ref, gate_ref, xn_ref, rows_ref, o_ref, buf, sems,
                    *, tm, cap):
    i = pl.program_id(0)
    n = pl.num_programs(0)
    slot = lax.rem(i, 2)
    half = o_ref.shape[1] // 2

    def windows(tile):
        out, off = [], jnp.int32(0)
        for e in range(N_EXPERTS):
            start = rstart_ref[tile * N_EXPERTS + e]
            cnt = rcnt_ref[tile * N_EXPERTS + e]
            first = (start >> 3) << 3
            pieces = jnp.where(cnt > 0, (start - first + cnt + PIECE - 1) // PIECE, 0)
            out.append((first, pieces, off))
            off = off + pieces * PIECE
        return out, off

    def fetch(tile, s):
        for first, pieces, off in windows(tile)[0]:
            def body(p, _):
                src = rows_ref.at[pl.ds(pl.multiple_of(first + p * PIECE, SUBLANES), PIECE)]
                dst = buf.at[s, pl.ds(pl.multiple_of(off + p * PIECE, PIECE), PIECE)]
                pltpu.make_async_copy(src, dst, sems.at[s]).start()
                return 0

            lax.fori_loop(0, pieces, body, 0)

    @pl.when(i == 0)
    def _():
        buf[...] = jnp.zeros_like(buf)
        fetch(0, 0)

    @pl.when(i + 1 < n)
    def _():
        fetch(i + 1, 1 - slot)

    wins, total = windows(i)
    eid = eid_ref[...]
    staged = pos_ref[...]
    shift = jnp.zeros_like(staged)
    for e, (first, _, off) in enumerate(wins):
        shift = jnp.where(eid == e, off - first, shift)
    staged = staged + shift
    gates = gate_ref[...]

    @pl.when(total > 0)
    def _():
        rows = pl.ds(0, pl.multiple_of(total, PIECE))
        pltpu.make_async_copy(rows_ref.at[rows], buf.at[slot, rows], sems.at[slot]).wait()

    o_ref[...] = xn_ref[...]

    def chunk(c, _):
        c0 = pl.multiple_of(c * COMBINE_CHUNK, COMBINE_CHUNK)
        col = lax.broadcasted_iota(jnp.int32, (tm, COMBINE_CHUNK), 1) + c0
        sel = jnp.zeros((tm, COMBINE_CHUNK), F32)
        for k in range(TOP_K):
            sel = sel + jnp.where(staged[:, k:k + 1] == col, gates[:, k:k + 1], 0.0)
        sel = sel.astype(BF16)
        lo, hi = _unpack_halves(buf[slot, pl.ds(c0, COMBINE_CHUNK), :])
        o_ref[:, :half] += jnp.dot(sel, lo, preferred_element_type=F32)
        o_ref[:, half:] += jnp.dot(sel, hi, preferred_element_type=F32)
        return 0

    lax.fori_loop(0, (total + COMBINE_CHUNK - 1) // COMBINE_CHUNK, chunk, 0)


def _combine(rstart, rcnt, eid_t, pos_t, gates_t, xn, y_rows, tm):
    t, d = xn.shape
    n = t // tm
    cap = -(-(TOP_K * tm + 2 * PIECE * N_EXPERTS) // COMBINE_CHUNK) * COMBINE_CHUNK
    tok = lambda i, *_: (i, 0)
    grid_spec = pltpu.PrefetchScalarGridSpec(
        num_scalar_prefetch=2,
        grid=(n,),
        in_specs=[
            pl.BlockSpec((tm, TOP_K), tok),
            pl.BlockSpec((tm, TOP_K), tok),
            pl.BlockSpec((tm, TOP_K), tok),
            pl.BlockSpec((tm, d), tok),
            pl.BlockSpec(memory_space=pl.ANY),
        ],
        out_specs=pl.BlockSpec((tm, d), tok),
        scratch_shapes=[pltpu.VMEM((2, cap, d // 2), jnp.uint32), pltpu.SemaphoreType.DMA((2,))],
    )
    return pl.pallas_call(
        functools.partial(_combine_kernel, tm=tm, cap=cap),
        grid_spec=grid_spec,
        out_shape=jax.ShapeDtypeStruct((t, d), F32),
        compiler_params=_params(1),
        name="combine",
    )(rstart, rcnt, eid_t, pos_t, gates_t, xn, y_rows)


def _tile(n, want):
    t = min(n, want)
    assert n % t == 0, (n, t)
    return t


def kernel(x, meta_tokens, norm1_g, w_in, q_norm_g, k_norm_g, conv_w, conv_b, conv_ln_g, conv_ln_b,
           attn_out_g, w_out, norm2_g, w_router, b_router, w_up, b_up, w_down, b_down):
    assert norm1_g.shape[0] == 1, "single layer: meta-token rows are only keys/values and conv context"
    b, s, d = x.shape
    t = b * s
    cw = conv_w.shape[-1]

    g1 = norm1_g[0][None, :]
    w_in_bf = w_in[0].astype(BF16)
    qg = jnp.tile(q_norm_g[0], SB_HEADS)[None, :]
    kg = jnp.tile(k_norm_g[0], SB_HEADS)[None, :]

    q, k, v, hcv = _inproj(x.reshape(t, d), g1, w_in_bf, qg, kg, _tile(t, 512))
    _, km, vm, hm = _inproj(meta_tokens, g1, w_in_bf, qg, kg, N_META)
    pad = ((0, LANES - N_META), (0, 0))
    sb = _attention(q.reshape(b, s, SB_WIDTH), k.reshape(b, s, SB_WIDTH), v.reshape(b, s, SB_WIDTH),
                    jnp.pad(km, pad), jnp.pad(vm, pad), attn_out_g[0].reshape(1, SB_WIDTH), _tile(s, 256))

    mhalo = jnp.concatenate([jnp.zeros((HALO - N_META, cw), F32), hm], axis=0)
    tm = _tile(s, 512)
    xn, hn, eid, gates, rank, before, counts = _mix(
        hcv.reshape(b, s, cw), mhalo, sb, x, conv_w[0], conv_b[0][None, :], conv_ln_g[0][None, :],
        conv_ln_b[0][None, :], w_out[0].astype(BF16), norm2_g[0][None, :], w_router[0].T,
        b_router[0][:, None], tm, _tile(tm, 64))

    counts = counts[:, 0].astype(jnp.int32)
    padded = (counts + EXPERT_ROWS - 1) // EXPERT_ROWS * EXPERT_ROWS
    pad_end = jnp.cumsum(padded)
    pad_start = pad_end - padded
    n_blocks = t * TOP_K // EXPERT_ROWS + N_EXPERTS
    block_row = jnp.arange(n_blocks, dtype=jnp.int32) * EXPERT_ROWS
    block_expert = jnp.minimum(jnp.sum(block_row[:, None] >= pad_end[None, :], axis=1), N_EXPERTS - 1).astype(jnp.int32)
    n_used = (pad_end[-1:] // EXPERT_ROWS).astype(jnp.int32)
    onehot = eid[:, :, None] == jnp.arange(N_EXPERTS, dtype=jnp.int32)
    pos = rank + jnp.sum(jnp.where(onehot, pad_start, 0), axis=-1)

    x_rows = _dispatch(pad_start + counts, padded - counts, n_used, pos, hn.reshape(t, d), n_blocks,
                       _tile(t, 512))
    y_rows = _experts(block_expert, n_used, x_rows, w_up[0], b_up[0], w_down[0], b_down[0])
    tc = _tile(t, 256)
    run_start = pad_start[:, None] + before[:, ::tc]
    run_end = jnp.concatenate([run_start[:, 1:], (pad_start + counts)[:, None]], axis=1)
    out = _combine(run_start.T.reshape(-1), (run_end - run_start).T.reshape(-1), eid.T, pos.T, gates.T,
                   xn.reshape(t, d), y_rows, tc)
    return out.reshape(b, s, d)
```

```python
import functools

import jax
import jax.numpy as jnp
from jax import lax
from jax.experimental import pallas as pl
from jax.experimental.pallas import tpu as pltpu

N_META = 16
SB_HEADS = 8
SB_HEAD_DIM = 64
SB_WIDTH = SB_HEADS * SB_HEAD_DIM
CONV_KERNEL = 31
N_EXPERTS = 32
TOP_K = 4
SWIGLU_LIMIT = 7.0
SWIGLU_ALPHA = 1.702
EPS = 1e-6
F32_EXP_UNDERFLOW = -104.0

LANES = 128
SUBLANES = 8
PIECE = 16
COMBINE_CHUNK = 512
HALO = 32
EXPERT_ROWS = 512
VMEM_LIMIT = 56 * 1024 * 1024

F32 = jnp.float32
BF16 = jnp.bfloat16


def _params(n_axes, vmem=VMEM_LIMIT):
    return pltpu.CompilerParams(dimension_semantics=("arbitrary",) * n_axes, vmem_limit_bytes=vmem)


def _inproj_kernel(x_ref, g1_ref, w_ref, qg_ref, kg_ref, q_ref, k_ref, v_ref, h_ref):
    x = x_ref[...]
    ms = jnp.mean(x * x, axis=-1, keepdims=True)
    n = (x * lax.rsqrt(ms + EPS) * g1_ref[...]).astype(BF16)
    lo = lax.broadcasted_iota(jnp.int32, (1, LANES), 1) < SB_HEAD_DIM

    def proj(c0, c1):
        return jnp.dot(n, w_ref[:, c0:c1], preferred_element_type=F32)

    def head_norm(acc, g_ref, out_ref, scale):
        for c in range(SB_WIDTH // LANES):
            sl = slice(c * LANES, (c + 1) * LANES)
            a = acc[:, sl]
            sq = a * a
            s_lo = jnp.sum(jnp.where(lo, sq, 0.0), axis=-1, keepdims=True)
            s_hi = jnp.sum(jnp.where(lo, 0.0, sq), axis=-1, keepdims=True)
            r = lax.rsqrt(jnp.where(lo, s_lo, s_hi) * (1.0 / SB_HEAD_DIM) + EPS)
            out_ref[:, sl] = (a * r * (g_ref[:, sl] * scale)).astype(BF16)

    head_norm(proj(0, SB_WIDTH), qg_ref, q_ref, SB_HEAD_DIM ** -0.5)
    head_norm(proj(SB_WIDTH, 2 * SB_WIDTH), kg_ref, k_ref, 1.0)
    v_ref[...] = proj(2 * SB_WIDTH, 3 * SB_WIDTH).astype(BF16)
    cw = (w_ref.shape[1] - 3 * SB_WIDTH) // 2
    val = proj(3 * SB_WIDTH, 3 * SB_WIDTH + cw)
    gate = proj(3 * SB_WIDTH + cw, 3 * SB_WIDTH + 2 * cw)
    h_ref[...] = val * jax.nn.sigmoid(gate)


def _inproj(x2, g1, w_in_bf, qg, kg, tm):
    t, d = x2.shape
    cols = w_in_bf.shape[1]
    cw = (cols - 3 * SB_WIDTH) // 2
    row = lambda i: (i, 0)
    fixed = lambda i: (0, 0)
    return pl.pallas_call(
        _inproj_kernel,
        grid=(t // tm,),
        in_specs=[
            pl.BlockSpec((tm, d), row),
            pl.BlockSpec((1, d), fixed),
            pl.BlockSpec((d, cols), fixed),
            pl.BlockSpec((1, SB_WIDTH), fixed),
            pl.BlockSpec((1, SB_WIDTH), fixed),
        ],
        out_specs=[
            pl.BlockSpec((tm, SB_WIDTH), row),
            pl.BlockSpec((tm, SB_WIDTH), row),
            pl.BlockSpec((tm, SB_WIDTH), row),
            pl.BlockSpec((tm, cw), row),
        ],
        out_shape=[
            jax.ShapeDtypeStruct((t, SB_WIDTH), BF16),
            jax.ShapeDtypeStruct((t, SB_WIDTH), BF16),
            jax.ShapeDtypeStruct((t, SB_WIDTH), BF16),
            jax.ShapeDtypeStruct((t, cw), F32),
        ],
        compiler_params=_params(1),
        name="inproj",
    )(x2, g1, w_in_bf, qg, kg)


def _attn_kernel(q_ref, k_ref, v_ref, km_ref, vm_ref, g_ref, o_ref, *, tq):
    i = pl.program_id(2)
    lane = lax.broadcasted_iota(jnp.int32, (1, LANES), 1)
    lo = lane < SB_HEAD_DIM
    q2 = q_ref[0]
    zero_bf = jnp.zeros((), BF16)
    qs = jnp.concatenate([jnp.where(lo, q2, zero_bf), jnp.where(lo, zero_bf, q2)], axis=0)
    row = lax.broadcasted_iota(jnp.int32, (2 * tq, tq), 0)
    col = lax.broadcasted_iota(jnp.int32, (2 * tq, tq), 1)
    causal = col < jnp.where(row >= tq, row - tq, row)
    srow = lax.broadcasted_iota(jnp.int32, (tq, tq), 0)
    scol = lax.broadcasted_iota(jnp.int32, (tq, tq), 1)
    neg_suffix = jnp.where(srow > scol, -1.0, 0.0).astype(BF16)

    def block(kb, vb, acc, r, mask, neg_suffix_m):
        z = lax.dot_general(qs, kb, (((1,), (1,)), ((), ())), preferred_element_type=F32)
        m = jnp.minimum(z, 0.0)
        p = jnp.maximum(z, 0.0)
        l1p = jnp.log(1.0 + jnp.exp(m - p))
        log_beta = m - l1p
        neg_keep = p + l1p
        if mask is not None:
            neg_keep = jnp.where(mask, neg_keep, 0.0)
        later = jnp.dot(neg_keep.astype(BF16), neg_suffix_m, preferred_element_type=F32)
        a = jnp.exp(log_beta + later + r)
        if mask is not None:
            a = jnp.where(mask, a, 0.0)
        res = jnp.dot(a.astype(BF16), vb, preferred_element_type=F32)
        acc = acc + jnp.where(lo, res[:tq], res[tq:])
        return acc, r - jnp.sum(neg_keep, axis=-1, keepdims=True)

    def kv_block(j):
        start = pl.multiple_of(j * tq, tq)
        return k_ref[0, pl.ds(start, tq), :], v_ref[0, pl.ds(start, tq), :]

    def live(r):
        return jnp.max(r) > F32_EXP_UNDERFLOW

    acc0 = jnp.zeros((tq, LANES), F32)
    r0 = jnp.zeros((2 * tq, 1), F32)

    def diagonal(_):
        return block(*kv_block(i), acc0, r0, causal, neg_suffix)

    def diagonal_and_previous(_):
        acc, r = block(*kv_block(i), acc0, r0, causal, neg_suffix)
        return block(*kv_block(i - 1), acc, r, None, neg_suffix)

    acc, r = lax.cond(i > 0, diagonal_and_previous, diagonal, 0)

    def cond(c):
        return jnp.logical_and(c[0] < i, c[3])

    def body(c):
        acc, r = block(*kv_block(i - 1 - c[0]), c[1], c[2], None, neg_suffix)
        return c[0] + 1, acc, r, live(r)

    _, acc, r, alive = lax.while_loop(cond, body, (jnp.int32(1), acc, r, live(r)))

    def meta_block(acc):
        return block(km_ref[...], vm_ref[...], acc, r, lane < N_META, neg_suffix[:LANES, :LANES])[0]

    acc = lax.cond(alive, meta_block, lambda acc: acc, acc)

    sq = acc * acc
    s_lo = jnp.sum(jnp.where(lo, sq, 0.0), axis=-1, keepdims=True)
    s_hi = jnp.sum(jnp.where(lo, 0.0, sq), axis=-1, keepdims=True)
    rn = lax.rsqrt(jnp.where(lo, s_lo, s_hi) * (1.0 / SB_HEAD_DIM) + EPS)
    o_ref[0] = (acc * rn * g_ref[...]).astype(BF16)


def _attention(q, k, v, km, vm, og, tq):
    b, s, _ = q.shape
    n_pairs = SB_WIDTH // LANES
    return pl.pallas_call(
        functools.partial(_attn_kernel, tq=tq),
        grid=(b, n_pairs, s // tq),
        in_specs=[
            pl.BlockSpec((1, tq, LANES), lambda b, p, i: (b, i, p)),
            pl.BlockSpec((1, s, LANES), lambda b, p, i: (b, 0, p)),
            pl.BlockSpec((1, s, LANES), lambda b, p, i: (b, 0, p)),
            pl.BlockSpec((LANES, LANES), lambda b, p, i: (0, p)),
            pl.BlockSpec((LANES, LANES), lambda b, p, i: (0, p)),
            pl.BlockSpec((1, LANES), lambda b, p, i: (0, p)),
        ],
        out_specs=pl.BlockSpec((1, tq, LANES), lambda b, p, i: (b, i, p)),
        out_shape=jax.ShapeDtypeStruct((b, s, SB_WIDTH), BF16),
        compiler_params=_params(3),
        name="attention",
    )(q, k, v, km, vm, og)


def _mix_kernel(h_ref, halo_ref, mhalo_ref, sb_ref, x_ref, cw_ref, cb_ref, lg_ref, lb_ref, wo_ref,
                g2_ref, wr_ref, br_ref,
                xn_ref, hn_ref, eid_ref, gate_ref, rank_ref, before_ref, cnt_ref,
                win_ref, conv_ref, cv_ref, carry_ref, *, tm, chunk):
    b = pl.program_id(0)
    i = pl.program_id(1)

    @pl.when(jnp.logical_and(b == 0, i == 0))
    def _():
        carry_ref[...] = jnp.zeros_like(carry_ref)

    @pl.when(i == 0)
    def _():
        win_ref[0:HALO, :] = mhalo_ref[...]

    @pl.when(i > 0)
    def _():
        win_ref[0:HALO, :] = halo_ref[0]

    win_ref[HALO:, :] = h_ref[0]

    first_tap = HALO - (CONV_KERNEL - 1)

    def conv_chunk(c, _):
        r0 = pl.multiple_of(c * chunk, chunk)
        rows = chunk + HALO
        for lt in range(cb_ref.shape[1] // LANES):
            sl = slice(lt * LANES, (lt + 1) * LANES)
            window = win_ref[pl.ds(r0, rows), sl]
            acc = jnp.broadcast_to(cb_ref[:, sl], (chunk, LANES))
            for b in range(SUBLANES):
                shifted = window if b == 0 else pltpu.roll(window, shift=rows - b, axis=0)
                for a in range(HALO // SUBLANES + 1):
                    j = SUBLANES * a + b - first_tap
                    if 0 <= j < CONV_KERNEL:
                        acc = acc + cw_ref[j:j + 1, sl] * shifted[SUBLANES * a:SUBLANES * a + chunk, :]
            conv_ref[:, sl] = acc
        acc = conv_ref[...]
        mu = jnp.mean(acc, axis=-1, keepdims=True)
        cen = acc - mu
        var = jnp.mean(cen * cen, axis=-1, keepdims=True)
        y = cen * lax.rsqrt(var + EPS) * lg_ref[...] + lb_ref[...]
        cv_ref[pl.ds(r0, chunk), :] = (y * jax.nn.sigmoid(y)).astype(BF16)
        return 0

    lax.fori_loop(0, tm // chunk, conv_chunk, 0)

    mixed = jnp.dot(sb_ref[0], wo_ref[0:SB_WIDTH, :], preferred_element_type=F32)
    mixed = mixed + jnp.dot(cv_ref[...], wo_ref[SB_WIDTH:, :], preferred_element_type=F32)
    xn = x_ref[0] + mixed
    xn_ref[0] = xn
    ms = jnp.mean(xn * xn, axis=-1, keepdims=True)
    hn = xn * lax.rsqrt(ms + EPS) * g2_ref[...]
    hn_ref[0] = hn

    logits = lax.dot_general(wr_ref[...], hn, (((1,), (1,)), ((), ())),
                             precision=lax.Precision.HIGHEST, preferred_element_type=F32) + br_ref[...]
    eidx = lax.broadcasted_iota(jnp.int32, logits.shape, 0)
    vals, idxs = [], []
    for _ in range(TOP_K):
        m = jnp.max(logits, axis=0, keepdims=True)
        sel = jnp.min(jnp.where(logits == m, eidx, N_EXPERTS), axis=0, keepdims=True)
        vals.append(m)
        idxs.append(sel)
        logits = jnp.where(eidx == sel, -jnp.inf, logits)
    exps = [jnp.exp(v - vals[0]) for v in vals]
    denom = exps[0] + exps[1] + exps[2] + exps[3]
    gate_ref[...] = jnp.concatenate([e / denom for e in exps], axis=0)
    eid_ref[...] = jnp.concatenate(idxs, axis=0)

    onehots = [(eidx == s).astype(F32) for s in idxs]
    chosen = onehots[0] + onehots[1] + onehots[2] + onehots[3]
    tr = lax.broadcasted_iota(jnp.int32, (tm, tm), 0)
    tc = lax.broadcasted_iota(jnp.int32, (tm, tm), 1)
    before = (tr < tc).astype(BF16)
    prefix = jnp.dot(chosen.astype(BF16), before, preferred_element_type=F32) + carry_ref[...]
    rank_ref[...] = jnp.concatenate(
        [jnp.sum(o * prefix, axis=0, keepdims=True) for o in onehots], axis=0).astype(jnp.int32)
    before_ref[...] = prefix.astype(jnp.int32)
    carry_ref[...] = carry_ref[...] + jnp.sum(chosen, axis=1, keepdims=True)
    cnt_ref[...] = jnp.broadcast_to(carry_ref[...], cnt_ref.shape)


def _mix(hcv, mhalo, sb, x, conv_w, conv_b, ln_g, ln_b, w_out_bf, g2, wr_t, br, tm, chunk):
    b, s, d = x.shape
    cw = hcv.shape[-1]
    t = b * s
    per = s // tm
    tile = lambda b, i: (b, i, 0)
    fixed = lambda b, i: (0, 0)
    tok = lambda b, i: (0, b * per + i)
    return pl.pallas_call(
        functools.partial(_mix_kernel, tm=tm, chunk=chunk),
        grid=(b, per),
        in_specs=[
            pl.BlockSpec((1, tm, cw), tile),
            pl.BlockSpec((1, HALO, cw), lambda b, i: (b, jnp.maximum(i * (tm // HALO) - 1, 0), 0)),
            pl.BlockSpec((HALO, cw), fixed),
            pl.BlockSpec((1, tm, SB_WIDTH), tile),
            pl.BlockSpec((1, tm, d), tile),
            pl.BlockSpec((CONV_KERNEL, cw), fixed),
            pl.BlockSpec((1, cw), fixed),
            pl.BlockSpec((1, cw), fixed),
            pl.BlockSpec((1, cw), fixed),
            pl.BlockSpec((SB_WIDTH + cw, d), fixed),
            pl.BlockSpec((1, d), fixed),
            pl.BlockSpec((N_EXPERTS, d), fixed),
            pl.BlockSpec((N_EXPERTS, 1), fixed),
        ],
        out_specs=[
            pl.BlockSpec((1, tm, d), tile),
            pl.BlockSpec((1, tm, d), tile),
            pl.BlockSpec((TOP_K, tm), tok),
            pl.BlockSpec((TOP_K, tm), tok),
            pl.BlockSpec((TOP_K, tm), tok),
            pl.BlockSpec((N_EXPERTS, tm), tok),
            pl.BlockSpec((N_EXPERTS, LANES), fixed),
        ],
        out_shape=[
            jax.ShapeDtypeStruct((b, s, d), F32),
            jax.ShapeDtypeStruct((b, s, d), F32),
            jax.ShapeDtypeStruct((TOP_K, t), jnp.int32),
            jax.ShapeDtypeStruct((TOP_K, t), F32),
            jax.ShapeDtypeStruct((TOP_K, t), jnp.int32),
            jax.ShapeDtypeStruct((N_EXPERTS, t), jnp.int32),
            jax.ShapeDtypeStruct((N_EXPERTS, LANES), F32),
        ],
        scratch_shapes=[
            pltpu.VMEM((tm + HALO, cw), F32),
            pltpu.VMEM((chunk, cw), F32),
            pltpu.VMEM((tm, cw), BF16),
            pltpu.VMEM((N_EXPERTS, 1), F32),
        ],
        compiler_params=_params(2),
        name="mix_router",
    )(hcv, hcv, mhalo, sb, x, conv_w, conv_b, ln_g, ln_b, w_out_bf, g2, wr_t, br)


def _row_tiles(ref, row, n=1):
    return ref.at[pl.ds(pl.multiple_of(row * SUBLANES, SUBLANES), n * SUBLANES)]


def _wait_rows(rows_ref, n, sem):
    pltpu.make_async_copy(_row_tiles(rows_ref, 0, n), _row_tiles(rows_ref, 0, n), sem).wait()


def _dispatch_kernel(padlo_ref, padlen_ref, nused_ref, pos_ref, hn_ref, rows_ref, src_ref, zeros_ref, sem, zsem,
                     *, tm, n_blocks, n_pad):
    i = pl.program_id(0)

    @pl.when(i == 0)
    def _():
        zeros_ref[...] = jnp.zeros_like(zeros_ref)
        for e in range(N_EXPERTS):
            lo, ln = padlo_ref[e], padlen_ref[e]
            for bit in range(EXPERT_ROWS.bit_length() - 1):
                size = 1 << bit

                @pl.when((ln >> bit) & 1 == 1)
                def _():
                    start = lo + ((ln >> (bit + 1)) << (bit + 1))
                    pltpu.make_async_copy(_row_tiles(zeros_ref, 0, size), _row_tiles(rows_ref, start, size),
                                          zsem).start()

        def tail(j, _):
            pltpu.make_async_copy(zeros_ref, _row_tiles(rows_ref, j * EXPERT_ROWS, EXPERT_ROWS), zsem).start()
            return 0

        lax.fori_loop(nused_ref[0], n_blocks, tail, 0)

    groups = hn_ref.shape[1] // LANES
    for c in range(groups):
        src_ref[pl.ds(c, tm, stride=groups), :] = hn_ref[:, c * LANES:(c + 1) * LANES]

    def issue(t, _):
        for k in range(TOP_K):
            pltpu.make_async_copy(_row_tiles(src_ref, t), _row_tiles(rows_ref, pos_ref[t * TOP_K + k]),
                                  sem).start(priority=k % 2)
        return 0

    lax.fori_loop(0, tm, issue, 0, unroll=2)
    _wait_rows(rows_ref, TOP_K * tm, sem)

    @pl.when(i == pl.num_programs(0) - 1)
    def _():
        _wait_rows(rows_ref, n_pad, zsem)


def _dispatch(pad_lo, pad_len, n_used, pos, hn, n_blocks, tm):
    t, d = hn.shape
    n_rows = n_blocks * EXPERT_ROWS
    grid_spec = pltpu.PrefetchScalarGridSpec(
        num_scalar_prefetch=3,
        grid=(t // tm,),
        in_specs=[
            pl.BlockSpec((TOP_K * tm,), lambda i, *_: (i,), memory_space=pltpu.SMEM),
            pl.BlockSpec((tm, d), lambda i, *_: (i, 0)),
        ],
        out_specs=pl.BlockSpec(memory_space=pl.ANY),
        scratch_shapes=[pltpu.VMEM((tm * SUBLANES, LANES), hn.dtype),
                        pltpu.VMEM((EXPERT_ROWS * SUBLANES, LANES), hn.dtype),
                        pltpu.SemaphoreType.DMA(()), pltpu.SemaphoreType.DMA(())],
    )
    assert d == SUBLANES * LANES, d
    return pl.pallas_call(
        functools.partial(_dispatch_kernel, tm=tm, n_blocks=n_blocks, n_pad=n_rows - t * TOP_K),
        grid_spec=grid_spec,
        out_shape=jax.ShapeDtypeStruct((n_rows * SUBLANES, LANES), hn.dtype),
        compiler_params=_params(1),
        name="dispatch",
    )(pad_lo, pad_len, n_used, pos, hn)


def _expert_kernel(be_ref, nused_ref, x_ref, wu_ref, bu_ref, wd_ref, bd_ref, y_ref,
                   wu_bf, wd_bf, *, cast_rows):
    j = pl.program_id(0)
    de = wd_ref.shape[1]

    @pl.when(j >= nused_ref[0])
    def _():
        y_ref[...] = jnp.zeros_like(y_ref)

    @pl.when(j < nused_ref[0])
    def _():
        prev = be_ref[jnp.maximum(j - 1, 0)]

        @pl.when(jnp.logical_or(j == 0, be_ref[j] != prev))
        def _():
            def cast(c, _):
                r0 = pl.multiple_of(c * cast_rows, cast_rows)
                wu_bf[pl.ds(r0, cast_rows), :] = wu_ref[0, pl.ds(r0, cast_rows), :].astype(BF16)
                wd_bf[pl.ds(r0, cast_rows), :] = wd_ref[0, pl.ds(r0, cast_rows), :].astype(BF16)
                return 0

            lax.fori_loop(0, wu_ref.shape[1] // cast_rows, cast, 0)

        x = jnp.concatenate([x_ref[pl.ds(c, EXPERT_ROWS, stride=SUBLANES), :] for c in range(SUBLANES)], axis=1)
        up = jnp.dot(x.astype(BF16), wu_bf[...], preferred_element_type=F32) + bu_ref[0]
        glu = jnp.minimum(up[:, :de], SWIGLU_LIMIT)
        lin = jnp.clip(up[:, de:], -SWIGLU_LIMIT, SWIGLU_LIMIT)
        act = glu * jax.nn.sigmoid(SWIGLU_ALPHA * glu) * (lin + 1.0)
        y = jnp.dot(act.astype(BF16), wd_bf[...], preferred_element_type=F32) + bd_ref[0]
        y_ref[...] = _pack_halves(y)


def _pack_halves(y):
    half = y.shape[1] // 2
    bits = lambda v: lax.bitcast_convert_type(v.astype(BF16).astype(F32), jnp.uint32)
    return (bits(y[:, half:]) & jnp.uint32(0xFFFF0000)) | (bits(y[:, :half]) >> 16)


def _unpack_halves(w):
    lo = lax.bitcast_convert_type(w << 16, F32).astype(BF16)
    hi = lax.bitcast_convert_type(w & jnp.uint32(0xFFFF0000), F32).astype(BF16)
    return lo, hi


def _experts(block_expert, n_used, x_rows, w_up, b_up, w_down, b_down):
    n_rows = x_rows.shape[0] // SUBLANES
    n_blocks = n_rows // EXPERT_ROWS + 1
    ne, d, up_cols = w_up.shape
    de = w_down.shape[1]
    blk = lambda j, be, nu: (jnp.minimum(j, nu[0] - 1), 0)
    wsel = lambda j, be, nu: (be[jnp.minimum(j, nu[0] - 1)], 0, 0)
    grid_spec = pltpu.PrefetchScalarGridSpec(
        num_scalar_prefetch=2,
        grid=(n_blocks,),
        in_specs=[
            pl.BlockSpec((EXPERT_ROWS * SUBLANES, LANES), blk),
            pl.BlockSpec((1, d, up_cols), wsel),
            pl.BlockSpec((1, 1, up_cols), wsel),
            pl.BlockSpec((1, de, d), wsel),
            pl.BlockSpec((1, 1, d), wsel),
        ],
        out_specs=pl.BlockSpec((EXPERT_ROWS, d // 2), lambda j, be, nu: (j, 0)),
        scratch_shapes=[pltpu.VMEM((d, up_cols), BF16), pltpu.VMEM((de, d), BF16)],
    )
    return pl.pallas_call(
        functools.partial(_expert_kernel, cast_rows=64),
        grid_spec=grid_spec,
        out_shape=jax.ShapeDtypeStruct((n_blocks * EXPERT_ROWS, d // 2), jnp.uint32),
        compiler_params=_params(1),
        name="experts",
    )(block_expert, n_used, x_rows, w_up, b_up.reshape(ne, 1, up_cols),
      w_down, b_down.reshape(ne, 1, d))


def _combine_kernel(rstart_ref, rcnt_ref, eid_ref, pos_ref, gate_ref, xn_ref, rows_ref, o_ref, buf, sems,
                    *, tm, cap):
    i = pl.program_id(0)
    n = pl.num_programs(0)
    slot = lax.rem(i, 2)
    half = o_ref.shape[1] // 2

    def windows(tile):
        out, off = [], jnp.int32(0)
        for e in range(N_EXPERTS):
            start = rstart_ref[tile * N_EXPERTS + e]
            cnt = rcnt_ref[tile * N_EXPERTS + e]
            first = (start >> 3) << 3
            pieces = jnp.where(cnt > 0, (start - first + cnt + PIECE - 1) // PIECE, 0)
            out.append((first, pieces, off))
            off = off + pieces * PIECE
        return out, off

    def fetch(tile, s):
        for first, pieces, off in windows(tile)[0]:
            def body(p, _):
                src = rows_ref.at[pl.ds(pl.multiple_of(first + p * PIECE, SUBLANES), PIECE)]
                dst = buf.at[s, pl.ds(pl.multiple_of(off + p * PIECE, PIECE), PIECE)]
                pltpu.make_async_copy(src, dst, sems.at[s]).start()
                return 0

            lax.fori_loop(0, pieces, body, 0)

    @pl.when(i == 0)
    def _():
        buf[...] = jnp.zeros_like(buf)
        fetch(0, 0)

    @pl.when(i + 1 < n)
    def _():
        fetch(i + 1, 1 - slot)

    wins, total = windows(i)
    eid = eid_ref[...]
    staged = pos_ref[...]
    shift = jnp.zeros_like(staged)
    for e, (first, _, off) in enumerate(wins):
        shift = jnp.where(eid == e, off - first, shift)
    staged = staged + shift
    gates = gate_ref[...]

    @pl.when(total > 0)
    def _():
        rows = pl.ds(0, pl.multiple_of(total, PIECE))
        pltpu.make_async_copy(rows_ref.at[rows], buf.at[slot, rows], sems.at[slot]).wait()

    o_ref[...] = xn_ref[...]

    def chunk(c, _):
        c0 = pl.multiple_of(c * COMBINE_CHUNK, COMBINE_CHUNK)
        col = lax.broadcasted_iota(jnp.int32, (tm, COMBINE_CHUNK), 1) + c0
        sel = jnp.zeros((tm, COMBINE_CHUNK), F32)
        for k in range(TOP_K):
            sel = sel + jnp.where(staged[:, k:k + 1] == col, gates[:, k:k + 1], 0.0)
        sel = sel.astype(BF16)
        lo, hi = _unpack_halves(buf[slot, pl.ds(c0, COMBINE_CHUNK), :])
        o_ref[:, :half] += jnp.dot(sel, lo, preferred_element_type=F32)
        o_ref[:, half:] += jnp.dot(sel, hi, preferred_element_type=F32)
        return 0

    lax.fori_loop(0, (total + COMBINE_CHUNK - 1) // COMBINE_CHUNK, chunk, 0)


def _combine(rstart, rcnt, eid_t, pos_t, gates_t, xn, y_rows, tm):
    t, d = xn.shape
    n = t // tm
    cap = -(-(TOP_K * tm + 2 * PIECE * N_EXPERTS) // COMBINE_CHUNK) * COMBINE_CHUNK
    tok = lambda i, *_: (i, 0)
    grid_spec = pltpu.PrefetchScalarGridSpec(
        num_scalar_prefetch=2,
        grid=(n,),
        in_specs=[
            pl.BlockSpec((tm, TOP_K), tok),
            pl.BlockSpec((tm, TOP_K), tok),
            pl.BlockSpec((tm, TOP_K), tok),
            pl.BlockSpec((tm, d), tok),
            pl.BlockSpec(memory_space=pl.ANY),
        ],
        out_specs=pl.BlockSpec((tm, d), tok),
        scratch_shapes=[pltpu.VMEM((2, cap, d // 2), jnp.uint32), pltpu.SemaphoreType.DMA((2,))],
    )
    return pl.pallas_call(
        functools.partial(_combine_kernel, tm=tm, cap=cap),
        grid_spec=grid_spec,
        out_shape=jax.ShapeDtypeStruct((t, d), F32),
        compiler_params=_params(1),
        name="combine",
    )(rstart, rcnt, eid_t, pos_t, gates_t, xn, y_rows)


def _tile(n, want):
    t = min(n, want)
    assert n % t == 0, (n, t)
    return t


def kernel(x, meta_tokens, norm1_g, w_in, q_norm_g, k_norm_g, conv_w, conv_b, conv_ln_g, conv_ln_b,
           attn_out_g, w_out, norm2_g, w_router, b_router, w_up, b_up, w_down, b_down):
    assert norm1_g.shape[0] == 1, "single layer: meta-token rows are only keys/values and conv context"
    b, s, d = x.shape
    t = b * s
    cw = conv_w.shape[-1]

    g1 = norm1_g[0][None, :]
    w_in_bf = w_in[0].astype(BF16)
    qg = jnp.tile(q_norm_g[0], SB_HEADS)[None, :]
    kg = jnp.tile(k_norm_g[0], SB_HEADS)[None, :]

    q, k, v, hcv = _inproj(x.reshape(t, d), g1, w_in_bf, qg, kg, _tile(t, 512))
    _, km, vm, hm = _inproj(meta_tokens, g1, w_in_bf, qg, kg, N_META)
    pad = ((0, LANES - N_META), (0, 0))
    sb = _attention(q.reshape(b, s, SB_WIDTH), k.reshape(b, s, SB_WIDTH), v.reshape(b, s, SB_WIDTH),
                    jnp.pad(km, pad), jnp.pad(vm, pad), attn_out_g[0].reshape(1, SB_WIDTH), _tile(s, 256))

    mhalo = jnp.concatenate([jnp.zeros((HALO - N_META, cw), F32), hm], axis=0)
    tm = _tile(s, 512)
    xn, hn, eid, gates, rank, before, counts = _mix(
        hcv.reshape(b, s, cw), mhalo, sb, x, conv_w[0], conv_b[0][None, :], conv_ln_g[0][None, :],
        conv_ln_b[0][None, :], w_out[0].astype(BF16), norm2_g[0][None, :], w_router[0].T,
        b_router[0][:, None], tm, _tile(tm, 64))

    counts = counts[:, 0].astype(jnp.int32)
    padded = (counts + EXPERT_ROWS - 1) // EXPERT_ROWS * EXPERT_ROWS
    pad_end = jnp.cumsum(padded)
    pad_start = pad_end - padded
    n_blocks = t * TOP_K // EXPERT_ROWS + N_EXPERTS
    block_row = jnp.arange(n_blocks, dtype=jnp.int32) * EXPERT_ROWS
    block_expert = jnp.minimum(jnp.sum(block_row[:, None] >= pad_end[None, :], axis=1), N_EXPERTS - 1).astype(jnp.int32)
    n_used = (pad_end[-1:] // EXPERT_ROWS).astype(jnp.int32)
    onehot = eid[:, :, None] == jnp.arange(N_EXPERTS, dtype=jnp.int32)
    pos = rank + jnp.sum(jnp.where(onehot, pad_start, 0), axis=-1)

    pos_t = pos.T
    x_rows = _dispatch(pad_start + counts, padded - counts, n_used, pos_t.reshape(-1), hn.reshape(t, d),
                       n_blocks, _tile(t, 512))
    y_rows = _experts(block_expert, n_used, x_rows, w_up[0], b_up[0], w_down[0], b_down[0])
    tc = _tile(t, 256)
    run_start = pad_start[:, None] + before[:, ::tc]
    run_end = jnp.concatenate([run_start[:, 1:], (pad_start + counts)[:, None]], axis=1)
    out = _combine(run_start.T.reshape(-1), (run_end - run_start).T.reshape(-1), eid.T, pos_t, gates.T,
                   xn.reshape(t, d), y_rows, tc)
    return out.reshape(b, s, d)
```

```python
import functools

import jax
import jax.numpy as jnp
from jax import lax
from jax.experimental import pallas as pl
from jax.experimental.pallas import tpu as pltpu

N_META = 16
SB_HEADS = 8
SB_HEAD_DIM = 64
SB_WIDTH = SB_HEADS * SB_HEAD_DIM
CONV_KERNEL = 31
N_EXPERTS = 32
TOP_K = 4
SWIGLU_LIMIT = 7.0
SWIGLU_ALPHA = 1.702
EPS = 1e-6
F32_EXP2_UNDERFLOW = -150.0
LOG2_E = 1.4426950408889634

LANES = 128
SUBLANES = 8
PIECE = 16
COMBINE_CHUNK = 512
HALO = 32
EXPERT_ROWS = 512
VMEM_LIMIT = 56 * 1024 * 1024

F32 = jnp.float32
BF16 = jnp.bfloat16


def _params(n_axes, vmem=VMEM_LIMIT):
    return pltpu.CompilerParams(dimension_semantics=("arbitrary",) * n_axes, vmem_limit_bytes=vmem)


def _inproj_kernel(x_ref, g1_ref, w_ref, qg_ref, kg_ref, q_ref, k_ref, v_ref, h_ref):
    x = x_ref[...]
    ms = jnp.mean(x * x, axis=-1, keepdims=True)
    n = (x * lax.rsqrt(ms + EPS) * g1_ref[...]).astype(BF16)
    lo = lax.broadcasted_iota(jnp.int32, (1, LANES), 1) < SB_HEAD_DIM

    def proj(c0, c1):
        return jnp.dot(n, w_ref[:, c0:c1], preferred_element_type=F32)

    def head_norm(acc, g_ref, out_ref, scale):
        for c in range(SB_WIDTH // LANES):
            sl = slice(c * LANES, (c + 1) * LANES)
            a = acc[:, sl]
            sq = a * a
            s_lo = jnp.sum(jnp.where(lo, sq, 0.0), axis=-1, keepdims=True)
            s_hi = jnp.sum(jnp.where(lo, 0.0, sq), axis=-1, keepdims=True)
            r = lax.rsqrt(jnp.where(lo, s_lo, s_hi) * (1.0 / SB_HEAD_DIM) + EPS)
            out_ref[:, sl] = (a * r * (g_ref[:, sl] * scale)).astype(BF16)

    head_norm(proj(0, SB_WIDTH), qg_ref, q_ref, LOG2_E * SB_HEAD_DIM ** -0.5)
    head_norm(proj(SB_WIDTH, 2 * SB_WIDTH), kg_ref, k_ref, 1.0)
    v_ref[...] = proj(2 * SB_WIDTH, 3 * SB_WIDTH).astype(BF16)
    cw = (w_ref.shape[1] - 3 * SB_WIDTH) // 2
    val = proj(3 * SB_WIDTH, 3 * SB_WIDTH + cw)
    gate = proj(3 * SB_WIDTH + cw, 3 * SB_WIDTH + 2 * cw)
    h_ref[...] = val * jax.nn.sigmoid(gate)


def _inproj(x2, g1, w_in_bf, qg, kg, tm):
    t, d = x2.shape
    cols = w_in_bf.shape[1]
    cw = (cols - 3 * SB_WIDTH) // 2
    row = lambda i: (i, 0)
    fixed = lambda i: (0, 0)
    return pl.pallas_call(
        _inproj_kernel,
        grid=(t // tm,),
        in_specs=[
            pl.BlockSpec((tm, d), row),
            pl.BlockSpec((1, d), fixed),
            pl.BlockSpec((d, cols), fixed),
            pl.BlockSpec((1, SB_WIDTH), fixed),
            pl.BlockSpec((1, SB_WIDTH), fixed),
        ],
        out_specs=[
            pl.BlockSpec((tm, SB_WIDTH), row),
            pl.BlockSpec((tm, SB_WIDTH), row),
            pl.BlockSpec((tm, SB_WIDTH), row),
            pl.BlockSpec((tm, cw), row),
        ],
        out_shape=[
            jax.ShapeDtypeStruct((t, SB_WIDTH), BF16),
            jax.ShapeDtypeStruct((t, SB_WIDTH), BF16),
            jax.ShapeDtypeStruct((t, SB_WIDTH), BF16),
            jax.ShapeDtypeStruct((t, cw), F32),
        ],
        compiler_params=_params(1),
        name="inproj",
    )(x2, g1, w_in_bf, qg, kg)


def _attn_kernel(q_ref, k_ref, v_ref, km_ref, vm_ref, g_ref, o_ref, *, tq):
    i = pl.program_id(2)
    lane = lax.broadcasted_iota(jnp.int32, (1, LANES), 1)
    lo = lane < SB_HEAD_DIM
    q2 = q_ref[0]
    zero_bf = jnp.zeros((), BF16)
    qs = jnp.concatenate([jnp.where(lo, q2, zero_bf), jnp.where(lo, zero_bf, q2)], axis=0)
    row = lax.broadcasted_iota(jnp.int32, (2 * tq, tq), 0)
    col = lax.broadcasted_iota(jnp.int32, (2 * tq, tq), 1)
    causal = col < jnp.where(row >= tq, row - tq, row)
    srow = lax.broadcasted_iota(jnp.int32, (tq, tq), 0)
    scol = lax.broadcasted_iota(jnp.int32, (tq, tq), 1)
    neg_suffix = jnp.where(srow > scol, -1.0, 0.0).astype(BF16)

    def block(kb, vb, acc, r, mask, neg_suffix_m):
        z = lax.dot_general(qs, kb, (((1,), (1,)), ((), ())), preferred_element_type=F32)
        m = jnp.minimum(z, 0.0)
        p = jnp.maximum(z, 0.0)
        l1p = jnp.log2(1.0 + jnp.exp2(m - p))
        log_beta = m - l1p
        neg_keep = p + l1p
        if mask is not None:
            neg_keep = jnp.where(mask, neg_keep, 0.0)
        later = jnp.dot(neg_keep.astype(BF16), neg_suffix_m, preferred_element_type=F32)
        a = jnp.exp2(log_beta + later + r)
        if mask is not None:
            a = jnp.where(mask, a, 0.0)
        res = jnp.dot(a.astype(BF16), vb, preferred_element_type=F32)
        acc = acc + jnp.where(lo, res[:tq], res[tq:])
        return acc, r - jnp.sum(neg_keep, axis=-1, keepdims=True)

    def kv_block(j):
        start = pl.multiple_of(j * tq, tq)
        return k_ref[0, pl.ds(start, tq), :], v_ref[0, pl.ds(start, tq), :]

    def live(r):
        return jnp.max(r) > F32_EXP2_UNDERFLOW

    acc0 = jnp.zeros((tq, LANES), F32)
    r0 = jnp.zeros((2 * tq, 1), F32)

    def diagonal(_):
        return block(*kv_block(i), acc0, r0, causal, neg_suffix)

    def diagonal_and_previous(_):
        acc, r = block(*kv_block(i), acc0, r0, causal, neg_suffix)
        return block(*kv_block(i - 1), acc, r, None, neg_suffix)

    acc, r = lax.cond(i > 0, diagonal_and_previous, diagonal, 0)

    def cond(c):
        return jnp.logical_and(c[0] < i, c[3])

    def body(c):
        acc, r = block(*kv_block(i - 1 - c[0]), c[1], c[2], None, neg_suffix)
        return c[0] + 1, acc, r, live(r)

    _, acc, r, alive = lax.while_loop(cond, body, (jnp.int32(1), acc, r, live(r)))

    def meta_block(acc):
        return block(km_ref[...], vm_ref[...], acc, r, lane < N_META, neg_suffix[:LANES, :LANES])[0]

    acc = lax.cond(alive, meta_block, lambda acc: acc, acc)

    sq = acc * acc
    s_lo = jnp.sum(jnp.where(lo, sq, 0.0), axis=-1, keepdims=True)
    s_hi = jnp.sum(jnp.where(lo, 0.0, sq), axis=-1, keepdims=True)
    rn = lax.rsqrt(jnp.where(lo, s_lo, s_hi) * (1.0 / SB_HEAD_DIM) + EPS)
    o_ref[0] = (acc * rn * g_ref[...]).astype(BF16)


def _attention(q, k, v, km, vm, og, tq):
    b, s, _ = q.shape
    n_pairs = SB_WIDTH // LANES
    return pl.pallas_call(
        functools.partial(_attn_kernel, tq=tq),
        grid=(b, n_pairs, s // tq),
        in_specs=[
            pl.BlockSpec((1, tq, LANES), lambda b, p, i: (b, i, p)),
            pl.BlockSpec((1, s, LANES), lambda b, p, i: (b, 0, p)),
            pl.BlockSpec((1, s, LANES), lambda b, p, i: (b, 0, p)),
            pl.BlockSpec((LANES, LANES), lambda b, p, i: (0, p)),
            pl.BlockSpec((LANES, LANES), lambda b, p, i: (0, p)),
            pl.BlockSpec((1, LANES), lambda b, p, i: (0, p)),
        ],
        out_specs=pl.BlockSpec((1, tq, LANES), lambda b, p, i: (b, i, p)),
        out_shape=jax.ShapeDtypeStruct((b, s, SB_WIDTH), BF16),
        compiler_params=_params(3),
        name="attention",
    )(q, k, v, km, vm, og)


def _mix_kernel(h_ref, halo_ref, mhalo_ref, sb_ref, x_ref, cw_ref, cb_ref, lg_ref, lb_ref, wo_ref,
                g2_ref, wr_ref, br_ref,
                xn_ref, hn_ref, eid_ref, gate_ref, rank_ref, before_ref, cnt_ref,
                win_ref, conv_ref, cv_ref, carry_ref, *, tm, chunk):
    b = pl.program_id(0)
    i = pl.program_id(1)

    @pl.when(jnp.logical_and(b == 0, i == 0))
    def _():
        carry_ref[...] = jnp.zeros_like(carry_ref)

    @pl.when(i == 0)
    def _():
        win_ref[0:HALO, :] = mhalo_ref[...]

    @pl.when(i > 0)
    def _():
        win_ref[0:HALO, :] = halo_ref[0]

    win_ref[HALO:, :] = h_ref[0]

    first_tap = HALO - (CONV_KERNEL - 1)

    def conv_chunk(c, _):
        r0 = pl.multiple_of(c * chunk, chunk)
        rows = chunk + HALO
        for lt in range(cb_ref.shape[1] // LANES):
            sl = slice(lt * LANES, (lt + 1) * LANES)
            window = win_ref[pl.ds(r0, rows), sl]
            acc = jnp.broadcast_to(cb_ref[:, sl], (chunk, LANES))
            for b in range(SUBLANES):
                shifted = window if b == 0 else pltpu.roll(window, shift=rows - b, axis=0)
                for a in range(HALO // SUBLANES + 1):
                    j = SUBLANES * a + b - first_tap
                    if 0 <= j < CONV_KERNEL:
                        acc = acc + cw_ref[j:j + 1, sl] * shifted[SUBLANES * a:SUBLANES * a + chunk, :]
            conv_ref[:, sl] = acc
        acc = conv_ref[...]
        mu = jnp.mean(acc, axis=-1, keepdims=True)
        cen = acc - mu
        var = jnp.mean(cen * cen, axis=-1, keepdims=True)
        y = cen * lax.rsqrt(var + EPS) * lg_ref[...] + lb_ref[...]
        cv_ref[pl.ds(r0, chunk), :] = (y * jax.nn.sigmoid(y)).astype(BF16)
        return 0

    lax.fori_loop(0, tm // chunk, conv_chunk, 0)

    mixed = jnp.dot(sb_ref[0], wo_ref[0:SB_WIDTH, :], preferred_element_type=F32)
    mixed = mixed + jnp.dot(cv_ref[...], wo_ref[SB_WIDTH:, :], preferred_element_type=F32)
    xn = x_ref[0] + mixed
    xn_ref[0] = xn
    ms = jnp.mean(xn * xn, axis=-1, keepdims=True)
    hn = xn * lax.rsqrt(ms + EPS) * g2_ref[...]
    hn_ref[0] = hn

    nt = (((1,), (1,)), ((), ()))
    hn_hi = hn.astype(BF16)
    hn_lo = (hn - hn_hi.astype(F32)).astype(BF16)
    both = lax.dot_general(wr_ref[...], hn_hi, nt, preferred_element_type=F32)
    logits = (both[:N_EXPERTS] + both[N_EXPERTS:]
              + lax.dot_general(wr_ref[0:N_EXPERTS, :], hn_lo, nt, preferred_element_type=F32) + br_ref[...])
    eidx = lax.broadcasted_iota(jnp.int32, logits.shape, 0)
    vals, idxs = [], []
    for _ in range(TOP_K):
        m = jnp.max(logits, axis=0, keepdims=True)
        sel = jnp.min(jnp.where(logits == m, eidx, N_EXPERTS), axis=0, keepdims=True)
        vals.append(m)
        idxs.append(sel)
        logits = jnp.where(eidx == sel, -jnp.inf, logits)
    exps = [jnp.exp(v - vals[0]) for v in vals]
    denom = exps[0] + exps[1] + exps[2] + exps[3]
    gate_ref[...] = jnp.concatenate([e / denom for e in exps], axis=0)
    eid_ref[...] = jnp.concatenate(idxs, axis=0)

    onehots = [(eidx == s).astype(F32) for s in idxs]
    chosen = onehots[0] + onehots[1] + onehots[2] + onehots[3]
    tr = lax.broadcasted_iota(jnp.int32, (tm, tm), 0)
    tc = lax.broadcasted_iota(jnp.int32, (tm, tm), 1)
    before = (tr < tc).astype(BF16)
    prefix = jnp.dot(chosen.astype(BF16), before, preferred_element_type=F32) + carry_ref[...]
    rank_ref[...] = jnp.concatenate(
        [jnp.sum(o * prefix, axis=0, keepdims=True) for o in onehots], axis=0).astype(jnp.int32)
    before_ref[...] = prefix.astype(jnp.int32)
    carry_ref[...] = carry_ref[...] + jnp.sum(chosen, axis=1, keepdims=True)
    cnt_ref[...] = jnp.broadcast_to(carry_ref[...], cnt_ref.shape)


def _mix(hcv, mhalo, sb, x, conv_w, conv_b, ln_g, ln_b, w_out_bf, g2, wr_t, br, tm, chunk):
    b, s, d = x.shape
    cw = hcv.shape[-1]
    t = b * s
    per = s // tm
    tile = lambda b, i: (b, i, 0)
    fixed = lambda b, i: (0, 0)
    tok = lambda b, i: (0, b * per + i)
    return pl.pallas_call(
        functools.partial(_mix_kernel, tm=tm, chunk=chunk),
        grid=(b, per),
        in_specs=[
            pl.BlockSpec((1, tm, cw), tile),
            pl.BlockSpec((1, HALO, cw), lambda b, i: (b, jnp.maximum(i * (tm // HALO) - 1, 0), 0)),
            pl.BlockSpec((HALO, cw), fixed),
            pl.BlockSpec((1, tm, SB_WIDTH), tile),
            pl.BlockSpec((1, tm, d), tile),
            pl.BlockSpec((CONV_KERNEL, cw), fixed),
            pl.BlockSpec((1, cw), fixed),
            pl.BlockSpec((1, cw), fixed),
            pl.BlockSpec((1, cw), fixed),
            pl.BlockSpec((SB_WIDTH + cw, d), fixed),
            pl.BlockSpec((1, d), fixed),
            pl.BlockSpec((2 * N_EXPERTS, d), fixed),
            pl.BlockSpec((N_EXPERTS, 1), fixed),
        ],
        out_specs=[
            pl.BlockSpec((1, tm, d), tile),
            pl.BlockSpec((1, tm, d), tile),
            pl.BlockSpec((TOP_K, tm), tok),
            pl.BlockSpec((TOP_K, tm), tok),
            pl.BlockSpec((TOP_K, tm), tok),
            pl.BlockSpec((N_EXPERTS, tm), tok),
            pl.BlockSpec((N_EXPERTS, LANES), fixed),
        ],
        out_shape=[
            jax.ShapeDtypeStruct((b, s, d), F32),
            jax.ShapeDtypeStruct((b, s, d), F32),
            jax.ShapeDtypeStruct((TOP_K, t), jnp.int32),
            jax.ShapeDtypeStruct((TOP_K, t), F32),
            jax.ShapeDtypeStruct((TOP_K, t), jnp.int32),
            jax.ShapeDtypeStruct((N_EXPERTS, t), jnp.int32),
            jax.ShapeDtypeStruct((N_EXPERTS, LANES), F32),
        ],
        scratch_shapes=[
            pltpu.VMEM((tm + HALO, cw), F32),
            pltpu.VMEM((chunk, cw), F32),
            pltpu.VMEM((tm, cw), BF16),
            pltpu.VMEM((N_EXPERTS, 1), F32),
        ],
        compiler_params=_params(2),
        name="mix_router",
    )(hcv, hcv, mhalo, sb, x, conv_w, conv_b, ln_g, ln_b, w_out_bf, g2, wr_t, br)


def _row_tiles(ref, row, n=1):
    return ref.at[pl.ds(pl.multiple_of(row * SUBLANES, SUBLANES), n * SUBLANES)]


def _wait_rows(rows_ref, n, sem):
    pltpu.make_async_copy(_row_tiles(rows_ref, 0, n), _row_tiles(rows_ref, 0, n), sem).wait()


def _dispatch_kernel(padlo_ref, padlen_ref, nused_ref, pos_ref, hn_ref, rows_ref, src_ref, zeros_ref, sem, zsem,
                     *, tm, n_blocks, n_pad):
    i = pl.program_id(0)

    @pl.when(i == 0)
    def _():
        zeros_ref[...] = jnp.zeros_like(zeros_ref)
        for e in range(N_EXPERTS):
            lo, ln = padlo_ref[e], padlen_ref[e]
            for bit in range(EXPERT_ROWS.bit_length() - 1):
                size = 1 << bit

                @pl.when((ln >> bit) & 1 == 1)
                def _():
                    start = lo + ((ln >> (bit + 1)) << (bit + 1))
                    pltpu.make_async_copy(_row_tiles(zeros_ref, 0, size), _row_tiles(rows_ref, start, size),
                                          zsem).start()

        def tail(j, _):
            pltpu.make_async_copy(zeros_ref, _row_tiles(rows_ref, j * EXPERT_ROWS, EXPERT_ROWS), zsem).start()
            return 0

        lax.fori_loop(nused_ref[0], n_blocks, tail, 0)

    groups = hn_ref.shape[1] // LANES
    for c in range(groups):
        src_ref[pl.ds(c, tm, stride=groups), :] = hn_ref[:, c * LANES:(c + 1) * LANES]

    def issue(t, _):
        for k in range(TOP_K):
            pltpu.make_async_copy(_row_tiles(src_ref, t), _row_tiles(rows_ref, pos_ref[t * TOP_K + k]),
                                  sem).start(priority=k % 2)
        return 0

    lax.fori_loop(0, tm, issue, 0, unroll=2)
    _wait_rows(rows_ref, TOP_K * tm, sem)

    @pl.when(i == pl.num_programs(0) - 1)
    def _():
        _wait_rows(rows_ref, n_pad, zsem)


def _dispatch(pad_lo, pad_len, n_used, pos, hn, n_blocks, tm):
    t, d = hn.shape
    n_rows = n_blocks * EXPERT_ROWS
    grid_spec = pltpu.PrefetchScalarGridSpec(
        num_scalar_prefetch=3,
        grid=(t // tm,),
        in_specs=[
            pl.BlockSpec((TOP_K * tm,), lambda i, *_: (i,), memory_space=pltpu.SMEM),
            pl.BlockSpec((tm, d), lambda i, *_: (i, 0)),
        ],
        out_specs=pl.BlockSpec(memory_space=pl.ANY),
        scratch_shapes=[pltpu.VMEM((tm * SUBLANES, LANES), hn.dtype),
                        pltpu.VMEM((EXPERT_ROWS * SUBLANES, LANES), hn.dtype),
                        pltpu.SemaphoreType.DMA(()), pltpu.SemaphoreType.DMA(())],
    )
    assert d == SUBLANES * LANES, d
    return pl.pallas_call(
        functools.partial(_dispatch_kernel, tm=tm, n_blocks=n_blocks, n_pad=n_rows - t * TOP_K),
        grid_spec=grid_spec,
        out_shape=jax.ShapeDtypeStruct((n_rows * SUBLANES, LANES), hn.dtype),
        compiler_params=_params(1),
        name="dispatch",
    )(pad_lo, pad_len, n_used, pos, hn)


def _expert_kernel(be_ref, nused_ref, x_ref, wu_ref, bu_ref, wd_ref, bd_ref, y_ref,
                   wu_bf, wd_bf, *, cast_rows):
    j = pl.program_id(0)
    de = wd_ref.shape[1]

    @pl.when(j >= nused_ref[0])
    def _():
        y_ref[...] = jnp.zeros_like(y_ref)

    @pl.when(j < nused_ref[0])
    def _():
        prev = be_ref[jnp.maximum(j - 1, 0)]

        @pl.when(jnp.logical_or(j == 0, be_ref[j] != prev))
        def _():
            def cast(c, _):
                r0 = pl.multiple_of(c * cast_rows, cast_rows)
                wu_bf[pl.ds(r0, cast_rows), :] = wu_ref[0, pl.ds(r0, cast_rows), :].astype(BF16)
                wd_bf[pl.ds(r0, cast_rows), :] = wd_ref[0, pl.ds(r0, cast_rows), :].astype(BF16)
                return 0

            lax.fori_loop(0, wu_ref.shape[1] // cast_rows, cast, 0)

        x = jnp.concatenate([x_ref[pl.ds(c, EXPERT_ROWS, stride=SUBLANES), :] for c in range(SUBLANES)], axis=1)
        up = jnp.dot(x.astype(BF16), wu_bf[...], preferred_element_type=F32) + bu_ref[0]
        glu = jnp.minimum(up[:, :de], SWIGLU_LIMIT)
        lin = jnp.clip(up[:, de:], -SWIGLU_LIMIT, SWIGLU_LIMIT)
        act = glu * jax.nn.sigmoid(SWIGLU_ALPHA * glu) * (lin + 1.0)
        y = jnp.dot(act.astype(BF16), wd_bf[...], preferred_element_type=F32) + bd_ref[0]
        y_ref[...] = _pack_halves(y)


def _pack_halves(y):
    half = y.shape[1] // 2
    bits = lambda v: lax.bitcast_convert_type(v.astype(BF16).astype(F32), jnp.uint32)
    return (bits(y[:, half:]) & jnp.uint32(0xFFFF0000)) | (bits(y[:, :half]) >> 16)


def _unpack_halves(w):
    lo = lax.bitcast_convert_type(w << 16, F32).astype(BF16)
    hi = lax.bitcast_convert_type(w & jnp.uint32(0xFFFF0000), F32).astype(BF16)
    return lo, hi


def _experts(block_expert, n_used, x_rows, w_up, b_up, w_down, b_down):
    n_rows = x_rows.shape[0] // SUBLANES
    n_blocks = n_rows // EXPERT_ROWS + 1
    ne, d, up_cols = w_up.shape
    de = w_down.shape[1]
    blk = lambda j, be, nu: (jnp.minimum(j, nu[0] - 1), 0)
    wsel = lambda j, be, nu: (be[jnp.minimum(j, nu[0] - 1)], 0, 0)
    grid_spec = pltpu.PrefetchScalarGridSpec(
        num_scalar_prefetch=2,
        grid=(n_blocks,),
        in_specs=[
            pl.BlockSpec((EXPERT_ROWS * SUBLANES, LANES), blk),
            pl.BlockSpec((1, d, up_cols), wsel),
            pl.BlockSpec((1, 1, up_cols), wsel),
            pl.BlockSpec((1, de, d), wsel),
            pl.BlockSpec((1, 1, d), wsel),
        ],
        out_specs=pl.BlockSpec((EXPERT_ROWS, d // 2), lambda j, be, nu: (j, 0)),
        scratch_shapes=[pltpu.VMEM((d, up_cols), BF16), pltpu.VMEM((de, d), BF16)],
    )
    return pl.pallas_call(
        functools.partial(_expert_kernel, cast_rows=64),
        grid_spec=grid_spec,
        out_shape=jax.ShapeDtypeStruct((n_blocks * EXPERT_ROWS, d // 2), jnp.uint32),
        compiler_params=_params(1),
        name="experts",
    )(block_expert, n_used, x_rows, w_up, b_up.reshape(ne, 1, up_cols),
      w_down, b_down.reshape(ne, 1, d))


def _combine_kernel(rstart_ref, rcnt_ref, eid_ref, pos_ref, gate_ref, xn_ref, rows_ref, o_ref, buf, sems,
                    *, tm, cap):
    i = pl.program_id(0)
    n = pl.num_programs(0)
    slot = lax.rem(i, 2)
    half = o_ref.shape[1] // 2

    def windows(tile):
        out, off = [], jnp.int32(0)
        for e in range(N_EXPERTS):
            start = rstart_ref[tile * N_EXPERTS + e]
            cnt = rcnt_ref[tile * N_EXPERTS + e]
            first = (start >> 3) << 3
            pieces = jnp.where(cnt > 0, (start - first + cnt + PIECE - 1) // PIECE, 0)
            out.append((first, pieces, off))
            off = off + pieces * PIECE
        return out, off

    def fetch(tile, s):
        for first, pieces, off in windows(tile)[0]:
            def body(p, _):
                src = rows_ref.at[pl.ds(pl.multiple_of(first + p * PIECE, SUBLANES), PIECE)]
                dst = buf.at[s, pl.ds(pl.multiple_of(off + p * PIECE, PIECE), PIECE)]
                pltpu.make_async_copy(src, dst, sems.at[s]).start()
                return 0

            lax.fori_loop(0, pieces, body, 0)

    @pl.when(i == 0)
    def _():
        buf[...] = jnp.zeros_like(buf)
        fetch(0, 0)

    @pl.when(i + 1 < n)
    def _():
        fetch(i + 1, 1 - slot)

    wins, total = windows(i)
    eid = eid_ref[...]
    staged = pos_ref[...]
    shift = jnp.zeros_like(staged)
    for e, (first, _, off) in enumerate(wins):
        shift = jnp.where(eid == e, off - first, shift)
    staged = staged + shift
    gates = gate_ref[...]

    @pl.when(total > 0)
    def _():
        rows = pl.ds(0, pl.multiple_of(total, PIECE))
        pltpu.make_async_copy(rows_ref.at[rows], buf.at[slot, rows], sems.at[slot]).wait()

    o_ref[...] = xn_ref[...]

    def chunk(c, _):
        c0 = pl.multiple_of(c * COMBINE_CHUNK, COMBINE_CHUNK)
        col = lax.broadcasted_iota(jnp.int32, (tm, COMBINE_CHUNK), 1) + c0
        sel = jnp.zeros((tm, COMBINE_CHUNK), F32)
        for k in range(TOP_K):
            sel = jnp.where(staged[:, k:k + 1] == col, gates[:, k:k + 1], sel)
        sel = sel.astype(BF16)
        lo, hi = _unpack_halves(buf[slot, pl.ds(c0, COMBINE_CHUNK), :])
        o_ref[:, :half] += jnp.dot(sel, lo, preferred_element_type=F32)
        o_ref[:, half:] += jnp.dot(sel, hi, preferred_element_type=F32)
        return 0

    lax.fori_loop(0, (total + COMBINE_CHUNK - 1) // COMBINE_CHUNK, chunk, 0)


def _combine(rstart, rcnt, eid_t, pos_t, gates_t, xn, y_rows, tm):
    t, d = xn.shape
    n = t // tm
    cap = -(-(TOP_K * tm + 2 * PIECE * N_EXPERTS) // COMBINE_CHUNK) * COMBINE_CHUNK
    tok = lambda i, *_: (i, 0)
    grid_spec = pltpu.PrefetchScalarGridSpec(
        num_scalar_prefetch=2,
        grid=(n,),
        in_specs=[
            pl.BlockSpec((tm, TOP_K), tok),
            pl.BlockSpec((tm, TOP_K), tok),
            pl.BlockSpec((tm, TOP_K), tok),
            pl.BlockSpec((tm, d), tok),
            pl.BlockSpec(memory_space=pl.ANY),
        ],
        out_specs=pl.BlockSpec((tm, d), tok),
        scratch_shapes=[pltpu.VMEM((2, cap, d // 2), jnp.uint32), pltpu.SemaphoreType.DMA((2,))],
    )
    return pl.pallas_call(
        functools.partial(_combine_kernel, tm=tm, cap=cap),
        grid_spec=grid_spec,
        out_shape=jax.ShapeDtypeStruct((t, d), F32),
        compiler_params=_params(1),
        name="combine",
    )(rstart, rcnt, eid_t, pos_t, gates_t, xn, y_rows)


def _tile(n, want):
    t = min(n, want)
    assert n % t == 0, (n, t)
    return t


def kernel(x, meta_tokens, norm1_g, w_in, q_norm_g, k_norm_g, conv_w, conv_b, conv_ln_g, conv_ln_b,
           attn_out_g, w_out, norm2_g, w_router, b_router, w_up, b_up, w_down, b_down):
    assert norm1_g.shape[0] == 1, "single layer: meta-token rows are only keys/values and conv context"
    b, s, d = x.shape
    t = b * s
    cw = conv_w.shape[-1]

    g1 = norm1_g[0][None, :]
    w_in_bf = w_in[0].astype(BF16)
    qg = jnp.tile(q_norm_g[0], SB_HEADS)[None, :]
    kg = jnp.tile(k_norm_g[0], SB_HEADS)[None, :]

    q, k, v, hcv = _inproj(x.reshape(t, d), g1, w_in_bf, qg, kg, _tile(t, 512))
    _, km, vm, hm = _inproj(meta_tokens, g1, w_in_bf, qg, kg, N_META)
    pad = ((0, LANES - N_META), (0, 0))
    sb = _attention(q.reshape(b, s, SB_WIDTH), k.reshape(b, s, SB_WIDTH), v.reshape(b, s, SB_WIDTH),
                    jnp.pad(km, pad), jnp.pad(vm, pad), attn_out_g[0].reshape(1, SB_WIDTH), _tile(s, 256))

    mhalo = jnp.concatenate([jnp.zeros((HALO - N_META, cw), F32), hm], axis=0)
    tm = _tile(s, 512)
    wr_t = w_router[0].T
    wr_hi = wr_t.astype(BF16)
    wr_lo = (wr_t - wr_hi.astype(F32)).astype(BF16)
    xn, hn, eid, gates, rank, before, counts = _mix(
        hcv.reshape(b, s, cw), mhalo, sb, x, conv_w[0], conv_b[0][None, :], conv_ln_g[0][None, :],
        conv_ln_b[0][None, :], w_out[0].astype(BF16), norm2_g[0][None, :],
        jnp.concatenate([wr_hi, wr_lo], axis=0), b_router[0][:, None], tm, _tile(tm, 64))

    counts = counts[:, 0].astype(jnp.int32)
    padded = (counts + EXPERT_ROWS - 1) // EXPERT_ROWS * EXPERT_ROWS
    pad_end = jnp.cumsum(padded)
    pad_start = pad_end - padded
    n_blocks = t * TOP_K // EXPERT_ROWS + N_EXPERTS
    block_row = jnp.arange(n_blocks, dtype=jnp.int32) * EXPERT_ROWS
    block_expert = jnp.minimum(jnp.sum(block_row[:, None] >= pad_end[None, :], axis=1), N_EXPERTS - 1).astype(jnp.int32)
    n_used = (pad_end[-1:] // EXPERT_ROWS).astype(jnp.int32)
    onehot = eid[:, :, None] == jnp.arange(N_EXPERTS, dtype=jnp.int32)
    pos = rank + jnp.sum(jnp.where(onehot, pad_start, 0), axis=-1)

    pos_t = pos.T
    x_rows = _dispatch(pad_start + counts, padded - counts, n_used, pos_t.reshape(-1), hn.reshape(t, d),
                       n_blocks, _tile(t, 512))
    y_rows = _experts(block_expert, n_used, x_rows, w_up[0], b_up[0], w_down[0], b_down[0])
    tc = _tile(t, 256)
    run_start = pad_start[:, None] + before[:, ::tc]
    run_end = jnp.concatenate([run_start[:, 1:], (pad_start + counts)[:, None]], axis=1)
    out = _combine(run_start.T.reshape(-1), (run_end - run_start).T.reshape(-1), eid.T, pos_t, gates.T,
                   xn.reshape(t, d), y_rows, tc)
    return out.reshape(b, s, d)
```

```python
import functools

import jax
import jax.numpy as jnp
from jax import lax
from jax.experimental import pallas as pl
from jax.experimental.pallas import tpu as pltpu

N_META = 16
SB_HEADS = 8
SB_HEAD_DIM = 64
SB_WIDTH = SB_HEADS * SB_HEAD_DIM
CONV_KERNEL = 31
N_EXPERTS = 32
TOP_K = 4
SWIGLU_LIMIT = 7.0
SWIGLU_ALPHA = 1.702
EPS = 1e-6
F32_EXP2_UNDERFLOW = -150.0
LOG2_E = 1.4426950408889634

LANES = 128
SUBLANES = 8
PIECE = 16
COMBINE_CHUNK = 512
HALO = 32
EXPERT_ROWS = 512
VMEM_LIMIT = 56 * 1024 * 1024

F32 = jnp.float32
BF16 = jnp.bfloat16


def _params(n_axes, vmem=VMEM_LIMIT):
    return pltpu.CompilerParams(dimension_semantics=("arbitrary",) * n_axes, vmem_limit_bytes=vmem)


def _inproj_kernel(x_ref, g1_ref, w_ref, qg_ref, kg_ref, q_ref, k_ref, v_ref, h_ref):
    x = x_ref[...]
    ms = jnp.mean(x * x, axis=-1, keepdims=True)
    n = (x * lax.rsqrt(ms + EPS) * g1_ref[...]).astype(BF16)
    lo = lax.broadcasted_iota(jnp.int32, (1, LANES), 1) < SB_HEAD_DIM

    def proj(c0, c1):
        return jnp.dot(n, w_ref[:, c0:c1], preferred_element_type=F32)

    def head_norm(acc, g_ref, out_ref, scale):
        for c in range(SB_WIDTH // LANES):
            sl = slice(c * LANES, (c + 1) * LANES)
            a = acc[:, sl]
            sq = a * a
            s_lo = jnp.sum(jnp.where(lo, sq, 0.0), axis=-1, keepdims=True)
            s_hi = jnp.sum(jnp.where(lo, 0.0, sq), axis=-1, keepdims=True)
            r = lax.rsqrt(jnp.where(lo, s_lo, s_hi) * (1.0 / SB_HEAD_DIM) + EPS)
            out_ref[:, sl] = (a * r * (g_ref[:, sl] * scale)).astype(BF16)

    head_norm(proj(0, SB_WIDTH), qg_ref, q_ref, LOG2_E * SB_HEAD_DIM ** -0.5)
    head_norm(proj(SB_WIDTH, 2 * SB_WIDTH), kg_ref, k_ref, 1.0)
    v_ref[...] = proj(2 * SB_WIDTH, 3 * SB_WIDTH).astype(BF16)
    cw = (w_ref.shape[1] - 3 * SB_WIDTH) // 2
    val = proj(3 * SB_WIDTH, 3 * SB_WIDTH + cw)
    gate = proj(3 * SB_WIDTH + cw, 3 * SB_WIDTH + 2 * cw)
    h_ref[...] = val * jax.nn.sigmoid(gate)


def _inproj(x2, g1, w_in_bf, qg, kg, tm):
    t, d = x2.shape
    cols = w_in_bf.shape[1]
    cw = (cols - 3 * SB_WIDTH) // 2
    row = lambda i: (i, 0)
    fixed = lambda i: (0, 0)
    return pl.pallas_call(
        _inproj_kernel,
        grid=(t // tm,),
        in_specs=[
            pl.BlockSpec((tm, d), row),
            pl.BlockSpec((1, d), fixed),
            pl.BlockSpec((d, cols), fixed),
            pl.BlockSpec((1, SB_WIDTH), fixed),
            pl.BlockSpec((1, SB_WIDTH), fixed),
        ],
        out_specs=[
            pl.BlockSpec((tm, SB_WIDTH), row),
            pl.BlockSpec((tm, SB_WIDTH), row),
            pl.BlockSpec((tm, SB_WIDTH), row),
            pl.BlockSpec((tm, cw), row),
        ],
        out_shape=[
            jax.ShapeDtypeStruct((t, SB_WIDTH), BF16),
            jax.ShapeDtypeStruct((t, SB_WIDTH), BF16),
            jax.ShapeDtypeStruct((t, SB_WIDTH), BF16),
            jax.ShapeDtypeStruct((t, cw), F32),
        ],
        compiler_params=_params(1),
        name="inproj",
    )(x2, g1, w_in_bf, qg, kg)


def _attn_kernel(q_ref, k_ref, v_ref, km_ref, vm_ref, g_ref, o_ref, *, tq):
    i = pl.program_id(2)
    lane = lax.broadcasted_iota(jnp.int32, (1, LANES), 1)
    lo = lane < SB_HEAD_DIM
    q2 = q_ref[0]
    zero_bf = jnp.zeros((), BF16)
    qs = jnp.concatenate([jnp.where(lo, q2, zero_bf), jnp.where(lo, zero_bf, q2)], axis=0)
    row = lax.broadcasted_iota(jnp.int32, (2 * tq, tq), 0)
    col = lax.broadcasted_iota(jnp.int32, (2 * tq, tq), 1)
    causal = col < jnp.where(row >= tq, row - tq, row)
    srow = lax.broadcasted_iota(jnp.int32, (tq, tq), 0)
    scol = lax.broadcasted_iota(jnp.int32, (tq, tq), 1)
    neg_suffix = jnp.where(srow > scol, -1.0, 0.0).astype(BF16)

    def block(kb, vb, acc, r, mask, neg_suffix_m):
        z = lax.dot_general(qs, kb, (((1,), (1,)), ((), ())), preferred_element_type=F32)
        m = jnp.minimum(z, 0.0)
        p = jnp.maximum(z, 0.0)
        l1p = jnp.log2(1.0 + jnp.exp2(m - p))
        log_beta = m - l1p
        neg_keep = p + l1p
        if mask is not None:
            neg_keep = jnp.where(mask, neg_keep, 0.0)
        later = jnp.dot(neg_keep.astype(BF16), neg_suffix_m, preferred_element_type=F32)
        a = jnp.exp2(log_beta + later + r)
        if mask is not None:
            a = jnp.where(mask, a, 0.0)
        res = jnp.dot(a.astype(BF16), vb, preferred_element_type=F32)
        acc = acc + jnp.where(lo, res[:tq], res[tq:])
        return acc, r - jnp.sum(neg_keep, axis=-1, keepdims=True)

    def kv_block(j):
        start = pl.multiple_of(j * tq, tq)
        return k_ref[0, pl.ds(start, tq), :], v_ref[0, pl.ds(start, tq), :]

    def live(r):
        return jnp.max(r) > F32_EXP2_UNDERFLOW

    acc0 = jnp.zeros((tq, LANES), F32)
    r0 = jnp.zeros((2 * tq, 1), F32)

    def diagonal(_):
        return block(*kv_block(i), acc0, r0, causal, neg_suffix)

    def diagonal_and_previous(_):
        acc, r = block(*kv_block(i), acc0, r0, causal, neg_suffix)
        return block(*kv_block(i - 1), acc, r, None, neg_suffix)

    acc, r = lax.cond(i > 0, diagonal_and_previous, diagonal, 0)

    def cond(c):
        return jnp.logical_and(c[0] < i, c[3])

    def body(c):
        acc, r = block(*kv_block(i - 1 - c[0]), c[1], c[2], None, neg_suffix)
        return c[0] + 1, acc, r, live(r)

    _, acc, r, alive = lax.while_loop(cond, body, (jnp.int32(1), acc, r, live(r)))

    def meta_block(acc):
        return block(km_ref[...], vm_ref[...], acc, r, lane < N_META, neg_suffix[:LANES, :LANES])[0]

    acc = lax.cond(alive, meta_block, lambda acc: acc, acc)

    sq = acc * acc
    s_lo = jnp.sum(jnp.where(lo, sq, 0.0), axis=-1, keepdims=True)
    s_hi = jnp.sum(jnp.where(lo, 0.0, sq), axis=-1, keepdims=True)
    rn = lax.rsqrt(jnp.where(lo, s_lo, s_hi) * (1.0 / SB_HEAD_DIM) + EPS)
    o_ref[0] = (acc * rn * g_ref[...]).astype(BF16)


def _attention(q, k, v, km, vm, og, tq):
    b, s, _ = q.shape
    n_pairs = SB_WIDTH // LANES
    return pl.pallas_call(
        functools.partial(_attn_kernel, tq=tq),
        grid=(b, n_pairs, s // tq),
        in_specs=[
            pl.BlockSpec((1, tq, LANES), lambda b, p, i: (b, i, p)),
            pl.BlockSpec((1, s, LANES), lambda b, p, i: (b, 0, p)),
            pl.BlockSpec((1, s, LANES), lambda b, p, i: (b, 0, p)),
            pl.BlockSpec((LANES, LANES), lambda b, p, i: (0, p)),
            pl.BlockSpec((LANES, LANES), lambda b, p, i: (0, p)),
            pl.BlockSpec((1, LANES), lambda b, p, i: (0, p)),
        ],
        out_specs=pl.BlockSpec((1, tq, LANES), lambda b, p, i: (b, i, p)),
        out_shape=jax.ShapeDtypeStruct((b, s, SB_WIDTH), BF16),
        compiler_params=_params(3),
        name="attention",
    )(q, k, v, km, vm, og)


def _mix_kernel(h_ref, halo_ref, mhalo_ref, sb_ref, x_ref, cw_ref, cb_ref, lg_ref, lb_ref, wo_ref,
                g2_ref, wr_ref, br_ref,
                xn_ref, hn_ref, eid_ref, gate_ref, rank_ref, before_ref, cnt_ref,
                win_ref, conv_ref, cv_ref, carry_ref, *, tm, chunk):
    b = pl.program_id(0)
    i = pl.program_id(1)

    @pl.when(jnp.logical_and(b == 0, i == 0))
    def _():
        carry_ref[...] = jnp.zeros_like(carry_ref)

    @pl.when(i == 0)
    def _():
        win_ref[0:HALO, :] = mhalo_ref[...]

    @pl.when(i > 0)
    def _():
        win_ref[0:HALO, :] = halo_ref[0]

    win_ref[HALO:, :] = h_ref[0]

    first_tap = HALO - (CONV_KERNEL - 1)

    def conv_chunk(c, _):
        r0 = pl.multiple_of(c * chunk, chunk)
        rows = chunk + HALO
        for lt in range(cb_ref.shape[1] // LANES):
            sl = slice(lt * LANES, (lt + 1) * LANES)
            window = win_ref[pl.ds(r0, rows), sl]
            acc = jnp.broadcast_to(cb_ref[:, sl], (chunk, LANES))
            for b in range(SUBLANES):
                shifted = window if b == 0 else pltpu.roll(window, shift=rows - b, axis=0)
                for a in range(HALO // SUBLANES + 1):
                    j = SUBLANES * a + b - first_tap
                    if 0 <= j < CONV_KERNEL:
                        acc = acc + cw_ref[j:j + 1, sl] * shifted[SUBLANES * a:SUBLANES * a + chunk, :]
            conv_ref[:, sl] = acc
        acc = conv_ref[...]
        mu = jnp.mean(acc, axis=-1, keepdims=True)
        cen = acc - mu
        var = jnp.mean(cen * cen, axis=-1, keepdims=True)
        y = cen * lax.rsqrt(var + EPS) * lg_ref[...] + lb_ref[...]
        cv_ref[pl.ds(r0, chunk), :] = (y * jax.nn.sigmoid(y)).astype(BF16)
        return 0

    lax.fori_loop(0, tm // chunk, conv_chunk, 0)

    mixed = jnp.dot(sb_ref[0], wo_ref[0:SB_WIDTH, :], preferred_element_type=F32)
    mixed = mixed + jnp.dot(cv_ref[...], wo_ref[SB_WIDTH:, :], preferred_element_type=F32)
    xn = x_ref[0] + mixed
    xn_ref[0] = xn
    ms = jnp.mean(xn * xn, axis=-1, keepdims=True)
    hn = xn * lax.rsqrt(ms + EPS) * g2_ref[...]
    hn_ref[0] = hn

    nt = (((1,), (1,)), ((), ()))
    hn_hi = hn.astype(BF16)
    hn_lo = (hn - hn_hi.astype(F32)).astype(BF16)
    both = lax.dot_general(wr_ref[...], hn_hi, nt, preferred_element_type=F32)
    logits = (both[:N_EXPERTS] + both[N_EXPERTS:]
              + lax.dot_general(wr_ref[0:N_EXPERTS, :], hn_lo, nt, preferred_element_type=F32) + br_ref[...])
    eidx = lax.broadcasted_iota(jnp.int32, logits.shape, 0)
    vals, idxs = [], []
    for _ in range(TOP_K):
        m = jnp.max(logits, axis=0, keepdims=True)
        sel = jnp.min(jnp.where(logits == m, eidx, N_EXPERTS), axis=0, keepdims=True)
        vals.append(m)
        idxs.append(sel)
        logits = jnp.where(eidx == sel, -jnp.inf, logits)
    exps = [jnp.exp(v - vals[0]) for v in vals]
    denom = exps[0] + exps[1] + exps[2] + exps[3]
    gate_ref[...] = jnp.concatenate([e / denom for e in exps], axis=0)
    eid_ref[...] = jnp.concatenate(idxs, axis=0)

    onehots = [(eidx == s).astype(F32) for s in idxs]
    chosen = onehots[0] + onehots[1] + onehots[2] + onehots[3]
    tr = lax.broadcasted_iota(jnp.int32, (tm, tm), 0)
    tc = lax.broadcasted_iota(jnp.int32, (tm, tm), 1)
    before = (tr < tc).astype(BF16)
    prefix = jnp.dot(chosen.astype(BF16), before, preferred_element_type=F32) + carry_ref[...]
    rank_ref[...] = jnp.concatenate(
        [jnp.sum(o * prefix, axis=0, keepdims=True) for o in onehots], axis=0).astype(jnp.int32)
    before_ref[...] = prefix.astype(jnp.int32)
    carry_ref[...] = carry_ref[...] + jnp.sum(chosen, axis=1, keepdims=True)
    cnt_ref[...] = jnp.broadcast_to(carry_ref[...], cnt_ref.shape)


def _mix(hcv, mhalo, sb, x, conv_w, conv_b, ln_g, ln_b, w_out_bf, g2, wr_t, br, tm, chunk):
    b, s, d = x.shape
    cw = hcv.shape[-1]
    t = b * s
    per = s // tm
    tile = lambda b, i: (b, i, 0)
    fixed = lambda b, i: (0, 0)
    tok = lambda b, i: (0, b * per + i)
    return pl.pallas_call(
        functools.partial(_mix_kernel, tm=tm, chunk=chunk),
        grid=(b, per),
        in_specs=[
            pl.BlockSpec((1, tm, cw), tile),
            pl.BlockSpec((1, HALO, cw), lambda b, i: (b, jnp.maximum(i * (tm // HALO) - 1, 0), 0)),
            pl.BlockSpec((HALO, cw), fixed),
            pl.BlockSpec((1, tm, SB_WIDTH), tile),
            pl.BlockSpec((1, tm, d), tile),
            pl.BlockSpec((CONV_KERNEL, cw), fixed),
            pl.BlockSpec((1, cw), fixed),
            pl.BlockSpec((1, cw), fixed),
            pl.BlockSpec((1, cw), fixed),
            pl.BlockSpec((SB_WIDTH + cw, d), fixed),
            pl.BlockSpec((1, d), fixed),
            pl.BlockSpec((2 * N_EXPERTS, d), fixed),
            pl.BlockSpec((N_EXPERTS, 1), fixed),
        ],
        out_specs=[
            pl.BlockSpec((1, tm, d), tile),
            pl.BlockSpec((1, tm, d), tile),
            pl.BlockSpec((TOP_K, tm), tok),
            pl.BlockSpec((TOP_K, tm), tok),
            pl.BlockSpec((TOP_K, tm), tok),
            pl.BlockSpec((N_EXPERTS, tm), tok),
            pl.BlockSpec((N_EXPERTS, LANES), fixed),
        ],
        out_shape=[
            jax.ShapeDtypeStruct((b, s, d), F32),
            jax.ShapeDtypeStruct((b, s, d), F32),
            jax.ShapeDtypeStruct((TOP_K, t), jnp.int32),
            jax.ShapeDtypeStruct((TOP_K, t), F32),
            jax.ShapeDtypeStruct((TOP_K, t), jnp.int32),
            jax.ShapeDtypeStruct((N_EXPERTS, t), jnp.int32),
            jax.ShapeDtypeStruct((N_EXPERTS, LANES), F32),
        ],
        scratch_shapes=[
            pltpu.VMEM((tm + HALO, cw), F32),
            pltpu.VMEM((chunk, cw), F32),
            pltpu.VMEM((tm, cw), BF16),
            pltpu.VMEM((N_EXPERTS, 1), F32),
        ],
        compiler_params=_params(2),
        name="mix_router",
    )(hcv, hcv, mhalo, sb, x, conv_w, conv_b, ln_g, ln_b, w_out_bf, g2, wr_t, br)


def _row_tiles(ref, row, n=1):
    return ref.at[pl.ds(pl.multiple_of(row * SUBLANES, SUBLANES), n * SUBLANES)]


def _wait_rows(rows_ref, n, sem):
    pltpu.make_async_copy(_row_tiles(rows_ref, 0, n), _row_tiles(rows_ref, 0, n), sem).wait()


def _dispatch_kernel(padlo_ref, padlen_ref, nused_ref, pos_ref, hn_ref, rows_ref, src_ref, zeros_ref, sem, zsem,
                     *, tm, n_blocks, n_pad):
    i = pl.program_id(0)

    @pl.when(i == 0)
    def _():
        zeros_ref[...] = jnp.zeros_like(zeros_ref)
        for e in range(N_EXPERTS):
            lo, ln = padlo_ref[e], padlen_ref[e]
            for bit in range(EXPERT_ROWS.bit_length() - 1):
                size = 1 << bit

                @pl.when((ln >> bit) & 1 == 1)
                def _():
                    start = lo + ((ln >> (bit + 1)) << (bit + 1))
                    pltpu.make_async_copy(_row_tiles(zeros_ref, 0, size), _row_tiles(rows_ref, start, size),
                                          zsem).start()

        def tail(j, _):
            pltpu.make_async_copy(zeros_ref, _row_tiles(rows_ref, j * EXPERT_ROWS, EXPERT_ROWS), zsem).start()
            return 0

        lax.fori_loop(nused_ref[0], n_blocks, tail, 0)

    groups = hn_ref.shape[1] // LANES
    for c in range(groups):
        src_ref[pl.ds(c, tm, stride=groups), :] = hn_ref[:, c * LANES:(c + 1) * LANES]

    def issue(t, _):
        for k in range(TOP_K):
            pltpu.make_async_copy(_row_tiles(src_ref, t), _row_tiles(rows_ref, pos_ref[t * TOP_K + k]),
                                  sem).start(priority=k % 2)
        return 0

    lax.fori_loop(0, tm, issue, 0, unroll=2)
    _wait_rows(rows_ref, TOP_K * tm, sem)

    @pl.when(i == pl.num_programs(0) - 1)
    def _():
        _wait_rows(rows_ref, n_pad, zsem)


def _dispatch(pad_lo, pad_len, n_used, pos, hn, n_blocks, tm):
    t, d = hn.shape
    n_rows = n_blocks * EXPERT_ROWS
    grid_spec = pltpu.PrefetchScalarGridSpec(
        num_scalar_prefetch=3,
        grid=(t // tm,),
        in_specs=[
            pl.BlockSpec((TOP_K * tm,), lambda i, *_: (i,), memory_space=pltpu.SMEM),
            pl.BlockSpec((tm, d), lambda i, *_: (i, 0)),
        ],
        out_specs=pl.BlockSpec(memory_space=pl.ANY),
        scratch_shapes=[pltpu.VMEM((tm * SUBLANES, LANES), hn.dtype),
                        pltpu.VMEM((EXPERT_ROWS * SUBLANES, LANES), hn.dtype),
                        pltpu.SemaphoreType.DMA(()), pltpu.SemaphoreType.DMA(())],
    )
    assert d == SUBLANES * LANES, d
    return pl.pallas_call(
        functools.partial(_dispatch_kernel, tm=tm, n_blocks=n_blocks, n_pad=n_rows - t * TOP_K),
        grid_spec=grid_spec,
        out_shape=jax.ShapeDtypeStruct((n_rows * SUBLANES, LANES), hn.dtype),
        compiler_params=_params(1),
        name="dispatch",
    )(pad_lo, pad_len, n_used, pos, hn)


def _expert_kernel(be_ref, nused_ref, x_ref, wu_ref, bu_ref, wd_ref, bd_ref, y_ref,
                   wu_bf, wd_bf, *, cast_rows):
    j = pl.program_id(0)
    de = wd_ref.shape[1]

    @pl.when(j >= nused_ref[0])
    def _():
        y_ref[...] = jnp.zeros_like(y_ref)

    @pl.when(j < nused_ref[0])
    def _():
        prev = be_ref[jnp.maximum(j - 1, 0)]

        @pl.when(jnp.logical_or(j == 0, be_ref[j] != prev))
        def _():
            def cast(c, _):
                r0 = pl.multiple_of(c * cast_rows, cast_rows)
                wu_bf[pl.ds(r0, cast_rows), :] = wu_ref[0, pl.ds(r0, cast_rows), :].astype(BF16)
                wd_bf[pl.ds(r0, cast_rows), :] = wd_ref[0, pl.ds(r0, cast_rows), :].astype(BF16)
                return 0

            lax.fori_loop(0, wu_ref.shape[1] // cast_rows, cast, 0)

        x = jnp.concatenate([x_ref[pl.ds(c, EXPERT_ROWS, stride=SUBLANES), :] for c in range(SUBLANES)], axis=1)
        up = jnp.dot(x.astype(BF16), wu_bf[...], preferred_element_type=F32) + bu_ref[0]
        glu = jnp.minimum(up[:, :de], SWIGLU_LIMIT)
        lin = jnp.clip(up[:, de:], -SWIGLU_LIMIT, SWIGLU_LIMIT)
        act = glu * jax.nn.sigmoid(SWIGLU_ALPHA * glu) * (lin + 1.0)
        y = jnp.dot(act.astype(BF16), wd_bf[...], preferred_element_type=F32) + bd_ref[0]
        y_ref[...] = _pack_halves(y)


def _pack_halves(y):
    half = y.shape[1] // 2
    bits = lambda v: lax.bitcast_convert_type(v.astype(BF16).astype(F32), jnp.uint32)
    return (bits(y[:, half:]) & jnp.uint32(0xFFFF0000)) | (bits(y[:, :half]) >> 16)


def _unpack_halves(w):
    lo = lax.bitcast_convert_type(w << 16, F32).astype(BF16)
    hi = lax.bitcast_convert_type(w & jnp.uint32(0xFFFF0000), F32).astype(BF16)
    return lo, hi


def _experts(block_expert, n_used, x_rows, w_up, b_up, w_down, b_down):
    n_rows = x_rows.shape[0] // SUBLANES
    n_blocks = n_rows // EXPERT_ROWS + 1
    ne, d, up_cols = w_up.shape
    de = w_down.shape[1]
    blk = lambda j, be, nu: (jnp.minimum(j, nu[0] - 1), 0)
    wsel = lambda j, be, nu: (be[jnp.minimum(j, nu[0] - 1)], 0, 0)
    grid_spec = pltpu.PrefetchScalarGridSpec(
        num_scalar_prefetch=2,
        grid=(n_blocks,),
        in_specs=[
            pl.BlockSpec((EXPERT_ROWS * SUBLANES, LANES), blk),
            pl.BlockSpec((1, d, up_cols), wsel),
            pl.BlockSpec((1, 1, up_cols), wsel),
            pl.BlockSpec((1, de, d), wsel),
            pl.BlockSpec((1, 1, d), wsel),
        ],
        out_specs=pl.BlockSpec((EXPERT_ROWS, d // 2), lambda j, be, nu: (j, 0)),
        scratch_shapes=[pltpu.VMEM((d, up_cols), BF16), pltpu.VMEM((de, d), BF16)],
    )
    return pl.pallas_call(
        functools.partial(_expert_kernel, cast_rows=64),
        grid_spec=grid_spec,
        out_shape=jax.ShapeDtypeStruct((n_blocks * EXPERT_ROWS, d // 2), jnp.uint32),
        compiler_params=_params(1),
        name="experts",
    )(block_expert, n_used, x_rows, w_up, b_up.reshape(ne, 1, up_cols),
      w_down, b_down.reshape(ne, 1, d))


def _combine_plan(run_start, run_cnt, eid_t, pos_t, tm, max_pieces):
    first = (run_start >> 3) << 3
    pieces = jnp.where(run_cnt > 0, (run_start - first + run_cnt + PIECE - 1) // PIECE, 0)
    piece_end = jnp.cumsum(pieces, axis=0)
    piece_off = piece_end - pieces
    p = jnp.arange(max_pieces, dtype=jnp.int32)
    expert_of = jnp.minimum(jnp.sum(p[None, :, None] >= piece_end.T[:, None, :], axis=-1), N_EXPERTS - 1)
    onehot = expert_of[:, :, None] == jnp.arange(N_EXPERTS, dtype=jnp.int32)
    src = jnp.sum(jnp.where(onehot, (first - piece_off * PIECE).T[:, None, :], 0), axis=-1) + p * PIECE
    shift = jnp.repeat((piece_off * PIECE - first).T, tm, axis=0)
    tok_onehot = eid_t[:, :, None] == jnp.arange(N_EXPERTS, dtype=jnp.int32)
    staged_t = pos_t + jnp.sum(jnp.where(tok_onehot, shift[:, None, :], 0), axis=-1)
    return src.reshape(-1).astype(jnp.int32), piece_end[-1].astype(jnp.int32), staged_t.astype(jnp.int32)


def _combine_kernel(src_ref, npieces_ref, staged_ref, gate_ref, xn_ref, rows_ref, o_ref, buf, sems,
                    *, tm, max_pieces):
    i = pl.program_id(0)
    n = pl.num_programs(0)
    slot = lax.rem(i, 2)
    half = o_ref.shape[1] // 2

    def fetch(tile, s):
        def body(p, _):
            src = rows_ref.at[pl.ds(pl.multiple_of(src_ref[tile * max_pieces + p], SUBLANES), PIECE)]
            dst = buf.at[s, pl.ds(pl.multiple_of(p * PIECE, PIECE), PIECE)]
            pltpu.make_async_copy(src, dst, sems.at[s]).start()
            return 0

        lax.fori_loop(0, npieces_ref[tile], body, 0)

    @pl.when(i == 0)
    def _():
        buf[...] = jnp.zeros_like(buf)
        fetch(0, 0)

    @pl.when(i + 1 < n)
    def _():
        fetch(i + 1, 1 - slot)

    total = npieces_ref[i] * PIECE
    staged = staged_ref[...]
    gates = gate_ref[...]

    @pl.when(total > 0)
    def _():
        rows = pl.ds(0, pl.multiple_of(total, PIECE))
        pltpu.make_async_copy(rows_ref.at[rows], buf.at[slot, rows], sems.at[slot]).wait()

    o_ref[...] = xn_ref[...]

    def chunk(c, _):
        c0 = pl.multiple_of(c * COMBINE_CHUNK, COMBINE_CHUNK)
        col = lax.broadcasted_iota(jnp.int32, (tm, COMBINE_CHUNK), 1) + c0
        sel = jnp.zeros((tm, COMBINE_CHUNK), F32)
        for k in range(TOP_K):
            sel = jnp.where(staged[:, k:k + 1] == col, gates[:, k:k + 1], sel)
        sel = sel.astype(BF16)
        lo, hi = _unpack_halves(buf[slot, pl.ds(c0, COMBINE_CHUNK), :])
        o_ref[:, :half] += jnp.dot(sel, lo, preferred_element_type=F32)
        o_ref[:, half:] += jnp.dot(sel, hi, preferred_element_type=F32)
        return 0

    lax.fori_loop(0, (total + COMBINE_CHUNK - 1) // COMBINE_CHUNK, chunk, 0)


def _combine(run_start, run_cnt, eid_t, pos_t, gates_t, xn, y_rows, tm):
    t, d = xn.shape
    n = t // tm
    cap = -(-(TOP_K * tm + 2 * PIECE * N_EXPERTS) // COMBINE_CHUNK) * COMBINE_CHUNK
    max_pieces = cap // PIECE
    src, npieces, staged_t = _combine_plan(run_start, run_cnt, eid_t, pos_t, tm, max_pieces)
    tok = lambda i, *_: (i, 0)
    grid_spec = pltpu.PrefetchScalarGridSpec(
        num_scalar_prefetch=2,
        grid=(n,),
        in_specs=[
            pl.BlockSpec((tm, TOP_K), tok),
            pl.BlockSpec((tm, TOP_K), tok),
            pl.BlockSpec((tm, d), tok),
            pl.BlockSpec(memory_space=pl.ANY),
        ],
        out_specs=pl.BlockSpec((tm, d), tok),
        scratch_shapes=[pltpu.VMEM((2, cap, d // 2), jnp.uint32), pltpu.SemaphoreType.DMA((2,))],
    )
    return pl.pallas_call(
        functools.partial(_combine_kernel, tm=tm, max_pieces=max_pieces),
        grid_spec=grid_spec,
        out_shape=jax.ShapeDtypeStruct((t, d), F32),
        compiler_params=_params(1),
        name="combine",
    )(src, npieces, staged_t, gates_t, xn, y_rows)


def _tile(n, want):
    t = min(n, want)
    assert n % t == 0, (n, t)
    return t


def kernel(x, meta_tokens, norm1_g, w_in, q_norm_g, k_norm_g, conv_w, conv_b, conv_ln_g, conv_ln_b,
           attn_out_g, w_out, norm2_g, w_router, b_router, w_up, b_up, w_down, b_down):
    assert norm1_g.shape[0] == 1, "single layer: meta-token rows are only keys/values and conv context"
    b, s, d = x.shape
    t = b * s
    cw = conv_w.shape[-1]

    g1 = norm1_g[0][None, :]
    w_in_bf = w_in[0].astype(BF16)
    qg = jnp.tile(q_norm_g[0], SB_HEADS)[None, :]
    kg = jnp.tile(k_norm_g[0], SB_HEADS)[None, :]

    q, k, v, hcv = _inproj(x.reshape(t, d), g1, w_in_bf, qg, kg, _tile(t, 512))
    _, km, vm, hm = _inproj(meta_tokens, g1, w_in_bf, qg, kg, N_META)
    pad = ((0, LANES - N_META), (0, 0))
    sb = _attention(q.reshape(b, s, SB_WIDTH), k.reshape(b, s, SB_WIDTH), v.reshape(b, s, SB_WIDTH),
                    jnp.pad(km, pad), jnp.pad(vm, pad), attn_out_g[0].reshape(1, SB_WIDTH), _tile(s, 256))

    mhalo = jnp.concatenate([jnp.zeros((HALO - N_META, cw), F32), hm], axis=0)
    tm = _tile(s, 512)
    wr_t = w_router[0].T
    wr_hi = wr_t.astype(BF16)
    wr_lo = (wr_t - wr_hi.astype(F32)).astype(BF16)
    xn, hn, eid, gates, rank, before, counts = _mix(
        hcv.reshape(b, s, cw), mhalo, sb, x, conv_w[0], conv_b[0][None, :], conv_ln_g[0][None, :],
        conv_ln_b[0][None, :], w_out[0].astype(BF16), norm2_g[0][None, :],
        jnp.concatenate([wr_hi, wr_lo], axis=0), b_router[0][:, None], tm, _tile(tm, 64))

    counts = counts[:, 0].astype(jnp.int32)
    padded = (counts + EXPERT_ROWS - 1) // EXPERT_ROWS * EXPERT_ROWS
    pad_end = jnp.cumsum(padded)
    pad_start = pad_end - padded
    n_blocks = t * TOP_K // EXPERT_ROWS + N_EXPERTS
    block_row = jnp.arange(n_blocks, dtype=jnp.int32) * EXPERT_ROWS
    block_expert = jnp.minimum(jnp.sum(block_row[:, None] >= pad_end[None, :], axis=1), N_EXPERTS - 1).astype(jnp.int32)
    n_used = (pad_end[-1:] // EXPERT_ROWS).astype(jnp.int32)
    onehot = eid[:, :, None] == jnp.arange(N_EXPERTS, dtype=jnp.int32)
    pos = rank + jnp.sum(jnp.where(onehot, pad_start, 0), axis=-1)

    pos_t = pos.T
    x_rows = _dispatch(pad_start + counts, padded - counts, n_used, pos_t.reshape(-1), hn.reshape(t, d),
                       n_blocks, _tile(t, 512))
    y_rows = _experts(block_expert, n_used, x_rows, w_up[0], b_up[0], w_down[0], b_down[0])
    tc = _tile(t, 256)
    run_start = pad_start[:, None] + before[:, ::tc]
    run_end = jnp.concatenate([run_start[:, 1:], (pad_start + counts)[:, None]], axis=1)
    out = _combine(run_start, run_end - run_start, eid.T, pos_t, gates.T, xn.reshape(t, d), y_rows, tc)
    return out.reshape(b, s, d)
```

```python
import functools

import jax
import jax.numpy as jnp
from jax import lax
from jax.experimental import pallas as pl
from jax.experimental.pallas import tpu as pltpu

N_META = 16
SB_HEADS = 8
SB_HEAD_DIM = 64
SB_WIDTH = SB_HEADS * SB_HEAD_DIM
CONV_KERNEL = 31
N_EXPERTS = 32
TOP_K = 4
SWIGLU_LIMIT = 7.0
SWIGLU_ALPHA = 1.702
EPS = 1e-6
F32_EXP2_UNDERFLOW = -150.0
LOG2_E = 1.4426950408889634

LANES = 128
SUBLANES = 8
PIECE = 16
COMBINE_CHUNK = 512
COMBINE_STATIC_CHUNKS = 3
HALO = 32
EXPERT_ROWS = 512
VMEM_LIMIT = 56 * 1024 * 1024

F32 = jnp.float32
BF16 = jnp.bfloat16


def _params(n_axes, vmem=VMEM_LIMIT):
    return pltpu.CompilerParams(dimension_semantics=("arbitrary",) * n_axes, vmem_limit_bytes=vmem)


def _inproj_kernel(x_ref, g1_ref, w_ref, qg_ref, kg_ref, q_ref, k_ref, v_ref, h_ref):
    x = x_ref[...]
    ms = jnp.mean(x * x, axis=-1, keepdims=True)
    n = (x * lax.rsqrt(ms + EPS) * g1_ref[...]).astype(BF16)
    lo = lax.broadcasted_iota(jnp.int32, (1, LANES), 1) < SB_HEAD_DIM

    def proj(c0, c1):
        return jnp.dot(n, w_ref[:, c0:c1], preferred_element_type=F32)

    def head_norm(acc, g_ref, out_ref, scale):
        for c in range(SB_WIDTH // LANES):
            sl = slice(c * LANES, (c + 1) * LANES)
            a = acc[:, sl]
            sq = a * a
            s_lo = jnp.sum(jnp.where(lo, sq, 0.0), axis=-1, keepdims=True)
            s_hi = jnp.sum(jnp.where(lo, 0.0, sq), axis=-1, keepdims=True)
            r = lax.rsqrt(jnp.where(lo, s_lo, s_hi) * (1.0 / SB_HEAD_DIM) + EPS)
            out_ref[:, sl] = (a * r * (g_ref[:, sl] * scale)).astype(BF16)

    head_norm(proj(0, SB_WIDTH), qg_ref, q_ref, LOG2_E * SB_HEAD_DIM ** -0.5)
    head_norm(proj(SB_WIDTH, 2 * SB_WIDTH), kg_ref, k_ref, 1.0)
    v_ref[...] = proj(2 * SB_WIDTH, 3 * SB_WIDTH).astype(BF16)
    cw = (w_ref.shape[1] - 3 * SB_WIDTH) // 2
    val = proj(3 * SB_WIDTH, 3 * SB_WIDTH + cw)
    gate = proj(3 * SB_WIDTH + cw, 3 * SB_WIDTH + 2 * cw)
    h_ref[...] = val * jax.nn.sigmoid(gate)


def _inproj(x2, g1, w_in_bf, qg, kg, tm):
    t, d = x2.shape
    cols = w_in_bf.shape[1]
    cw = (cols - 3 * SB_WIDTH) // 2
    row = lambda i: (i, 0)
    fixed = lambda i: (0, 0)
    return pl.pallas_call(
        _inproj_kernel,
        grid=(t // tm,),
        in_specs=[
            pl.BlockSpec((tm, d), row),
            pl.BlockSpec((1, d), fixed),
            pl.BlockSpec((d, cols), fixed),
            pl.BlockSpec((1, SB_WIDTH), fixed),
            pl.BlockSpec((1, SB_WIDTH), fixed),
        ],
        out_specs=[
            pl.BlockSpec((tm, SB_WIDTH), row),
            pl.BlockSpec((tm, SB_WIDTH), row),
            pl.BlockSpec((tm, SB_WIDTH), row),
            pl.BlockSpec((tm, cw), row),
        ],
        out_shape=[
            jax.ShapeDtypeStruct((t, SB_WIDTH), BF16),
            jax.ShapeDtypeStruct((t, SB_WIDTH), BF16),
            jax.ShapeDtypeStruct((t, SB_WIDTH), BF16),
            jax.ShapeDtypeStruct((t, cw), F32),
        ],
        compiler_params=_params(1),
        name="inproj",
    )(x2, g1, w_in_bf, qg, kg)


def _attn_kernel(q_ref, k_ref, v_ref, km_ref, vm_ref, g_ref, o_ref, *, tq):
    i = pl.program_id(2)
    lane = lax.broadcasted_iota(jnp.int32, (1, LANES), 1)
    lo = lane < SB_HEAD_DIM
    q2 = q_ref[0]
    zero_bf = jnp.zeros((), BF16)
    qs = jnp.concatenate([jnp.where(lo, q2, zero_bf), jnp.where(lo, zero_bf, q2)], axis=0)
    row = lax.broadcasted_iota(jnp.int32, (2 * tq, tq), 0)
    col = lax.broadcasted_iota(jnp.int32, (2 * tq, tq), 1)
    causal = col < jnp.where(row >= tq, row - tq, row)
    srow = lax.broadcasted_iota(jnp.int32, (tq, tq), 0)
    scol = lax.broadcasted_iota(jnp.int32, (tq, tq), 1)
    neg_suffix = jnp.where(srow > scol, -1.0, 0.0).astype(BF16)

    def block(kb, vb, acc, r, mask, neg_suffix_m):
        z = lax.dot_general(qs, kb, (((1,), (1,)), ((), ())), preferred_element_type=F32)
        m = jnp.minimum(z, 0.0)
        p = jnp.maximum(z, 0.0)
        l1p = jnp.log2(1.0 + jnp.exp2(m - p))
        log_beta = m - l1p
        neg_keep = p + l1p
        if mask is not None:
            neg_keep = jnp.where(mask, neg_keep, 0.0)
        later = jnp.dot(neg_keep.astype(BF16), neg_suffix_m, preferred_element_type=F32)
        a = jnp.exp2(log_beta + later + r)
        if mask is not None:
            a = jnp.where(mask, a, 0.0)
        res = jnp.dot(a.astype(BF16), vb, preferred_element_type=F32)
        acc = acc + jnp.where(lo, res[:tq], res[tq:])
        return acc, r - jnp.sum(neg_keep, axis=-1, keepdims=True)

    def kv_block(j):
        start = pl.multiple_of(j * tq, tq)
        return k_ref[0, pl.ds(start, tq), :], v_ref[0, pl.ds(start, tq), :]

    def live(r):
        return jnp.max(r) > F32_EXP2_UNDERFLOW

    acc0 = jnp.zeros((tq, LANES), F32)
    r0 = jnp.zeros((2 * tq, 1), F32)

    def diagonal(_):
        return block(*kv_block(i), acc0, r0, causal, neg_suffix)

    def diagonal_and_previous(_):
        acc, r = block(*kv_block(i), acc0, r0, causal, neg_suffix)
        return block(*kv_block(i - 1), acc, r, None, neg_suffix)

    acc, r = lax.cond(i > 0, diagonal_and_previous, diagonal, 0)

    def cond(c):
        return jnp.logical_and(c[0] < i, c[3])

    def body(c):
        acc, r = block(*kv_block(i - 1 - c[0]), c[1], c[2], None, neg_suffix)
        return c[0] + 1, acc, r, live(r)

    _, acc, r, alive = lax.while_loop(cond, body, (jnp.int32(1), acc, r, live(r)))

    def meta_block(acc):
        return block(km_ref[...], vm_ref[...], acc, r, lane < N_META, neg_suffix[:LANES, :LANES])[0]

    acc = lax.cond(alive, meta_block, lambda acc: acc, acc)

    sq = acc * acc
    s_lo = jnp.sum(jnp.where(lo, sq, 0.0), axis=-1, keepdims=True)
    s_hi = jnp.sum(jnp.where(lo, 0.0, sq), axis=-1, keepdims=True)
    rn = lax.rsqrt(jnp.where(lo, s_lo, s_hi) * (1.0 / SB_HEAD_DIM) + EPS)
    o_ref[0] = (acc * rn * g_ref[...]).astype(BF16)


def _attention(q, k, v, km, vm, og, tq):
    b, s, _ = q.shape
    n_pairs = SB_WIDTH // LANES
    return pl.pallas_call(
        functools.partial(_attn_kernel, tq=tq),
        grid=(b, n_pairs, s // tq),
        in_specs=[
            pl.BlockSpec((1, tq, LANES), lambda b, p, i: (b, i, p)),
            pl.BlockSpec((1, s, LANES), lambda b, p, i: (b, 0, p)),
            pl.BlockSpec((1, s, LANES), lambda b, p, i: (b, 0, p)),
            pl.BlockSpec((LANES, LANES), lambda b, p, i: (0, p)),
            pl.BlockSpec((LANES, LANES), lambda b, p, i: (0, p)),
            pl.BlockSpec((1, LANES), lambda b, p, i: (0, p)),
        ],
        out_specs=pl.BlockSpec((1, tq, LANES), lambda b, p, i: (b, i, p)),
        out_shape=jax.ShapeDtypeStruct((b, s, SB_WIDTH), BF16),
        compiler_params=_params(3),
        name="attention",
    )(q, k, v, km, vm, og)


def _mix_kernel(h_ref, halo_ref, mhalo_ref, sb_ref, x_ref, cw_ref, cb_ref, lg_ref, lb_ref, wo_ref,
                g2_ref, wr_ref, br_ref,
                xn_ref, hn_ref, eid_ref, gate_ref, rank_ref, before_ref, cnt_ref,
                win_ref, conv_ref, cv_ref, carry_ref, *, tm, chunk):
    b = pl.program_id(0)
    i = pl.program_id(1)

    @pl.when(jnp.logical_and(b == 0, i == 0))
    def _():
        carry_ref[...] = jnp.zeros_like(carry_ref)

    @pl.when(i == 0)
    def _():
        win_ref[0:HALO, :] = mhalo_ref[...]

    @pl.when(i > 0)
    def _():
        win_ref[0:HALO, :] = halo_ref[0]

    win_ref[HALO:, :] = h_ref[0]

    first_tap = HALO - (CONV_KERNEL - 1)

    def conv_chunk(c, _):
        r0 = pl.multiple_of(c * chunk, chunk)
        rows = chunk + HALO
        for lt in range(cb_ref.shape[1] // LANES):
            sl = slice(lt * LANES, (lt + 1) * LANES)
            window = win_ref[pl.ds(r0, rows), sl]
            acc = jnp.broadcast_to(cb_ref[:, sl], (chunk, LANES))
            for b in range(SUBLANES):
                shifted = window if b == 0 else pltpu.roll(window, shift=rows - b, axis=0)
                for a in range(HALO // SUBLANES + 1):
                    j = SUBLANES * a + b - first_tap
                    if 0 <= j < CONV_KERNEL:
                        acc = acc + cw_ref[j:j + 1, sl] * shifted[SUBLANES * a:SUBLANES * a + chunk, :]
            conv_ref[:, sl] = acc
        acc = conv_ref[...]
        mu = jnp.mean(acc, axis=-1, keepdims=True)
        cen = acc - mu
        var = jnp.mean(cen * cen, axis=-1, keepdims=True)
        y = cen * lax.rsqrt(var + EPS) * lg_ref[...] + lb_ref[...]
        cv_ref[pl.ds(r0, chunk), :] = (y * jax.nn.sigmoid(y)).astype(BF16)
        return 0

    lax.fori_loop(0, tm // chunk, conv_chunk, 0)

    mixed = jnp.dot(sb_ref[0], wo_ref[0:SB_WIDTH, :], preferred_element_type=F32)
    mixed = mixed + jnp.dot(cv_ref[...], wo_ref[SB_WIDTH:, :], preferred_element_type=F32)
    xn = x_ref[0] + mixed
    xn_ref[0] = xn
    ms = jnp.mean(xn * xn, axis=-1, keepdims=True)
    hn = xn * lax.rsqrt(ms + EPS) * g2_ref[...]
    hn_ref[0] = hn

    nt = (((1,), (1,)), ((), ()))
    hn_hi = hn.astype(BF16)
    hn_lo = (hn - hn_hi.astype(F32)).astype(BF16)
    both = lax.dot_general(wr_ref[...], hn_hi, nt, preferred_element_type=F32)
    logits = (both[:N_EXPERTS] + both[N_EXPERTS:]
              + lax.dot_general(wr_ref[0:N_EXPERTS, :], hn_lo, nt, preferred_element_type=F32) + br_ref[...])
    eidx = lax.broadcasted_iota(jnp.int32, logits.shape, 0)
    vals, idxs = [], []
    for _ in range(TOP_K):
        m = jnp.max(logits, axis=0, keepdims=True)
        sel = jnp.min(jnp.where(logits == m, eidx, N_EXPERTS), axis=0, keepdims=True)
        vals.append(m)
        idxs.append(sel)
        logits = jnp.where(eidx == sel, -jnp.inf, logits)
    exps = [jnp.exp(v - vals[0]) for v in vals]
    denom = exps[0] + exps[1] + exps[2] + exps[3]
    gate_ref[...] = jnp.concatenate([e / denom for e in exps], axis=0)
    eid_ref[...] = jnp.concatenate(idxs, axis=0)

    onehots = [(eidx == s).astype(F32) for s in idxs]
    chosen = onehots[0] + onehots[1] + onehots[2] + onehots[3]
    tr = lax.broadcasted_iota(jnp.int32, (tm, tm), 0)
    tc = lax.broadcasted_iota(jnp.int32, (tm, tm), 1)
    before = (tr < tc).astype(BF16)
    prefix = jnp.dot(chosen.astype(BF16), before, preferred_element_type=F32) + carry_ref[...]
    rank_ref[...] = jnp.concatenate(
        [jnp.sum(o * prefix, axis=0, keepdims=True) for o in onehots], axis=0).astype(jnp.int32)
    before_ref[...] = prefix.astype(jnp.int32)
    carry_ref[...] = carry_ref[...] + jnp.sum(chosen, axis=1, keepdims=True)
    cnt_ref[...] = jnp.broadcast_to(carry_ref[...], cnt_ref.shape)


def _mix(hcv, mhalo, sb, x, conv_w, conv_b, ln_g, ln_b, w_out_bf, g2, wr_t, br, tm, chunk):
    b, s, d = x.shape
    cw = hcv.shape[-1]
    t = b * s
    per = s // tm
    tile = lambda b, i: (b, i, 0)
    fixed = lambda b, i: (0, 0)
    tok = lambda b, i: (0, b * per + i)
    return pl.pallas_call(
        functools.partial(_mix_kernel, tm=tm, chunk=chunk),
        grid=(b, per),
        in_specs=[
            pl.BlockSpec((1, tm, cw), tile),
            pl.BlockSpec((1, HALO, cw), lambda b, i: (b, jnp.maximum(i * (tm // HALO) - 1, 0), 0)),
            pl.BlockSpec((HALO, cw), fixed),
            pl.BlockSpec((1, tm, SB_WIDTH), tile),
            pl.BlockSpec((1, tm, d), tile),
            pl.BlockSpec((CONV_KERNEL, cw), fixed),
            pl.BlockSpec((1, cw), fixed),
            pl.BlockSpec((1, cw), fixed),
            pl.BlockSpec((1, cw), fixed),
            pl.BlockSpec((SB_WIDTH + cw, d), fixed),
            pl.BlockSpec((1, d), fixed),
            pl.BlockSpec((2 * N_EXPERTS, d), fixed),
            pl.BlockSpec((N_EXPERTS, 1), fixed),
        ],
        out_specs=[
            pl.BlockSpec((1, tm, d), tile),
            pl.BlockSpec((1, tm, d), tile),
            pl.BlockSpec((TOP_K, tm), tok),
            pl.BlockSpec((TOP_K, tm), tok),
            pl.BlockSpec((TOP_K, tm), tok),
            pl.BlockSpec((N_EXPERTS, tm), tok),
            pl.BlockSpec((N_EXPERTS, LANES), fixed),
        ],
        out_shape=[
            jax.ShapeDtypeStruct((b, s, d), F32),
            jax.ShapeDtypeStruct((b, s, d), F32),
            jax.ShapeDtypeStruct((TOP_K, t), jnp.int32),
            jax.ShapeDtypeStruct((TOP_K, t), F32),
            jax.ShapeDtypeStruct((TOP_K, t), jnp.int32),
            jax.ShapeDtypeStruct((N_EXPERTS, t), jnp.int32),
            jax.ShapeDtypeStruct((N_EXPERTS, LANES), F32),
        ],
        scratch_shapes=[
            pltpu.VMEM((tm + HALO, cw), F32),
            pltpu.VMEM((chunk, cw), F32),
            pltpu.VMEM((tm, cw), BF16),
            pltpu.VMEM((N_EXPERTS, 1), F32),
        ],
        compiler_params=_params(2),
        name="mix_router",
    )(hcv, hcv, mhalo, sb, x, conv_w, conv_b, ln_g, ln_b, w_out_bf, g2, wr_t, br)


def _row_tiles(ref, row, n=1):
    return ref.at[pl.ds(pl.multiple_of(row * SUBLANES, SUBLANES), n * SUBLANES)]


def _wait_rows(rows_ref, n, sem):
    pltpu.make_async_copy(_row_tiles(rows_ref, 0, n), _row_tiles(rows_ref, 0, n), sem).wait()


def _dispatch_kernel(padlo_ref, padlen_ref, nused_ref, pos_ref, hn_ref, rows_ref, src_ref, zeros_ref, sem, zsem,
                     *, tm, n_blocks, n_pad):
    i = pl.program_id(0)

    @pl.when(i == 0)
    def _():
        zeros_ref[...] = jnp.zeros_like(zeros_ref)
        for e in range(N_EXPERTS):
            lo, ln = padlo_ref[e], padlen_ref[e]
            for bit in range(EXPERT_ROWS.bit_length() - 1):
                size = 1 << bit

                @pl.when((ln >> bit) & 1 == 1)
                def _():
                    start = lo + ((ln >> (bit + 1)) << (bit + 1))
                    pltpu.make_async_copy(_row_tiles(zeros_ref, 0, size), _row_tiles(rows_ref, start, size),
                                          zsem).start()

        def tail(j, _):
            pltpu.make_async_copy(zeros_ref, _row_tiles(rows_ref, j * EXPERT_ROWS, EXPERT_ROWS), zsem).start()
            return 0

        lax.fori_loop(nused_ref[0], n_blocks, tail, 0)

    groups = hn_ref.shape[1] // LANES
    for c in range(groups):
        src_ref[pl.ds(c, tm, stride=groups), :] = hn_ref[:, c * LANES:(c + 1) * LANES]

    def issue(t, _):
        for k in range(TOP_K):
            pltpu.make_async_copy(_row_tiles(src_ref, t), _row_tiles(rows_ref, pos_ref[t * TOP_K + k]),
                                  sem).start(priority=k % 2)
        return 0

    lax.fori_loop(0, tm, issue, 0, unroll=2)
    _wait_rows(rows_ref, TOP_K * tm, sem)

    @pl.when(i == pl.num_programs(0) - 1)
    def _():
        _wait_rows(rows_ref, n_pad, zsem)


def _dispatch(pad_lo, pad_len, n_used, pos, hn, n_blocks, tm):
    t, d = hn.shape
    n_rows = n_blocks * EXPERT_ROWS
    grid_spec = pltpu.PrefetchScalarGridSpec(
        num_scalar_prefetch=3,
        grid=(t // tm,),
        in_specs=[
            pl.BlockSpec((TOP_K * tm,), lambda i, *_: (i,), memory_space=pltpu.SMEM),
            pl.BlockSpec((tm, d), lambda i, *_: (i, 0)),
        ],
        out_specs=pl.BlockSpec(memory_space=pl.ANY),
        scratch_shapes=[pltpu.VMEM((tm * SUBLANES, LANES), hn.dtype),
                        pltpu.VMEM((EXPERT_ROWS * SUBLANES, LANES), hn.dtype),
                        pltpu.SemaphoreType.DMA(()), pltpu.SemaphoreType.DMA(())],
    )
    assert d == SUBLANES * LANES, d
    return pl.pallas_call(
        functools.partial(_dispatch_kernel, tm=tm, n_blocks=n_blocks, n_pad=n_rows - t * TOP_K),
        grid_spec=grid_spec,
        out_shape=jax.ShapeDtypeStruct((n_rows * SUBLANES, LANES), hn.dtype),
        compiler_params=_params(1),
        name="dispatch",
    )(pad_lo, pad_len, n_used, pos, hn)


def _expert_kernel(be_ref, nused_ref, x_ref, wu_ref, bu_ref, wd_ref, bd_ref, y_ref,
                   wu_bf, wd_bf, *, cast_rows):
    j = pl.program_id(0)
    de = wd_ref.shape[1]

    @pl.when(j >= nused_ref[0])
    def _():
        y_ref[...] = jnp.zeros_like(y_ref)

    @pl.when(j < nused_ref[0])
    def _():
        prev = be_ref[jnp.maximum(j - 1, 0)]

        @pl.when(jnp.logical_or(j == 0, be_ref[j] != prev))
        def _():
            def cast(c, _):
                r0 = pl.multiple_of(c * cast_rows, cast_rows)
                wu_bf[pl.ds(r0, cast_rows), :] = wu_ref[0, pl.ds(r0, cast_rows), :].astype(BF16)
                wd_bf[pl.ds(r0, cast_rows), :] = wd_ref[0, pl.ds(r0, cast_rows), :].astype(BF16)
                return 0

            lax.fori_loop(0, wu_ref.shape[1] // cast_rows, cast, 0)

        x = jnp.concatenate([x_ref[pl.ds(c, EXPERT_ROWS, stride=SUBLANES), :] for c in range(SUBLANES)], axis=1)
        up = jnp.dot(x.astype(BF16), wu_bf[...], preferred_element_type=F32) + bu_ref[0]
        glu = jnp.minimum(up[:, :de], SWIGLU_LIMIT)
        lin = jnp.clip(up[:, de:], -SWIGLU_LIMIT, SWIGLU_LIMIT)
        act = glu * jax.nn.sigmoid(SWIGLU_ALPHA * glu) * (lin + 1.0)
        y = jnp.dot(act.astype(BF16), wd_bf[...], preferred_element_type=F32) + bd_ref[0]
        y_ref[...] = _pack_halves(y)


def _pack_halves(y):
    half = y.shape[1] // 2
    bits = lambda v: lax.bitcast_convert_type(v.astype(BF16).astype(F32), jnp.uint32)
    return (bits(y[:, half:]) & jnp.uint32(0xFFFF0000)) | (bits(y[:, :half]) >> 16)


def _unpack_halves(w):
    lo = lax.bitcast_convert_type(w << 16, F32).astype(BF16)
    hi = lax.bitcast_convert_type(w & jnp.uint32(0xFFFF0000), F32).astype(BF16)
    return lo, hi


def _experts(block_expert, n_used, x_rows, w_up, b_up, w_down, b_down):
    n_rows = x_rows.shape[0] // SUBLANES
    n_blocks = n_rows // EXPERT_ROWS + 1
    ne, d, up_cols = w_up.shape
    de = w_down.shape[1]
    blk = lambda j, be, nu: (jnp.minimum(j, nu[0] - 1), 0)
    wsel = lambda j, be, nu: (be[jnp.minimum(j, nu[0] - 1)], 0, 0)
    grid_spec = pltpu.PrefetchScalarGridSpec(
        num_scalar_prefetch=2,
        grid=(n_blocks,),
        in_specs=[
            pl.BlockSpec((EXPERT_ROWS * SUBLANES, LANES), blk),
            pl.BlockSpec((1, d, up_cols), wsel),
            pl.BlockSpec((1, 1, up_cols), wsel),
            pl.BlockSpec((1, de, d), wsel),
            pl.BlockSpec((1, 1, d), wsel),
        ],
        out_specs=pl.BlockSpec((EXPERT_ROWS, d // 2), lambda j, be, nu: (j, 0)),
        scratch_shapes=[pltpu.VMEM((d, up_cols), BF16), pltpu.VMEM((de, d), BF16)],
    )
    return pl.pallas_call(
        functools.partial(_expert_kernel, cast_rows=64),
        grid_spec=grid_spec,
        out_shape=jax.ShapeDtypeStruct((n_blocks * EXPERT_ROWS, d // 2), jnp.uint32),
        compiler_params=_params(1),
        name="experts",
    )(block_expert, n_used, x_rows, w_up, b_up.reshape(ne, 1, up_cols),
      w_down, b_down.reshape(ne, 1, d))


def _combine_plan(run_start, run_cnt, eid_t, pos_t, tm, max_pieces):
    first = (run_start >> 3) << 3
    pieces = jnp.where(run_cnt > 0, (run_start - first + run_cnt + PIECE - 1) // PIECE, 0)
    piece_end = jnp.cumsum(pieces, axis=0)
    piece_off = piece_end - pieces
    p = jnp.arange(max_pieces, dtype=jnp.int32)
    expert_of = jnp.minimum(jnp.sum(p[None, :, None] >= piece_end.T[:, None, :], axis=-1), N_EXPERTS - 1)
    onehot = expert_of[:, :, None] == jnp.arange(N_EXPERTS, dtype=jnp.int32)
    src = jnp.sum(jnp.where(onehot, (first - piece_off * PIECE).T[:, None, :], 0), axis=-1) + p * PIECE
    shift = jnp.repeat((piece_off * PIECE - first).T, tm, axis=0)
    tok_onehot = eid_t[:, :, None] == jnp.arange(N_EXPERTS, dtype=jnp.int32)
    staged_t = pos_t + jnp.sum(jnp.where(tok_onehot, shift[:, None, :], 0), axis=-1)
    return src.reshape(-1).astype(jnp.int32), piece_end[-1].astype(jnp.int32), staged_t.astype(jnp.int32)


def _combine_kernel(src_ref, npieces_ref, staged_ref, gate_ref, xn_ref, rows_ref, o_ref, buf, sems,
                    *, tm, max_pieces):
    i = pl.program_id(0)
    n = pl.num_programs(0)
    slot = lax.rem(i, 2)
    half = o_ref.shape[1] // 2

    def fetch(tile, s):
        def body(p, _):
            src = rows_ref.at[pl.ds(pl.multiple_of(src_ref[tile * max_pieces + p], SUBLANES), PIECE)]
            dst = buf.at[s, pl.ds(pl.multiple_of(p * PIECE, PIECE), PIECE)]
            pltpu.make_async_copy(src, dst, sems.at[s]).start()
            return 0

        lax.fori_loop(0, npieces_ref[tile], body, 0)

    @pl.when(i == 0)
    def _():
        buf[...] = jnp.zeros_like(buf)
        fetch(0, 0)

    @pl.when(i + 1 < n)
    def _():
        fetch(i + 1, 1 - slot)

    total = npieces_ref[i] * PIECE
    staged = staged_ref[...]
    gates = gate_ref[...]

    @pl.when(total > 0)
    def _():
        rows = pl.ds(0, pl.multiple_of(total, PIECE))
        pltpu.make_async_copy(rows_ref.at[rows], buf.at[slot, rows], sems.at[slot]).wait()

    def chunk_products(c0):
        col = lax.broadcasted_iota(jnp.int32, (tm, COMBINE_CHUNK), 1) + c0
        sel = jnp.zeros((tm, COMBINE_CHUNK), F32)
        for k in range(TOP_K):
            sel = jnp.where(staged[:, k:k + 1] == col, gates[:, k:k + 1], sel)
        sel = sel.astype(BF16)
        lo, hi = _unpack_halves(buf[slot, pl.ds(c0, COMBINE_CHUNK), :])
        return jnp.dot(sel, lo, preferred_element_type=F32), jnp.dot(sel, hi, preferred_element_type=F32)

    xn = xn_ref[...]
    acc_lo, acc_hi = xn[:, :half], xn[:, half:]
    for c in range(COMBINE_STATIC_CHUNKS):
        d_lo, d_hi = chunk_products(c * COMBINE_CHUNK)
        acc_lo, acc_hi = acc_lo + d_lo, acc_hi + d_hi
    o_ref[:, :half] = acc_lo
    o_ref[:, half:] = acc_hi

    def chunk(c, _):
        d_lo, d_hi = chunk_products(pl.multiple_of(c * COMBINE_CHUNK, COMBINE_CHUNK))
        o_ref[:, :half] += d_lo
        o_ref[:, half:] += d_hi
        return 0

    lax.fori_loop(COMBINE_STATIC_CHUNKS, (total + COMBINE_CHUNK - 1) // COMBINE_CHUNK, chunk, 0)


def _combine(run_start, run_cnt, eid_t, pos_t, gates_t, xn, y_rows, tm):
    t, d = xn.shape
    n = t // tm
    cap = -(-(TOP_K * tm + 2 * PIECE * N_EXPERTS) // COMBINE_CHUNK) * COMBINE_CHUNK
    max_pieces = cap // PIECE
    src, npieces, staged_t = _combine_plan(run_start, run_cnt, eid_t, pos_t, tm, max_pieces)
    tok = lambda i, *_: (i, 0)
    grid_spec = pltpu.PrefetchScalarGridSpec(
        num_scalar_prefetch=2,
        grid=(n,),
        in_specs=[
            pl.BlockSpec((tm, TOP_K), tok),
            pl.BlockSpec((tm, TOP_K), tok),
            pl.BlockSpec((tm, d), tok),
            pl.BlockSpec(memory_space=pl.ANY),
        ],
        out_specs=pl.BlockSpec((tm, d), tok),
        scratch_shapes=[pltpu.VMEM((2, cap, d // 2), jnp.uint32), pltpu.SemaphoreType.DMA((2,))],
    )
    return pl.pallas_call(
        functools.partial(_combine_kernel, tm=tm, max_pieces=max_pieces),
        grid_spec=grid_spec,
        out_shape=jax.ShapeDtypeStruct((t, d), F32),
        compiler_params=_params(1),
        name="combine",
    )(src, npieces, staged_t, gates_t, xn, y_rows)


def _tile(n, want):
    t = min(n, want)
    assert n % t == 0, (n, t)
    return t


def kernel(x, meta_tokens, norm1_g, w_in, q_norm_g, k_norm_g, conv_w, conv_b, conv_ln_g, conv_ln_b,
           attn_out_g, w_out, norm2_g, w_router, b_router, w_up, b_up, w_down, b_down):
    assert norm1_g.shape[0] == 1, "single layer: meta-token rows are only keys/values and conv context"
    b, s, d = x.shape
    t = b * s
    cw = conv_w.shape[-1]

    g1 = norm1_g[0][None, :]
    w_in_bf = w_in[0].astype(BF16)
    qg = jnp.tile(q_norm_g[0], SB_HEADS)[None, :]
    kg = jnp.tile(k_norm_g[0], SB_HEADS)[None, :]

    q, k, v, hcv = _inproj(x.reshape(t, d), g1, w_in_bf, qg, kg, _tile(t, 512))
    _, km, vm, hm = _inproj(meta_tokens, g1, w_in_bf, qg, kg, N_META)
    pad = ((0, LANES - N_META), (0, 0))
    sb = _attention(q.reshape(b, s, SB_WIDTH), k.reshape(b, s, SB_WIDTH), v.reshape(b, s, SB_WIDTH),
                    jnp.pad(km, pad), jnp.pad(vm, pad), attn_out_g[0].reshape(1, SB_WIDTH), _tile(s, 256))

    mhalo = jnp.concatenate([jnp.zeros((HALO - N_META, cw), F32), hm], axis=0)
    tm = _tile(s, 512)
    wr_t = w_router[0].T
    wr_hi = wr_t.astype(BF16)
    wr_lo = (wr_t - wr_hi.astype(F32)).astype(BF16)
    xn, hn, eid, gates, rank, before, counts = _mix(
        hcv.reshape(b, s, cw), mhalo, sb, x, conv_w[0], conv_b[0][None, :], conv_ln_g[0][None, :],
        conv_ln_b[0][None, :], w_out[0].astype(BF16), norm2_g[0][None, :],
        jnp.concatenate([wr_hi, wr_lo], axis=0), b_router[0][:, None], tm, _tile(tm, 64))

    counts = counts[:, 0].astype(jnp.int32)
    padded = (counts + EXPERT_ROWS - 1) // EXPERT_ROWS * EXPERT_ROWS
    pad_end = jnp.cumsum(padded)
    pad_start = pad_end - padded
    n_blocks = t * TOP_K // EXPERT_ROWS + N_EXPERTS
    block_row = jnp.arange(n_blocks, dtype=jnp.int32) * EXPERT_ROWS
    block_expert = jnp.minimum(jnp.sum(block_row[:, None] >= pad_end[None, :], axis=1), N_EXPERTS - 1).astype(jnp.int32)
    n_used = (pad_end[-1:] // EXPERT_ROWS).astype(jnp.int32)
    onehot = eid[:, :, None] == jnp.arange(N_EXPERTS, dtype=jnp.int32)
    pos = rank + jnp.sum(jnp.where(onehot, pad_start, 0), axis=-1)

    pos_t = pos.T
    x_rows = _dispatch(pad_start + counts, padded - counts, n_used, pos_t.reshape(-1), hn.reshape(t, d),
                       n_blocks, _tile(t, 512))
    y_rows = _experts(block_expert, n_used, x_rows, w_up[0], b_up[0], w_down[0], b_down[0])
    tc = _tile(t, 256)
    run_start = pad_start[:, None] + before[:, ::tc]
    run_end = jnp.concatenate([run_start[:, 1:], (pad_start + counts)[:, None]], axis=1)
    out = _combine(run_start, run_end - run_start, eid.T, pos_t, gates.T, xn.reshape(t, d), y_rows, tc)
    return out.reshape(b, s, d)
```

```python
import functools

import jax
import jax.numpy as jnp
from jax import lax
from jax.experimental import pallas as pl
from jax.experimental.pallas import tpu as pltpu

N_META = 16
SB_HEADS = 8
SB_HEAD_DIM = 64
SB_WIDTH = SB_HEADS * SB_HEAD_DIM
CONV_KERNEL = 31
N_EXPERTS = 32
TOP_K = 4
SWIGLU_LIMIT = 7.0
SWIGLU_ALPHA = 1.702
EPS = 1e-6
F32_EXP2_UNDERFLOW = -150.0
LOG2_E = 1.4426950408889634

LANES = 128
SUBLANES = 8
PIECE = 16
COMBINE_CHUNK = 512
COMBINE_STATIC_CHUNKS = 3
HALO = 32
EXPERT_ROWS = 512
VMEM_LIMIT = 56 * 1024 * 1024

F32 = jnp.float32
BF16 = jnp.bfloat16


def _params(n_axes, vmem=VMEM_LIMIT):
    return pltpu.CompilerParams(dimension_semantics=("arbitrary",) * n_axes, vmem_limit_bytes=vmem)


def _inproj_kernel(x_ref, g1_ref, w_ref, qg_ref, kg_ref, q_ref, k_ref, v_ref, h_ref):
    x = x_ref[...]
    ms = jnp.mean(x * x, axis=-1, keepdims=True)
    n = (x * lax.rsqrt(ms + EPS) * g1_ref[...]).astype(BF16)
    lo = lax.broadcasted_iota(jnp.int32, (1, LANES), 1) < SB_HEAD_DIM

    def proj(c0, c1):
        return jnp.dot(n, w_ref[:, c0:c1], preferred_element_type=F32)

    def head_norm(acc, g_ref, out_ref, scale):
        for c in range(SB_WIDTH // LANES):
            sl = slice(c * LANES, (c + 1) * LANES)
            a = acc[:, sl]
            sq = a * a
            s_lo = jnp.sum(jnp.where(lo, sq, 0.0), axis=-1, keepdims=True)
            s_hi = jnp.sum(jnp.where(lo, 0.0, sq), axis=-1, keepdims=True)
            r = lax.rsqrt(jnp.where(lo, s_lo, s_hi) * (1.0 / SB_HEAD_DIM) + EPS)
            out_ref[:, sl] = (a * r * (g_ref[:, sl] * scale)).astype(BF16)

    head_norm(proj(0, SB_WIDTH), qg_ref, q_ref, LOG2_E * SB_HEAD_DIM ** -0.5)
    head_norm(proj(SB_WIDTH, 2 * SB_WIDTH), kg_ref, k_ref, 1.0)
    v_ref[...] = proj(2 * SB_WIDTH, 3 * SB_WIDTH).astype(BF16)
    cw = (w_ref.shape[1] - 3 * SB_WIDTH) // 2
    val = proj(3 * SB_WIDTH, 3 * SB_WIDTH + cw)
    gate = proj(3 * SB_WIDTH + cw, 3 * SB_WIDTH + 2 * cw)
    h_ref[...] = val * jax.nn.sigmoid(gate)


def _inproj(x2, g1, w_in_bf, qg, kg, tm):
    t, d = x2.shape
    cols = w_in_bf.shape[1]
    cw = (cols - 3 * SB_WIDTH) // 2
    row = lambda i: (i, 0)
    fixed = lambda i: (0, 0)
    return pl.pallas_call(
        _inproj_kernel,
        grid=(t // tm,),
        in_specs=[
            pl.BlockSpec((tm, d), row),
            pl.BlockSpec((1, d), fixed),
            pl.BlockSpec((d, cols), fixed),
            pl.BlockSpec((1, SB_WIDTH), fixed),
            pl.BlockSpec((1, SB_WIDTH), fixed),
        ],
        out_specs=[
            pl.BlockSpec((tm, SB_WIDTH), row),
            pl.BlockSpec((tm, SB_WIDTH), row),
            pl.BlockSpec((tm, SB_WIDTH), row),
            pl.BlockSpec((tm, cw), row),
        ],
        out_shape=[
            jax.ShapeDtypeStruct((t, SB_WIDTH), BF16),
            jax.ShapeDtypeStruct((t, SB_WIDTH), BF16),
            jax.ShapeDtypeStruct((t, SB_WIDTH), BF16),
            jax.ShapeDtypeStruct((t, cw), F32),
        ],
        compiler_params=_params(1),
        name="inproj",
    )(x2, g1, w_in_bf, qg, kg)


def _attn_kernel(q_ref, k_ref, v_ref, km_ref, vm_ref, g_ref, o_ref, *, tq):
    i = pl.program_id(2)
    lane = lax.broadcasted_iota(jnp.int32, (1, LANES), 1)
    lo = lane < SB_HEAD_DIM
    q2 = q_ref[0]
    zero_bf = jnp.zeros((), BF16)
    qs = jnp.concatenate([jnp.where(lo, q2, zero_bf), jnp.where(lo, zero_bf, q2)], axis=0)
    row = lax.broadcasted_iota(jnp.int32, (2 * tq, tq), 0)
    col = lax.broadcasted_iota(jnp.int32, (2 * tq, tq), 1)
    causal = col < jnp.where(row >= tq, row - tq, row)
    srow = lax.broadcasted_iota(jnp.int32, (tq, tq), 0)
    scol = lax.broadcasted_iota(jnp.int32, (tq, tq), 1)
    neg_suffix = jnp.where(srow > scol, -1.0, 0.0).astype(BF16)

    def block(kb, vb, acc, r, mask, neg_suffix_m):
        z = lax.dot_general(qs, kb, (((1,), (1,)), ((), ())), preferred_element_type=F32)
        m = jnp.minimum(z, 0.0)
        p = jnp.maximum(z, 0.0)
        l1p = jnp.log2(1.0 + jnp.exp2(m - p))
        log_beta = m - l1p
        neg_keep = p + l1p
        if mask is not None:
            neg_keep = jnp.where(mask, neg_keep, 0.0)
        later = jnp.dot(neg_keep.astype(BF16), neg_suffix_m, preferred_element_type=F32)
        a = jnp.exp2(log_beta + later + r)
        if mask is not None:
            a = jnp.where(mask, a, 0.0)
        res = jnp.dot(a.astype(BF16), vb, preferred_element_type=F32)
        acc = acc + jnp.where(lo, res[:tq], res[tq:])
        return acc, r - jnp.sum(neg_keep, axis=-1, keepdims=True)

    def kv_block(j):
        start = pl.multiple_of(j * tq, tq)
        return k_ref[0, pl.ds(start, tq), :], v_ref[0, pl.ds(start, tq), :]

    def live(r):
        return jnp.max(r) > F32_EXP2_UNDERFLOW

    acc0 = jnp.zeros((tq, LANES), F32)
    r0 = jnp.zeros((2 * tq, 1), F32)

    def diagonal(_):
        return block(*kv_block(i), acc0, r0, causal, neg_suffix)

    def diagonal_and_previous(_):
        acc, r = block(*kv_block(i), acc0, r0, causal, neg_suffix)
        return block(*kv_block(i - 1), acc, r, None, neg_suffix)

    acc, r = lax.cond(i > 0, diagonal_and_previous, diagonal, 0)

    def cond(c):
        return jnp.logical_and(c[0] < i, c[3])

    def body(c):
        acc, r = block(*kv_block(i - 1 - c[0]), c[1], c[2], None, neg_suffix)
        return c[0] + 1, acc, r, live(r)

    _, acc, r, alive = lax.while_loop(cond, body, (jnp.int32(1), acc, r, live(r)))

    def meta_block(acc):
        return block(km_ref[...], vm_ref[...], acc, r, lane < N_META, neg_suffix[:LANES, :LANES])[0]

    acc = lax.cond(alive, meta_block, lambda acc: acc, acc)

    sq = acc * acc
    s_lo = jnp.sum(jnp.where(lo, sq, 0.0), axis=-1, keepdims=True)
    s_hi = jnp.sum(jnp.where(lo, 0.0, sq), axis=-1, keepdims=True)
    rn = lax.rsqrt(jnp.where(lo, s_lo, s_hi) * (1.0 / SB_HEAD_DIM) + EPS)
    o_ref[0] = (acc * rn * g_ref[...]).astype(BF16)


def _attention(q, k, v, km, vm, og, tq):
    b, s, _ = q.shape
    n_pairs = SB_WIDTH // LANES
    return pl.pallas_call(
        functools.partial(_attn_kernel, tq=tq),
        grid=(b, n_pairs, s // tq),
        in_specs=[
            pl.BlockSpec((1, tq, LANES), lambda b, p, i: (b, i, p)),
            pl.BlockSpec((1, s, LANES), lambda b, p, i: (b, 0, p)),
            pl.BlockSpec((1, s, LANES), lambda b, p, i: (b, 0, p)),
            pl.BlockSpec((LANES, LANES), lambda b, p, i: (0, p)),
            pl.BlockSpec((LANES, LANES), lambda b, p, i: (0, p)),
            pl.BlockSpec((1, LANES), lambda b, p, i: (0, p)),
        ],
        out_specs=pl.BlockSpec((1, tq, LANES), lambda b, p, i: (b, i, p)),
        out_shape=jax.ShapeDtypeStruct((b, s, SB_WIDTH), BF16),
        compiler_params=_params(3),
        name="attention",
    )(q, k, v, km, vm, og)


def _mix_kernel(h_ref, halo_ref, mhalo_ref, sb_ref, x_ref, cw_ref, cb_ref, lg_ref, lb_ref, wo_ref,
                g2_ref, wr_ref, br_ref,
                xn_ref, eid_ref, gate_ref, rank_ref, before_ref, cnt_ref, rows_ref,
                win_ref, conv_ref, cv_ref, carry_ref, hn3_ref, posv_ref, pos_ref, dsem, psem, *, tm, chunk, cap):
    b = pl.program_id(0)
    i = pl.program_id(1)
    step = b * pl.num_programs(1) + i

    def pos_copy():
        return pltpu.make_async_copy(posv_ref, pos_ref, psem)

    def row_copy(t, k):
        return pltpu.make_async_copy(_row_tiles(hn3_ref, t), _row_tiles(rows_ref, pos_ref[0, k * tm + t]), dsem)

    @pl.when(step == 0)
    def _():
        carry_ref[...] = jnp.zeros_like(carry_ref)
        hn3_ref[...] = jnp.zeros_like(hn3_ref)
        posv_ref[...] = N_EXPERTS * cap + lax.broadcasted_iota(jnp.int32, posv_ref.shape, 1)
        pos_copy().start()

    pos_copy().wait()

    @pl.when(i == 0)
    def _():
        win_ref[0:HALO, :] = mhalo_ref[...]

    @pl.when(i > 0)
    def _():
        win_ref[0:HALO, :] = halo_ref[0]

    win_ref[HALO:, :] = h_ref[0]

    first_tap = HALO - (CONV_KERNEL - 1)

    def conv_chunk(c, _):
        r0 = pl.multiple_of(c * chunk, chunk)
        rows = chunk + HALO
        for lt in range(cb_ref.shape[1] // LANES):
            sl = slice(lt * LANES, (lt + 1) * LANES)
            window = win_ref[pl.ds(r0, rows), sl]
            acc = jnp.broadcast_to(cb_ref[:, sl], (chunk, LANES))
            for b in range(SUBLANES):
                shifted = window if b == 0 else pltpu.roll(window, shift=rows - b, axis=0)
                for a in range(HALO // SUBLANES + 1):
                    j = SUBLANES * a + b - first_tap
                    if 0 <= j < CONV_KERNEL:
                        acc = acc + cw_ref[j:j + 1, sl] * shifted[SUBLANES * a:SUBLANES * a + chunk, :]
            conv_ref[:, sl] = acc
        for tt in range(chunk):
            for k in range(TOP_K):
                row_copy(r0 + tt, k).start(priority=k % 2)
        acc = conv_ref[...]
        mu = jnp.mean(acc, axis=-1, keepdims=True)
        cen = acc - mu
        var = jnp.mean(cen * cen, axis=-1, keepdims=True)
        y = cen * lax.rsqrt(var + EPS) * lg_ref[...] + lb_ref[...]
        cv_ref[pl.ds(r0, chunk), :] = (y * jax.nn.sigmoid(y)).astype(BF16)
        return 0

    lax.fori_loop(0, tm // chunk, conv_chunk, 0)

    mixed = jnp.dot(sb_ref[0], wo_ref[0:SB_WIDTH, :], preferred_element_type=F32)
    mixed = mixed + jnp.dot(cv_ref[...], wo_ref[SB_WIDTH:, :], preferred_element_type=F32)
    xn = x_ref[0] + mixed
    xn_ref[0] = xn
    ms = jnp.mean(xn * xn, axis=-1, keepdims=True)
    hn = xn * lax.rsqrt(ms + EPS) * g2_ref[...]

    nt = (((1,), (1,)), ((), ()))
    hn_hi = hn.astype(BF16)
    hn_lo = (hn - hn_hi.astype(F32)).astype(BF16)
    both = lax.dot_general(wr_ref[...], hn_hi, nt, preferred_element_type=F32)
    logits = (both[:N_EXPERTS] + both[N_EXPERTS:]
              + lax.dot_general(wr_ref[0:N_EXPERTS, :], hn_lo, nt, preferred_element_type=F32) + br_ref[...])
    eidx = lax.broadcasted_iota(jnp.int32, logits.shape, 0)
    vals, idxs = [], []
    for _ in range(TOP_K):
        m = jnp.max(logits, axis=0, keepdims=True)
        sel = jnp.min(jnp.where(logits == m, eidx, N_EXPERTS), axis=0, keepdims=True)
        vals.append(m)
        idxs.append(sel)
        logits = jnp.where(eidx == sel, -jnp.inf, logits)
    exps = [jnp.exp(v - vals[0]) for v in vals]
    denom = exps[0] + exps[1] + exps[2] + exps[3]
    gate_ref[...] = jnp.concatenate([e / denom for e in exps], axis=0)
    eid_ref[...] = jnp.concatenate(idxs, axis=0)

    onehots = [(eidx == s).astype(F32) for s in idxs]
    chosen = onehots[0] + onehots[1] + onehots[2] + onehots[3]
    tr = lax.broadcasted_iota(jnp.int32, (tm, tm), 0)
    tc = lax.broadcasted_iota(jnp.int32, (tm, tm), 1)
    before = (tr < tc).astype(BF16)
    prefix = jnp.dot(chosen.astype(BF16), before, preferred_element_type=F32) + carry_ref[...]
    ranks = [jnp.sum(o * prefix, axis=0, keepdims=True).astype(jnp.int32) for o in onehots]
    rank_ref[...] = jnp.concatenate(ranks, axis=0)
    before_ref[...] = prefix.astype(jnp.int32)
    carry_ref[...] = carry_ref[...] + jnp.sum(chosen, axis=1, keepdims=True)
    cnt_ref[...] = jnp.broadcast_to(carry_ref[...], cnt_ref.shape)

    _wait_rows(rows_ref, TOP_K * tm, dsem)
    groups = hn.shape[1] // LANES
    for c in range(groups):
        hn3_ref[pl.ds(c, tm, stride=groups), :] = hn[:, c * LANES:(c + 1) * LANES]
    for k in range(TOP_K):
        posv_ref[:, k * tm:(k + 1) * tm] = idxs[k] * cap + ranks[k]
    pos_copy().start()

    @pl.when(step == pl.num_programs(0) * pl.num_programs(1) - 1)
    def _():
        pos_copy().wait()

        def issue(t, _):
            for k in range(TOP_K):
                row_copy(t, k).start(priority=k % 2)
            return 0

        lax.fori_loop(0, tm, issue, 0, unroll=2)
        _wait_rows(rows_ref, TOP_K * tm, dsem)


def _mix(hcv, mhalo, sb, x, conv_w, conv_b, ln_g, ln_b, w_out_bf, g2, wr_t, br, tm, chunk, cap):
    b, s, d = x.shape
    cw = hcv.shape[-1]
    t = b * s
    per = s // tm
    assert d == SUBLANES * LANES, d
    tile = lambda b, i: (b, i, 0)
    fixed = lambda b, i: (0, 0)
    tok = lambda b, i: (0, b * per + i)
    x_rows = N_EXPERTS * cap + TOP_K * tm
    return pl.pallas_call(
        functools.partial(_mix_kernel, tm=tm, chunk=chunk, cap=cap),
        grid=(b, per),
        in_specs=[
            pl.BlockSpec((1, tm, cw), tile),
            pl.BlockSpec((1, HALO, cw), lambda b, i: (b, jnp.maximum(i * (tm // HALO) - 1, 0), 0)),
            pl.BlockSpec((HALO, cw), fixed),
            pl.BlockSpec((1, tm, SB_WIDTH), tile),
            pl.BlockSpec((1, tm, d), tile),
            pl.BlockSpec((CONV_KERNEL, cw), fixed),
            pl.BlockSpec((1, cw), fixed),
            pl.BlockSpec((1, cw), fixed),
            pl.BlockSpec((1, cw), fixed),
            pl.BlockSpec((SB_WIDTH + cw, d), fixed),
            pl.BlockSpec((1, d), fixed),
            pl.BlockSpec((2 * N_EXPERTS, d), fixed),
            pl.BlockSpec((N_EXPERTS, 1), fixed),
        ],
        out_specs=[
            pl.BlockSpec((1, tm, d), tile),
            pl.BlockSpec((TOP_K, tm), tok),
            pl.BlockSpec((TOP_K, tm), tok),
            pl.BlockSpec((TOP_K, tm), tok),
            pl.BlockSpec((N_EXPERTS, tm), tok),
            pl.BlockSpec((N_EXPERTS, LANES), fixed),
            pl.BlockSpec(memory_space=pl.ANY),
        ],
        out_shape=[
            jax.ShapeDtypeStruct((b, s, d), F32),
            jax.ShapeDtypeStruct((TOP_K, t), jnp.int32),
            jax.ShapeDtypeStruct((TOP_K, t), F32),
            jax.ShapeDtypeStruct((TOP_K, t), jnp.int32),
            jax.ShapeDtypeStruct((N_EXPERTS, t), jnp.int32),
            jax.ShapeDtypeStruct((N_EXPERTS, LANES), F32),
            jax.ShapeDtypeStruct((x_rows * SUBLANES, LANES), F32),
        ],
        scratch_shapes=[
            pltpu.VMEM((tm + HALO, cw), F32),
            pltpu.VMEM((chunk, cw), F32),
            pltpu.VMEM((tm, cw), BF16),
            pltpu.VMEM((N_EXPERTS, 1), F32),
            pltpu.VMEM((tm * SUBLANES, LANES), F32),
            pltpu.VMEM((1, TOP_K * tm), jnp.int32),
            pltpu.SMEM((1, TOP_K * tm), jnp.int32),
            pltpu.SemaphoreType.DMA(()),
            pltpu.SemaphoreType.DMA(()),
        ],
        compiler_params=_params(2),
        name="mix_router",
    )(hcv, hcv, mhalo, sb, x, conv_w, conv_b, ln_g, ln_b, w_out_bf, g2, wr_t, br)


def _row_tiles(ref, row, n=1):
    size = n * SUBLANES if isinstance(n, int) else pl.multiple_of(n * SUBLANES, SUBLANES)
    return ref.at[pl.ds(pl.multiple_of(row * SUBLANES, SUBLANES), size)]


def _wait_rows(rows_ref, n, sem):
    pltpu.make_async_copy(_row_tiles(rows_ref, 0, n), _row_tiles(rows_ref, 0, n), sem).wait()


def _padfill_kernel(padlo_ref, padlen_ref, rows_in_ref, rows_ref, zeros_ref, zsem):
    del rows_in_ref
    zeros_ref[...] = jnp.zeros_like(zeros_ref)
    total = jnp.int32(0)
    for e in range(N_EXPERTS):
        lo, ln = padlo_ref[e], padlen_ref[e]
        total = total + ln
        for bit in range(EXPERT_ROWS.bit_length() - 1):
            size = 1 << bit

            @pl.when((ln >> bit) & 1 == 1)
            def _():
                start = lo + ((ln >> (bit + 1)) << (bit + 1))
                pltpu.make_async_copy(_row_tiles(zeros_ref, 0, size), _row_tiles(rows_ref, start, size),
                                      zsem).start()

    @pl.when(total > 0)
    def _():
        _wait_rows(rows_ref, total, zsem)


def _padfill(pad_lo, pad_len, x_rows):
    smem = pl.BlockSpec(memory_space=pltpu.SMEM)
    return pl.pallas_call(
        _padfill_kernel,
        in_specs=[smem, smem, pl.BlockSpec(memory_space=pl.ANY)],
        out_specs=pl.BlockSpec(memory_space=pl.ANY),
        out_shape=jax.ShapeDtypeStruct(x_rows.shape, x_rows.dtype),
        scratch_shapes=[pltpu.VMEM((EXPERT_ROWS * SUBLANES // 2, LANES), x_rows.dtype), pltpu.SemaphoreType.DMA(())],
        input_output_aliases={2: 0},
        compiler_params=pltpu.CompilerParams(vmem_limit_bytes=VMEM_LIMIT),
        name="padfill",
    )(pad_lo, pad_len, x_rows)


def _expert_kernel(be_ref, bsrc_ref, nused_ref, x_ref, wu_ref, bu_ref, wd_ref, bd_ref, y_ref,
                   wu_bf, wd_bf, *, cast_rows):
    del bsrc_ref
    j = pl.program_id(0)
    de = wd_ref.shape[1]

    @pl.when(j >= nused_ref[0])
    def _():
        y_ref[...] = jnp.zeros_like(y_ref)

    @pl.when(j < nused_ref[0])
    def _():
        prev = be_ref[jnp.maximum(j - 1, 0)]

        @pl.when(jnp.logical_or(j == 0, be_ref[j] != prev))
        def _():
            def cast(c, _):
                r0 = pl.multiple_of(c * cast_rows, cast_rows)
                wu_bf[pl.ds(r0, cast_rows), :] = wu_ref[0, pl.ds(r0, cast_rows), :].astype(BF16)
                wd_bf[pl.ds(r0, cast_rows), :] = wd_ref[0, pl.ds(r0, cast_rows), :].astype(BF16)
                return 0

            lax.fori_loop(0, wu_ref.shape[1] // cast_rows, cast, 0)

        x = jnp.concatenate([x_ref[pl.ds(c, EXPERT_ROWS, stride=SUBLANES), :] for c in range(SUBLANES)], axis=1)
        up = jnp.dot(x.astype(BF16), wu_bf[...], preferred_element_type=F32) + bu_ref[0]
        glu = jnp.minimum(up[:, :de], SWIGLU_LIMIT)
        lin = jnp.clip(up[:, de:], -SWIGLU_LIMIT, SWIGLU_LIMIT)
        act = glu * jax.nn.sigmoid(SWIGLU_ALPHA * glu) * (lin + 1.0)
        y = jnp.dot(act.astype(BF16), wd_bf[...], preferred_element_type=F32) + bd_ref[0]
        y_ref[...] = _pack_halves(y)


def _pack_halves(y):
    half = y.shape[1] // 2
    bits = lambda v: lax.bitcast_convert_type(v.astype(BF16).astype(F32), jnp.uint32)
    return (bits(y[:, half:]) & jnp.uint32(0xFFFF0000)) | (bits(y[:, :half]) >> 16)


def _unpack_halves(w):
    lo = lax.bitcast_convert_type(w << 16, F32).astype(BF16)
    hi = lax.bitcast_convert_type(w & jnp.uint32(0xFFFF0000), F32).astype(BF16)
    return lo, hi


def _experts(block_expert, block_src, n_used, x_rows, w_up, b_up, w_down, b_down):
    n_blocks = block_expert.shape[0] + 1
    ne, d, up_cols = w_up.shape
    de = w_down.shape[1]
    blk = lambda j, be, bs, nu: (bs[jnp.minimum(j, nu[0] - 1)], 0)
    wsel = lambda j, be, bs, nu: (be[jnp.minimum(j, nu[0] - 1)], 0, 0)
    grid_spec = pltpu.PrefetchScalarGridSpec(
        num_scalar_prefetch=3,
        grid=(n_blocks,),
        in_specs=[
            pl.BlockSpec((EXPERT_ROWS * SUBLANES, LANES), blk),
            pl.BlockSpec((1, d, up_cols), wsel),
            pl.BlockSpec((1, 1, up_cols), wsel),
            pl.BlockSpec((1, de, d), wsel),
            pl.BlockSpec((1, 1, d), wsel),
        ],
        out_specs=pl.BlockSpec((EXPERT_ROWS, d // 2), lambda j, be, bs, nu: (j, 0)),
        scratch_shapes=[pltpu.VMEM((d, up_cols), BF16), pltpu.VMEM((de, d), BF16)],
    )
    return pl.pallas_call(
        functools.partial(_expert_kernel, cast_rows=64),
        grid_spec=grid_spec,
        out_shape=jax.ShapeDtypeStruct((n_blocks * EXPERT_ROWS, d // 2), jnp.uint32),
        compiler_params=_params(1),
        name="experts",
    )(block_expert, block_src, n_used, x_rows, w_up, b_up.reshape(ne, 1, up_cols),
      w_down, b_down.reshape(ne, 1, d))


def _combine_plan(run_start, run_cnt, eid_t, pos_t, tm, max_pieces):
    first = (run_start >> 3) << 3
    pieces = jnp.where(run_cnt > 0, (run_start - first + run_cnt + PIECE - 1) // PIECE, 0)
    piece_end = jnp.cumsum(pieces, axis=0)
    piece_off = piece_end - pieces
    p = jnp.arange(max_pieces, dtype=jnp.int32)
    expert_of = jnp.minimum(jnp.sum(p[None, :, None] >= piece_end.T[:, None, :], axis=-1), N_EXPERTS - 1)
    onehot = expert_of[:, :, None] == jnp.arange(N_EXPERTS, dtype=jnp.int32)
    src = jnp.sum(jnp.where(onehot, (first - piece_off * PIECE).T[:, None, :], 0), axis=-1) + p * PIECE
    shift = jnp.repeat((piece_off * PIECE - first).T, tm, axis=0)
    tok_onehot = eid_t[:, :, None] == jnp.arange(N_EXPERTS, dtype=jnp.int32)
    staged_t = pos_t + jnp.sum(jnp.where(tok_onehot, shift[:, None, :], 0), axis=-1)
    return src.reshape(-1).astype(jnp.int32), piece_end[-1].astype(jnp.int32), staged_t.astype(jnp.int32)


def _combine_kernel(src_ref, npieces_ref, staged_ref, gate_ref, xn_ref, rows_ref, o_ref, buf, sems,
                    *, tm, max_pieces):
    i = pl.program_id(0)
    n = pl.num_programs(0)
    slot = lax.rem(i, 2)
    half = o_ref.shape[1] // 2

    def fetch(tile, s):
        def body(p, _):
            src = rows_ref.at[pl.ds(pl.multiple_of(src_ref[tile * max_pieces + p], SUBLANES), PIECE)]
            dst = buf.at[s, pl.ds(pl.multiple_of(p * PIECE, PIECE), PIECE)]
            pltpu.make_async_copy(src, dst, sems.at[s]).start()
            return 0

        lax.fori_loop(0, npieces_ref[tile], body, 0)

    @pl.when(i == 0)
    def _():
        buf[...] = jnp.zeros_like(buf)
        fetch(0, 0)

    @pl.when(i + 1 < n)
    def _():
        fetch(i + 1, 1 - slot)

    total = npieces_ref[i] * PIECE
    staged = staged_ref[...]
    gates = gate_ref[...]

    @pl.when(total > 0)
    def _():
        rows = pl.ds(0, pl.multiple_of(total, PIECE))
        pltpu.make_async_copy(rows_ref.at[rows], buf.at[slot, rows], sems.at[slot]).wait()

    def chunk_products(c0):
        col = lax.broadcasted_iota(jnp.int32, (tm, COMBINE_CHUNK), 1) + c0
        sel = jnp.zeros((tm, COMBINE_CHUNK), F32)
        for k in range(TOP_K):
            sel = jnp.where(staged[:, k:k + 1] == col, gates[:, k:k + 1], sel)
        sel = sel.astype(BF16)
        lo, hi = _unpack_halves(buf[slot, pl.ds(c0, COMBINE_CHUNK), :])
        return jnp.dot(sel, lo, preferred_element_type=F32), jnp.dot(sel, hi, preferred_element_type=F32)

    xn = xn_ref[...]
    acc_lo, acc_hi = xn[:, :half], xn[:, half:]
    for c in range(COMBINE_STATIC_CHUNKS):
        d_lo, d_hi = chunk_products(c * COMBINE_CHUNK)
        acc_lo, acc_hi = acc_lo + d_lo, acc_hi + d_hi
    o_ref[:, :half] = acc_lo
    o_ref[:, half:] = acc_hi

    def chunk(c, _):
        d_lo, d_hi = chunk_products(pl.multiple_of(c * COMBINE_CHUNK, COMBINE_CHUNK))
        o_ref[:, :half] += d_lo
        o_ref[:, half:] += d_hi
        return 0

    lax.fori_loop(COMBINE_STATIC_CHUNKS, (total + COMBINE_CHUNK - 1) // COMBINE_CHUNK, chunk, 0)


def _combine(run_start, run_cnt, eid_t, pos_t, gates_t, xn, y_rows, tm):
    t, d = xn.shape
    n = t // tm
    cap = -(-(TOP_K * tm + 2 * PIECE * N_EXPERTS) // COMBINE_CHUNK) * COMBINE_CHUNK
    max_pieces = cap // PIECE
    src, npieces, staged_t = _combine_plan(run_start, run_cnt, eid_t, pos_t, tm, max_pieces)
    tok = lambda i, *_: (i, 0)
    grid_spec = pltpu.PrefetchScalarGridSpec(
        num_scalar_prefetch=2,
        grid=(n,),
        in_specs=[
            pl.BlockSpec((tm, TOP_K), tok),
            pl.BlockSpec((tm, TOP_K), tok),
            pl.BlockSpec((tm, d), tok),
            pl.BlockSpec(memory_space=pl.ANY),
        ],
        out_specs=pl.BlockSpec((tm, d), tok),
        scratch_shapes=[pltpu.VMEM((2, cap, d // 2), jnp.uint32), pltpu.SemaphoreType.DMA((2,))],
    )
    return pl.pallas_call(
        functools.partial(_combine_kernel, tm=tm, max_pieces=max_pieces),
        grid_spec=grid_spec,
        out_shape=jax.ShapeDtypeStruct((t, d), F32),
        compiler_params=_params(1),
        name="combine",
    )(src, npieces, staged_t, gates_t, xn, y_rows)


def _tile(n, want):
    t = min(n, want)
    assert n % t == 0, (n, t)
    return t


def kernel(x, meta_tokens, norm1_g, w_in, q_norm_g, k_norm_g, conv_w, conv_b, conv_ln_g, conv_ln_b,
           attn_out_g, w_out, norm2_g, w_router, b_router, w_up, b_up, w_down, b_down):
    assert norm1_g.shape[0] == 1, "single layer: meta-token rows are only keys/values and conv context"
    b, s, d = x.shape
    t = b * s
    cw = conv_w.shape[-1]

    g1 = norm1_g[0][None, :]
    w_in_bf = w_in[0].astype(BF16)
    qg = jnp.tile(q_norm_g[0], SB_HEADS)[None, :]
    kg = jnp.tile(k_norm_g[0], SB_HEADS)[None, :]

    q, k, v, hcv = _inproj(x.reshape(t, d), g1, w_in_bf, qg, kg, _tile(t, 512))
    _, km, vm, hm = _inproj(meta_tokens, g1, w_in_bf, qg, kg, N_META)
    pad = ((0, LANES - N_META), (0, 0))
    sb = _attention(q.reshape(b, s, SB_WIDTH), k.reshape(b, s, SB_WIDTH), v.reshape(b, s, SB_WIDTH),
                    jnp.pad(km, pad), jnp.pad(vm, pad), attn_out_g[0].reshape(1, SB_WIDTH), _tile(s, 256))

    mhalo = jnp.concatenate([jnp.zeros((HALO - N_META, cw), F32), hm], axis=0)
    tm = _tile(s, 512)
    wr_t = w_router[0].T
    wr_hi = wr_t.astype(BF16)
    wr_lo = (wr_t - wr_hi.astype(F32)).astype(BF16)
    cap = -(-t // EXPERT_ROWS) * EXPERT_ROWS
    xn, eid, gates, rank, before, counts, x_rows = _mix(
        hcv.reshape(b, s, cw), mhalo, sb, x, conv_w[0], conv_b[0][None, :], conv_ln_g[0][None, :],
        conv_ln_b[0][None, :], w_out[0].astype(BF16), norm2_g[0][None, :],
        jnp.concatenate([wr_hi, wr_lo], axis=0), b_router[0][:, None], tm, _tile(tm, 64), cap)

    counts = counts[:, 0].astype(jnp.int32)
    padded = (counts + EXPERT_ROWS - 1) // EXPERT_ROWS * EXPERT_ROWS
    pad_end = jnp.cumsum(padded)
    pad_start = pad_end - padded
    n_blocks = t * TOP_K // EXPERT_ROWS + N_EXPERTS
    block_row = jnp.arange(n_blocks, dtype=jnp.int32) * EXPERT_ROWS
    block_expert = jnp.minimum(jnp.sum(block_row[:, None] >= pad_end[None, :], axis=1), N_EXPERTS - 1).astype(jnp.int32)
    block_src = (block_expert * cap + jnp.maximum(block_row - pad_start[block_expert], 0)) // EXPERT_ROWS
    n_used = (pad_end[-1:] // EXPERT_ROWS).astype(jnp.int32)
    onehot = eid[:, :, None] == jnp.arange(N_EXPERTS, dtype=jnp.int32)
    pos = rank + jnp.sum(jnp.where(onehot, pad_start, 0), axis=-1)

    pos_t = pos.T
    seg = jnp.arange(N_EXPERTS, dtype=jnp.int32) * cap
    x_rows = _padfill(seg + counts, padded - counts, x_rows)
    y_rows = _experts(block_expert, block_src.astype(jnp.int32), n_used, x_rows, w_up[0], b_up[0], w_down[0],
                      b_down[0])
    tc = _tile(t, 256)
    run_start = pad_start[:, None] + before[:, ::tc]
    run_end = jnp.concatenate([run_start[:, 1:], (pad_start + counts)[:, None]], axis=1)
    out = _combine(run_start, run_end - run_start, eid.T, pos_t, gates.T, xn.reshape(t, d), y_rows, tc)
    return out.reshape(b, s, d)
```

```python
import functools

import jax
import jax.numpy as jnp
from jax import lax
from jax.experimental import pallas as pl
from jax.experimental.pallas import tpu as pltpu

N_META = 16
SB_HEADS = 8
SB_HEAD_DIM = 64
SB_WIDTH = SB_HEADS * SB_HEAD_DIM
CONV_KERNEL = 31
N_EXPERTS = 32
TOP_K = 4
SWIGLU_LIMIT = 7.0
SWIGLU_ALPHA = 1.702
EPS = 1e-6
F32_EXP2_UNDERFLOW = -150.0
LOG2_E = 1.4426950408889634

LANES = 128
SUBLANES = 8
PIECE = 16
COMBINE_CHUNK = 512
COMBINE_STATIC_CHUNKS = 3
HALO = 32
EXPERT_ROWS = 512
VMEM_LIMIT = 56 * 1024 * 1024

F32 = jnp.float32
BF16 = jnp.bfloat16


def _params(n_axes, vmem=VMEM_LIMIT):
    return pltpu.CompilerParams(dimension_semantics=("arbitrary",) * n_axes, vmem_limit_bytes=vmem)


def _inproj_kernel(x_ref, g1_ref, w_ref, qg_ref, kg_ref, q_ref, k_ref, v_ref, h_ref):
    x = x_ref[...]
    ms = jnp.mean(x * x, axis=-1, keepdims=True)
    n = (x * lax.rsqrt(ms + EPS) * g1_ref[...]).astype(BF16)
    lo = lax.broadcasted_iota(jnp.int32, (1, LANES), 1) < SB_HEAD_DIM

    def proj(c0, c1):
        return jnp.dot(n, w_ref[:, c0:c1], preferred_element_type=F32)

    def head_norm(acc, g_ref, out_ref, scale):
        for c in range(SB_WIDTH // LANES):
            sl = slice(c * LANES, (c + 1) * LANES)
            a = acc[:, sl]
            sq = a * a
            s_lo = jnp.sum(jnp.where(lo, sq, 0.0), axis=-1, keepdims=True)
            s_hi = jnp.sum(jnp.where(lo, 0.0, sq), axis=-1, keepdims=True)
            r = lax.rsqrt(jnp.where(lo, s_lo, s_hi) * (1.0 / SB_HEAD_DIM) + EPS)
            out_ref[:, sl] = (a * r * (g_ref[:, sl] * scale)).astype(BF16)

    head_norm(proj(0, SB_WIDTH), qg_ref, q_ref, LOG2_E * SB_HEAD_DIM ** -0.5)
    head_norm(proj(SB_WIDTH, 2 * SB_WIDTH), kg_ref, k_ref, 1.0)
    v_ref[...] = proj(2 * SB_WIDTH, 3 * SB_WIDTH).astype(BF16)
    cw = (w_ref.shape[1] - 3 * SB_WIDTH) // 2
    val = proj(3 * SB_WIDTH, 3 * SB_WIDTH + cw)
    gate = proj(3 * SB_WIDTH + cw, 3 * SB_WIDTH + 2 * cw)
    h_ref[...] = val * jax.nn.sigmoid(gate)


def _inproj(x2, g1, w_in_bf, qg, kg, tm):
    t, d = x2.shape
    cols = w_in_bf.shape[1]
    cw = (cols - 3 * SB_WIDTH) // 2
    row = lambda i: (i, 0)
    fixed = lambda i: (0, 0)
    return pl.pallas_call(
        _inproj_kernel,
        grid=(t // tm,),
        in_specs=[
            pl.BlockSpec((tm, d), row),
            pl.BlockSpec((1, d), fixed),
            pl.BlockSpec((d, cols), fixed),
            pl.BlockSpec((1, SB_WIDTH), fixed),
            pl.BlockSpec((1, SB_WIDTH), fixed),
        ],
        out_specs=[
            pl.BlockSpec((tm, SB_WIDTH), row),
            pl.BlockSpec((tm, SB_WIDTH), row),
            pl.BlockSpec((tm, SB_WIDTH), row),
            pl.BlockSpec((tm, cw), row),
        ],
        out_shape=[
            jax.ShapeDtypeStruct((t, SB_WIDTH), BF16),
            jax.ShapeDtypeStruct((t, SB_WIDTH), BF16),
            jax.ShapeDtypeStruct((t, SB_WIDTH), BF16),
            jax.ShapeDtypeStruct((t, cw), F32),
        ],
        compiler_params=_params(1),
        name="inproj",
    )(x2, g1, w_in_bf, qg, kg)


def _attn_kernel(q_ref, k_ref, v_ref, km_ref, vm_ref, g_ref, o_ref, *, tq):
    i = pl.program_id(2)
    lane = lax.broadcasted_iota(jnp.int32, (1, LANES), 1)
    lo = lane < SB_HEAD_DIM
    q2 = q_ref[0]
    zero_bf = jnp.zeros((), BF16)
    qs = jnp.concatenate([jnp.where(lo, q2, zero_bf), jnp.where(lo, zero_bf, q2)], axis=0)
    row = lax.broadcasted_iota(jnp.int32, (2 * tq, tq), 0)
    col = lax.broadcasted_iota(jnp.int32, (2 * tq, tq), 1)
    causal = col < jnp.where(row >= tq, row - tq, row)
    srow = lax.broadcasted_iota(jnp.int32, (tq, tq), 0)
    scol = lax.broadcasted_iota(jnp.int32, (tq, tq), 1)
    neg_suffix = jnp.where(srow > scol, -1.0, 0.0).astype(BF16)

    def block(kb, vb, acc, r, mask, neg_suffix_m):
        z = lax.dot_general(qs, kb, (((1,), (1,)), ((), ())), preferred_element_type=F32)
        m = jnp.minimum(z, 0.0)
        p = jnp.maximum(z, 0.0)
        l1p = jnp.log2(1.0 + jnp.exp2(m - p))
        log_beta = m - l1p
        neg_keep = p + l1p
        if mask is not None:
            neg_keep = jnp.where(mask, neg_keep, 0.0)
        later = jnp.dot(neg_keep.astype(BF16), neg_suffix_m, preferred_element_type=F32)
        a = jnp.exp2(log_beta + later + r)
        if mask is not None:
            a = jnp.where(mask, a, 0.0)
        res = jnp.dot(a.astype(BF16), vb, preferred_element_type=F32)
        acc = acc + jnp.where(lo, res[:tq], res[tq:])
        return acc, r - jnp.sum(neg_keep, axis=-1, keepdims=True)

    def kv_block(j):
        start = pl.multiple_of(j * tq, tq)
        return k_ref[0, pl.ds(start, tq), :], v_ref[0, pl.ds(start, tq), :]

    def live(r):
        return jnp.max(r) > F32_EXP2_UNDERFLOW

    acc0 = jnp.zeros((tq, LANES), F32)
    r0 = jnp.zeros((2 * tq, 1), F32)

    def diagonal(_):
        return block(*kv_block(i), acc0, r0, causal, neg_suffix)

    def diagonal_and_previous(_):
        acc, r = block(*kv_block(i), acc0, r0, causal, neg_suffix)
        return block(*kv_block(i - 1), acc, r, None, neg_suffix)

    acc, r = lax.cond(i > 0, diagonal_and_previous, diagonal, 0)

    def cond(c):
        return jnp.logical_and(c[0] < i, c[3])

    def body(c):
        acc, r = block(*kv_block(i - 1 - c[0]), c[1], c[2], None, neg_suffix)
        return c[0] + 1, acc, r, live(r)

    _, acc, r, alive = lax.while_loop(cond, body, (jnp.int32(1), acc, r, live(r)))

    def meta_block(acc):
        return block(km_ref[...], vm_ref[...], acc, r, lane < N_META, neg_suffix[:LANES, :LANES])[0]

    acc = lax.cond(alive, meta_block, lambda acc: acc, acc)

    sq = acc * acc
    s_lo = jnp.sum(jnp.where(lo, sq, 0.0), axis=-1, keepdims=True)
    s_hi = jnp.sum(jnp.where(lo, 0.0, sq), axis=-1, keepdims=True)
    rn = lax.rsqrt(jnp.where(lo, s_lo, s_hi) * (1.0 / SB_HEAD_DIM) + EPS)
    o_ref[0] = (acc * rn * g_ref[...]).astype(BF16)


def _attention(q, k, v, km, vm, og, tq):
    b, s, _ = q.shape
    n_pairs = SB_WIDTH // LANES
    return pl.pallas_call(
        functools.partial(_attn_kernel, tq=tq),
        grid=(b, n_pairs, s // tq),
        in_specs=[
            pl.BlockSpec((1, tq, LANES), lambda b, p, i: (b, i, p)),
            pl.BlockSpec((1, s, LANES), lambda b, p, i: (b, 0, p)),
            pl.BlockSpec((1, s, LANES), lambda b, p, i: (b, 0, p)),
            pl.BlockSpec((LANES, LANES), lambda b, p, i: (0, p)),
            pl.BlockSpec((LANES, LANES), lambda b, p, i: (0, p)),
            pl.BlockSpec((1, LANES), lambda b, p, i: (0, p)),
        ],
        out_specs=pl.BlockSpec((1, tq, LANES), lambda b, p, i: (b, i, p)),
        out_shape=jax.ShapeDtypeStruct((b, s, SB_WIDTH), BF16),
        compiler_params=_params(3),
        name="attention",
    )(q, k, v, km, vm, og)


def _mix_kernel(h_ref, halo_ref, mhalo_ref, sb_ref, x_ref, cw_ref, cb_ref, lg_ref, lb_ref, wo_ref,
                g2_ref, wr_ref, br_ref,
                xn_ref, eid_ref, gate_ref, rank_ref, before_ref, cnt_ref, rows_ref,
                win_ref, conv_ref, cv_ref, carry_ref, hn3_ref, posv_ref, pos_ref, dsem, psem, *, tm, chunk, cap):
    b = pl.program_id(0)
    i = pl.program_id(1)
    step = b * pl.num_programs(1) + i

    def pos_copy():
        return pltpu.make_async_copy(posv_ref, pos_ref, psem)

    def row_copy(t, k):
        return pltpu.make_async_copy(_row_tiles(hn3_ref, t), _row_tiles(rows_ref, pos_ref[0, k * tm + t]), dsem)

    @pl.when(step == 0)
    def _():
        carry_ref[...] = jnp.zeros_like(carry_ref)
        hn3_ref[...] = jnp.zeros_like(hn3_ref)
        posv_ref[...] = N_EXPERTS * cap + lax.broadcasted_iota(jnp.int32, posv_ref.shape, 1)
        pos_copy().start()

    pos_copy().wait()

    @pl.when(i == 0)
    def _():
        win_ref[0:HALO, :] = mhalo_ref[...]

    @pl.when(i > 0)
    def _():
        win_ref[0:HALO, :] = halo_ref[0]

    win_ref[HALO:, :] = h_ref[0]

    first_tap = HALO - (CONV_KERNEL - 1)

    def conv_chunk(c, _):
        r0 = pl.multiple_of(c * chunk, chunk)
        rows = chunk + HALO
        for lt in range(cb_ref.shape[1] // LANES):
            sl = slice(lt * LANES, (lt + 1) * LANES)
            window = win_ref[pl.ds(r0, rows), sl]
            acc = jnp.broadcast_to(cb_ref[:, sl], (chunk, LANES))
            for b in range(SUBLANES):
                shifted = window if b == 0 else pltpu.roll(window, shift=rows - b, axis=0)
                for a in range(HALO // SUBLANES + 1):
                    j = SUBLANES * a + b - first_tap
                    if 0 <= j < CONV_KERNEL:
                        acc = acc + cw_ref[j:j + 1, sl] * shifted[SUBLANES * a:SUBLANES * a + chunk, :]
            conv_ref[:, sl] = acc
        for tt in range(chunk):
            for k in range(TOP_K):
                row_copy(r0 + tt, k).start(priority=k % 2)
        acc = conv_ref[...]
        mu = jnp.mean(acc, axis=-1, keepdims=True)
        cen = acc - mu
        var = jnp.mean(cen * cen, axis=-1, keepdims=True)
        y = cen * lax.rsqrt(var + EPS) * lg_ref[...] + lb_ref[...]
        cv_ref[pl.ds(r0, chunk), :] = (y * jax.nn.sigmoid(y)).astype(BF16)
        return 0

    lax.fori_loop(0, tm // chunk, conv_chunk, 0)

    mixed = jnp.dot(sb_ref[0], wo_ref[0:SB_WIDTH, :], preferred_element_type=F32)
    mixed = mixed + jnp.dot(cv_ref[...], wo_ref[SB_WIDTH:, :], preferred_element_type=F32)
    xn = x_ref[0] + mixed
    xn_ref[0] = xn
    ms = jnp.mean(xn * xn, axis=-1, keepdims=True)
    hn = xn * lax.rsqrt(ms + EPS) * g2_ref[...]

    nt = (((1,), (1,)), ((), ()))
    hn_hi = hn.astype(BF16)
    hn_lo = (hn - hn_hi.astype(F32)).astype(BF16)
    both = lax.dot_general(wr_ref[...], hn_hi, nt, preferred_element_type=F32)
    logits = (both[:N_EXPERTS] + both[N_EXPERTS:]
              + lax.dot_general(wr_ref[0:N_EXPERTS, :], hn_lo, nt, preferred_element_type=F32) + br_ref[...])
    eidx = lax.broadcasted_iota(jnp.int32, logits.shape, 0)
    vals, idxs = [], []
    for _ in range(TOP_K):
        m = jnp.max(logits, axis=0, keepdims=True)
        sel = jnp.min(jnp.where(logits == m, eidx, N_EXPERTS), axis=0, keepdims=True)
        vals.append(m)
        idxs.append(sel)
        logits = jnp.where(eidx == sel, -jnp.inf, logits)
    exps = [jnp.exp(v - vals[0]) for v in vals]
    denom = exps[0] + exps[1] + exps[2] + exps[3]
    gate_ref[...] = jnp.concatenate([e / denom for e in exps], axis=0)
    eid_ref[...] = jnp.concatenate(idxs, axis=0)

    onehots = [(eidx == s).astype(F32) for s in idxs]
    chosen = onehots[0] + onehots[1] + onehots[2] + onehots[3]
    tr = lax.broadcasted_iota(jnp.int32, (tm, tm), 0)
    tc = lax.broadcasted_iota(jnp.int32, (tm, tm), 1)
    before = (tr < tc).astype(BF16)
    prefix = jnp.dot(chosen.astype(BF16), before, preferred_element_type=F32) + carry_ref[...]
    ranks = [jnp.sum(o * prefix, axis=0, keepdims=True).astype(jnp.int32) for o in onehots]
    rank_ref[...] = jnp.concatenate(ranks, axis=0)
    before_ref[...] = prefix.astype(jnp.int32)
    carry_ref[...] = carry_ref[...] + jnp.sum(chosen, axis=1, keepdims=True)
    cnt_ref[...] = jnp.broadcast_to(carry_ref[...], cnt_ref.shape)

    _wait_rows(rows_ref, TOP_K * tm, dsem)
    groups = hn.shape[1] // LANES
    for c in range(groups):
        hn3_ref[pl.ds(c, tm, stride=groups), :] = hn[:, c * LANES:(c + 1) * LANES]
    for k in range(TOP_K):
        posv_ref[:, k * tm:(k + 1) * tm] = idxs[k] * cap + ranks[k]
    pos_copy().start()

    @pl.when(step == pl.num_programs(0) * pl.num_programs(1) - 1)
    def _():
        pos_copy().wait()

        def issue(t, _):
            for k in range(TOP_K):
                row_copy(t, k).start(priority=k % 2)
            return 0

        lax.fori_loop(0, tm, issue, 0, unroll=2)
        _wait_rows(rows_ref, TOP_K * tm, dsem)


def _mix(hcv, mhalo, sb, x, conv_w, conv_b, ln_g, ln_b, w_out_bf, g2, wr_t, br, tm, chunk, cap):
    b, s, d = x.shape
    cw = hcv.shape[-1]
    t = b * s
    per = s // tm
    assert d == SUBLANES * LANES, d
    tile = lambda b, i: (b, i, 0)
    fixed = lambda b, i: (0, 0)
    tok = lambda b, i: (0, b * per + i)
    x_rows = N_EXPERTS * cap + TOP_K * tm
    return pl.pallas_call(
        functools.partial(_mix_kernel, tm=tm, chunk=chunk, cap=cap),
        grid=(b, per),
        in_specs=[
            pl.BlockSpec((1, tm, cw), tile),
            pl.BlockSpec((1, HALO, cw), lambda b, i: (b, jnp.maximum(i * (tm // HALO) - 1, 0), 0)),
            pl.BlockSpec((HALO, cw), fixed),
            pl.BlockSpec((1, tm, SB_WIDTH), tile),
            pl.BlockSpec((1, tm, d), tile),
            pl.BlockSpec((CONV_KERNEL, cw), fixed),
            pl.BlockSpec((1, cw), fixed),
            pl.BlockSpec((1, cw), fixed),
            pl.BlockSpec((1, cw), fixed),
            pl.BlockSpec((SB_WIDTH + cw, d), fixed),
            pl.BlockSpec((1, d), fixed),
            pl.BlockSpec((2 * N_EXPERTS, d), fixed),
            pl.BlockSpec((N_EXPERTS, 1), fixed),
        ],
        out_specs=[
            pl.BlockSpec((1, tm, d), tile),
            pl.BlockSpec((TOP_K, tm), tok),
            pl.BlockSpec((TOP_K, tm), tok),
            pl.BlockSpec((TOP_K, tm), tok),
            pl.BlockSpec((N_EXPERTS, tm), tok),
            pl.BlockSpec((N_EXPERTS, LANES), fixed),
            pl.BlockSpec(memory_space=pl.ANY),
        ],
        out_shape=[
            jax.ShapeDtypeStruct((b, s, d), F32),
            jax.ShapeDtypeStruct((TOP_K, t), jnp.int32),
            jax.ShapeDtypeStruct((TOP_K, t), F32),
            jax.ShapeDtypeStruct((TOP_K, t), jnp.int32),
            jax.ShapeDtypeStruct((N_EXPERTS, t), jnp.int32),
            jax.ShapeDtypeStruct((N_EXPERTS, LANES), F32),
            jax.ShapeDtypeStruct((x_rows * SUBLANES, LANES), F32),
        ],
        scratch_shapes=[
            pltpu.VMEM((tm + HALO, cw), F32),
            pltpu.VMEM((chunk, cw), F32),
            pltpu.VMEM((tm, cw), BF16),
            pltpu.VMEM((N_EXPERTS, 1), F32),
            pltpu.VMEM((tm * SUBLANES, LANES), F32),
            pltpu.VMEM((1, TOP_K * tm), jnp.int32),
            pltpu.SMEM((1, TOP_K * tm), jnp.int32),
            pltpu.SemaphoreType.DMA(()),
            pltpu.SemaphoreType.DMA(()),
        ],
        compiler_params=_params(2),
        name="mix_router",
    )(hcv, hcv, mhalo, sb, x, conv_w, conv_b, ln_g, ln_b, w_out_bf, g2, wr_t, br)


def _row_tiles(ref, row, n=1):
    size = n * SUBLANES if isinstance(n, int) else pl.multiple_of(n * SUBLANES, SUBLANES)
    return ref.at[pl.ds(pl.multiple_of(row * SUBLANES, SUBLANES), size)]


def _wait_rows(rows_ref, n, sem):
    pltpu.make_async_copy(_row_tiles(rows_ref, 0, n), _row_tiles(rows_ref, 0, n), sem).wait()


def _padfill_kernel(padlo_ref, padlen_ref, rows_in_ref, rows_ref, zeros_ref, zsem):
    del rows_in_ref
    zeros_ref[...] = jnp.zeros_like(zeros_ref)
    total = jnp.int32(0)
    for e in range(N_EXPERTS):
        lo, ln = padlo_ref[e], padlen_ref[e]
        total = total + ln
        for bit in range(EXPERT_ROWS.bit_length() - 1):
            size = 1 << bit

            @pl.when((ln >> bit) & 1 == 1)
            def _():
                start = lo + ((ln >> (bit + 1)) << (bit + 1))
                pltpu.make_async_copy(_row_tiles(zeros_ref, 0, size), _row_tiles(rows_ref, start, size),
                                      zsem).start()

    @pl.when(total > 0)
    def _():
        _wait_rows(rows_ref, total, zsem)


def _padfill(pad_lo, pad_len, x_rows):
    smem = pl.BlockSpec(memory_space=pltpu.SMEM)
    return pl.pallas_call(
        _padfill_kernel,
        in_specs=[smem, smem, pl.BlockSpec(memory_space=pl.ANY)],
        out_specs=pl.BlockSpec(memory_space=pl.ANY),
        out_shape=jax.ShapeDtypeStruct(x_rows.shape, x_rows.dtype),
        scratch_shapes=[pltpu.VMEM((EXPERT_ROWS * SUBLANES // 2, LANES), x_rows.dtype), pltpu.SemaphoreType.DMA(())],
        input_output_aliases={2: 0},
        compiler_params=pltpu.CompilerParams(vmem_limit_bytes=VMEM_LIMIT),
        name="padfill",
    )(pad_lo, pad_len, x_rows)


def _expert_kernel(be_ref, bsrc_ref, nused_ref, x_ref, wu_ref, bu_ref, wd_ref, bd_ref, y_ref,
                   wu_bf, wd_bf, *, cast_rows):
    del bsrc_ref
    j = pl.program_id(0)
    de = wd_ref.shape[1]

    @pl.when(j >= nused_ref[0])
    def _():
        y_ref[...] = jnp.zeros_like(y_ref)

    @pl.when(j < nused_ref[0])
    def _():
        prev = be_ref[jnp.maximum(j - 1, 0)]

        @pl.when(jnp.logical_or(j == 0, be_ref[j] != prev))
        def _():
            def cast(c, _):
                r0 = pl.multiple_of(c * cast_rows, cast_rows)
                wu_bf[pl.ds(r0, cast_rows), :] = wu_ref[0, pl.ds(r0, cast_rows), :].astype(BF16)
                wd_bf[pl.ds(r0, cast_rows), :] = wd_ref[0, pl.ds(r0, cast_rows), :].astype(BF16)
                return 0

            lax.fori_loop(0, wu_ref.shape[1] // cast_rows, cast, 0)

        x = jnp.concatenate([x_ref[pl.ds(c, EXPERT_ROWS, stride=SUBLANES), :] for c in range(SUBLANES)], axis=1)
        up = jnp.dot(x.astype(BF16), wu_bf[...], preferred_element_type=F32) + bu_ref[0]
        glu = jnp.minimum(up[:, :de], SWIGLU_LIMIT)
        lin = jnp.clip(up[:, de:], -SWIGLU_LIMIT, SWIGLU_LIMIT)
        act = glu * jax.nn.sigmoid(SWIGLU_ALPHA * glu) * (lin + 1.0)
        y = jnp.dot(act.astype(BF16), wd_bf[...], preferred_element_type=F32) + bd_ref[0]
        y_ref[...] = _pack_halves(y)


def _pack_halves(y):
    half = y.shape[1] // 2
    bits = lambda v: lax.bitcast_convert_type(v.astype(BF16).astype(F32), jnp.uint32)
    return (bits(y[:, half:]) & jnp.uint32(0xFFFF0000)) | (bits(y[:, :half]) >> 16)


def _unpack_halves(w):
    lo = lax.bitcast_convert_type(w << 16, F32).astype(BF16)
    hi = lax.bitcast_convert_type(w & jnp.uint32(0xFFFF0000), F32).astype(BF16)
    return lo, hi


def _experts(block_expert, block_src, n_used, x_rows, w_up, b_up, w_down, b_down):
    n_blocks = block_expert.shape[0] + 1
    ne, d, up_cols = w_up.shape
    de = w_down.shape[1]
    blk = lambda j, be, bs, nu: (bs[jnp.minimum(j, nu[0] - 1)], 0)
    wsel = lambda j, be, bs, nu: (be[jnp.minimum(j, nu[0] - 1)], 0, 0)
    grid_spec = pltpu.PrefetchScalarGridSpec(
        num_scalar_prefetch=3,
        grid=(n_blocks,),
        in_specs=[
            pl.BlockSpec((EXPERT_ROWS * SUBLANES, LANES), blk),
            pl.BlockSpec((1, d, up_cols), wsel),
            pl.BlockSpec((1, 1, up_cols), wsel),
            pl.BlockSpec((1, de, d), wsel),
            pl.BlockSpec((1, 1, d), wsel),
        ],
        out_specs=pl.BlockSpec((EXPERT_ROWS, d // 2), lambda j, be, bs, nu: (j, 0)),
        scratch_shapes=[pltpu.VMEM((d, up_cols), BF16), pltpu.VMEM((de, d), BF16)],
    )
    return pl.pallas_call(
        functools.partial(_expert_kernel, cast_rows=64),
        grid_spec=grid_spec,
        out_shape=jax.ShapeDtypeStruct((n_blocks * EXPERT_ROWS, d // 2), jnp.uint32),
        compiler_params=_params(1),
        name="experts",
    )(block_expert, block_src, n_used, x_rows, w_up, b_up.reshape(ne, 1, up_cols),
      w_down, b_down.reshape(ne, 1, d))


def _combine_plan(run_start, run_cnt, seg_start, eid, rank, tm, max_pieces):
    first = (run_start >> 3) << 3
    pieces = jnp.where(run_cnt > 0, (run_start - first + run_cnt + PIECE - 1) // PIECE, 0)
    piece_end = jnp.cumsum(pieces, axis=0)
    piece_off = piece_end - pieces
    p = jnp.arange(max_pieces, dtype=jnp.int32)
    expert_of = jnp.minimum(jnp.sum(p[None, :, None] >= piece_end.T[:, None, :], axis=-1), N_EXPERTS - 1)
    onehot = expert_of[:, :, None] == jnp.arange(N_EXPERTS, dtype=jnp.int32)
    src = jnp.sum(jnp.where(onehot, (first - piece_off * PIECE).T[:, None, :], 0), axis=-1) + p * PIECE
    shift = jnp.repeat((seg_start[:, None] + piece_off * PIECE - first).T, tm, axis=0)
    tok_onehot = eid[:, :, None] == jnp.arange(N_EXPERTS, dtype=jnp.int32)
    staged = rank + jnp.sum(jnp.where(tok_onehot, shift[None, :, :], 0), axis=-1)
    return src.reshape(-1).astype(jnp.int32), piece_end[-1].astype(jnp.int32), staged.T.astype(jnp.int32)


def _combine_kernel(src_ref, npieces_ref, staged_ref, gate_ref, xn_ref, rows_ref, o_ref, buf, sems,
                    *, tm, max_pieces):
    i = pl.program_id(0)
    n = pl.num_programs(0)
    slot = lax.rem(i, 2)
    half = o_ref.shape[1] // 2

    def fetch(tile, s):
        def body(p, _):
            src = rows_ref.at[pl.ds(pl.multiple_of(src_ref[tile * max_pieces + p], SUBLANES), PIECE)]
            dst = buf.at[s, pl.ds(pl.multiple_of(p * PIECE, PIECE), PIECE)]
            pltpu.make_async_copy(src, dst, sems.at[s]).start()
            return 0

        lax.fori_loop(0, npieces_ref[tile], body, 0)

    @pl.when(i == 0)
    def _():
        buf[...] = jnp.zeros_like(buf)
        fetch(0, 0)

    @pl.when(i + 1 < n)
    def _():
        fetch(i + 1, 1 - slot)

    total = npieces_ref[i] * PIECE
    staged = staged_ref[...]
    gates = gate_ref[...]

    @pl.when(total > 0)
    def _():
        rows = pl.ds(0, pl.multiple_of(total, PIECE))
        pltpu.make_async_copy(rows_ref.at[rows], buf.at[slot, rows], sems.at[slot]).wait()

    def chunk_products(c0):
        col = lax.broadcasted_iota(jnp.int32, (tm, COMBINE_CHUNK), 1) + c0
        sel = jnp.zeros((tm, COMBINE_CHUNK), F32)
        for k in range(TOP_K):
            sel = jnp.where(staged[:, k:k + 1] == col, gates[:, k:k + 1], sel)
        sel = sel.astype(BF16)
        lo, hi = _unpack_halves(buf[slot, pl.ds(c0, COMBINE_CHUNK), :])
        return jnp.dot(sel, lo, preferred_element_type=F32), jnp.dot(sel, hi, preferred_element_type=F32)

    xn = xn_ref[...]
    acc_lo, acc_hi = xn[:, :half], xn[:, half:]
    for c in range(COMBINE_STATIC_CHUNKS):
        d_lo, d_hi = chunk_products(c * COMBINE_CHUNK)
        acc_lo, acc_hi = acc_lo + d_lo, acc_hi + d_hi
    o_ref[:, :half] = acc_lo
    o_ref[:, half:] = acc_hi

    def chunk(c, _):
        d_lo, d_hi = chunk_products(pl.multiple_of(c * COMBINE_CHUNK, COMBINE_CHUNK))
        o_ref[:, :half] += d_lo
        o_ref[:, half:] += d_hi
        return 0

    lax.fori_loop(COMBINE_STATIC_CHUNKS, (total + COMBINE_CHUNK - 1) // COMBINE_CHUNK, chunk, 0)


def _combine(run_start, run_cnt, seg_start, eid, rank, gates_t, xn, y_rows, tm):
    t, d = xn.shape
    n = t // tm
    cap = -(-(TOP_K * tm + 2 * PIECE * N_EXPERTS) // COMBINE_CHUNK) * COMBINE_CHUNK
    max_pieces = cap // PIECE
    src, npieces, staged_t = _combine_plan(run_start, run_cnt, seg_start, eid, rank, tm, max_pieces)
    tok = lambda i, *_: (i, 0)
    grid_spec = pltpu.PrefetchScalarGridSpec(
        num_scalar_prefetch=2,
        grid=(n,),
        in_specs=[
            pl.BlockSpec((tm, TOP_K), tok),
            pl.BlockSpec((tm, TOP_K), tok),
            pl.BlockSpec((tm, d), tok),
            pl.BlockSpec(memory_space=pl.ANY),
        ],
        out_specs=pl.BlockSpec((tm, d), tok),
        scratch_shapes=[pltpu.VMEM((2, cap, d // 2), jnp.uint32), pltpu.SemaphoreType.DMA((2,))],
    )
    return pl.pallas_call(
        functools.partial(_combine_kernel, tm=tm, max_pieces=max_pieces),
        grid_spec=grid_spec,
        out_shape=jax.ShapeDtypeStruct((t, d), F32),
        compiler_params=_params(1),
        name="combine",
    )(src, npieces, staged_t, gates_t, xn, y_rows)


def _tile(n, want):
    t = min(n, want)
    assert n % t == 0, (n, t)
    return t


def kernel(x, meta_tokens, norm1_g, w_in, q_norm_g, k_norm_g, conv_w, conv_b, conv_ln_g, conv_ln_b,
           attn_out_g, w_out, norm2_g, w_router, b_router, w_up, b_up, w_down, b_down):
    assert norm1_g.shape[0] == 1, "single layer: meta-token rows are only keys/values and conv context"
    b, s, d = x.shape
    t = b * s
    cw = conv_w.shape[-1]

    g1 = norm1_g[0][None, :]
    w_in_bf = w_in[0].astype(BF16)
    qg = jnp.tile(q_norm_g[0], SB_HEADS)[None, :]
    kg = jnp.tile(k_norm_g[0], SB_HEADS)[None, :]

    q, k, v, hcv = _inproj(x.reshape(t, d), g1, w_in_bf, qg, kg, _tile(t, 512))
    _, km, vm, hm = _inproj(meta_tokens, g1, w_in_bf, qg, kg, N_META)
    pad = ((0, LANES - N_META), (0, 0))
    sb = _attention(q.reshape(b, s, SB_WIDTH), k.reshape(b, s, SB_WIDTH), v.reshape(b, s, SB_WIDTH),
                    jnp.pad(km, pad), jnp.pad(vm, pad), attn_out_g[0].reshape(1, SB_WIDTH), _tile(s, 256))

    mhalo = jnp.concatenate([jnp.zeros((HALO - N_META, cw), F32), hm], axis=0)
    tm = _tile(s, 512)
    wr_t = w_router[0].T
    wr_hi = wr_t.astype(BF16)
    wr_lo = (wr_t - wr_hi.astype(F32)).astype(BF16)
    cap = -(-t // EXPERT_ROWS) * EXPERT_ROWS
    xn, eid, gates, rank, before, counts, x_rows = _mix(
        hcv.reshape(b, s, cw), mhalo, sb, x, conv_w[0], conv_b[0][None, :], conv_ln_g[0][None, :],
        conv_ln_b[0][None, :], w_out[0].astype(BF16), norm2_g[0][None, :],
        jnp.concatenate([wr_hi, wr_lo], axis=0), b_router[0][:, None], tm, _tile(tm, 64), cap)

    counts = counts[:, 0].astype(jnp.int32)
    padded = (counts + EXPERT_ROWS - 1) // EXPERT_ROWS * EXPERT_ROWS
    pad_end = jnp.cumsum(padded)
    pad_start = pad_end - padded
    n_blocks = t * TOP_K // EXPERT_ROWS + N_EXPERTS
    block_row = jnp.arange(n_blocks, dtype=jnp.int32) * EXPERT_ROWS
    block_expert = jnp.minimum(jnp.sum(block_row[:, None] >= pad_end[None, :], axis=1), N_EXPERTS - 1).astype(jnp.int32)
    block_src = (block_expert * cap + jnp.maximum(block_row - pad_start[block_expert], 0)) // EXPERT_ROWS
    n_used = (pad_end[-1:] // EXPERT_ROWS).astype(jnp.int32)
    seg = jnp.arange(N_EXPERTS, dtype=jnp.int32) * cap
    x_rows = _padfill(seg + counts, padded - counts, x_rows)
    y_rows = _experts(block_expert, block_src.astype(jnp.int32), n_used, x_rows, w_up[0], b_up[0], w_down[0],
                      b_down[0])
    tc = _tile(t, 256)
    run_start = pad_start[:, None] + before[:, ::tc]
    run_end = jnp.concatenate([run_start[:, 1:], (pad_start + counts)[:, None]], axis=1)
    out = _combine(run_start, run_end - run_start, pad_start, eid, rank, gates.T, xn.reshape(t, d), y_rows, tc)
    return out.reshape(b, s, d)
```

```python
import functools

import jax
import jax.numpy as jnp
from jax import lax
from jax.experimental import pallas as pl
from jax.experimental.pallas import tpu as pltpu

N_META = 16
SB_HEADS = 8
SB_HEAD_DIM = 64
SB_WIDTH = SB_HEADS * SB_HEAD_DIM
CONV_KERNEL = 31
N_EXPERTS = 32
TOP_K = 4
SWIGLU_LIMIT = 7.0
SWIGLU_ALPHA = 1.702
EPS = 1e-6
F32_EXP2_UNDERFLOW = -150.0
LOG2_E = 1.4426950408889634

LANES = 128
SUBLANES = 8
ATTN_TILES_PER_STEP = 4
PIECE = 16
COMBINE_CHUNK = 512
COMBINE_STATIC_CHUNKS = 3
HALO = 32
EXPERT_ROWS = 512
VMEM_LIMIT = 56 * 1024 * 1024

F32 = jnp.float32
BF16 = jnp.bfloat16


def _params(n_axes, vmem=VMEM_LIMIT):
    return pltpu.CompilerParams(dimension_semantics=("arbitrary",) * n_axes, vmem_limit_bytes=vmem)


def _inproj_kernel(x_ref, g1_ref, w_ref, qg_ref, kg_ref, q_ref, k_ref, v_ref, h_ref):
    x = x_ref[...]
    ms = jnp.mean(x * x, axis=-1, keepdims=True)
    n = (x * lax.rsqrt(ms + EPS) * g1_ref[...]).astype(BF16)
    lo = lax.broadcasted_iota(jnp.int32, (1, LANES), 1) < SB_HEAD_DIM

    def proj(c0, c1):
        return jnp.dot(n, w_ref[:, c0:c1], preferred_element_type=F32)

    def head_norm(acc, g_ref, out_ref, scale):
        for c in range(SB_WIDTH // LANES):
            sl = slice(c * LANES, (c + 1) * LANES)
            a = acc[:, sl]
            sq = a * a
            s_lo = jnp.sum(jnp.where(lo, sq, 0.0), axis=-1, keepdims=True)
            s_hi = jnp.sum(jnp.where(lo, 0.0, sq), axis=-1, keepdims=True)
            r = lax.rsqrt(jnp.where(lo, s_lo, s_hi) * (1.0 / SB_HEAD_DIM) + EPS)
            out_ref[:, sl] = (a * r * (g_ref[:, sl] * scale)).astype(BF16)

    head_norm(proj(0, SB_WIDTH), qg_ref, q_ref, LOG2_E * SB_HEAD_DIM ** -0.5)
    head_norm(proj(SB_WIDTH, 2 * SB_WIDTH), kg_ref, k_ref, 1.0)
    v_ref[...] = proj(2 * SB_WIDTH, 3 * SB_WIDTH).astype(BF16)
    cw = (w_ref.shape[1] - 3 * SB_WIDTH) // 2
    val = proj(3 * SB_WIDTH, 3 * SB_WIDTH + cw)
    gate = proj(3 * SB_WIDTH + cw, 3 * SB_WIDTH + 2 * cw)
    h_ref[...] = val * jax.nn.sigmoid(gate)


def _inproj(x2, g1, w_in_bf, qg, kg, tm):
    t, d = x2.shape
    cols = w_in_bf.shape[1]
    cw = (cols - 3 * SB_WIDTH) // 2
    row = lambda i: (i, 0)
    fixed = lambda i: (0, 0)
    return pl.pallas_call(
        _inproj_kernel,
        grid=(t // tm,),
        in_specs=[
            pl.BlockSpec((tm, d), row),
            pl.BlockSpec((1, d), fixed),
            pl.BlockSpec((d, cols), fixed),
            pl.BlockSpec((1, SB_WIDTH), fixed),
            pl.BlockSpec((1, SB_WIDTH), fixed),
        ],
        out_specs=[
            pl.BlockSpec((tm, SB_WIDTH), row),
            pl.BlockSpec((tm, SB_WIDTH), row),
            pl.BlockSpec((tm, SB_WIDTH), row),
            pl.BlockSpec((tm, cw), row),
        ],
        out_shape=[
            jax.ShapeDtypeStruct((t, SB_WIDTH), BF16),
            jax.ShapeDtypeStruct((t, SB_WIDTH), BF16),
            jax.ShapeDtypeStruct((t, SB_WIDTH), BF16),
            jax.ShapeDtypeStruct((t, cw), F32),
        ],
        compiler_params=_params(1),
        name="inproj",
    )(x2, g1, w_in_bf, qg, kg)


def _attn_kernel(q_ref, k_ref, v_ref, km_ref, vm_ref, g_ref, o_ref, *, tq, nsub):
    i = pl.program_id(2)
    lane = lax.broadcasted_iota(jnp.int32, (1, LANES), 1)
    lo = lane < SB_HEAD_DIM
    zero_bf = jnp.zeros((), BF16)

    def stacked_q(sub):
        q2 = q_ref[0, sub * tq:(sub + 1) * tq, :]
        return jnp.concatenate([jnp.where(lo, q2, zero_bf), jnp.where(lo, zero_bf, q2)], axis=0)

    qss = [stacked_q(sub) for sub in range(nsub)]
    row = lax.broadcasted_iota(jnp.int32, (2 * tq, tq), 0)
    col = lax.broadcasted_iota(jnp.int32, (2 * tq, tq), 1)
    causal = col < jnp.where(row >= tq, row - tq, row)
    srow = lax.broadcasted_iota(jnp.int32, (tq, tq), 0)
    scol = lax.broadcasted_iota(jnp.int32, (tq, tq), 1)
    neg_suffix = jnp.where(srow > scol, -1.0, 0.0).astype(BF16)

    def block(qs, kb, vb, acc, r, mask, neg_suffix_m):
        z = lax.dot_general(qs, kb, (((1,), (1,)), ((), ())), preferred_element_type=F32)
        m = jnp.minimum(z, 0.0)
        p = jnp.maximum(z, 0.0)
        l1p = jnp.log2(1.0 + jnp.exp2(m - p))
        log_beta = m - l1p
        neg_keep = p + l1p
        if mask is not None:
            neg_keep = jnp.where(mask, neg_keep, 0.0)
        later = jnp.dot(neg_keep.astype(BF16), neg_suffix_m, preferred_element_type=F32)
        a = jnp.exp2(log_beta + later + r)
        if mask is not None:
            a = jnp.where(mask, a, 0.0)
        res = jnp.dot(a.astype(BF16), vb, preferred_element_type=F32)
        acc = acc + jnp.where(lo, res[:tq], res[tq:])
        return acc, r - jnp.sum(neg_keep, axis=-1, keepdims=True)

    def kv_block(j):
        start = pl.multiple_of(j * tq, tq)
        return k_ref[0, pl.ds(start, tq), :], v_ref[0, pl.ds(start, tq), :]

    def live(r):
        return jnp.max(r) > F32_EXP2_UNDERFLOW

    acc0 = jnp.zeros((tq, LANES), F32)
    r0 = jnp.zeros((2 * tq, 1), F32)
    first = i * nsub

    def guaranteed(first_has_previous):
        state = []
        for sub in range(nsub):
            acc, r = block(qss[sub], *kv_block(first + sub), acc0, r0, causal, neg_suffix)
            if sub > 0 or first_has_previous:
                acc, r = block(qss[sub], *kv_block(first + sub - 1), acc, r, None, neg_suffix)
            state += [acc, r]
        return tuple(state)

    state = lax.cond(i > 0, lambda _: guaranteed(True), lambda _: guaranteed(False), 0)

    for sub in range(nsub):
        j = first + sub
        qs, acc, r = qss[sub], state[2 * sub], state[2 * sub + 1]

        def cond(c):
            return jnp.logical_and(c[0] <= j, c[3])

        def body(c):
            acc, r = block(qs, *kv_block(j - c[0]), c[1], c[2], None, neg_suffix)
            return c[0] + 1, acc, r, live(r)

        _, acc, r, alive = lax.while_loop(cond, body, (jnp.int32(2), acc, r, live(r)))

        def meta_block(acc):
            return block(qs, km_ref[...], vm_ref[...], acc, r, lane < N_META, neg_suffix[:LANES, :LANES])[0]

        acc = lax.cond(alive, meta_block, lambda acc: acc, acc)

        sq = acc * acc
        s_lo = jnp.sum(jnp.where(lo, sq, 0.0), axis=-1, keepdims=True)
        s_hi = jnp.sum(jnp.where(lo, 0.0, sq), axis=-1, keepdims=True)
        rn = lax.rsqrt(jnp.where(lo, s_lo, s_hi) * (1.0 / SB_HEAD_DIM) + EPS)
        o_ref[0, sub * tq:(sub + 1) * tq, :] = (acc * rn * g_ref[...]).astype(BF16)


def _attention(q, k, v, km, vm, og, tq, nsub):
    b, s, _ = q.shape
    n_pairs = SB_WIDTH // LANES
    step = tq * nsub
    return pl.pallas_call(
        functools.partial(_attn_kernel, tq=tq, nsub=nsub),
        grid=(b, n_pairs, s // step),
        in_specs=[
            pl.BlockSpec((1, step, LANES), lambda b, p, i: (b, i, p)),
            pl.BlockSpec((1, s, LANES), lambda b, p, i: (b, 0, p)),
            pl.BlockSpec((1, s, LANES), lambda b, p, i: (b, 0, p)),
            pl.BlockSpec((LANES, LANES), lambda b, p, i: (0, p)),
            pl.BlockSpec((LANES, LANES), lambda b, p, i: (0, p)),
            pl.BlockSpec((1, LANES), lambda b, p, i: (0, p)),
        ],
        out_specs=pl.BlockSpec((1, step, LANES), lambda b, p, i: (b, i, p)),
        out_shape=jax.ShapeDtypeStruct((b, s, SB_WIDTH), BF16),
        compiler_params=_params(3),
        name="attention",
    )(q, k, v, km, vm, og)


def _mix_kernel(h_ref, halo_ref, mhalo_ref, sb_ref, x_ref, cw_ref, cb_ref, lg_ref, lb_ref, wo_ref,
                g2_ref, wr_ref, br_ref,
                xn_ref, eid_ref, gate_ref, rank_ref, before_ref, cnt_ref, rows_ref,
                win_ref, conv_ref, cv_ref, carry_ref, hn3_ref, posv_ref, pos_ref, dsem, psem, *, tm, chunk, cap):
    b = pl.program_id(0)
    i = pl.program_id(1)
    step = b * pl.num_programs(1) + i

    def pos_copy():
        return pltpu.make_async_copy(posv_ref, pos_ref, psem)

    def row_copy(t, k):
        return pltpu.make_async_copy(_row_tiles(hn3_ref, t), _row_tiles(rows_ref, pos_ref[0, k * tm + t]), dsem)

    @pl.when(step == 0)
    def _():
        carry_ref[...] = jnp.zeros_like(carry_ref)
        hn3_ref[...] = jnp.zeros_like(hn3_ref)
        posv_ref[...] = N_EXPERTS * cap + lax.broadcasted_iota(jnp.int32, posv_ref.shape, 1)
        pos_copy().start()

    pos_copy().wait()

    @pl.when(i == 0)
    def _():
        win_ref[0:HALO, :] = mhalo_ref[...]

    @pl.when(i > 0)
    def _():
        win_ref[0:HALO, :] = halo_ref[0]

    win_ref[HALO:, :] = h_ref[0]

    first_tap = HALO - (CONV_KERNEL - 1)

    def conv_chunk(c, _):
        r0 = pl.multiple_of(c * chunk, chunk)
        rows = chunk + HALO
        for lt in range(cb_ref.shape[1] // LANES):
            sl = slice(lt * LANES, (lt + 1) * LANES)
            window = win_ref[pl.ds(r0, rows), sl]
            acc = jnp.broadcast_to(cb_ref[:, sl], (chunk, LANES))
            for b in range(SUBLANES):
                shifted = window if b == 0 else pltpu.roll(window, shift=rows - b, axis=0)
                for a in range(HALO // SUBLANES + 1):
                    j = SUBLANES * a + b - first_tap
                    if 0 <= j < CONV_KERNEL:
                        acc = acc + cw_ref[j:j + 1, sl] * shifted[SUBLANES * a:SUBLANES * a + chunk, :]
            conv_ref[:, sl] = acc
        for tt in range(chunk):
            for k in range(TOP_K):
                row_copy(r0 + tt, k).start(priority=k % 2)
        acc = conv_ref[...]
        mu = jnp.mean(acc, axis=-1, keepdims=True)
        cen = acc - mu
        var = jnp.mean(cen * cen, axis=-1, keepdims=True)
        y = cen * lax.rsqrt(var + EPS) * lg_ref[...] + lb_ref[...]
        cv_ref[pl.ds(r0, chunk), :] = (y * jax.nn.sigmoid(y)).astype(BF16)
        return 0

    lax.fori_loop(0, tm // chunk, conv_chunk, 0)

    mixed = jnp.dot(sb_ref[0], wo_ref[0:SB_WIDTH, :], preferred_element_type=F32)
    mixed = mixed + jnp.dot(cv_ref[...], wo_ref[SB_WIDTH:, :], preferred_element_type=F32)
    xn = x_ref[0] + mixed
    xn_ref[0] = xn
    ms = jnp.mean(xn * xn, axis=-1, keepdims=True)
    hn = xn * lax.rsqrt(ms + EPS) * g2_ref[...]

    nt = (((1,), (1,)), ((), ()))
    hn_hi = hn.astype(BF16)
    hn_lo = (hn - hn_hi.astype(F32)).astype(BF16)
    both = lax.dot_general(wr_ref[...], hn_hi, nt, preferred_element_type=F32)
    logits = (both[:N_EXPERTS] + both[N_EXPERTS:]
              + lax.dot_general(wr_ref[0:N_EXPERTS, :], hn_lo, nt, preferred_element_type=F32) + br_ref[...])
    eidx = lax.broadcasted_iota(jnp.int32, logits.shape, 0)
    vals, idxs = [], []
    for _ in range(TOP_K):
        m = jnp.max(logits, axis=0, keepdims=True)
        sel = jnp.min(jnp.where(logits == m, eidx, N_EXPERTS), axis=0, keepdims=True)
        vals.append(m)
        idxs.append(sel)
        logits = jnp.where(eidx == sel, -jnp.inf, logits)
    exps = [jnp.exp(v - vals[0]) for v in vals]
    denom = exps[0] + exps[1] + exps[2] + exps[3]
    gate_ref[...] = jnp.concatenate([e / denom for e in exps], axis=0)
    eid_ref[...] = jnp.concatenate(idxs, axis=0)

    onehots = [(eidx == s).astype(F32) for s in idxs]
    chosen = onehots[0] + onehots[1] + onehots[2] + onehots[3]
    tr = lax.broadcasted_iota(jnp.int32, (tm, tm), 0)
    tc = lax.broadcasted_iota(jnp.int32, (tm, tm), 1)
    before = (tr < tc).astype(BF16)
    prefix = jnp.dot(chosen.astype(BF16), before, preferred_element_type=F32) + carry_ref[...]
    ranks = [jnp.sum(o * prefix, axis=0, keepdims=True).astype(jnp.int32) for o in onehots]
    rank_ref[...] = jnp.concatenate(ranks, axis=0)
    before_ref[...] = prefix.astype(jnp.int32)
    carry_ref[...] = carry_ref[...] + jnp.sum(chosen, axis=1, keepdims=True)
    cnt_ref[...] = jnp.broadcast_to(carry_ref[...], cnt_ref.shape)

    _wait_rows(rows_ref, TOP_K * tm, dsem)
    groups = hn.shape[1] // LANES
    for c in range(groups):
        hn3_ref[pl.ds(c, tm, stride=groups), :] = hn[:, c * LANES:(c + 1) * LANES]
    for k in range(TOP_K):
        posv_ref[:, k * tm:(k + 1) * tm] = idxs[k] * cap + ranks[k]
    pos_copy().start()

    @pl.when(step == pl.num_programs(0) * pl.num_programs(1) - 1)
    def _():
        pos_copy().wait()

        def issue(t, _):
            for k in range(TOP_K):
                row_copy(t, k).start(priority=k % 2)
            return 0

        lax.fori_loop(0, tm, issue, 0, unroll=2)
        _wait_rows(rows_ref, TOP_K * tm, dsem)


def _mix(hcv, mhalo, sb, x, conv_w, conv_b, ln_g, ln_b, w_out_bf, g2, wr_t, br, tm, chunk, cap):
    b, s, d = x.shape
    cw = hcv.shape[-1]
    t = b * s
    per = s // tm
    assert d == SUBLANES * LANES, d
    tile = lambda b, i: (b, i, 0)
    fixed = lambda b, i: (0, 0)
    tok = lambda b, i: (0, b * per + i)
    x_rows = N_EXPERTS * cap + TOP_K * tm
    return pl.pallas_call(
        functools.partial(_mix_kernel, tm=tm, chunk=chunk, cap=cap),
        grid=(b, per),
        in_specs=[
            pl.BlockSpec((1, tm, cw), tile),
            pl.BlockSpec((1, HALO, cw), lambda b, i: (b, jnp.maximum(i * (tm // HALO) - 1, 0), 0)),
            pl.BlockSpec((HALO, cw), fixed),
            pl.BlockSpec((1, tm, SB_WIDTH), tile),
            pl.BlockSpec((1, tm, d), tile),
            pl.BlockSpec((CONV_KERNEL, cw), fixed),
            pl.BlockSpec((1, cw), fixed),
            pl.BlockSpec((1, cw), fixed),
            pl.BlockSpec((1, cw), fixed),
            pl.BlockSpec((SB_WIDTH + cw, d), fixed),
            pl.BlockSpec((1, d), fixed),
            pl.BlockSpec((2 * N_EXPERTS, d), fixed),
            pl.BlockSpec((N_EXPERTS, 1), fixed),
        ],
        out_specs=[
            pl.BlockSpec((1, tm, d), tile),
            pl.BlockSpec((TOP_K, tm), tok),
            pl.BlockSpec((TOP_K, tm), tok),
            pl.BlockSpec((TOP_K, tm), tok),
            pl.BlockSpec((N_EXPERTS, tm), tok),
            pl.BlockSpec((N_EXPERTS, LANES), fixed),
            pl.BlockSpec(memory_space=pl.ANY),
        ],
        out_shape=[
            jax.ShapeDtypeStruct((b, s, d), F32),
            jax.ShapeDtypeStruct((TOP_K, t), jnp.int32),
            jax.ShapeDtypeStruct((TOP_K, t), F32),
            jax.ShapeDtypeStruct((TOP_K, t), jnp.int32),
            jax.ShapeDtypeStruct((N_EXPERTS, t), jnp.int32),
            jax.ShapeDtypeStruct((N_EXPERTS, LANES), F32),
            jax.ShapeDtypeStruct((x_rows * SUBLANES, LANES), F32),
        ],
        scratch_shapes=[
            pltpu.VMEM((tm + HALO, cw), F32),
            pltpu.VMEM((chunk, cw), F32),
            pltpu.VMEM((tm, cw), BF16),
            pltpu.VMEM((N_EXPERTS, 1), F32),
            pltpu.VMEM((tm * SUBLANES, LANES), F32),
            pltpu.VMEM((1, TOP_K * tm), jnp.int32),
            pltpu.SMEM((1, TOP_K * tm), jnp.int32),
            pltpu.SemaphoreType.DMA(()),
            pltpu.SemaphoreType.DMA(()),
        ],
        compiler_params=_params(2),
        name="mix_router",
    )(hcv, hcv, mhalo, sb, x, conv_w, conv_b, ln_g, ln_b, w_out_bf, g2, wr_t, br)


def _row_tiles(ref, row, n=1):
    size = n * SUBLANES if isinstance(n, int) else pl.multiple_of(n * SUBLANES, SUBLANES)
    return ref.at[pl.ds(pl.multiple_of(row * SUBLANES, SUBLANES), size)]


def _wait_rows(rows_ref, n, sem):
    pltpu.make_async_copy(_row_tiles(rows_ref, 0, n), _row_tiles(rows_ref, 0, n), sem).wait()


def _padfill_kernel(padlo_ref, padlen_ref, rows_in_ref, rows_ref, zeros_ref, zsem):
    del rows_in_ref
    zeros_ref[...] = jnp.zeros_like(zeros_ref)
    total = jnp.int32(0)
    for e in range(N_EXPERTS):
        lo, ln = padlo_ref[e], padlen_ref[e]
        total = total + ln
        for bit in range(EXPERT_ROWS.bit_length() - 1):
            size = 1 << bit

            @pl.when((ln >> bit) & 1 == 1)
            def _():
                start = lo + ((ln >> (bit + 1)) << (bit + 1))
                pltpu.make_async_copy(_row_tiles(zeros_ref, 0, size), _row_tiles(rows_ref, start, size),
                                      zsem).start()

    @pl.when(total > 0)
    def _():
        _wait_rows(rows_ref, total, zsem)


def _padfill(pad_lo, pad_len, x_rows):
    smem = pl.BlockSpec(memory_space=pltpu.SMEM)
    return pl.pallas_call(
        _padfill_kernel,
        in_specs=[smem, smem, pl.BlockSpec(memory_space=pl.ANY)],
        out_specs=pl.BlockSpec(memory_space=pl.ANY),
        out_shape=jax.ShapeDtypeStruct(x_rows.shape, x_rows.dtype),
        scratch_shapes=[pltpu.VMEM((EXPERT_ROWS * SUBLANES // 2, LANES), x_rows.dtype), pltpu.SemaphoreType.DMA(())],
        input_output_aliases={2: 0},
        compiler_params=pltpu.CompilerParams(vmem_limit_bytes=VMEM_LIMIT),
        name="padfill",
    )(pad_lo, pad_len, x_rows)


def _expert_kernel(be_ref, bsrc_ref, nused_ref, x_ref, wu_ref, bu_ref, wd_ref, bd_ref, y_ref,
                   wu_bf, wd_bf, *, cast_rows):
    del bsrc_ref
    j = pl.program_id(0)
    de = wd_ref.shape[1]

    @pl.when(j >= nused_ref[0])
    def _():
        y_ref[...] = jnp.zeros_like(y_ref)

    @pl.when(j < nused_ref[0])
    def _():
        prev = be_ref[jnp.maximum(j - 1, 0)]

        @pl.when(jnp.logical_or(j == 0, be_ref[j] != prev))
        def _():
            def cast(c, _):
                r0 = pl.multiple_of(c * cast_rows, cast_rows)
                wu_bf[pl.ds(r0, cast_rows), :] = wu_ref[0, pl.ds(r0, cast_rows), :].astype(BF16)
                wd_bf[pl.ds(r0, cast_rows), :] = wd_ref[0, pl.ds(r0, cast_rows), :].astype(BF16)
                return 0

            lax.fori_loop(0, wu_ref.shape[1] // cast_rows, cast, 0)

        x = jnp.concatenate([x_ref[pl.ds(c, EXPERT_ROWS, stride=SUBLANES), :] for c in range(SUBLANES)], axis=1)
        up = jnp.dot(x.astype(BF16), wu_bf[...], preferred_element_type=F32) + bu_ref[0]
        glu = jnp.minimum(up[:, :de], SWIGLU_LIMIT)
        lin = jnp.clip(up[:, de:], -SWIGLU_LIMIT, SWIGLU_LIMIT)
        act = glu * jax.nn.sigmoid(SWIGLU_ALPHA * glu) * (lin + 1.0)
        y = jnp.dot(act.astype(BF16), wd_bf[...], preferred_element_type=F32) + bd_ref[0]
        y_ref[...] = _pack_halves(y)


def _pack_halves(y):
    half = y.shape[1] // 2
    bits = lambda v: lax.bitcast_convert_type(v.astype(BF16).astype(F32), jnp.uint32)
    return (bits(y[:, half:]) & jnp.uint32(0xFFFF0000)) | (bits(y[:, :half]) >> 16)


def _unpack_halves(w):
    lo = lax.bitcast_convert_type(w << 16, F32).astype(BF16)
    hi = lax.bitcast_convert_type(w & jnp.uint32(0xFFFF0000), F32).astype(BF16)
    return lo, hi


def _experts(block_expert, block_src, n_used, x_rows, w_up, b_up, w_down, b_down):
    n_blocks = block_expert.shape[0] + 1
    ne, d, up_cols = w_up.shape
    de = w_down.shape[1]
    blk = lambda j, be, bs, nu: (bs[jnp.minimum(j, nu[0] - 1)], 0)
    wsel = lambda j, be, bs, nu: (be[jnp.minimum(j, nu[0] - 1)], 0, 0)
    grid_spec = pltpu.PrefetchScalarGridSpec(
        num_scalar_prefetch=3,
        grid=(n_blocks,),
        in_specs=[
            pl.BlockSpec((EXPERT_ROWS * SUBLANES, LANES), blk),
            pl.BlockSpec((1, d, up_cols), wsel),
            pl.BlockSpec((1, 1, up_cols), wsel),
            pl.BlockSpec((1, de, d), wsel),
            pl.BlockSpec((1, 1, d), wsel),
        ],
        out_specs=pl.BlockSpec((EXPERT_ROWS, d // 2), lambda j, be, bs, nu: (j, 0)),
        scratch_shapes=[pltpu.VMEM((d, up_cols), BF16), pltpu.VMEM((de, d), BF16)],
    )
    return pl.pallas_call(
        functools.partial(_expert_kernel, cast_rows=64),
        grid_spec=grid_spec,
        out_shape=jax.ShapeDtypeStruct((n_blocks * EXPERT_ROWS, d // 2), jnp.uint32),
        compiler_params=_params(1),
        name="experts",
    )(block_expert, block_src, n_used, x_rows, w_up, b_up.reshape(ne, 1, up_cols),
      w_down, b_down.reshape(ne, 1, d))


def _combine_plan(run_start, run_cnt, seg_start, eid, rank, tm, max_pieces):
    first = (run_start >> 3) << 3
    pieces = jnp.where(run_cnt > 0, (run_start - first + run_cnt + PIECE - 1) // PIECE, 0)
    piece_end = jnp.cumsum(pieces, axis=0)
    piece_off = piece_end - pieces
    p = jnp.arange(max_pieces, dtype=jnp.int32)
    expert_of = jnp.minimum(jnp.sum(p[None, :, None] >= piece_end.T[:, None, :], axis=-1), N_EXPERTS - 1)
    onehot = expert_of[:, :, None] == jnp.arange(N_EXPERTS, dtype=jnp.int32)
    src = jnp.sum(jnp.where(onehot, (first - piece_off * PIECE).T[:, None, :], 0), axis=-1) + p * PIECE
    shift = jnp.repeat((seg_start[:, None] + piece_off * PIECE - first).T, tm, axis=0)
    tok_onehot = eid[:, :, None] == jnp.arange(N_EXPERTS, dtype=jnp.int32)
    staged = rank + jnp.sum(jnp.where(tok_onehot, shift[None, :, :], 0), axis=-1)
    return src.reshape(-1).astype(jnp.int32), piece_end[-1].astype(jnp.int32), staged.T.astype(jnp.int32)


def _combine_kernel(src_ref, npieces_ref, staged_ref, gate_ref, xn_ref, rows_ref, o_ref, buf, sems,
                    *, tm, max_pieces):
    i = pl.program_id(0)
    n = pl.num_programs(0)
    slot = lax.rem(i, 2)
    half = o_ref.shape[1] // 2

    def fetch(tile, s):
        def body(p, _):
            src = rows_ref.at[pl.ds(pl.multiple_of(src_ref[tile * max_pieces + p], SUBLANES), PIECE)]
            dst = buf.at[s, pl.ds(pl.multiple_of(p * PIECE, PIECE), PIECE)]
            pltpu.make_async_copy(src, dst, sems.at[s]).start()
            return 0

        lax.fori_loop(0, npieces_ref[tile], body, 0)

    @pl.when(i == 0)
    def _():
        buf[...] = jnp.zeros_like(buf)
        fetch(0, 0)

    @pl.when(i + 1 < n)
    def _():
        fetch(i + 1, 1 - slot)

    total = npieces_ref[i] * PIECE
    staged = staged_ref[...]
    gates = gate_ref[...]

    @pl.when(total > 0)
    def _():
        rows = pl.ds(0, pl.multiple_of(total, PIECE))
        pltpu.make_async_copy(rows_ref.at[rows], buf.at[slot, rows], sems.at[slot]).wait()

    def chunk_products(c0):
        col = lax.broadcasted_iota(jnp.int32, (tm, COMBINE_CHUNK), 1) + c0
        sel = jnp.zeros((tm, COMBINE_CHUNK), F32)
        for k in range(TOP_K):
            sel = jnp.where(staged[:, k:k + 1] == col, gates[:, k:k + 1], sel)
        sel = sel.astype(BF16)
        lo, hi = _unpack_halves(buf[slot, pl.ds(c0, COMBINE_CHUNK), :])
        return jnp.dot(sel, lo, preferred_element_type=F32), jnp.dot(sel, hi, preferred_element_type=F32)

    xn = xn_ref[...]
    acc_lo, acc_hi = xn[:, :half], xn[:, half:]
    for c in range(COMBINE_STATIC_CHUNKS):
        d_lo, d_hi = chunk_products(c * COMBINE_CHUNK)
        acc_lo, acc_hi = acc_lo + d_lo, acc_hi + d_hi
    o_ref[:, :half] = acc_lo
    o_ref[:, half:] = acc_hi

    def chunk(c, _):
        d_lo, d_hi = chunk_products(pl.multiple_of(c * COMBINE_CHUNK, COMBINE_CHUNK))
        o_ref[:, :half] += d_lo
        o_ref[:, half:] += d_hi
        return 0

    lax.fori_loop(COMBINE_STATIC_CHUNKS, (total + COMBINE_CHUNK - 1) // COMBINE_CHUNK, chunk, 0)


def _combine(run_start, run_cnt, seg_start, eid, rank, gates_t, xn, y_rows, tm):
    t, d = xn.shape
    n = t // tm
    cap = -(-(TOP_K * tm + 2 * PIECE * N_EXPERTS) // COMBINE_CHUNK) * COMBINE_CHUNK
    max_pieces = cap // PIECE
    src, npieces, staged_t = _combine_plan(run_start, run_cnt, seg_start, eid, rank, tm, max_pieces)
    tok = lambda i, *_: (i, 0)
    grid_spec = pltpu.PrefetchScalarGridSpec(
        num_scalar_prefetch=2,
        grid=(n,),
        in_specs=[
            pl.BlockSpec((tm, TOP_K), tok),
            pl.BlockSpec((tm, TOP_K), tok),
            pl.BlockSpec((tm, d), tok),
            pl.BlockSpec(memory_space=pl.ANY),
        ],
        out_specs=pl.BlockSpec((tm, d), tok),
        scratch_shapes=[pltpu.VMEM((2, cap, d // 2), jnp.uint32), pltpu.SemaphoreType.DMA((2,))],
    )
    return pl.pallas_call(
        functools.partial(_combine_kernel, tm=tm, max_pieces=max_pieces),
        grid_spec=grid_spec,
        out_shape=jax.ShapeDtypeStruct((t, d), F32),
        compiler_params=_params(1),
        name="combine",
    )(src, npieces, staged_t, gates_t, xn, y_rows)


def _tile(n, want):
    t = min(n, want)
    assert n % t == 0, (n, t)
    return t


def kernel(x, meta_tokens, norm1_g, w_in, q_norm_g, k_norm_g, conv_w, conv_b, conv_ln_g, conv_ln_b,
           attn_out_g, w_out, norm2_g, w_router, b_router, w_up, b_up, w_down, b_down):
    assert norm1_g.shape[0] == 1, "single layer: meta-token rows are only keys/values and conv context"
    b, s, d = x.shape
    t = b * s
    cw = conv_w.shape[-1]

    g1 = norm1_g[0][None, :]
    w_in_bf = w_in[0].astype(BF16)
    qg = jnp.tile(q_norm_g[0], SB_HEADS)[None, :]
    kg = jnp.tile(k_norm_g[0], SB_HEADS)[None, :]

    q, k, v, hcv = _inproj(x.reshape(t, d), g1, w_in_bf, qg, kg, _tile(t, 512))
    _, km, vm, hm = _inproj(meta_tokens, g1, w_in_bf, qg, kg, N_META)
    pad = ((0, LANES - N_META), (0, 0))
    tq = _tile(s, 256)
    sb = _attention(q.reshape(b, s, SB_WIDTH), k.reshape(b, s, SB_WIDTH), v.reshape(b, s, SB_WIDTH),
                    jnp.pad(km, pad), jnp.pad(vm, pad), attn_out_g[0].reshape(1, SB_WIDTH), tq,
                    _tile(s, ATTN_TILES_PER_STEP * tq) // tq)

    mhalo = jnp.concatenate([jnp.zeros((HALO - N_META, cw), F32), hm], axis=0)
    tm = _tile(s, 512)
    wr_t = w_router[0].T
    wr_hi = wr_t.astype(BF16)
    wr_lo = (wr_t - wr_hi.astype(F32)).astype(BF16)
    cap = -(-t // EXPERT_ROWS) * EXPERT_ROWS
    xn, eid, gates, rank, before, counts, x_rows = _mix(
        hcv.reshape(b, s, cw), mhalo, sb, x, conv_w[0], conv_b[0][None, :], conv_ln_g[0][None, :],
        conv_ln_b[0][None, :], w_out[0].astype(BF16), norm2_g[0][None, :],
        jnp.concatenate([wr_hi, wr_lo], axis=0), b_router[0][:, None], tm, _tile(tm, 64), cap)

    counts = counts[:, 0].astype(jnp.int32)
    padded = (counts + EXPERT_ROWS - 1) // EXPERT_ROWS * EXPERT_ROWS
    pad_end = jnp.cumsum(padded)
    pad_start = pad_end - padded
    n_blocks = t * TOP_K // EXPERT_ROWS + N_EXPERTS
    block_row = jnp.arange(n_blocks, dtype=jnp.int32) * EXPERT_ROWS
    block_expert = jnp.minimum(jnp.sum(block_row[:, None] >= pad_end[None, :], axis=1), N_EXPERTS - 1).astype(jnp.int32)
    block_src = (block_expert * cap + jnp.maximum(block_row - pad_start[block_expert], 0)) // EXPERT_ROWS
    n_used = (pad_end[-1:] // EXPERT_ROWS).astype(jnp.int32)
    seg = jnp.arange(N_EXPERTS, dtype=jnp.int32) * cap
    x_rows = _padfill(seg + counts, padded - counts, x_rows)
    y_rows = _experts(block_expert, block_src.astype(jnp.int32), n_used, x_rows, w_up[0], b_up[0], w_down[0],
                      b_down[0])
    tc = _tile(t, 256)
    run_start = pad_start[:, None] + before[:, ::tc]
    run_end = jnp.concatenate([run_start[:, 1:], (pad_start + counts)[:, None]], axis=1)
    out = _combine(run_start, run_end - run_start, pad_start, eid, rank, gates.T, xn.reshape(t, d), y_rows, tc)
    return out.reshape(b, s, d)
```

```python
import functools

import jax
import jax.numpy as jnp
from jax import lax
from jax.experimental import pallas as pl
from jax.experimental.pallas import tpu as pltpu

N_META = 16
SB_HEADS = 8
SB_HEAD_DIM = 64
SB_WIDTH = SB_HEADS * SB_HEAD_DIM
CONV_KERNEL = 31
N_EXPERTS = 32
TOP_K = 4
SWIGLU_LIMIT = 7.0
SWIGLU_ALPHA = 1.702
EPS = 1e-6
F32_EXP2_UNDERFLOW = -150.0
LOG2_E = 1.4426950408889634

LANES = 128
SUBLANES = 8
ATTN_TILES_PER_STEP = 8
PIECE = 16
COMBINE_CHUNK = 512
COMBINE_STATIC_CHUNKS = 3
HALO = 32
EXPERT_HALF = 512
EXPERT_ROWS = 2 * EXPERT_HALF
VMEM_LIMIT = 56 * 1024 * 1024

F32 = jnp.float32
BF16 = jnp.bfloat16


def _params(n_axes, vmem=VMEM_LIMIT):
    return pltpu.CompilerParams(dimension_semantics=("arbitrary",) * n_axes, vmem_limit_bytes=vmem)


def _inproj_kernel(x_ref, g1_ref, w_ref, qg_ref, kg_ref, q_ref, k_ref, v_ref, h_ref):
    x = x_ref[...]
    ms = jnp.mean(x * x, axis=-1, keepdims=True)
    n = (x * lax.rsqrt(ms + EPS) * g1_ref[...]).astype(BF16)
    lo = lax.broadcasted_iota(jnp.int32, (1, LANES), 1) < SB_HEAD_DIM

    def proj(c0, c1):
        return jnp.dot(n, w_ref[:, c0:c1], preferred_element_type=F32)

    def head_norm(acc, g_ref, out_ref, scale):
        for c in range(SB_WIDTH // LANES):
            sl = slice(c * LANES, (c + 1) * LANES)
            a = acc[:, sl]
            sq = a * a
            s_lo = jnp.sum(jnp.where(lo, sq, 0.0), axis=-1, keepdims=True)
            s_hi = jnp.sum(jnp.where(lo, 0.0, sq), axis=-1, keepdims=True)
            r = lax.rsqrt(jnp.where(lo, s_lo, s_hi) * (1.0 / SB_HEAD_DIM) + EPS)
            out_ref[:, sl] = (a * r * (g_ref[:, sl] * scale)).astype(BF16)

    head_norm(proj(0, SB_WIDTH), qg_ref, q_ref, LOG2_E * SB_HEAD_DIM ** -0.5)
    head_norm(proj(SB_WIDTH, 2 * SB_WIDTH), kg_ref, k_ref, 1.0)
    v_ref[...] = proj(2 * SB_WIDTH, 3 * SB_WIDTH).astype(BF16)
    cw = (w_ref.shape[1] - 3 * SB_WIDTH) // 2
    val = proj(3 * SB_WIDTH, 3 * SB_WIDTH + cw)
    gate = proj(3 * SB_WIDTH + cw, 3 * SB_WIDTH + 2 * cw)
    h_ref[...] = val * jax.nn.sigmoid(gate)


def _inproj(x2, g1, w_in_bf, qg, kg, tm):
    t, d = x2.shape
    cols = w_in_bf.shape[1]
    cw = (cols - 3 * SB_WIDTH) // 2
    row = lambda i: (i, 0)
    fixed = lambda i: (0, 0)
    return pl.pallas_call(
        _inproj_kernel,
        grid=(t // tm,),
        in_specs=[
            pl.BlockSpec((tm, d), row),
            pl.BlockSpec((1, d), fixed),
            pl.BlockSpec((d, cols), fixed),
            pl.BlockSpec((1, SB_WIDTH), fixed),
            pl.BlockSpec((1, SB_WIDTH), fixed),
        ],
        out_specs=[
            pl.BlockSpec((tm, SB_WIDTH), row),
            pl.BlockSpec((tm, SB_WIDTH), row),
            pl.BlockSpec((tm, SB_WIDTH), row),
            pl.BlockSpec((tm, cw), row),
        ],
        out_shape=[
            jax.ShapeDtypeStruct((t, SB_WIDTH), BF16),
            jax.ShapeDtypeStruct((t, SB_WIDTH), BF16),
            jax.ShapeDtypeStruct((t, SB_WIDTH), BF16),
            jax.ShapeDtypeStruct((t, cw), F32),
        ],
        compiler_params=_params(1),
        name="inproj",
    )(x2, g1, w_in_bf, qg, kg)


def _attn_kernel(q_ref, k_ref, v_ref, km_ref, vm_ref, g_ref, o_ref, *, tq, nsub):
    i = pl.program_id(2)
    lane = lax.broadcasted_iota(jnp.int32, (1, LANES), 1)
    lo = lane < SB_HEAD_DIM
    zero_bf = jnp.zeros((), BF16)

    def stacked_q(sub):
        q2 = q_ref[0, sub * tq:(sub + 1) * tq, :]
        return jnp.concatenate([jnp.where(lo, q2, zero_bf), jnp.where(lo, zero_bf, q2)], axis=0)

    qss = [stacked_q(sub) for sub in range(nsub)]
    row = lax.broadcasted_iota(jnp.int32, (2 * tq, tq), 0)
    col = lax.broadcasted_iota(jnp.int32, (2 * tq, tq), 1)
    causal = col < jnp.where(row >= tq, row - tq, row)
    srow = lax.broadcasted_iota(jnp.int32, (tq, tq), 0)
    scol = lax.broadcasted_iota(jnp.int32, (tq, tq), 1)
    neg_suffix = jnp.where(srow > scol, -1.0, 0.0).astype(BF16)

    def block(qs, kb, vb, acc, r, mask, neg_suffix_m):
        z = lax.dot_general(qs, kb, (((1,), (1,)), ((), ())), preferred_element_type=F32)
        m = jnp.minimum(z, 0.0)
        p = jnp.maximum(z, 0.0)
        l1p = jnp.log2(1.0 + jnp.exp2(m - p))
        log_beta = m - l1p
        neg_keep = p + l1p
        if mask is not None:
            neg_keep = jnp.where(mask, neg_keep, 0.0)
        later = jnp.dot(neg_keep.astype(BF16), neg_suffix_m, preferred_element_type=F32)
        a = jnp.exp2(log_beta + later + r)
        if mask is not None:
            a = jnp.where(mask, a, 0.0)
        res = jnp.dot(a.astype(BF16), vb, preferred_element_type=F32)
        acc = acc + jnp.where(lo, res[:tq], res[tq:])
        return acc, r - jnp.sum(neg_keep, axis=-1, keepdims=True)

    def kv_block(j):
        start = pl.multiple_of(j * tq, tq)
        return k_ref[0, pl.ds(start, tq), :], v_ref[0, pl.ds(start, tq), :]

    def live(r):
        return jnp.max(r) > F32_EXP2_UNDERFLOW

    acc0 = jnp.zeros((tq, LANES), F32)
    r0 = jnp.zeros((2 * tq, 1), F32)
    first = i * nsub

    def guaranteed(first_has_previous):
        state = []
        for sub in range(nsub):
            acc, r = block(qss[sub], *kv_block(first + sub), acc0, r0, causal, neg_suffix)
            if sub > 0 or first_has_previous:
                acc, r = block(qss[sub], *kv_block(first + sub - 1), acc, r, None, neg_suffix)
            state += [acc, r]
        return tuple(state)

    state = lax.cond(i > 0, lambda _: guaranteed(True), lambda _: guaranteed(False), 0)

    for sub in range(nsub):
        j = first + sub
        qs, acc, r = qss[sub], state[2 * sub], state[2 * sub + 1]

        def cond(c):
            return jnp.logical_and(c[0] <= j, c[3])

        def body(c):
            acc, r = block(qs, *kv_block(j - c[0]), c[1], c[2], None, neg_suffix)
            return c[0] + 1, acc, r, live(r)

        _, acc, r, alive = lax.while_loop(cond, body, (jnp.int32(2), acc, r, live(r)))

        def meta_block(acc):
            return block(qs, km_ref[...], vm_ref[...], acc, r, lane < N_META, neg_suffix[:LANES, :LANES])[0]

        acc = lax.cond(alive, meta_block, lambda acc: acc, acc)

        sq = acc * acc
        s_lo = jnp.sum(jnp.where(lo, sq, 0.0), axis=-1, keepdims=True)
        s_hi = jnp.sum(jnp.where(lo, 0.0, sq), axis=-1, keepdims=True)
        rn = lax.rsqrt(jnp.where(lo, s_lo, s_hi) * (1.0 / SB_HEAD_DIM) + EPS)
        o_ref[0, sub * tq:(sub + 1) * tq, :] = (acc * rn * g_ref[...]).astype(BF16)


def _attention(q, k, v, km, vm, og, tq, nsub):
    b, s, _ = q.shape
    n_pairs = SB_WIDTH // LANES
    step = tq * nsub
    return pl.pallas_call(
        functools.partial(_attn_kernel, tq=tq, nsub=nsub),
        grid=(b, n_pairs, s // step),
        in_specs=[
            pl.BlockSpec((1, step, LANES), lambda b, p, i: (b, i, p)),
            pl.BlockSpec((1, s, LANES), lambda b, p, i: (b, 0, p)),
            pl.BlockSpec((1, s, LANES), lambda b, p, i: (b, 0, p)),
            pl.BlockSpec((LANES, LANES), lambda b, p, i: (0, p)),
            pl.BlockSpec((LANES, LANES), lambda b, p, i: (0, p)),
            pl.BlockSpec((1, LANES), lambda b, p, i: (0, p)),
        ],
        out_specs=pl.BlockSpec((1, step, LANES), lambda b, p, i: (b, i, p)),
        out_shape=jax.ShapeDtypeStruct((b, s, SB_WIDTH), BF16),
        compiler_params=_params(3),
        name="attention",
    )(q, k, v, km, vm, og)


def _mix_kernel(h_ref, halo_ref, mhalo_ref, sb_ref, x_ref, cw_ref, cb_ref, lg_ref, lb_ref, wo_ref,
                g2_ref, wr_ref, br_ref,
                xn_ref, eid_ref, gate_ref, rank_ref, before_ref, cnt_ref, rows_ref,
                win_ref, conv_ref, cv_ref, carry_ref, hn3_ref, posv_ref, pos_ref, dsem, psem, *, tm, chunk, cap):
    b = pl.program_id(0)
    i = pl.program_id(1)
    step = b * pl.num_programs(1) + i

    def pos_copy():
        return pltpu.make_async_copy(posv_ref, pos_ref, psem)

    def row_copy(t, k):
        return pltpu.make_async_copy(_row_tiles(hn3_ref, t), _row_tiles(rows_ref, pos_ref[0, k * tm + t]), dsem)

    @pl.when(step == 0)
    def _():
        carry_ref[...] = jnp.zeros_like(carry_ref)
        hn3_ref[...] = jnp.zeros_like(hn3_ref)
        posv_ref[...] = N_EXPERTS * cap + lax.broadcasted_iota(jnp.int32, posv_ref.shape, 1)
        pos_copy().start()

    pos_copy().wait()

    @pl.when(i == 0)
    def _():
        win_ref[0:HALO, :] = mhalo_ref[...]

    @pl.when(i > 0)
    def _():
        win_ref[0:HALO, :] = halo_ref[0]

    win_ref[HALO:, :] = h_ref[0]

    first_tap = HALO - (CONV_KERNEL - 1)

    def conv_chunk(c, _):
        r0 = pl.multiple_of(c * chunk, chunk)
        rows = chunk + HALO
        for lt in range(cb_ref.shape[1] // LANES):
            sl = slice(lt * LANES, (lt + 1) * LANES)
            window = win_ref[pl.ds(r0, rows), sl]
            acc = jnp.broadcast_to(cb_ref[:, sl], (chunk, LANES))
            for b in range(SUBLANES):
                shifted = window if b == 0 else pltpu.roll(window, shift=rows - b, axis=0)
                for a in range(HALO // SUBLANES + 1):
                    j = SUBLANES * a + b - first_tap
                    if 0 <= j < CONV_KERNEL:
                        acc = acc + cw_ref[j:j + 1, sl] * shifted[SUBLANES * a:SUBLANES * a + chunk, :]
            conv_ref[:, sl] = acc
        for tt in range(chunk):
            for k in range(TOP_K):
                row_copy(r0 + tt, k).start(priority=k % 2)
        acc = conv_ref[...]
        mu = jnp.mean(acc, axis=-1, keepdims=True)
        cen = acc - mu
        var = jnp.mean(cen * cen, axis=-1, keepdims=True)
        y = cen * lax.rsqrt(var + EPS) * lg_ref[...] + lb_ref[...]
        cv_ref[pl.ds(r0, chunk), :] = (y * jax.nn.sigmoid(y)).astype(BF16)
        return 0

    lax.fori_loop(0, tm // chunk, conv_chunk, 0)

    mixed = jnp.dot(sb_ref[0], wo_ref[0:SB_WIDTH, :], preferred_element_type=F32)
    mixed = mixed + jnp.dot(cv_ref[...], wo_ref[SB_WIDTH:, :], preferred_element_type=F32)
    xn = x_ref[0] + mixed
    xn_ref[0] = xn
    ms = jnp.mean(xn * xn, axis=-1, keepdims=True)
    hn = xn * lax.rsqrt(ms + EPS) * g2_ref[...]

    nt = (((1,), (1,)), ((), ()))
    hn_hi = hn.astype(BF16)
    hn_lo = (hn - hn_hi.astype(F32)).astype(BF16)
    both = lax.dot_general(wr_ref[...], hn_hi, nt, preferred_element_type=F32)
    logits = (both[:N_EXPERTS] + both[N_EXPERTS:]
              + lax.dot_general(wr_ref[0:N_EXPERTS, :], hn_lo, nt, preferred_element_type=F32) + br_ref[...])
    eidx = lax.broadcasted_iota(jnp.int32, logits.shape, 0)
    vals, idxs = [], []
    for _ in range(TOP_K):
        m = jnp.max(logits, axis=0, keepdims=True)
        sel = jnp.min(jnp.where(logits == m, eidx, N_EXPERTS), axis=0, keepdims=True)
        vals.append(m)
        idxs.append(sel)
        logits = jnp.where(eidx == sel, -jnp.inf, logits)
    exps = [jnp.exp(v - vals[0]) for v in vals]
    denom = exps[0] + exps[1] + exps[2] + exps[3]
    gate_ref[...] = jnp.concatenate([e / denom for e in exps], axis=0)
    eid_ref[...] = jnp.concatenate(idxs, axis=0)

    onehots = [(eidx == s).astype(F32) for s in idxs]
    chosen = onehots[0] + onehots[1] + onehots[2] + onehots[3]
    tr = lax.broadcasted_iota(jnp.int32, (tm, tm), 0)
    tc = lax.broadcasted_iota(jnp.int32, (tm, tm), 1)
    before = (tr < tc).astype(BF16)
    prefix = jnp.dot(chosen.astype(BF16), before, preferred_element_type=F32) + carry_ref[...]
    ranks = [jnp.sum(o * prefix, axis=0, keepdims=True).astype(jnp.int32) for o in onehots]
    rank_ref[...] = jnp.concatenate(ranks, axis=0)
    before_ref[...] = prefix.astype(jnp.int32)
    carry_ref[...] = carry_ref[...] + jnp.sum(chosen, axis=1, keepdims=True)
    cnt_ref[...] = jnp.broadcast_to(carry_ref[...], cnt_ref.shape)

    _wait_rows(rows_ref, TOP_K * tm, dsem)
    groups = hn.shape[1] // LANES
    for c in range(groups):
        hn3_ref[pl.ds(c, tm, stride=groups), :] = hn[:, c * LANES:(c + 1) * LANES]
    for k in range(TOP_K):
        posv_ref[:, k * tm:(k + 1) * tm] = idxs[k] * cap + ranks[k]
    pos_copy().start()

    @pl.when(step == pl.num_programs(0) * pl.num_programs(1) - 1)
    def _():
        pos_copy().wait()

        def issue(t, _):
            for k in range(TOP_K):
                row_copy(t, k).start(priority=k % 2)
            return 0

        lax.fori_loop(0, tm, issue, 0, unroll=2)
        _wait_rows(rows_ref, TOP_K * tm, dsem)


def _mix(hcv, mhalo, sb, x, conv_w, conv_b, ln_g, ln_b, w_out_bf, g2, wr_t, br, tm, chunk, cap):
    b, s, d = x.shape
    cw = hcv.shape[-1]
    t = b * s
    per = s // tm
    assert d == SUBLANES * LANES, d
    tile = lambda b, i: (b, i, 0)
    fixed = lambda b, i: (0, 0)
    tok = lambda b, i: (0, b * per + i)
    x_rows = N_EXPERTS * cap + TOP_K * tm
    return pl.pallas_call(
        functools.partial(_mix_kernel, tm=tm, chunk=chunk, cap=cap),
        grid=(b, per),
        in_specs=[
            pl.BlockSpec((1, tm, cw), tile),
            pl.BlockSpec((1, HALO, cw), lambda b, i: (b, jnp.maximum(i * (tm // HALO) - 1, 0), 0)),
            pl.BlockSpec((HALO, cw), fixed),
            pl.BlockSpec((1, tm, SB_WIDTH), tile),
            pl.BlockSpec((1, tm, d), tile),
            pl.BlockSpec((CONV_KERNEL, cw), fixed),
            pl.BlockSpec((1, cw), fixed),
            pl.BlockSpec((1, cw), fixed),
            pl.BlockSpec((1, cw), fixed),
            pl.BlockSpec((SB_WIDTH + cw, d), fixed),
            pl.BlockSpec((1, d), fixed),
            pl.BlockSpec((2 * N_EXPERTS, d), fixed),
            pl.BlockSpec((N_EXPERTS, 1), fixed),
        ],
        out_specs=[
            pl.BlockSpec((1, tm, d), tile),
            pl.BlockSpec((TOP_K, tm), tok),
            pl.BlockSpec((TOP_K, tm), tok),
            pl.BlockSpec((TOP_K, tm), tok),
            pl.BlockSpec((N_EXPERTS, tm), tok),
            pl.BlockSpec((N_EXPERTS, LANES), fixed),
            pl.BlockSpec(memory_space=pl.ANY),
        ],
        out_shape=[
            jax.ShapeDtypeStruct((b, s, d), F32),
            jax.ShapeDtypeStruct((TOP_K, t), jnp.int32),
            jax.ShapeDtypeStruct((TOP_K, t), F32),
            jax.ShapeDtypeStruct((TOP_K, t), jnp.int32),
            jax.ShapeDtypeStruct((N_EXPERTS, t), jnp.int32),
            jax.ShapeDtypeStruct((N_EXPERTS, LANES), F32),
            jax.ShapeDtypeStruct((x_rows * SUBLANES, LANES), F32),
        ],
        scratch_shapes=[
            pltpu.VMEM((tm + HALO, cw), F32),
            pltpu.VMEM((chunk, cw), F32),
            pltpu.VMEM((tm, cw), BF16),
            pltpu.VMEM((N_EXPERTS, 1), F32),
            pltpu.VMEM((tm * SUBLANES, LANES), F32),
            pltpu.VMEM((1, TOP_K * tm), jnp.int32),
            pltpu.SMEM((1, TOP_K * tm), jnp.int32),
            pltpu.SemaphoreType.DMA(()),
            pltpu.SemaphoreType.DMA(()),
        ],
        compiler_params=_params(2),
        name="mix_router",
    )(hcv, hcv, mhalo, sb, x, conv_w, conv_b, ln_g, ln_b, w_out_bf, g2, wr_t, br)


def _row_tiles(ref, row, n=1):
    size = n * SUBLANES if isinstance(n, int) else pl.multiple_of(n * SUBLANES, SUBLANES)
    return ref.at[pl.ds(pl.multiple_of(row * SUBLANES, SUBLANES), size)]


def _wait_rows(rows_ref, n, sem):
    pltpu.make_async_copy(_row_tiles(rows_ref, 0, n), _row_tiles(rows_ref, 0, n), sem).wait()


def _padfill_kernel(padlo_ref, padlen_ref, rows_in_ref, rows_ref, zeros_ref, zsem):
    del rows_in_ref
    zeros_ref[...] = jnp.zeros_like(zeros_ref)
    total = jnp.int32(0)
    for e in range(N_EXPERTS):
        lo, ln = padlo_ref[e], padlen_ref[e]
        total = total + ln
        for bit in range(EXPERT_HALF.bit_length() - 1):
            size = 1 << bit

            @pl.when((ln >> bit) & 1 == 1)
            def _():
                start = lo + ((ln >> (bit + 1)) << (bit + 1))
                pltpu.make_async_copy(_row_tiles(zeros_ref, 0, size), _row_tiles(rows_ref, start, size),
                                      zsem).start()

    @pl.when(total > 0)
    def _():
        _wait_rows(rows_ref, total, zsem)


def _padfill(pad_lo, pad_len, x_rows):
    smem = pl.BlockSpec(memory_space=pltpu.SMEM)
    return pl.pallas_call(
        _padfill_kernel,
        in_specs=[smem, smem, pl.BlockSpec(memory_space=pl.ANY)],
        out_specs=pl.BlockSpec(memory_space=pl.ANY),
        out_shape=jax.ShapeDtypeStruct(x_rows.shape, x_rows.dtype),
        scratch_shapes=[pltpu.VMEM((EXPERT_HALF * SUBLANES // 2, LANES), x_rows.dtype), pltpu.SemaphoreType.DMA(())],
        input_output_aliases={2: 0},
        compiler_params=pltpu.CompilerParams(vmem_limit_bytes=VMEM_LIMIT),
        name="padfill",
    )(pad_lo, pad_len, x_rows)


def _expert_kernel(be_ref, bsrc_ref, valid_ref, nused_ref, x_ref, wu_ref, bu_ref, wd_ref, bd_ref, y_ref,
                   wu_bf, wd_bf, *, cast_rows):
    del bsrc_ref, nused_ref
    j = pl.program_id(0)
    de = wd_ref.shape[1]
    valid = valid_ref[j]

    def half(h):
        x = jnp.concatenate([x_ref[pl.ds(h * EXPERT_HALF * SUBLANES + c, EXPERT_HALF, stride=SUBLANES), :]
                             for c in range(SUBLANES)], axis=1)
        up = jnp.dot(x.astype(BF16), wu_bf[...], preferred_element_type=F32) + bu_ref[0]
        glu = jnp.minimum(up[:, :de], SWIGLU_LIMIT)
        lin = jnp.clip(up[:, de:], -SWIGLU_LIMIT, SWIGLU_LIMIT)
        act = glu * jax.nn.sigmoid(SWIGLU_ALPHA * glu) * (lin + 1.0)
        y = jnp.dot(act.astype(BF16), wd_bf[...], preferred_element_type=F32) + bd_ref[0]
        y_ref[h * EXPERT_HALF:(h + 1) * EXPERT_HALF, :] = _pack_halves(y)

    @pl.when(valid == 0)
    def _():
        y_ref[...] = jnp.zeros_like(y_ref)

    @pl.when(valid > 0)
    def _():
        prev = be_ref[jnp.maximum(j - 1, 0)]

        @pl.when(jnp.logical_or(j == 0, be_ref[j] != prev))
        def _():
            def cast(c, _):
                r0 = pl.multiple_of(c * cast_rows, cast_rows)
                wu_bf[pl.ds(r0, cast_rows), :] = wu_ref[0, pl.ds(r0, cast_rows), :].astype(BF16)
                wd_bf[pl.ds(r0, cast_rows), :] = wd_ref[0, pl.ds(r0, cast_rows), :].astype(BF16)
                return 0

            lax.fori_loop(0, wu_ref.shape[1] // cast_rows, cast, 0)

    @pl.when(valid > EXPERT_HALF)
    def _():
        half(0)
        half(1)

    @pl.when(jnp.logical_and(valid > 0, valid <= EXPERT_HALF))
    def _():
        half(0)
        y_ref[EXPERT_HALF:, :] = jnp.zeros((EXPERT_ROWS - EXPERT_HALF, y_ref.shape[1]), y_ref.dtype)


def _pack_halves(y):
    half = y.shape[1] // 2
    bits = lambda v: lax.bitcast_convert_type(v.astype(BF16).astype(F32), jnp.uint32)
    return (bits(y[:, half:]) & jnp.uint32(0xFFFF0000)) | (bits(y[:, :half]) >> 16)


def _unpack_halves(w):
    lo = lax.bitcast_convert_type(w << 16, F32).astype(BF16)
    hi = lax.bitcast_convert_type(w & jnp.uint32(0xFFFF0000), F32).astype(BF16)
    return lo, hi


def _experts(block_expert, block_src, block_valid, n_used, x_rows, w_up, b_up, w_down, b_down):
    n_blocks = block_expert.shape[0] + 1
    block_valid = jnp.concatenate([block_valid, jnp.zeros((1,), jnp.int32)])
    ne, d, up_cols = w_up.shape
    de = w_down.shape[1]
    blk = lambda j, be, bs, bv, nu: (bs[jnp.minimum(j, nu[0] - 1)], 0)
    wsel = lambda j, be, bs, bv, nu: (be[jnp.minimum(j, nu[0] - 1)], 0, 0)
    grid_spec = pltpu.PrefetchScalarGridSpec(
        num_scalar_prefetch=4,
        grid=(n_blocks,),
        in_specs=[
            pl.BlockSpec((EXPERT_ROWS * SUBLANES, LANES), blk),
            pl.BlockSpec((1, d, up_cols), wsel),
            pl.BlockSpec((1, 1, up_cols), wsel),
            pl.BlockSpec((1, de, d), wsel),
            pl.BlockSpec((1, 1, d), wsel),
        ],
        out_specs=pl.BlockSpec((EXPERT_ROWS, d // 2), lambda j, be, bs, bv, nu: (j, 0)),
        scratch_shapes=[pltpu.VMEM((d, up_cols), BF16), pltpu.VMEM((de, d), BF16)],
    )
    return pl.pallas_call(
        functools.partial(_expert_kernel, cast_rows=64),
        grid_spec=grid_spec,
        out_shape=jax.ShapeDtypeStruct((n_blocks * EXPERT_ROWS, d // 2), jnp.uint32),
        compiler_params=_params(1),
        name="experts",
    )(block_expert, block_src, block_valid, n_used, x_rows, w_up, b_up.reshape(ne, 1, up_cols),
      w_down, b_down.reshape(ne, 1, d))


def _combine_plan(run_start, run_cnt, seg_start, eid, rank, tm, max_pieces):
    first = (run_start >> 3) << 3
    pieces = jnp.where(run_cnt > 0, (run_start - first + run_cnt + PIECE - 1) // PIECE, 0)
    piece_end = jnp.cumsum(pieces, axis=0)
    piece_off = piece_end - pieces
    p = jnp.arange(max_pieces, dtype=jnp.int32)
    expert_of = jnp.minimum(jnp.sum(p[None, :, None] >= piece_end.T[:, None, :], axis=-1), N_EXPERTS - 1)
    onehot = expert_of[:, :, None] == jnp.arange(N_EXPERTS, dtype=jnp.int32)
    src = jnp.sum(jnp.where(onehot, (first - piece_off * PIECE).T[:, None, :], 0), axis=-1) + p * PIECE
    shift = jnp.repeat((seg_start[:, None] + piece_off * PIECE - first).T, tm, axis=0)
    tok_onehot = eid[:, :, None] == jnp.arange(N_EXPERTS, dtype=jnp.int32)
    staged = rank + jnp.sum(jnp.where(tok_onehot, shift[None, :, :], 0), axis=-1)
    return src.reshape(-1).astype(jnp.int32), piece_end[-1].astype(jnp.int32), staged.T.astype(jnp.int32)


def _combine_kernel(src_ref, npieces_ref, staged_ref, gate_ref, xn_ref, rows_ref, o_ref, buf, sems,
                    *, tm, max_pieces):
    i = pl.program_id(0)
    n = pl.num_programs(0)
    slot = lax.rem(i, 2)
    half = o_ref.shape[1] // 2

    def fetch(tile, s):
        def body(p, _):
            src = rows_ref.at[pl.ds(pl.multiple_of(src_ref[tile * max_pieces + p], SUBLANES), PIECE)]
            dst = buf.at[s, pl.ds(pl.multiple_of(p * PIECE, PIECE), PIECE)]
            pltpu.make_async_copy(src, dst, sems.at[s]).start()
            return 0

        lax.fori_loop(0, npieces_ref[tile], body, 0)

    @pl.when(i == 0)
    def _():
        buf[...] = jnp.zeros_like(buf)
        fetch(0, 0)

    @pl.when(i + 1 < n)
    def _():
        fetch(i + 1, 1 - slot)

    total = npieces_ref[i] * PIECE
    staged = staged_ref[...]
    gates = gate_ref[...]

    @pl.when(total > 0)
    def _():
        rows = pl.ds(0, pl.multiple_of(total, PIECE))
        pltpu.make_async_copy(rows_ref.at[rows], buf.at[slot, rows], sems.at[slot]).wait()

    def chunk_products(c0):
        col = lax.broadcasted_iota(jnp.int32, (tm, COMBINE_CHUNK), 1) + c0
        sel = jnp.zeros((tm, COMBINE_CHUNK), F32)
        for k in range(TOP_K):
            sel = jnp.where(staged[:, k:k + 1] == col, gates[:, k:k + 1], sel)
        sel = sel.astype(BF16)
        lo, hi = _unpack_halves(buf[slot, pl.ds(c0, COMBINE_CHUNK), :])
        return jnp.dot(sel, lo, preferred_element_type=F32), jnp.dot(sel, hi, preferred_element_type=F32)

    xn = xn_ref[...]
    acc_lo, acc_hi = xn[:, :half], xn[:, half:]
    for c in range(COMBINE_STATIC_CHUNKS):
        d_lo, d_hi = chunk_products(c * COMBINE_CHUNK)
        acc_lo, acc_hi = acc_lo + d_lo, acc_hi + d_hi
    o_ref[:, :half] = acc_lo
    o_ref[:, half:] = acc_hi

    def chunk(c, _):
        d_lo, d_hi = chunk_products(pl.multiple_of(c * COMBINE_CHUNK, COMBINE_CHUNK))
        o_ref[:, :half] += d_lo
        o_ref[:, half:] += d_hi
        return 0

    lax.fori_loop(COMBINE_STATIC_CHUNKS, (total + COMBINE_CHUNK - 1) // COMBINE_CHUNK, chunk, 0)


def _combine(run_start, run_cnt, seg_start, eid, rank, gates_t, xn, y_rows, tm):
    t, d = xn.shape
    n = t // tm
    cap = -(-(TOP_K * tm + 2 * PIECE * N_EXPERTS) // COMBINE_CHUNK) * COMBINE_CHUNK
    max_pieces = cap // PIECE
    src, npieces, staged_t = _combine_plan(run_start, run_cnt, seg_start, eid, rank, tm, max_pieces)
    tok = lambda i, *_: (i, 0)
    grid_spec = pltpu.PrefetchScalarGridSpec(
        num_scalar_prefetch=2,
        grid=(n,),
        in_specs=[
            pl.BlockSpec((tm, TOP_K), tok),
            pl.BlockSpec((tm, TOP_K), tok),
            pl.BlockSpec((tm, d), tok),
            pl.BlockSpec(memory_space=pl.ANY),
        ],
        out_specs=pl.BlockSpec((tm, d), tok),
        scratch_shapes=[pltpu.VMEM((2, cap, d // 2), jnp.uint32), pltpu.SemaphoreType.DMA((2,))],
    )
    return pl.pallas_call(
        functools.partial(_combine_kernel, tm=tm, max_pieces=max_pieces),
        grid_spec=grid_spec,
        out_shape=jax.ShapeDtypeStruct((t, d), F32),
        compiler_params=_params(1),
        name="combine",
    )(src, npieces, staged_t, gates_t, xn, y_rows)


def _tile(n, want):
    t = min(n, want)
    assert n % t == 0, (n, t)
    return t


def kernel(x, meta_tokens, norm1_g, w_in, q_norm_g, k_norm_g, conv_w, conv_b, conv_ln_g, conv_ln_b,
           attn_out_g, w_out, norm2_g, w_router, b_router, w_up, b_up, w_down, b_down):
    assert norm1_g.shape[0] == 1, "single layer: meta-token rows are only keys/values and conv context"
    b, s, d = x.shape
    t = b * s
    cw = conv_w.shape[-1]

    g1 = norm1_g[0][None, :]
    w_in_bf = w_in[0].astype(BF16)
    qg = jnp.tile(q_norm_g[0], SB_HEADS)[None, :]
    kg = jnp.tile(k_norm_g[0], SB_HEADS)[None, :]

    q, k, v, hcv = _inproj(x.reshape(t, d), g1, w_in_bf, qg, kg, _tile(t, 1024))
    _, km, vm, hm = _inproj(meta_tokens, g1, w_in_bf, qg, kg, N_META)
    pad = ((0, LANES - N_META), (0, 0))
    tq = _tile(s, 256)
    sb = _attention(q.reshape(b, s, SB_WIDTH), k.reshape(b, s, SB_WIDTH), v.reshape(b, s, SB_WIDTH),
                    jnp.pad(km, pad), jnp.pad(vm, pad), attn_out_g[0].reshape(1, SB_WIDTH), tq,
                    _tile(s, ATTN_TILES_PER_STEP * tq) // tq)

    mhalo = jnp.concatenate([jnp.zeros((HALO - N_META, cw), F32), hm], axis=0)
    tm = _tile(s, 512)
    wr_t = w_router[0].T
    wr_hi = wr_t.astype(BF16)
    wr_lo = (wr_t - wr_hi.astype(F32)).astype(BF16)
    cap = -(-t // EXPERT_ROWS) * EXPERT_ROWS
    xn, eid, gates, rank, before, counts, x_rows = _mix(
        hcv.reshape(b, s, cw), mhalo, sb, x, conv_w[0], conv_b[0][None, :], conv_ln_g[0][None, :],
        conv_ln_b[0][None, :], w_out[0].astype(BF16), norm2_g[0][None, :],
        jnp.concatenate([wr_hi, wr_lo], axis=0), b_router[0][:, None], tm, _tile(tm, 64), cap)

    counts = counts[:, 0].astype(jnp.int32)
    padded = (counts + EXPERT_ROWS - 1) // EXPERT_ROWS * EXPERT_ROWS
    pad_end = jnp.cumsum(padded)
    pad_start = pad_end - padded
    n_blocks = t * TOP_K // EXPERT_ROWS + N_EXPERTS
    block_row = jnp.arange(n_blocks, dtype=jnp.int32) * EXPERT_ROWS
    block_expert = jnp.minimum(jnp.sum(block_row[:, None] >= pad_end[None, :], axis=1), N_EXPERTS - 1).astype(jnp.int32)
    block_off = block_row - pad_start[block_expert]
    block_src = (block_expert * cap + jnp.maximum(block_off, 0)) // EXPERT_ROWS
    block_valid = jnp.clip(counts[block_expert] - block_off, 0, EXPERT_ROWS)
    block_valid = jnp.where(block_row < pad_end[-1], block_valid, 0)
    n_used = (pad_end[-1:] // EXPERT_ROWS).astype(jnp.int32)
    seg = jnp.arange(N_EXPERTS, dtype=jnp.int32) * cap
    x_rows = _padfill(seg + counts, (-counts) % EXPERT_HALF, x_rows)
    y_rows = _experts(block_expert, block_src.astype(jnp.int32), block_valid.astype(jnp.int32), n_used, x_rows,
                      w_up[0], b_up[0], w_down[0], b_down[0])
    tc = _tile(t, 256)
    run_start = pad_start[:, None] + before[:, ::tc]
    run_end = jnp.concatenate([run_start[:, 1:], (pad_start + counts)[:, None]], axis=1)
    out = _combine(run_start, run_end - run_start, pad_start, eid, rank, gates.T, xn.reshape(t, d), y_rows, tc)
    return out.reshape(b, s, d)
```

```python
import functools

import jax
import jax.numpy as jnp
from jax import lax
from jax.experimental import pallas as pl
from jax.experimental.pallas import tpu as pltpu

N_META = 16
SB_HEADS = 8
SB_HEAD_DIM = 64
SB_WIDTH = SB_HEADS * SB_HEAD_DIM
CONV_KERNEL = 31
N_EXPERTS = 32
TOP_K = 4
SWIGLU_LIMIT = 7.0
SWIGLU_ALPHA = 1.702
EPS = 1e-6
F32_EXP2_UNDERFLOW = -150.0
LOG2_E = 1.4426950408889634

LANES = 128
SUBLANES = 8
ATTN_TILES_PER_STEP = 8
PIECE = 16
COMBINE_CHUNK = 512
COMBINE_STATIC_CHUNKS = 3
HALO = 32
EXPERT_HALF = 512
EXPERT_ROWS = 2 * EXPERT_HALF
VMEM_LIMIT = 56 * 1024 * 1024

F32 = jnp.float32
BF16 = jnp.bfloat16


def _params(n_axes, vmem=VMEM_LIMIT):
    return pltpu.CompilerParams(dimension_semantics=("arbitrary",) * n_axes, vmem_limit_bytes=vmem)


def _inproj_kernel(x_ref, g1_ref, w_ref, qg_ref, kg_ref, q_ref, k_ref, v_ref, h_ref):
    x = x_ref[...]
    ms = jnp.mean(x * x, axis=-1, keepdims=True)
    n = (x * lax.rsqrt(ms + EPS) * g1_ref[...]).astype(BF16)
    lo = lax.broadcasted_iota(jnp.int32, (1, LANES), 1) < SB_HEAD_DIM

    def proj(c0, c1):
        return jnp.dot(n, w_ref[:, c0:c1], preferred_element_type=F32)

    def head_norm(acc, g_ref, out_ref, scale):
        for c in range(SB_WIDTH // LANES):
            sl = slice(c * LANES, (c + 1) * LANES)
            a = acc[:, sl]
            sq = a * a
            s_lo = jnp.sum(jnp.where(lo, sq, 0.0), axis=-1, keepdims=True)
            s_hi = jnp.sum(jnp.where(lo, 0.0, sq), axis=-1, keepdims=True)
            r = lax.rsqrt(jnp.where(lo, s_lo, s_hi) * (1.0 / SB_HEAD_DIM) + EPS)
            out_ref[:, sl] = (a * r * (g_ref[:, sl] * scale)).astype(BF16)

    head_norm(proj(0, SB_WIDTH), qg_ref, q_ref, LOG2_E * SB_HEAD_DIM ** -0.5)
    head_norm(proj(SB_WIDTH, 2 * SB_WIDTH), kg_ref, k_ref, 1.0)
    v_ref[...] = proj(2 * SB_WIDTH, 3 * SB_WIDTH).astype(BF16)
    cw = (w_ref.shape[1] - 3 * SB_WIDTH) // 2
    val = proj(3 * SB_WIDTH, 3 * SB_WIDTH + cw)
    gate = proj(3 * SB_WIDTH + cw, 3 * SB_WIDTH + 2 * cw)
    h_ref[...] = val * jax.nn.sigmoid(gate)


def _inproj(x2, g1, w_in_bf, qg, kg, tm):
    t, d = x2.shape
    cols = w_in_bf.shape[1]
    cw = (cols - 3 * SB_WIDTH) // 2
    row = lambda i: (i, 0)
    fixed = lambda i: (0, 0)
    return pl.pallas_call(
        _inproj_kernel,
        grid=(t // tm,),
        in_specs=[
            pl.BlockSpec((tm, d), row),
            pl.BlockSpec((1, d), fixed),
            pl.BlockSpec((d, cols), fixed),
            pl.BlockSpec((1, SB_WIDTH), fixed),
            pl.BlockSpec((1, SB_WIDTH), fixed),
        ],
        out_specs=[
            pl.BlockSpec((tm, SB_WIDTH), row),
            pl.BlockSpec((tm, SB_WIDTH), row),
            pl.BlockSpec((tm, SB_WIDTH), row),
            pl.BlockSpec((tm, cw), row),
        ],
        out_shape=[
            jax.ShapeDtypeStruct((t, SB_WIDTH), BF16),
            jax.ShapeDtypeStruct((t, SB_WIDTH), BF16),
            jax.ShapeDtypeStruct((t, SB_WIDTH), BF16),
            jax.ShapeDtypeStruct((t, cw), F32),
        ],
        compiler_params=_params(1),
        name="inproj",
    )(x2, g1, w_in_bf, qg, kg)


def _attn_kernel(q_ref, k_ref, v_ref, km_ref, vm_ref, g_ref, o_ref, *, tq, nsub):
    i = pl.program_id(2)
    lane = lax.broadcasted_iota(jnp.int32, (1, LANES), 1)
    lo = lane < SB_HEAD_DIM
    zero_bf = jnp.zeros((), BF16)

    def stacked_q(sub):
        q2 = q_ref[0, sub * tq:(sub + 1) * tq, :]
        return jnp.concatenate([jnp.where(lo, q2, zero_bf), jnp.where(lo, zero_bf, q2)], axis=0)

    qss = [stacked_q(sub) for sub in range(nsub)]
    row = lax.broadcasted_iota(jnp.int32, (2 * tq, tq), 0)
    col = lax.broadcasted_iota(jnp.int32, (2 * tq, tq), 1)
    causal = col < jnp.where(row >= tq, row - tq, row)
    srow = lax.broadcasted_iota(jnp.int32, (tq, tq), 0)
    scol = lax.broadcasted_iota(jnp.int32, (tq, tq), 1)
    neg_suffix = jnp.where(srow > scol, -1.0, 0.0).astype(BF16)

    def block(qs, kb, vb, acc, r, mask, neg_suffix_m):
        z = lax.dot_general(qs, kb, (((1,), (1,)), ((), ())), preferred_element_type=F32)
        m = jnp.minimum(z, 0.0)
        p = jnp.maximum(z, 0.0)
        l1p = jnp.log2(1.0 + jnp.exp2(m - p))
        log_beta = m - l1p
        neg_keep = p + l1p
        if mask is not None:
            neg_keep = jnp.where(mask, neg_keep, 0.0)
        later = jnp.dot(neg_keep.astype(BF16), neg_suffix_m, preferred_element_type=F32)
        a = jnp.exp2(log_beta + later + r)
        if mask is not None:
            a = jnp.where(mask, a, 0.0)
        res = jnp.dot(a.astype(BF16), vb, preferred_element_type=F32)
        acc = acc + jnp.where(lo, res[:tq], res[tq:])
        return acc, r - jnp.sum(neg_keep, axis=-1, keepdims=True)

    def kv_block(j):
        start = pl.multiple_of(j * tq, tq)
        return k_ref[0, pl.ds(start, tq), :], v_ref[0, pl.ds(start, tq), :]

    def live(r):
        return jnp.max(r) > F32_EXP2_UNDERFLOW

    acc0 = jnp.zeros((tq, LANES), F32)
    r0 = jnp.zeros((2 * tq, 1), F32)
    first = i * nsub

    def guaranteed(first_has_previous):
        state = []
        for sub in range(nsub):
            acc, r = block(qss[sub], *kv_block(first + sub), acc0, r0, causal, neg_suffix)
            if sub > 0 or first_has_previous:
                acc, r = block(qss[sub], *kv_block(first + sub - 1), acc, r, None, neg_suffix)
            state += [acc, r]
        return tuple(state)

    state = lax.cond(i > 0, lambda _: guaranteed(True), lambda _: guaranteed(False), 0)

    for sub in range(nsub):
        j = first + sub
        qs, acc, r = qss[sub], state[2 * sub], state[2 * sub + 1]

        def cond(c):
            return jnp.logical_and(c[0] <= j, c[3])

        def body(c):
            acc, r = block(qs, *kv_block(j - c[0]), c[1], c[2], None, neg_suffix)
            return c[0] + 1, acc, r, live(r)

        _, acc, r, alive = lax.while_loop(cond, body, (jnp.int32(2), acc, r, live(r)))

        def meta_block(acc):
            return block(qs, km_ref[...], vm_ref[...], acc, r, lane < N_META, neg_suffix[:LANES, :LANES])[0]

        acc = lax.cond(alive, meta_block, lambda acc: acc, acc)

        sq = acc * acc
        s_lo = jnp.sum(jnp.where(lo, sq, 0.0), axis=-1, keepdims=True)
        s_hi = jnp.sum(jnp.where(lo, 0.0, sq), axis=-1, keepdims=True)
        rn = lax.rsqrt(jnp.where(lo, s_lo, s_hi) * (1.0 / SB_HEAD_DIM) + EPS)
        o_ref[0, sub * tq:(sub + 1) * tq, :] = (acc * rn * g_ref[...]).astype(BF16)


def _attention(q, k, v, km, vm, og, tq, nsub):
    b, s, _ = q.shape
    n_pairs = SB_WIDTH // LANES
    step = tq * nsub
    return pl.pallas_call(
        functools.partial(_attn_kernel, tq=tq, nsub=nsub),
        grid=(b, n_pairs, s // step),
        in_specs=[
            pl.BlockSpec((1, step, LANES), lambda b, p, i: (b, i, p)),
            pl.BlockSpec((1, s, LANES), lambda b, p, i: (b, 0, p)),
            pl.BlockSpec((1, s, LANES), lambda b, p, i: (b, 0, p)),
            pl.BlockSpec((LANES, LANES), lambda b, p, i: (0, p)),
            pl.BlockSpec((LANES, LANES), lambda b, p, i: (0, p)),
            pl.BlockSpec((1, LANES), lambda b, p, i: (0, p)),
        ],
        out_specs=pl.BlockSpec((1, step, LANES), lambda b, p, i: (b, i, p)),
        out_shape=jax.ShapeDtypeStruct((b, s, SB_WIDTH), BF16),
        compiler_params=_params(3),
        name="attention",
    )(q, k, v, km, vm, og)


def _mix_kernel(h_ref, halo_ref, mhalo_ref, sb_ref, x_ref, cw_ref, cb_ref, lg_ref, lb_ref, wo_ref,
                g2_ref, wr_ref, br_ref,
                xn_ref, eid_ref, gate_ref, rank_ref, before_ref, cnt_ref, rows_ref,
                win_ref, conv_ref, cv_ref, carry_ref, hn3_ref, posv_ref, pos_ref, dsem, psem, *, tm, chunk, cap):
    b = pl.program_id(0)
    i = pl.program_id(1)
    step = b * pl.num_programs(1) + i

    def pos_copy():
        return pltpu.make_async_copy(posv_ref, pos_ref, psem)

    def row_copy(t, k):
        return pltpu.make_async_copy(_row_tiles(hn3_ref, t), _row_tiles(rows_ref, pos_ref[0, k * tm + t]), dsem)

    @pl.when(step == 0)
    def _():
        carry_ref[...] = jnp.zeros_like(carry_ref)
        hn3_ref[...] = jnp.zeros_like(hn3_ref)
        posv_ref[...] = N_EXPERTS * cap + lax.broadcasted_iota(jnp.int32, posv_ref.shape, 1)
        pos_copy().start()

    pos_copy().wait()

    @pl.when(i == 0)
    def _():
        win_ref[0:HALO, :] = mhalo_ref[...]

    @pl.when(i > 0)
    def _():
        win_ref[0:HALO, :] = halo_ref[0]

    win_ref[HALO:, :] = h_ref[0]

    first_tap = HALO - (CONV_KERNEL - 1)

    def conv_chunk(c, _):
        r0 = pl.multiple_of(c * chunk, chunk)
        rows = chunk + HALO
        for lt in range(cb_ref.shape[1] // LANES):
            sl = slice(lt * LANES, (lt + 1) * LANES)
            window = win_ref[pl.ds(r0, rows), sl]
            acc = jnp.broadcast_to(cb_ref[:, sl], (chunk, LANES))
            for b in range(SUBLANES):
                shifted = window if b == 0 else pltpu.roll(window, shift=rows - b, axis=0)
                for a in range(HALO // SUBLANES + 1):
                    j = SUBLANES * a + b - first_tap
                    if 0 <= j < CONV_KERNEL:
                        acc = acc + cw_ref[j:j + 1, sl] * shifted[SUBLANES * a:SUBLANES * a + chunk, :]
            conv_ref[:, sl] = acc
        for tt in range(chunk):
            for k in range(TOP_K):
                row_copy(r0 + tt, k).start(priority=k % 2)
        acc = conv_ref[...]
        mu = jnp.mean(acc, axis=-1, keepdims=True)
        cen = acc - mu
        var = jnp.mean(cen * cen, axis=-1, keepdims=True)
        y = cen * lax.rsqrt(var + EPS) * lg_ref[...] + lb_ref[...]
        cv_ref[pl.ds(r0, chunk), :] = (y * jax.nn.sigmoid(y)).astype(BF16)
        return 0

    lax.fori_loop(0, tm // chunk, conv_chunk, 0)

    mixed = jnp.dot(sb_ref[0], wo_ref[0:SB_WIDTH, :], preferred_element_type=F32)
    mixed = mixed + jnp.dot(cv_ref[...], wo_ref[SB_WIDTH:, :], preferred_element_type=F32)
    xn = x_ref[0] + mixed
    xn_ref[0] = xn
    ms = jnp.mean(xn * xn, axis=-1, keepdims=True)
    hn = xn * lax.rsqrt(ms + EPS) * g2_ref[...]

    nt = (((1,), (1,)), ((), ()))
    hn_hi = hn.astype(BF16)
    hn_lo = (hn - hn_hi.astype(F32)).astype(BF16)
    both = lax.dot_general(wr_ref[...], hn_hi, nt, preferred_element_type=F32)
    logits = (both[:N_EXPERTS] + both[N_EXPERTS:]
              + lax.dot_general(wr_ref[0:N_EXPERTS, :], hn_lo, nt, preferred_element_type=F32) + br_ref[...])
    eidx = lax.broadcasted_iota(jnp.int32, logits.shape, 0)
    vals, idxs = [], []
    for _ in range(TOP_K):
        m = jnp.max(logits, axis=0, keepdims=True)
        sel = jnp.min(jnp.where(logits == m, eidx, N_EXPERTS), axis=0, keepdims=True)
        vals.append(m)
        idxs.append(sel)
        logits = jnp.where(eidx == sel, -jnp.inf, logits)
    exps = [jnp.exp(v - vals[0]) for v in vals]
    denom = exps[0] + exps[1] + exps[2] + exps[3]
    gate_ref[...] = jnp.concatenate([e / denom for e in exps], axis=0)
    eid_ref[...] = jnp.concatenate(idxs, axis=0)

    onehots = [(eidx == s).astype(F32) for s in idxs]
    chosen = onehots[0] + onehots[1] + onehots[2] + onehots[3]
    tr = lax.broadcasted_iota(jnp.int32, (tm, tm), 0)
    tc = lax.broadcasted_iota(jnp.int32, (tm, tm), 1)
    before = (tr < tc).astype(BF16)
    prefix = jnp.dot(chosen.astype(BF16), before, preferred_element_type=F32) + carry_ref[...]
    ranks = [jnp.sum(o * prefix, axis=0, keepdims=True).astype(jnp.int32) for o in onehots]
    rank_ref[...] = jnp.concatenate(ranks, axis=0)
    before_ref[...] = prefix.astype(jnp.int32)
    carry_ref[...] = carry_ref[...] + jnp.sum(chosen, axis=1, keepdims=True)
    cnt_ref[...] = jnp.broadcast_to(carry_ref[...], cnt_ref.shape)

    _wait_rows(rows_ref, TOP_K * tm, dsem)
    groups = hn.shape[1] // LANES
    for c in range(groups):
        hn3_ref[pl.ds(c, tm, stride=groups), :] = hn[:, c * LANES:(c + 1) * LANES]
    for k in range(TOP_K):
        posv_ref[:, k * tm:(k + 1) * tm] = idxs[k] * cap + ranks[k]
    pos_copy().start()

    @pl.when(step == pl.num_programs(0) * pl.num_programs(1) - 1)
    def _():
        pos_copy().wait()

        def issue(t, _):
            for k in range(TOP_K):
                row_copy(t, k).start(priority=k % 2)
            return 0

        lax.fori_loop(0, tm, issue, 0, unroll=2)
        _wait_rows(rows_ref, TOP_K * tm, dsem)


def _mix(hcv, mhalo, sb, x, conv_w, conv_b, ln_g, ln_b, w_out_bf, g2, wr_t, br, tm, chunk, cap):
    b, s, d = x.shape
    cw = hcv.shape[-1]
    t = b * s
    per = s // tm
    assert d == SUBLANES * LANES, d
    tile = lambda b, i: (b, i, 0)
    fixed = lambda b, i: (0, 0)
    tok = lambda b, i: (0, b * per + i)
    x_rows = N_EXPERTS * cap + TOP_K * tm
    return pl.pallas_call(
        functools.partial(_mix_kernel, tm=tm, chunk=chunk, cap=cap),
        grid=(b, per),
        in_specs=[
            pl.BlockSpec((1, tm, cw), tile),
            pl.BlockSpec((1, HALO, cw), lambda b, i: (b, jnp.maximum(i * (tm // HALO) - 1, 0), 0)),
            pl.BlockSpec((HALO, cw), fixed),
            pl.BlockSpec((1, tm, SB_WIDTH), tile),
            pl.BlockSpec((1, tm, d), tile),
            pl.BlockSpec((CONV_KERNEL, cw), fixed),
            pl.BlockSpec((1, cw), fixed),
            pl.BlockSpec((1, cw), fixed),
            pl.BlockSpec((1, cw), fixed),
            pl.BlockSpec((SB_WIDTH + cw, d), fixed),
            pl.BlockSpec((1, d), fixed),
            pl.BlockSpec((2 * N_EXPERTS, d), fixed),
            pl.BlockSpec((N_EXPERTS, 1), fixed),
        ],
        out_specs=[
            pl.BlockSpec((1, tm, d), tile),
            pl.BlockSpec((TOP_K, tm), tok),
            pl.BlockSpec((TOP_K, tm), tok),
            pl.BlockSpec((TOP_K, tm), tok),
            pl.BlockSpec((N_EXPERTS, tm), tok),
            pl.BlockSpec((N_EXPERTS, LANES), fixed),
            pl.BlockSpec(memory_space=pl.ANY),
        ],
        out_shape=[
            jax.ShapeDtypeStruct((b, s, d), F32),
            jax.ShapeDtypeStruct((TOP_K, t), jnp.int32),
            jax.ShapeDtypeStruct((TOP_K, t), F32),
            jax.ShapeDtypeStruct((TOP_K, t), jnp.int32),
            jax.ShapeDtypeStruct((N_EXPERTS, t), jnp.int32),
            jax.ShapeDtypeStruct((N_EXPERTS, LANES), F32),
            jax.ShapeDtypeStruct((x_rows * SUBLANES, LANES), F32),
        ],
        scratch_shapes=[
            pltpu.VMEM((tm + HALO, cw), F32),
            pltpu.VMEM((chunk, cw), F32),
            pltpu.VMEM((tm, cw), BF16),
            pltpu.VMEM((N_EXPERTS, 1), F32),
            pltpu.VMEM((tm * SUBLANES, LANES), F32),
            pltpu.VMEM((1, TOP_K * tm), jnp.int32),
            pltpu.SMEM((1, TOP_K * tm), jnp.int32),
            pltpu.SemaphoreType.DMA(()),
            pltpu.SemaphoreType.DMA(()),
        ],
        compiler_params=_params(2),
        name="mix_router",
    )(hcv, hcv, mhalo, sb, x, conv_w, conv_b, ln_g, ln_b, w_out_bf, g2, wr_t, br)


def _row_tiles(ref, row, n=1):
    size = n * SUBLANES if isinstance(n, int) else pl.multiple_of(n * SUBLANES, SUBLANES)
    return ref.at[pl.ds(pl.multiple_of(row * SUBLANES, SUBLANES), size)]


def _wait_rows(rows_ref, n, sem):
    pltpu.make_async_copy(_row_tiles(rows_ref, 0, n), _row_tiles(rows_ref, 0, n), sem).wait()


def _padfill_kernel(padlo_ref, padlen_ref, rows_in_ref, rows_ref, zeros_ref, zsem):
    del rows_in_ref
    zeros_ref[...] = jnp.zeros_like(zeros_ref)
    total = jnp.int32(0)
    for e in range(N_EXPERTS):
        lo, ln = padlo_ref[e], padlen_ref[e]
        total = total + ln
        for bit in range(EXPERT_HALF.bit_length() - 1):
            size = 1 << bit

            @pl.when((ln >> bit) & 1 == 1)
            def _():
                start = lo + ((ln >> (bit + 1)) << (bit + 1))
                pltpu.make_async_copy(_row_tiles(zeros_ref, 0, size), _row_tiles(rows_ref, start, size),
                                      zsem).start()

    @pl.when(total > 0)
    def _():
        _wait_rows(rows_ref, total, zsem)


def _padfill(pad_lo, pad_len, x_rows):
    smem = pl.BlockSpec(memory_space=pltpu.SMEM)
    return pl.pallas_call(
        _padfill_kernel,
        in_specs=[smem, smem, pl.BlockSpec(memory_space=pl.ANY)],
        out_specs=pl.BlockSpec(memory_space=pl.ANY),
        out_shape=jax.ShapeDtypeStruct(x_rows.shape, x_rows.dtype),
        scratch_shapes=[pltpu.VMEM((EXPERT_HALF * SUBLANES // 2, LANES), x_rows.dtype), pltpu.SemaphoreType.DMA(())],
        input_output_aliases={2: 0},
        compiler_params=pltpu.CompilerParams(vmem_limit_bytes=VMEM_LIMIT),
        name="padfill",
    )(pad_lo, pad_len, x_rows)


def _expert_kernel(be_ref, bsrc_ref, valid_ref, nused_ref, x_ref, wu_ref, bu_ref, wd_ref, bd_ref, y_ref,
                   wu_bf, wd_bf, *, cast_rows):
    del bsrc_ref, nused_ref
    j = pl.program_id(0)
    de = wd_ref.shape[1]
    valid = valid_ref[j]

    def half(h):
        x = jnp.concatenate([x_ref[pl.ds(h * EXPERT_HALF * SUBLANES + c, EXPERT_HALF, stride=SUBLANES), :]
                             for c in range(SUBLANES)], axis=1)
        up = jnp.dot(x.astype(BF16), wu_bf[...], preferred_element_type=F32) + bu_ref[0]
        glu = jnp.minimum(up[:, :de], SWIGLU_LIMIT)
        lin = jnp.clip(up[:, de:], -SWIGLU_LIMIT, SWIGLU_LIMIT)
        act = glu * jax.nn.sigmoid(SWIGLU_ALPHA * glu) * (lin + 1.0)
        y = jnp.dot(act.astype(BF16), wd_bf[...], preferred_element_type=F32) + bd_ref[0]
        y_ref[h * EXPERT_HALF:(h + 1) * EXPERT_HALF, :] = _pack_halves(y)

    @pl.when(valid == 0)
    def _():
        y_ref[...] = jnp.zeros_like(y_ref)

    @pl.when(valid > 0)
    def _():
        prev = be_ref[jnp.maximum(j - 1, 0)]

        @pl.when(jnp.logical_or(j == 0, be_ref[j] != prev))
        def _():
            def cast(c, _):
                r0 = pl.multiple_of(c * cast_rows, cast_rows)
                wu_bf[pl.ds(r0, cast_rows), :] = wu_ref[0, pl.ds(r0, cast_rows), :].astype(BF16)
                wd_bf[pl.ds(r0, cast_rows), :] = wd_ref[0, pl.ds(r0, cast_rows), :].astype(BF16)
                return 0

            lax.fori_loop(0, wu_ref.shape[1] // cast_rows, cast, 0)

    @pl.when(valid > EXPERT_HALF)
    def _():
        half(0)
        half(1)

    @pl.when(jnp.logical_and(valid > 0, valid <= EXPERT_HALF))
    def _():
        half(0)
        y_ref[EXPERT_HALF:, :] = jnp.zeros((EXPERT_ROWS - EXPERT_HALF, y_ref.shape[1]), y_ref.dtype)


def _pack_halves(y):
    half = y.shape[1] // 2
    bits = lambda v: lax.bitcast_convert_type(v.astype(BF16).astype(F32), jnp.uint32)
    return (bits(y[:, half:]) & jnp.uint32(0xFFFF0000)) | (bits(y[:, :half]) >> 16)


def _unpack_halves(w):
    lo = lax.bitcast_convert_type(w << 16, F32).astype(BF16)
    hi = lax.bitcast_convert_type(w & jnp.uint32(0xFFFF0000), F32).astype(BF16)
    return lo, hi


def _experts(block_expert, block_src, block_valid, n_used, x_rows, w_up, b_up, w_down, b_down):
    n_blocks = block_expert.shape[0] + 1
    block_valid = jnp.concatenate([block_valid, jnp.zeros((1,), jnp.int32)])
    ne, d, up_cols = w_up.shape
    de = w_down.shape[1]
    blk = lambda j, be, bs, bv, nu: (bs[jnp.minimum(j, nu[0] - 1)], 0)
    wsel = lambda j, be, bs, bv, nu: (be[jnp.minimum(j, nu[0] - 1)], 0, 0)
    grid_spec = pltpu.PrefetchScalarGridSpec(
        num_scalar_prefetch=4,
        grid=(n_blocks,),
        in_specs=[
            pl.BlockSpec((EXPERT_ROWS * SUBLANES, LANES), blk),
            pl.BlockSpec((1, d, up_cols), wsel),
            pl.BlockSpec((1, 1, up_cols), wsel),
            pl.BlockSpec((1, de, d), wsel),
            pl.BlockSpec((1, 1, d), wsel),
        ],
        out_specs=pl.BlockSpec((EXPERT_ROWS, d // 2), lambda j, be, bs, bv, nu: (j, 0)),
        scratch_shapes=[pltpu.VMEM((d, up_cols), BF16), pltpu.VMEM((de, d), BF16)],
    )
    return pl.pallas_call(
        functools.partial(_expert_kernel, cast_rows=64),
        grid_spec=grid_spec,
        out_shape=jax.ShapeDtypeStruct((n_blocks * EXPERT_ROWS, d // 2), jnp.uint32),
        compiler_params=_params(1),
        name="experts",
    )(block_expert, block_src, block_valid, n_used, x_rows, w_up, b_up.reshape(ne, 1, up_cols),
      w_down, b_down.reshape(ne, 1, d))


def _combine_plan(run_start, run_cnt, seg_start, eid, rank, tm, max_pieces):
    first = (run_start >> 3) << 3
    pieces = jnp.where(run_cnt > 0, (run_start - first + run_cnt + PIECE - 1) // PIECE, 0)
    piece_end = jnp.cumsum(pieces, axis=0)
    piece_off = piece_end - pieces
    p = jnp.arange(max_pieces, dtype=jnp.int32)
    expert_of = jnp.minimum(jnp.sum(p[None, :, None] >= piece_end.T[:, None, :], axis=-1), N_EXPERTS - 1)
    onehot = expert_of[:, :, None] == jnp.arange(N_EXPERTS, dtype=jnp.int32)
    src = jnp.sum(jnp.where(onehot, (first - piece_off * PIECE).T[:, None, :], 0), axis=-1) + p * PIECE
    shift = jnp.repeat(seg_start[:, None] + piece_off * PIECE - first, tm, axis=1)
    experts = jnp.arange(N_EXPERTS, dtype=jnp.int32)[:, None, None]
    staged = rank + jnp.sum(jnp.where(eid[None] == experts, shift[:, None, :], 0), axis=0)
    return src.reshape(-1).astype(jnp.int32), piece_end[-1].astype(jnp.int32), staged.T.astype(jnp.int32)


def _combine_kernel(src_ref, npieces_ref, staged_ref, gate_ref, xn_ref, rows_ref, o_ref, buf, sems,
                    *, tm, max_pieces):
    i = pl.program_id(0)
    n = pl.num_programs(0)
    slot = lax.rem(i, 2)
    half = o_ref.shape[1] // 2

    def fetch(tile, s):
        def body(p, _):
            src = rows_ref.at[pl.ds(pl.multiple_of(src_ref[tile * max_pieces + p], SUBLANES), PIECE)]
            dst = buf.at[s, pl.ds(pl.multiple_of(p * PIECE, PIECE), PIECE)]
            pltpu.make_async_copy(src, dst, sems.at[s]).start()
            return 0

        lax.fori_loop(0, npieces_ref[tile], body, 0)

    @pl.when(i == 0)
    def _():
        buf[...] = jnp.zeros_like(buf)
        fetch(0, 0)

    @pl.when(i + 1 < n)
    def _():
        fetch(i + 1, 1 - slot)

    total = npieces_ref[i] * PIECE
    staged = staged_ref[...]
    gates = gate_ref[...]

    @pl.when(total > 0)
    def _():
        rows = pl.ds(0, pl.multiple_of(total, PIECE))
        pltpu.make_async_copy(rows_ref.at[rows], buf.at[slot, rows], sems.at[slot]).wait()

    def chunk_products(c0):
        col = lax.broadcasted_iota(jnp.int32, (tm, COMBINE_CHUNK), 1) + c0
        sel = jnp.zeros((tm, COMBINE_CHUNK), F32)
        for k in range(TOP_K):
            sel = jnp.where(staged[:, k:k + 1] == col, gates[:, k:k + 1], sel)
        sel = sel.astype(BF16)
        lo, hi = _unpack_halves(buf[slot, pl.ds(c0, COMBINE_CHUNK), :])
        return jnp.dot(sel, lo, preferred_element_type=F32), jnp.dot(sel, hi, preferred_element_type=F32)

    xn = xn_ref[...]
    acc_lo, acc_hi = xn[:, :half], xn[:, half:]
    for c in range(COMBINE_STATIC_CHUNKS):
        d_lo, d_hi = chunk_products(c * COMBINE_CHUNK)
        acc_lo, acc_hi = acc_lo + d_lo, acc_hi + d_hi
    o_ref[:, :half] = acc_lo
    o_ref[:, half:] = acc_hi

    def chunk(c, _):
        d_lo, d_hi = chunk_products(pl.multiple_of(c * COMBINE_CHUNK, COMBINE_CHUNK))
        o_ref[:, :half] += d_lo
        o_ref[:, half:] += d_hi
        return 0

    lax.fori_loop(COMBINE_STATIC_CHUNKS, (total + COMBINE_CHUNK - 1) // COMBINE_CHUNK, chunk, 0)


def _combine(run_start, run_cnt, seg_start, eid, rank, gates_t, xn, y_rows, tm):
    t, d = xn.shape
    n = t // tm
    cap = -(-(TOP_K * tm + 2 * PIECE * N_EXPERTS) // COMBINE_CHUNK) * COMBINE_CHUNK
    max_pieces = cap // PIECE
    src, npieces, staged_t = _combine_plan(run_start, run_cnt, seg_start, eid, rank, tm, max_pieces)
    tok = lambda i, *_: (i, 0)
    grid_spec = pltpu.PrefetchScalarGridSpec(
        num_scalar_prefetch=2,
        grid=(n,),
        in_specs=[
            pl.BlockSpec((tm, TOP_K), tok),
            pl.BlockSpec((tm, TOP_K), tok),
            pl.BlockSpec((tm, d), tok),
            pl.BlockSpec(memory_space=pl.ANY),
        ],
        out_specs=pl.BlockSpec((tm, d), tok),
        scratch_shapes=[pltpu.VMEM((2, cap, d // 2), jnp.uint32), pltpu.SemaphoreType.DMA((2,))],
    )
    return pl.pallas_call(
        functools.partial(_combine_kernel, tm=tm, max_pieces=max_pieces),
        grid_spec=grid_spec,
        out_shape=jax.ShapeDtypeStruct((t, d), F32),
        compiler_params=_params(1),
        name="combine",
    )(src, npieces, staged_t, gates_t, xn, y_rows)


def _tile(n, want):
    t = min(n, want)
    assert n % t == 0, (n, t)
    return t


def kernel(x, meta_tokens, norm1_g, w_in, q_norm_g, k_norm_g, conv_w, conv_b, conv_ln_g, conv_ln_b,
           attn_out_g, w_out, norm2_g, w_router, b_router, w_up, b_up, w_down, b_down):
    assert norm1_g.shape[0] == 1, "single layer: meta-token rows are only keys/values and conv context"
    b, s, d = x.shape
    t = b * s
    cw = conv_w.shape[-1]

    g1 = norm1_g[0][None, :]
    w_in_bf = w_in[0].astype(BF16)
    qg = jnp.tile(q_norm_g[0], SB_HEADS)[None, :]
    kg = jnp.tile(k_norm_g[0], SB_HEADS)[None, :]

    q, k, v, hcv = _inproj(x.reshape(t, d), g1, w_in_bf, qg, kg, _tile(t, 1024))
    _, km, vm, hm = _inproj(meta_tokens, g1, w_in_bf, qg, kg, N_META)
    pad = ((0, LANES - N_META), (0, 0))
    tq = _tile(s, 256)
    sb = _attention(q.reshape(b, s, SB_WIDTH), k.reshape(b, s, SB_WIDTH), v.reshape(b, s, SB_WIDTH),
                    jnp.pad(km, pad), jnp.pad(vm, pad), attn_out_g[0].reshape(1, SB_WIDTH), tq,
                    _tile(s, ATTN_TILES_PER_STEP * tq) // tq)

    mhalo = jnp.concatenate([jnp.zeros((HALO - N_META, cw), F32), hm], axis=0)
    tm = _tile(s, 512)
    wr_t = w_router[0].T
    wr_hi = wr_t.astype(BF16)
    wr_lo = (wr_t - wr_hi.astype(F32)).astype(BF16)
    cap = -(-t // EXPERT_ROWS) * EXPERT_ROWS
    xn, eid, gates, rank, before, counts, x_rows = _mix(
        hcv.reshape(b, s, cw), mhalo, sb, x, conv_w[0], conv_b[0][None, :], conv_ln_g[0][None, :],
        conv_ln_b[0][None, :], w_out[0].astype(BF16), norm2_g[0][None, :],
        jnp.concatenate([wr_hi, wr_lo], axis=0), b_router[0][:, None], tm, _tile(tm, 64), cap)

    counts = counts[:, 0].astype(jnp.int32)
    padded = (counts + EXPERT_ROWS - 1) // EXPERT_ROWS * EXPERT_ROWS
    pad_end = jnp.cumsum(padded)
    pad_start = pad_end - padded
    n_blocks = t * TOP_K // EXPERT_ROWS + N_EXPERTS
    block_row = jnp.arange(n_blocks, dtype=jnp.int32) * EXPERT_ROWS
    block_expert = jnp.minimum(jnp.sum(block_row[:, None] >= pad_end[None, :], axis=1), N_EXPERTS - 1).astype(jnp.int32)
    block_off = block_row - pad_start[block_expert]
    block_src = (block_expert * cap + jnp.maximum(block_off, 0)) // EXPERT_ROWS
    block_valid = jnp.clip(counts[block_expert] - block_off, 0, EXPERT_ROWS)
    block_valid = jnp.where(block_row < pad_end[-1], block_valid, 0)
    n_used = (pad_end[-1:] // EXPERT_ROWS).astype(jnp.int32)
    seg = jnp.arange(N_EXPERTS, dtype=jnp.int32) * cap
    x_rows = _padfill(seg + counts, (-counts) % EXPERT_HALF, x_rows)
    y_rows = _experts(block_expert, block_src.astype(jnp.int32), block_valid.astype(jnp.int32), n_used, x_rows,
                      w_up[0], b_up[0], w_down[0], b_down[0])
    tc = _tile(t, 256)
    run_start = pad_start[:, None] + before[:, ::tc]
    run_end = jnp.concatenate([run_start[:, 1:], (pad_start + counts)[:, None]], axis=1)
    out = _combine(run_start, run_end - run_start, pad_start, eid, rank, gates.T, xn.reshape(t, d), y_rows, tc)
    return out.reshape(b, s, d)
```

```python
import functools

import jax
import jax.numpy as jnp
from jax import lax
from jax.experimental import pallas as pl
from jax.experimental.pallas import tpu as pltpu

N_META = 16
SB_HEADS = 8
SB_HEAD_DIM = 64
SB_WIDTH = SB_HEADS * SB_HEAD_DIM
CONV_KERNEL = 31
N_EXPERTS = 32
TOP_K = 4
SWIGLU_LIMIT = 7.0
SWIGLU_ALPHA = 1.702
EPS = 1e-6
F32_EXP2_UNDERFLOW = -150.0
LOG2_E = 1.4426950408889634

LANES = 128
SUBLANES = 8
INPROJ_ROWS = 1024
ATTN_TILE = 256
ATTN_TILES_PER_STEP = 8
MIX_ROWS = 1024
CONV_CHUNK = 64
COMBINE_ROWS = 256
WEIGHT_CAST_ROWS = 64
PIECE = 16
COMBINE_CHUNK = 512
COMBINE_STATIC_CHUNKS = 3
HALO = 32
EXPERT_HALF = 512
EXPERT_ROWS = 2 * EXPERT_HALF
VMEM_LIMIT = 56 * 1024 * 1024

F32 = jnp.float32
BF16 = jnp.bfloat16


def _params(n_axes, vmem=VMEM_LIMIT):
    return pltpu.CompilerParams(dimension_semantics=("arbitrary",) * n_axes, vmem_limit_bytes=vmem)


def _inproj_kernel(x_ref, g1_ref, w_ref, qg_ref, kg_ref, q_ref, k_ref, v_ref, h_ref):
    x = x_ref[...]
    ms = jnp.mean(x * x, axis=-1, keepdims=True)
    n = (x * lax.rsqrt(ms + EPS) * g1_ref[...]).astype(BF16)
    lo = lax.broadcasted_iota(jnp.int32, (1, LANES), 1) < SB_HEAD_DIM

    def proj(c0, c1):
        return jnp.dot(n, w_ref[:, c0:c1], preferred_element_type=F32)

    def head_norm(acc, g_ref, out_ref, scale):
        for c in range(SB_WIDTH // LANES):
            sl = slice(c * LANES, (c + 1) * LANES)
            a = acc[:, sl]
            sq = a * a
            s_lo = jnp.sum(jnp.where(lo, sq, 0.0), axis=-1, keepdims=True)
            s_hi = jnp.sum(jnp.where(lo, 0.0, sq), axis=-1, keepdims=True)
            r = lax.rsqrt(jnp.where(lo, s_lo, s_hi) * (1.0 / SB_HEAD_DIM) + EPS)
            out_ref[:, sl] = (a * r * (g_ref[:, sl] * scale)).astype(BF16)

    head_norm(proj(0, SB_WIDTH), qg_ref, q_ref, LOG2_E * SB_HEAD_DIM ** -0.5)
    head_norm(proj(SB_WIDTH, 2 * SB_WIDTH), kg_ref, k_ref, 1.0)
    v_ref[...] = proj(2 * SB_WIDTH, 3 * SB_WIDTH).astype(BF16)
    cw = (w_ref.shape[1] - 3 * SB_WIDTH) // 2
    val = proj(3 * SB_WIDTH, 3 * SB_WIDTH + cw)
    gate = proj(3 * SB_WIDTH + cw, 3 * SB_WIDTH + 2 * cw)
    h_ref[...] = val * jax.nn.sigmoid(gate)


def _inproj(x2, g1, w_in_bf, qg, kg, tm):
    t, d = x2.shape
    cols = w_in_bf.shape[1]
    cw = (cols - 3 * SB_WIDTH) // 2
    row = lambda i: (i, 0)
    fixed = lambda i: (0, 0)
    return pl.pallas_call(
        _inproj_kernel,
        grid=(t // tm,),
        in_specs=[
            pl.BlockSpec((tm, d), row),
            pl.BlockSpec((1, d), fixed),
            pl.BlockSpec((d, cols), fixed),
            pl.BlockSpec((1, SB_WIDTH), fixed),
            pl.BlockSpec((1, SB_WIDTH), fixed),
        ],
        out_specs=[
            pl.BlockSpec((tm, SB_WIDTH), row),
            pl.BlockSpec((tm, SB_WIDTH), row),
            pl.BlockSpec((tm, SB_WIDTH), row),
            pl.BlockSpec((tm, cw), row),
        ],
        out_shape=[
            jax.ShapeDtypeStruct((t, SB_WIDTH), BF16),
            jax.ShapeDtypeStruct((t, SB_WIDTH), BF16),
            jax.ShapeDtypeStruct((t, SB_WIDTH), BF16),
            jax.ShapeDtypeStruct((t, cw), F32),
        ],
        compiler_params=_params(1),
        name="inproj",
    )(x2, g1, w_in_bf, qg, kg)


def _attn_kernel(q_ref, k_ref, v_ref, km_ref, vm_ref, g_ref, o_ref, *, tq, nsub):
    i = pl.program_id(2)
    lane = lax.broadcasted_iota(jnp.int32, (1, LANES), 1)
    lo = lane < SB_HEAD_DIM
    zero_bf = jnp.zeros((), BF16)

    def stacked_q(sub):
        q2 = q_ref[0, sub * tq:(sub + 1) * tq, :]
        return jnp.concatenate([jnp.where(lo, q2, zero_bf), jnp.where(lo, zero_bf, q2)], axis=0)

    qss = [stacked_q(sub) for sub in range(nsub)]
    row = lax.broadcasted_iota(jnp.int32, (2 * tq, tq), 0)
    col = lax.broadcasted_iota(jnp.int32, (2 * tq, tq), 1)
    causal = col < jnp.where(row >= tq, row - tq, row)
    srow = lax.broadcasted_iota(jnp.int32, (tq, tq), 0)
    scol = lax.broadcasted_iota(jnp.int32, (tq, tq), 1)
    neg_suffix = jnp.where(srow > scol, -1.0, 0.0).astype(BF16)

    def block(qs, kb, vb, acc, r, mask, neg_suffix_m):
        z = lax.dot_general(qs, kb, (((1,), (1,)), ((), ())), preferred_element_type=F32)
        m = jnp.minimum(z, 0.0)
        p = jnp.maximum(z, 0.0)
        l1p = jnp.log2(1.0 + jnp.exp2(m - p))
        log_beta = m - l1p
        neg_keep = p + l1p
        if mask is not None:
            neg_keep = jnp.where(mask, neg_keep, 0.0)
        later = jnp.dot(neg_keep.astype(BF16), neg_suffix_m, preferred_element_type=F32)
        a = jnp.exp2(log_beta + later + r)
        if mask is not None:
            a = jnp.where(mask, a, 0.0)
        res = jnp.dot(a.astype(BF16), vb, preferred_element_type=F32)
        acc = acc + jnp.where(lo, res[:tq], res[tq:])
        return acc, r - jnp.sum(neg_keep, axis=-1, keepdims=True)

    def kv_block(j):
        start = pl.multiple_of(j * tq, tq)
        return k_ref[0, pl.ds(start, tq), :], v_ref[0, pl.ds(start, tq), :]

    def live(r):
        return jnp.max(r) > F32_EXP2_UNDERFLOW

    acc0 = jnp.zeros((tq, LANES), F32)
    r0 = jnp.zeros((2 * tq, 1), F32)
    first = i * nsub

    def guaranteed(first_has_previous):
        state = []
        for sub in range(nsub):
            acc, r = block(qss[sub], *kv_block(first + sub), acc0, r0, causal, neg_suffix)
            if sub > 0 or first_has_previous:
                acc, r = block(qss[sub], *kv_block(first + sub - 1), acc, r, None, neg_suffix)
            state += [acc, r]
        return tuple(state)

    state = lax.cond(i > 0, lambda _: guaranteed(True), lambda _: guaranteed(False), 0)

    for sub in range(nsub):
        j = first + sub
        qs, acc, r = qss[sub], state[2 * sub], state[2 * sub + 1]

        def cond(c):
            return jnp.logical_and(c[0] <= j, c[3])

        def body(c):
            acc, r = block(qs, *kv_block(j - c[0]), c[1], c[2], None, neg_suffix)
            return c[0] + 1, acc, r, live(r)

        _, acc, r, alive = lax.while_loop(cond, body, (jnp.int32(2), acc, r, live(r)))

        def meta_block(acc):
            return block(qs, km_ref[...], vm_ref[...], acc, r, lane < N_META, neg_suffix[:LANES, :LANES])[0]

        acc = lax.cond(alive, meta_block, lambda acc: acc, acc)

        sq = acc * acc
        s_lo = jnp.sum(jnp.where(lo, sq, 0.0), axis=-1, keepdims=True)
        s_hi = jnp.sum(jnp.where(lo, 0.0, sq), axis=-1, keepdims=True)
        rn = lax.rsqrt(jnp.where(lo, s_lo, s_hi) * (1.0 / SB_HEAD_DIM) + EPS)
        o_ref[0, sub * tq:(sub + 1) * tq, :] = (acc * rn * g_ref[...]).astype(BF16)


def _attention(q, k, v, km, vm, og, tq, nsub):
    b, s, _ = q.shape
    n_pairs = SB_WIDTH // LANES
    step = tq * nsub
    return pl.pallas_call(
        functools.partial(_attn_kernel, tq=tq, nsub=nsub),
        grid=(b, n_pairs, s // step),
        in_specs=[
            pl.BlockSpec((1, step, LANES), lambda b, p, i: (b, i, p)),
            pl.BlockSpec((1, s, LANES), lambda b, p, i: (b, 0, p)),
            pl.BlockSpec((1, s, LANES), lambda b, p, i: (b, 0, p)),
            pl.BlockSpec((LANES, LANES), lambda b, p, i: (0, p)),
            pl.BlockSpec((LANES, LANES), lambda b, p, i: (0, p)),
            pl.BlockSpec((1, LANES), lambda b, p, i: (0, p)),
        ],
        out_specs=pl.BlockSpec((1, step, LANES), lambda b, p, i: (b, i, p)),
        out_shape=jax.ShapeDtypeStruct((b, s, SB_WIDTH), BF16),
        compiler_params=_params(3),
        name="attention",
    )(q, k, v, km, vm, og)


def _mix_kernel(h_ref, halo_ref, mhalo_ref, sb_ref, x_ref, cw_ref, cb_ref, lg_ref, lb_ref, wo_ref,
                g2_ref, wr_ref, br_ref,
                xn_ref, eid_ref, gate_ref, rank_ref, before_ref, cnt_ref, rows_ref,
                win_ref, conv_ref, cv_ref, carry_ref, hn3_ref, posv_ref, pos_ref, dsem, psem, *, tm, chunk, cap):
    b = pl.program_id(0)
    i = pl.program_id(1)
    step = b * pl.num_programs(1) + i

    def pos_copy():
        return pltpu.make_async_copy(posv_ref, pos_ref, psem)

    def row_copy(t, k):
        return pltpu.make_async_copy(_row_tiles(hn3_ref, t), _row_tiles(rows_ref, pos_ref[0, k * tm + t]), dsem)

    @pl.when(step == 0)
    def _():
        carry_ref[...] = jnp.zeros_like(carry_ref)
        hn3_ref[...] = jnp.zeros_like(hn3_ref)
        posv_ref[...] = N_EXPERTS * cap + lax.broadcasted_iota(jnp.int32, posv_ref.shape, 1)
        pos_copy().start()

    pos_copy().wait()

    @pl.when(i == 0)
    def _():
        win_ref[0:HALO, :] = mhalo_ref[...]

    @pl.when(i > 0)
    def _():
        win_ref[0:HALO, :] = halo_ref[0]

    win_ref[HALO:, :] = h_ref[0]

    first_tap = HALO - (CONV_KERNEL - 1)

    def conv_chunk(c, _):
        r0 = pl.multiple_of(c * chunk, chunk)
        rows = chunk + HALO
        for lt in range(cb_ref.shape[1] // LANES):
            sl = slice(lt * LANES, (lt + 1) * LANES)
            window = win_ref[pl.ds(r0, rows), sl]
            acc = jnp.broadcast_to(cb_ref[:, sl], (chunk, LANES))
            for b in range(SUBLANES):
                shifted = window if b == 0 else pltpu.roll(window, shift=rows - b, axis=0)
                for a in range(HALO // SUBLANES + 1):
                    j = SUBLANES * a + b - first_tap
                    if 0 <= j < CONV_KERNEL:
                        acc = acc + cw_ref[j:j + 1, sl] * shifted[SUBLANES * a:SUBLANES * a + chunk, :]
            conv_ref[:, sl] = acc
        for tt in range(chunk):
            for k in range(TOP_K):
                row_copy(r0 + tt, k).start(priority=k % 2)
        acc = conv_ref[...]
        mu = jnp.mean(acc, axis=-1, keepdims=True)
        cen = acc - mu
        var = jnp.mean(cen * cen, axis=-1, keepdims=True)
        y = cen * lax.rsqrt(var + EPS) * lg_ref[...] + lb_ref[...]
        cv_ref[pl.ds(r0, chunk), :] = (y * jax.nn.sigmoid(y)).astype(BF16)
        return 0

    lax.fori_loop(0, tm // chunk, conv_chunk, 0)

    mixed = jnp.dot(sb_ref[0], wo_ref[0:SB_WIDTH, :], preferred_element_type=F32)
    mixed = mixed + jnp.dot(cv_ref[...], wo_ref[SB_WIDTH:, :], preferred_element_type=F32)
    xn = x_ref[0] + mixed
    xn_ref[0] = xn
    ms = jnp.mean(xn * xn, axis=-1, keepdims=True)
    hn = xn * lax.rsqrt(ms + EPS) * g2_ref[...]

    nt = (((1,), (1,)), ((), ()))
    hn_hi = hn.astype(BF16)
    hn_lo = (hn - hn_hi.astype(F32)).astype(BF16)
    both = lax.dot_general(wr_ref[...], hn_hi, nt, preferred_element_type=F32)
    logits = (both[:N_EXPERTS] + both[N_EXPERTS:]
              + lax.dot_general(wr_ref[0:N_EXPERTS, :], hn_lo, nt, preferred_element_type=F32) + br_ref[...])
    eidx = lax.broadcasted_iota(jnp.int32, logits.shape, 0)
    vals, idxs = [], []
    for _ in range(TOP_K):
        m = jnp.max(logits, axis=0, keepdims=True)
        sel = jnp.min(jnp.where(logits == m, eidx, N_EXPERTS), axis=0, keepdims=True)
        vals.append(m)
        idxs.append(sel)
        logits = jnp.where(eidx == sel, -jnp.inf, logits)
    exps = [jnp.exp(v - vals[0]) for v in vals]
    denom = exps[0] + exps[1] + exps[2] + exps[3]
    gate_ref[...] = jnp.concatenate([e / denom for e in exps], axis=0)
    eid_ref[...] = jnp.concatenate(idxs, axis=0)

    onehots = [(eidx == s).astype(F32) for s in idxs]
    chosen = onehots[0] + onehots[1] + onehots[2] + onehots[3]
    tr = lax.broadcasted_iota(jnp.int32, (tm, tm), 0)
    tc = lax.broadcasted_iota(jnp.int32, (tm, tm), 1)
    before = (tr < tc).astype(BF16)
    prefix = jnp.dot(chosen.astype(BF16), before, preferred_element_type=F32) + carry_ref[...]
    ranks = [jnp.sum(o * prefix, axis=0, keepdims=True).astype(jnp.int32) for o in onehots]
    rank_ref[...] = jnp.concatenate(ranks, axis=0)
    before_ref[...] = prefix.astype(jnp.int32)
    carry_ref[...] = carry_ref[...] + jnp.sum(chosen, axis=1, keepdims=True)
    cnt_ref[...] = jnp.broadcast_to(carry_ref[...], cnt_ref.shape)

    _wait_rows(rows_ref, TOP_K * tm, dsem)
    groups = hn.shape[1] // LANES
    for c in range(groups):
        hn3_ref[pl.ds(c, tm, stride=groups), :] = hn[:, c * LANES:(c + 1) * LANES]
    for k in range(TOP_K):
        posv_ref[:, k * tm:(k + 1) * tm] = idxs[k] * cap + ranks[k]
    pos_copy().start()

    @pl.when(step == pl.num_programs(0) * pl.num_programs(1) - 1)
    def _():
        pos_copy().wait()

        def issue(t, _):
            for k in range(TOP_K):
                row_copy(t, k).start(priority=k % 2)
            return 0

        lax.fori_loop(0, tm, issue, 0, unroll=2)
        _wait_rows(rows_ref, TOP_K * tm, dsem)


def _mix(hcv, mhalo, sb, x, conv_w, conv_b, ln_g, ln_b, w_out_bf, g2, wr_t, br, tm, chunk, cap):
    b, s, d = x.shape
    cw = hcv.shape[-1]
    t = b * s
    per = s // tm
    assert d == SUBLANES * LANES, d
    tile = lambda b, i: (b, i, 0)
    fixed = lambda b, i: (0, 0)
    tok = lambda b, i: (0, b * per + i)
    x_rows = N_EXPERTS * cap + TOP_K * tm
    return pl.pallas_call(
        functools.partial(_mix_kernel, tm=tm, chunk=chunk, cap=cap),
        grid=(b, per),
        in_specs=[
            pl.BlockSpec((1, tm, cw), tile),
            pl.BlockSpec((1, HALO, cw), lambda b, i: (b, jnp.maximum(i * (tm // HALO) - 1, 0), 0)),
            pl.BlockSpec((HALO, cw), fixed),
            pl.BlockSpec((1, tm, SB_WIDTH), tile),
            pl.BlockSpec((1, tm, d), tile),
            pl.BlockSpec((CONV_KERNEL, cw), fixed),
            pl.BlockSpec((1, cw), fixed),
            pl.BlockSpec((1, cw), fixed),
            pl.BlockSpec((1, cw), fixed),
            pl.BlockSpec((SB_WIDTH + cw, d), fixed),
            pl.BlockSpec((1, d), fixed),
            pl.BlockSpec((2 * N_EXPERTS, d), fixed),
            pl.BlockSpec((N_EXPERTS, 1), fixed),
        ],
        out_specs=[
            pl.BlockSpec((1, tm, d), tile),
            pl.BlockSpec((TOP_K, tm), tok),
            pl.BlockSpec((TOP_K, tm), tok),
            pl.BlockSpec((TOP_K, tm), tok),
            pl.BlockSpec((N_EXPERTS, tm), tok),
            pl.BlockSpec((N_EXPERTS, LANES), fixed),
            pl.BlockSpec(memory_space=pl.ANY),
        ],
        out_shape=[
            jax.ShapeDtypeStruct((b, s, d), F32),
            jax.ShapeDtypeStruct((TOP_K, t), jnp.int32),
            jax.ShapeDtypeStruct((TOP_K, t), F32),
            jax.ShapeDtypeStruct((TOP_K, t), jnp.int32),
            jax.ShapeDtypeStruct((N_EXPERTS, t), jnp.int32),
            jax.ShapeDtypeStruct((N_EXPERTS, LANES), F32),
            jax.ShapeDtypeStruct((x_rows * SUBLANES, LANES), F32),
        ],
        scratch_shapes=[
            pltpu.VMEM((tm + HALO, cw), F32),
            pltpu.VMEM((chunk, cw), F32),
            pltpu.VMEM((tm, cw), BF16),
            pltpu.VMEM((N_EXPERTS, 1), F32),
            pltpu.VMEM((tm * SUBLANES, LANES), F32),
            pltpu.VMEM((1, TOP_K * tm), jnp.int32),
            pltpu.SMEM((1, TOP_K * tm), jnp.int32),
            pltpu.SemaphoreType.DMA(()),
            pltpu.SemaphoreType.DMA(()),
        ],
        compiler_params=_params(2),
        name="mix_router",
    )(hcv, hcv, mhalo, sb, x, conv_w, conv_b, ln_g, ln_b, w_out_bf, g2, wr_t, br)


def _row_tiles(ref, row, n=1):
    size = n * SUBLANES if isinstance(n, int) else pl.multiple_of(n * SUBLANES, SUBLANES)
    return ref.at[pl.ds(pl.multiple_of(row * SUBLANES, SUBLANES), size)]


def _wait_rows(rows_ref, n, sem):
    pltpu.make_async_copy(_row_tiles(rows_ref, 0, n), _row_tiles(rows_ref, 0, n), sem).wait()


def _padfill_kernel(padlo_ref, padlen_ref, rows_in_ref, rows_ref, zeros_ref, zsem):
    del rows_in_ref
    zeros_ref[...] = jnp.zeros_like(zeros_ref)
    total = jnp.int32(0)
    for e in range(N_EXPERTS):
        lo, ln = padlo_ref[e], padlen_ref[e]
        total = total + ln
        for bit in range(EXPERT_HALF.bit_length() - 1):
            size = 1 << bit

            @pl.when((ln >> bit) & 1 == 1)
            def _():
                start = lo + ((ln >> (bit + 1)) << (bit + 1))
                pltpu.make_async_copy(_row_tiles(zeros_ref, 0, size), _row_tiles(rows_ref, start, size),
                                      zsem).start()

    @pl.when(total > 0)
    def _():
        _wait_rows(rows_ref, total, zsem)


def _padfill(pad_lo, pad_len, x_rows):
    smem = pl.BlockSpec(memory_space=pltpu.SMEM)
    return pl.pallas_call(
        _padfill_kernel,
        in_specs=[smem, smem, pl.BlockSpec(memory_space=pl.ANY)],
        out_specs=pl.BlockSpec(memory_space=pl.ANY),
        out_shape=jax.ShapeDtypeStruct(x_rows.shape, x_rows.dtype),
        scratch_shapes=[pltpu.VMEM((EXPERT_HALF * SUBLANES // 2, LANES), x_rows.dtype), pltpu.SemaphoreType.DMA(())],
        input_output_aliases={2: 0},
        compiler_params=pltpu.CompilerParams(vmem_limit_bytes=VMEM_LIMIT),
        name="padfill",
    )(pad_lo, pad_len, x_rows)


def _expert_kernel(be_ref, bsrc_ref, valid_ref, nused_ref, x_ref, wu_ref, bu_ref, wd_ref, bd_ref, y_ref,
                   wu_bf, wd_bf, *, cast_rows):
    del bsrc_ref, nused_ref
    j = pl.program_id(0)
    de = wd_ref.shape[1]
    valid = valid_ref[j]

    def half(h):
        x = jnp.concatenate([x_ref[pl.ds(h * EXPERT_HALF * SUBLANES + c, EXPERT_HALF, stride=SUBLANES), :]
                             for c in range(SUBLANES)], axis=1)
        up = jnp.dot(x.astype(BF16), wu_bf[...], preferred_element_type=F32) + bu_ref[0]
        glu = jnp.minimum(up[:, :de], SWIGLU_LIMIT)
        lin = jnp.clip(up[:, de:], -SWIGLU_LIMIT, SWIGLU_LIMIT)
        act = glu * jax.nn.sigmoid(SWIGLU_ALPHA * glu) * (lin + 1.0)
        y = jnp.dot(act.astype(BF16), wd_bf[...], preferred_element_type=F32) + bd_ref[0]
        y_ref[h * EXPERT_HALF:(h + 1) * EXPERT_HALF, :] = _pack_halves(y)

    @pl.when(valid == 0)
    def _():
        y_ref[...] = jnp.zeros_like(y_ref)

    @pl.when(valid > 0)
    def _():
        prev = be_ref[jnp.maximum(j - 1, 0)]

        @pl.when(jnp.logical_or(j == 0, be_ref[j] != prev))
        def _():
            def cast(c, _):
                r0 = pl.multiple_of(c * cast_rows, cast_rows)
                wu_bf[pl.ds(r0, cast_rows), :] = wu_ref[0, pl.ds(r0, cast_rows), :].astype(BF16)
                wd_bf[pl.ds(r0, cast_rows), :] = wd_ref[0, pl.ds(r0, cast_rows), :].astype(BF16)
                return 0

            lax.fori_loop(0, wu_ref.shape[1] // cast_rows, cast, 0)

    @pl.when(valid > EXPERT_HALF)
    def _():
        half(0)
        half(1)

    @pl.when(jnp.logical_and(valid > 0, valid <= EXPERT_HALF))
    def _():
        half(0)
        y_ref[EXPERT_HALF:, :] = jnp.zeros((EXPERT_ROWS - EXPERT_HALF, y_ref.shape[1]), y_ref.dtype)


def _pack_halves(y):
    half = y.shape[1] // 2
    bits = lambda v: lax.bitcast_convert_type(v.astype(BF16).astype(F32), jnp.uint32)
    return (bits(y[:, half:]) & jnp.uint32(0xFFFF0000)) | (bits(y[:, :half]) >> 16)


def _unpack_halves(w):
    lo = lax.bitcast_convert_type(w << 16, F32).astype(BF16)
    hi = lax.bitcast_convert_type(w & jnp.uint32(0xFFFF0000), F32).astype(BF16)
    return lo, hi


def _experts(block_expert, block_src, block_valid, n_used, x_rows, w_up, b_up, w_down, b_down):
    n_blocks = block_expert.shape[0] + 1
    block_valid = jnp.concatenate([block_valid, jnp.zeros((1,), jnp.int32)])
    ne, d, up_cols = w_up.shape
    de = w_down.shape[1]
    blk = lambda j, be, bs, bv, nu: (bs[jnp.minimum(j, nu[0] - 1)], 0)
    wsel = lambda j, be, bs, bv, nu: (be[jnp.minimum(j, nu[0] - 1)], 0, 0)
    grid_spec = pltpu.PrefetchScalarGridSpec(
        num_scalar_prefetch=4,
        grid=(n_blocks,),
        in_specs=[
            pl.BlockSpec((EXPERT_ROWS * SUBLANES, LANES), blk),
            pl.BlockSpec((1, d, up_cols), wsel),
            pl.BlockSpec((1, 1, up_cols), wsel),
            pl.BlockSpec((1, de, d), wsel),
            pl.BlockSpec((1, 1, d), wsel),
        ],
        out_specs=pl.BlockSpec((EXPERT_ROWS, d // 2), lambda j, be, bs, bv, nu: (j, 0)),
        scratch_shapes=[pltpu.VMEM((d, up_cols), BF16), pltpu.VMEM((de, d), BF16)],
    )
    return pl.pallas_call(
        functools.partial(_expert_kernel, cast_rows=WEIGHT_CAST_ROWS),
        grid_spec=grid_spec,
        out_shape=jax.ShapeDtypeStruct((n_blocks * EXPERT_ROWS, d // 2), jnp.uint32),
        compiler_params=_params(1),
        name="experts",
    )(block_expert, block_src, block_valid, n_used, x_rows, w_up, b_up.reshape(ne, 1, up_cols),
      w_down, b_down.reshape(ne, 1, d))


def _combine_plan(run_start, run_cnt, seg_start, eid, rank, tm, max_pieces):
    first = run_start // SUBLANES * SUBLANES
    pieces = jnp.where(run_cnt > 0, (run_start - first + run_cnt + PIECE - 1) // PIECE, 0)
    piece_end = jnp.cumsum(pieces, axis=0)
    piece_off = piece_end - pieces
    p = jnp.arange(max_pieces, dtype=jnp.int32)
    expert_of = jnp.minimum(jnp.sum(p[None, :, None] >= piece_end.T[:, None, :], axis=-1), N_EXPERTS - 1)
    onehot = expert_of[:, :, None] == jnp.arange(N_EXPERTS, dtype=jnp.int32)
    src = jnp.sum(jnp.where(onehot, (first - piece_off * PIECE).T[:, None, :], 0), axis=-1) + p * PIECE
    shift = jnp.repeat(seg_start[:, None] + piece_off * PIECE - first, tm, axis=1)
    experts = jnp.arange(N_EXPERTS, dtype=jnp.int32)[:, None, None]
    staged = rank + jnp.sum(jnp.where(eid[None] == experts, shift[:, None, :], 0), axis=0)
    return src.reshape(-1).astype(jnp.int32), piece_end[-1].astype(jnp.int32), staged.T.astype(jnp.int32)


def _combine_kernel(src_ref, npieces_ref, staged_ref, gate_ref, xn_ref, rows_ref, o_ref, buf, sems,
                    *, tm, max_pieces):
    i = pl.program_id(0)
    n = pl.num_programs(0)
    slot = lax.rem(i, 2)
    half = o_ref.shape[1] // 2

    def fetch(tile, s):
        def body(p, _):
            src = rows_ref.at[pl.ds(pl.multiple_of(src_ref[tile * max_pieces + p], SUBLANES), PIECE)]
            dst = buf.at[s, pl.ds(pl.multiple_of(p * PIECE, PIECE), PIECE)]
            pltpu.make_async_copy(src, dst, sems.at[s]).start()
            return 0

        lax.fori_loop(0, npieces_ref[tile], body, 0)

    @pl.when(i == 0)
    def _():
        buf[...] = jnp.zeros_like(buf)
        fetch(0, 0)

    @pl.when(i + 1 < n)
    def _():
        fetch(i + 1, 1 - slot)

    total = npieces_ref[i] * PIECE
    staged = staged_ref[...]
    gates = gate_ref[...]

    @pl.when(total > 0)
    def _():
        rows = pl.ds(0, pl.multiple_of(total, PIECE))
        pltpu.make_async_copy(rows_ref.at[rows], buf.at[slot, rows], sems.at[slot]).wait()

    def chunk_products(c0):
        col = lax.broadcasted_iota(jnp.int32, (tm, COMBINE_CHUNK), 1) + c0
        sel = jnp.zeros((tm, COMBINE_CHUNK), F32)
        for k in range(TOP_K):
            sel = jnp.where(staged[:, k:k + 1] == col, gates[:, k:k + 1], sel)
        sel = sel.astype(BF16)
        lo, hi = _unpack_halves(buf[slot, pl.ds(c0, COMBINE_CHUNK), :])
        return jnp.dot(sel, lo, preferred_element_type=F32), jnp.dot(sel, hi, preferred_element_type=F32)

    xn = xn_ref[...]
    acc_lo, acc_hi = xn[:, :half], xn[:, half:]
    for c in range(COMBINE_STATIC_CHUNKS):
        d_lo, d_hi = chunk_products(c * COMBINE_CHUNK)
        acc_lo, acc_hi = acc_lo + d_lo, acc_hi + d_hi
    o_ref[:, :half] = acc_lo
    o_ref[:, half:] = acc_hi

    def chunk(c, _):
        d_lo, d_hi = chunk_products(pl.multiple_of(c * COMBINE_CHUNK, COMBINE_CHUNK))
        o_ref[:, :half] += d_lo
        o_ref[:, half:] += d_hi
        return 0

    lax.fori_loop(COMBINE_STATIC_CHUNKS, (total + COMBINE_CHUNK - 1) // COMBINE_CHUNK, chunk, 0)


def _combine(run_start, run_cnt, seg_start, eid, rank, gates_t, xn, y_rows, tm):
    t, d = xn.shape
    n = t // tm
    cap = -(-(TOP_K * tm + 2 * PIECE * N_EXPERTS) // COMBINE_CHUNK) * COMBINE_CHUNK
    max_pieces = cap // PIECE
    src, npieces, staged_t = _combine_plan(run_start, run_cnt, seg_start, eid, rank, tm, max_pieces)
    tok = lambda i, *_: (i, 0)
    grid_spec = pltpu.PrefetchScalarGridSpec(
        num_scalar_prefetch=2,
        grid=(n,),
        in_specs=[
            pl.BlockSpec((tm, TOP_K), tok),
            pl.BlockSpec((tm, TOP_K), tok),
            pl.BlockSpec((tm, d), tok),
            pl.BlockSpec(memory_space=pl.ANY),
        ],
        out_specs=pl.BlockSpec((tm, d), tok),
        scratch_shapes=[pltpu.VMEM((2, cap, d // 2), jnp.uint32), pltpu.SemaphoreType.DMA((2,))],
    )
    return pl.pallas_call(
        functools.partial(_combine_kernel, tm=tm, max_pieces=max_pieces),
        grid_spec=grid_spec,
        out_shape=jax.ShapeDtypeStruct((t, d), F32),
        compiler_params=_params(1),
        name="combine",
    )(src, npieces, staged_t, gates_t, xn, y_rows)


def _tile(n, want):
    t = min(n, want)
    assert n % t == 0, (n, t)
    return t


def kernel(x, meta_tokens, norm1_g, w_in, q_norm_g, k_norm_g, conv_w, conv_b, conv_ln_g, conv_ln_b,
           attn_out_g, w_out, norm2_g, w_router, b_router, w_up, b_up, w_down, b_down):
    assert norm1_g.shape[0] == 1, "single layer: meta-token rows are only keys/values and conv context"
    b, s, d = x.shape
    t = b * s
    cw = conv_w.shape[-1]

    g1 = norm1_g[0][None, :]
    w_in_bf = w_in[0].astype(BF16)
    qg = jnp.tile(q_norm_g[0], SB_HEADS)[None, :]
    kg = jnp.tile(k_norm_g[0], SB_HEADS)[None, :]

    q, k, v, hcv = _inproj(x.reshape(t, d), g1, w_in_bf, qg, kg, _tile(t, INPROJ_ROWS))
    _, km, vm, hm = _inproj(meta_tokens, g1, w_in_bf, qg, kg, N_META)
    pad = ((0, LANES - N_META), (0, 0))
    tq = _tile(s, ATTN_TILE)
    sb = _attention(q.reshape(b, s, SB_WIDTH), k.reshape(b, s, SB_WIDTH), v.reshape(b, s, SB_WIDTH),
                    jnp.pad(km, pad), jnp.pad(vm, pad), attn_out_g[0].reshape(1, SB_WIDTH), tq,
                    _tile(s, ATTN_TILES_PER_STEP * tq) // tq)

    mhalo = jnp.concatenate([jnp.zeros((HALO - N_META, cw), F32), hm], axis=0)
    tm = _tile(s, MIX_ROWS)
    wr_t = w_router[0].T
    wr_hi = wr_t.astype(BF16)
    wr_lo = (wr_t - wr_hi.astype(F32)).astype(BF16)
    cap = -(-t // EXPERT_ROWS) * EXPERT_ROWS
    xn, eid, gates, rank, before, counts, x_rows = _mix(
        hcv.reshape(b, s, cw), mhalo, sb, x, conv_w[0], conv_b[0][None, :], conv_ln_g[0][None, :],
        conv_ln_b[0][None, :], w_out[0].astype(BF16), norm2_g[0][None, :],
        jnp.concatenate([wr_hi, wr_lo], axis=0), b_router[0][:, None], tm, _tile(tm, CONV_CHUNK), cap)

    counts = counts[:, 0].astype(jnp.int32)
    padded = (counts + EXPERT_ROWS - 1) // EXPERT_ROWS * EXPERT_ROWS
    pad_end = jnp.cumsum(padded)
    pad_start = pad_end - padded
    n_blocks = t * TOP_K // EXPERT_ROWS + N_EXPERTS
    block_row = jnp.arange(n_blocks, dtype=jnp.int32) * EXPERT_ROWS
    block_expert = jnp.minimum(jnp.sum(block_row[:, None] >= pad_end[None, :], axis=1), N_EXPERTS - 1).astype(jnp.int32)
    block_off = block_row - pad_start[block_expert]
    block_src = (block_expert * cap + jnp.maximum(block_off, 0)) // EXPERT_ROWS
    block_valid = jnp.clip(counts[block_expert] - block_off, 0, EXPERT_ROWS)
    block_valid = jnp.where(block_row < pad_end[-1], block_valid, 0)
    n_used = (pad_end[-1:] // EXPERT_ROWS).astype(jnp.int32)
    seg = jnp.arange(N_EXPERTS, dtype=jnp.int32) * cap
    x_rows = _padfill(seg + counts, (-counts) % EXPERT_HALF, x_rows)
    y_rows = _experts(block_expert, block_src.astype(jnp.int32), block_valid.astype(jnp.int32), n_used, x_rows,
                      w_up[0], b_up[0], w_down[0], b_down[0])
    tc = _tile(t, COMBINE_ROWS)
    run_start = pad_start[:, None] + before[:, ::tc]
    run_end = jnp.concatenate([run_start[:, 1:], (pad_start + counts)[:, None]], axis=1)
    out = _combine(run_start, run_end - run_start, pad_start, eid, rank, gates.T, xn.reshape(t, d), y_rows, tc)
    return out.reshape(b, s, d)
```

```python
import functools

import jax
import jax.numpy as jnp
from jax import lax
from jax.experimental import pallas as pl
from jax.experimental.pallas import tpu as pltpu

N_META = 16
SB_HEADS = 8
SB_HEAD_DIM = 64
SB_WIDTH = SB_HEADS * SB_HEAD_DIM
CONV_KERNEL = 31
N_EXPERTS = 32
TOP_K = 4
SWIGLU_LIMIT = 7.0
SWIGLU_ALPHA = 1.702
EPS = 1e-6
F32_EXP2_UNDERFLOW = -150.0
LOG2_E = 1.4426950408889634

LANES = 128
SUBLANES = 8
INPROJ_ROWS = 1024
ATTN_TILE = 256
ATTN_TILES_PER_STEP = 8
MIX_ROWS = 1024
CONV_CHUNK = 64
COMBINE_ROWS = 256
WEIGHT_CAST_ROWS = 64
PIECE = 16
COMBINE_CHUNK = 512
COMBINE_STATIC_CHUNKS = 3
HALO = 32
EXPERT_HALF = 512
EXPERT_ROWS = 2 * EXPERT_HALF
VMEM_LIMIT = 56 * 1024 * 1024

F32 = jnp.float32
BF16 = jnp.bfloat16


def _params(n_axes, vmem=VMEM_LIMIT):
    return pltpu.CompilerParams(dimension_semantics=("arbitrary",) * n_axes, vmem_limit_bytes=vmem)


def _inproj_kernel(x_ref, g1_ref, w_ref, qg_ref, kg_ref, q_ref, k_ref, v_ref, h_ref):
    x = x_ref[...]
    ms = jnp.mean(x * x, axis=-1, keepdims=True)
    n = (x * lax.rsqrt(ms + EPS) * g1_ref[...]).astype(BF16)
    lo = lax.broadcasted_iota(jnp.int32, (1, LANES), 1) < SB_HEAD_DIM

    def proj(c0, c1):
        return jnp.dot(n, w_ref[:, c0:c1], preferred_element_type=F32)

    def head_norm(acc, g_ref, out_ref, scale):
        for c in range(SB_WIDTH // LANES):
            sl = slice(c * LANES, (c + 1) * LANES)
            a = acc[:, sl]
            sq = a * a
            s_lo = jnp.sum(jnp.where(lo, sq, 0.0), axis=-1, keepdims=True)
            s_hi = jnp.sum(jnp.where(lo, 0.0, sq), axis=-1, keepdims=True)
            r = lax.rsqrt(jnp.where(lo, s_lo, s_hi) * (1.0 / SB_HEAD_DIM) + EPS)
            out_ref[:, sl] = (a * r * (g_ref[:, sl] * scale)).astype(BF16)

    head_norm(proj(0, SB_WIDTH), qg_ref, q_ref, LOG2_E * SB_HEAD_DIM ** -0.5)
    head_norm(proj(SB_WIDTH, 2 * SB_WIDTH), kg_ref, k_ref, 1.0)
    v_ref[...] = proj(2 * SB_WIDTH, 3 * SB_WIDTH).astype(BF16)
    cw = (w_ref.shape[1] - 3 * SB_WIDTH) // 2
    val = proj(3 * SB_WIDTH, 3 * SB_WIDTH + cw)
    gate = proj(3 * SB_WIDTH + cw, 3 * SB_WIDTH + 2 * cw)
    h_ref[...] = val * jax.nn.sigmoid(gate)


def _inproj(x2, g1, w_in_bf, qg, kg, tm):
    t, d = x2.shape
    cols = w_in_bf.shape[1]
    cw = (cols - 3 * SB_WIDTH) // 2
    row = lambda i: (i, 0)
    fixed = lambda i: (0, 0)
    return pl.pallas_call(
        _inproj_kernel,
        grid=(t // tm,),
        in_specs=[
            pl.BlockSpec((tm, d), row),
            pl.BlockSpec((1, d), fixed),
            pl.BlockSpec((d, cols), fixed),
            pl.BlockSpec((1, SB_WIDTH), fixed),
            pl.BlockSpec((1, SB_WIDTH), fixed),
        ],
        out_specs=[
            pl.BlockSpec((tm, SB_WIDTH), row),
            pl.BlockSpec((tm, SB_WIDTH), row),
            pl.BlockSpec((tm, SB_WIDTH), row),
            pl.BlockSpec((tm, cw), row),
        ],
        out_shape=[
            jax.ShapeDtypeStruct((t, SB_WIDTH), BF16),
            jax.ShapeDtypeStruct((t, SB_WIDTH), BF16),
            jax.ShapeDtypeStruct((t, SB_WIDTH), BF16),
            jax.ShapeDtypeStruct((t, cw), F32),
        ],
        compiler_params=_params(1),
        name="inproj",
    )(x2, g1, w_in_bf, qg, kg)


def _attn_kernel(q_ref, k_ref, v_ref, km_ref, vm_ref, g_ref, o_ref, *, tq, nsub):
    i = pl.program_id(2)
    lane = lax.broadcasted_iota(jnp.int32, (1, LANES), 1)
    lo = lane < SB_HEAD_DIM
    zero_bf = jnp.zeros((), BF16)

    def stacked_q(sub):
        q2 = q_ref[0, sub * tq:(sub + 1) * tq, :]
        return jnp.concatenate([jnp.where(lo, q2, zero_bf), jnp.where(lo, zero_bf, q2)], axis=0)

    qss = [stacked_q(sub) for sub in range(nsub)]
    row = lax.broadcasted_iota(jnp.int32, (2 * tq, tq), 0)
    col = lax.broadcasted_iota(jnp.int32, (2 * tq, tq), 1)
    causal = col < jnp.where(row >= tq, row - tq, row)
    srow = lax.broadcasted_iota(jnp.int32, (tq, tq), 0)
    scol = lax.broadcasted_iota(jnp.int32, (tq, tq), 1)
    neg_suffix = jnp.where(srow > scol, -1.0, 0.0).astype(BF16)

    def block(qs, kb, vb, acc, r, mask, neg_suffix_m):
        z = lax.dot_general(qs, kb, (((1,), (1,)), ((), ())), preferred_element_type=F32)
        m = jnp.minimum(z, 0.0)
        p = jnp.maximum(z, 0.0)
        l1p = jnp.log2(1.0 + jnp.exp2(m - p))
        log_beta = m - l1p
        neg_keep = p + l1p
        if mask is not None:
            neg_keep = jnp.where(mask, neg_keep, 0.0)
        later = jnp.dot(neg_keep.astype(BF16), neg_suffix_m, preferred_element_type=F32)
        a = jnp.exp2(log_beta + later + r)
        if mask is not None:
            a = jnp.where(mask, a, 0.0)
        res = jnp.dot(a.astype(BF16), vb, preferred_element_type=F32)
        acc = acc + jnp.where(lo, res[:tq], res[tq:])
        return acc, r - jnp.sum(neg_keep, axis=-1, keepdims=True)

    def kv_block(j):
        start = pl.multiple_of(j * tq, tq)
        return k_ref[0, pl.ds(start, tq), :], v_ref[0, pl.ds(start, tq), :]

    def live(r):
        return jnp.max(r) > F32_EXP2_UNDERFLOW

    acc0 = jnp.zeros((tq, LANES), F32)
    r0 = jnp.zeros((2 * tq, 1), F32)
    first = i * nsub

    def guaranteed(first_has_previous):
        state = []
        for sub in range(nsub):
            acc, r = block(qss[sub], *kv_block(first + sub), acc0, r0, causal, neg_suffix)
            if sub > 0 or first_has_previous:
                acc, r = block(qss[sub], *kv_block(first + sub - 1), acc, r, None, neg_suffix)
            state += [acc, r]
        return tuple(state)

    state = lax.cond(i > 0, lambda _: guaranteed(True), lambda _: guaranteed(False), 0)

    for sub in range(nsub):
        j = first + sub
        qs, acc, r = qss[sub], state[2 * sub], state[2 * sub + 1]

        def cond(c):
            return jnp.logical_and(c[0] <= j, c[3])

        def body(c):
            acc, r = block(qs, *kv_block(j - c[0]), c[1], c[2], None, neg_suffix)
            return c[0] + 1, acc, r, live(r)

        _, acc, r, alive = lax.while_loop(cond, body, (jnp.int32(2), acc, r, live(r)))

        def meta_block(acc):
            return block(qs, km_ref[...], vm_ref[...], acc, r, lane < N_META, neg_suffix[:LANES, :LANES])[0]

        acc = lax.cond(alive, meta_block, lambda acc: acc, acc)

        sq = acc * acc
        s_lo = jnp.sum(jnp.where(lo, sq, 0.0), axis=-1, keepdims=True)
        s_hi = jnp.sum(jnp.where(lo, 0.0, sq), axis=-1, keepdims=True)
        rn = lax.rsqrt(jnp.where(lo, s_lo, s_hi) * (1.0 / SB_HEAD_DIM) + EPS)
        o_ref[0, sub * tq:(sub + 1) * tq, :] = (acc * rn * g_ref[...]).astype(BF16)


def _attention(q, k, v, km, vm, og, tq, nsub):
    b, s, _ = q.shape
    n_pairs = SB_WIDTH // LANES
    step = tq * nsub
    return pl.pallas_call(
        functools.partial(_attn_kernel, tq=tq, nsub=nsub),
        grid=(b, n_pairs, s // step),
        in_specs=[
            pl.BlockSpec((1, step, LANES), lambda b, p, i: (b, i, p)),
            pl.BlockSpec((1, s, LANES), lambda b, p, i: (b, 0, p)),
            pl.BlockSpec((1, s, LANES), lambda b, p, i: (b, 0, p)),
            pl.BlockSpec((LANES, LANES), lambda b, p, i: (0, p)),
            pl.BlockSpec((LANES, LANES), lambda b, p, i: (0, p)),
            pl.BlockSpec((1, LANES), lambda b, p, i: (0, p)),
        ],
        out_specs=pl.BlockSpec((1, step, LANES), lambda b, p, i: (b, i, p)),
        out_shape=jax.ShapeDtypeStruct((b, s, SB_WIDTH), BF16),
        compiler_params=_params(3),
        name="attention",
    )(q, k, v, km, vm, og)


def _mix_kernel(h_ref, halo_ref, mhalo_ref, sb_ref, x_ref, cw_ref, cb_ref, lg_ref, lb_ref, wo_ref,
                g2_ref, wr_ref, br_ref,
                xn_ref, eid_ref, gate_ref, rank_ref, before_ref, cnt_ref, rows_ref,
                win_ref, conv_ref, cv_ref, carry_ref, hn3_ref, posv_ref, pos_ref, dsem, psem, *, tm, chunk, cap):
    b = pl.program_id(0)
    i = pl.program_id(1)
    step = b * pl.num_programs(1) + i

    def pos_copy():
        return pltpu.make_async_copy(posv_ref, pos_ref, psem)

    def row_copy(t, k):
        return pltpu.make_async_copy(_row_tiles(hn3_ref, t), _row_tiles(rows_ref, pos_ref[0, k * tm + t]), dsem)

    @pl.when(step == 0)
    def _():
        carry_ref[...] = jnp.zeros_like(carry_ref)
        hn3_ref[...] = jnp.zeros_like(hn3_ref)
        posv_ref[...] = N_EXPERTS * cap + lax.broadcasted_iota(jnp.int32, posv_ref.shape, 1)
        pos_copy().start()

    pos_copy().wait()

    @pl.when(i == 0)
    def _():
        win_ref[0:HALO, :] = mhalo_ref[...]

    @pl.when(i > 0)
    def _():
        win_ref[0:HALO, :] = halo_ref[0]

    win_ref[HALO:, :] = h_ref[0]

    first_tap = HALO - (CONV_KERNEL - 1)

    def conv_chunk(c, _):
        r0 = pl.multiple_of(c * chunk, chunk)
        rows = chunk + HALO
        for lt in range(cb_ref.shape[1] // LANES):
            sl = slice(lt * LANES, (lt + 1) * LANES)
            window = win_ref[pl.ds(r0, rows), sl]
            acc = jnp.broadcast_to(cb_ref[:, sl], (chunk, LANES))
            for b in range(SUBLANES):
                shifted = window if b == 0 else pltpu.roll(window, shift=rows - b, axis=0)
                for a in range(HALO // SUBLANES + 1):
                    j = SUBLANES * a + b - first_tap
                    if 0 <= j < CONV_KERNEL:
                        acc = acc + cw_ref[j:j + 1, sl] * shifted[SUBLANES * a:SUBLANES * a + chunk, :]
            conv_ref[:, sl] = acc
        for tt in range(chunk):
            for k in range(TOP_K):
                row_copy(r0 + tt, k).start(priority=k % 2)
        acc = conv_ref[...]
        mu = jnp.mean(acc, axis=-1, keepdims=True)
        cen = acc - mu
        var = jnp.mean(cen * cen, axis=-1, keepdims=True)
        y = cen * lax.rsqrt(var + EPS) * lg_ref[...] + lb_ref[...]
        cv_ref[pl.ds(r0, chunk), :] = (y * jax.nn.sigmoid(y)).astype(BF16)
        return 0

    lax.fori_loop(0, tm // chunk, conv_chunk, 0)

    mixed = jnp.dot(sb_ref[0], wo_ref[0:SB_WIDTH, :], preferred_element_type=F32)
    mixed = mixed + jnp.dot(cv_ref[...], wo_ref[SB_WIDTH:, :], preferred_element_type=F32)
    xn = x_ref[0] + mixed
    xn_ref[0] = xn
    ms = jnp.mean(xn * xn, axis=-1, keepdims=True)
    hn = xn * lax.rsqrt(ms + EPS) * g2_ref[...]

    nt = (((1,), (1,)), ((), ()))
    hn_hi = hn.astype(BF16)
    hn_lo = (hn - hn_hi.astype(F32)).astype(BF16)
    both = lax.dot_general(wr_ref[...], hn_hi, nt, preferred_element_type=F32)
    logits = (both[:N_EXPERTS] + both[N_EXPERTS:]
              + lax.dot_general(wr_ref[0:N_EXPERTS, :], hn_lo, nt, preferred_element_type=F32) + br_ref[...])
    eidx = lax.broadcasted_iota(jnp.int32, logits.shape, 0)
    vals, idxs = [], []
    for _ in range(TOP_K):
        m = jnp.max(logits, axis=0, keepdims=True)
        sel = jnp.min(jnp.where(logits == m, eidx, N_EXPERTS), axis=0, keepdims=True)
        vals.append(m)
        idxs.append(sel)
        logits = jnp.where(eidx == sel, -jnp.inf, logits)
    exps = [jnp.exp(v - vals[0]) for v in vals]
    denom = exps[0] + exps[1] + exps[2] + exps[3]
    gate_ref[...] = jnp.concatenate([e / denom for e in exps], axis=0)
    eid_ref[...] = jnp.concatenate(idxs, axis=0)

    onehots = [(eidx == s).astype(F32) for s in idxs]
    chosen = onehots[0] + onehots[1] + onehots[2] + onehots[3]
    tr = lax.broadcasted_iota(jnp.int32, (tm, tm), 0)
    tc = lax.broadcasted_iota(jnp.int32, (tm, tm), 1)
    before = (tr < tc).astype(BF16)
    prefix = jnp.dot(chosen.astype(BF16), before, preferred_element_type=F32) + carry_ref[...]
    ranks = [jnp.sum(o * prefix, axis=0, keepdims=True).astype(jnp.int32) for o in onehots]
    rank_ref[...] = jnp.concatenate(ranks, axis=0)
    before_ref[...] = prefix.astype(jnp.int32)
    carry_ref[...] = carry_ref[...] + jnp.sum(chosen, axis=1, keepdims=True)
    cnt_ref[...] = jnp.broadcast_to(carry_ref[...], cnt_ref.shape)

    _wait_rows(rows_ref, TOP_K * tm, dsem)
    groups = hn.shape[1] // LANES
    for c in range(groups):
        hn3_ref[pl.ds(c, tm, stride=groups), :] = hn[:, c * LANES:(c + 1) * LANES]
    for k in range(TOP_K):
        posv_ref[:, k * tm:(k + 1) * tm] = idxs[k] * cap + ranks[k]
    pos_copy().start()

    @pl.when(step == pl.num_programs(0) * pl.num_programs(1) - 1)
    def _():
        pos_copy().wait()

        def issue(t, _):
            for k in range(TOP_K):
                row_copy(t, k).start(priority=k % 2)
            return 0

        lax.fori_loop(0, tm, issue, 0, unroll=2)
        _wait_rows(rows_ref, TOP_K * tm, dsem)


def _mix(hcv, mhalo, sb, x, conv_w, conv_b, ln_g, ln_b, w_out_bf, g2, wr_t, br, tm, chunk, cap):
    b, s, d = x.shape
    cw = hcv.shape[-1]
    t = b * s
    per = s // tm
    assert d == SUBLANES * LANES, d
    tile = lambda b, i: (b, i, 0)
    fixed = lambda b, i: (0, 0)
    tok = lambda b, i: (0, b * per + i)
    x_rows = N_EXPERTS * cap + TOP_K * tm
    return pl.pallas_call(
        functools.partial(_mix_kernel, tm=tm, chunk=chunk, cap=cap),
        grid=(b, per),
        in_specs=[
            pl.BlockSpec((1, tm, cw), tile),
            pl.BlockSpec((1, HALO, cw), lambda b, i: (b, jnp.maximum(i * (tm // HALO) - 1, 0), 0)),
            pl.BlockSpec((HALO, cw), fixed),
            pl.BlockSpec((1, tm, SB_WIDTH), tile),
            pl.BlockSpec((1, tm, d), tile),
            pl.BlockSpec((CONV_KERNEL, cw), fixed),
            pl.BlockSpec((1, cw), fixed),
            pl.BlockSpec((1, cw), fixed),
            pl.BlockSpec((1, cw), fixed),
            pl.BlockSpec((SB_WIDTH + cw, d), fixed),
            pl.BlockSpec((1, d), fixed),
            pl.BlockSpec((2 * N_EXPERTS, d), fixed),
            pl.BlockSpec((N_EXPERTS, 1), fixed),
        ],
        out_specs=[
            pl.BlockSpec((1, tm, d), tile),
            pl.BlockSpec((TOP_K, tm), tok),
            pl.BlockSpec((TOP_K, tm), tok),
            pl.BlockSpec((TOP_K, tm), tok),
            pl.BlockSpec((N_EXPERTS, tm), tok),
            pl.BlockSpec((N_EXPERTS, LANES), fixed),
            pl.BlockSpec(memory_space=pl.ANY),
        ],
        out_shape=[
            jax.ShapeDtypeStruct((b, s, d), F32),
            jax.ShapeDtypeStruct((TOP_K, t), jnp.int32),
            jax.ShapeDtypeStruct((TOP_K, t), F32),
            jax.ShapeDtypeStruct((TOP_K, t), jnp.int32),
            jax.ShapeDtypeStruct((N_EXPERTS, t), jnp.int32),
            jax.ShapeDtypeStruct((N_EXPERTS, LANES), F32),
            jax.ShapeDtypeStruct((x_rows * SUBLANES, LANES), F32),
        ],
        scratch_shapes=[
            pltpu.VMEM((tm + HALO, cw), F32),
            pltpu.VMEM((chunk, cw), F32),
            pltpu.VMEM((tm, cw), BF16),
            pltpu.VMEM((N_EXPERTS, 1), F32),
            pltpu.VMEM((tm * SUBLANES, LANES), F32),
            pltpu.VMEM((1, TOP_K * tm), jnp.int32),
            pltpu.SMEM((1, TOP_K * tm), jnp.int32),
            pltpu.SemaphoreType.DMA(()),
            pltpu.SemaphoreType.DMA(()),
        ],
        compiler_params=_params(2),
        name="mix_router",
    )(hcv, hcv, mhalo, sb, x, conv_w, conv_b, ln_g, ln_b, w_out_bf, g2, wr_t, br)


def _row_tiles(ref, row, n=1):
    size = n * SUBLANES if isinstance(n, int) else pl.multiple_of(n * SUBLANES, SUBLANES)
    return ref.at[pl.ds(pl.multiple_of(row * SUBLANES, SUBLANES), size)]


def _wait_rows(rows_ref, n, sem):
    pltpu.make_async_copy(_row_tiles(rows_ref, 0, n), _row_tiles(rows_ref, 0, n), sem).wait()


def _padfill_kernel(padlo_ref, padlen_ref, rows_in_ref, rows_ref, zeros_ref, zsem):
    del rows_in_ref
    zeros_ref[...] = jnp.zeros_like(zeros_ref)
    total = jnp.int32(0)
    for e in range(N_EXPERTS):
        lo, ln = padlo_ref[e], padlen_ref[e]
        total = total + ln
        for bit in range(EXPERT_HALF.bit_length() - 1):
            size = 1 << bit

            @pl.when((ln >> bit) & 1 == 1)
            def _():
                start = lo + ((ln >> (bit + 1)) << (bit + 1))
                pltpu.make_async_copy(_row_tiles(zeros_ref, 0, size), _row_tiles(rows_ref, start, size),
                                      zsem).start()

    @pl.when(total > 0)
    def _():
        _wait_rows(rows_ref, total, zsem)


def _padfill(pad_lo, pad_len, x_rows):
    smem = pl.BlockSpec(memory_space=pltpu.SMEM)
    return pl.pallas_call(
        _padfill_kernel,
        in_specs=[smem, smem, pl.BlockSpec(memory_space=pl.ANY)],
        out_specs=pl.BlockSpec(memory_space=pl.ANY),
        out_shape=jax.ShapeDtypeStruct(x_rows.shape, x_rows.dtype),
        scratch_shapes=[pltpu.VMEM((EXPERT_HALF * SUBLANES // 2, LANES), x_rows.dtype), pltpu.SemaphoreType.DMA(())],
        input_output_aliases={2: 0},
        compiler_params=pltpu.CompilerParams(vmem_limit_bytes=VMEM_LIMIT),
        name="padfill",
    )(pad_lo, pad_len, x_rows)


def _expert_kernel(be_ref, bsrc_ref, valid_ref, nused_ref, x_ref, wu_ref, bu_ref, wd_ref, bd_ref, y_ref,
                   wu_bf, wd_bf, *, cast_rows):
    del bsrc_ref, nused_ref
    j = pl.program_id(0)
    de = wd_ref.shape[1]
    valid = valid_ref[j]

    def half(h):
        x = jnp.concatenate([x_ref[pl.ds(h * EXPERT_HALF * SUBLANES + c, EXPERT_HALF, stride=SUBLANES), :]
                             for c in range(SUBLANES)], axis=1)
        up = jnp.dot(x.astype(BF16), wu_bf[...], preferred_element_type=F32) + bu_ref[0]
        glu = jnp.minimum(up[:, :de], SWIGLU_LIMIT)
        lin = jnp.clip(up[:, de:], -SWIGLU_LIMIT, SWIGLU_LIMIT)
        act = glu * jax.nn.sigmoid(SWIGLU_ALPHA * glu) * (lin + 1.0)
        y = jnp.dot(act.astype(BF16), wd_bf[...], preferred_element_type=F32) + bd_ref[0]
        y_ref[h * EXPERT_HALF:(h + 1) * EXPERT_HALF, :] = _pack_halves(y)

    @pl.when(valid == 0)
    def _():
        y_ref[...] = jnp.zeros_like(y_ref)

    @pl.when(valid > 0)
    def _():
        prev = be_ref[jnp.maximum(j - 1, 0)]

        @pl.when(jnp.logical_or(j == 0, be_ref[j] != prev))
        def _():
            def cast(c, _):
                r0 = pl.multiple_of(c * cast_rows, cast_rows)
                wu_bf[pl.ds(r0, cast_rows), :] = wu_ref[0, pl.ds(r0, cast_rows), :].astype(BF16)
                wd_bf[pl.ds(r0, cast_rows), :] = wd_ref[0, pl.ds(r0, cast_rows), :].astype(BF16)
                return 0

            lax.fori_loop(0, wu_ref.shape[1] // cast_rows, cast, 0)

    @pl.when(valid > EXPERT_HALF)
    def _():
        half(0)
        half(1)

    @pl.when(jnp.logical_and(valid > 0, valid <= EXPERT_HALF))
    def _():
        half(0)
        y_ref[EXPERT_HALF:, :] = jnp.zeros((EXPERT_ROWS - EXPERT_HALF, y_ref.shape[1]), y_ref.dtype)


def _pack_halves(y):
    half = y.shape[1] // 2
    bits = lambda v: lax.bitcast_convert_type(v.astype(BF16).astype(F32), jnp.uint32)
    return (bits(y[:, half:]) & jnp.uint32(0xFFFF0000)) | (bits(y[:, :half]) >> 16)


def _unpack_halves(w):
    lo = lax.bitcast_convert_type(w << 16, F32).astype(BF16)
    hi = lax.bitcast_convert_type(w & jnp.uint32(0xFFFF0000), F32).astype(BF16)
    return lo, hi


def _experts(block_expert, block_src, block_valid, n_used, x_rows, w_up, b_up, w_down, b_down):
    n_blocks = block_expert.shape[0] + 1
    block_valid = jnp.concatenate([block_valid, jnp.zeros((1,), jnp.int32)])
    ne, d, up_cols = w_up.shape
    de = w_down.shape[1]
    blk = lambda j, be, bs, bv, nu: (bs[jnp.minimum(j, nu[0] - 1)], 0)
    wsel = lambda j, be, bs, bv, nu: (be[jnp.minimum(j, nu[0] - 1)], 0, 0)
    grid_spec = pltpu.PrefetchScalarGridSpec(
        num_scalar_prefetch=4,
        grid=(n_blocks,),
        in_specs=[
            pl.BlockSpec((EXPERT_ROWS * SUBLANES, LANES), blk),
            pl.BlockSpec((1, d, up_cols), wsel),
            pl.BlockSpec((1, 1, up_cols), wsel),
            pl.BlockSpec((1, de, d), wsel),
            pl.BlockSpec((1, 1, d), wsel),
        ],
        out_specs=pl.BlockSpec((EXPERT_ROWS, d // 2), lambda j, be, bs, bv, nu: (j, 0)),
        scratch_shapes=[pltpu.VMEM((d, up_cols), BF16), pltpu.VMEM((de, d), BF16)],
    )
    return pl.pallas_call(
        functools.partial(_expert_kernel, cast_rows=WEIGHT_CAST_ROWS),
        grid_spec=grid_spec,
        out_shape=jax.ShapeDtypeStruct((n_blocks * EXPERT_ROWS, d // 2), jnp.uint32),
        compiler_params=_params(1),
        name="experts",
    )(block_expert, block_src, block_valid, n_used, x_rows, w_up, b_up.reshape(ne, 1, up_cols),
      w_down, b_down.reshape(ne, 1, d))


def _combine_plan(run_start, run_cnt, seg_start, max_pieces):
    first = run_start // SUBLANES * SUBLANES
    pieces = jnp.where(run_cnt > 0, (run_start - first + run_cnt + PIECE - 1) // PIECE, 0)
    piece_end = jnp.cumsum(pieces, axis=0)
    piece_off = piece_end - pieces
    p = jnp.arange(max_pieces, dtype=jnp.int32)
    expert_of = jnp.minimum(jnp.sum(p[None, :, None] >= piece_end.T[:, None, :], axis=-1), N_EXPERTS - 1)
    onehot = expert_of[:, :, None] == jnp.arange(N_EXPERTS, dtype=jnp.int32)
    src = jnp.sum(jnp.where(onehot, (first - piece_off * PIECE).T[:, None, :], 0), axis=-1) + p * PIECE
    shift = (seg_start[:, None] + piece_off * PIECE - first).T
    return (src.reshape(-1).astype(jnp.int32), piece_end[-1].astype(jnp.int32),
            shift.reshape(-1).astype(jnp.int32))


def _to_columns(x):
    pad = jnp.zeros((LANES - x.shape[0], x.shape[1]), x.dtype)
    return jnp.concatenate([x, pad], axis=0).T


def _combine_kernel(src_ref, npieces_ref, shift_ref, eid_ref, rank_ref, gate_ref, xn_ref, rows_ref, o_ref, buf,
                    sems, *, tm, max_pieces):
    i = pl.program_id(0)
    n = pl.num_programs(0)
    slot = lax.rem(i, 2)
    half = o_ref.shape[1] // 2

    def fetch(tile, s):
        def body(p, _):
            src = rows_ref.at[pl.ds(pl.multiple_of(src_ref[tile * max_pieces + p], SUBLANES), PIECE)]
            dst = buf.at[s, pl.ds(pl.multiple_of(p * PIECE, PIECE), PIECE)]
            pltpu.make_async_copy(src, dst, sems.at[s]).start()
            return 0

        lax.fori_loop(0, npieces_ref[tile], body, 0)

    @pl.when(i == 0)
    def _():
        buf[...] = jnp.zeros_like(buf)
        fetch(0, 0)

    @pl.when(i + 1 < n)
    def _():
        fetch(i + 1, 1 - slot)

    total = npieces_ref[i] * PIECE
    eid = eid_ref[...]
    shift = jnp.zeros_like(eid)
    for e in range(N_EXPERTS):
        shift = jnp.where(eid == e, shift_ref[i * N_EXPERTS + e], shift)
    staged = _to_columns((rank_ref[...] + shift).astype(F32)).astype(jnp.int32)
    gates = _to_columns(gate_ref[...])

    @pl.when(total > 0)
    def _():
        rows = pl.ds(0, pl.multiple_of(total, PIECE))
        pltpu.make_async_copy(rows_ref.at[rows], buf.at[slot, rows], sems.at[slot]).wait()

    def chunk_products(c0):
        col = lax.broadcasted_iota(jnp.int32, (tm, COMBINE_CHUNK), 1) + c0
        sel = jnp.zeros((tm, COMBINE_CHUNK), F32)
        for k in range(TOP_K):
            sel = jnp.where(staged[:, k:k + 1] == col, gates[:, k:k + 1], sel)
        sel = sel.astype(BF16)
        lo, hi = _unpack_halves(buf[slot, pl.ds(c0, COMBINE_CHUNK), :])
        return jnp.dot(sel, lo, preferred_element_type=F32), jnp.dot(sel, hi, preferred_element_type=F32)

    xn = xn_ref[...]
    acc_lo, acc_hi = xn[:, :half], xn[:, half:]
    for c in range(COMBINE_STATIC_CHUNKS):
        d_lo, d_hi = chunk_products(c * COMBINE_CHUNK)
        acc_lo, acc_hi = acc_lo + d_lo, acc_hi + d_hi
    o_ref[:, :half] = acc_lo
    o_ref[:, half:] = acc_hi

    def chunk(c, _):
        d_lo, d_hi = chunk_products(pl.multiple_of(c * COMBINE_CHUNK, COMBINE_CHUNK))
        o_ref[:, :half] += d_lo
        o_ref[:, half:] += d_hi
        return 0

    lax.fori_loop(COMBINE_STATIC_CHUNKS, (total + COMBINE_CHUNK - 1) // COMBINE_CHUNK, chunk, 0)


def _combine(run_start, run_cnt, seg_start, eid, rank, gates, xn, y_rows, tm):
    t, d = xn.shape
    n = t // tm
    cap = -(-(TOP_K * tm + 2 * PIECE * N_EXPERTS) // COMBINE_CHUNK) * COMBINE_CHUNK
    max_pieces = cap // PIECE
    src, npieces, shift = _combine_plan(run_start, run_cnt, seg_start, max_pieces)
    tok = lambda i, *_: (i, 0)
    per_k = lambda i, *_: (0, i)
    grid_spec = pltpu.PrefetchScalarGridSpec(
        num_scalar_prefetch=3,
        grid=(n,),
        in_specs=[
            pl.BlockSpec((TOP_K, tm), per_k),
            pl.BlockSpec((TOP_K, tm), per_k),
            pl.BlockSpec((TOP_K, tm), per_k),
            pl.BlockSpec((tm, d), tok),
            pl.BlockSpec(memory_space=pl.ANY),
        ],
        out_specs=pl.BlockSpec((tm, d), tok),
        scratch_shapes=[pltpu.VMEM((2, cap, d // 2), jnp.uint32), pltpu.SemaphoreType.DMA((2,))],
    )
    return pl.pallas_call(
        functools.partial(_combine_kernel, tm=tm, max_pieces=max_pieces),
        grid_spec=grid_spec,
        out_shape=jax.ShapeDtypeStruct((t, d), F32),
        compiler_params=_params(1),
        name="combine",
    )(src, npieces, shift, eid, rank, gates, xn, y_rows)


def _tile(n, want):
    t = min(n, want)
    assert n % t == 0, (n, t)
    return t


def kernel(x, meta_tokens, norm1_g, w_in, q_norm_g, k_norm_g, conv_w, conv_b, conv_ln_g, conv_ln_b,
           attn_out_g, w_out, norm2_g, w_router, b_router, w_up, b_up, w_down, b_down):
    assert norm1_g.shape[0] == 1, "single layer: meta-token rows are only keys/values and conv context"
    b, s, d = x.shape
    t = b * s
    cw = conv_w.shape[-1]

    g1 = norm1_g[0][None, :]
    w_in_bf = w_in[0].astype(BF16)
    qg = jnp.tile(q_norm_g[0], SB_HEADS)[None, :]
    kg = jnp.tile(k_norm_g[0], SB_HEADS)[None, :]

    q, k, v, hcv = _inproj(x.reshape(t, d), g1, w_in_bf, qg, kg, _tile(t, INPROJ_ROWS))
    _, km, vm, hm = _inproj(meta_tokens, g1, w_in_bf, qg, kg, N_META)
    pad = ((0, LANES - N_META), (0, 0))
    tq = _tile(s, ATTN_TILE)
    sb = _attention(q.reshape(b, s, SB_WIDTH), k.reshape(b, s, SB_WIDTH), v.reshape(b, s, SB_WIDTH),
                    jnp.pad(km, pad), jnp.pad(vm, pad), attn_out_g[0].reshape(1, SB_WIDTH), tq,
                    _tile(s, ATTN_TILES_PER_STEP * tq) // tq)

    mhalo = jnp.concatenate([jnp.zeros((HALO - N_META, cw), F32), hm], axis=0)
    tm = _tile(s, MIX_ROWS)
    wr_t = w_router[0].T
    wr_hi = wr_t.astype(BF16)
    wr_lo = (wr_t - wr_hi.astype(F32)).astype(BF16)
    cap = -(-t // EXPERT_ROWS) * EXPERT_ROWS
    xn, eid, gates, rank, before, counts, x_rows = _mix(
        hcv.reshape(b, s, cw), mhalo, sb, x, conv_w[0], conv_b[0][None, :], conv_ln_g[0][None, :],
        conv_ln_b[0][None, :], w_out[0].astype(BF16), norm2_g[0][None, :],
        jnp.concatenate([wr_hi, wr_lo], axis=0), b_router[0][:, None], tm, _tile(tm, CONV_CHUNK), cap)

    counts = counts[:, 0].astype(jnp.int32)
    padded = (counts + EXPERT_ROWS - 1) // EXPERT_ROWS * EXPERT_ROWS
    pad_end = jnp.cumsum(padded)
    pad_start = pad_end - padded
    n_blocks = t * TOP_K // EXPERT_ROWS + N_EXPERTS
    block_row = jnp.arange(n_blocks, dtype=jnp.int32) * EXPERT_ROWS
    block_expert = jnp.minimum(jnp.sum(block_row[:, None] >= pad_end[None, :], axis=1), N_EXPERTS - 1).astype(jnp.int32)
    block_off = block_row - pad_start[block_expert]
    block_src = (block_expert * cap + jnp.maximum(block_off, 0)) // EXPERT_ROWS
    block_valid = jnp.clip(counts[block_expert] - block_off, 0, EXPERT_ROWS)
    block_valid = jnp.where(block_row < pad_end[-1], block_valid, 0)
    n_used = (pad_end[-1:] // EXPERT_ROWS).astype(jnp.int32)
    seg = jnp.arange(N_EXPERTS, dtype=jnp.int32) * cap
    x_rows = _padfill(seg + counts, (-counts) % EXPERT_HALF, x_rows)
    y_rows = _experts(block_expert, block_src.astype(jnp.int32), block_valid.astype(jnp.int32), n_used, x_rows,
                      w_up[0], b_up[0], w_down[0], b_down[0])
    tc = _tile(t, COMBINE_ROWS)
    run_start = pad_start[:, None] + before[:, ::tc]
    run_end = jnp.concatenate([run_start[:, 1:], (pad_start + counts)[:, None]], axis=1)
    out = _combine(run_start, run_end - run_start, pad_start, eid, rank, gates, xn.reshape(t, d), y_rows, tc)
    return out.reshape(b, s, d)
```

```python
import functools

import jax
import jax.numpy as jnp
from jax import lax
from jax.experimental import pallas as pl
from jax.experimental.pallas import tpu as pltpu

N_META = 16
SB_HEADS = 8
SB_HEAD_DIM = 64
SB_WIDTH = SB_HEADS * SB_HEAD_DIM
CONV_KERNEL = 31
N_EXPERTS = 32
TOP_K = 4
SWIGLU_LIMIT = 7.0
SWIGLU_ALPHA = 1.702
EPS = 1e-6
F32_EXP2_UNDERFLOW = -150.0
LOG2_E = 1.4426950408889634

LANES = 128
SUBLANES = 8
INPROJ_ROWS = 1024
ATTN_TILE = 256
ATTN_TILES_PER_STEP = 8
MIX_ROWS = 1024
CONV_CHUNK = 64
COMBINE_ROWS = 256
WEIGHT_CAST_ROWS = 64
PIECE = 16
COMBINE_CHUNK = 512
COMBINE_STATIC_CHUNKS = 3
HALO = 32
EXPERT_HALF = 512
EXPERT_ROWS = 2 * EXPERT_HALF
VMEM_LIMIT = 56 * 1024 * 1024

F32 = jnp.float32
BF16 = jnp.bfloat16


def _params(n_axes, vmem=VMEM_LIMIT):
    return pltpu.CompilerParams(dimension_semantics=("arbitrary",) * n_axes, vmem_limit_bytes=vmem)


def _inproj_kernel(x_ref, g1_ref, w_ref, qg_ref, kg_ref, q_ref, k_ref, v_ref, h_ref):
    x = x_ref[...]
    ms = jnp.mean(x * x, axis=-1, keepdims=True)
    n = (x * lax.rsqrt(ms + EPS) * g1_ref[...]).astype(BF16)
    lo = lax.broadcasted_iota(jnp.int32, (1, LANES), 1) < SB_HEAD_DIM

    def proj(c0, c1):
        return jnp.dot(n, w_ref[:, c0:c1], preferred_element_type=F32)

    def head_norm(acc, g_ref, out_ref, scale):
        for c in range(SB_WIDTH // LANES):
            sl = slice(c * LANES, (c + 1) * LANES)
            a = acc[:, sl]
            sq = a * a
            s_lo = jnp.sum(jnp.where(lo, sq, 0.0), axis=-1, keepdims=True)
            s_hi = jnp.sum(jnp.where(lo, 0.0, sq), axis=-1, keepdims=True)
            r = lax.rsqrt(jnp.where(lo, s_lo, s_hi) * (1.0 / SB_HEAD_DIM) + EPS)
            out_ref[:, sl] = (a * r * (g_ref[:, sl] * scale)).astype(BF16)

    head_norm(proj(0, SB_WIDTH), qg_ref, q_ref, LOG2_E * SB_HEAD_DIM ** -0.5)
    head_norm(proj(SB_WIDTH, 2 * SB_WIDTH), kg_ref, k_ref, 1.0)
    v_ref[...] = proj(2 * SB_WIDTH, 3 * SB_WIDTH).astype(BF16)
    cw = (w_ref.shape[1] - 3 * SB_WIDTH) // 2
    val = proj(3 * SB_WIDTH, 3 * SB_WIDTH + cw)
    gate = proj(3 * SB_WIDTH + cw, 3 * SB_WIDTH + 2 * cw)
    h_ref[...] = val * jax.nn.sigmoid(gate)


def _inproj(x2, g1, w_in_bf, qg, kg, tm):
    t, d = x2.shape
    cols = w_in_bf.shape[1]
    cw = (cols - 3 * SB_WIDTH) // 2
    row = lambda i: (i, 0)
    fixed = lambda i: (0, 0)
    return pl.pallas_call(
        _inproj_kernel,
        grid=(t // tm,),
        in_specs=[
            pl.BlockSpec((tm, d), row),
            pl.BlockSpec((1, d), fixed),
            pl.BlockSpec((d, cols), fixed),
            pl.BlockSpec((1, SB_WIDTH), fixed),
            pl.BlockSpec((1, SB_WIDTH), fixed),
        ],
        out_specs=[
            pl.BlockSpec((tm, SB_WIDTH), row),
            pl.BlockSpec((tm, SB_WIDTH), row),
            pl.BlockSpec((tm, SB_WIDTH), row),
            pl.BlockSpec((tm, cw), row),
        ],
        out_shape=[
            jax.ShapeDtypeStruct((t, SB_WIDTH), BF16),
            jax.ShapeDtypeStruct((t, SB_WIDTH), BF16),
            jax.ShapeDtypeStruct((t, SB_WIDTH), BF16),
            jax.ShapeDtypeStruct((t, cw), F32),
        ],
        compiler_params=_params(1),
        name="inproj",
    )(x2, g1, w_in_bf, qg, kg)


def _attn_kernel(q_ref, k_ref, v_ref, km_ref, vm_ref, g_ref, o_ref, *, tq, nsub):
    i = pl.program_id(2)
    lane = lax.broadcasted_iota(jnp.int32, (1, LANES), 1)
    lo = lane < SB_HEAD_DIM
    zero_bf = jnp.zeros((), BF16)

    def stacked_q(sub):
        q2 = q_ref[0, sub * tq:(sub + 1) * tq, :]
        return jnp.concatenate([jnp.where(lo, q2, zero_bf), jnp.where(lo, zero_bf, q2)], axis=0)

    qss = [stacked_q(sub) for sub in range(nsub)]
    row = lax.broadcasted_iota(jnp.int32, (2 * tq, tq), 0)
    col = lax.broadcasted_iota(jnp.int32, (2 * tq, tq), 1)
    causal = col < jnp.where(row >= tq, row - tq, row)
    srow = lax.broadcasted_iota(jnp.int32, (tq, tq), 0)
    scol = lax.broadcasted_iota(jnp.int32, (tq, tq), 1)
    neg_suffix = jnp.where(srow > scol, -1.0, 0.0).astype(BF16)

    def block(qs, kb, vb, acc, r, mask, neg_suffix_m):
        z = lax.dot_general(qs, kb, (((1,), (1,)), ((), ())), preferred_element_type=F32)
        m = jnp.minimum(z, 0.0)
        p = jnp.maximum(z, 0.0)
        l1p = jnp.log2(1.0 + jnp.exp2(m - p))
        log_beta = m - l1p
        neg_keep = p + l1p
        if mask is not None:
            neg_keep = jnp.where(mask, neg_keep, 0.0)
        later = jnp.dot(neg_keep.astype(BF16), neg_suffix_m, preferred_element_type=F32)
        a = jnp.exp2(log_beta + later + r)
        if mask is not None:
            a = jnp.where(mask, a, 0.0)
        res = jnp.dot(a.astype(BF16), vb, preferred_element_type=F32)
        acc = acc + jnp.where(lo, res[:tq], res[tq:])
        return acc, r - jnp.sum(neg_keep, axis=-1, keepdims=True)

    def kv_block(j):
        start = pl.multiple_of(j * tq, tq)
        return k_ref[0, pl.ds(start, tq), :], v_ref[0, pl.ds(start, tq), :]

    def live(r):
        return jnp.max(r) > F32_EXP2_UNDERFLOW

    acc0 = jnp.zeros((tq, LANES), F32)
    r0 = jnp.zeros((2 * tq, 1), F32)
    first = i * nsub

    def guaranteed(first_has_previous):
        state = []
        for sub in range(nsub):
            acc, r = block(qss[sub], *kv_block(first + sub), acc0, r0, causal, neg_suffix)
            if sub > 0 or first_has_previous:
                acc, r = block(qss[sub], *kv_block(first + sub - 1), acc, r, None, neg_suffix)
            state += [acc, r]
        return tuple(state)

    state = lax.cond(i > 0, lambda _: guaranteed(True), lambda _: guaranteed(False), 0)

    for sub in range(nsub):
        j = first + sub
        qs, acc, r = qss[sub], state[2 * sub], state[2 * sub + 1]

        def cond(c):
            return jnp.logical_and(c[0] <= j, c[3])

        def body(c):
            acc, r = block(qs, *kv_block(j - c[0]), c[1], c[2], None, neg_suffix)
            return c[0] + 1, acc, r, live(r)

        _, acc, r, alive = lax.while_loop(cond, body, (jnp.int32(2), acc, r, live(r)))

        def meta_block(acc):
            return block(qs, km_ref[...], vm_ref[...], acc, r, lane < N_META, neg_suffix[:LANES, :LANES])[0]

        acc = lax.cond(alive, meta_block, lambda acc: acc, acc)

        sq = acc * acc
        s_lo = jnp.sum(jnp.where(lo, sq, 0.0), axis=-1, keepdims=True)
        s_hi = jnp.sum(jnp.where(lo, 0.0, sq), axis=-1, keepdims=True)
        rn = lax.rsqrt(jnp.where(lo, s_lo, s_hi) * (1.0 / SB_HEAD_DIM) + EPS)
        o_ref[0, sub * tq:(sub + 1) * tq, :] = (acc * rn * g_ref[...]).astype(BF16)


def _attention(q, k, v, km, vm, og, tq, nsub):
    b, s, _ = q.shape
    n_pairs = SB_WIDTH // LANES
    step = tq * nsub
    return pl.pallas_call(
        functools.partial(_attn_kernel, tq=tq, nsub=nsub),
        grid=(b, n_pairs, s // step),
        in_specs=[
            pl.BlockSpec((1, step, LANES), lambda b, p, i: (b, i, p)),
            pl.BlockSpec((1, s, LANES), lambda b, p, i: (b, 0, p)),
            pl.BlockSpec((1, s, LANES), lambda b, p, i: (b, 0, p)),
            pl.BlockSpec((LANES, LANES), lambda b, p, i: (0, p)),
            pl.BlockSpec((LANES, LANES), lambda b, p, i: (0, p)),
            pl.BlockSpec((1, LANES), lambda b, p, i: (0, p)),
        ],
        out_specs=pl.BlockSpec((1, step, LANES), lambda b, p, i: (b, i, p)),
        out_shape=jax.ShapeDtypeStruct((b, s, SB_WIDTH), BF16),
        compiler_params=_params(3),
        name="attention",
    )(q, k, v, km, vm, og)


def _mix_kernel(h_ref, halo_ref, mhalo_ref, sb_ref, x_ref, cw_ref, cb_ref, lg_ref, lb_ref, wo_ref,
                g2_ref, wr_ref, br_ref,
                xn_ref, eid_ref, gate_ref, rank_ref, before_ref, cnt_ref, rows_ref,
                win_ref, conv_ref, cv_ref, carry_ref, hn3_ref, posv_ref, pos_ref, dsem, psem, *, tm, chunk, cap):
    b = pl.program_id(0)
    i = pl.program_id(1)
    step = b * pl.num_programs(1) + i

    def pos_copy():
        return pltpu.make_async_copy(posv_ref, pos_ref, psem)

    def row_copy(t, k):
        return pltpu.make_async_copy(_row_tiles(hn3_ref, t), _row_tiles(rows_ref, pos_ref[0, k * tm + t]), dsem)

    @pl.when(step == 0)
    def _():
        carry_ref[...] = jnp.zeros_like(carry_ref)
        hn3_ref[...] = jnp.zeros_like(hn3_ref)
        posv_ref[...] = N_EXPERTS * cap + lax.broadcasted_iota(jnp.int32, posv_ref.shape, 1)
        pos_copy().start()

    pos_copy().wait()

    @pl.when(i == 0)
    def _():
        win_ref[0:HALO, :] = mhalo_ref[...]

    @pl.when(i > 0)
    def _():
        win_ref[0:HALO, :] = halo_ref[0]

    win_ref[HALO:, :] = h_ref[0]

    first_tap = HALO - (CONV_KERNEL - 1)

    def conv_chunk(c, _):
        r0 = pl.multiple_of(c * chunk, chunk)
        rows = chunk + HALO
        for lt in range(cb_ref.shape[1] // LANES):
            sl = slice(lt * LANES, (lt + 1) * LANES)
            window = win_ref[pl.ds(r0, rows), sl]
            acc = jnp.broadcast_to(cb_ref[:, sl], (chunk, LANES))
            for b in range(SUBLANES):
                shifted = window if b == 0 else pltpu.roll(window, shift=rows - b, axis=0)
                for a in range(HALO // SUBLANES + 1):
                    j = SUBLANES * a + b - first_tap
                    if 0 <= j < CONV_KERNEL:
                        acc = acc + cw_ref[j:j + 1, sl] * shifted[SUBLANES * a:SUBLANES * a + chunk, :]
            conv_ref[:, sl] = acc
        for tt in range(chunk):
            for k in range(TOP_K):
                row_copy(r0 + tt, k).start(priority=k % 2)
        acc = conv_ref[...]
        mu = jnp.mean(acc, axis=-1, keepdims=True)
        cen = acc - mu
        var = jnp.mean(cen * cen, axis=-1, keepdims=True)
        y = cen * lax.rsqrt(var + EPS) * lg_ref[...] + lb_ref[...]
        cv_ref[pl.ds(r0, chunk), :] = (y * jax.nn.sigmoid(y)).astype(BF16)
        return 0

    lax.fori_loop(0, tm // chunk, conv_chunk, 0)

    mixed = jnp.dot(sb_ref[0], wo_ref[0:SB_WIDTH, :], preferred_element_type=F32)
    mixed = mixed + jnp.dot(cv_ref[...], wo_ref[SB_WIDTH:, :], preferred_element_type=F32)
    xn = x_ref[0] + mixed
    xn_ref[0] = xn
    ms = jnp.mean(xn * xn, axis=-1, keepdims=True)
    hn = xn * lax.rsqrt(ms + EPS) * g2_ref[...]

    nt = (((1,), (1,)), ((), ()))
    hn_hi = hn.astype(BF16)
    hn_lo = (hn - hn_hi.astype(F32)).astype(BF16)
    both = lax.dot_general(wr_ref[...], hn_hi, nt, preferred_element_type=F32)
    logits = (both[:N_EXPERTS] + both[N_EXPERTS:]
              + lax.dot_general(wr_ref[0:N_EXPERTS, :], hn_lo, nt, preferred_element_type=F32) + br_ref[...])
    eidx = lax.broadcasted_iota(jnp.int32, logits.shape, 0)
    vals, idxs = [], []
    for _ in range(TOP_K):
        m = jnp.max(logits, axis=0, keepdims=True)
        sel = jnp.min(jnp.where(logits == m, eidx, N_EXPERTS), axis=0, keepdims=True)
        vals.append(m)
        idxs.append(sel)
        logits = jnp.where(eidx == sel, -jnp.inf, logits)
    exps = [jnp.exp(v - vals[0]) for v in vals]
    denom = exps[0] + exps[1] + exps[2] + exps[3]
    gate_ref[...] = jnp.concatenate([e / denom for e in exps], axis=0)
    eid_ref[...] = jnp.concatenate(idxs, axis=0)

    onehots = [(eidx == s).astype(F32) for s in idxs]
    chosen = onehots[0] + onehots[1] + onehots[2] + onehots[3]
    tr = lax.broadcasted_iota(jnp.int32, (tm, tm), 0)
    tc = lax.broadcasted_iota(jnp.int32, (tm, tm), 1)
    before = (tr < tc).astype(BF16)
    prefix = jnp.dot(chosen.astype(BF16), before, preferred_element_type=F32) + carry_ref[...]
    ranks = [jnp.sum(o * prefix, axis=0, keepdims=True).astype(jnp.int32) for o in onehots]
    rank_ref[...] = jnp.concatenate(ranks, axis=0)
    before_ref[...] = prefix.astype(jnp.int32)
    carry_ref[...] = carry_ref[...] + jnp.sum(chosen, axis=1, keepdims=True)
    cnt_ref[...] = jnp.broadcast_to(carry_ref[...], cnt_ref.shape)

    _wait_rows(rows_ref, TOP_K * tm, dsem)
    groups = hn.shape[1] // LANES
    for c in range(groups):
        hn3_ref[pl.ds(c, tm, stride=groups), :] = hn[:, c * LANES:(c + 1) * LANES]
    for k in range(TOP_K):
        posv_ref[:, k * tm:(k + 1) * tm] = idxs[k] * cap + ranks[k]
    pos_copy().start()

    @pl.when(step == pl.num_programs(0) * pl.num_programs(1) - 1)
    def _():
        pos_copy().wait()

        def issue(t, _):
            for k in range(TOP_K):
                row_copy(t, k).start(priority=k % 2)
            return 0

        lax.fori_loop(0, tm, issue, 0, unroll=2)
        _wait_rows(rows_ref, TOP_K * tm, dsem)


def _mix(hcv, mhalo, sb, x, conv_w, conv_b, ln_g, ln_b, w_out_bf, g2, wr_t, br, tm, chunk, cap):
    b, s, d = x.shape
    cw = hcv.shape[-1]
    t = b * s
    per = s // tm
    assert d == SUBLANES * LANES, d
    tile = lambda b, i: (b, i, 0)
    fixed = lambda b, i: (0, 0)
    tok = lambda b, i: (0, b * per + i)
    x_rows = N_EXPERTS * cap + TOP_K * tm
    return pl.pallas_call(
        functools.partial(_mix_kernel, tm=tm, chunk=chunk, cap=cap),
        grid=(b, per),
        in_specs=[
            pl.BlockSpec((1, tm, cw), tile),
            pl.BlockSpec((1, HALO, cw), lambda b, i: (b, jnp.maximum(i * (tm // HALO) - 1, 0), 0)),
            pl.BlockSpec((HALO, cw), fixed),
            pl.BlockSpec((1, tm, SB_WIDTH), tile),
            pl.BlockSpec((1, tm, d), tile),
            pl.BlockSpec((CONV_KERNEL, cw), fixed),
            pl.BlockSpec((1, cw), fixed),
            pl.BlockSpec((1, cw), fixed),
            pl.BlockSpec((1, cw), fixed),
            pl.BlockSpec((SB_WIDTH + cw, d), fixed),
            pl.BlockSpec((1, d), fixed),
            pl.BlockSpec((2 * N_EXPERTS, d), fixed),
            pl.BlockSpec((N_EXPERTS, 1), fixed),
        ],
        out_specs=[
            pl.BlockSpec((1, tm, d), tile),
            pl.BlockSpec((TOP_K, tm), tok),
            pl.BlockSpec((TOP_K, tm), tok),
            pl.BlockSpec((TOP_K, tm), tok),
            pl.BlockSpec((N_EXPERTS, tm), tok),
            pl.BlockSpec((N_EXPERTS, LANES), fixed),
            pl.BlockSpec(memory_space=pl.ANY),
        ],
        out_shape=[
            jax.ShapeDtypeStruct((b, s, d), F32),
            jax.ShapeDtypeStruct((TOP_K, t), jnp.int32),
            jax.ShapeDtypeStruct((TOP_K, t), F32),
            jax.ShapeDtypeStruct((TOP_K, t), jnp.int32),
            jax.ShapeDtypeStruct((N_EXPERTS, t), jnp.int32),
            jax.ShapeDtypeStruct((N_EXPERTS, LANES), F32),
            jax.ShapeDtypeStruct((x_rows * SUBLANES, LANES), F32),
        ],
        scratch_shapes=[
            pltpu.VMEM((tm + HALO, cw), F32),
            pltpu.VMEM((chunk, cw), F32),
            pltpu.VMEM((tm, cw), BF16),
            pltpu.VMEM((N_EXPERTS, 1), F32),
            pltpu.VMEM((tm * SUBLANES, LANES), F32),
            pltpu.VMEM((1, TOP_K * tm), jnp.int32),
            pltpu.SMEM((1, TOP_K * tm), jnp.int32),
            pltpu.SemaphoreType.DMA(()),
            pltpu.SemaphoreType.DMA(()),
        ],
        compiler_params=_params(2),
        name="mix_router",
    )(hcv, hcv, mhalo, sb, x, conv_w, conv_b, ln_g, ln_b, w_out_bf, g2, wr_t, br)


def _row_tiles(ref, row, n=1):
    size = n * SUBLANES if isinstance(n, int) else pl.multiple_of(n * SUBLANES, SUBLANES)
    return ref.at[pl.ds(pl.multiple_of(row * SUBLANES, SUBLANES), size)]


def _wait_rows(rows_ref, n, sem):
    pltpu.make_async_copy(_row_tiles(rows_ref, 0, n), _row_tiles(rows_ref, 0, n), sem).wait()


def _padfill_kernel(padlo_ref, padlen_ref, rows_in_ref, rows_ref, zeros_ref, zsem):
    del rows_in_ref
    zeros_ref[...] = jnp.zeros_like(zeros_ref)
    total = jnp.int32(0)
    for e in range(N_EXPERTS):
        lo, ln = padlo_ref[e], padlen_ref[e]
        total = total + ln
        for bit in range(EXPERT_HALF.bit_length() - 1):
            size = 1 << bit

            @pl.when((ln >> bit) & 1 == 1)
            def _():
                start = lo + ((ln >> (bit + 1)) << (bit + 1))
                pltpu.make_async_copy(_row_tiles(zeros_ref, 0, size), _row_tiles(rows_ref, start, size),
                                      zsem).start()

    @pl.when(total > 0)
    def _():
        _wait_rows(rows_ref, total, zsem)


def _padfill(pad_lo, pad_len, x_rows):
    smem = pl.BlockSpec(memory_space=pltpu.SMEM)
    return pl.pallas_call(
        _padfill_kernel,
        in_specs=[smem, smem, pl.BlockSpec(memory_space=pl.ANY)],
        out_specs=pl.BlockSpec(memory_space=pl.ANY),
        out_shape=jax.ShapeDtypeStruct(x_rows.shape, x_rows.dtype),
        scratch_shapes=[pltpu.VMEM((EXPERT_HALF * SUBLANES // 2, LANES), x_rows.dtype), pltpu.SemaphoreType.DMA(())],
        input_output_aliases={2: 0},
        compiler_params=pltpu.CompilerParams(vmem_limit_bytes=VMEM_LIMIT),
        name="padfill",
    )(pad_lo, pad_len, x_rows)


def _expert_kernel(be_ref, bsrc_ref, valid_ref, nused_ref, x_ref, wu_ref, bu_ref, wd_ref, bd_ref, y_ref,
                   wu_bf, wd_bf, *, cast_rows):
    del bsrc_ref, nused_ref
    j = pl.program_id(0)
    de = wd_ref.shape[1]
    valid = valid_ref[j]

    def half(h):
        x = jnp.concatenate([x_ref[pl.ds(h * EXPERT_HALF * SUBLANES + c, EXPERT_HALF, stride=SUBLANES), :]
                             for c in range(SUBLANES)], axis=1)
        up = jnp.dot(x.astype(BF16), wu_bf[...], preferred_element_type=F32) + bu_ref[0]
        glu = jnp.minimum(up[:, :de], SWIGLU_LIMIT)
        lin = jnp.clip(up[:, de:], -SWIGLU_LIMIT, SWIGLU_LIMIT)
        act = glu * jax.nn.sigmoid(SWIGLU_ALPHA * glu) * (lin + 1.0)
        y = jnp.dot(act.astype(BF16), wd_bf[...], preferred_element_type=F32) + bd_ref[0]
        y_ref[h * EXPERT_HALF:(h + 1) * EXPERT_HALF, :] = _pack_halves(y)

    @pl.when(valid == 0)
    def _():
        y_ref[...] = jnp.zeros_like(y_ref)

    @pl.when(valid > 0)
    def _():
        prev = be_ref[jnp.maximum(j - 1, 0)]

        @pl.when(jnp.logical_or(j == 0, be_ref[j] != prev))
        def _():
            def cast(c, _):
                r0 = pl.multiple_of(c * cast_rows, cast_rows)
                wu_bf[pl.ds(r0, cast_rows), :] = wu_ref[0, pl.ds(r0, cast_rows), :].astype(BF16)
                wd_bf[pl.ds(r0, cast_rows), :] = wd_ref[0, pl.ds(r0, cast_rows), :].astype(BF16)
                return 0

            lax.fori_loop(0, wu_ref.shape[1] // cast_rows, cast, 0)

    @pl.when(valid > EXPERT_HALF)
    def _():
        half(0)
        half(1)

    @pl.when(jnp.logical_and(valid > 0, valid <= EXPERT_HALF))
    def _():
        half(0)
        y_ref[EXPERT_HALF:, :] = jnp.zeros((EXPERT_ROWS - EXPERT_HALF, y_ref.shape[1]), y_ref.dtype)


def _pack_halves(y):
    half = y.shape[1] // 2
    bits = lambda v: lax.bitcast_convert_type(v.astype(BF16).astype(F32), jnp.uint32)
    return (bits(y[:, half:]) & jnp.uint32(0xFFFF0000)) | (bits(y[:, :half]) >> 16)


def _unpack_halves(w):
    lo = lax.bitcast_convert_type(w << 16, F32).astype(BF16)
    hi = lax.bitcast_convert_type(w & jnp.uint32(0xFFFF0000), F32).astype(BF16)
    return lo, hi


def _experts(block_expert, block_src, block_valid, n_used, x_rows, w_up, b_up, w_down, b_down):
    n_blocks = block_expert.shape[0] + 1
    block_valid = jnp.concatenate([block_valid, jnp.zeros((1,), jnp.int32)])
    ne, d, up_cols = w_up.shape
    de = w_down.shape[1]
    blk = lambda j, be, bs, bv, nu: (bs[jnp.minimum(j, nu[0] - 1)], 0)
    wsel = lambda j, be, bs, bv, nu: (be[jnp.minimum(j, nu[0] - 1)], 0, 0)
    grid_spec = pltpu.PrefetchScalarGridSpec(
        num_scalar_prefetch=4,
        grid=(n_blocks,),
        in_specs=[
            pl.BlockSpec((EXPERT_ROWS * SUBLANES, LANES), blk),
            pl.BlockSpec((1, d, up_cols), wsel),
            pl.BlockSpec((1, 1, up_cols), wsel),
            pl.BlockSpec((1, de, d), wsel),
            pl.BlockSpec((1, 1, d), wsel),
        ],
        out_specs=pl.BlockSpec((EXPERT_ROWS, d // 2), lambda j, be, bs, bv, nu: (j, 0)),
        scratch_shapes=[pltpu.VMEM((d, up_cols), BF16), pltpu.VMEM((de, d), BF16)],
    )
    return pl.pallas_call(
        functools.partial(_expert_kernel, cast_rows=WEIGHT_CAST_ROWS),
        grid_spec=grid_spec,
        out_shape=jax.ShapeDtypeStruct((n_blocks * EXPERT_ROWS, d // 2), jnp.uint32),
        compiler_params=_params(1),
        name="experts",
    )(block_expert, block_src, block_valid, n_used, x_rows, w_up, b_up.reshape(ne, 1, up_cols),
      w_down, b_down.reshape(ne, 1, d))


def _combine_plan(run_start, run_cnt, seg_start, max_pieces):
    first = run_start // SUBLANES * SUBLANES
    pieces = jnp.where(run_cnt > 0, (run_start - first + run_cnt + PIECE - 1) // PIECE, 0)
    piece_end = jnp.cumsum(pieces, axis=0)
    piece_off = piece_end - pieces
    p = jnp.arange(max_pieces, dtype=jnp.int32)
    mine = jnp.logical_and(p >= piece_off[:, :, None], p < piece_end[:, :, None])
    src = jnp.sum(jnp.where(mine, (first - piece_off * PIECE)[:, :, None], 0), axis=0) + p * PIECE
    shift = (seg_start[:, None] + piece_off * PIECE - first).T
    return (src.reshape(-1).astype(jnp.int32), piece_end[-1].astype(jnp.int32),
            shift.reshape(-1).astype(jnp.int32))


def _to_columns(x):
    pad = jnp.zeros((LANES - x.shape[0], x.shape[1]), x.dtype)
    return jnp.concatenate([x, pad], axis=0).T


def _combine_kernel(src_ref, npieces_ref, shift_ref, eid_ref, rank_ref, gate_ref, xn_ref, rows_ref, o_ref, buf,
                    sems, *, tm, max_pieces):
    i = pl.program_id(0)
    n = pl.num_programs(0)
    slot = lax.rem(i, 2)
    half = o_ref.shape[1] // 2

    def fetch(tile, s):
        def body(p, _):
            src = rows_ref.at[pl.ds(pl.multiple_of(src_ref[tile * max_pieces + p], SUBLANES), PIECE)]
            dst = buf.at[s, pl.ds(pl.multiple_of(p * PIECE, PIECE), PIECE)]
            pltpu.make_async_copy(src, dst, sems.at[s]).start()
            return 0

        lax.fori_loop(0, npieces_ref[tile], body, 0)

    @pl.when(i == 0)
    def _():
        buf[...] = jnp.zeros_like(buf)
        fetch(0, 0)

    @pl.when(i + 1 < n)
    def _():
        fetch(i + 1, 1 - slot)

    total = npieces_ref[i] * PIECE
    eid = eid_ref[...]
    shift = jnp.zeros_like(eid)
    for e in range(N_EXPERTS):
        shift = jnp.where(eid == e, shift_ref[i * N_EXPERTS + e], shift)
    staged = _to_columns((rank_ref[...] + shift).astype(F32)).astype(jnp.int32)
    gates = _to_columns(gate_ref[...])

    @pl.when(total > 0)
    def _():
        rows = pl.ds(0, pl.multiple_of(total, PIECE))
        pltpu.make_async_copy(rows_ref.at[rows], buf.at[slot, rows], sems.at[slot]).wait()

    def chunk_products(c0):
        col = lax.broadcasted_iota(jnp.int32, (tm, COMBINE_CHUNK), 1) + c0
        sel = jnp.zeros((tm, COMBINE_CHUNK), F32)
        for k in range(TOP_K):
            sel = jnp.where(staged[:, k:k + 1] == col, gates[:, k:k + 1], sel)
        sel = sel.astype(BF16)
        lo, hi = _unpack_halves(buf[slot, pl.ds(c0, COMBINE_CHUNK), :])
        return jnp.dot(sel, lo, preferred_element_type=F32), jnp.dot(sel, hi, preferred_element_type=F32)

    xn = xn_ref[...]
    acc_lo, acc_hi = xn[:, :half], xn[:, half:]
    for c in range(COMBINE_STATIC_CHUNKS):
        d_lo, d_hi = chunk_products(c * COMBINE_CHUNK)
        acc_lo, acc_hi = acc_lo + d_lo, acc_hi + d_hi
    o_ref[:, :half] = acc_lo
    o_ref[:, half:] = acc_hi

    def chunk(c, _):
        d_lo, d_hi = chunk_products(pl.multiple_of(c * COMBINE_CHUNK, COMBINE_CHUNK))
        o_ref[:, :half] += d_lo
        o_ref[:, half:] += d_hi
        return 0

    lax.fori_loop(COMBINE_STATIC_CHUNKS, (total + COMBINE_CHUNK - 1) // COMBINE_CHUNK, chunk, 0)


def _combine(run_start, run_cnt, seg_start, eid, rank, gates, xn, y_rows, tm):
    t, d = xn.shape
    n = t // tm
    cap = -(-(TOP_K * tm + 2 * PIECE * N_EXPERTS) // COMBINE_CHUNK) * COMBINE_CHUNK
    max_pieces = cap // PIECE
    src, npieces, shift = _combine_plan(run_start, run_cnt, seg_start, max_pieces)
    tok = lambda i, *_: (i, 0)
    per_k = lambda i, *_: (0, i)
    grid_spec = pltpu.PrefetchScalarGridSpec(
        num_scalar_prefetch=3,
        grid=(n,),
        in_specs=[
            pl.BlockSpec((TOP_K, tm), per_k),
            pl.BlockSpec((TOP_K, tm), per_k),
            pl.BlockSpec((TOP_K, tm), per_k),
            pl.BlockSpec((tm, d), tok),
            pl.BlockSpec(memory_space=pl.ANY),
        ],
        out_specs=pl.BlockSpec((tm, d), tok),
        scratch_shapes=[pltpu.VMEM((2, cap, d // 2), jnp.uint32), pltpu.SemaphoreType.DMA((2,))],
    )
    return pl.pallas_call(
        functools.partial(_combine_kernel, tm=tm, max_pieces=max_pieces),
        grid_spec=grid_spec,
        out_shape=jax.ShapeDtypeStruct((t, d), F32),
        compiler_params=_params(1),
        name="combine",
    )(src, npieces, shift, eid, rank, gates, xn, y_rows)


def _tile(n, want):
    t = min(n, want)
    assert n % t == 0, (n, t)
    return t


def kernel(x, meta_tokens, norm1_g, w_in, q_norm_g, k_norm_g, conv_w, conv_b, conv_ln_g, conv_ln_b,
           attn_out_g, w_out, norm2_g, w_router, b_router, w_up, b_up, w_down, b_down):
    assert norm1_g.shape[0] == 1, "single layer: meta-token rows are only keys/values and conv context"
    b, s, d = x.shape
    t = b * s
    cw = conv_w.shape[-1]

    g1 = norm1_g[0][None, :]
    w_in_bf = w_in[0].astype(BF16)
    qg = jnp.tile(q_norm_g[0], SB_HEADS)[None, :]
    kg = jnp.tile(k_norm_g[0], SB_HEADS)[None, :]

    q, k, v, hcv = _inproj(x.reshape(t, d), g1, w_in_bf, qg, kg, _tile(t, INPROJ_ROWS))
    _, km, vm, hm = _inproj(meta_tokens, g1, w_in_bf, qg, kg, N_META)
    pad = ((0, LANES - N_META), (0, 0))
    tq = _tile(s, ATTN_TILE)
    sb = _attention(q.reshape(b, s, SB_WIDTH), k.reshape(b, s, SB_WIDTH), v.reshape(b, s, SB_WIDTH),
                    jnp.pad(km, pad), jnp.pad(vm, pad), attn_out_g[0].reshape(1, SB_WIDTH), tq,
                    _tile(s, ATTN_TILES_PER_STEP * tq) // tq)

    mhalo = jnp.concatenate([jnp.zeros((HALO - N_META, cw), F32), hm], axis=0)
    tm = _tile(s, MIX_ROWS)
    wr_t = w_router[0].T
    wr_hi = wr_t.astype(BF16)
    wr_lo = (wr_t - wr_hi.astype(F32)).astype(BF16)
    cap = -(-t // EXPERT_ROWS) * EXPERT_ROWS
    xn, eid, gates, rank, before, counts, x_rows = _mix(
        hcv.reshape(b, s, cw), mhalo, sb, x, conv_w[0], conv_b[0][None, :], conv_ln_g[0][None, :],
        conv_ln_b[0][None, :], w_out[0].astype(BF16), norm2_g[0][None, :],
        jnp.concatenate([wr_hi, wr_lo], axis=0), b_router[0][:, None], tm, _tile(tm, CONV_CHUNK), cap)

    counts = counts[:, 0].astype(jnp.int32)
    padded = (counts + EXPERT_ROWS - 1) // EXPERT_ROWS * EXPERT_ROWS
    pad_end = jnp.cumsum(padded)
    pad_start = pad_end - padded
    n_blocks = t * TOP_K // EXPERT_ROWS + N_EXPERTS
    block_row = jnp.arange(n_blocks, dtype=jnp.int32) * EXPERT_ROWS
    block_expert = jnp.minimum(jnp.sum(block_row[:, None] >= pad_end[None, :], axis=1), N_EXPERTS - 1).astype(jnp.int32)
    block_off = block_row - pad_start[block_expert]
    block_src = (block_expert * cap + jnp.maximum(block_off, 0)) // EXPERT_ROWS
    block_valid = jnp.clip(counts[block_expert] - block_off, 0, EXPERT_ROWS)
    block_valid = jnp.where(block_row < pad_end[-1], block_valid, 0)
    n_used = (pad_end[-1:] // EXPERT_ROWS).astype(jnp.int32)
    seg = jnp.arange(N_EXPERTS, dtype=jnp.int32) * cap
    x_rows = _padfill(seg + counts, (-counts) % EXPERT_HALF, x_rows)
    y_rows = _experts(block_expert, block_src.astype(jnp.int32), block_valid.astype(jnp.int32), n_used, x_rows,
                      w_up[0], b_up[0], w_down[0], b_down[0])
    tc = _tile(t, COMBINE_ROWS)
    run_start = pad_start[:, None] + before[:, ::tc]
    run_end = jnp.concatenate([run_start[:, 1:], (pad_start + counts)[:, None]], axis=1)
    out = _combine(run_start, run_end - run_start, pad_start, eid, rank, gates, xn.reshape(t, d), y_rows, tc)
    return out.reshape(b, s, d)
```

```python
import functools

import jax
import jax.numpy as jnp
from jax import lax
from jax.experimental import pallas as pl
from jax.experimental.pallas import tpu as pltpu

N_META = 16
SB_HEADS = 8
SB_HEAD_DIM = 64
SB_WIDTH = SB_HEADS * SB_HEAD_DIM
CONV_KERNEL = 31
N_EXPERTS = 32
TOP_K = 4
SWIGLU_LIMIT = 7.0
SWIGLU_ALPHA = 1.702
EPS = 1e-6
F32_EXP2_UNDERFLOW = -150.0
LOG2_E = 1.4426950408889634

LANES = 128
SUBLANES = 8
INPROJ_ROWS = 1024
ATTN_TILE = 256
ATTN_TILES_PER_STEP = 8
MIX_ROWS = 1024
CONV_CHUNK = 64
COMBINE_ROWS = 256
WEIGHT_CAST_ROWS = 64
PIECE = 16
COMBINE_CHUNK = 512
COMBINE_STATIC_CHUNKS = 3
HALO = 32
EXPERT_HALF = 512
EXPERT_ROWS = 2 * EXPERT_HALF
VMEM_LIMIT = 56 * 1024 * 1024

F32 = jnp.float32
BF16 = jnp.bfloat16


def _params(n_axes, vmem=VMEM_LIMIT):
    return pltpu.CompilerParams(dimension_semantics=("arbitrary",) * n_axes, vmem_limit_bytes=vmem)


def _inproj_kernel(x_ref, g1_ref, w_ref, qg_ref, kg_ref, q_ref, k_ref, v_ref, h_ref):
    x = x_ref[...]
    ms = jnp.mean(x * x, axis=-1, keepdims=True)
    n = (x * lax.rsqrt(ms + EPS) * g1_ref[...]).astype(BF16)
    lo = lax.broadcasted_iota(jnp.int32, (1, LANES), 1) < SB_HEAD_DIM

    def proj(c0, c1):
        return jnp.dot(n, w_ref[:, c0:c1], preferred_element_type=F32)

    def head_norm(acc, g_ref, out_ref, scale):
        for c in range(SB_WIDTH // LANES):
            sl = slice(c * LANES, (c + 1) * LANES)
            a = acc[:, sl]
            sq = a * a
            s_lo = jnp.sum(jnp.where(lo, sq, 0.0), axis=-1, keepdims=True)
            s_hi = jnp.sum(jnp.where(lo, 0.0, sq), axis=-1, keepdims=True)
            r = lax.rsqrt(jnp.where(lo, s_lo, s_hi) * (1.0 / SB_HEAD_DIM) + EPS)
            out_ref[:, sl] = (a * r * (g_ref[:, sl] * scale)).astype(BF16)

    head_norm(proj(0, SB_WIDTH), qg_ref, q_ref, LOG2_E * SB_HEAD_DIM ** -0.5)
    head_norm(proj(SB_WIDTH, 2 * SB_WIDTH), kg_ref, k_ref, 1.0)
    v_ref[...] = proj(2 * SB_WIDTH, 3 * SB_WIDTH).astype(BF16)
    cw = (w_ref.shape[1] - 3 * SB_WIDTH) // 2
    val = proj(3 * SB_WIDTH, 3 * SB_WIDTH + cw)
    gate = proj(3 * SB_WIDTH + cw, 3 * SB_WIDTH + 2 * cw)
    h_ref[...] = val * jax.nn.sigmoid(gate)


def _inproj(x2, g1, w_in_bf, qg, kg, tm):
    t, d = x2.shape
    cols = w_in_bf.shape[1]
    cw = (cols - 3 * SB_WIDTH) // 2
    row = lambda i: (i, 0)
    fixed = lambda i: (0, 0)
    return pl.pallas_call(
        _inproj_kernel,
        grid=(t // tm,),
        in_specs=[
            pl.BlockSpec((tm, d), row),
            pl.BlockSpec((1, d), fixed),
            pl.BlockSpec((d, cols), fixed),
            pl.BlockSpec((1, SB_WIDTH), fixed),
            pl.BlockSpec((1, SB_WIDTH), fixed),
        ],
        out_specs=[
            pl.BlockSpec((tm, SB_WIDTH), row),
            pl.BlockSpec((tm, SB_WIDTH), row),
            pl.BlockSpec((tm, SB_WIDTH), row),
            pl.BlockSpec((tm, cw), row),
        ],
        out_shape=[
            jax.ShapeDtypeStruct((t, SB_WIDTH), BF16),
            jax.ShapeDtypeStruct((t, SB_WIDTH), BF16),
            jax.ShapeDtypeStruct((t, SB_WIDTH), BF16),
            jax.ShapeDtypeStruct((t, cw), F32),
        ],
        compiler_params=_params(1),
        name="inproj",
    )(x2, g1, w_in_bf, qg, kg)


def _attn_kernel(q_ref, k_ref, v_ref, km_ref, vm_ref, g_ref, o_ref, *, tq, nsub):
    i = pl.program_id(2)
    lane = lax.broadcasted_iota(jnp.int32, (1, LANES), 1)
    lo = lane < SB_HEAD_DIM
    zero_bf = jnp.zeros((), BF16)

    def stacked_q(sub):
        q2 = q_ref[0, sub * tq:(sub + 1) * tq, :]
        return jnp.concatenate([jnp.where(lo, q2, zero_bf), jnp.where(lo, zero_bf, q2)], axis=0)

    qss = [stacked_q(sub) for sub in range(nsub)]
    row = lax.broadcasted_iota(jnp.int32, (2 * tq, tq), 0)
    col = lax.broadcasted_iota(jnp.int32, (2 * tq, tq), 1)
    causal = col < jnp.where(row >= tq, row - tq, row)
    srow = lax.broadcasted_iota(jnp.int32, (tq, tq), 0)
    scol = lax.broadcasted_iota(jnp.int32, (tq, tq), 1)
    neg_suffix = jnp.where(srow > scol, -1.0, 0.0).astype(BF16)

    def block(qs, kb, vb, acc, r, mask, neg_suffix_m):
        z = lax.dot_general(qs, kb, (((1,), (1,)), ((), ())), preferred_element_type=F32)
        m = jnp.minimum(z, 0.0)
        p = jnp.maximum(z, 0.0)
        l1p = jnp.log2(1.0 + jnp.exp2(m - p))
        log_beta = m - l1p
        neg_keep = p + l1p
        if mask is not None:
            neg_keep = jnp.where(mask, neg_keep, 0.0)
        later = jnp.dot(neg_keep.astype(BF16), neg_suffix_m, preferred_element_type=F32)
        a = jnp.exp2(log_beta + later + r)
        if mask is not None:
            a = jnp.where(mask, a, 0.0)
        res = jnp.dot(a.astype(BF16), vb, preferred_element_type=F32)
        acc = acc + jnp.where(lo, res[:tq], res[tq:])
        return acc, r - jnp.sum(neg_keep, axis=-1, keepdims=True)

    def kv_block(j):
        start = pl.multiple_of(j * tq, tq)
        return k_ref[0, pl.ds(start, tq), :], v_ref[0, pl.ds(start, tq), :]

    def live(r):
        return jnp.max(r) > F32_EXP2_UNDERFLOW

    acc0 = jnp.zeros((tq, LANES), F32)
    r0 = jnp.zeros((2 * tq, 1), F32)
    first = i * nsub

    def guaranteed(first_has_previous):
        state = []
        for sub in range(nsub):
            acc, r = block(qss[sub], *kv_block(first + sub), acc0, r0, causal, neg_suffix)
            if sub > 0 or first_has_previous:
                acc, r = block(qss[sub], *kv_block(first + sub - 1), acc, r, None, neg_suffix)
            state += [acc, r]
        return tuple(state)

    state = lax.cond(i > 0, lambda _: guaranteed(True), lambda _: guaranteed(False), 0)

    for sub in range(nsub):
        j = first + sub
        qs, acc, r = qss[sub], state[2 * sub], state[2 * sub + 1]

        def cond(c):
            return jnp.logical_and(c[0] <= j, c[3])

        def body(c):
            acc, r = block(qs, *kv_block(j - c[0]), c[1], c[2], None, neg_suffix)
            return c[0] + 1, acc, r, live(r)

        _, acc, r, alive = lax.while_loop(cond, body, (jnp.int32(2), acc, r, live(r)))

        def meta_block(acc):
            return block(qs, km_ref[...], vm_ref[...], acc, r, lane < N_META, neg_suffix[:LANES, :LANES])[0]

        acc = lax.cond(alive, meta_block, lambda acc: acc, acc)

        sq = acc * acc
        s_lo = jnp.sum(jnp.where(lo, sq, 0.0), axis=-1, keepdims=True)
        s_hi = jnp.sum(jnp.where(lo, 0.0, sq), axis=-1, keepdims=True)
        rn = lax.rsqrt(jnp.where(lo, s_lo, s_hi) * (1.0 / SB_HEAD_DIM) + EPS)
        o_ref[0, sub * tq:(sub + 1) * tq, :] = (acc * rn * g_ref[...]).astype(BF16)


def _attention(q, k, v, km, vm, og, tq, nsub):
    b, s, _ = q.shape
    n_pairs = SB_WIDTH // LANES
    step = tq * nsub
    return pl.pallas_call(
        functools.partial(_attn_kernel, tq=tq, nsub=nsub),
        grid=(b, n_pairs, s // step),
        in_specs=[
            pl.BlockSpec((1, step, LANES), lambda b, p, i: (b, i, p)),
            pl.BlockSpec((1, s, LANES), lambda b, p, i: (b, 0, p)),
            pl.BlockSpec((1, s, LANES), lambda b, p, i: (b, 0, p)),
            pl.BlockSpec((LANES, LANES), lambda b, p, i: (0, p)),
            pl.BlockSpec((LANES, LANES), lambda b, p, i: (0, p)),
            pl.BlockSpec((1, LANES), lambda b, p, i: (0, p)),
        ],
        out_specs=pl.BlockSpec((1, step, LANES), lambda b, p, i: (b, i, p)),
        out_shape=jax.ShapeDtypeStruct((b, s, SB_WIDTH), BF16),
        compiler_params=_params(3),
        name="attention",
    )(q, k, v, km, vm, og)


def _mix_kernel(h_ref, halo_ref, mhalo_ref, sb_ref, x_ref, cw_ref, cb_ref, lg_ref, lb_ref, wo_ref,
                g2_ref, wr_ref, br_ref,
                xn_ref, eid_ref, gate_ref, rank_ref, before_ref, cnt_ref, rows_ref,
                win_ref, conv_ref, cv_ref, carry_ref, hn3_ref, posv_ref, pos_ref, dsem, psem, *, tm, chunk, cap):
    b = pl.program_id(0)
    i = pl.program_id(1)
    step = b * pl.num_programs(1) + i

    def pos_copy():
        return pltpu.make_async_copy(posv_ref, pos_ref, psem)

    def row_copy(t, k):
        return pltpu.make_async_copy(_row_tiles(hn3_ref, t), _row_tiles(rows_ref, pos_ref[0, k * tm + t]), dsem)

    @pl.when(step == 0)
    def _():
        carry_ref[...] = jnp.zeros_like(carry_ref)
        hn3_ref[...] = jnp.zeros_like(hn3_ref)
        posv_ref[...] = N_EXPERTS * cap + lax.broadcasted_iota(jnp.int32, posv_ref.shape, 1)
        pos_copy().start()

    pos_copy().wait()

    @pl.when(i == 0)
    def _():
        win_ref[0:HALO, :] = mhalo_ref[...]

    @pl.when(i > 0)
    def _():
        win_ref[0:HALO, :] = halo_ref[0]

    win_ref[HALO:, :] = h_ref[0]

    first_tap = HALO - (CONV_KERNEL - 1)

    def conv_chunk(c, _):
        r0 = pl.multiple_of(c * chunk, chunk)
        rows = chunk + HALO
        for lt in range(cb_ref.shape[1] // LANES):
            sl = slice(lt * LANES, (lt + 1) * LANES)
            window = win_ref[pl.ds(r0, rows), sl]
            acc = jnp.broadcast_to(cb_ref[:, sl], (chunk, LANES))
            for b in range(SUBLANES):
                shifted = window if b == 0 else pltpu.roll(window, shift=rows - b, axis=0)
                for a in range(HALO // SUBLANES + 1):
                    j = SUBLANES * a + b - first_tap
                    if 0 <= j < CONV_KERNEL:
                        acc = acc + cw_ref[j:j + 1, sl] * shifted[SUBLANES * a:SUBLANES * a + chunk, :]
            conv_ref[:, sl] = acc
        for tt in range(chunk):
            for k in range(TOP_K):
                row_copy(r0 + tt, k).start(priority=k % 2)
        acc = conv_ref[...]
        mu = jnp.mean(acc, axis=-1, keepdims=True)
        cen = acc - mu
        var = jnp.mean(cen * cen, axis=-1, keepdims=True)
        y = cen * lax.rsqrt(var + EPS) * lg_ref[...] + lb_ref[...]
        cv_ref[pl.ds(r0, chunk), :] = (y * jax.nn.sigmoid(y)).astype(BF16)
        return 0

    lax.fori_loop(0, tm // chunk, conv_chunk, 0)

    mixed = jnp.dot(sb_ref[0], wo_ref[0:SB_WIDTH, :], preferred_element_type=F32)
    mixed = mixed + jnp.dot(cv_ref[...], wo_ref[SB_WIDTH:, :], preferred_element_type=F32)
    xn = x_ref[0] + mixed
    xn_ref[0] = xn
    ms = jnp.mean(xn * xn, axis=-1, keepdims=True)
    hn = xn * lax.rsqrt(ms + EPS) * g2_ref[...]

    nt = (((1,), (1,)), ((), ()))
    hn_hi = hn.astype(BF16)
    hn_lo = (hn - hn_hi.astype(F32)).astype(BF16)
    both = lax.dot_general(wr_ref[...], hn_hi, nt, preferred_element_type=F32)
    logits = (both[:N_EXPERTS] + both[N_EXPERTS:]
              + lax.dot_general(wr_ref[0:N_EXPERTS, :], hn_lo, nt, preferred_element_type=F32) + br_ref[...])
    eidx = lax.broadcasted_iota(jnp.int32, logits.shape, 0)
    vals, idxs = [], []
    for _ in range(TOP_K):
        m = jnp.max(logits, axis=0, keepdims=True)
        sel = jnp.min(jnp.where(logits == m, eidx, N_EXPERTS), axis=0, keepdims=True)
        vals.append(m)
        idxs.append(sel)
        logits = jnp.where(eidx == sel, -jnp.inf, logits)
    exps = [jnp.exp(v - vals[0]) for v in vals]
    denom = exps[0] + exps[1] + exps[2] + exps[3]
    gate_ref[...] = jnp.concatenate([e / denom for e in exps], axis=0)
    eid_ref[...] = jnp.concatenate(idxs, axis=0)

    onehots = [(eidx == s).astype(F32) for s in idxs]
    chosen = onehots[0] + onehots[1] + onehots[2] + onehots[3]
    tr = lax.broadcasted_iota(jnp.int32, (tm, tm), 0)
    tc = lax.broadcasted_iota(jnp.int32, (tm, tm), 1)
    before = (tr < tc).astype(BF16)
    prefix = jnp.dot(chosen.astype(BF16), before, preferred_element_type=F32) + carry_ref[...]
    ranks = [jnp.sum(o * prefix, axis=0, keepdims=True).astype(jnp.int32) for o in onehots]
    rank_ref[...] = jnp.concatenate(ranks, axis=0)
    before_ref[...] = prefix.astype(jnp.int32)
    carry_ref[...] = carry_ref[...] + jnp.sum(chosen, axis=1, keepdims=True)
    cnt_ref[...] = jnp.broadcast_to(carry_ref[...], cnt_ref.shape)

    _wait_rows(rows_ref, TOP_K * tm, dsem)
    groups = hn.shape[1] // LANES
    for c in range(groups):
        hn3_ref[pl.ds(c, tm, stride=groups), :] = hn[:, c * LANES:(c + 1) * LANES]
    for k in range(TOP_K):
        posv_ref[:, k * tm:(k + 1) * tm] = idxs[k] * cap + ranks[k]
    pos_copy().start()

    @pl.when(step == pl.num_programs(0) * pl.num_programs(1) - 1)
    def _():
        pos_copy().wait()

        def issue(t, _):
            for k in range(TOP_K):
                row_copy(t, k).start(priority=k % 2)
            return 0

        lax.fori_loop(0, tm, issue, 0, unroll=2)
        _wait_rows(rows_ref, TOP_K * tm, dsem)


def _mix(hcv, mhalo, sb, x, conv_w, conv_b, ln_g, ln_b, w_out_bf, g2, wr_t, br, tm, chunk, cap):
    b, s, d = x.shape
    cw = hcv.shape[-1]
    t = b * s
    per = s // tm
    assert d == SUBLANES * LANES, d
    tile = lambda b, i: (b, i, 0)
    fixed = lambda b, i: (0, 0)
    tok = lambda b, i: (0, b * per + i)
    x_rows = N_EXPERTS * cap + TOP_K * tm
    return pl.pallas_call(
        functools.partial(_mix_kernel, tm=tm, chunk=chunk, cap=cap),
        grid=(b, per),
        in_specs=[
            pl.BlockSpec((1, tm, cw), tile),
            pl.BlockSpec((1, HALO, cw), lambda b, i: (b, jnp.maximum(i * (tm // HALO) - 1, 0), 0)),
            pl.BlockSpec((HALO, cw), fixed),
            pl.BlockSpec((1, tm, SB_WIDTH), tile),
            pl.BlockSpec((1, tm, d), tile),
            pl.BlockSpec((CONV_KERNEL, cw), fixed),
            pl.BlockSpec((1, cw), fixed),
            pl.BlockSpec((1, cw), fixed),
            pl.BlockSpec((1, cw), fixed),
            pl.BlockSpec((SB_WIDTH + cw, d), fixed),
            pl.BlockSpec((1, d), fixed),
            pl.BlockSpec((2 * N_EXPERTS, d), fixed),
            pl.BlockSpec((N_EXPERTS, 1), fixed),
        ],
        out_specs=[
            pl.BlockSpec((1, tm, d), tile),
            pl.BlockSpec((TOP_K, tm), tok),
            pl.BlockSpec((TOP_K, tm), tok),
            pl.BlockSpec((TOP_K, tm), tok),
            pl.BlockSpec((N_EXPERTS, tm), tok),
            pl.BlockSpec((N_EXPERTS, LANES), fixed),
            pl.BlockSpec(memory_space=pl.ANY),
        ],
        out_shape=[
            jax.ShapeDtypeStruct((b, s, d), F32),
            jax.ShapeDtypeStruct((TOP_K, t), jnp.int32),
            jax.ShapeDtypeStruct((TOP_K, t), F32),
            jax.ShapeDtypeStruct((TOP_K, t), jnp.int32),
            jax.ShapeDtypeStruct((N_EXPERTS, t), jnp.int32),
            jax.ShapeDtypeStruct((N_EXPERTS, LANES), F32),
            jax.ShapeDtypeStruct((x_rows * SUBLANES, LANES), F32),
        ],
        scratch_shapes=[
            pltpu.VMEM((tm + HALO, cw), F32),
            pltpu.VMEM((chunk, cw), F32),
            pltpu.VMEM((tm, cw), BF16),
            pltpu.VMEM((N_EXPERTS, 1), F32),
            pltpu.VMEM((tm * SUBLANES, LANES), F32),
            pltpu.VMEM((1, TOP_K * tm), jnp.int32),
            pltpu.SMEM((1, TOP_K * tm), jnp.int32),
            pltpu.SemaphoreType.DMA(()),
            pltpu.SemaphoreType.DMA(()),
        ],
        compiler_params=_params(2),
        name="mix_router",
    )(hcv, hcv, mhalo, sb, x, conv_w, conv_b, ln_g, ln_b, w_out_bf, g2, wr_t, br)


def _row_tiles(ref, row, n=1):
    size = n * SUBLANES if isinstance(n, int) else pl.multiple_of(n * SUBLANES, SUBLANES)
    return ref.at[pl.ds(pl.multiple_of(row * SUBLANES, SUBLANES), size)]


def _wait_rows(rows_ref, n, sem):
    pltpu.make_async_copy(_row_tiles(rows_ref, 0, n), _row_tiles(rows_ref, 0, n), sem).wait()


def _padfill_kernel(padlo_ref, padlen_ref, rows_in_ref, rows_ref, zeros_ref, zsem):
    del rows_in_ref
    zeros_ref[...] = jnp.zeros_like(zeros_ref)
    total = jnp.int32(0)
    for e in range(N_EXPERTS):
        lo, ln = padlo_ref[e], padlen_ref[e]
        total = total + ln
        for bit in range(EXPERT_HALF.bit_length() - 1):
            size = 1 << bit

            @pl.when((ln >> bit) & 1 == 1)
            def _():
                start = lo + ((ln >> (bit + 1)) << (bit + 1))
                pltpu.make_async_copy(_row_tiles(zeros_ref, 0, size), _row_tiles(rows_ref, start, size),
                                      zsem).start()

    @pl.when(total > 0)
    def _():
        _wait_rows(rows_ref, total, zsem)


def _padfill(pad_lo, pad_len, x_rows):
    smem = pl.BlockSpec(memory_space=pltpu.SMEM)
    return pl.pallas_call(
        _padfill_kernel,
        in_specs=[smem, smem, pl.BlockSpec(memory_space=pl.ANY)],
        out_specs=pl.BlockSpec(memory_space=pl.ANY),
        out_shape=jax.ShapeDtypeStruct(x_rows.shape, x_rows.dtype),
        scratch_shapes=[pltpu.VMEM((EXPERT_HALF * SUBLANES // 2, LANES), x_rows.dtype), pltpu.SemaphoreType.DMA(())],
        input_output_aliases={2: 0},
        compiler_params=pltpu.CompilerParams(vmem_limit_bytes=VMEM_LIMIT),
        name="padfill",
    )(pad_lo, pad_len, x_rows)


def _expert_kernel(be_ref, bsrc_ref, valid_ref, nused_ref, x_ref, wu_ref, bu_ref, wd_ref, bd_ref, y_ref,
                   wu_bf, wd_bf, *, cast_rows):
    del bsrc_ref, nused_ref
    j = pl.program_id(0)
    de = wd_ref.shape[1]
    valid = valid_ref[j]

    def half(h):
        x = jnp.concatenate([x_ref[pl.ds(h * EXPERT_HALF * SUBLANES + c, EXPERT_HALF, stride=SUBLANES), :]
                             for c in range(SUBLANES)], axis=1)
        up = jnp.dot(x.astype(BF16), wu_bf[...], preferred_element_type=F32) + bu_ref[0]
        glu = jnp.minimum(up[:, :de], SWIGLU_LIMIT)
        lin = jnp.clip(up[:, de:], -SWIGLU_LIMIT, SWIGLU_LIMIT)
        act = glu * jax.nn.sigmoid(SWIGLU_ALPHA * glu) * (lin + 1.0)
        y = jnp.dot(act.astype(BF16), wd_bf[...], preferred_element_type=F32) + bd_ref[0]
        y_ref[h * EXPERT_HALF:(h + 1) * EXPERT_HALF, :] = _pack_halves(y)

    @pl.when(valid == 0)
    def _():
        y_ref[...] = jnp.zeros_like(y_ref)

    @pl.when(valid > 0)
    def _():
        prev = be_ref[jnp.maximum(j - 1, 0)]

        @pl.when(jnp.logical_or(j == 0, be_ref[j] != prev))
        def _():
            def cast(c, _):
                r0 = pl.multiple_of(c * cast_rows, cast_rows)
                wu_bf[pl.ds(r0, cast_rows), :] = wu_ref[0, pl.ds(r0, cast_rows), :].astype(BF16)
                wd_bf[pl.ds(r0, cast_rows), :] = wd_ref[0, pl.ds(r0, cast_rows), :].astype(BF16)
                return 0

            lax.fori_loop(0, wu_ref.shape[1] // cast_rows, cast, 0)

    @pl.when(valid > EXPERT_HALF)
    def _():
        half(0)
        half(1)

    @pl.when(jnp.logical_and(valid > 0, valid <= EXPERT_HALF))
    def _():
        half(0)
        y_ref[EXPERT_HALF:, :] = jnp.zeros((EXPERT_ROWS - EXPERT_HALF, y_ref.shape[1]), y_ref.dtype)


def _pack_halves(y):
    half = y.shape[1] // 2
    bits = lambda v: lax.bitcast_convert_type(v.astype(BF16).astype(F32), jnp.uint32)
    return (bits(y[:, half:]) & jnp.uint32(0xFFFF0000)) | (bits(y[:, :half]) >> 16)


def _unpack_halves(w):
    lo = lax.bitcast_convert_type(w << 16, F32).astype(BF16)
    hi = lax.bitcast_convert_type(w & jnp.uint32(0xFFFF0000), F32).astype(BF16)
    return lo, hi


def _experts(block_expert, block_src, block_valid, n_used, x_rows, w_up, b_up, w_down, b_down):
    n_blocks = block_expert.shape[0] + 1
    block_valid = jnp.concatenate([block_valid, jnp.zeros((1,), jnp.int32)])
    ne, d, up_cols = w_up.shape
    de = w_down.shape[1]
    blk = lambda j, be, bs, bv, nu: (bs[jnp.minimum(j, nu[0] - 1)], 0)
    wsel = lambda j, be, bs, bv, nu: (be[jnp.minimum(j, nu[0] - 1)], 0, 0)
    grid_spec = pltpu.PrefetchScalarGridSpec(
        num_scalar_prefetch=4,
        grid=(n_blocks,),
        in_specs=[
            pl.BlockSpec((EXPERT_ROWS * SUBLANES, LANES), blk),
            pl.BlockSpec((1, d, up_cols), wsel),
            pl.BlockSpec((1, 1, up_cols), wsel),
            pl.BlockSpec((1, de, d), wsel),
            pl.BlockSpec((1, 1, d), wsel),
        ],
        out_specs=pl.BlockSpec((EXPERT_ROWS, d // 2), lambda j, be, bs, bv, nu: (j, 0)),
        scratch_shapes=[pltpu.VMEM((d, up_cols), BF16), pltpu.VMEM((de, d), BF16)],
    )
    return pl.pallas_call(
        functools.partial(_expert_kernel, cast_rows=WEIGHT_CAST_ROWS),
        grid_spec=grid_spec,
        out_shape=jax.ShapeDtypeStruct((n_blocks * EXPERT_ROWS, d // 2), jnp.uint32),
        compiler_params=_params(1),
        name="experts",
    )(block_expert, block_src, block_valid, n_used, x_rows, w_up, b_up.reshape(ne, 1, up_cols),
      w_down, b_down.reshape(ne, 1, d))


def _combine_plan(run_start, run_cnt, seg_start, max_pieces):
    first = run_start // SUBLANES * SUBLANES
    pieces = jnp.where(run_cnt > 0, (run_start - first + run_cnt + PIECE - 1) // PIECE, 0)
    piece_end = jnp.cumsum(pieces, axis=0)
    piece_off = piece_end - pieces
    p = jnp.arange(max_pieces, dtype=jnp.int32)
    mine = jnp.logical_and(p >= piece_off[:, :, None], p < piece_end[:, :, None])
    src = jnp.sum(jnp.where(mine, (first - piece_off * PIECE)[:, :, None], 0), axis=0) + p * PIECE
    shift = (seg_start[:, None] + piece_off * PIECE - first).T
    return (src.reshape(-1).astype(jnp.int32), piece_end[-1].astype(jnp.int32),
            shift.reshape(-1).astype(jnp.int32))


def _to_columns(x):
    pad = jnp.zeros((LANES - x.shape[0], x.shape[1]), x.dtype)
    return jnp.concatenate([x, pad], axis=0).T


def _combine_kernel(src_ref, npieces_ref, shift_ref, eid_ref, rank_ref, gate_ref, xn_ref, rows_ref, o_ref, buf,
                    sems, *, tm, max_pieces):
    i = pl.program_id(0)
    n = pl.num_programs(0)
    slot = lax.rem(i, 2)
    half = o_ref.shape[1] // 2

    def fetch(tile, s):
        def body(p, _):
            src = rows_ref.at[pl.ds(pl.multiple_of(src_ref[tile * max_pieces + p], SUBLANES), PIECE)]
            dst = buf.at[s, pl.ds(pl.multiple_of(p * PIECE, PIECE), PIECE)]
            pltpu.make_async_copy(src, dst, sems.at[s]).start()
            return 0

        lax.fori_loop(0, npieces_ref[tile], body, 0)

    @pl.when(i == 0)
    def _():
        buf[...] = jnp.zeros_like(buf)
        fetch(0, 0)

    @pl.when(i + 1 < n)
    def _():
        fetch(i + 1, 1 - slot)

    total = npieces_ref[i] * PIECE
    eid = eid_ref[...]
    shift = jnp.zeros_like(eid)
    for e in range(N_EXPERTS):
        shift = jnp.where(eid == e, shift_ref[i * N_EXPERTS + e], shift)
    staged = _to_columns((rank_ref[...] + shift).astype(F32)).astype(jnp.int32)
    gates = _to_columns(gate_ref[...])

    @pl.when(total > 0)
    def _():
        rows = pl.ds(0, pl.multiple_of(total, PIECE))
        pltpu.make_async_copy(rows_ref.at[rows], buf.at[slot, rows], sems.at[slot]).wait()

    def chunk_products(c0):
        col = lax.broadcasted_iota(jnp.int32, (tm, COMBINE_CHUNK), 1) + c0
        sel = jnp.zeros((tm, COMBINE_CHUNK), F32)
        for k in range(TOP_K):
            sel = jnp.where(staged[:, k:k + 1] == col, gates[:, k:k + 1], sel)
        sel = sel.astype(BF16)
        lo, hi = _unpack_halves(buf[slot, pl.ds(c0, COMBINE_CHUNK), :])
        return jnp.dot(sel, lo, preferred_element_type=F32), jnp.dot(sel, hi, preferred_element_type=F32)

    xn = xn_ref[...]
    acc_lo, acc_hi = xn[:, :half], xn[:, half:]
    for c in range(COMBINE_STATIC_CHUNKS):
        d_lo, d_hi = chunk_products(c * COMBINE_CHUNK)
        acc_lo, acc_hi = acc_lo + d_lo, acc_hi + d_hi
    o_ref[:, :half] = acc_lo
    o_ref[:, half:] = acc_hi

    def chunk(c, _):
        d_lo, d_hi = chunk_products(pl.multiple_of(c * COMBINE_CHUNK, COMBINE_CHUNK))
        o_ref[:, :half] += d_lo
        o_ref[:, half:] += d_hi
        return 0

    lax.fori_loop(COMBINE_STATIC_CHUNKS, (total + COMBINE_CHUNK - 1) // COMBINE_CHUNK, chunk, 0)


def _combine(run_start, run_cnt, seg_start, eid, rank, gates, xn, y_rows, tm):
    t, d = xn.shape
    n = t // tm
    cap = -(-(TOP_K * tm + 2 * PIECE * N_EXPERTS) // COMBINE_CHUNK) * COMBINE_CHUNK
    max_pieces = cap // PIECE
    src, npieces, shift = _combine_plan(run_start, run_cnt, seg_start, max_pieces)
    tok = lambda i, *_: (i, 0)
    per_k = lambda i, *_: (0, i)
    grid_spec = pltpu.PrefetchScalarGridSpec(
        num_scalar_prefetch=3,
        grid=(n,),
        in_specs=[
            pl.BlockSpec((TOP_K, tm), per_k),
            pl.BlockSpec((TOP_K, tm), per_k),
            pl.BlockSpec((TOP_K, tm), per_k),
            pl.BlockSpec((tm, d), tok),
            pl.BlockSpec(memory_space=pl.ANY),
        ],
        out_specs=pl.BlockSpec((tm, d), tok),
        scratch_shapes=[pltpu.VMEM((2, cap, d // 2), jnp.uint32), pltpu.SemaphoreType.DMA((2,))],
    )
    return pl.pallas_call(
        functools.partial(_combine_kernel, tm=tm, max_pieces=max_pieces),
        grid_spec=grid_spec,
        out_shape=jax.ShapeDtypeStruct((t, d), F32),
        compiler_params=_params(1),
        name="combine",
    )(src, npieces, shift, eid, rank, gates, xn, y_rows)


def _tile(n, want):
    t = min(n, want)
    assert n % t == 0, (n, t)
    return t


def kernel(x, meta_tokens, norm1_g, w_in, q_norm_g, k_norm_g, conv_w, conv_b, conv_ln_g, conv_ln_b,
           attn_out_g, w_out, norm2_g, w_router, b_router, w_up, b_up, w_down, b_down):
    assert norm1_g.shape[0] == 1, "single layer: meta-token rows are only keys/values and conv context"
    b, s, d = x.shape
    t = b * s
    cw = conv_w.shape[-1]

    g1 = norm1_g[0][None, :]
    w_in_bf = w_in[0].astype(BF16)
    qg = jnp.tile(q_norm_g[0], SB_HEADS)[None, :]
    kg = jnp.tile(k_norm_g[0], SB_HEADS)[None, :]

    q, k, v, hcv = _inproj(x.reshape(t, d), g1, w_in_bf, qg, kg, _tile(t, INPROJ_ROWS))
    _, km, vm, hm = _inproj(meta_tokens, g1, w_in_bf, qg, kg, N_META)
    pad = ((0, LANES - N_META), (0, 0))
    tq = _tile(s, ATTN_TILE)
    sb = _attention(q.reshape(b, s, SB_WIDTH), k.reshape(b, s, SB_WIDTH), v.reshape(b, s, SB_WIDTH),
                    jnp.pad(km, pad), jnp.pad(vm, pad), attn_out_g[0].reshape(1, SB_WIDTH), tq,
                    _tile(s, ATTN_TILES_PER_STEP * tq) // tq)

    mhalo = jnp.concatenate([jnp.zeros((HALO - N_META, cw), F32), hm], axis=0)
    tm = _tile(s, MIX_ROWS)
    wr_t = w_router[0].T
    wr_hi = wr_t.astype(BF16)
    wr_lo = (wr_t - wr_hi.astype(F32)).astype(BF16)
    cap = -(-t // EXPERT_ROWS) * EXPERT_ROWS
    xn, eid, gates, rank, before, counts, x_rows = _mix(
        hcv.reshape(b, s, cw), mhalo, sb, x, conv_w[0], conv_b[0][None, :], conv_ln_g[0][None, :],
        conv_ln_b[0][None, :], w_out[0].astype(BF16), norm2_g[0][None, :],
        jnp.concatenate([wr_hi, wr_lo], axis=0), b_router[0][:, None], tm, _tile(tm, CONV_CHUNK), cap)

    counts = counts[:, 0].astype(jnp.int32)
    padded = (counts + EXPERT_ROWS - 1) // EXPERT_ROWS * EXPERT_ROWS
    pad_end = jnp.cumsum(padded)
    pad_start = pad_end - padded
    n_blocks = t * TOP_K // EXPERT_ROWS + N_EXPERTS
    block_row = jnp.arange(n_blocks, dtype=jnp.int32) * EXPERT_ROWS
    block_expert = jnp.minimum(jnp.sum(block_row[:, None] >= pad_end[None, :], axis=1), N_EXPERTS - 1).astype(jnp.int32)
    mine = block_expert[:, None] == jnp.arange(N_EXPERTS, dtype=jnp.int32)
    lookup = lambda table: jnp.sum(jnp.where(mine, table[None, :], 0), axis=1)
    block_off = block_row - lookup(pad_start)
    block_src = (block_expert * cap + jnp.maximum(block_off, 0)) // EXPERT_ROWS
    block_valid = jnp.clip(lookup(counts) - block_off, 0, EXPERT_ROWS)
    block_valid = jnp.where(block_row < pad_end[-1], block_valid, 0)
    n_used = (pad_end[-1:] // EXPERT_ROWS).astype(jnp.int32)
    seg = jnp.arange(N_EXPERTS, dtype=jnp.int32) * cap
    x_rows = _padfill(seg + counts, (-counts) % EXPERT_HALF, x_rows)
    y_rows = _experts(block_expert, block_src.astype(jnp.int32), block_valid.astype(jnp.int32), n_used, x_rows,
                      w_up[0], b_up[0], w_down[0], b_down[0])
    tc = _tile(t, COMBINE_ROWS)
    run_start = pad_start[:, None] + before[:, ::tc]
    run_end = jnp.concatenate([run_start[:, 1:], (pad_start + counts)[:, None]], axis=1)
    out = _combine(run_start, run_end - run_start, pad_start, eid, rank, gates, xn.reshape(t, d), y_rows, tc)
    return out.reshape(b, s, d)
```

```python
import functools

import jax
import jax.numpy as jnp
from jax import lax
from jax.experimental import pallas as pl
from jax.experimental.pallas import tpu as pltpu

N_META = 16
SB_HEADS = 8
SB_HEAD_DIM = 64
SB_WIDTH = SB_HEADS * SB_HEAD_DIM
CONV_KERNEL = 31
N_EXPERTS = 32
TOP_K = 4
SWIGLU_LIMIT = 7.0
SWIGLU_ALPHA = 1.702
EPS = 1e-6
F32_EXP2_UNDERFLOW = -150.0
LOG2_E = 1.4426950408889634

LANES = 128
SUBLANES = 8
INPROJ_ROWS = 1024
ATTN_TILE = 256
ATTN_TILES_PER_STEP = 8
MIX_ROWS = 1024
CONV_CHUNK = 64
COMBINE_ROWS = 256
WEIGHT_CAST_ROWS = 64
PIECE = 16
COMBINE_CHUNK = 512
COMBINE_STATIC_CHUNKS = 3
HALO = 32
EXPERT_HALF = 512
EXPERT_ROWS = 2 * EXPERT_HALF
VMEM_LIMIT = 56 * 1024 * 1024

F32 = jnp.float32
BF16 = jnp.bfloat16


def _params(n_axes, vmem=VMEM_LIMIT):
    return pltpu.CompilerParams(dimension_semantics=("arbitrary",) * n_axes, vmem_limit_bytes=vmem)


def _inproj_kernel(x_ref, g1_ref, w_ref, qg_ref, kg_ref, q_ref, k_ref, v_ref, h_ref):
    x = x_ref[...]
    ms = jnp.mean(x * x, axis=-1, keepdims=True)
    n = (x * lax.rsqrt(ms + EPS) * g1_ref[...]).astype(BF16)
    lo = lax.broadcasted_iota(jnp.int32, (1, LANES), 1) < SB_HEAD_DIM

    def proj(c0, c1):
        return jnp.dot(n, w_ref[:, c0:c1], preferred_element_type=F32)

    def head_norm(acc, g_ref, out_ref, scale):
        for c in range(SB_WIDTH // LANES):
            sl = slice(c * LANES, (c + 1) * LANES)
            a = acc[:, sl]
            sq = a * a
            s_lo = jnp.sum(jnp.where(lo, sq, 0.0), axis=-1, keepdims=True)
            s_hi = jnp.sum(jnp.where(lo, 0.0, sq), axis=-1, keepdims=True)
            r = lax.rsqrt(jnp.where(lo, s_lo, s_hi) * (1.0 / SB_HEAD_DIM) + EPS)
            out_ref[:, sl] = (a * r * (g_ref[:, sl] * scale)).astype(BF16)

    head_norm(proj(0, SB_WIDTH), qg_ref, q_ref, LOG2_E * SB_HEAD_DIM ** -0.5)
    head_norm(proj(SB_WIDTH, 2 * SB_WIDTH), kg_ref, k_ref, 1.0)
    v_ref[...] = proj(2 * SB_WIDTH, 3 * SB_WIDTH).astype(BF16)
    cw = (w_ref.shape[1] - 3 * SB_WIDTH) // 2
    val = proj(3 * SB_WIDTH, 3 * SB_WIDTH + cw)
    gate = proj(3 * SB_WIDTH + cw, 3 * SB_WIDTH + 2 * cw)
    h_ref[...] = val * jax.nn.sigmoid(gate)


def _inproj(x2, g1, w_in_bf, qg, kg, tm):
    t, d = x2.shape
    cols = w_in_bf.shape[1]
    cw = (cols - 3 * SB_WIDTH) // 2
    row = lambda i: (i, 0)
    fixed = lambda i: (0, 0)
    return pl.pallas_call(
        _inproj_kernel,
        grid=(t // tm,),
        in_specs=[
            pl.BlockSpec((tm, d), row),
            pl.BlockSpec((1, d), fixed),
            pl.BlockSpec((d, cols), fixed),
            pl.BlockSpec((1, SB_WIDTH), fixed),
            pl.BlockSpec((1, SB_WIDTH), fixed),
        ],
        out_specs=[
            pl.BlockSpec((tm, SB_WIDTH), row),
            pl.BlockSpec((tm, SB_WIDTH), row),
            pl.BlockSpec((tm, SB_WIDTH), row),
            pl.BlockSpec((tm, cw), row),
        ],
        out_shape=[
            jax.ShapeDtypeStruct((t, SB_WIDTH), BF16),
            jax.ShapeDtypeStruct((t, SB_WIDTH), BF16),
            jax.ShapeDtypeStruct((t, SB_WIDTH), BF16),
            jax.ShapeDtypeStruct((t, cw), F32),
        ],
        compiler_params=_params(1),
        name="inproj",
    )(x2, g1, w_in_bf, qg, kg)


def _attn_kernel(q_ref, k_ref, v_ref, km_ref, vm_ref, g_ref, o_ref, *, tq, nsub):
    i = pl.program_id(2)
    lane = lax.broadcasted_iota(jnp.int32, (1, LANES), 1)
    lo = lane < SB_HEAD_DIM
    zero_bf = jnp.zeros((), BF16)

    def stacked_q(sub):
        q2 = q_ref[0, sub * tq:(sub + 1) * tq, :]
        return jnp.concatenate([jnp.where(lo, q2, zero_bf), jnp.where(lo, zero_bf, q2)], axis=0)

    qss = [stacked_q(sub) for sub in range(nsub)]
    row = lax.broadcasted_iota(jnp.int32, (2 * tq, tq), 0)
    col = lax.broadcasted_iota(jnp.int32, (2 * tq, tq), 1)
    causal = col < jnp.where(row >= tq, row - tq, row)
    srow = lax.broadcasted_iota(jnp.int32, (tq, tq), 0)
    scol = lax.broadcasted_iota(jnp.int32, (tq, tq), 1)
    neg_suffix = jnp.where(srow > scol, -1.0, 0.0).astype(BF16)

    def block(qs, kb, vb, acc, r, mask, neg_suffix_m):
        z = lax.dot_general(qs, kb, (((1,), (1,)), ((), ())), preferred_element_type=F32)
        m = jnp.minimum(z, 0.0)
        p = jnp.maximum(z, 0.0)
        l1p = jnp.log2(1.0 + jnp.exp2(m - p))
        log_beta = m - l1p
        neg_keep = p + l1p
        if mask is not None:
            neg_keep = jnp.where(mask, neg_keep, 0.0)
        later = jnp.dot(neg_keep.astype(BF16), neg_suffix_m, preferred_element_type=F32)
        a = jnp.exp2(log_beta + later + r)
        if mask is not None:
            a = jnp.where(mask, a, 0.0)
        res = jnp.dot(a.astype(BF16), vb, preferred_element_type=F32)
        acc = acc + jnp.where(lo, res[:tq], res[tq:])
        return acc, r - jnp.sum(neg_keep, axis=-1, keepdims=True)

    def kv_block(j):
        start = pl.multiple_of(j * tq, tq)
        return k_ref[0, pl.ds(start, tq), :], v_ref[0, pl.ds(start, tq), :]

    def live(r):
        return jnp.max(r) > F32_EXP2_UNDERFLOW

    acc0 = jnp.zeros((tq, LANES), F32)
    r0 = jnp.zeros((2 * tq, 1), F32)
    first = i * nsub

    def guaranteed(first_has_previous):
        state = []
        for sub in range(nsub):
            acc, r = block(qss[sub], *kv_block(first + sub), acc0, r0, causal, neg_suffix)
            if sub > 0 or first_has_previous:
                acc, r = block(qss[sub], *kv_block(first + sub - 1), acc, r, None, neg_suffix)
            state += [acc, r]
        return tuple(state)

    state = lax.cond(i > 0, lambda _: guaranteed(True), lambda _: guaranteed(False), 0)

    for sub in range(nsub):
        j = first + sub
        qs, acc, r = qss[sub], state[2 * sub], state[2 * sub + 1]

        def cond(c):
            return jnp.logical_and(c[0] <= j, c[3])

        def body(c):
            acc, r = block(qs, *kv_block(j - c[0]), c[1], c[2], None, neg_suffix)
            return c[0] + 1, acc, r, live(r)

        _, acc, r, alive = lax.while_loop(cond, body, (jnp.int32(2), acc, r, live(r)))

        def meta_block(acc):
            return block(qs, km_ref[...], vm_ref[...], acc, r, lane < N_META, neg_suffix[:LANES, :LANES])[0]

        acc = lax.cond(alive, meta_block, lambda acc: acc, acc)

        sq = acc * acc
        s_lo = jnp.sum(jnp.where(lo, sq, 0.0), axis=-1, keepdims=True)
        s_hi = jnp.sum(jnp.where(lo, 0.0, sq), axis=-1, keepdims=True)
        rn = lax.rsqrt(jnp.where(lo, s_lo, s_hi) * (1.0 / SB_HEAD_DIM) + EPS)
        o_ref[0, sub * tq:(sub + 1) * tq, :] = (acc * rn * g_ref[...]).astype(BF16)


def _attention(q, k, v, km, vm, og, tq, nsub):
    b, s, _ = q.shape
    n_pairs = SB_WIDTH // LANES
    step = tq * nsub
    return pl.pallas_call(
        functools.partial(_attn_kernel, tq=tq, nsub=nsub),
        grid=(b, n_pairs, s // step),
        in_specs=[
            pl.BlockSpec((1, step, LANES), lambda b, p, i: (b, i, p)),
            pl.BlockSpec((1, s, LANES), lambda b, p, i: (b, 0, p)),
            pl.BlockSpec((1, s, LANES), lambda b, p, i: (b, 0, p)),
            pl.BlockSpec((LANES, LANES), lambda b, p, i: (0, p)),
            pl.BlockSpec((LANES, LANES), lambda b, p, i: (0, p)),
            pl.BlockSpec((1, LANES), lambda b, p, i: (0, p)),
        ],
        out_specs=pl.BlockSpec((1, step, LANES), lambda b, p, i: (b, i, p)),
        out_shape=jax.ShapeDtypeStruct((b, s, SB_WIDTH), BF16),
        compiler_params=_params(3),
        name="attention",
    )(q, k, v, km, vm, og)


def _mix_kernel(h_ref, halo_ref, mhalo_ref, sb_ref, x_ref, cw_ref, cb_ref, lg_ref, lb_ref, wo_ref,
                g2_ref, wr_ref, br_ref,
                xn_ref, eid_ref, gate_ref, rank_ref, before_ref, cnt_ref, rows_ref,
                win_ref, conv_ref, cv_ref, carry_ref, hn3_ref, posv_ref, pos_ref, dsem, psem, *, tm, chunk, cap):
    b = pl.program_id(0)
    i = pl.program_id(1)
    step = b * pl.num_programs(1) + i

    def pos_copy():
        return pltpu.make_async_copy(posv_ref, pos_ref, psem)

    def row_copy(t, k):
        return pltpu.make_async_copy(_row_tiles(hn3_ref, t), _row_tiles(rows_ref, pos_ref[0, k * tm + t]), dsem)

    @pl.when(step == 0)
    def _():
        carry_ref[...] = jnp.zeros_like(carry_ref)
        hn3_ref[...] = jnp.zeros_like(hn3_ref)
        posv_ref[...] = N_EXPERTS * cap + lax.broadcasted_iota(jnp.int32, posv_ref.shape, 1)
        pos_copy().start()

    pos_copy().wait()

    @pl.when(i == 0)
    def _():
        win_ref[0:HALO, :] = mhalo_ref[...]

    @pl.when(i > 0)
    def _():
        win_ref[0:HALO, :] = halo_ref[0]

    win_ref[HALO:, :] = h_ref[0]

    first_tap = HALO - (CONV_KERNEL - 1)

    def conv_chunk(c, _):
        r0 = pl.multiple_of(c * chunk, chunk)
        rows = chunk + HALO
        for lt in range(cb_ref.shape[1] // LANES):
            sl = slice(lt * LANES, (lt + 1) * LANES)
            window = win_ref[pl.ds(r0, rows), sl]
            acc = jnp.broadcast_to(cb_ref[:, sl], (chunk, LANES))
            for b in range(SUBLANES):
                shifted = window if b == 0 else pltpu.roll(window, shift=rows - b, axis=0)
                for a in range(HALO // SUBLANES + 1):
                    j = SUBLANES * a + b - first_tap
                    if 0 <= j < CONV_KERNEL:
                        acc = acc + cw_ref[j:j + 1, sl] * shifted[SUBLANES * a:SUBLANES * a + chunk, :]
            conv_ref[:, sl] = acc
        for tt in range(chunk):
            for k in range(TOP_K):
                row_copy(r0 + tt, k).start(priority=k % 2)
        acc = conv_ref[...]
        mu = jnp.mean(acc, axis=-1, keepdims=True)
        cen = acc - mu
        var = jnp.mean(cen * cen, axis=-1, keepdims=True)
        y = cen * lax.rsqrt(var + EPS) * lg_ref[...] + lb_ref[...]
        cv_ref[pl.ds(r0, chunk), :] = (y * jax.nn.sigmoid(y)).astype(BF16)
        return 0

    lax.fori_loop(0, tm // chunk, conv_chunk, 0)

    mixed = jnp.dot(sb_ref[0], wo_ref[0:SB_WIDTH, :], preferred_element_type=F32)
    mixed = mixed + jnp.dot(cv_ref[...], wo_ref[SB_WIDTH:, :], preferred_element_type=F32)
    xn = x_ref[0] + mixed
    xn_ref[0] = xn
    ms = jnp.mean(xn * xn, axis=-1, keepdims=True)
    hn = xn * lax.rsqrt(ms + EPS) * g2_ref[...]

    nt = (((1,), (1,)), ((), ()))
    hn_hi = hn.astype(BF16)
    hn_lo = (hn - hn_hi.astype(F32)).astype(BF16)
    both = lax.dot_general(wr_ref[...], hn_hi, nt, preferred_element_type=F32)
    logits = (both[:N_EXPERTS] + both[N_EXPERTS:]
              + lax.dot_general(wr_ref[0:N_EXPERTS, :], hn_lo, nt, preferred_element_type=F32) + br_ref[...])
    eidx = lax.broadcasted_iota(jnp.int32, logits.shape, 0)
    vals, idxs = [], []
    for _ in range(TOP_K):
        m = jnp.max(logits, axis=0, keepdims=True)
        sel = jnp.min(jnp.where(logits == m, eidx, N_EXPERTS), axis=0, keepdims=True)
        vals.append(m)
        idxs.append(sel)
        logits = jnp.where(eidx == sel, -jnp.inf, logits)
    exps = [jnp.exp(v - vals[0]) for v in vals]
    denom = exps[0] + exps[1] + exps[2] + exps[3]
    gate_ref[...] = jnp.concatenate([e / denom for e in exps], axis=0)
    eid_ref[...] = jnp.concatenate(idxs, axis=0)

    onehots = [(eidx == s).astype(F32) for s in idxs]
    chosen = onehots[0] + onehots[1] + onehots[2] + onehots[3]
    tr = lax.broadcasted_iota(jnp.int32, (tm, tm), 0)
    tc = lax.broadcasted_iota(jnp.int32, (tm, tm), 1)
    before = (tr < tc).astype(BF16)
    prefix = jnp.dot(chosen.astype(BF16), before, preferred_element_type=F32) + carry_ref[...]
    ranks = [jnp.sum(o * prefix, axis=0, keepdims=True).astype(jnp.int32) for o in onehots]
    rank_ref[...] = jnp.concatenate(ranks, axis=0)
    before_ref[...] = prefix.astype(jnp.int32)
    carry_ref[...] = carry_ref[...] + jnp.sum(chosen, axis=1, keepdims=True)
    cnt_ref[...] = jnp.broadcast_to(carry_ref[...], cnt_ref.shape)

    _wait_rows(rows_ref, TOP_K * tm, dsem)
    groups = hn.shape[1] // LANES
    for c in range(groups):
        hn3_ref[pl.ds(c, tm, stride=groups), :] = hn[:, c * LANES:(c + 1) * LANES]
    for k in range(TOP_K):
        posv_ref[:, k * tm:(k + 1) * tm] = idxs[k] * cap + ranks[k]
    pos_copy().start()

    @pl.when(step == pl.num_programs(0) * pl.num_programs(1) - 1)
    def _():
        pos_copy().wait()

        def issue(t, _):
            for k in range(TOP_K):
                row_copy(t, k).start(priority=k % 2)
            return 0

        lax.fori_loop(0, tm, issue, 0, unroll=2)
        _wait_rows(rows_ref, TOP_K * tm, dsem)


def _mix(hcv, mhalo, sb, x, conv_w, conv_b, ln_g, ln_b, w_out_bf, g2, wr_t, br, tm, chunk, cap):
    b, s, d = x.shape
    cw = hcv.shape[-1]
    t = b * s
    per = s // tm
    assert d == SUBLANES * LANES, d
    tile = lambda b, i: (b, i, 0)
    fixed = lambda b, i: (0, 0)
    tok = lambda b, i: (0, b * per + i)
    x_rows = N_EXPERTS * cap + TOP_K * tm
    return pl.pallas_call(
        functools.partial(_mix_kernel, tm=tm, chunk=chunk, cap=cap),
        grid=(b, per),
        in_specs=[
            pl.BlockSpec((1, tm, cw), tile),
            pl.BlockSpec((1, HALO, cw), lambda b, i: (b, jnp.maximum(i * (tm // HALO) - 1, 0), 0)),
            pl.BlockSpec((HALO, cw), fixed),
            pl.BlockSpec((1, tm, SB_WIDTH), tile),
            pl.BlockSpec((1, tm, d), tile),
            pl.BlockSpec((CONV_KERNEL, cw), fixed),
            pl.BlockSpec((1, cw), fixed),
            pl.BlockSpec((1, cw), fixed),
            pl.BlockSpec((1, cw), fixed),
            pl.BlockSpec((SB_WIDTH + cw, d), fixed),
            pl.BlockSpec((1, d), fixed),
            pl.BlockSpec((2 * N_EXPERTS, d), fixed),
            pl.BlockSpec((N_EXPERTS, 1), fixed),
        ],
        out_specs=[
            pl.BlockSpec((1, tm, d), tile),
            pl.BlockSpec((TOP_K, tm), tok),
            pl.BlockSpec((TOP_K, tm), tok),
            pl.BlockSpec((TOP_K, tm), tok),
            pl.BlockSpec((N_EXPERTS, tm), tok),
            pl.BlockSpec((N_EXPERTS, LANES), fixed),
            pl.BlockSpec(memory_space=pl.ANY),
        ],
        out_shape=[
            jax.ShapeDtypeStruct((b, s, d), F32),
            jax.ShapeDtypeStruct((TOP_K, t), jnp.int32),
            jax.ShapeDtypeStruct((TOP_K, t), F32),
            jax.ShapeDtypeStruct((TOP_K, t), jnp.int32),
            jax.ShapeDtypeStruct((N_EXPERTS, t), jnp.int32),
            jax.ShapeDtypeStruct((N_EXPERTS, LANES), F32),
            jax.ShapeDtypeStruct((x_rows * SUBLANES, LANES), F32),
        ],
        scratch_shapes=[
            pltpu.VMEM((tm + HALO, cw), F32),
            pltpu.VMEM((chunk, cw), F32),
            pltpu.VMEM((tm, cw), BF16),
            pltpu.VMEM((N_EXPERTS, 1), F32),
            pltpu.VMEM((tm * SUBLANES, LANES), F32),
            pltpu.VMEM((1, TOP_K * tm), jnp.int32),
            pltpu.SMEM((1, TOP_K * tm), jnp.int32),
            pltpu.SemaphoreType.DMA(()),
            pltpu.SemaphoreType.DMA(()),
        ],
        compiler_params=_params(2),
        name="mix_router",
    )(hcv, hcv, mhalo, sb, x, conv_w, conv_b, ln_g, ln_b, w_out_bf, g2, wr_t, br)


def _row_tiles(ref, row, n=1):
    size = n * SUBLANES if isinstance(n, int) else pl.multiple_of(n * SUBLANES, SUBLANES)
    return ref.at[pl.ds(pl.multiple_of(row * SUBLANES, SUBLANES), size)]


def _wait_rows(rows_ref, n, sem):
    pltpu.make_async_copy(_row_tiles(rows_ref, 0, n), _row_tiles(rows_ref, 0, n), sem).wait()


def _padfill_kernel(padlo_ref, padlen_ref, rows_in_ref, rows_ref, zeros_ref, zsem):
    del rows_in_ref
    zeros_ref[...] = jnp.zeros_like(zeros_ref)
    total = jnp.int32(0)
    for e in range(N_EXPERTS):
        lo, ln = padlo_ref[e], padlen_ref[e]
        total = total + ln
        for bit in range(EXPERT_HALF.bit_length() - 1):
            size = 1 << bit

            @pl.when((ln >> bit) & 1 == 1)
            def _():
                start = lo + ((ln >> (bit + 1)) << (bit + 1))
                pltpu.make_async_copy(_row_tiles(zeros_ref, 0, size), _row_tiles(rows_ref, start, size),
                                      zsem).start()

    @pl.when(total > 0)
    def _():
        _wait_rows(rows_ref, total, zsem)


def _padfill(pad_lo, pad_len, x_rows):
    smem = pl.BlockSpec(memory_space=pltpu.SMEM)
    return pl.pallas_call(
        _padfill_kernel,
        in_specs=[smem, smem, pl.BlockSpec(memory_space=pl.ANY)],
        out_specs=pl.BlockSpec(memory_space=pl.ANY),
        out_shape=jax.ShapeDtypeStruct(x_rows.shape, x_rows.dtype),
        scratch_shapes=[pltpu.VMEM((EXPERT_HALF * SUBLANES // 2, LANES), x_rows.dtype), pltpu.SemaphoreType.DMA(())],
        input_output_aliases={2: 0},
        compiler_params=pltpu.CompilerParams(vmem_limit_bytes=VMEM_LIMIT),
        name="padfill",
    )(pad_lo, pad_len, x_rows)


def _expert_kernel(be_ref, bsrc_ref, valid_ref, nused_ref, next_ref, slot_ref,
                   x_ref, wu_hbm, bu_ref, wd_hbm, bd_ref, y_ref,
                   wu_f32, wd_f32, wu_bf, wd_bf, wsem, *, cast_rows):
    del bsrc_ref, nused_ref
    j = pl.program_id(0)
    de = wd_hbm.shape[1]
    valid = valid_ref[j]

    def weight_copies(e, s):
        return (pltpu.make_async_copy(wu_hbm.at[e], wu_f32.at[s], wsem.at[0, s]),
                pltpu.make_async_copy(wd_hbm.at[e], wd_f32.at[s], wsem.at[1, s]))

    def half(h):
        x = jnp.concatenate([x_ref[pl.ds(h * EXPERT_HALF * SUBLANES + c, EXPERT_HALF, stride=SUBLANES), :]
                             for c in range(SUBLANES)], axis=1)
        up = jnp.dot(x.astype(BF16), wu_bf[...], preferred_element_type=F32) + bu_ref[0]
        glu = jnp.minimum(up[:, :de], SWIGLU_LIMIT)
        lin = jnp.clip(up[:, de:], -SWIGLU_LIMIT, SWIGLU_LIMIT)
        act = glu * jax.nn.sigmoid(SWIGLU_ALPHA * glu) * (lin + 1.0)
        y = jnp.dot(act.astype(BF16), wd_bf[...], preferred_element_type=F32) + bd_ref[0]
        y_ref[h * EXPERT_HALF:(h + 1) * EXPERT_HALF, :] = _pack_halves(y)

    @pl.when(valid == 0)
    def _():
        y_ref[...] = jnp.zeros_like(y_ref)

    @pl.when(valid > 0)
    def _():
        prev = be_ref[jnp.maximum(j - 1, 0)]

        @pl.when(jnp.logical_or(j == 0, be_ref[j] != prev))
        def _():
            s = slot_ref[j]

            @pl.when(j == 0)
            def _():
                for c in weight_copies(be_ref[0], s):
                    c.start()

            for c in weight_copies(be_ref[j], s):
                c.wait()

            def cast(c, _):
                r0 = pl.multiple_of(c * cast_rows, cast_rows)
                wu_bf[pl.ds(r0, cast_rows), :] = wu_f32[s, pl.ds(r0, cast_rows), :].astype(BF16)
                wd_bf[pl.ds(r0, cast_rows), :] = wd_f32[s, pl.ds(r0, cast_rows), :].astype(BF16)
                return 0

            lax.fori_loop(0, wu_f32.shape[1] // cast_rows, cast, 0)

            @pl.when(next_ref[j] >= 0)
            def _():
                for c in weight_copies(next_ref[j], 1 - s):
                    c.start()

    @pl.when(valid > EXPERT_HALF)
    def _():
        half(0)
        half(1)

    @pl.when(jnp.logical_and(valid > 0, valid <= EXPERT_HALF))
    def _():
        half(0)
        y_ref[EXPERT_HALF:, :] = jnp.zeros((EXPERT_ROWS - EXPERT_HALF, y_ref.shape[1]), y_ref.dtype)


def _pack_halves(y):
    half = y.shape[1] // 2
    bits = lambda v: lax.bitcast_convert_type(v.astype(BF16).astype(F32), jnp.uint32)
    return (bits(y[:, half:]) & jnp.uint32(0xFFFF0000)) | (bits(y[:, :half]) >> 16)


def _unpack_halves(w):
    lo = lax.bitcast_convert_type(w << 16, F32).astype(BF16)
    hi = lax.bitcast_convert_type(w & jnp.uint32(0xFFFF0000), F32).astype(BF16)
    return lo, hi


def _experts(block_expert, block_src, block_valid, n_used, x_rows, w_up, b_up, w_down, b_down):
    n_blocks = block_expert.shape[0] + 1
    block_valid = jnp.concatenate([block_valid, jnp.zeros((1,), jnp.int32)])
    ne, d, up_cols = w_up.shape
    de = w_down.shape[1]
    steps = jnp.arange(n_blocks - 1, dtype=jnp.int32)
    used = block_valid[:-1] > 0
    first = jnp.logical_and(used, jnp.concatenate([jnp.ones((1,), bool), block_expert[1:] != block_expert[:-1]]))
    slot = (jnp.cumsum(first.astype(jnp.int32)) - 1) % 2
    later_first = jnp.logical_and(first[None, :], steps[None, :] > steps[:, None])
    next_pos = jnp.min(jnp.where(later_first, steps[None, :], n_blocks), axis=1)
    next_expert = jnp.sum(jnp.where(steps[None, :] == next_pos[:, None], block_expert[None, :], 0), axis=1)
    next_expert = jnp.where(next_pos < n_blocks, next_expert, -1).astype(jnp.int32)
    blk = lambda j, be, bs, bv, nu, nx, sl: (bs[jnp.minimum(j, nu[0] - 1)], 0)
    wsel = lambda j, be, bs, bv, nu, nx, sl: (be[jnp.minimum(j, nu[0] - 1)], 0, 0)
    grid_spec = pltpu.PrefetchScalarGridSpec(
        num_scalar_prefetch=6,
        grid=(n_blocks,),
        in_specs=[
            pl.BlockSpec((EXPERT_ROWS * SUBLANES, LANES), blk),
            pl.BlockSpec(memory_space=pl.ANY),
            pl.BlockSpec((1, 1, up_cols), wsel),
            pl.BlockSpec(memory_space=pl.ANY),
            pl.BlockSpec((1, 1, d), wsel),
        ],
        out_specs=pl.BlockSpec((EXPERT_ROWS, d // 2), lambda j, be, bs, bv, nu, nx, sl: (j, 0)),
        scratch_shapes=[pltpu.VMEM((2, d, up_cols), F32), pltpu.VMEM((2, de, d), F32),
                        pltpu.VMEM((d, up_cols), BF16), pltpu.VMEM((de, d), BF16),
                        pltpu.SemaphoreType.DMA((2, 2))],
    )
    return pl.pallas_call(
        functools.partial(_expert_kernel, cast_rows=WEIGHT_CAST_ROWS),
        grid_spec=grid_spec,
        out_shape=jax.ShapeDtypeStruct((n_blocks * EXPERT_ROWS, d // 2), jnp.uint32),
        compiler_params=_params(1),
        name="experts",
    )(block_expert, block_src, block_valid, n_used, jnp.concatenate([next_expert, -jnp.ones((1,), jnp.int32)]),
      jnp.concatenate([slot, jnp.zeros((1,), slot.dtype)]).astype(jnp.int32), x_rows, w_up,
      b_up.reshape(ne, 1, up_cols), w_down, b_down.reshape(ne, 1, d))


def _combine_plan(run_start, run_cnt, seg_start, max_pieces):
    first = run_start // SUBLANES * SUBLANES
    pieces = jnp.where(run_cnt > 0, (run_start - first + run_cnt + PIECE - 1) // PIECE, 0)
    piece_end = jnp.cumsum(pieces, axis=0)
    piece_off = piece_end - pieces
    p = jnp.arange(max_pieces, dtype=jnp.int32)
    mine = jnp.logical_and(p >= piece_off[:, :, None], p < piece_end[:, :, None])
    src = jnp.sum(jnp.where(mine, (first - piece_off * PIECE)[:, :, None], 0), axis=0) + p * PIECE
    shift = (seg_start[:, None] + piece_off * PIECE - first).T
    return (src.reshape(-1).astype(jnp.int32), piece_end[-1].astype(jnp.int32),
            shift.reshape(-1).astype(jnp.int32))


def _to_columns(x):
    pad = jnp.zeros((LANES - x.shape[0], x.shape[1]), x.dtype)
    return jnp.concatenate([x, pad], axis=0).T


def _combine_kernel(src_ref, npieces_ref, shift_ref, eid_ref, rank_ref, gate_ref, xn_ref, rows_ref, o_ref, buf,
                    sems, *, tm, max_pieces):
    i = pl.program_id(0)
    n = pl.num_programs(0)
    slot = lax.rem(i, 2)
    half = o_ref.shape[1] // 2

    def fetch(tile, s):
        def body(p, _):
            src = rows_ref.at[pl.ds(pl.multiple_of(src_ref[tile * max_pieces + p], SUBLANES), PIECE)]
            dst = buf.at[s, pl.ds(pl.multiple_of(p * PIECE, PIECE), PIECE)]
            pltpu.make_async_copy(src, dst, sems.at[s]).start()
            return 0

        lax.fori_loop(0, npieces_ref[tile], body, 0)

    @pl.when(i == 0)
    def _():
        buf[...] = jnp.zeros_like(buf)
        fetch(0, 0)

    @pl.when(i + 1 < n)
    def _():
        fetch(i + 1, 1 - slot)

    total = npieces_ref[i] * PIECE
    eid = eid_ref[...]
    shift = jnp.zeros_like(eid)
    for e in range(N_EXPERTS):
        shift = jnp.where(eid == e, shift_ref[i * N_EXPERTS + e], shift)
    staged = _to_columns((rank_ref[...] + shift).astype(F32)).astype(jnp.int32)
    gates = _to_columns(gate_ref[...])

    @pl.when(total > 0)
    def _():
        rows = pl.ds(0, pl.multiple_of(total, PIECE))
        pltpu.make_async_copy(rows_ref.at[rows], buf.at[slot, rows], sems.at[slot]).wait()

    def chunk_products(c0):
        col = lax.broadcasted_iota(jnp.int32, (tm, COMBINE_CHUNK), 1) + c0
        sel = jnp.zeros((tm, COMBINE_CHUNK), F32)
        for k in range(TOP_K):
            sel = jnp.where(staged[:, k:k + 1] == col, gates[:, k:k + 1], sel)
        sel = sel.astype(BF16)
        lo, hi = _unpack_halves(buf[slot, pl.ds(c0, COMBINE_CHUNK), :])
        return jnp.dot(sel, lo, preferred_element_type=F32), jnp.dot(sel, hi, preferred_element_type=F32)

    xn = xn_ref[...]
    acc_lo, acc_hi = xn[:, :half], xn[:, half:]
    for c in range(COMBINE_STATIC_CHUNKS):
        d_lo, d_hi = chunk_products(c * COMBINE_CHUNK)
        acc_lo, acc_hi = acc_lo + d_lo, acc_hi + d_hi
    o_ref[:, :half] = acc_lo
    o_ref[:, half:] = acc_hi

    def chunk(c, _):
        d_lo, d_hi = chunk_products(pl.multiple_of(c * COMBINE_CHUNK, COMBINE_CHUNK))
        o_ref[:, :half] += d_lo
        o_ref[:, half:] += d_hi
        return 0

    lax.fori_loop(COMBINE_STATIC_CHUNKS, (total + COMBINE_CHUNK - 1) // COMBINE_CHUNK, chunk, 0)


def _combine(run_start, run_cnt, seg_start, eid, rank, gates, xn, y_rows, tm):
    t, d = xn.shape
    n = t // tm
    cap = -(-(TOP_K * tm + 2 * PIECE * N_EXPERTS) // COMBINE_CHUNK) * COMBINE_CHUNK
    max_pieces = cap // PIECE
    src, npieces, shift = _combine_plan(run_start, run_cnt, seg_start, max_pieces)
    tok = lambda i, *_: (i, 0)
    per_k = lambda i, *_: (0, i)
    grid_spec = pltpu.PrefetchScalarGridSpec(
        num_scalar_prefetch=3,
        grid=(n,),
        in_specs=[
            pl.BlockSpec((TOP_K, tm), per_k),
            pl.BlockSpec((TOP_K, tm), per_k),
            pl.BlockSpec((TOP_K, tm), per_k),
            pl.BlockSpec((tm, d), tok),
            pl.BlockSpec(memory_space=pl.ANY),
        ],
        out_specs=pl.BlockSpec((tm, d), tok),
        scratch_shapes=[pltpu.VMEM((2, cap, d // 2), jnp.uint32), pltpu.SemaphoreType.DMA((2,))],
    )
    return pl.pallas_call(
        functools.partial(_combine_kernel, tm=tm, max_pieces=max_pieces),
        grid_spec=grid_spec,
        out_shape=jax.ShapeDtypeStruct((t, d), F32),
        compiler_params=_params(1),
        name="combine",
    )(src, npieces, shift, eid, rank, gates, xn, y_rows)


def _tile(n, want):
    t = min(n, want)
    assert n % t == 0, (n, t)
    return t


def kernel(x, meta_tokens, norm1_g, w_in, q_norm_g, k_norm_g, conv_w, conv_b, conv_ln_g, conv_ln_b,
           attn_out_g, w_out, norm2_g, w_router, b_router, w_up, b_up, w_down, b_down):
    assert norm1_g.shape[0] == 1, "single layer: meta-token rows are only keys/values and conv context"
    b, s, d = x.shape
    t = b * s
    cw = conv_w.shape[-1]

    g1 = norm1_g[0][None, :]
    w_in_bf = w_in[0].astype(BF16)
    qg = jnp.tile(q_norm_g[0], SB_HEADS)[None, :]
    kg = jnp.tile(k_norm_g[0], SB_HEADS)[None, :]

    q, k, v, hcv = _inproj(x.reshape(t, d), g1, w_in_bf, qg, kg, _tile(t, INPROJ_ROWS))
    _, km, vm, hm = _inproj(meta_tokens, g1, w_in_bf, qg, kg, N_META)
    pad = ((0, LANES - N_META), (0, 0))
    tq = _tile(s, ATTN_TILE)
    sb = _attention(q.reshape(b, s, SB_WIDTH), k.reshape(b, s, SB_WIDTH), v.reshape(b, s, SB_WIDTH),
                    jnp.pad(km, pad), jnp.pad(vm, pad), attn_out_g[0].reshape(1, SB_WIDTH), tq,
                    _tile(s, ATTN_TILES_PER_STEP * tq) // tq)

    mhalo = jnp.concatenate([jnp.zeros((HALO - N_META, cw), F32), hm], axis=0)
    tm = _tile(s, MIX_ROWS)
    wr_t = w_router[0].T
    wr_hi = wr_t.astype(BF16)
    wr_lo = (wr_t - wr_hi.astype(F32)).astype(BF16)
    cap = -(-t // EXPERT_ROWS) * EXPERT_ROWS
    xn, eid, gates, rank, before, counts, x_rows = _mix(
        hcv.reshape(b, s, cw), mhalo, sb, x, conv_w[0], conv_b[0][None, :], conv_ln_g[0][None, :],
        conv_ln_b[0][None, :], w_out[0].astype(BF16), norm2_g[0][None, :],
        jnp.concatenate([wr_hi, wr_lo], axis=0), b_router[0][:, None], tm, _tile(tm, CONV_CHUNK), cap)

    counts = counts[:, 0].astype(jnp.int32)
    padded = (counts + EXPERT_ROWS - 1) // EXPERT_ROWS * EXPERT_ROWS
    pad_end = jnp.cumsum(padded)
    pad_start = pad_end - padded
    n_blocks = t * TOP_K // EXPERT_ROWS + N_EXPERTS
    block_row = jnp.arange(n_blocks, dtype=jnp.int32) * EXPERT_ROWS
    block_expert = jnp.minimum(jnp.sum(block_row[:, None] >= pad_end[None, :], axis=1), N_EXPERTS - 1).astype(jnp.int32)
    mine = block_expert[:, None] == jnp.arange(N_EXPERTS, dtype=jnp.int32)
    lookup = lambda table: jnp.sum(jnp.where(mine, table[None, :], 0), axis=1)
    block_off = block_row - lookup(pad_start)
    block_src = (block_expert * cap + jnp.maximum(block_off, 0)) // EXPERT_ROWS
    block_valid = jnp.clip(lookup(counts) - block_off, 0, EXPERT_ROWS)
    block_valid = jnp.where(block_row < pad_end[-1], block_valid, 0)
    n_used = (pad_end[-1:] // EXPERT_ROWS).astype(jnp.int32)
    seg = jnp.arange(N_EXPERTS, dtype=jnp.int32) * cap
    x_rows = _padfill(seg + counts, (-counts) % EXPERT_HALF, x_rows)
    y_rows = _experts(block_expert, block_src.astype(jnp.int32), block_valid.astype(jnp.int32), n_used, x_rows,
                      w_up[0], b_up[0], w_down[0], b_down[0])
    tc = _tile(t, COMBINE_ROWS)
    run_start = pad_start[:, None] + before[:, ::tc]
    run_end = jnp.concatenate([run_start[:, 1:], (pad_start + counts)[:, None]], axis=1)
    out = _combine(run_start, run_end - run_start, pad_start, eid, rank, gates, xn.reshape(t, d), y_rows, tc)
    return out.reshape(b, s, d)
```

```python
import functools

import jax
import jax.numpy as jnp
from jax import lax
from jax.experimental import pallas as pl
from jax.experimental.pallas import tpu as pltpu

N_META = 16
SB_HEADS = 8
SB_HEAD_DIM = 64
SB_WIDTH = SB_HEADS * SB_HEAD_DIM
CONV_KERNEL = 31
N_EXPERTS = 32
TOP_K = 4
SWIGLU_LIMIT = 7.0
SWIGLU_ALPHA = 1.702
EPS = 1e-6
F32_EXP2_UNDERFLOW = -150.0
LOG2_E = 1.4426950408889634

LANES = 128
SUBLANES = 8
INPROJ_ROWS = 1024
ATTN_TILE = 256
ATTN_TILES_PER_STEP = 16
MIX_ROWS = 1024
CONV_CHUNK = 64
COMBINE_ROWS = 256
WEIGHT_CAST_ROWS = 64
PIECE = 16
COMBINE_CHUNK = 512
COMBINE_STATIC_CHUNKS = 3
HALO = 32
EXPERT_HALF = 512
EXPERT_ROWS = 2 * EXPERT_HALF
VMEM_LIMIT = 56 * 1024 * 1024

F32 = jnp.float32
BF16 = jnp.bfloat16


def _params(n_axes, vmem=VMEM_LIMIT):
    return pltpu.CompilerParams(dimension_semantics=("arbitrary",) * n_axes, vmem_limit_bytes=vmem)


def _inproj_kernel(x_ref, g1_ref, w_ref, qg_ref, kg_ref, q_ref, k_ref, v_ref, h_ref):
    x = x_ref[...]
    ms = jnp.mean(x * x, axis=-1, keepdims=True)
    n = (x * lax.rsqrt(ms + EPS) * g1_ref[...]).astype(BF16)
    lo = lax.broadcasted_iota(jnp.int32, (1, LANES), 1) < SB_HEAD_DIM

    def proj(c0, c1):
        return jnp.dot(n, w_ref[:, c0:c1], preferred_element_type=F32)

    def head_norm(acc, g_ref, out_ref, scale):
        for c in range(SB_WIDTH // LANES):
            sl = slice(c * LANES, (c + 1) * LANES)
            a = acc[:, sl]
            sq = a * a
            s_lo = jnp.sum(jnp.where(lo, sq, 0.0), axis=-1, keepdims=True)
            s_hi = jnp.sum(jnp.where(lo, 0.0, sq), axis=-1, keepdims=True)
            r = lax.rsqrt(jnp.where(lo, s_lo, s_hi) * (1.0 / SB_HEAD_DIM) + EPS)
            out_ref[:, sl] = (a * r * (g_ref[:, sl] * scale)).astype(BF16)

    head_norm(proj(0, SB_WIDTH), qg_ref, q_ref, LOG2_E * SB_HEAD_DIM ** -0.5)
    head_norm(proj(SB_WIDTH, 2 * SB_WIDTH), kg_ref, k_ref, 1.0)
    v_ref[...] = proj(2 * SB_WIDTH, 3 * SB_WIDTH).astype(BF16)
    cw = (w_ref.shape[1] - 3 * SB_WIDTH) // 2
    val = proj(3 * SB_WIDTH, 3 * SB_WIDTH + cw)
    gate = proj(3 * SB_WIDTH + cw, 3 * SB_WIDTH + 2 * cw)
    h_ref[...] = val * jax.nn.sigmoid(gate)


def _inproj(x2, g1, w_in_bf, qg, kg, tm):
    t, d = x2.shape
    cols = w_in_bf.shape[1]
    cw = (cols - 3 * SB_WIDTH) // 2
    row = lambda i: (i, 0)
    fixed = lambda i: (0, 0)
    return pl.pallas_call(
        _inproj_kernel,
        grid=(t // tm,),
        in_specs=[
            pl.BlockSpec((tm, d), row),
            pl.BlockSpec((1, d), fixed),
            pl.BlockSpec((d, cols), fixed),
            pl.BlockSpec((1, SB_WIDTH), fixed),
            pl.BlockSpec((1, SB_WIDTH), fixed),
        ],
        out_specs=[
            pl.BlockSpec((tm, SB_WIDTH), row),
            pl.BlockSpec((tm, SB_WIDTH), row),
            pl.BlockSpec((tm, SB_WIDTH), row),
            pl.BlockSpec((tm, cw), row),
        ],
        out_shape=[
            jax.ShapeDtypeStruct((t, SB_WIDTH), BF16),
            jax.ShapeDtypeStruct((t, SB_WIDTH), BF16),
            jax.ShapeDtypeStruct((t, SB_WIDTH), BF16),
            jax.ShapeDtypeStruct((t, cw), F32),
        ],
        compiler_params=_params(1),
        name="inproj",
    )(x2, g1, w_in_bf, qg, kg)


def _attn_kernel(q_ref, k_ref, v_ref, km_ref, vm_ref, g_ref, o_ref, *, tq, nsub):
    i = pl.program_id(2)
    lane = lax.broadcasted_iota(jnp.int32, (1, LANES), 1)
    lo = lane < SB_HEAD_DIM
    zero_bf = jnp.zeros((), BF16)

    def stacked_q(sub):
        q2 = q_ref[0, sub * tq:(sub + 1) * tq, :]
        return jnp.concatenate([jnp.where(lo, q2, zero_bf), jnp.where(lo, zero_bf, q2)], axis=0)

    qss = [stacked_q(sub) for sub in range(nsub)]
    row = lax.broadcasted_iota(jnp.int32, (2 * tq, tq), 0)
    col = lax.broadcasted_iota(jnp.int32, (2 * tq, tq), 1)
    causal = col < jnp.where(row >= tq, row - tq, row)
    srow = lax.broadcasted_iota(jnp.int32, (tq, tq), 0)
    scol = lax.broadcasted_iota(jnp.int32, (tq, tq), 1)
    neg_suffix = jnp.where(srow > scol, -1.0, 0.0).astype(BF16)

    def block(qs, kb, vb, acc, r, mask, neg_suffix_m):
        z = lax.dot_general(qs, kb, (((1,), (1,)), ((), ())), preferred_element_type=F32)
        m = jnp.minimum(z, 0.0)
        p = jnp.maximum(z, 0.0)
        l1p = jnp.log2(1.0 + jnp.exp2(m - p))
        log_beta = m - l1p
        neg_keep = p + l1p
        if mask is not None:
            neg_keep = jnp.where(mask, neg_keep, 0.0)
        later = jnp.dot(neg_keep.astype(BF16), neg_suffix_m, preferred_element_type=F32)
        a = jnp.exp2(log_beta + later + r)
        if mask is not None:
            a = jnp.where(mask, a, 0.0)
        res = jnp.dot(a.astype(BF16), vb, preferred_element_type=F32)
        acc = acc + jnp.where(lo, res[:tq], res[tq:])
        return acc, r - jnp.sum(neg_keep, axis=-1, keepdims=True)

    def kv_block(j):
        start = pl.multiple_of(j * tq, tq)
        return k_ref[0, pl.ds(start, tq), :], v_ref[0, pl.ds(start, tq), :]

    def live(r):
        return jnp.max(r) > F32_EXP2_UNDERFLOW

    acc0 = jnp.zeros((tq, LANES), F32)
    r0 = jnp.zeros((2 * tq, 1), F32)
    first = i * nsub

    def guaranteed(first_has_previous):
        state = []
        for sub in range(nsub):
            acc, r = block(qss[sub], *kv_block(first + sub), acc0, r0, causal, neg_suffix)
            if sub > 0 or first_has_previous:
                acc, r = block(qss[sub], *kv_block(first + sub - 1), acc, r, None, neg_suffix)
            state += [acc, r]
        return tuple(state)

    state = lax.cond(i > 0, lambda _: guaranteed(True), lambda _: guaranteed(False), 0)

    for sub in range(nsub):
        j = first + sub
        qs, acc, r = qss[sub], state[2 * sub], state[2 * sub + 1]

        def cond(c):
            return jnp.logical_and(c[0] <= j, c[3])

        def body(c):
            acc, r = block(qs, *kv_block(j - c[0]), c[1], c[2], None, neg_suffix)
            return c[0] + 1, acc, r, live(r)

        _, acc, r, alive = lax.while_loop(cond, body, (jnp.int32(2), acc, r, live(r)))

        def meta_block(acc):
            return block(qs, km_ref[...], vm_ref[...], acc, r, lane < N_META, neg_suffix[:LANES, :LANES])[0]

        acc = lax.cond(alive, meta_block, lambda acc: acc, acc)

        sq = acc * acc
        s_lo = jnp.sum(jnp.where(lo, sq, 0.0), axis=-1, keepdims=True)
        s_hi = jnp.sum(jnp.where(lo, 0.0, sq), axis=-1, keepdims=True)
        rn = lax.rsqrt(jnp.where(lo, s_lo, s_hi) * (1.0 / SB_HEAD_DIM) + EPS)
        o_ref[0, sub * tq:(sub + 1) * tq, :] = (acc * rn * g_ref[...]).astype(BF16)


def _attention(q, k, v, km, vm, og, tq, nsub):
    b, s, _ = q.shape
    n_pairs = SB_WIDTH // LANES
    step = tq * nsub
    return pl.pallas_call(
        functools.partial(_attn_kernel, tq=tq, nsub=nsub),
        grid=(b, n_pairs, s // step),
        in_specs=[
            pl.BlockSpec((1, step, LANES), lambda b, p, i: (b, i, p)),
            pl.BlockSpec((1, s, LANES), lambda b, p, i: (b, 0, p)),
            pl.BlockSpec((1, s, LANES), lambda b, p, i: (b, 0, p)),
            pl.BlockSpec((LANES, LANES), lambda b, p, i: (0, p)),
            pl.BlockSpec((LANES, LANES), lambda b, p, i: (0, p)),
            pl.BlockSpec((1, LANES), lambda b, p, i: (0, p)),
        ],
        out_specs=pl.BlockSpec((1, step, LANES), lambda b, p, i: (b, i, p)),
        out_shape=jax.ShapeDtypeStruct((b, s, SB_WIDTH), BF16),
        compiler_params=_params(3),
        name="attention",
    )(q, k, v, km, vm, og)


def _mix_kernel(h_ref, halo_ref, mhalo_ref, sb_ref, x_ref, cw_ref, cb_ref, lg_ref, lb_ref, wo_ref,
                g2_ref, wr_ref, br_ref,
                xn_ref, eid_ref, gate_ref, rank_ref, before_ref, cnt_ref, rows_ref,
                win_ref, conv_ref, cv_ref, carry_ref, hn3_ref, posv_ref, pos_ref, dsem, psem, *, tm, chunk, cap):
    b = pl.program_id(0)
    i = pl.program_id(1)
    step = b * pl.num_programs(1) + i

    def pos_copy():
        return pltpu.make_async_copy(posv_ref, pos_ref, psem)

    def row_copy(t, k):
        return pltpu.make_async_copy(_row_tiles(hn3_ref, t), _row_tiles(rows_ref, pos_ref[0, k * tm + t]), dsem)

    @pl.when(step == 0)
    def _():
        carry_ref[...] = jnp.zeros_like(carry_ref)
        hn3_ref[...] = jnp.zeros_like(hn3_ref)
        posv_ref[...] = N_EXPERTS * cap + lax.broadcasted_iota(jnp.int32, posv_ref.shape, 1)
        pos_copy().start()

    pos_copy().wait()

    @pl.when(i == 0)
    def _():
        win_ref[0:HALO, :] = mhalo_ref[...]

    @pl.when(i > 0)
    def _():
        win_ref[0:HALO, :] = halo_ref[0]

    win_ref[HALO:, :] = h_ref[0]

    first_tap = HALO - (CONV_KERNEL - 1)

    def conv_chunk(c, _):
        r0 = pl.multiple_of(c * chunk, chunk)
        rows = chunk + HALO
        for lt in range(cb_ref.shape[1] // LANES):
            sl = slice(lt * LANES, (lt + 1) * LANES)
            window = win_ref[pl.ds(r0, rows), sl]
            acc = jnp.broadcast_to(cb_ref[:, sl], (chunk, LANES))
            for b in range(SUBLANES):
                shifted = window if b == 0 else pltpu.roll(window, shift=rows - b, axis=0)
                for a in range(HALO // SUBLANES + 1):
                    j = SUBLANES * a + b - first_tap
                    if 0 <= j < CONV_KERNEL:
                        acc = acc + cw_ref[j:j + 1, sl] * shifted[SUBLANES * a:SUBLANES * a + chunk, :]
            conv_ref[:, sl] = acc
        for tt in range(chunk):
            for k in range(TOP_K):
                row_copy(r0 + tt, k).start(priority=k % 2)
        acc = conv_ref[...]
        mu = jnp.mean(acc, axis=-1, keepdims=True)
        cen = acc - mu
        var = jnp.mean(cen * cen, axis=-1, keepdims=True)
        y = cen * lax.rsqrt(var + EPS) * lg_ref[...] + lb_ref[...]
        cv_ref[pl.ds(r0, chunk), :] = (y * jax.nn.sigmoid(y)).astype(BF16)
        return 0

    lax.fori_loop(0, tm // chunk, conv_chunk, 0)

    mixed = jnp.dot(sb_ref[0], wo_ref[0:SB_WIDTH, :], preferred_element_type=F32)
    mixed = mixed + jnp.dot(cv_ref[...], wo_ref[SB_WIDTH:, :], preferred_element_type=F32)
    xn = x_ref[0] + mixed
    xn_ref[0] = xn
    ms = jnp.mean(xn * xn, axis=-1, keepdims=True)
    hn = xn * lax.rsqrt(ms + EPS) * g2_ref[...]

    nt = (((1,), (1,)), ((), ()))
    hn_hi = hn.astype(BF16)
    hn_lo = (hn - hn_hi.astype(F32)).astype(BF16)
    both = lax.dot_general(wr_ref[...], hn_hi, nt, preferred_element_type=F32)
    logits = (both[:N_EXPERTS] + both[N_EXPERTS:]
              + lax.dot_general(wr_ref[0:N_EXPERTS, :], hn_lo, nt, preferred_element_type=F32) + br_ref[...])
    eidx = lax.broadcasted_iota(jnp.int32, logits.shape, 0)
    vals, idxs = [], []
    for _ in range(TOP_K):
        m = jnp.max(logits, axis=0, keepdims=True)
        sel = jnp.min(jnp.where(logits == m, eidx, N_EXPERTS), axis=0, keepdims=True)
        vals.append(m)
        idxs.append(sel)
        logits = jnp.where(eidx == sel, -jnp.inf, logits)
    exps = [jnp.exp(v - vals[0]) for v in vals]
    denom = exps[0] + exps[1] + exps[2] + exps[3]
    gate_ref[...] = jnp.concatenate([e / denom for e in exps], axis=0)
    eid_ref[...] = jnp.concatenate(idxs, axis=0)

    onehots = [(eidx == s).astype(F32) for s in idxs]
    chosen = onehots[0] + onehots[1] + onehots[2] + onehots[3]
    tr = lax.broadcasted_iota(jnp.int32, (tm, tm), 0)
    tc = lax.broadcasted_iota(jnp.int32, (tm, tm), 1)
    before = (tr < tc).astype(BF16)
    prefix = jnp.dot(chosen.astype(BF16), before, preferred_element_type=F32) + carry_ref[...]
    ranks = [jnp.sum(o * prefix, axis=0, keepdims=True).astype(jnp.int32) for o in onehots]
    rank_ref[...] = jnp.concatenate(ranks, axis=0)
    before_ref[...] = prefix.astype(jnp.int32)
    carry_ref[...] = carry_ref[...] + jnp.sum(chosen, axis=1, keepdims=True)
    cnt_ref[...] = jnp.broadcast_to(carry_ref[...], cnt_ref.shape)

    _wait_rows(rows_ref, TOP_K * tm, dsem)
    groups = hn.shape[1] // LANES
    for c in range(groups):
        hn3_ref[pl.ds(c, tm, stride=groups), :] = hn[:, c * LANES:(c + 1) * LANES]
    for k in range(TOP_K):
        posv_ref[:, k * tm:(k + 1) * tm] = idxs[k] * cap + ranks[k]
    pos_copy().start()

    @pl.when(step == pl.num_programs(0) * pl.num_programs(1) - 1)
    def _():
        pos_copy().wait()

        def issue(t, _):
            for k in range(TOP_K):
                row_copy(t, k).start(priority=k % 2)
            return 0

        lax.fori_loop(0, tm, issue, 0, unroll=2)
        _wait_rows(rows_ref, TOP_K * tm, dsem)


def _mix(hcv, mhalo, sb, x, conv_w, conv_b, ln_g, ln_b, w_out_bf, g2, wr_t, br, tm, chunk, cap):
    b, s, d = x.shape
    cw = hcv.shape[-1]
    t = b * s
    per = s // tm
    assert d == SUBLANES * LANES, d
    tile = lambda b, i: (b, i, 0)
    fixed = lambda b, i: (0, 0)
    tok = lambda b, i: (0, b * per + i)
    x_rows = N_EXPERTS * cap + TOP_K * tm
    return pl.pallas_call(
        functools.partial(_mix_kernel, tm=tm, chunk=chunk, cap=cap),
        grid=(b, per),
        in_specs=[
            pl.BlockSpec((1, tm, cw), tile),
            pl.BlockSpec((1, HALO, cw), lambda b, i: (b, jnp.maximum(i * (tm // HALO) - 1, 0), 0)),
            pl.BlockSpec((HALO, cw), fixed),
            pl.BlockSpec((1, tm, SB_WIDTH), tile),
            pl.BlockSpec((1, tm, d), tile),
            pl.BlockSpec((CONV_KERNEL, cw), fixed),
            pl.BlockSpec((1, cw), fixed),
            pl.BlockSpec((1, cw), fixed),
            pl.BlockSpec((1, cw), fixed),
            pl.BlockSpec((SB_WIDTH + cw, d), fixed),
            pl.BlockSpec((1, d), fixed),
            pl.BlockSpec((2 * N_EXPERTS, d), fixed),
            pl.BlockSpec((N_EXPERTS, 1), fixed),
        ],
        out_specs=[
            pl.BlockSpec((1, tm, d), tile),
            pl.BlockSpec((TOP_K, tm), tok),
            pl.BlockSpec((TOP_K, tm), tok),
            pl.BlockSpec((TOP_K, tm), tok),
            pl.BlockSpec((N_EXPERTS, tm), tok),
            pl.BlockSpec((N_EXPERTS, LANES), fixed),
            pl.BlockSpec(memory_space=pl.ANY),
        ],
        out_shape=[
            jax.ShapeDtypeStruct((b, s, d), F32),
            jax.ShapeDtypeStruct((TOP_K, t), jnp.int32),
            jax.ShapeDtypeStruct((TOP_K, t), F32),
            jax.ShapeDtypeStruct((TOP_K, t), jnp.int32),
            jax.ShapeDtypeStruct((N_EXPERTS, t), jnp.int32),
            jax.ShapeDtypeStruct((N_EXPERTS, LANES), F32),
            jax.ShapeDtypeStruct((x_rows * SUBLANES, LANES), F32),
        ],
        scratch_shapes=[
            pltpu.VMEM((tm + HALO, cw), F32),
            pltpu.VMEM((chunk, cw), F32),
            pltpu.VMEM((tm, cw), BF16),
            pltpu.VMEM((N_EXPERTS, 1), F32),
            pltpu.VMEM((tm * SUBLANES, LANES), F32),
            pltpu.VMEM((1, TOP_K * tm), jnp.int32),
            pltpu.SMEM((1, TOP_K * tm), jnp.int32),
            pltpu.SemaphoreType.DMA(()),
            pltpu.SemaphoreType.DMA(()),
        ],
        compiler_params=_params(2),
        name="mix_router",
    )(hcv, hcv, mhalo, sb, x, conv_w, conv_b, ln_g, ln_b, w_out_bf, g2, wr_t, br)


def _row_tiles(ref, row, n=1):
    size = n * SUBLANES if isinstance(n, int) else pl.multiple_of(n * SUBLANES, SUBLANES)
    return ref.at[pl.ds(pl.multiple_of(row * SUBLANES, SUBLANES), size)]


def _wait_rows(rows_ref, n, sem):
    pltpu.make_async_copy(_row_tiles(rows_ref, 0, n), _row_tiles(rows_ref, 0, n), sem).wait()


def _padfill_kernel(padlo_ref, padlen_ref, rows_in_ref, rows_ref, zeros_ref, zsem):
    del rows_in_ref
    zeros_ref[...] = jnp.zeros_like(zeros_ref)
    total = jnp.int32(0)
    for e in range(N_EXPERTS):
        lo, ln = padlo_ref[e], padlen_ref[e]
        total = total + ln
        for bit in range(EXPERT_HALF.bit_length() - 1):
            size = 1 << bit

            @pl.when((ln >> bit) & 1 == 1)
            def _():
                start = lo + ((ln >> (bit + 1)) << (bit + 1))
                pltpu.make_async_copy(_row_tiles(zeros_ref, 0, size), _row_tiles(rows_ref, start, size),
                                      zsem).start()

    @pl.when(total > 0)
    def _():
        _wait_rows(rows_ref, total, zsem)


def _padfill(pad_lo, pad_len, x_rows):
    smem = pl.BlockSpec(memory_space=pltpu.SMEM)
    return pl.pallas_call(
        _padfill_kernel,
        in_specs=[smem, smem, pl.BlockSpec(memory_space=pl.ANY)],
        out_specs=pl.BlockSpec(memory_space=pl.ANY),
        out_shape=jax.ShapeDtypeStruct(x_rows.shape, x_rows.dtype),
        scratch_shapes=[pltpu.VMEM((EXPERT_HALF * SUBLANES // 2, LANES), x_rows.dtype), pltpu.SemaphoreType.DMA(())],
        input_output_aliases={2: 0},
        compiler_params=pltpu.CompilerParams(vmem_limit_bytes=VMEM_LIMIT),
        name="padfill",
    )(pad_lo, pad_len, x_rows)


def _expert_kernel(be_ref, bsrc_ref, valid_ref, nused_ref, next_ref, slot_ref,
                   x_ref, wu_hbm, bu_ref, wd_hbm, bd_ref, y_ref,
                   wu_f32, wd_f32, wu_bf, wd_bf, wsem, *, cast_rows):
    del bsrc_ref, nused_ref
    j = pl.program_id(0)
    de = wd_hbm.shape[1]
    valid = valid_ref[j]

    def weight_copies(e, s):
        return (pltpu.make_async_copy(wu_hbm.at[e], wu_f32.at[s], wsem.at[0, s]),
                pltpu.make_async_copy(wd_hbm.at[e], wd_f32.at[s], wsem.at[1, s]))

    def half(h):
        x = jnp.concatenate([x_ref[pl.ds(h * EXPERT_HALF * SUBLANES + c, EXPERT_HALF, stride=SUBLANES), :]
                             for c in range(SUBLANES)], axis=1)
        up = jnp.dot(x.astype(BF16), wu_bf[...], preferred_element_type=F32) + bu_ref[0]
        glu = jnp.minimum(up[:, :de], SWIGLU_LIMIT)
        lin = jnp.clip(up[:, de:], -SWIGLU_LIMIT, SWIGLU_LIMIT)
        act = glu * jax.nn.sigmoid(SWIGLU_ALPHA * glu) * (lin + 1.0)
        y = jnp.dot(act.astype(BF16), wd_bf[...], preferred_element_type=F32) + bd_ref[0]
        y_ref[h * EXPERT_HALF:(h + 1) * EXPERT_HALF, :] = _pack_halves(y)

    @pl.when(valid == 0)
    def _():
        y_ref[...] = jnp.zeros_like(y_ref)

    @pl.when(valid > 0)
    def _():
        prev = be_ref[jnp.maximum(j - 1, 0)]

        @pl.when(jnp.logical_or(j == 0, be_ref[j] != prev))
        def _():
            s = slot_ref[j]

            @pl.when(j == 0)
            def _():
                for c in weight_copies(be_ref[0], s):
                    c.start()

            for c in weight_copies(be_ref[j], s):
                c.wait()

            def cast(c, _):
                r0 = pl.multiple_of(c * cast_rows, cast_rows)
                wu_bf[pl.ds(r0, cast_rows), :] = wu_f32[s, pl.ds(r0, cast_rows), :].astype(BF16)
                wd_bf[pl.ds(r0, cast_rows), :] = wd_f32[s, pl.ds(r0, cast_rows), :].astype(BF16)
                return 0

            lax.fori_loop(0, wu_f32.shape[1] // cast_rows, cast, 0)

            @pl.when(next_ref[j] >= 0)
            def _():
                for c in weight_copies(next_ref[j], 1 - s):
                    c.start()

    @pl.when(valid > EXPERT_HALF)
    def _():
        half(0)
        half(1)

    @pl.when(jnp.logical_and(valid > 0, valid <= EXPERT_HALF))
    def _():
        half(0)
        y_ref[EXPERT_HALF:, :] = jnp.zeros((EXPERT_ROWS - EXPERT_HALF, y_ref.shape[1]), y_ref.dtype)


def _pack_halves(y):
    half = y.shape[1] // 2
    bits = lambda v: lax.bitcast_convert_type(v.astype(BF16).astype(F32), jnp.uint32)
    return (bits(y[:, half:]) & jnp.uint32(0xFFFF0000)) | (bits(y[:, :half]) >> 16)


def _unpack_halves(w):
    lo = lax.bitcast_convert_type(w << 16, F32).astype(BF16)
    hi = lax.bitcast_convert_type(w & jnp.uint32(0xFFFF0000), F32).astype(BF16)
    return lo, hi


def _experts(block_expert, block_src, block_valid, n_used, x_rows, w_up, b_up, w_down, b_down):
    n_blocks = block_expert.shape[0] + 1
    block_valid = jnp.concatenate([block_valid, jnp.zeros((1,), jnp.int32)])
    ne, d, up_cols = w_up.shape
    de = w_down.shape[1]
    steps = jnp.arange(n_blocks - 1, dtype=jnp.int32)
    used = block_valid[:-1] > 0
    first = jnp.logical_and(used, jnp.concatenate([jnp.ones((1,), bool), block_expert[1:] != block_expert[:-1]]))
    slot = (jnp.cumsum(first.astype(jnp.int32)) - 1) % 2
    later_first = jnp.logical_and(first[None, :], steps[None, :] > steps[:, None])
    next_pos = jnp.min(jnp.where(later_first, steps[None, :], n_blocks), axis=1)
    next_expert = jnp.sum(jnp.where(steps[None, :] == next_pos[:, None], block_expert[None, :], 0), axis=1)
    next_expert = jnp.where(next_pos < n_blocks, next_expert, -1).astype(jnp.int32)
    blk = lambda j, be, bs, bv, nu, nx, sl: (bs[jnp.minimum(j, nu[0] - 1)], 0)
    wsel = lambda j, be, bs, bv, nu, nx, sl: (be[jnp.minimum(j, nu[0] - 1)], 0, 0)
    grid_spec = pltpu.PrefetchScalarGridSpec(
        num_scalar_prefetch=6,
        grid=(n_blocks,),
        in_specs=[
            pl.BlockSpec((EXPERT_ROWS * SUBLANES, LANES), blk),
            pl.BlockSpec(memory_space=pl.ANY),
            pl.BlockSpec((1, 1, up_cols), wsel),
            pl.BlockSpec(memory_space=pl.ANY),
            pl.BlockSpec((1, 1, d), wsel),
        ],
        out_specs=pl.BlockSpec((EXPERT_ROWS, d // 2), lambda j, be, bs, bv, nu, nx, sl: (j, 0)),
        scratch_shapes=[pltpu.VMEM((2, d, up_cols), F32), pltpu.VMEM((2, de, d), F32),
                        pltpu.VMEM((d, up_cols), BF16), pltpu.VMEM((de, d), BF16),
                        pltpu.SemaphoreType.DMA((2, 2))],
    )
    return pl.pallas_call(
        functools.partial(_expert_kernel, cast_rows=WEIGHT_CAST_ROWS),
        grid_spec=grid_spec,
        out_shape=jax.ShapeDtypeStruct((n_blocks * EXPERT_ROWS, d // 2), jnp.uint32),
        compiler_params=_params(1),
        name="experts",
    )(block_expert, block_src, block_valid, n_used, jnp.concatenate([next_expert, -jnp.ones((1,), jnp.int32)]),
      jnp.concatenate([slot, jnp.zeros((1,), slot.dtype)]).astype(jnp.int32), x_rows, w_up,
      b_up.reshape(ne, 1, up_cols), w_down, b_down.reshape(ne, 1, d))


def _combine_plan(run_start, run_cnt, seg_start, max_pieces):
    first = run_start // SUBLANES * SUBLANES
    pieces = jnp.where(run_cnt > 0, (run_start - first + run_cnt + PIECE - 1) // PIECE, 0)
    piece_end = jnp.cumsum(pieces, axis=0)
    piece_off = piece_end - pieces
    p = jnp.arange(max_pieces, dtype=jnp.int32)
    mine = jnp.logical_and(p >= piece_off[:, :, None], p < piece_end[:, :, None])
    src = jnp.sum(jnp.where(mine, (first - piece_off * PIECE)[:, :, None], 0), axis=0) + p * PIECE
    shift = (seg_start[:, None] + piece_off * PIECE - first).T
    return (src.reshape(-1).astype(jnp.int32), piece_end[-1].astype(jnp.int32),
            shift.reshape(-1).astype(jnp.int32))


def _to_columns(x):
    pad = jnp.zeros((LANES - x.shape[0], x.shape[1]), x.dtype)
    return jnp.concatenate([x, pad], axis=0).T


def _combine_kernel(src_ref, npieces_ref, shift_ref, eid_ref, rank_ref, gate_ref, xn_ref, rows_ref, o_ref, buf,
                    sems, *, tm, max_pieces):
    i = pl.program_id(0)
    n = pl.num_programs(0)
    slot = lax.rem(i, 2)
    half = o_ref.shape[1] // 2

    def fetch(tile, s):
        def body(p, _):
            src = rows_ref.at[pl.ds(pl.multiple_of(src_ref[tile * max_pieces + p], SUBLANES), PIECE)]
            dst = buf.at[s, pl.ds(pl.multiple_of(p * PIECE, PIECE), PIECE)]
            pltpu.make_async_copy(src, dst, sems.at[s]).start()
            return 0

        lax.fori_loop(0, npieces_ref[tile], body, 0)

    @pl.when(i == 0)
    def _():
        buf[...] = jnp.zeros_like(buf)
        fetch(0, 0)

    @pl.when(i + 1 < n)
    def _():
        fetch(i + 1, 1 - slot)

    total = npieces_ref[i] * PIECE
    eid = eid_ref[...]
    shift = jnp.zeros_like(eid)
    for e in range(N_EXPERTS):
        shift = jnp.where(eid == e, shift_ref[i * N_EXPERTS + e], shift)
    staged = _to_columns((rank_ref[...] + shift).astype(F32)).astype(jnp.int32)
    gates = _to_columns(gate_ref[...])

    @pl.when(total > 0)
    def _():
        rows = pl.ds(0, pl.multiple_of(total, PIECE))
        pltpu.make_async_copy(rows_ref.at[rows], buf.at[slot, rows], sems.at[slot]).wait()

    def chunk_products(c0):
        col = lax.broadcasted_iota(jnp.int32, (tm, COMBINE_CHUNK), 1) + c0
        sel = jnp.zeros((tm, COMBINE_CHUNK), F32)
        for k in range(TOP_K):
            sel = jnp.where(staged[:, k:k + 1] == col, gates[:, k:k + 1], sel)
        sel = sel.astype(BF16)
        lo, hi = _unpack_halves(buf[slot, pl.ds(c0, COMBINE_CHUNK), :])
        return jnp.dot(sel, lo, preferred_element_type=F32), jnp.dot(sel, hi, preferred_element_type=F32)

    xn = xn_ref[...]
    acc_lo, acc_hi = xn[:, :half], xn[:, half:]
    for c in range(COMBINE_STATIC_CHUNKS):
        d_lo, d_hi = chunk_products(c * COMBINE_CHUNK)
        acc_lo, acc_hi = acc_lo + d_lo, acc_hi + d_hi
    o_ref[:, :half] = acc_lo
    o_ref[:, half:] = acc_hi

    def chunk(c, _):
        d_lo, d_hi = chunk_products(pl.multiple_of(c * COMBINE_CHUNK, COMBINE_CHUNK))
        o_ref[:, :half] += d_lo
        o_ref[:, half:] += d_hi
        return 0

    lax.fori_loop(COMBINE_STATIC_CHUNKS, (total + COMBINE_CHUNK - 1) // COMBINE_CHUNK, chunk, 0)


def _combine(run_start, run_cnt, seg_start, eid, rank, gates, xn, y_rows, tm):
    t, d = xn.shape
    n = t // tm
    cap = -(-(TOP_K * tm + 2 * PIECE * N_EXPERTS) // COMBINE_CHUNK) * COMBINE_CHUNK
    max_pieces = cap // PIECE
    src, npieces, shift = _combine_plan(run_start, run_cnt, seg_start, max_pieces)
    tok = lambda i, *_: (i, 0)
    per_k = lambda i, *_: (0, i)
    grid_spec = pltpu.PrefetchScalarGridSpec(
        num_scalar_prefetch=3,
        grid=(n,),
        in_specs=[
            pl.BlockSpec((TOP_K, tm), per_k),
            pl.BlockSpec((TOP_K, tm), per_k),
            pl.BlockSpec((TOP_K, tm), per_k),
            pl.BlockSpec((tm, d), tok),
            pl.BlockSpec(memory_space=pl.ANY),
        ],
        out_specs=pl.BlockSpec((tm, d), tok),
        scratch_shapes=[pltpu.VMEM((2, cap, d // 2), jnp.uint32), pltpu.SemaphoreType.DMA((2,))],
    )
    return pl.pallas_call(
        functools.partial(_combine_kernel, tm=tm, max_pieces=max_pieces),
        grid_spec=grid_spec,
        out_shape=jax.ShapeDtypeStruct((t, d), F32),
        compiler_params=_params(1),
        name="combine",
    )(src, npieces, shift, eid, rank, gates, xn, y_rows)


def _tile(n, want):
    t = min(n, want)
    assert n % t == 0, (n, t)
    return t


def kernel(x, meta_tokens, norm1_g, w_in, q_norm_g, k_norm_g, conv_w, conv_b, conv_ln_g, conv_ln_b,
           attn_out_g, w_out, norm2_g, w_router, b_router, w_up, b_up, w_down, b_down):
    assert norm1_g.shape[0] == 1, "single layer: meta-token rows are only keys/values and conv context"
    b, s, d = x.shape
    t = b * s
    cw = conv_w.shape[-1]

    g1 = norm1_g[0][None, :]
    w_in_bf = w_in[0].astype(BF16)
    qg = jnp.tile(q_norm_g[0], SB_HEADS)[None, :]
    kg = jnp.tile(k_norm_g[0], SB_HEADS)[None, :]

    q, k, v, hcv = _inproj(x.reshape(t, d), g1, w_in_bf, qg, kg, _tile(t, INPROJ_ROWS))
    _, km, vm, hm = _inproj(meta_tokens, g1, w_in_bf, qg, kg, N_META)
    pad = ((0, LANES - N_META), (0, 0))
    tq = _tile(s, ATTN_TILE)
    sb = _attention(q.reshape(b, s, SB_WIDTH), k.reshape(b, s, SB_WIDTH), v.reshape(b, s, SB_WIDTH),
                    jnp.pad(km, pad), jnp.pad(vm, pad), attn_out_g[0].reshape(1, SB_WIDTH), tq,
                    _tile(s, ATTN_TILES_PER_STEP * tq) // tq)

    mhalo = jnp.concatenate([jnp.zeros((HALO - N_META, cw), F32), hm], axis=0)
    tm = _tile(s, MIX_ROWS)
    wr_t = w_router[0].T
    wr_hi = wr_t.astype(BF16)
    wr_lo = (wr_t - wr_hi.astype(F32)).astype(BF16)
    cap = -(-t // EXPERT_ROWS) * EXPERT_ROWS
    xn, eid, gates, rank, before, counts, x_rows = _mix(
        hcv.reshape(b, s, cw), mhalo, sb, x, conv_w[0], conv_b[0][None, :], conv_ln_g[0][None, :],
        conv_ln_b[0][None, :], w_out[0].astype(BF16), norm2_g[0][None, :],
        jnp.concatenate([wr_hi, wr_lo], axis=0), b_router[0][:, None], tm, _tile(tm, CONV_CHUNK), cap)

    counts = counts[:, 0].astype(jnp.int32)
    padded = (counts + EXPERT_ROWS - 1) // EXPERT_ROWS * EXPERT_ROWS
    pad_end = jnp.cumsum(padded)
    pad_start = pad_end - padded
    n_blocks = t * TOP_K // EXPERT_ROWS + N_EXPERTS
    block_row = jnp.arange(n_blocks, dtype=jnp.int32) * EXPERT_ROWS
    block_expert = jnp.minimum(jnp.sum(block_row[:, None] >= pad_end[None, :], axis=1), N_EXPERTS - 1).astype(jnp.int32)
    mine = block_expert[:, None] == jnp.arange(N_EXPERTS, dtype=jnp.int32)
    lookup = lambda table: jnp.sum(jnp.where(mine, table[None, :], 0), axis=1)
    block_off = block_row - lookup(pad_start)
    block_src = (block_expert * cap + jnp.maximum(block_off, 0)) // EXPERT_ROWS
    block_valid = jnp.clip(lookup(counts) - block_off, 0, EXPERT_ROWS)
    block_valid = jnp.where(block_row < pad_end[-1], block_valid, 0)
    n_used = (pad_end[-1:] // EXPERT_ROWS).astype(jnp.int32)
    seg = jnp.arange(N_EXPERTS, dtype=jnp.int32) * cap
    x_rows = _padfill(seg + counts, (-counts) % EXPERT_HALF, x_rows)
    y_rows = _experts(block_expert, block_src.astype(jnp.int32), block_valid.astype(jnp.int32), n_used, x_rows,
                      w_up[0], b_up[0], w_down[0], b_down[0])
    tc = _tile(t, COMBINE_ROWS)
    run_start = pad_start[:, None] + before[:, ::tc]
    run_end = jnp.concatenate([run_start[:, 1:], (pad_start + counts)[:, None]], axis=1)
    out = _combine(run_start, run_end - run_start, pad_start, eid, rank, gates, xn.reshape(t, d), y_rows, tc)
    return out.reshape(b, s, d)
```

```python
import functools

import jax
import jax.numpy as jnp
from jax import lax
from jax.experimental import pallas as pl
from jax.experimental.pallas import tpu as pltpu

N_META = 16
SB_HEADS = 8
SB_HEAD_DIM = 64
SB_WIDTH = SB_HEADS * SB_HEAD_DIM
CONV_KERNEL = 31
N_EXPERTS = 32
TOP_K = 4
SWIGLU_LIMIT = 7.0
SWIGLU_ALPHA = 1.702
EPS = 1e-6
F32_EXP2_UNDERFLOW = -150.0
LOG2_E = 1.4426950408889634

LANES = 128
SUBLANES = 8
INPROJ_ROWS = 1024
ATTN_TILE = 256
ATTN_TILES_PER_STEP = 16
ATTN_PAIRS_PER_STEP = 2
MIX_ROWS = 1024
CONV_CHUNK = 64
COMBINE_ROWS = 256
WEIGHT_CAST_ROWS = 64
PIECE = 16
COMBINE_CHUNK = 512
COMBINE_STATIC_CHUNKS = 3
HALO = 32
EXPERT_HALF = 512
EXPERT_ROWS = 2 * EXPERT_HALF
VMEM_LIMIT = 56 * 1024 * 1024

F32 = jnp.float32
BF16 = jnp.bfloat16


def _params(n_axes, vmem=VMEM_LIMIT):
    return pltpu.CompilerParams(dimension_semantics=("arbitrary",) * n_axes, vmem_limit_bytes=vmem)


def _inproj_kernel(x_ref, g1_ref, w_ref, qg_ref, kg_ref, q_ref, k_ref, v_ref, h_ref):
    x = x_ref[...]
    ms = jnp.mean(x * x, axis=-1, keepdims=True)
    n = (x * lax.rsqrt(ms + EPS) * g1_ref[...]).astype(BF16)
    lo = lax.broadcasted_iota(jnp.int32, (1, LANES), 1) < SB_HEAD_DIM

    def proj(c0, c1):
        return jnp.dot(n, w_ref[:, c0:c1], preferred_element_type=F32)

    def head_norm(acc, g_ref, out_ref, scale):
        for c in range(SB_WIDTH // LANES):
            sl = slice(c * LANES, (c + 1) * LANES)
            a = acc[:, sl]
            sq = a * a
            s_lo = jnp.sum(jnp.where(lo, sq, 0.0), axis=-1, keepdims=True)
            s_hi = jnp.sum(jnp.where(lo, 0.0, sq), axis=-1, keepdims=True)
            r = lax.rsqrt(jnp.where(lo, s_lo, s_hi) * (1.0 / SB_HEAD_DIM) + EPS)
            out_ref[:, sl] = (a * r * (g_ref[:, sl] * scale)).astype(BF16)

    head_norm(proj(0, SB_WIDTH), qg_ref, q_ref, LOG2_E * SB_HEAD_DIM ** -0.5)
    head_norm(proj(SB_WIDTH, 2 * SB_WIDTH), kg_ref, k_ref, 1.0)
    v_ref[...] = proj(2 * SB_WIDTH, 3 * SB_WIDTH).astype(BF16)
    cw = (w_ref.shape[1] - 3 * SB_WIDTH) // 2
    val = proj(3 * SB_WIDTH, 3 * SB_WIDTH + cw)
    gate = proj(3 * SB_WIDTH + cw, 3 * SB_WIDTH + 2 * cw)
    h_ref[...] = val * jax.nn.sigmoid(gate)


def _inproj(x2, g1, w_in_bf, qg, kg, tm):
    t, d = x2.shape
    cols = w_in_bf.shape[1]
    cw = (cols - 3 * SB_WIDTH) // 2
    row = lambda i: (i, 0)
    fixed = lambda i: (0, 0)
    return pl.pallas_call(
        _inproj_kernel,
        grid=(t // tm,),
        in_specs=[
            pl.BlockSpec((tm, d), row),
            pl.BlockSpec((1, d), fixed),
            pl.BlockSpec((d, cols), fixed),
            pl.BlockSpec((1, SB_WIDTH), fixed),
            pl.BlockSpec((1, SB_WIDTH), fixed),
        ],
        out_specs=[
            pl.BlockSpec((tm, SB_WIDTH), row),
            pl.BlockSpec((tm, SB_WIDTH), row),
            pl.BlockSpec((tm, SB_WIDTH), row),
            pl.BlockSpec((tm, cw), row),
        ],
        out_shape=[
            jax.ShapeDtypeStruct((t, SB_WIDTH), BF16),
            jax.ShapeDtypeStruct((t, SB_WIDTH), BF16),
            jax.ShapeDtypeStruct((t, SB_WIDTH), BF16),
            jax.ShapeDtypeStruct((t, cw), F32),
        ],
        compiler_params=_params(1),
        name="inproj",
    )(x2, g1, w_in_bf, qg, kg)


def _attn_kernel(q_ref, k_ref, v_ref, km_ref, vm_ref, g_ref, o_ref, *, tq, nsub, npair):
    i = pl.program_id(2)
    lane = lax.broadcasted_iota(jnp.int32, (1, LANES), 1)
    lo = lane < SB_HEAD_DIM
    zero_bf = jnp.zeros((), BF16)
    units = [(pp, sub) for pp in range(npair) for sub in range(nsub)]
    pair_lanes = lambda pp: slice(pp * LANES, (pp + 1) * LANES)

    def stacked_q(pp, sub):
        q2 = q_ref[0, sub * tq:(sub + 1) * tq, pair_lanes(pp)]
        return jnp.concatenate([jnp.where(lo, q2, zero_bf), jnp.where(lo, zero_bf, q2)], axis=0)

    qss = [stacked_q(pp, sub) for pp, sub in units]
    row = lax.broadcasted_iota(jnp.int32, (2 * tq, tq), 0)
    col = lax.broadcasted_iota(jnp.int32, (2 * tq, tq), 1)
    causal = col < jnp.where(row >= tq, row - tq, row)
    srow = lax.broadcasted_iota(jnp.int32, (tq, tq), 0)
    scol = lax.broadcasted_iota(jnp.int32, (tq, tq), 1)
    neg_suffix = jnp.where(srow > scol, -1.0, 0.0).astype(BF16)

    def block(qs, kb, vb, acc, r, mask, neg_suffix_m):
        z = lax.dot_general(qs, kb, (((1,), (1,)), ((), ())), preferred_element_type=F32)
        m = jnp.minimum(z, 0.0)
        p = jnp.maximum(z, 0.0)
        l1p = jnp.log2(1.0 + jnp.exp2(m - p))
        log_beta = m - l1p
        neg_keep = p + l1p
        if mask is not None:
            neg_keep = jnp.where(mask, neg_keep, 0.0)
        later = jnp.dot(neg_keep.astype(BF16), neg_suffix_m, preferred_element_type=F32)
        a = jnp.exp2(log_beta + later + r)
        if mask is not None:
            a = jnp.where(mask, a, 0.0)
        res = jnp.dot(a.astype(BF16), vb, preferred_element_type=F32)
        acc = acc + jnp.where(lo, res[:tq], res[tq:])
        return acc, r - jnp.sum(neg_keep, axis=-1, keepdims=True)

    def kv_block(pp, j):
        start = pl.multiple_of(j * tq, tq)
        return k_ref[0, pl.ds(start, tq), pair_lanes(pp)], v_ref[0, pl.ds(start, tq), pair_lanes(pp)]

    def live(r):
        return jnp.max(r) > F32_EXP2_UNDERFLOW

    acc0 = jnp.zeros((tq, LANES), F32)
    r0 = jnp.zeros((2 * tq, 1), F32)
    first = i * nsub

    def guaranteed(first_has_previous):
        state = []
        for u, (pp, sub) in enumerate(units):
            acc, r = block(qss[u], *kv_block(pp, first + sub), acc0, r0, causal, neg_suffix)
            if sub > 0 or first_has_previous:
                acc, r = block(qss[u], *kv_block(pp, first + sub - 1), acc, r, None, neg_suffix)
            state += [acc, r]
        return tuple(state)

    state = lax.cond(i > 0, lambda _: guaranteed(True), lambda _: guaranteed(False), 0)

    for u, (pp, sub) in enumerate(units):
        j = first + sub
        qs, acc, r = qss[u], state[2 * u], state[2 * u + 1]

        def cond(c):
            return jnp.logical_and(c[0] <= j, c[3])

        def body(c):
            acc, r = block(qs, *kv_block(pp, j - c[0]), c[1], c[2], None, neg_suffix)
            return c[0] + 1, acc, r, live(r)

        _, acc, r, alive = lax.while_loop(cond, body, (jnp.int32(2), acc, r, live(r)))

        def meta_block(acc):
            return block(qs, km_ref[:, pair_lanes(pp)], vm_ref[:, pair_lanes(pp)], acc, r, lane < N_META,
                         neg_suffix[:LANES, :LANES])[0]

        acc = lax.cond(alive, meta_block, lambda acc: acc, acc)

        sq = acc * acc
        s_lo = jnp.sum(jnp.where(lo, sq, 0.0), axis=-1, keepdims=True)
        s_hi = jnp.sum(jnp.where(lo, 0.0, sq), axis=-1, keepdims=True)
        rn = lax.rsqrt(jnp.where(lo, s_lo, s_hi) * (1.0 / SB_HEAD_DIM) + EPS)
        o_ref[0, sub * tq:(sub + 1) * tq, pair_lanes(pp)] = (acc * rn * g_ref[:, pair_lanes(pp)]).astype(BF16)


def _attention(q, k, v, km, vm, og, tq, nsub):
    b, s, _ = q.shape
    npair = ATTN_PAIRS_PER_STEP
    width = npair * LANES
    step = tq * nsub
    return pl.pallas_call(
        functools.partial(_attn_kernel, tq=tq, nsub=nsub, npair=npair),
        grid=(b, SB_WIDTH // width, s // step),
        in_specs=[
            pl.BlockSpec((1, step, width), lambda b, p, i: (b, i, p)),
            pl.BlockSpec((1, s, width), lambda b, p, i: (b, 0, p)),
            pl.BlockSpec((1, s, width), lambda b, p, i: (b, 0, p)),
            pl.BlockSpec((LANES, width), lambda b, p, i: (0, p)),
            pl.BlockSpec((LANES, width), lambda b, p, i: (0, p)),
            pl.BlockSpec((1, width), lambda b, p, i: (0, p)),
        ],
        out_specs=pl.BlockSpec((1, step, width), lambda b, p, i: (b, i, p)),
        out_shape=jax.ShapeDtypeStruct((b, s, SB_WIDTH), BF16),
        compiler_params=_params(3),
        name="attention",
    )(q, k, v, km, vm, og)


def _mix_kernel(h_ref, halo_ref, mhalo_ref, sb_ref, x_ref, cw_ref, cb_ref, lg_ref, lb_ref, wo_ref,
                g2_ref, wr_ref, br_ref,
                xn_ref, eid_ref, gate_ref, rank_ref, before_ref, cnt_ref, rows_ref,
                win_ref, conv_ref, cv_ref, carry_ref, hn3_ref, posv_ref, pos_ref, dsem, psem, *, tm, chunk, cap):
    b = pl.program_id(0)
    i = pl.program_id(1)
    step = b * pl.num_programs(1) + i

    def pos_copy():
        return pltpu.make_async_copy(posv_ref, pos_ref, psem)

    def row_copy(t, k):
        return pltpu.make_async_copy(_row_tiles(hn3_ref, t), _row_tiles(rows_ref, pos_ref[0, k * tm + t]), dsem)

    @pl.when(step == 0)
    def _():
        carry_ref[...] = jnp.zeros_like(carry_ref)
        hn3_ref[...] = jnp.zeros_like(hn3_ref)
        posv_ref[...] = N_EXPERTS * cap + lax.broadcasted_iota(jnp.int32, posv_ref.shape, 1)
        pos_copy().start()

    pos_copy().wait()

    @pl.when(i == 0)
    def _():
        win_ref[0:HALO, :] = mhalo_ref[...]

    @pl.when(i > 0)
    def _():
        win_ref[0:HALO, :] = halo_ref[0]

    win_ref[HALO:, :] = h_ref[0]

    first_tap = HALO - (CONV_KERNEL - 1)

    def conv_chunk(c, _):
        r0 = pl.multiple_of(c * chunk, chunk)
        rows = chunk + HALO
        for lt in range(cb_ref.shape[1] // LANES):
            sl = slice(lt * LANES, (lt + 1) * LANES)
            window = win_ref[pl.ds(r0, rows), sl]
            acc = jnp.broadcast_to(cb_ref[:, sl], (chunk, LANES))
            for b in range(SUBLANES):
                shifted = window if b == 0 else pltpu.roll(window, shift=rows - b, axis=0)
                for a in range(HALO // SUBLANES + 1):
                    j = SUBLANES * a + b - first_tap
                    if 0 <= j < CONV_KERNEL:
                        acc = acc + cw_ref[j:j + 1, sl] * shifted[SUBLANES * a:SUBLANES * a + chunk, :]
            conv_ref[:, sl] = acc
        for tt in range(chunk):
            for k in range(TOP_K):
                row_copy(r0 + tt, k).start(priority=k % 2)
        acc = conv_ref[...]
        mu = jnp.mean(acc, axis=-1, keepdims=True)
        cen = acc - mu
        var = jnp.mean(cen * cen, axis=-1, keepdims=True)
        y = cen * lax.rsqrt(var + EPS) * lg_ref[...] + lb_ref[...]
        cv_ref[pl.ds(r0, chunk), :] = (y * jax.nn.sigmoid(y)).astype(BF16)
        return 0

    lax.fori_loop(0, tm // chunk, conv_chunk, 0)

    mixed = jnp.dot(sb_ref[0], wo_ref[0:SB_WIDTH, :], preferred_element_type=F32)
    mixed = mixed + jnp.dot(cv_ref[...], wo_ref[SB_WIDTH:, :], preferred_element_type=F32)
    xn = x_ref[0] + mixed
    xn_ref[0] = xn
    ms = jnp.mean(xn * xn, axis=-1, keepdims=True)
    hn = xn * lax.rsqrt(ms + EPS) * g2_ref[...]

    nt = (((1,), (1,)), ((), ()))
    hn_hi = hn.astype(BF16)
    hn_lo = (hn - hn_hi.astype(F32)).astype(BF16)
    both = lax.dot_general(wr_ref[...], hn_hi, nt, preferred_element_type=F32)
    logits = (both[:N_EXPERTS] + both[N_EXPERTS:]
              + lax.dot_general(wr_ref[0:N_EXPERTS, :], hn_lo, nt, preferred_element_type=F32) + br_ref[...])
    eidx = lax.broadcasted_iota(jnp.int32, logits.shape, 0)
    vals, idxs = [], []
    for _ in range(TOP_K):
        m = jnp.max(logits, axis=0, keepdims=True)
        sel = jnp.min(jnp.where(logits == m, eidx, N_EXPERTS), axis=0, keepdims=True)
        vals.append(m)
        idxs.append(sel)
        logits = jnp.where(eidx == sel, -jnp.inf, logits)
    exps = [jnp.exp(v - vals[0]) for v in vals]
    denom = exps[0] + exps[1] + exps[2] + exps[3]
    gate_ref[...] = jnp.concatenate([e / denom for e in exps], axis=0)
    eid_ref[...] = jnp.concatenate(idxs, axis=0)

    onehots = [(eidx == s).astype(F32) for s in idxs]
    chosen = onehots[0] + onehots[1] + onehots[2] + onehots[3]
    tr = lax.broadcasted_iota(jnp.int32, (tm, tm), 0)
    tc = lax.broadcasted_iota(jnp.int32, (tm, tm), 1)
    before = (tr < tc).astype(BF16)
    prefix = jnp.dot(chosen.astype(BF16), before, preferred_element_type=F32) + carry_ref[...]
    ranks = [jnp.sum(o * prefix, axis=0, keepdims=True).astype(jnp.int32) for o in onehots]
    rank_ref[...] = jnp.concatenate(ranks, axis=0)
    before_ref[...] = prefix.astype(jnp.int32)
    carry_ref[...] = carry_ref[...] + jnp.sum(chosen, axis=1, keepdims=True)
    cnt_ref[...] = jnp.broadcast_to(carry_ref[...], cnt_ref.shape)

    _wait_rows(rows_ref, TOP_K * tm, dsem)
    groups = hn.shape[1] // LANES
    for c in range(groups):
        hn3_ref[pl.ds(c, tm, stride=groups), :] = hn[:, c * LANES:(c + 1) * LANES]
    for k in range(TOP_K):
        posv_ref[:, k * tm:(k + 1) * tm] = idxs[k] * cap + ranks[k]
    pos_copy().start()

    @pl.when(step == pl.num_programs(0) * pl.num_programs(1) - 1)
    def _():
        pos_copy().wait()

        def issue(t, _):
            for k in range(TOP_K):
                row_copy(t, k).start(priority=k % 2)
            return 0

        lax.fori_loop(0, tm, issue, 0, unroll=2)
        _wait_rows(rows_ref, TOP_K * tm, dsem)


def _mix(hcv, mhalo, sb, x, conv_w, conv_b, ln_g, ln_b, w_out_bf, g2, wr_t, br, tm, chunk, cap):
    b, s, d = x.shape
    cw = hcv.shape[-1]
    t = b * s
    per = s // tm
    assert d == SUBLANES * LANES, d
    tile = lambda b, i: (b, i, 0)
    fixed = lambda b, i: (0, 0)
    tok = lambda b, i: (0, b * per + i)
    x_rows = N_EXPERTS * cap + TOP_K * tm
    return pl.pallas_call(
        functools.partial(_mix_kernel, tm=tm, chunk=chunk, cap=cap),
        grid=(b, per),
        in_specs=[
            pl.BlockSpec((1, tm, cw), tile),
            pl.BlockSpec((1, HALO, cw), lambda b, i: (b, jnp.maximum(i * (tm // HALO) - 1, 0), 0)),
            pl.BlockSpec((HALO, cw), fixed),
            pl.BlockSpec((1, tm, SB_WIDTH), tile),
            pl.BlockSpec((1, tm, d), tile),
            pl.BlockSpec((CONV_KERNEL, cw), fixed),
            pl.BlockSpec((1, cw), fixed),
            pl.BlockSpec((1, cw), fixed),
            pl.BlockSpec((1, cw), fixed),
            pl.BlockSpec((SB_WIDTH + cw, d), fixed),
            pl.BlockSpec((1, d), fixed),
            pl.BlockSpec((2 * N_EXPERTS, d), fixed),
            pl.BlockSpec((N_EXPERTS, 1), fixed),
        ],
        out_specs=[
            pl.BlockSpec((1, tm, d), tile),
            pl.BlockSpec((TOP_K, tm), tok),
            pl.BlockSpec((TOP_K, tm), tok),
            pl.BlockSpec((TOP_K, tm), tok),
            pl.BlockSpec((N_EXPERTS, tm), tok),
            pl.BlockSpec((N_EXPERTS, LANES), fixed),
            pl.BlockSpec(memory_space=pl.ANY),
        ],
        out_shape=[
            jax.ShapeDtypeStruct((b, s, d), F32),
            jax.ShapeDtypeStruct((TOP_K, t), jnp.int32),
            jax.ShapeDtypeStruct((TOP_K, t), F32),
            jax.ShapeDtypeStruct((TOP_K, t), jnp.int32),
            jax.ShapeDtypeStruct((N_EXPERTS, t), jnp.int32),
            jax.ShapeDtypeStruct((N_EXPERTS, LANES), F32),
            jax.ShapeDtypeStruct((x_rows * SUBLANES, LANES), F32),
        ],
        scratch_shapes=[
            pltpu.VMEM((tm + HALO, cw), F32),
            pltpu.VMEM((chunk, cw), F32),
            pltpu.VMEM((tm, cw), BF16),
            pltpu.VMEM((N_EXPERTS, 1), F32),
            pltpu.VMEM((tm * SUBLANES, LANES), F32),
            pltpu.VMEM((1, TOP_K * tm), jnp.int32),
            pltpu.SMEM((1, TOP_K * tm), jnp.int32),
            pltpu.SemaphoreType.DMA(()),
            pltpu.SemaphoreType.DMA(()),
        ],
        compiler_params=_params(2),
        name="mix_router",
    )(hcv, hcv, mhalo, sb, x, conv_w, conv_b, ln_g, ln_b, w_out_bf, g2, wr_t, br)


def _row_tiles(ref, row, n=1):
    size = n * SUBLANES if isinstance(n, int) else pl.multiple_of(n * SUBLANES, SUBLANES)
    return ref.at[pl.ds(pl.multiple_of(row * SUBLANES, SUBLANES), size)]


def _wait_rows(rows_ref, n, sem):
    pltpu.make_async_copy(_row_tiles(rows_ref, 0, n), _row_tiles(rows_ref, 0, n), sem).wait()


def _padfill_kernel(padlo_ref, padlen_ref, rows_in_ref, rows_ref, zeros_ref, zsem):
    del rows_in_ref
    zeros_ref[...] = jnp.zeros_like(zeros_ref)
    total = jnp.int32(0)
    for e in range(N_EXPERTS):
        lo, ln = padlo_ref[e], padlen_ref[e]
        total = total + ln
        for bit in range(EXPERT_HALF.bit_length() - 1):
            size = 1 << bit

            @pl.when((ln >> bit) & 1 == 1)
            def _():
                start = lo + ((ln >> (bit + 1)) << (bit + 1))
                pltpu.make_async_copy(_row_tiles(zeros_ref, 0, size), _row_tiles(rows_ref, start, size),
                                      zsem).start()

    @pl.when(total > 0)
    def _():
        _wait_rows(rows_ref, total, zsem)


def _padfill(pad_lo, pad_len, x_rows):
    smem = pl.BlockSpec(memory_space=pltpu.SMEM)
    return pl.pallas_call(
        _padfill_kernel,
        in_specs=[smem, smem, pl.BlockSpec(memory_space=pl.ANY)],
        out_specs=pl.BlockSpec(memory_space=pl.ANY),
        out_shape=jax.ShapeDtypeStruct(x_rows.shape, x_rows.dtype),
        scratch_shapes=[pltpu.VMEM((EXPERT_HALF * SUBLANES // 2, LANES), x_rows.dtype), pltpu.SemaphoreType.DMA(())],
        input_output_aliases={2: 0},
        compiler_params=pltpu.CompilerParams(vmem_limit_bytes=VMEM_LIMIT),
        name="padfill",
    )(pad_lo, pad_len, x_rows)


def _expert_kernel(be_ref, bsrc_ref, valid_ref, nused_ref, next_ref, slot_ref,
                   x_ref, wu_hbm, bu_ref, wd_hbm, bd_ref, y_ref,
                   wu_f32, wd_f32, wu_bf, wd_bf, wsem, *, cast_rows):
    del bsrc_ref, nused_ref
    j = pl.program_id(0)
    de = wd_hbm.shape[1]
    valid = valid_ref[j]

    def weight_copies(e, s):
        return (pltpu.make_async_copy(wu_hbm.at[e], wu_f32.at[s], wsem.at[0, s]),
                pltpu.make_async_copy(wd_hbm.at[e], wd_f32.at[s], wsem.at[1, s]))

    def half(h):
        x = jnp.concatenate([x_ref[pl.ds(h * EXPERT_HALF * SUBLANES + c, EXPERT_HALF, stride=SUBLANES), :]
                             for c in range(SUBLANES)], axis=1)
        up = jnp.dot(x.astype(BF16), wu_bf[...], preferred_element_type=F32) + bu_ref[0]
        glu = jnp.minimum(up[:, :de], SWIGLU_LIMIT)
        lin = jnp.clip(up[:, de:], -SWIGLU_LIMIT, SWIGLU_LIMIT)
        act = glu * jax.nn.sigmoid(SWIGLU_ALPHA * glu) * (lin + 1.0)
        y = jnp.dot(act.astype(BF16), wd_bf[...], preferred_element_type=F32) + bd_ref[0]
        y_ref[h * EXPERT_HALF:(h + 1) * EXPERT_HALF, :] = _pack_halves(y)

    @pl.when(valid == 0)
    def _():
        y_ref[...] = jnp.zeros_like(y_ref)

    @pl.when(valid > 0)
    def _():
        prev = be_ref[jnp.maximum(j - 1, 0)]

        @pl.when(jnp.logical_or(j == 0, be_ref[j] != prev))
        def _():
            s = slot_ref[j]

            @pl.when(j == 0)
            def _():
                for c in weight_copies(be_ref[0], s):
                    c.start()

            for c in weight_copies(be_ref[j], s):
                c.wait()

            def cast(c, _):
                r0 = pl.multiple_of(c * cast_rows, cast_rows)
                wu_bf[pl.ds(r0, cast_rows), :] = wu_f32[s, pl.ds(r0, cast_rows), :].astype(BF16)
                wd_bf[pl.ds(r0, cast_rows), :] = wd_f32[s, pl.ds(r0, cast_rows), :].astype(BF16)
                return 0

            lax.fori_loop(0, wu_f32.shape[1] // cast_rows, cast, 0)

            @pl.when(next_ref[j] >= 0)
            def _():
                for c in weight_copies(next_ref[j], 1 - s):
                    c.start()

    @pl.when(valid > EXPERT_HALF)
    def _():
        half(0)
        half(1)

    @pl.when(jnp.logical_and(valid > 0, valid <= EXPERT_HALF))
    def _():
        half(0)
        y_ref[EXPERT_HALF:, :] = jnp.zeros((EXPERT_ROWS - EXPERT_HALF, y_ref.shape[1]), y_ref.dtype)


def _pack_halves(y):
    half = y.shape[1] // 2
    bits = lambda v: lax.bitcast_convert_type(v.astype(BF16).astype(F32), jnp.uint32)
    return (bits(y[:, half:]) & jnp.uint32(0xFFFF0000)) | (bits(y[:, :half]) >> 16)


def _unpack_halves(w):
    lo = lax.bitcast_convert_type(w << 16, F32).astype(BF16)
    hi = lax.bitcast_convert_type(w & jnp.uint32(0xFFFF0000), F32).astype(BF16)
    return lo, hi


def _experts(block_expert, block_src, block_valid, n_used, x_rows, w_up, b_up, w_down, b_down):
    n_blocks = block_expert.shape[0] + 1
    block_valid = jnp.concatenate([block_valid, jnp.zeros((1,), jnp.int32)])
    ne, d, up_cols = w_up.shape
    de = w_down.shape[1]
    steps = jnp.arange(n_blocks - 1, dtype=jnp.int32)
    used = block_valid[:-1] > 0
    first = jnp.logical_and(used, jnp.concatenate([jnp.ones((1,), bool), block_expert[1:] != block_expert[:-1]]))
    slot = (jnp.cumsum(first.astype(jnp.int32)) - 1) % 2
    later_first = jnp.logical_and(first[None, :], steps[None, :] > steps[:, None])
    next_pos = jnp.min(jnp.where(later_first, steps[None, :], n_blocks), axis=1)
    next_expert = jnp.sum(jnp.where(steps[None, :] == next_pos[:, None], block_expert[None, :], 0), axis=1)
    next_expert = jnp.where(next_pos < n_blocks, next_expert, -1).astype(jnp.int32)
    blk = lambda j, be, bs, bv, nu, nx, sl: (bs[jnp.minimum(j, nu[0] - 1)], 0)
    wsel = lambda j, be, bs, bv, nu, nx, sl: (be[jnp.minimum(j, nu[0] - 1)], 0, 0)
    grid_spec = pltpu.PrefetchScalarGridSpec(
        num_scalar_prefetch=6,
        grid=(n_blocks,),
        in_specs=[
            pl.BlockSpec((EXPERT_ROWS * SUBLANES, LANES), blk),
            pl.BlockSpec(memory_space=pl.ANY),
            pl.BlockSpec((1, 1, up_cols), wsel),
            pl.BlockSpec(memory_space=pl.ANY),
            pl.BlockSpec((1, 1, d), wsel),
        ],
        out_specs=pl.BlockSpec((EXPERT_ROWS, d // 2), lambda j, be, bs, bv, nu, nx, sl: (j, 0)),
        scratch_shapes=[pltpu.VMEM((2, d, up_cols), F32), pltpu.VMEM((2, de, d), F32),
                        pltpu.VMEM((d, up_cols), BF16), pltpu.VMEM((de, d), BF16),
                        pltpu.SemaphoreType.DMA((2, 2))],
    )
    return pl.pallas_call(
        functools.partial(_expert_kernel, cast_rows=WEIGHT_CAST_ROWS),
        grid_spec=grid_spec,
        out_shape=jax.ShapeDtypeStruct((n_blocks * EXPERT_ROWS, d // 2), jnp.uint32),
        compiler_params=_params(1),
        name="experts",
    )(block_expert, block_src, block_valid, n_used, jnp.concatenate([next_expert, -jnp.ones((1,), jnp.int32)]),
      jnp.concatenate([slot, jnp.zeros((1,), slot.dtype)]).astype(jnp.int32), x_rows, w_up,
      b_up.reshape(ne, 1, up_cols), w_down, b_down.reshape(ne, 1, d))


def _combine_plan(run_start, run_cnt, seg_start, max_pieces):
    first = run_start // SUBLANES * SUBLANES
    pieces = jnp.where(run_cnt > 0, (run_start - first + run_cnt + PIECE - 1) // PIECE, 0)
    piece_end = jnp.cumsum(pieces, axis=0)
    piece_off = piece_end - pieces
    p = jnp.arange(max_pieces, dtype=jnp.int32)
    mine = jnp.logical_and(p >= piece_off[:, :, None], p < piece_end[:, :, None])
    src = jnp.sum(jnp.where(mine, (first - piece_off * PIECE)[:, :, None], 0), axis=0) + p * PIECE
    shift = (seg_start[:, None] + piece_off * PIECE - first).T
    return (src.reshape(-1).astype(jnp.int32), piece_end[-1].astype(jnp.int32),
            shift.reshape(-1).astype(jnp.int32))


def _to_columns(x):
    pad = jnp.zeros((LANES - x.shape[0], x.shape[1]), x.dtype)
    return jnp.concatenate([x, pad], axis=0).T


def _combine_kernel(src_ref, npieces_ref, shift_ref, eid_ref, rank_ref, gate_ref, xn_ref, rows_ref, o_ref, buf,
                    sems, *, tm, max_pieces):
    i = pl.program_id(0)
    n = pl.num_programs(0)
    slot = lax.rem(i, 2)
    half = o_ref.shape[1] // 2

    def fetch(tile, s):
        def body(p, _):
            src = rows_ref.at[pl.ds(pl.multiple_of(src_ref[tile * max_pieces + p], SUBLANES), PIECE)]
            dst = buf.at[s, pl.ds(pl.multiple_of(p * PIECE, PIECE), PIECE)]
            pltpu.make_async_copy(src, dst, sems.at[s]).start()
            return 0

        lax.fori_loop(0, npieces_ref[tile], body, 0)

    @pl.when(i == 0)
    def _():
        buf[...] = jnp.zeros_like(buf)
        fetch(0, 0)

    @pl.when(i + 1 < n)
    def _():
        fetch(i + 1, 1 - slot)

    total = npieces_ref[i] * PIECE
    eid = eid_ref[...]
    shift = jnp.zeros_like(eid)
    for e in range(N_EXPERTS):
        shift = jnp.where(eid == e, shift_ref[i * N_EXPERTS + e], shift)
    staged = _to_columns((rank_ref[...] + shift).astype(F32)).astype(jnp.int32)
    gates = _to_columns(gate_ref[...])

    @pl.when(total > 0)
    def _():
        rows = pl.ds(0, pl.multiple_of(total, PIECE))
        pltpu.make_async_copy(rows_ref.at[rows], buf.at[slot, rows], sems.at[slot]).wait()

    def chunk_products(c0):
        col = lax.broadcasted_iota(jnp.int32, (tm, COMBINE_CHUNK), 1) + c0
        sel = jnp.zeros((tm, COMBINE_CHUNK), F32)
        for k in range(TOP_K):
            sel = jnp.where(staged[:, k:k + 1] == col, gates[:, k:k + 1], sel)
        sel = sel.astype(BF16)
        lo, hi = _unpack_halves(buf[slot, pl.ds(c0, COMBINE_CHUNK), :])
        return jnp.dot(sel, lo, preferred_element_type=F32), jnp.dot(sel, hi, preferred_element_type=F32)

    xn = xn_ref[...]
    acc_lo, acc_hi = xn[:, :half], xn[:, half:]
    for c in range(COMBINE_STATIC_CHUNKS):
        d_lo, d_hi = chunk_products(c * COMBINE_CHUNK)
        acc_lo, acc_hi = acc_lo + d_lo, acc_hi + d_hi
    o_ref[:, :half] = acc_lo
    o_ref[:, half:] = acc_hi

    def chunk(c, _):
        d_lo, d_hi = chunk_products(pl.multiple_of(c * COMBINE_CHUNK, COMBINE_CHUNK))
        o_ref[:, :half] += d_lo
        o_ref[:, half:] += d_hi
        return 0

    lax.fori_loop(COMBINE_STATIC_CHUNKS, (total + COMBINE_CHUNK - 1) // COMBINE_CHUNK, chunk, 0)


def _combine(run_start, run_cnt, seg_start, eid, rank, gates, xn, y_rows, tm):
    t, d = xn.shape
    n = t // tm
    cap = -(-(TOP_K * tm + 2 * PIECE * N_EXPERTS) // COMBINE_CHUNK) * COMBINE_CHUNK
    max_pieces = cap // PIECE
    src, npieces, shift = _combine_plan(run_start, run_cnt, seg_start, max_pieces)
    tok = lambda i, *_: (i, 0)
    per_k = lambda i, *_: (0, i)
    grid_spec = pltpu.PrefetchScalarGridSpec(
        num_scalar_prefetch=3,
        grid=(n,),
        in_specs=[
            pl.BlockSpec((TOP_K, tm), per_k),
            pl.BlockSpec((TOP_K, tm), per_k),
            pl.BlockSpec((TOP_K, tm), per_k),
            pl.BlockSpec((tm, d), tok),
            pl.BlockSpec(memory_space=pl.ANY),
        ],
        out_specs=pl.BlockSpec((tm, d), tok),
        scratch_shapes=[pltpu.VMEM((2, cap, d // 2), jnp.uint32), pltpu.SemaphoreType.DMA((2,))],
    )
    return pl.pallas_call(
        functools.partial(_combine_kernel, tm=tm, max_pieces=max_pieces),
        grid_spec=grid_spec,
        out_shape=jax.ShapeDtypeStruct((t, d), F32),
        compiler_params=_params(1),
        name="combine",
    )(src, npieces, shift, eid, rank, gates, xn, y_rows)


def _tile(n, want):
    t = min(n, want)
    assert n % t == 0, (n, t)
    return t


def kernel(x, meta_tokens, norm1_g, w_in, q_norm_g, k_norm_g, conv_w, conv_b, conv_ln_g, conv_ln_b,
           attn_out_g, w_out, norm2_g, w_router, b_router, w_up, b_up, w_down, b_down):
    assert norm1_g.shape[0] == 1, "single layer: meta-token rows are only keys/values and conv context"
    b, s, d = x.shape
    t = b * s
    cw = conv_w.shape[-1]

    g1 = norm1_g[0][None, :]
    w_in_bf = w_in[0].astype(BF16)
    qg = jnp.tile(q_norm_g[0], SB_HEADS)[None, :]
    kg = jnp.tile(k_norm_g[0], SB_HEADS)[None, :]

    q, k, v, hcv = _inproj(x.reshape(t, d), g1, w_in_bf, qg, kg, _tile(t, INPROJ_ROWS))
    _, km, vm, hm = _inproj(meta_tokens, g1, w_in_bf, qg, kg, N_META)
    pad = ((0, LANES - N_META), (0, 0))
    tq = _tile(s, ATTN_TILE)
    sb = _attention(q.reshape(b, s, SB_WIDTH), k.reshape(b, s, SB_WIDTH), v.reshape(b, s, SB_WIDTH),
                    jnp.pad(km, pad), jnp.pad(vm, pad), attn_out_g[0].reshape(1, SB_WIDTH), tq,
                    _tile(s, ATTN_TILES_PER_STEP * tq) // tq)

    mhalo = jnp.concatenate([jnp.zeros((HALO - N_META, cw), F32), hm], axis=0)
    tm = _tile(s, MIX_ROWS)
    wr_t = w_router[0].T
    wr_hi = wr_t.astype(BF16)
    wr_lo = (wr_t - wr_hi.astype(F32)).astype(BF16)
    cap = -(-t // EXPERT_ROWS) * EXPERT_ROWS
    xn, eid, gates, rank, before, counts, x_rows = _mix(
        hcv.reshape(b, s, cw), mhalo, sb, x, conv_w[0], conv_b[0][None, :], conv_ln_g[0][None, :],
        conv_ln_b[0][None, :], w_out[0].astype(BF16), norm2_g[0][None, :],
        jnp.concatenate([wr_hi, wr_lo], axis=0), b_router[0][:, None], tm, _tile(tm, CONV_CHUNK), cap)

    counts = counts[:, 0].astype(jnp.int32)
    padded = (counts + EXPERT_ROWS - 1) // EXPERT_ROWS * EXPERT_ROWS
    pad_end = jnp.cumsum(padded)
    pad_start = pad_end - padded
    n_blocks = t * TOP_K // EXPERT_ROWS + N_EXPERTS
    block_row = jnp.arange(n_blocks, dtype=jnp.int32) * EXPERT_ROWS
    block_expert = jnp.minimum(jnp.sum(block_row[:, None] >= pad_end[None, :], axis=1), N_EXPERTS - 1).astype(jnp.int32)
    mine = block_expert[:, None] == jnp.arange(N_EXPERTS, dtype=jnp.int32)
    lookup = lambda table: jnp.sum(jnp.where(mine, table[None, :], 0), axis=1)
    block_off = block_row - lookup(pad_start)
    block_src = (block_expert * cap + jnp.maximum(block_off, 0)) // EXPERT_ROWS
    block_valid = jnp.clip(lookup(counts) - block_off, 0, EXPERT_ROWS)
    block_valid = jnp.where(block_row < pad_end[-1], block_valid, 0)
    n_used = (pad_end[-1:] // EXPERT_ROWS).astype(jnp.int32)
    seg = jnp.arange(N_EXPERTS, dtype=jnp.int32) * cap
    x_rows = _padfill(seg + counts, (-counts) % EXPERT_HALF, x_rows)
    y_rows = _experts(block_expert, block_src.astype(jnp.int32), block_valid.astype(jnp.int32), n_used, x_rows,
                      w_up[0], b_up[0], w_down[0], b_down[0])
    tc = _tile(t, COMBINE_ROWS)
    run_start = pad_start[:, None] + before[:, ::tc]
    run_end = jnp.concatenate([run_start[:, 1:], (pad_start + counts)[:, None]], axis=1)
    out = _combine(run_start, run_end - run_start, pad_start, eid, rank, gates, xn.reshape(t, d), y_rows, tc)
    return out.reshape(b, s, d)
```

```python
import functools

import jax
import jax.numpy as jnp
from jax import lax
from jax.experimental import pallas as pl
from jax.experimental.pallas import tpu as pltpu

N_META = 16
SB_HEADS = 8
SB_HEAD_DIM = 64
SB_WIDTH = SB_HEADS * SB_HEAD_DIM
CONV_KERNEL = 31
N_EXPERTS = 32
TOP_K = 4
SWIGLU_LIMIT = 7.0
SWIGLU_ALPHA = 1.702
EPS = 1e-6
F32_EXP2_UNDERFLOW = -150.0
LOG2_E = 1.4426950408889634

LANES = 128
SUBLANES = 8
INPROJ_ROWS = 1024
ATTN_TILE = 256
ATTN_TILES_PER_STEP = 16
MIX_ROWS = 1024
CONV_CHUNK = 64
CONV_ISSUE = 48
COMBINE_ROWS = 256
WEIGHT_CAST_ROWS = 64
PIECE = 16
COMBINE_CHUNK = 512
COMBINE_STATIC_CHUNKS = 3
HALO = 32
EXPERT_HALF = 512
EXPERT_ROWS = 2 * EXPERT_HALF
VMEM_LIMIT = 56 * 1024 * 1024

F32 = jnp.float32
BF16 = jnp.bfloat16


def _params(n_axes, vmem=VMEM_LIMIT):
    return pltpu.CompilerParams(dimension_semantics=("arbitrary",) * n_axes, vmem_limit_bytes=vmem)


def _inproj_kernel(x_ref, g1_ref, w_ref, qg_ref, kg_ref, q_ref, k_ref, v_ref, h_ref):
    x = x_ref[...]
    ms = jnp.mean(x * x, axis=-1, keepdims=True)
    n = (x * lax.rsqrt(ms + EPS) * g1_ref[...]).astype(BF16)
    lo = lax.broadcasted_iota(jnp.int32, (1, LANES), 1) < SB_HEAD_DIM

    def proj(c0, c1):
        return jnp.dot(n, w_ref[:, c0:c1], preferred_element_type=F32)

    def head_norm(acc, g_ref, out_ref, scale):
        for c in range(SB_WIDTH // LANES):
            sl = slice(c * LANES, (c + 1) * LANES)
            a = acc[:, sl]
            sq = a * a
            s_lo = jnp.sum(jnp.where(lo, sq, 0.0), axis=-1, keepdims=True)
            s_hi = jnp.sum(jnp.where(lo, 0.0, sq), axis=-1, keepdims=True)
            r = lax.rsqrt(jnp.where(lo, s_lo, s_hi) * (1.0 / SB_HEAD_DIM) + EPS)
            out_ref[:, sl] = (a * r * (g_ref[:, sl] * scale)).astype(BF16)

    head_norm(proj(0, SB_WIDTH), qg_ref, q_ref, LOG2_E * SB_HEAD_DIM ** -0.5)
    head_norm(proj(SB_WIDTH, 2 * SB_WIDTH), kg_ref, k_ref, 1.0)
    v_ref[...] = proj(2 * SB_WIDTH, 3 * SB_WIDTH).astype(BF16)
    cw = (w_ref.shape[1] - 3 * SB_WIDTH) // 2
    val = proj(3 * SB_WIDTH, 3 * SB_WIDTH + cw)
    gate = proj(3 * SB_WIDTH + cw, 3 * SB_WIDTH + 2 * cw)
    h_ref[...] = val * jax.nn.sigmoid(gate)


def _inproj(x2, g1, w_in_bf, qg, kg, tm):
    t, d = x2.shape
    cols = w_in_bf.shape[1]
    cw = (cols - 3 * SB_WIDTH) // 2
    row = lambda i: (i, 0)
    fixed = lambda i: (0, 0)
    return pl.pallas_call(
        _inproj_kernel,
        grid=(t // tm,),
        in_specs=[
            pl.BlockSpec((tm, d), row),
            pl.BlockSpec((1, d), fixed),
            pl.BlockSpec((d, cols), fixed),
            pl.BlockSpec((1, SB_WIDTH), fixed),
            pl.BlockSpec((1, SB_WIDTH), fixed),
        ],
        out_specs=[
            pl.BlockSpec((tm, SB_WIDTH), row),
            pl.BlockSpec((tm, SB_WIDTH), row),
            pl.BlockSpec((tm, SB_WIDTH), row),
            pl.BlockSpec((tm, cw), row),
        ],
        out_shape=[
            jax.ShapeDtypeStruct((t, SB_WIDTH), BF16),
            jax.ShapeDtypeStruct((t, SB_WIDTH), BF16),
            jax.ShapeDtypeStruct((t, SB_WIDTH), BF16),
            jax.ShapeDtypeStruct((t, cw), F32),
        ],
        compiler_params=_params(1),
        name="inproj",
    )(x2, g1, w_in_bf, qg, kg)


def _attn_kernel(q_ref, k_ref, v_ref, km_ref, vm_ref, g_ref, o_ref, *, tq, nsub):
    i = pl.program_id(2)
    lane = lax.broadcasted_iota(jnp.int32, (1, LANES), 1)
    lo = lane < SB_HEAD_DIM
    zero_bf = jnp.zeros((), BF16)

    def stacked_q(sub):
        q2 = q_ref[0, sub * tq:(sub + 1) * tq, :]
        return jnp.concatenate([jnp.where(lo, q2, zero_bf), jnp.where(lo, zero_bf, q2)], axis=0)

    qss = [stacked_q(sub) for sub in range(nsub)]
    row = lax.broadcasted_iota(jnp.int32, (2 * tq, tq), 0)
    col = lax.broadcasted_iota(jnp.int32, (2 * tq, tq), 1)
    causal = col < jnp.where(row >= tq, row - tq, row)
    srow = lax.broadcasted_iota(jnp.int32, (tq, tq), 0)
    scol = lax.broadcasted_iota(jnp.int32, (tq, tq), 1)
    neg_suffix = jnp.where(srow > scol, -1.0, 0.0).astype(BF16)

    def block(qs, kb, vb, acc, r, mask, neg_suffix_m):
        z = lax.dot_general(qs, kb, (((1,), (1,)), ((), ())), preferred_element_type=F32)
        m = jnp.minimum(z, 0.0)
        p = jnp.maximum(z, 0.0)
        l1p = jnp.log2(1.0 + jnp.exp2(m - p))
        log_beta = m - l1p
        neg_keep = p + l1p
        if mask is not None:
            neg_keep = jnp.where(mask, neg_keep, 0.0)
        later = jnp.dot(neg_keep.astype(BF16), neg_suffix_m, preferred_element_type=F32)
        a = jnp.exp2(log_beta + later + r)
        if mask is not None:
            a = jnp.where(mask, a, 0.0)
        res = jnp.dot(a.astype(BF16), vb, preferred_element_type=F32)
        acc = acc + jnp.where(lo, res[:tq], res[tq:])
        return acc, r - jnp.sum(neg_keep, axis=-1, keepdims=True)

    def kv_block(j):
        start = pl.multiple_of(j * tq, tq)
        return k_ref[0, pl.ds(start, tq), :], v_ref[0, pl.ds(start, tq), :]

    def live(r):
        return jnp.max(r) > F32_EXP2_UNDERFLOW

    acc0 = jnp.zeros((tq, LANES), F32)
    r0 = jnp.zeros((2 * tq, 1), F32)
    first = i * nsub

    def guaranteed(first_has_previous):
        state = []
        for sub in range(nsub):
            acc, r = block(qss[sub], *kv_block(first + sub), acc0, r0, causal, neg_suffix)
            if sub > 0 or first_has_previous:
                acc, r = block(qss[sub], *kv_block(first + sub - 1), acc, r, None, neg_suffix)
            state += [acc, r]
        return tuple(state)

    state = lax.cond(i > 0, lambda _: guaranteed(True), lambda _: guaranteed(False), 0)

    for sub in range(nsub):
        j = first + sub
        qs, acc, r = qss[sub], state[2 * sub], state[2 * sub + 1]

        def cond(c):
            return jnp.logical_and(c[0] <= j, c[3])

        def body(c):
            acc, r = block(qs, *kv_block(j - c[0]), c[1], c[2], None, neg_suffix)
            return c[0] + 1, acc, r, live(r)

        _, acc, r, alive = lax.while_loop(cond, body, (jnp.int32(2), acc, r, live(r)))

        def meta_block(acc):
            return block(qs, km_ref[...], vm_ref[...], acc, r, lane < N_META, neg_suffix[:LANES, :LANES])[0]

        acc = lax.cond(alive, meta_block, lambda acc: acc, acc)

        sq = acc * acc
        s_lo = jnp.sum(jnp.where(lo, sq, 0.0), axis=-1, keepdims=True)
        s_hi = jnp.sum(jnp.where(lo, 0.0, sq), axis=-1, keepdims=True)
        rn = lax.rsqrt(jnp.where(lo, s_lo, s_hi) * (1.0 / SB_HEAD_DIM) + EPS)
        o_ref[0, sub * tq:(sub + 1) * tq, :] = (acc * rn * g_ref[...]).astype(BF16)


def _attention(q, k, v, km, vm, og, tq, nsub):
    b, s, _ = q.shape
    n_pairs = SB_WIDTH // LANES
    step = tq * nsub
    return pl.pallas_call(
        functools.partial(_attn_kernel, tq=tq, nsub=nsub),
        grid=(b, n_pairs, s // step),
        in_specs=[
            pl.BlockSpec((1, step, LANES), lambda b, p, i: (b, i, p)),
            pl.BlockSpec((1, s, LANES), lambda b, p, i: (b, 0, p)),
            pl.BlockSpec((1, s, LANES), lambda b, p, i: (b, 0, p)),
            pl.BlockSpec((LANES, LANES), lambda b, p, i: (0, p)),
            pl.BlockSpec((LANES, LANES), lambda b, p, i: (0, p)),
            pl.BlockSpec((1, LANES), lambda b, p, i: (0, p)),
        ],
        out_specs=pl.BlockSpec((1, step, LANES), lambda b, p, i: (b, i, p)),
        out_shape=jax.ShapeDtypeStruct((b, s, SB_WIDTH), BF16),
        compiler_params=_params(3),
        name="attention",
    )(q, k, v, km, vm, og)


def _mix_kernel(h_ref, halo_ref, mhalo_ref, sb_ref, x_ref, cw_ref, cb_ref, lg_ref, lb_ref, wo_ref,
                g2_ref, wr_ref, br_ref,
                xn_ref, eid_ref, gate_ref, rank_ref, before_ref, cnt_ref, rows_ref,
                win_ref, conv_ref, cv_ref, carry_ref, hn3_ref, posv_ref, pos_ref, dsem, psem, *, tm, chunk, cap):
    b = pl.program_id(0)
    i = pl.program_id(1)
    step = b * pl.num_programs(1) + i

    def pos_copy():
        return pltpu.make_async_copy(posv_ref, pos_ref, psem)

    def row_copy(t, k):
        return pltpu.make_async_copy(_row_tiles(hn3_ref, t), _row_tiles(rows_ref, pos_ref[0, k * tm + t]), dsem)

    @pl.when(step == 0)
    def _():
        carry_ref[...] = jnp.zeros_like(carry_ref)
        hn3_ref[...] = jnp.zeros_like(hn3_ref)
        posv_ref[...] = N_EXPERTS * cap + lax.broadcasted_iota(jnp.int32, posv_ref.shape, 1)
        pos_copy().start()

    pos_copy().wait()

    @pl.when(i == 0)
    def _():
        win_ref[0:HALO, :] = mhalo_ref[...]

    @pl.when(i > 0)
    def _():
        win_ref[0:HALO, :] = halo_ref[0]

    win_ref[HALO:, :] = h_ref[0]

    first_tap = HALO - (CONV_KERNEL - 1)

    def conv_chunk(c, _):
        r0 = pl.multiple_of(c * chunk, chunk)
        rows = chunk + HALO
        for lt in range(cb_ref.shape[1] // LANES):
            sl = slice(lt * LANES, (lt + 1) * LANES)
            window = win_ref[pl.ds(r0, rows), sl]
            acc = jnp.broadcast_to(cb_ref[:, sl], (chunk, LANES))
            for b in range(SUBLANES):
                shifted = window if b == 0 else pltpu.roll(window, shift=rows - b, axis=0)
                for a in range(HALO // SUBLANES + 1):
                    j = SUBLANES * a + b - first_tap
                    if 0 <= j < CONV_KERNEL:
                        acc = acc + cw_ref[j:j + 1, sl] * shifted[SUBLANES * a:SUBLANES * a + chunk, :]
            conv_ref[:, sl] = acc
        issue0 = pl.multiple_of(c * CONV_ISSUE, SUBLANES)
        for tt in range(CONV_ISSUE):
            for k in range(TOP_K):
                row_copy(issue0 + tt, k).start(priority=k % 2)
        acc = conv_ref[...]
        mu = jnp.mean(acc, axis=-1, keepdims=True)
        cen = acc - mu
        var = jnp.mean(cen * cen, axis=-1, keepdims=True)
        y = cen * lax.rsqrt(var + EPS) * lg_ref[...] + lb_ref[...]
        cv_ref[pl.ds(r0, chunk), :] = (y * jax.nn.sigmoid(y)).astype(BF16)
        return 0

    lax.fori_loop(0, tm // chunk, conv_chunk, 0)
    for t_rest in range(tm // chunk * CONV_ISSUE, tm):
        for k in range(TOP_K):
            row_copy(t_rest, k).start(priority=k % 2)

    mixed = jnp.dot(sb_ref[0], wo_ref[0:SB_WIDTH, :], preferred_element_type=F32)
    mixed = mixed + jnp.dot(cv_ref[...], wo_ref[SB_WIDTH:, :], preferred_element_type=F32)
    xn = x_ref[0] + mixed
    xn_ref[0] = xn
    ms = jnp.mean(xn * xn, axis=-1, keepdims=True)
    hn = xn * lax.rsqrt(ms + EPS) * g2_ref[...]

    nt = (((1,), (1,)), ((), ()))
    hn_hi = hn.astype(BF16)
    hn_lo = (hn - hn_hi.astype(F32)).astype(BF16)
    both = lax.dot_general(wr_ref[...], hn_hi, nt, preferred_element_type=F32)
    logits = (both[:N_EXPERTS] + both[N_EXPERTS:]
              + lax.dot_general(wr_ref[0:N_EXPERTS, :], hn_lo, nt, preferred_element_type=F32) + br_ref[...])
    eidx = lax.broadcasted_iota(jnp.int32, logits.shape, 0)
    vals, idxs = [], []
    for _ in range(TOP_K):
        m = jnp.max(logits, axis=0, keepdims=True)
        sel = jnp.min(jnp.where(logits == m, eidx, N_EXPERTS), axis=0, keepdims=True)
        vals.append(m)
        idxs.append(sel)
        logits = jnp.where(eidx == sel, -jnp.inf, logits)
    exps = [jnp.exp(v - vals[0]) for v in vals]
    denom = exps[0] + exps[1] + exps[2] + exps[3]
    gate_ref[...] = jnp.concatenate([e / denom for e in exps], axis=0)
    eid_ref[...] = jnp.concatenate(idxs, axis=0)

    onehots = [(eidx == s).astype(F32) for s in idxs]
    chosen = onehots[0] + onehots[1] + onehots[2] + onehots[3]
    tr = lax.broadcasted_iota(jnp.int32, (tm, tm), 0)
    tc = lax.broadcasted_iota(jnp.int32, (tm, tm), 1)
    before = (tr < tc).astype(BF16)
    prefix = jnp.dot(chosen.astype(BF16), before, preferred_element_type=F32) + carry_ref[...]
    ranks = [jnp.sum(o * prefix, axis=0, keepdims=True).astype(jnp.int32) for o in onehots]
    rank_ref[...] = jnp.concatenate(ranks, axis=0)
    before_ref[...] = prefix.astype(jnp.int32)
    carry_ref[...] = carry_ref[...] + jnp.sum(chosen, axis=1, keepdims=True)
    cnt_ref[...] = jnp.broadcast_to(carry_ref[...], cnt_ref.shape)

    _wait_rows(rows_ref, TOP_K * tm, dsem)
    groups = hn.shape[1] // LANES
    for c in range(groups):
        hn3_ref[pl.ds(c, tm, stride=groups), :] = hn[:, c * LANES:(c + 1) * LANES]
    for k in range(TOP_K):
        posv_ref[:, k * tm:(k + 1) * tm] = idxs[k] * cap + ranks[k]
    pos_copy().start()

    @pl.when(step == pl.num_programs(0) * pl.num_programs(1) - 1)
    def _():
        pos_copy().wait()

        def issue(t, _):
            for k in range(TOP_K):
                row_copy(t, k).start(priority=k % 2)
            return 0

        lax.fori_loop(0, tm, issue, 0, unroll=2)
        _wait_rows(rows_ref, TOP_K * tm, dsem)


def _mix(hcv, mhalo, sb, x, conv_w, conv_b, ln_g, ln_b, w_out_bf, g2, wr_t, br, tm, chunk, cap):
    b, s, d = x.shape
    cw = hcv.shape[-1]
    t = b * s
    per = s // tm
    assert d == SUBLANES * LANES, d
    tile = lambda b, i: (b, i, 0)
    fixed = lambda b, i: (0, 0)
    tok = lambda b, i: (0, b * per + i)
    x_rows = N_EXPERTS * cap + TOP_K * tm
    return pl.pallas_call(
        functools.partial(_mix_kernel, tm=tm, chunk=chunk, cap=cap),
        grid=(b, per),
        in_specs=[
            pl.BlockSpec((1, tm, cw), tile),
            pl.BlockSpec((1, HALO, cw), lambda b, i: (b, jnp.maximum(i * (tm // HALO) - 1, 0), 0)),
            pl.BlockSpec((HALO, cw), fixed),
            pl.BlockSpec((1, tm, SB_WIDTH), tile),
            pl.BlockSpec((1, tm, d), tile),
            pl.BlockSpec((CONV_KERNEL, cw), fixed),
            pl.BlockSpec((1, cw), fixed),
            pl.BlockSpec((1, cw), fixed),
            pl.BlockSpec((1, cw), fixed),
            pl.BlockSpec((SB_WIDTH + cw, d), fixed),
            pl.BlockSpec((1, d), fixed),
            pl.BlockSpec((2 * N_EXPERTS, d), fixed),
            pl.BlockSpec((N_EXPERTS, 1), fixed),
        ],
        out_specs=[
            pl.BlockSpec((1, tm, d), tile),
            pl.BlockSpec((TOP_K, tm), tok),
            pl.BlockSpec((TOP_K, tm), tok),
            pl.BlockSpec((TOP_K, tm), tok),
            pl.BlockSpec((N_EXPERTS, tm), tok),
            pl.BlockSpec((N_EXPERTS, LANES), fixed),
            pl.BlockSpec(memory_space=pl.ANY),
        ],
        out_shape=[
            jax.ShapeDtypeStruct((b, s, d), F32),
            jax.ShapeDtypeStruct((TOP_K, t), jnp.int32),
            jax.ShapeDtypeStruct((TOP_K, t), F32),
            jax.ShapeDtypeStruct((TOP_K, t), jnp.int32),
            jax.ShapeDtypeStruct((N_EXPERTS, t), jnp.int32),
            jax.ShapeDtypeStruct((N_EXPERTS, LANES), F32),
            jax.ShapeDtypeStruct((x_rows * SUBLANES, LANES), F32),
        ],
        scratch_shapes=[
            pltpu.VMEM((tm + HALO, cw), F32),
            pltpu.VMEM((chunk, cw), F32),
            pltpu.VMEM((tm, cw), BF16),
            pltpu.VMEM((N_EXPERTS, 1), F32),
            pltpu.VMEM((tm * SUBLANES, LANES), F32),
            pltpu.VMEM((1, TOP_K * tm), jnp.int32),
            pltpu.SMEM((1, TOP_K * tm), jnp.int32),
            pltpu.SemaphoreType.DMA(()),
            pltpu.SemaphoreType.DMA(()),
        ],
        compiler_params=_params(2),
        name="mix_router",
    )(hcv, hcv, mhalo, sb, x, conv_w, conv_b, ln_g, ln_b, w_out_bf, g2, wr_t, br)


def _row_tiles(ref, row, n=1):
    size = n * SUBLANES if isinstance(n, int) else pl.multiple_of(n * SUBLANES, SUBLANES)
    return ref.at[pl.ds(pl.multiple_of(row * SUBLANES, SUBLANES), size)]


def _wait_rows(rows_ref, n, sem):
    pltpu.make_async_copy(_row_tiles(rows_ref, 0, n), _row_tiles(rows_ref, 0, n), sem).wait()


def _padfill_kernel(padlo_ref, padlen_ref, rows_in_ref, rows_ref, zeros_ref, zsem):
    del rows_in_ref
    zeros_ref[...] = jnp.zeros_like(zeros_ref)
    total = jnp.int32(0)
    for e in range(N_EXPERTS):
        lo, ln = padlo_ref[e], padlen_ref[e]
        total = total + ln
        for bit in range(EXPERT_HALF.bit_length() - 1):
            size = 1 << bit

            @pl.when((ln >> bit) & 1 == 1)
            def _():
                start = lo + ((ln >> (bit + 1)) << (bit + 1))
                pltpu.make_async_copy(_row_tiles(zeros_ref, 0, size), _row_tiles(rows_ref, start, size),
                                      zsem).start()

    @pl.when(total > 0)
    def _():
        _wait_rows(rows_ref, total, zsem)


def _padfill(pad_lo, pad_len, x_rows):
    smem = pl.BlockSpec(memory_space=pltpu.SMEM)
    return pl.pallas_call(
        _padfill_kernel,
        in_specs=[smem, smem, pl.BlockSpec(memory_space=pl.ANY)],
        out_specs=pl.BlockSpec(memory_space=pl.ANY),
        out_shape=jax.ShapeDtypeStruct(x_rows.shape, x_rows.dtype),
        scratch_shapes=[pltpu.VMEM((EXPERT_HALF * SUBLANES // 2, LANES), x_rows.dtype), pltpu.SemaphoreType.DMA(())],
        input_output_aliases={2: 0},
        compiler_params=pltpu.CompilerParams(vmem_limit_bytes=VMEM_LIMIT),
        name="padfill",
    )(pad_lo, pad_len, x_rows)


def _expert_kernel(be_ref, bsrc_ref, valid_ref, nused_ref, next_ref, slot_ref,
                   x_ref, wu_hbm, bu_ref, wd_hbm, bd_ref, y_ref,
                   wu_f32, wd_f32, wu_bf, wd_bf, wsem, *, cast_rows):
    del bsrc_ref, nused_ref
    j = pl.program_id(0)
    de = wd_hbm.shape[1]
    valid = valid_ref[j]

    def weight_copies(e, s):
        return (pltpu.make_async_copy(wu_hbm.at[e], wu_f32.at[s], wsem.at[0, s]),
                pltpu.make_async_copy(wd_hbm.at[e], wd_f32.at[s], wsem.at[1, s]))

    def half(h):
        x = jnp.concatenate([x_ref[pl.ds(h * EXPERT_HALF * SUBLANES + c, EXPERT_HALF, stride=SUBLANES), :]
                             for c in range(SUBLANES)], axis=1)
        up = jnp.dot(x.astype(BF16), wu_bf[...], preferred_element_type=F32) + bu_ref[0]
        glu = jnp.minimum(up[:, :de], SWIGLU_LIMIT)
        lin = jnp.clip(up[:, de:], -SWIGLU_LIMIT, SWIGLU_LIMIT)
        act = glu * jax.nn.sigmoid(SWIGLU_ALPHA * glu) * (lin + 1.0)
        y = jnp.dot(act.astype(BF16), wd_bf[...], preferred_element_type=F32) + bd_ref[0]
        y_ref[h * EXPERT_HALF:(h + 1) * EXPERT_HALF, :] = _pack_halves(y)

    @pl.when(valid == 0)
    def _():
        y_ref[...] = jnp.zeros_like(y_ref)

    @pl.when(valid > 0)
    def _():
        prev = be_ref[jnp.maximum(j - 1, 0)]

        @pl.when(jnp.logical_or(j == 0, be_ref[j] != prev))
        def _():
            s = slot_ref[j]

            @pl.when(j == 0)
            def _():
                for c in weight_copies(be_ref[0], s):
                    c.start()

            for c in weight_copies(be_ref[j], s):
                c.wait()

            def cast(c, _):
                r0 = pl.multiple_of(c * cast_rows, cast_rows)
                wu_bf[pl.ds(r0, cast_rows), :] = wu_f32[s, pl.ds(r0, cast_rows), :].astype(BF16)
                wd_bf[pl.ds(r0, cast_rows), :] = wd_f32[s, pl.ds(r0, cast_rows), :].astype(BF16)
                return 0

            lax.fori_loop(0, wu_f32.shape[1] // cast_rows, cast, 0)

            @pl.when(next_ref[j] >= 0)
            def _():
                for c in weight_copies(next_ref[j], 1 - s):
                    c.start()

    @pl.when(valid > EXPERT_HALF)
    def _():
        half(0)
        half(1)

    @pl.when(jnp.logical_and(valid > 0, valid <= EXPERT_HALF))
    def _():
        half(0)
        y_ref[EXPERT_HALF:, :] = jnp.zeros((EXPERT_ROWS - EXPERT_HALF, y_ref.shape[1]), y_ref.dtype)


def _pack_halves(y):
    half = y.shape[1] // 2
    bits = lambda v: lax.bitcast_convert_type(v.astype(BF16).astype(F32), jnp.uint32)
    return (bits(y[:, half:]) & jnp.uint32(0xFFFF0000)) | (bits(y[:, :half]) >> 16)


def _unpack_halves(w):
    lo = lax.bitcast_convert_type(w << 16, F32).astype(BF16)
    hi = lax.bitcast_convert_type(w & jnp.uint32(0xFFFF0000), F32).astype(BF16)
    return lo, hi


def _experts(block_expert, block_src, block_valid, n_used, x_rows, w_up, b_up, w_down, b_down):
    n_blocks = block_expert.shape[0] + 1
    block_valid = jnp.concatenate([block_valid, jnp.zeros((1,), jnp.int32)])
    ne, d, up_cols = w_up.shape
    de = w_down.shape[1]
    steps = jnp.arange(n_blocks - 1, dtype=jnp.int32)
    used = block_valid[:-1] > 0
    first = jnp.logical_and(used, jnp.concatenate([jnp.ones((1,), bool), block_expert[1:] != block_expert[:-1]]))
    slot = (jnp.cumsum(first.astype(jnp.int32)) - 1) % 2
    later_first = jnp.logical_and(first[None, :], steps[None, :] > steps[:, None])
    next_pos = jnp.min(jnp.where(later_first, steps[None, :], n_blocks), axis=1)
    next_expert = jnp.sum(jnp.where(steps[None, :] == next_pos[:, None], block_expert[None, :], 0), axis=1)
    next_expert = jnp.where(next_pos < n_blocks, next_expert, -1).astype(jnp.int32)
    blk = lambda j, be, bs, bv, nu, nx, sl: (bs[jnp.minimum(j, nu[0] - 1)], 0)
    wsel = lambda j, be, bs, bv, nu, nx, sl: (be[jnp.minimum(j, nu[0] - 1)], 0, 0)
    grid_spec = pltpu.PrefetchScalarGridSpec(
        num_scalar_prefetch=6,
        grid=(n_blocks,),
        in_specs=[
            pl.BlockSpec((EXPERT_ROWS * SUBLANES, LANES), blk),
            pl.BlockSpec(memory_space=pl.ANY),
            pl.BlockSpec((1, 1, up_cols), wsel),
            pl.BlockSpec(memory_space=pl.ANY),
            pl.BlockSpec((1, 1, d), wsel),
        ],
        out_specs=pl.BlockSpec((EXPERT_ROWS, d // 2), lambda j, be, bs, bv, nu, nx, sl: (j, 0)),
        scratch_shapes=[pltpu.VMEM((2, d, up_cols), F32), pltpu.VMEM((2, de, d), F32),
                        pltpu.VMEM((d, up_cols), BF16), pltpu.VMEM((de, d), BF16),
                        pltpu.SemaphoreType.DMA((2, 2))],
    )
    return pl.pallas_call(
        functools.partial(_expert_kernel, cast_rows=WEIGHT_CAST_ROWS),
        grid_spec=grid_spec,
        out_shape=jax.ShapeDtypeStruct((n_blocks * EXPERT_ROWS, d // 2), jnp.uint32),
        compiler_params=_params(1),
        name="experts",
    )(block_expert, block_src, block_valid, n_used, jnp.concatenate([next_expert, -jnp.ones((1,), jnp.int32)]),
      jnp.concatenate([slot, jnp.zeros((1,), slot.dtype)]).astype(jnp.int32), x_rows, w_up,
      b_up.reshape(ne, 1, up_cols), w_down, b_down.reshape(ne, 1, d))


def _combine_plan(run_start, run_cnt, seg_start, max_pieces):
    first = run_start // SUBLANES * SUBLANES
    pieces = jnp.where(run_cnt > 0, (run_start - first + run_cnt + PIECE - 1) // PIECE, 0)
    piece_end = jnp.cumsum(pieces, axis=0)
    piece_off = piece_end - pieces
    p = jnp.arange(max_pieces, dtype=jnp.int32)
    mine = jnp.logical_and(p >= piece_off[:, :, None], p < piece_end[:, :, None])
    src = jnp.sum(jnp.where(mine, (first - piece_off * PIECE)[:, :, None], 0), axis=0) + p * PIECE
    shift = (seg_start[:, None] + piece_off * PIECE - first).T
    return (src.reshape(-1).astype(jnp.int32), piece_end[-1].astype(jnp.int32),
            shift.reshape(-1).astype(jnp.int32))


def _to_columns(x):
    pad = jnp.zeros((LANES - x.shape[0], x.shape[1]), x.dtype)
    return jnp.concatenate([x, pad], axis=0).T


def _combine_kernel(src_ref, npieces_ref, shift_ref, eid_ref, rank_ref, gate_ref, xn_ref, rows_ref, o_ref, buf,
                    sems, *, tm, max_pieces):
    i = pl.program_id(0)
    n = pl.num_programs(0)
    slot = lax.rem(i, 2)
    half = o_ref.shape[1] // 2

    def fetch(tile, s):
        def body(p, _):
            src = rows_ref.at[pl.ds(pl.multiple_of(src_ref[tile * max_pieces + p], SUBLANES), PIECE)]
            dst = buf.at[s, pl.ds(pl.multiple_of(p * PIECE, PIECE), PIECE)]
            pltpu.make_async_copy(src, dst, sems.at[s]).start()
            return 0

        lax.fori_loop(0, npieces_ref[tile], body, 0)

    @pl.when(i == 0)
    def _():
        buf[...] = jnp.zeros_like(buf)
        fetch(0, 0)

    @pl.when(i + 1 < n)
    def _():
        fetch(i + 1, 1 - slot)

    total = npieces_ref[i] * PIECE
    eid = eid_ref[...]
    shift = jnp.zeros_like(eid)
    for e in range(N_EXPERTS):
        shift = jnp.where(eid == e, shift_ref[i * N_EXPERTS + e], shift)
    staged = _to_columns((rank_ref[...] + shift).astype(F32)).astype(jnp.int32)
    gates = _to_columns(gate_ref[...])

    @pl.when(total > 0)
    def _():
        rows = pl.ds(0, pl.multiple_of(total, PIECE))
        pltpu.make_async_copy(rows_ref.at[rows], buf.at[slot, rows], sems.at[slot]).wait()

    def chunk_products(c0):
        col = lax.broadcasted_iota(jnp.int32, (tm, COMBINE_CHUNK), 1) + c0
        sel = jnp.zeros((tm, COMBINE_CHUNK), F32)
        for k in range(TOP_K):
            sel = jnp.where(staged[:, k:k + 1] == col, gates[:, k:k + 1], sel)
        sel = sel.astype(BF16)
        lo, hi = _unpack_halves(buf[slot, pl.ds(c0, COMBINE_CHUNK), :])
        return jnp.dot(sel, lo, preferred_element_type=F32), jnp.dot(sel, hi, preferred_element_type=F32)

    xn = xn_ref[...]
    acc_lo, acc_hi = xn[:, :half], xn[:, half:]
    for c in range(COMBINE_STATIC_CHUNKS):
        d_lo, d_hi = chunk_products(c * COMBINE_CHUNK)
        acc_lo, acc_hi = acc_lo + d_lo, acc_hi + d_hi
    o_ref[:, :half] = acc_lo
    o_ref[:, half:] = acc_hi

    def chunk(c, _):
        d_lo, d_hi = chunk_products(pl.multiple_of(c * COMBINE_CHUNK, COMBINE_CHUNK))
        o_ref[:, :half] += d_lo
        o_ref[:, half:] += d_hi
        return 0

    lax.fori_loop(COMBINE_STATIC_CHUNKS, (total + COMBINE_CHUNK - 1) // COMBINE_CHUNK, chunk, 0)


def _combine(run_start, run_cnt, seg_start, eid, rank, gates, xn, y_rows, tm):
    t, d = xn.shape
    n = t // tm
    cap = -(-(TOP_K * tm + 2 * PIECE * N_EXPERTS) // COMBINE_CHUNK) * COMBINE_CHUNK
    max_pieces = cap // PIECE
    src, npieces, shift = _combine_plan(run_start, run_cnt, seg_start, max_pieces)
    tok = lambda i, *_: (i, 0)
    per_k = lambda i, *_: (0, i)
    grid_spec = pltpu.PrefetchScalarGridSpec(
        num_scalar_prefetch=3,
        grid=(n,),
        in_specs=[
            pl.BlockSpec((TOP_K, tm), per_k),
            pl.BlockSpec((TOP_K, tm), per_k),
            pl.BlockSpec((TOP_K, tm), per_k),
            pl.BlockSpec((tm, d), tok),
            pl.BlockSpec(memory_space=pl.ANY),
        ],
        out_specs=pl.BlockSpec((tm, d), tok),
        scratch_shapes=[pltpu.VMEM((2, cap, d // 2), jnp.uint32), pltpu.SemaphoreType.DMA((2,))],
    )
    return pl.pallas_call(
        functools.partial(_combine_kernel, tm=tm, max_pieces=max_pieces),
        grid_spec=grid_spec,
        out_shape=jax.ShapeDtypeStruct((t, d), F32),
        compiler_params=_params(1),
        name="combine",
    )(src, npieces, shift, eid, rank, gates, xn, y_rows)


def _tile(n, want):
    t = min(n, want)
    assert n % t == 0, (n, t)
    return t


def kernel(x, meta_tokens, norm1_g, w_in, q_norm_g, k_norm_g, conv_w, conv_b, conv_ln_g, conv_ln_b,
           attn_out_g, w_out, norm2_g, w_router, b_router, w_up, b_up, w_down, b_down):
    assert norm1_g.shape[0] == 1, "single layer: meta-token rows are only keys/values and conv context"
    b, s, d = x.shape
    t = b * s
    cw = conv_w.shape[-1]

    g1 = norm1_g[0][None, :]
    w_in_bf = w_in[0].astype(BF16)
    qg = jnp.tile(q_norm_g[0], SB_HEADS)[None, :]
    kg = jnp.tile(k_norm_g[0], SB_HEADS)[None, :]

    q, k, v, hcv = _inproj(x.reshape(t, d), g1, w_in_bf, qg, kg, _tile(t, INPROJ_ROWS))
    _, km, vm, hm = _inproj(meta_tokens, g1, w_in_bf, qg, kg, N_META)
    pad = ((0, LANES - N_META), (0, 0))
    tq = _tile(s, ATTN_TILE)
    sb = _attention(q.reshape(b, s, SB_WIDTH), k.reshape(b, s, SB_WIDTH), v.reshape(b, s, SB_WIDTH),
                    jnp.pad(km, pad), jnp.pad(vm, pad), attn_out_g[0].reshape(1, SB_WIDTH), tq,
                    _tile(s, ATTN_TILES_PER_STEP * tq) // tq)

    mhalo = jnp.concatenate([jnp.zeros((HALO - N_META, cw), F32), hm], axis=0)
    tm = _tile(s, MIX_ROWS)
    wr_t = w_router[0].T
    wr_hi = wr_t.astype(BF16)
    wr_lo = (wr_t - wr_hi.astype(F32)).astype(BF16)
    cap = -(-t // EXPERT_ROWS) * EXPERT_ROWS
    xn, eid, gates, rank, before, counts, x_rows = _mix(
        hcv.reshape(b, s, cw), mhalo, sb, x, conv_w[0], conv_b[0][None, :], conv_ln_g[0][None, :],
        conv_ln_b[0][None, :], w_out[0].astype(BF16), norm2_g[0][None, :],
        jnp.concatenate([wr_hi, wr_lo], axis=0), b_router[0][:, None], tm, _tile(tm, CONV_CHUNK), cap)

    counts = counts[:, 0].astype(jnp.int32)
    padded = (counts + EXPERT_ROWS - 1) // EXPERT_ROWS * EXPERT_ROWS
    pad_end = jnp.cumsum(padded)
    pad_start = pad_end - padded
    n_blocks = t * TOP_K // EXPERT_ROWS + N_EXPERTS
    block_row = jnp.arange(n_blocks, dtype=jnp.int32) * EXPERT_ROWS
    block_expert = jnp.minimum(jnp.sum(block_row[:, None] >= pad_end[None, :], axis=1), N_EXPERTS - 1).astype(jnp.int32)
    mine = block_expert[:, None] == jnp.arange(N_EXPERTS, dtype=jnp.int32)
    lookup = lambda table: jnp.sum(jnp.where(mine, table[None, :], 0), axis=1)
    block_off = block_row - lookup(pad_start)
    block_src = (block_expert * cap + jnp.maximum(block_off, 0)) // EXPERT_ROWS
    block_valid = jnp.clip(lookup(counts) - block_off, 0, EXPERT_ROWS)
    block_valid = jnp.where(block_row < pad_end[-1], block_valid, 0)
    n_used = (pad_end[-1:] // EXPERT_ROWS).astype(jnp.int32)
    seg = jnp.arange(N_EXPERTS, dtype=jnp.int32) * cap
    x_rows = _padfill(seg + counts, (-counts) % EXPERT_HALF, x_rows)
    y_rows = _experts(block_expert, block_src.astype(jnp.int32), block_valid.astype(jnp.int32), n_used, x_rows,
                      w_up[0], b_up[0], w_down[0], b_down[0])
    tc = _tile(t, COMBINE_ROWS)
    run_start = pad_start[:, None] + before[:, ::tc]
    run_end = jnp.concatenate([run_start[:, 1:], (pad_start + counts)[:, None]], axis=1)
    out = _combine(run_start, run_end - run_start, pad_start, eid, rank, gates, xn.reshape(t, d), y_rows, tc)
    return out.reshape(b, s, d)
```

```python
import functools

import jax
import jax.numpy as jnp
from jax import lax
from jax.experimental import pallas as pl
from jax.experimental.pallas import tpu as pltpu

N_META = 16
SB_HEADS = 8
SB_HEAD_DIM = 64
SB_WIDTH = SB_HEADS * SB_HEAD_DIM
CONV_KERNEL = 31
N_EXPERTS = 32
TOP_K = 4
SWIGLU_LIMIT = 7.0
SWIGLU_ALPHA = 1.702
EPS = 1e-6
F32_EXP2_UNDERFLOW = -150.0
LOG2_E = 1.4426950408889634

LANES = 128
SUBLANES = 8
INPROJ_ROWS = 1024
ATTN_TILE = 256
ATTN_TILES_PER_STEP = 16
MIX_ROWS = 1024
CONV_CHUNK = 64
COMBINE_ROWS = 256
WEIGHT_CAST_ROWS = 64
PIECE = 16
COMBINE_CHUNK = 512
COMBINE_STATIC_CHUNKS = 3
HALO = 32
EXPERT_HALF = 512
EXPERT_QUARTER = 256
EXPERT_ROWS = 2 * EXPERT_HALF
VMEM_LIMIT = 56 * 1024 * 1024

F32 = jnp.float32
BF16 = jnp.bfloat16


def _params(n_axes, vmem=VMEM_LIMIT):
    return pltpu.CompilerParams(dimension_semantics=("arbitrary",) * n_axes, vmem_limit_bytes=vmem)


def _inproj_kernel(x_ref, g1_ref, w_ref, qg_ref, kg_ref, q_ref, k_ref, v_ref, h_ref):
    x = x_ref[...]
    ms = jnp.mean(x * x, axis=-1, keepdims=True)
    n = (x * lax.rsqrt(ms + EPS) * g1_ref[...]).astype(BF16)
    lo = lax.broadcasted_iota(jnp.int32, (1, LANES), 1) < SB_HEAD_DIM

    def proj(c0, c1):
        return jnp.dot(n, w_ref[:, c0:c1], preferred_element_type=F32)

    def head_norm(acc, g_ref, out_ref, scale):
        for c in range(SB_WIDTH // LANES):
            sl = slice(c * LANES, (c + 1) * LANES)
            a = acc[:, sl]
            sq = a * a
            s_lo = jnp.sum(jnp.where(lo, sq, 0.0), axis=-1, keepdims=True)
            s_hi = jnp.sum(jnp.where(lo, 0.0, sq), axis=-1, keepdims=True)
            r = lax.rsqrt(jnp.where(lo, s_lo, s_hi) * (1.0 / SB_HEAD_DIM) + EPS)
            out_ref[:, sl] = (a * r * (g_ref[:, sl] * scale)).astype(BF16)

    head_norm(proj(0, SB_WIDTH), qg_ref, q_ref, LOG2_E * SB_HEAD_DIM ** -0.5)
    head_norm(proj(SB_WIDTH, 2 * SB_WIDTH), kg_ref, k_ref, 1.0)
    v_ref[...] = proj(2 * SB_WIDTH, 3 * SB_WIDTH).astype(BF16)
    cw = (w_ref.shape[1] - 3 * SB_WIDTH) // 2
    val = proj(3 * SB_WIDTH, 3 * SB_WIDTH + cw)
    gate = proj(3 * SB_WIDTH + cw, 3 * SB_WIDTH + 2 * cw)
    h_ref[...] = val * jax.nn.sigmoid(gate)


def _inproj(x2, g1, w_in_bf, qg, kg, tm):
    t, d = x2.shape
    cols = w_in_bf.shape[1]
    cw = (cols - 3 * SB_WIDTH) // 2
    row = lambda i: (i, 0)
    fixed = lambda i: (0, 0)
    return pl.pallas_call(
        _inproj_kernel,
        grid=(t // tm,),
        in_specs=[
            pl.BlockSpec((tm, d), row),
            pl.BlockSpec((1, d), fixed),
            pl.BlockSpec((d, cols), fixed),
            pl.BlockSpec((1, SB_WIDTH), fixed),
            pl.BlockSpec((1, SB_WIDTH), fixed),
        ],
        out_specs=[
            pl.BlockSpec((tm, SB_WIDTH), row),
            pl.BlockSpec((tm, SB_WIDTH), row),
            pl.BlockSpec((tm, SB_WIDTH), row),
            pl.BlockSpec((tm, cw), row),
        ],
        out_shape=[
            jax.ShapeDtypeStruct((t, SB_WIDTH), BF16),
            jax.ShapeDtypeStruct((t, SB_WIDTH), BF16),
            jax.ShapeDtypeStruct((t, SB_WIDTH), BF16),
            jax.ShapeDtypeStruct((t, cw), F32),
        ],
        compiler_params=_params(1),
        name="inproj",
    )(x2, g1, w_in_bf, qg, kg)


def _attn_kernel(q_ref, k_ref, v_ref, km_ref, vm_ref, g_ref, o_ref, *, tq, nsub):
    i = pl.program_id(2)
    lane = lax.broadcasted_iota(jnp.int32, (1, LANES), 1)
    lo = lane < SB_HEAD_DIM
    zero_bf = jnp.zeros((), BF16)

    def stacked_q(sub):
        q2 = q_ref[0, sub * tq:(sub + 1) * tq, :]
        return jnp.concatenate([jnp.where(lo, q2, zero_bf), jnp.where(lo, zero_bf, q2)], axis=0)

    qss = [stacked_q(sub) for sub in range(nsub)]
    row = lax.broadcasted_iota(jnp.int32, (2 * tq, tq), 0)
    col = lax.broadcasted_iota(jnp.int32, (2 * tq, tq), 1)
    causal = col < jnp.where(row >= tq, row - tq, row)
    srow = lax.broadcasted_iota(jnp.int32, (tq, tq), 0)
    scol = lax.broadcasted_iota(jnp.int32, (tq, tq), 1)
    neg_suffix = jnp.where(srow > scol, -1.0, 0.0).astype(BF16)

    def block(qs, kb, vb, acc, r, mask, neg_suffix_m):
        z = lax.dot_general(qs, kb, (((1,), (1,)), ((), ())), preferred_element_type=F32)
        m = jnp.minimum(z, 0.0)
        p = jnp.maximum(z, 0.0)
        l1p = jnp.log2(1.0 + jnp.exp2(m - p))
        log_beta = m - l1p
        neg_keep = p + l1p
        if mask is not None:
            neg_keep = jnp.where(mask, neg_keep, 0.0)
        later = jnp.dot(neg_keep.astype(BF16), neg_suffix_m, preferred_element_type=F32)
        a = jnp.exp2(log_beta + later + r)
        if mask is not None:
            a = jnp.where(mask, a, 0.0)
        res = jnp.dot(a.astype(BF16), vb, preferred_element_type=F32)
        acc = acc + jnp.where(lo, res[:tq], res[tq:])
        return acc, r - jnp.sum(neg_keep, axis=-1, keepdims=True)

    def kv_block(j):
        start = pl.multiple_of(j * tq, tq)
        return k_ref[0, pl.ds(start, tq), :], v_ref[0, pl.ds(start, tq), :]

    def live(r):
        return jnp.max(r) > F32_EXP2_UNDERFLOW

    acc0 = jnp.zeros((tq, LANES), F32)
    r0 = jnp.zeros((2 * tq, 1), F32)
    first = i * nsub

    def guaranteed(first_has_previous):
        state = []
        for sub in range(nsub):
            acc, r = block(qss[sub], *kv_block(first + sub), acc0, r0, causal, neg_suffix)
            if sub > 0 or first_has_previous:
                acc, r = block(qss[sub], *kv_block(first + sub - 1), acc, r, None, neg_suffix)
            state += [acc, r]
        return tuple(state)

    state = lax.cond(i > 0, lambda _: guaranteed(True), lambda _: guaranteed(False), 0)

    for sub in range(nsub):
        j = first + sub
        qs, acc, r = qss[sub], state[2 * sub], state[2 * sub + 1]

        def cond(c):
            return jnp.logical_and(c[0] <= j, c[3])

        def body(c):
            acc, r = block(qs, *kv_block(j - c[0]), c[1], c[2], None, neg_suffix)
            return c[0] + 1, acc, r, live(r)

        _, acc, r, alive = lax.while_loop(cond, body, (jnp.int32(2), acc, r, live(r)))

        def meta_block(acc):
            return block(qs, km_ref[...], vm_ref[...], acc, r, lane < N_META, neg_suffix[:LANES, :LANES])[0]

        acc = lax.cond(alive, meta_block, lambda acc: acc, acc)

        sq = acc * acc
        s_lo = jnp.sum(jnp.where(lo, sq, 0.0), axis=-1, keepdims=True)
        s_hi = jnp.sum(jnp.where(lo, 0.0, sq), axis=-1, keepdims=True)
        rn = lax.rsqrt(jnp.where(lo, s_lo, s_hi) * (1.0 / SB_HEAD_DIM) + EPS)
        o_ref[0, sub * tq:(sub + 1) * tq, :] = (acc * rn * g_ref[...]).astype(BF16)


def _attention(q, k, v, km, vm, og, tq, nsub):
    b, s, _ = q.shape
    n_pairs = SB_WIDTH // LANES
    step = tq * nsub
    return pl.pallas_call(
        functools.partial(_attn_kernel, tq=tq, nsub=nsub),
        grid=(b, n_pairs, s // step),
        in_specs=[
            pl.BlockSpec((1, step, LANES), lambda b, p, i: (b, i, p)),
            pl.BlockSpec((1, s, LANES), lambda b, p, i: (b, 0, p)),
            pl.BlockSpec((1, s, LANES), lambda b, p, i: (b, 0, p)),
            pl.BlockSpec((LANES, LANES), lambda b, p, i: (0, p)),
            pl.BlockSpec((LANES, LANES), lambda b, p, i: (0, p)),
            pl.BlockSpec((1, LANES), lambda b, p, i: (0, p)),
        ],
        out_specs=pl.BlockSpec((1, step, LANES), lambda b, p, i: (b, i, p)),
        out_shape=jax.ShapeDtypeStruct((b, s, SB_WIDTH), BF16),
        compiler_params=_params(3),
        name="attention",
    )(q, k, v, km, vm, og)


def _mix_kernel(h_ref, halo_ref, mhalo_ref, sb_ref, x_ref, cw_ref, cb_ref, lg_ref, lb_ref, wo_ref,
                g2_ref, wr_ref, br_ref,
                xn_ref, eid_ref, gate_ref, rank_ref, before_ref, cnt_ref, rows_ref,
                win_ref, conv_ref, cv_ref, carry_ref, hn3_ref, posv_ref, pos_ref, dsem, psem, *, tm, chunk, cap):
    b = pl.program_id(0)
    i = pl.program_id(1)
    step = b * pl.num_programs(1) + i

    def pos_copy():
        return pltpu.make_async_copy(posv_ref, pos_ref, psem)

    def row_copy(t, k):
        return pltpu.make_async_copy(_row_tiles(hn3_ref, t), _row_tiles(rows_ref, pos_ref[0, k * tm + t]), dsem)

    @pl.when(step == 0)
    def _():
        carry_ref[...] = jnp.zeros_like(carry_ref)
        hn3_ref[...] = jnp.zeros_like(hn3_ref)
        posv_ref[...] = N_EXPERTS * cap + lax.broadcasted_iota(jnp.int32, posv_ref.shape, 1)
        pos_copy().start()

    pos_copy().wait()

    @pl.when(i == 0)
    def _():
        win_ref[0:HALO, :] = mhalo_ref[...]

    @pl.when(i > 0)
    def _():
        win_ref[0:HALO, :] = halo_ref[0]

    win_ref[HALO:, :] = h_ref[0]

    first_tap = HALO - (CONV_KERNEL - 1)

    def conv_chunk(c, _):
        r0 = pl.multiple_of(c * chunk, chunk)
        rows = chunk + HALO
        for lt in range(cb_ref.shape[1] // LANES):
            sl = slice(lt * LANES, (lt + 1) * LANES)
            window = win_ref[pl.ds(r0, rows), sl]
            acc = jnp.broadcast_to(cb_ref[:, sl], (chunk, LANES))
            for b in range(SUBLANES):
                shifted = window if b == 0 else pltpu.roll(window, shift=rows - b, axis=0)
                for a in range(HALO // SUBLANES + 1):
                    j = SUBLANES * a + b - first_tap
                    if 0 <= j < CONV_KERNEL:
                        acc = acc + cw_ref[j:j + 1, sl] * shifted[SUBLANES * a:SUBLANES * a + chunk, :]
            conv_ref[:, sl] = acc
        for tt in range(chunk):
            for k in range(TOP_K):
                row_copy(r0 + tt, k).start(priority=k % 2)
        acc = conv_ref[...]
        mu = jnp.mean(acc, axis=-1, keepdims=True)
        cen = acc - mu
        var = jnp.mean(cen * cen, axis=-1, keepdims=True)
        y = cen * lax.rsqrt(var + EPS) * lg_ref[...] + lb_ref[...]
        cv_ref[pl.ds(r0, chunk), :] = (y * jax.nn.sigmoid(y)).astype(BF16)
        return 0

    lax.fori_loop(0, tm // chunk, conv_chunk, 0)

    mixed = jnp.dot(sb_ref[0], wo_ref[0:SB_WIDTH, :], preferred_element_type=F32)
    mixed = mixed + jnp.dot(cv_ref[...], wo_ref[SB_WIDTH:, :], preferred_element_type=F32)
    xn = x_ref[0] + mixed
    xn_ref[0] = xn
    ms = jnp.mean(xn * xn, axis=-1, keepdims=True)
    hn = xn * lax.rsqrt(ms + EPS) * g2_ref[...]

    nt = (((1,), (1,)), ((), ()))
    hn_hi = hn.astype(BF16)
    hn_lo = (hn - hn_hi.astype(F32)).astype(BF16)
    both = lax.dot_general(wr_ref[...], hn_hi, nt, preferred_element_type=F32)
    logits = (both[:N_EXPERTS] + both[N_EXPERTS:]
              + lax.dot_general(wr_ref[0:N_EXPERTS, :], hn_lo, nt, preferred_element_type=F32) + br_ref[...])
    eidx = lax.broadcasted_iota(jnp.int32, logits.shape, 0)
    vals, idxs = [], []
    for _ in range(TOP_K):
        m = jnp.max(logits, axis=0, keepdims=True)
        sel = jnp.min(jnp.where(logits == m, eidx, N_EXPERTS), axis=0, keepdims=True)
        vals.append(m)
        idxs.append(sel)
        logits = jnp.where(eidx == sel, -jnp.inf, logits)
    exps = [jnp.exp(v - vals[0]) for v in vals]
    denom = exps[0] + exps[1] + exps[2] + exps[3]
    gate_ref[...] = jnp.concatenate([e / denom for e in exps], axis=0)
    eid_ref[...] = jnp.concatenate(idxs, axis=0)

    onehots = [(eidx == s).astype(F32) for s in idxs]
    chosen = onehots[0] + onehots[1] + onehots[2] + onehots[3]
    tr = lax.broadcasted_iota(jnp.int32, (tm, tm), 0)
    tc = lax.broadcasted_iota(jnp.int32, (tm, tm), 1)
    before = (tr < tc).astype(BF16)
    prefix = jnp.dot(chosen.astype(BF16), before, preferred_element_type=F32) + carry_ref[...]
    ranks = [jnp.sum(o * prefix, axis=0, keepdims=True).astype(jnp.int32) for o in onehots]
    rank_ref[...] = jnp.concatenate(ranks, axis=0)
    before_ref[...] = prefix.astype(jnp.int32)
    carry_ref[...] = carry_ref[...] + jnp.sum(chosen, axis=1, keepdims=True)
    cnt_ref[...] = jnp.broadcast_to(carry_ref[...], cnt_ref.shape)

    _wait_rows(rows_ref, TOP_K * tm, dsem)
    groups = hn.shape[1] // LANES
    for c in range(groups):
        hn3_ref[pl.ds(c, tm, stride=groups), :] = hn[:, c * LANES:(c + 1) * LANES]
    for k in range(TOP_K):
        posv_ref[:, k * tm:(k + 1) * tm] = idxs[k] * cap + ranks[k]
    pos_copy().start()

    @pl.when(step == pl.num_programs(0) * pl.num_programs(1) - 1)
    def _():
        pos_copy().wait()

        def issue(t, _):
            for k in range(TOP_K):
                row_copy(t, k).start(priority=k % 2)
            return 0

        lax.fori_loop(0, tm, issue, 0, unroll=2)
        _wait_rows(rows_ref, TOP_K * tm, dsem)


def _mix(hcv, mhalo, sb, x, conv_w, conv_b, ln_g, ln_b, w_out_bf, g2, wr_t, br, tm, chunk, cap):
    b, s, d = x.shape
    cw = hcv.shape[-1]
    t = b * s
    per = s // tm
    assert d == SUBLANES * LANES, d
    tile = lambda b, i: (b, i, 0)
    fixed = lambda b, i: (0, 0)
    tok = lambda b, i: (0, b * per + i)
    x_rows = N_EXPERTS * cap + TOP_K * tm
    return pl.pallas_call(
        functools.partial(_mix_kernel, tm=tm, chunk=chunk, cap=cap),
        grid=(b, per),
        in_specs=[
            pl.BlockSpec((1, tm, cw), tile),
            pl.BlockSpec((1, HALO, cw), lambda b, i: (b, jnp.maximum(i * (tm // HALO) - 1, 0), 0)),
            pl.BlockSpec((HALO, cw), fixed),
            pl.BlockSpec((1, tm, SB_WIDTH), tile),
            pl.BlockSpec((1, tm, d), tile),
            pl.BlockSpec((CONV_KERNEL, cw), fixed),
            pl.BlockSpec((1, cw), fixed),
            pl.BlockSpec((1, cw), fixed),
            pl.BlockSpec((1, cw), fixed),
            pl.BlockSpec((SB_WIDTH + cw, d), fixed),
            pl.BlockSpec((1, d), fixed),
            pl.BlockSpec((2 * N_EXPERTS, d), fixed),
            pl.BlockSpec((N_EXPERTS, 1), fixed),
        ],
        out_specs=[
            pl.BlockSpec((1, tm, d), tile),
            pl.BlockSpec((TOP_K, tm), tok),
            pl.BlockSpec((TOP_K, tm), tok),
            pl.BlockSpec((TOP_K, tm), tok),
            pl.BlockSpec((N_EXPERTS, tm), tok),
            pl.BlockSpec((N_EXPERTS, LANES), fixed),
            pl.BlockSpec(memory_space=pl.ANY),
        ],
        out_shape=[
            jax.ShapeDtypeStruct((b, s, d), F32),
            jax.ShapeDtypeStruct((TOP_K, t), jnp.int32),
            jax.ShapeDtypeStruct((TOP_K, t), F32),
            jax.ShapeDtypeStruct((TOP_K, t), jnp.int32),
            jax.ShapeDtypeStruct((N_EXPERTS, t), jnp.int32),
            jax.ShapeDtypeStruct((N_EXPERTS, LANES), F32),
            jax.ShapeDtypeStruct((x_rows * SUBLANES, LANES), F32),
        ],
        scratch_shapes=[
            pltpu.VMEM((tm + HALO, cw), F32),
            pltpu.VMEM((chunk, cw), F32),
            pltpu.VMEM((tm, cw), BF16),
            pltpu.VMEM((N_EXPERTS, 1), F32),
            pltpu.VMEM((tm * SUBLANES, LANES), F32),
            pltpu.VMEM((1, TOP_K * tm), jnp.int32),
            pltpu.SMEM((1, TOP_K * tm), jnp.int32),
            pltpu.SemaphoreType.DMA(()),
            pltpu.SemaphoreType.DMA(()),
        ],
        compiler_params=_params(2),
        name="mix_router",
    )(hcv, hcv, mhalo, sb, x, conv_w, conv_b, ln_g, ln_b, w_out_bf, g2, wr_t, br)


def _row_tiles(ref, row, n=1):
    size = n * SUBLANES if isinstance(n, int) else pl.multiple_of(n * SUBLANES, SUBLANES)
    return ref.at[pl.ds(pl.multiple_of(row * SUBLANES, SUBLANES), size)]


def _wait_rows(rows_ref, n, sem):
    pltpu.make_async_copy(_row_tiles(rows_ref, 0, n), _row_tiles(rows_ref, 0, n), sem).wait()


def _padfill_kernel(padlo_ref, padlen_ref, rows_in_ref, rows_ref, zeros_ref, zsem):
    del rows_in_ref
    zeros_ref[...] = jnp.zeros_like(zeros_ref)
    total = jnp.int32(0)
    for e in range(N_EXPERTS):
        lo, ln = padlo_ref[e], padlen_ref[e]
        total = total + ln
        for bit in range(EXPERT_HALF.bit_length() - 1):
            size = 1 << bit

            @pl.when((ln >> bit) & 1 == 1)
            def _():
                start = lo + ((ln >> (bit + 1)) << (bit + 1))
                pltpu.make_async_copy(_row_tiles(zeros_ref, 0, size), _row_tiles(rows_ref, start, size),
                                      zsem).start()

    @pl.when(total > 0)
    def _():
        _wait_rows(rows_ref, total, zsem)


def _padfill(pad_lo, pad_len, x_rows):
    smem = pl.BlockSpec(memory_space=pltpu.SMEM)
    return pl.pallas_call(
        _padfill_kernel,
        in_specs=[smem, smem, pl.BlockSpec(memory_space=pl.ANY)],
        out_specs=pl.BlockSpec(memory_space=pl.ANY),
        out_shape=jax.ShapeDtypeStruct(x_rows.shape, x_rows.dtype),
        scratch_shapes=[pltpu.VMEM((EXPERT_HALF * SUBLANES // 2, LANES), x_rows.dtype), pltpu.SemaphoreType.DMA(())],
        input_output_aliases={2: 0},
        compiler_params=pltpu.CompilerParams(vmem_limit_bytes=VMEM_LIMIT),
        name="padfill",
    )(pad_lo, pad_len, x_rows)


def _expert_kernel(be_ref, bsrc_ref, valid_ref, nused_ref, next_ref, slot_ref,
                   x_ref, wu_hbm, bu_ref, wd_hbm, bd_ref, y_ref,
                   wu_f32, wd_f32, wu_bf, wd_bf, wsem, *, cast_rows):
    del bsrc_ref, nused_ref
    j = pl.program_id(0)
    de = wd_hbm.shape[1]
    valid = valid_ref[j]

    def weight_copies(e, s):
        return (pltpu.make_async_copy(wu_hbm.at[e], wu_f32.at[s], wsem.at[0, s]),
                pltpu.make_async_copy(wd_hbm.at[e], wd_f32.at[s], wsem.at[1, s]))

    def chain(row0, nrows):
        x = jnp.concatenate([x_ref[pl.ds(row0 * SUBLANES + c, nrows, stride=SUBLANES), :]
                             for c in range(SUBLANES)], axis=1)
        up = jnp.dot(x.astype(BF16), wu_bf[...], preferred_element_type=F32) + bu_ref[0]
        glu = jnp.minimum(up[:, :de], SWIGLU_LIMIT)
        lin = jnp.clip(up[:, de:], -SWIGLU_LIMIT, SWIGLU_LIMIT)
        act = glu * jax.nn.sigmoid(SWIGLU_ALPHA * glu) * (lin + 1.0)
        y = jnp.dot(act.astype(BF16), wd_bf[...], preferred_element_type=F32) + bd_ref[0]
        y_ref[row0:row0 + nrows, :] = _pack_halves(y)

    def zero_from(row0):
        y_ref[row0:, :] = jnp.zeros((EXPERT_ROWS - row0, y_ref.shape[1]), y_ref.dtype)

    @pl.when(valid == 0)
    def _():
        y_ref[...] = jnp.zeros_like(y_ref)

    @pl.when(valid > 0)
    def _():
        prev = be_ref[jnp.maximum(j - 1, 0)]

        @pl.when(jnp.logical_or(j == 0, be_ref[j] != prev))
        def _():
            s = slot_ref[j]

            @pl.when(j == 0)
            def _():
                for c in weight_copies(be_ref[0], s):
                    c.start()

            for c in weight_copies(be_ref[j], s):
                c.wait()

            def cast(c, _):
                r0 = pl.multiple_of(c * cast_rows, cast_rows)
                wu_bf[pl.ds(r0, cast_rows), :] = wu_f32[s, pl.ds(r0, cast_rows), :].astype(BF16)
                wd_bf[pl.ds(r0, cast_rows), :] = wd_f32[s, pl.ds(r0, cast_rows), :].astype(BF16)
                return 0

            lax.fori_loop(0, wu_f32.shape[1] // cast_rows, cast, 0)

            @pl.when(next_ref[j] >= 0)
            def _():
                for c in weight_copies(next_ref[j], 1 - s):
                    c.start()

    hb, qb = EXPERT_HALF, EXPERT_QUARTER

    @pl.when(valid > hb + qb)
    def _():
        chain(0, hb)
        chain(hb, hb)

    @pl.when(jnp.logical_and(valid > hb, valid <= hb + qb))
    def _():
        chain(0, hb)
        chain(hb, qb)
        zero_from(hb + qb)

    @pl.when(jnp.logical_and(valid > qb, valid <= hb))
    def _():
        chain(0, hb)
        zero_from(hb)

    @pl.when(jnp.logical_and(valid > 0, valid <= qb))
    def _():
        chain(0, qb)
        zero_from(qb)


def _pack_halves(y):
    half = y.shape[1] // 2
    bits = lambda v: lax.bitcast_convert_type(v.astype(BF16).astype(F32), jnp.uint32)
    return (bits(y[:, half:]) & jnp.uint32(0xFFFF0000)) | (bits(y[:, :half]) >> 16)


def _unpack_halves(w):
    lo = lax.bitcast_convert_type(w << 16, F32).astype(BF16)
    hi = lax.bitcast_convert_type(w & jnp.uint32(0xFFFF0000), F32).astype(BF16)
    return lo, hi


def _experts(block_expert, block_src, block_valid, n_used, x_rows, w_up, b_up, w_down, b_down):
    n_blocks = block_expert.shape[0] + 1
    block_valid = jnp.concatenate([block_valid, jnp.zeros((1,), jnp.int32)])
    ne, d, up_cols = w_up.shape
    de = w_down.shape[1]
    steps = jnp.arange(n_blocks - 1, dtype=jnp.int32)
    used = block_valid[:-1] > 0
    first = jnp.logical_and(used, jnp.concatenate([jnp.ones((1,), bool), block_expert[1:] != block_expert[:-1]]))
    slot = (jnp.cumsum(first.astype(jnp.int32)) - 1) % 2
    later_first = jnp.logical_and(first[None, :], steps[None, :] > steps[:, None])
    next_pos = jnp.min(jnp.where(later_first, steps[None, :], n_blocks), axis=1)
    next_expert = jnp.sum(jnp.where(steps[None, :] == next_pos[:, None], block_expert[None, :], 0), axis=1)
    next_expert = jnp.where(next_pos < n_blocks, next_expert, -1).astype(jnp.int32)
    blk = lambda j, be, bs, bv, nu, nx, sl: (bs[jnp.minimum(j, nu[0] - 1)], 0)
    wsel = lambda j, be, bs, bv, nu, nx, sl: (be[jnp.minimum(j, nu[0] - 1)], 0, 0)
    grid_spec = pltpu.PrefetchScalarGridSpec(
        num_scalar_prefetch=6,
        grid=(n_blocks,),
        in_specs=[
            pl.BlockSpec((EXPERT_ROWS * SUBLANES, LANES), blk),
            pl.BlockSpec(memory_space=pl.ANY),
            pl.BlockSpec((1, 1, up_cols), wsel),
            pl.BlockSpec(memory_space=pl.ANY),
            pl.BlockSpec((1, 1, d), wsel),
        ],
        out_specs=pl.BlockSpec((EXPERT_ROWS, d // 2), lambda j, be, bs, bv, nu, nx, sl: (j, 0)),
        scratch_shapes=[pltpu.VMEM((2, d, up_cols), F32), pltpu.VMEM((2, de, d), F32),
                        pltpu.VMEM((d, up_cols), BF16), pltpu.VMEM((de, d), BF16),
                        pltpu.SemaphoreType.DMA((2, 2))],
    )
    return pl.pallas_call(
        functools.partial(_expert_kernel, cast_rows=WEIGHT_CAST_ROWS),
        grid_spec=grid_spec,
        out_shape=jax.ShapeDtypeStruct((n_blocks * EXPERT_ROWS, d // 2), jnp.uint32),
        compiler_params=_params(1),
        name="experts",
    )(block_expert, block_src, block_valid, n_used, jnp.concatenate([next_expert, -jnp.ones((1,), jnp.int32)]),
      jnp.concatenate([slot, jnp.zeros((1,), slot.dtype)]).astype(jnp.int32), x_rows, w_up,
      b_up.reshape(ne, 1, up_cols), w_down, b_down.reshape(ne, 1, d))


def _combine_plan(run_start, run_cnt, seg_start, max_pieces):
    first = run_start // SUBLANES * SUBLANES
    pieces = jnp.where(run_cnt > 0, (run_start - first + run_cnt + PIECE - 1) // PIECE, 0)
    piece_end = jnp.cumsum(pieces, axis=0)
    piece_off = piece_end - pieces
    p = jnp.arange(max_pieces, dtype=jnp.int32)
    mine = jnp.logical_and(p >= piece_off[:, :, None], p < piece_end[:, :, None])
    src = jnp.sum(jnp.where(mine, (first - piece_off * PIECE)[:, :, None], 0), axis=0) + p * PIECE
    shift = (seg_start[:, None] + piece_off * PIECE - first).T
    return (src.reshape(-1).astype(jnp.int32), piece_end[-1].astype(jnp.int32),
            shift.reshape(-1).astype(jnp.int32))


def _to_columns(x):
    pad = jnp.zeros((LANES - x.shape[0], x.shape[1]), x.dtype)
    return jnp.concatenate([x, pad], axis=0).T


def _combine_kernel(src_ref, npieces_ref, shift_ref, eid_ref, rank_ref, gate_ref, xn_ref, rows_ref, o_ref, buf,
                    sems, *, tm, max_pieces):
    i = pl.program_id(0)
    n = pl.num_programs(0)
    slot = lax.rem(i, 2)
    half = o_ref.shape[1] // 2

    def fetch(tile, s):
        def body(p, _):
            src = rows_ref.at[pl.ds(pl.multiple_of(src_ref[tile * max_pieces + p], SUBLANES), PIECE)]
            dst = buf.at[s, pl.ds(pl.multiple_of(p * PIECE, PIECE), PIECE)]
            pltpu.make_async_copy(src, dst, sems.at[s]).start()
            return 0

        lax.fori_loop(0, npieces_ref[tile], body, 0)

    @pl.when(i == 0)
    def _():
        buf[...] = jnp.zeros_like(buf)
        fetch(0, 0)

    @pl.when(i + 1 < n)
    def _():
        fetch(i + 1, 1 - slot)

    total = npieces_ref[i] * PIECE
    eid = eid_ref[...]
    shift = jnp.zeros_like(eid)
    for e in range(N_EXPERTS):
        shift = jnp.where(eid == e, shift_ref[i * N_EXPERTS + e], shift)
    staged = _to_columns((rank_ref[...] + shift).astype(F32)).astype(jnp.int32)
    gates = _to_columns(gate_ref[...])

    @pl.when(total > 0)
    def _():
        rows = pl.ds(0, pl.multiple_of(total, PIECE))
        pltpu.make_async_copy(rows_ref.at[rows], buf.at[slot, rows], sems.at[slot]).wait()

    def chunk_products(c0):
        col = lax.broadcasted_iota(jnp.int32, (tm, COMBINE_CHUNK), 1) + c0
        sel = jnp.zeros((tm, COMBINE_CHUNK), F32)
        for k in range(TOP_K):
            sel = jnp.where(staged[:, k:k + 1] == col, gates[:, k:k + 1], sel)
        sel = sel.astype(BF16)
        lo, hi = _unpack_halves(buf[slot, pl.ds(c0, COMBINE_CHUNK), :])
        return jnp.dot(sel, lo, preferred_element_type=F32), jnp.dot(sel, hi, preferred_element_type=F32)

    xn = xn_ref[...]
    acc_lo, acc_hi = xn[:, :half], xn[:, half:]
    for c in range(COMBINE_STATIC_CHUNKS):
        d_lo, d_hi = chunk_products(c * COMBINE_CHUNK)
        acc_lo, acc_hi = acc_lo + d_lo, acc_hi + d_hi
    o_ref[:, :half] = acc_lo
    o_ref[:, half:] = acc_hi

    def chunk(c, _):
        d_lo, d_hi = chunk_products(pl.multiple_of(c * COMBINE_CHUNK, COMBINE_CHUNK))
        o_ref[:, :half] += d_lo
        o_ref[:, half:] += d_hi
        return 0

    lax.fori_loop(COMBINE_STATIC_CHUNKS, (total + COMBINE_CHUNK - 1) // COMBINE_CHUNK, chunk, 0)


def _combine(run_start, run_cnt, seg_start, eid, rank, gates, xn, y_rows, tm):
    t, d = xn.shape
    n = t // tm
    cap = -(-(TOP_K * tm + 2 * PIECE * N_EXPERTS) // COMBINE_CHUNK) * COMBINE_CHUNK
    max_pieces = cap // PIECE
    src, npieces, shift = _combine_plan(run_start, run_cnt, seg_start, max_pieces)
    tok = lambda i, *_: (i, 0)
    per_k = lambda i, *_: (0, i)
    grid_spec = pltpu.PrefetchScalarGridSpec(
        num_scalar_prefetch=3,
        grid=(n,),
        in_specs=[
            pl.BlockSpec((TOP_K, tm), per_k),
            pl.BlockSpec((TOP_K, tm), per_k),
            pl.BlockSpec((TOP_K, tm), per_k),
            pl.BlockSpec((tm, d), tok),
            pl.BlockSpec(memory_space=pl.ANY),
        ],
        out_specs=pl.BlockSpec((tm, d), tok),
        scratch_shapes=[pltpu.VMEM((2, cap, d // 2), jnp.uint32), pltpu.SemaphoreType.DMA((2,))],
    )
    return pl.pallas_call(
        functools.partial(_combine_kernel, tm=tm, max_pieces=max_pieces),
        grid_spec=grid_spec,
        out_shape=jax.ShapeDtypeStruct((t, d), F32),
        compiler_params=_params(1),
        name="combine",
    )(src, npieces, shift, eid, rank, gates, xn, y_rows)


def _tile(n, want):
    t = min(n, want)
    assert n % t == 0, (n, t)
    return t


def kernel(x, meta_tokens, norm1_g, w_in, q_norm_g, k_norm_g, conv_w, conv_b, conv_ln_g, conv_ln_b,
           attn_out_g, w_out, norm2_g, w_router, b_router, w_up, b_up, w_down, b_down):
    assert norm1_g.shape[0] == 1, "single layer: meta-token rows are only keys/values and conv context"
    b, s, d = x.shape
    t = b * s
    cw = conv_w.shape[-1]

    g1 = norm1_g[0][None, :]
    w_in_bf = w_in[0].astype(BF16)
    qg = jnp.tile(q_norm_g[0], SB_HEADS)[None, :]
    kg = jnp.tile(k_norm_g[0], SB_HEADS)[None, :]

    q, k, v, hcv = _inproj(x.reshape(t, d), g1, w_in_bf, qg, kg, _tile(t, INPROJ_ROWS))
    _, km, vm, hm = _inproj(meta_tokens, g1, w_in_bf, qg, kg, N_META)
    pad = ((0, LANES - N_META), (0, 0))
    tq = _tile(s, ATTN_TILE)
    sb = _attention(q.reshape(b, s, SB_WIDTH), k.reshape(b, s, SB_WIDTH), v.reshape(b, s, SB_WIDTH),
                    jnp.pad(km, pad), jnp.pad(vm, pad), attn_out_g[0].reshape(1, SB_WIDTH), tq,
                    _tile(s, ATTN_TILES_PER_STEP * tq) // tq)

    mhalo = jnp.concatenate([jnp.zeros((HALO - N_META, cw), F32), hm], axis=0)
    tm = _tile(s, MIX_ROWS)
    wr_t = w_router[0].T
    wr_hi = wr_t.astype(BF16)
    wr_lo = (wr_t - wr_hi.astype(F32)).astype(BF16)
    cap = -(-t // EXPERT_ROWS) * EXPERT_ROWS
    xn, eid, gates, rank, before, counts, x_rows = _mix(
        hcv.reshape(b, s, cw), mhalo, sb, x, conv_w[0], conv_b[0][None, :], conv_ln_g[0][None, :],
        conv_ln_b[0][None, :], w_out[0].astype(BF16), norm2_g[0][None, :],
        jnp.concatenate([wr_hi, wr_lo], axis=0), b_router[0][:, None], tm, _tile(tm, CONV_CHUNK), cap)

    counts = counts[:, 0].astype(jnp.int32)
    padded = (counts + EXPERT_ROWS - 1) // EXPERT_ROWS * EXPERT_ROWS
    pad_end = jnp.cumsum(padded)
    pad_start = pad_end - padded
    n_blocks = t * TOP_K // EXPERT_ROWS + N_EXPERTS
    block_row = jnp.arange(n_blocks, dtype=jnp.int32) * EXPERT_ROWS
    block_expert = jnp.minimum(jnp.sum(block_row[:, None] >= pad_end[None, :], axis=1), N_EXPERTS - 1).astype(jnp.int32)
    mine = block_expert[:, None] == jnp.arange(N_EXPERTS, dtype=jnp.int32)
    lookup = lambda table: jnp.sum(jnp.where(mine, table[None, :], 0), axis=1)
    block_off = block_row - lookup(pad_start)
    block_src = (block_expert * cap + jnp.maximum(block_off, 0)) // EXPERT_ROWS
    block_valid = jnp.clip(lookup(counts) - block_off, 0, EXPERT_ROWS)
    block_valid = jnp.where(block_row < pad_end[-1], block_valid, 0)
    n_used = (pad_end[-1:] // EXPERT_ROWS).astype(jnp.int32)
    seg = jnp.arange(N_EXPERTS, dtype=jnp.int32) * cap
    x_rows = _padfill(seg + counts, (-counts) % EXPERT_QUARTER, x_rows)
    y_rows = _experts(block_expert, block_src.astype(jnp.int32), block_valid.astype(jnp.int32), n_used, x_rows,
                      w_up[0], b_up[0], w_down[0], b_down[0])
    tc = _tile(t, COMBINE_ROWS)
    run_start = pad_start[:, None] + before[:, ::tc]
    run_end = jnp.concatenate([run_start[:, 1:], (pad_start + counts)[:, None]], axis=1)
    out = _combine(run_start, run_end - run_start, pad_start, eid, rank, gates, xn.reshape(t, d), y_rows, tc)
    return out.reshape(b, s, d)
```

```python
import functools

import jax
import jax.numpy as jnp
from jax import lax
from jax.experimental import pallas as pl
from jax.experimental.pallas import tpu as pltpu

N_META = 16
SB_HEADS = 8
SB_HEAD_DIM = 64
SB_WIDTH = SB_HEADS * SB_HEAD_DIM
CONV_KERNEL = 31
N_EXPERTS = 32
TOP_K = 4
SWIGLU_LIMIT = 7.0
SWIGLU_ALPHA = 1.702
EPS = 1e-6
F32_EXP2_UNDERFLOW = -150.0
LOG2_E = 1.4426950408889634

LANES = 128
SUBLANES = 8
INPROJ_ROWS = 1024
ATTN_TILE = 256
ATTN_TILES_PER_STEP = 16
MIX_ROWS = 1024
CONV_CHUNK = 64
COMBINE_ROWS = 256
WEIGHT_CAST_ROWS = 64
PIECE = 16
COMBINE_CHUNK = 512
COMBINE_STATIC_CHUNKS = 3
HALO = 32
EXPERT_HALF = 512
EXPERT_QUARTER = 256
EXPERT_ROWS = 2 * EXPERT_HALF
VMEM_LIMIT = 56 * 1024 * 1024

F32 = jnp.float32
BF16 = jnp.bfloat16


def _params(n_axes, vmem=VMEM_LIMIT):
    return pltpu.CompilerParams(dimension_semantics=("arbitrary",) * n_axes, vmem_limit_bytes=vmem)


def _inproj_kernel(x_ref, g1_ref, w_ref, qg_ref, kg_ref, q_ref, k_ref, v_ref, h_ref):
    x = x_ref[...]
    ms = jnp.mean(x * x, axis=-1, keepdims=True)
    n = (x * lax.rsqrt(ms + EPS) * g1_ref[...]).astype(BF16)
    lo = lax.broadcasted_iota(jnp.int32, (1, LANES), 1) < SB_HEAD_DIM

    def proj(c0, c1):
        return jnp.dot(n, w_ref[:, c0:c1], preferred_element_type=F32)

    def head_norm(acc, g_ref, out_ref, scale):
        for c in range(SB_WIDTH // LANES):
            sl = slice(c * LANES, (c + 1) * LANES)
            a = acc[:, sl]
            sq = a * a
            s_lo = jnp.sum(jnp.where(lo, sq, 0.0), axis=-1, keepdims=True)
            s_hi = jnp.sum(jnp.where(lo, 0.0, sq), axis=-1, keepdims=True)
            r = lax.rsqrt(jnp.where(lo, s_lo, s_hi) * (1.0 / SB_HEAD_DIM) + EPS)
            out_ref[:, sl] = (a * r * (g_ref[:, sl] * scale)).astype(BF16)

    head_norm(proj(0, SB_WIDTH), qg_ref, q_ref, LOG2_E * SB_HEAD_DIM ** -0.5)
    head_norm(proj(SB_WIDTH, 2 * SB_WIDTH), kg_ref, k_ref, 1.0)
    v_ref[...] = proj(2 * SB_WIDTH, 3 * SB_WIDTH).astype(BF16)
    cw = (w_ref.shape[1] - 3 * SB_WIDTH) // 2
    val = proj(3 * SB_WIDTH, 3 * SB_WIDTH + cw)
    gate = proj(3 * SB_WIDTH + cw, 3 * SB_WIDTH + 2 * cw)
    h_ref[...] = val * jax.nn.sigmoid(gate)


def _inproj(x2, g1, w_in_bf, qg, kg, tm):
    t, d = x2.shape
    cols = w_in_bf.shape[1]
    cw = (cols - 3 * SB_WIDTH) // 2
    row = lambda i: (i, 0)
    fixed = lambda i: (0, 0)
    return pl.pallas_call(
        _inproj_kernel,
        grid=(t // tm,),
        in_specs=[
            pl.BlockSpec((tm, d), row),
            pl.BlockSpec((1, d), fixed),
            pl.BlockSpec((d, cols), fixed),
            pl.BlockSpec((1, SB_WIDTH), fixed),
            pl.BlockSpec((1, SB_WIDTH), fixed),
        ],
        out_specs=[
            pl.BlockSpec((tm, SB_WIDTH), row),
            pl.BlockSpec((tm, SB_WIDTH), row),
            pl.BlockSpec((tm, SB_WIDTH), row),
            pl.BlockSpec((tm, cw), row),
        ],
        out_shape=[
            jax.ShapeDtypeStruct((t, SB_WIDTH), BF16),
            jax.ShapeDtypeStruct((t, SB_WIDTH), BF16),
            jax.ShapeDtypeStruct((t, SB_WIDTH), BF16),
            jax.ShapeDtypeStruct((t, cw), F32),
        ],
        compiler_params=_params(1),
        name="inproj",
    )(x2, g1, w_in_bf, qg, kg)


def _attn_kernel(q_ref, k_ref, v_ref, km_ref, vm_ref, g_ref, o_ref, *, tq, nsub):
    i = pl.program_id(2)
    lane = lax.broadcasted_iota(jnp.int32, (1, LANES), 1)
    lo = lane < SB_HEAD_DIM
    zero_bf = jnp.zeros((), BF16)

    def stacked_q(sub):
        q2 = q_ref[0, sub * tq:(sub + 1) * tq, :]
        return jnp.concatenate([jnp.where(lo, q2, zero_bf), jnp.where(lo, zero_bf, q2)], axis=0)

    qss = [stacked_q(sub) for sub in range(nsub)]
    row = lax.broadcasted_iota(jnp.int32, (2 * tq, tq), 0)
    col = lax.broadcasted_iota(jnp.int32, (2 * tq, tq), 1)
    causal = col < jnp.where(row >= tq, row - tq, row)
    srow = lax.broadcasted_iota(jnp.int32, (tq, tq), 0)
    scol = lax.broadcasted_iota(jnp.int32, (tq, tq), 1)
    neg_suffix = jnp.where(srow > scol, -1.0, 0.0).astype(BF16)

    def block(qs, kb, vb, acc, r, mask, neg_suffix_m):
        z = lax.dot_general(qs, kb, (((1,), (1,)), ((), ())), preferred_element_type=F32)
        m = jnp.minimum(z, 0.0)
        p = jnp.maximum(z, 0.0)
        l1p = jnp.log2(1.0 + jnp.exp2(m - p))
        log_beta = m - l1p
        neg_keep = p + l1p
        if mask is not None:
            neg_keep = jnp.where(mask, neg_keep, 0.0)
        later = jnp.dot(neg_keep.astype(BF16), neg_suffix_m, preferred_element_type=F32)
        a = jnp.exp2(log_beta + later + r)
        if mask is not None:
            a = jnp.where(mask, a, 0.0)
        res = jnp.dot(a.astype(BF16), vb, preferred_element_type=F32)
        acc = acc + jnp.where(lo, res[:tq], res[tq:])
        return acc, r - jnp.sum(neg_keep, axis=-1, keepdims=True)

    def kv_block(j):
        start = pl.multiple_of(j * tq, tq)
        return k_ref[0, pl.ds(start, tq), :], v_ref[0, pl.ds(start, tq), :]

    def live(r):
        return jnp.max(r) > F32_EXP2_UNDERFLOW

    acc0 = jnp.zeros((tq, LANES), F32)
    r0 = jnp.zeros((2 * tq, 1), F32)
    first = i * nsub

    def guaranteed(first_has_previous):
        state = []
        for sub in range(nsub):
            acc, r = block(qss[sub], *kv_block(first + sub), acc0, r0, causal, neg_suffix)
            if sub > 0 or first_has_previous:
                acc, r = block(qss[sub], *kv_block(first + sub - 1), acc, r, None, neg_suffix)
            state += [acc, r]
        return tuple(state)

    state = lax.cond(i > 0, lambda _: guaranteed(True), lambda _: guaranteed(False), 0)

    def finish(sub, acc, r):
        j = first + sub
        qs = qss[sub]

        def cond(c):
            return jnp.logical_and(c[0] <= j, c[3])

        def body(c):
            acc, r = block(qs, *kv_block(j - c[0]), c[1], c[2], None, neg_suffix)
            return c[0] + 1, acc, r, live(r)

        _, acc, r, alive = lax.while_loop(cond, body, (jnp.int32(2), acc, r, live(r)))

        def meta_block(acc):
            return block(qs, km_ref[...], vm_ref[...], acc, r, lane < N_META, neg_suffix[:LANES, :LANES])[0]

        return lax.cond(alive, meta_block, lambda acc: acc, acc)

    accs = [state[2 * sub] for sub in range(nsub)]
    rs = [state[2 * sub + 1] for sub in range(nsub)]
    solo = min(2, nsub)
    for sub in range(solo):
        accs[sub] = finish(sub, accs[sub], rs[sub])
    if nsub > solo:
        others = list(range(solo, nsub))
        joint = functools.reduce(jnp.maximum, [rs[sub] for sub in others])
        accs[solo:] = lax.cond(live(joint),
                               lambda a: tuple(finish(sub, a[n], rs[sub]) for n, sub in enumerate(others)),
                               lambda a: a, tuple(accs[solo:]))

    for sub in range(nsub):
        acc = accs[sub]
        sq = acc * acc
        s_lo = jnp.sum(jnp.where(lo, sq, 0.0), axis=-1, keepdims=True)
        s_hi = jnp.sum(jnp.where(lo, 0.0, sq), axis=-1, keepdims=True)
        rn = lax.rsqrt(jnp.where(lo, s_lo, s_hi) * (1.0 / SB_HEAD_DIM) + EPS)
        o_ref[0, sub * tq:(sub + 1) * tq, :] = (acc * rn * g_ref[...]).astype(BF16)


def _attention(q, k, v, km, vm, og, tq, nsub):
    b, s, _ = q.shape
    n_pairs = SB_WIDTH // LANES
    step = tq * nsub
    return pl.pallas_call(
        functools.partial(_attn_kernel, tq=tq, nsub=nsub),
        grid=(b, n_pairs, s // step),
        in_specs=[
            pl.BlockSpec((1, step, LANES), lambda b, p, i: (b, i, p)),
            pl.BlockSpec((1, s, LANES), lambda b, p, i: (b, 0, p)),
            pl.BlockSpec((1, s, LANES), lambda b, p, i: (b, 0, p)),
            pl.BlockSpec((LANES, LANES), lambda b, p, i: (0, p)),
            pl.BlockSpec((LANES, LANES), lambda b, p, i: (0, p)),
            pl.BlockSpec((1, LANES), lambda b, p, i: (0, p)),
        ],
        out_specs=pl.BlockSpec((1, step, LANES), lambda b, p, i: (b, i, p)),
        out_shape=jax.ShapeDtypeStruct((b, s, SB_WIDTH), BF16),
        compiler_params=_params(3),
        name="attention",
    )(q, k, v, km, vm, og)


def _mix_kernel(h_ref, halo_ref, mhalo_ref, sb_ref, x_ref, cw_ref, cb_ref, lg_ref, lb_ref, wo_ref,
                g2_ref, wr_ref, br_ref,
                xn_ref, eid_ref, gate_ref, rank_ref, before_ref, cnt_ref, rows_ref,
                win_ref, conv_ref, cv_ref, carry_ref, hn3_ref, posv_ref, pos_ref, dsem, psem, *, tm, chunk, cap):
    b = pl.program_id(0)
    i = pl.program_id(1)
    step = b * pl.num_programs(1) + i

    def pos_copy():
        return pltpu.make_async_copy(posv_ref, pos_ref, psem)

    def row_copy(t, k):
        return pltpu.make_async_copy(_row_tiles(hn3_ref, t), _row_tiles(rows_ref, pos_ref[0, k * tm + t]), dsem)

    @pl.when(step == 0)
    def _():
        carry_ref[...] = jnp.zeros_like(carry_ref)
        hn3_ref[...] = jnp.zeros_like(hn3_ref)
        posv_ref[...] = N_EXPERTS * cap + lax.broadcasted_iota(jnp.int32, posv_ref.shape, 1)
        pos_copy().start()

    pos_copy().wait()

    @pl.when(i == 0)
    def _():
        win_ref[0:HALO, :] = mhalo_ref[...]

    @pl.when(i > 0)
    def _():
        win_ref[0:HALO, :] = halo_ref[0]

    win_ref[HALO:, :] = h_ref[0]

    first_tap = HALO - (CONV_KERNEL - 1)

    def conv_chunk(c, _):
        r0 = pl.multiple_of(c * chunk, chunk)
        rows = chunk + HALO
        for lt in range(cb_ref.shape[1] // LANES):
            sl = slice(lt * LANES, (lt + 1) * LANES)
            window = win_ref[pl.ds(r0, rows), sl]
            acc = jnp.broadcast_to(cb_ref[:, sl], (chunk, LANES))
            for b in range(SUBLANES):
                shifted = window if b == 0 else pltpu.roll(window, shift=rows - b, axis=0)
                for a in range(HALO // SUBLANES + 1):
                    j = SUBLANES * a + b - first_tap
                    if 0 <= j < CONV_KERNEL:
                        acc = acc + cw_ref[j:j + 1, sl] * shifted[SUBLANES * a:SUBLANES * a + chunk, :]
            conv_ref[:, sl] = acc
        for tt in range(chunk):
            for k in range(TOP_K):
                row_copy(r0 + tt, k).start(priority=k % 2)
        acc = conv_ref[...]
        mu = jnp.mean(acc, axis=-1, keepdims=True)
        cen = acc - mu
        var = jnp.mean(cen * cen, axis=-1, keepdims=True)
        y = cen * lax.rsqrt(var + EPS) * lg_ref[...] + lb_ref[...]
        cv_ref[pl.ds(r0, chunk), :] = (y * jax.nn.sigmoid(y)).astype(BF16)
        return 0

    lax.fori_loop(0, tm // chunk, conv_chunk, 0)

    mixed = jnp.dot(sb_ref[0], wo_ref[0:SB_WIDTH, :], preferred_element_type=F32)
    mixed = mixed + jnp.dot(cv_ref[...], wo_ref[SB_WIDTH:, :], preferred_element_type=F32)
    xn = x_ref[0] + mixed
    xn_ref[0] = xn
    ms = jnp.mean(xn * xn, axis=-1, keepdims=True)
    hn = xn * lax.rsqrt(ms + EPS) * g2_ref[...]

    nt = (((1,), (1,)), ((), ()))
    hn_hi = hn.astype(BF16)
    hn_lo = (hn - hn_hi.astype(F32)).astype(BF16)
    both = lax.dot_general(wr_ref[...], hn_hi, nt, preferred_element_type=F32)
    logits = (both[:N_EXPERTS] + both[N_EXPERTS:]
              + lax.dot_general(wr_ref[0:N_EXPERTS, :], hn_lo, nt, preferred_element_type=F32) + br_ref[...])
    eidx = lax.broadcasted_iota(jnp.int32, logits.shape, 0)
    vals, idxs = [], []
    for _ in range(TOP_K):
        m = jnp.max(logits, axis=0, keepdims=True)
        sel = jnp.min(jnp.where(logits == m, eidx, N_EXPERTS), axis=0, keepdims=True)
        vals.append(m)
        idxs.append(sel)
        logits = jnp.where(eidx == sel, -jnp.inf, logits)
    exps = [jnp.exp(v - vals[0]) for v in vals]
    denom = exps[0] + exps[1] + exps[2] + exps[3]
    gate_ref[...] = jnp.concatenate([e / denom for e in exps], axis=0)
    eid_ref[...] = jnp.concatenate(idxs, axis=0)

    onehots = [(eidx == s).astype(F32) for s in idxs]
    chosen = onehots[0] + onehots[1] + onehots[2] + onehots[3]
    tr = lax.broadcasted_iota(jnp.int32, (tm, tm), 0)
    tc = lax.broadcasted_iota(jnp.int32, (tm, tm), 1)
    before = (tr < tc).astype(BF16)
    prefix = jnp.dot(chosen.astype(BF16), before, preferred_element_type=F32) + carry_ref[...]
    ranks = [jnp.sum(o * prefix, axis=0, keepdims=True).astype(jnp.int32) for o in onehots]
    rank_ref[...] = jnp.concatenate(ranks, axis=0)
    before_ref[...] = prefix.astype(jnp.int32)
    carry_ref[...] = carry_ref[...] + jnp.sum(chosen, axis=1, keepdims=True)
    cnt_ref[...] = jnp.broadcast_to(carry_ref[...], cnt_ref.shape)

    _wait_rows(rows_ref, TOP_K * tm, dsem)
    groups = hn.shape[1] // LANES
    for c in range(groups):
        hn3_ref[pl.ds(c, tm, stride=groups), :] = hn[:, c * LANES:(c + 1) * LANES]
    for k in range(TOP_K):
        posv_ref[:, k * tm:(k + 1) * tm] = idxs[k] * cap + ranks[k]
    pos_copy().start()

    @pl.when(step == pl.num_programs(0) * pl.num_programs(1) - 1)
    def _():
        pos_copy().wait()

        def issue(t, _):
            for k in range(TOP_K):
                row_copy(t, k).start(priority=k % 2)
            return 0

        lax.fori_loop(0, tm, issue, 0, unroll=2)
        _wait_rows(rows_ref, TOP_K * tm, dsem)


def _mix(hcv, mhalo, sb, x, conv_w, conv_b, ln_g, ln_b, w_out_bf, g2, wr_t, br, tm, chunk, cap):
    b, s, d = x.shape
    cw = hcv.shape[-1]
    t = b * s
    per = s // tm
    assert d == SUBLANES * LANES, d
    tile = lambda b, i: (b, i, 0)
    fixed = lambda b, i: (0, 0)
    tok = lambda b, i: (0, b * per + i)
    x_rows = N_EXPERTS * cap + TOP_K * tm
    return pl.pallas_call(
        functools.partial(_mix_kernel, tm=tm, chunk=chunk, cap=cap),
        grid=(b, per),
        in_specs=[
            pl.BlockSpec((1, tm, cw), tile),
            pl.BlockSpec((1, HALO, cw), lambda b, i: (b, jnp.maximum(i * (tm // HALO) - 1, 0), 0)),
            pl.BlockSpec((HALO, cw), fixed),
            pl.BlockSpec((1, tm, SB_WIDTH), tile),
            pl.BlockSpec((1, tm, d), tile),
            pl.BlockSpec((CONV_KERNEL, cw), fixed),
            pl.BlockSpec((1, cw), fixed),
            pl.BlockSpec((1, cw), fixed),
            pl.BlockSpec((1, cw), fixed),
            pl.BlockSpec((SB_WIDTH + cw, d), fixed),
            pl.BlockSpec((1, d), fixed),
            pl.BlockSpec((2 * N_EXPERTS, d), fixed),
            pl.BlockSpec((N_EXPERTS, 1), fixed),
        ],
        out_specs=[
            pl.BlockSpec((1, tm, d), tile),
            pl.BlockSpec((TOP_K, tm), tok),
            pl.BlockSpec((TOP_K, tm), tok),
            pl.BlockSpec((TOP_K, tm), tok),
            pl.BlockSpec((N_EXPERTS, tm), tok),
            pl.BlockSpec((N_EXPERTS, LANES), fixed),
            pl.BlockSpec(memory_space=pl.ANY),
        ],
        out_shape=[
            jax.ShapeDtypeStruct((b, s, d), F32),
            jax.ShapeDtypeStruct((TOP_K, t), jnp.int32),
            jax.ShapeDtypeStruct((TOP_K, t), F32),
            jax.ShapeDtypeStruct((TOP_K, t), jnp.int32),
            jax.ShapeDtypeStruct((N_EXPERTS, t), jnp.int32),
            jax.ShapeDtypeStruct((N_EXPERTS, LANES), F32),
            jax.ShapeDtypeStruct((x_rows * SUBLANES, LANES), F32),
        ],
        scratch_shapes=[
            pltpu.VMEM((tm + HALO, cw), F32),
            pltpu.VMEM((chunk, cw), F32),
            pltpu.VMEM((tm, cw), BF16),
            pltpu.VMEM((N_EXPERTS, 1), F32),
            pltpu.VMEM((tm * SUBLANES, LANES), F32),
            pltpu.VMEM((1, TOP_K * tm), jnp.int32),
            pltpu.SMEM((1, TOP_K * tm), jnp.int32),
            pltpu.SemaphoreType.DMA(()),
            pltpu.SemaphoreType.DMA(()),
        ],
        compiler_params=_params(2),
        name="mix_router",
    )(hcv, hcv, mhalo, sb, x, conv_w, conv_b, ln_g, ln_b, w_out_bf, g2, wr_t, br)


def _row_tiles(ref, row, n=1):
    size = n * SUBLANES if isinstance(n, int) else pl.multiple_of(n * SUBLANES, SUBLANES)
    return ref.at[pl.ds(pl.multiple_of(row * SUBLANES, SUBLANES), size)]


def _wait_rows(rows_ref, n, sem):
    pltpu.make_async_copy(_row_tiles(rows_ref, 0, n), _row_tiles(rows_ref, 0, n), sem).wait()


def _padfill_kernel(padlo_ref, padlen_ref, rows_in_ref, rows_ref, zeros_ref, zsem):
    del rows_in_ref
    zeros_ref[...] = jnp.zeros_like(zeros_ref)
    total = jnp.int32(0)
    for e in range(N_EXPERTS):
        lo, ln = padlo_ref[e], padlen_ref[e]
        total = total + ln
        for bit in range(EXPERT_HALF.bit_length() - 1):
            size = 1 << bit

            @pl.when((ln >> bit) & 1 == 1)
            def _():
                start = lo + ((ln >> (bit + 1)) << (bit + 1))
                pltpu.make_async_copy(_row_tiles(zeros_ref, 0, size), _row_tiles(rows_ref, start, size),
                                      zsem).start()

    @pl.when(total > 0)
    def _():
        _wait_rows(rows_ref, total, zsem)


def _padfill(pad_lo, pad_len, x_rows):
    smem = pl.BlockSpec(memory_space=pltpu.SMEM)
    return pl.pallas_call(
        _padfill_kernel,
        in_specs=[smem, smem, pl.BlockSpec(memory_space=pl.ANY)],
        out_specs=pl.BlockSpec(memory_space=pl.ANY),
        out_shape=jax.ShapeDtypeStruct(x_rows.shape, x_rows.dtype),
        scratch_shapes=[pltpu.VMEM((EXPERT_HALF * SUBLANES // 2, LANES), x_rows.dtype), pltpu.SemaphoreType.DMA(())],
        input_output_aliases={2: 0},
        compiler_params=pltpu.CompilerParams(vmem_limit_bytes=VMEM_LIMIT),
        name="padfill",
    )(pad_lo, pad_len, x_rows)


def _expert_kernel(be_ref, bsrc_ref, valid_ref, nused_ref, next_ref, slot_ref,
                   x_ref, wu_hbm, bu_ref, wd_hbm, bd_ref, y_ref,
                   wu_f32, wd_f32, wu_bf, wd_bf, wsem, *, cast_rows):
    del bsrc_ref, nused_ref
    j = pl.program_id(0)
    de = wd_hbm.shape[1]
    valid = valid_ref[j]

    def weight_copies(e, s):
        return (pltpu.make_async_copy(wu_hbm.at[e], wu_f32.at[s], wsem.at[0, s]),
                pltpu.make_async_copy(wd_hbm.at[e], wd_f32.at[s], wsem.at[1, s]))

    def chain(row0, nrows):
        x = jnp.concatenate([x_ref[pl.ds(row0 * SUBLANES + c, nrows, stride=SUBLANES), :]
                             for c in range(SUBLANES)], axis=1)
        up = jnp.dot(x.astype(BF16), wu_bf[...], preferred_element_type=F32) + bu_ref[0]
        glu = jnp.minimum(up[:, :de], SWIGLU_LIMIT)
        lin = jnp.clip(up[:, de:], -SWIGLU_LIMIT, SWIGLU_LIMIT)
        act = glu * jax.nn.sigmoid(SWIGLU_ALPHA * glu) * (lin + 1.0)
        y = jnp.dot(act.astype(BF16), wd_bf[...], preferred_element_type=F32) + bd_ref[0]
        y_ref[row0:row0 + nrows, :] = _pack_halves(y)

    def zero_from(row0):
        y_ref[row0:, :] = jnp.zeros((EXPERT_ROWS - row0, y_ref.shape[1]), y_ref.dtype)

    @pl.when(valid == 0)
    def _():
        y_ref[...] = jnp.zeros_like(y_ref)

    @pl.when(valid > 0)
    def _():
        prev = be_ref[jnp.maximum(j - 1, 0)]

        @pl.when(jnp.logical_or(j == 0, be_ref[j] != prev))
        def _():
            s = slot_ref[j]

            @pl.when(j == 0)
            def _():
                for c in weight_copies(be_ref[0], s):
                    c.start()

            for c in weight_copies(be_ref[j], s):
                c.wait()

            def cast(c, _):
                r0 = pl.multiple_of(c * cast_rows, cast_rows)
                wu_bf[pl.ds(r0, cast_rows), :] = wu_f32[s, pl.ds(r0, cast_rows), :].astype(BF16)
                wd_bf[pl.ds(r0, cast_rows), :] = wd_f32[s, pl.ds(r0, cast_rows), :].astype(BF16)
                return 0

            lax.fori_loop(0, wu_f32.shape[1] // cast_rows, cast, 0)

            @pl.when(next_ref[j] >= 0)
            def _():
                for c in weight_copies(next_ref[j], 1 - s):
                    c.start()

    hb, qb = EXPERT_HALF, EXPERT_QUARTER

    @pl.when(valid > hb + qb)
    def _():
        chain(0, hb)
        chain(hb, hb)

    @pl.when(jnp.logical_and(valid > hb, valid <= hb + qb))
    def _():
        chain(0, hb)
        chain(hb, qb)
        zero_from(hb + qb)

    @pl.when(jnp.logical_and(valid > qb, valid <= hb))
    def _():
        chain(0, hb)
        zero_from(hb)

    @pl.when(jnp.logical_and(valid > 0, valid <= qb))
    def _():
        chain(0, qb)
        zero_from(qb)


def _pack_halves(y):
    half = y.shape[1] // 2
    bits = lambda v: lax.bitcast_convert_type(v.astype(BF16).astype(F32), jnp.uint32)
    return (bits(y[:, half:]) & jnp.uint32(0xFFFF0000)) | (bits(y[:, :half]) >> 16)


def _unpack_halves(w):
    lo = lax.bitcast_convert_type(w << 16, F32).astype(BF16)
    hi = lax.bitcast_convert_type(w & jnp.uint32(0xFFFF0000), F32).astype(BF16)
    return lo, hi


def _experts(block_expert, block_src, block_valid, n_used, x_rows, w_up, b_up, w_down, b_down):
    n_blocks = block_expert.shape[0] + 1
    block_valid = jnp.concatenate([block_valid, jnp.zeros((1,), jnp.int32)])
    ne, d, up_cols = w_up.shape
    de = w_down.shape[1]
    steps = jnp.arange(n_blocks - 1, dtype=jnp.int32)
    used = block_valid[:-1] > 0
    first = jnp.logical_and(used, jnp.concatenate([jnp.ones((1,), bool), block_expert[1:] != block_expert[:-1]]))
    slot = (jnp.cumsum(first.astype(jnp.int32)) - 1) % 2
    later_first = jnp.logical_and(first[None, :], steps[None, :] > steps[:, None])
    next_pos = jnp.min(jnp.where(later_first, steps[None, :], n_blocks), axis=1)
    next_expert = jnp.sum(jnp.where(steps[None, :] == next_pos[:, None], block_expert[None, :], 0), axis=1)
    next_expert = jnp.where(next_pos < n_blocks, next_expert, -1).astype(jnp.int32)
    blk = lambda j, be, bs, bv, nu, nx, sl: (bs[jnp.minimum(j, nu[0] - 1)], 0)
    wsel = lambda j, be, bs, bv, nu, nx, sl: (be[jnp.minimum(j, nu[0] - 1)], 0, 0)
    grid_spec = pltpu.PrefetchScalarGridSpec(
        num_scalar_prefetch=6,
        grid=(n_blocks,),
        in_specs=[
            pl.BlockSpec((EXPERT_ROWS * SUBLANES, LANES), blk),
            pl.BlockSpec(memory_space=pl.ANY),
            pl.BlockSpec((1, 1, up_cols), wsel),
            pl.BlockSpec(memory_space=pl.ANY),
            pl.BlockSpec((1, 1, d), wsel),
        ],
        out_specs=pl.BlockSpec((EXPERT_ROWS, d // 2), lambda j, be, bs, bv, nu, nx, sl: (j, 0)),
        scratch_shapes=[pltpu.VMEM((2, d, up_cols), F32), pltpu.VMEM((2, de, d), F32),
                        pltpu.VMEM((d, up_cols), BF16), pltpu.VMEM((de, d), BF16),
                        pltpu.SemaphoreType.DMA((2, 2))],
    )
    return pl.pallas_call(
        functools.partial(_expert_kernel, cast_rows=WEIGHT_CAST_ROWS),
        grid_spec=grid_spec,
        out_shape=jax.ShapeDtypeStruct((n_blocks * EXPERT_ROWS, d // 2), jnp.uint32),
        compiler_params=_params(1),
        name="experts",
    )(block_expert, block_src, block_valid, n_used, jnp.concatenate([next_expert, -jnp.ones((1,), jnp.int32)]),
      jnp.concatenate([slot, jnp.zeros((1,), slot.dtype)]).astype(jnp.int32), x_rows, w_up,
      b_up.reshape(ne, 1, up_cols), w_down, b_down.reshape(ne, 1, d))


def _combine_plan(run_start, run_cnt, seg_start, max_pieces):
    first = run_start // SUBLANES * SUBLANES
    pieces = jnp.where(run_cnt > 0, (run_start - first + run_cnt + PIECE - 1) // PIECE, 0)
    piece_end = jnp.cumsum(pieces, axis=0)
    piece_off = piece_end - pieces
    p = jnp.arange(max_pieces, dtype=jnp.int32)
    mine = jnp.logical_and(p >= piece_off[:, :, None], p < piece_end[:, :, None])
    src = jnp.sum(jnp.where(mine, (first - piece_off * PIECE)[:, :, None], 0), axis=0) + p * PIECE
    shift = (seg_start[:, None] + piece_off * PIECE - first).T
    return (src.reshape(-1).astype(jnp.int32), piece_end[-1].astype(jnp.int32),
            shift.reshape(-1).astype(jnp.int32))


def _to_columns(x):
    pad = jnp.zeros((LANES - x.shape[0], x.shape[1]), x.dtype)
    return jnp.concatenate([x, pad], axis=0).T


def _combine_kernel(src_ref, npieces_ref, shift_ref, eid_ref, rank_ref, gate_ref, xn_ref, rows_ref, o_ref, buf,
                    sems, *, tm, max_pieces):
    i = pl.program_id(0)
    n = pl.num_programs(0)
    slot = lax.rem(i, 2)
    half = o_ref.shape[1] // 2

    def fetch(tile, s):
        def body(p, _):
            src = rows_ref.at[pl.ds(pl.multiple_of(src_ref[tile * max_pieces + p], SUBLANES), PIECE)]
            dst = buf.at[s, pl.ds(pl.multiple_of(p * PIECE, PIECE), PIECE)]
            pltpu.make_async_copy(src, dst, sems.at[s]).start()
            return 0

        lax.fori_loop(0, npieces_ref[tile], body, 0)

    @pl.when(i == 0)
    def _():
        buf[...] = jnp.zeros_like(buf)
        fetch(0, 0)

    @pl.when(i + 1 < n)
    def _():
        fetch(i + 1, 1 - slot)

    total = npieces_ref[i] * PIECE
    eid = eid_ref[...]
    shift = jnp.zeros_like(eid)
    for e in range(N_EXPERTS):
        shift = jnp.where(eid == e, shift_ref[i * N_EXPERTS + e], shift)
    staged = _to_columns((rank_ref[...] + shift).astype(F32)).astype(jnp.int32)
    gates = _to_columns(gate_ref[...])

    @pl.when(total > 0)
    def _():
        rows = pl.ds(0, pl.multiple_of(total, PIECE))
        pltpu.make_async_copy(rows_ref.at[rows], buf.at[slot, rows], sems.at[slot]).wait()

    def chunk_products(c0):
        col = lax.broadcasted_iota(jnp.int32, (tm, COMBINE_CHUNK), 1) + c0
        sel = jnp.zeros((tm, COMBINE_CHUNK), F32)
        for k in range(TOP_K):
            sel = jnp.where(staged[:, k:k + 1] == col, gates[:, k:k + 1], sel)
        sel = sel.astype(BF16)
        lo, hi = _unpack_halves(buf[slot, pl.ds(c0, COMBINE_CHUNK), :])
        return jnp.dot(sel, lo, preferred_element_type=F32), jnp.dot(sel, hi, preferred_element_type=F32)

    xn = xn_ref[...]
    acc_lo, acc_hi = xn[:, :half], xn[:, half:]
    for c in range(COMBINE_STATIC_CHUNKS):
        d_lo, d_hi = chunk_products(c * COMBINE_CHUNK)
        acc_lo, acc_hi = acc_lo + d_lo, acc_hi + d_hi
    o_ref[:, :half] = acc_lo
    o_ref[:, half:] = acc_hi

    def chunk(c, _):
        d_lo, d_hi = chunk_products(pl.multiple_of(c * COMBINE_CHUNK, COMBINE_CHUNK))
        o_ref[:, :half] += d_lo
        o_ref[:, half:] += d_hi
        return 0

    lax.fori_loop(COMBINE_STATIC_CHUNKS, (total + COMBINE_CHUNK - 1) // COMBINE_CHUNK, chunk, 0)


def _combine(run_start, run_cnt, seg_start, eid, rank, gates, xn, y_rows, tm):
    t, d = xn.shape
    n = t // tm
    cap = -(-(TOP_K * tm + 2 * PIECE * N_EXPERTS) // COMBINE_CHUNK) * COMBINE_CHUNK
    max_pieces = cap // PIECE
    src, npieces, shift = _combine_plan(run_start, run_cnt, seg_start, max_pieces)
    tok = lambda i, *_: (i, 0)
    per_k = lambda i, *_: (0, i)
    grid_spec = pltpu.PrefetchScalarGridSpec(
        num_scalar_prefetch=3,
        grid=(n,),
        in_specs=[
            pl.BlockSpec((TOP_K, tm), per_k),
            pl.BlockSpec((TOP_K, tm), per_k),
            pl.BlockSpec((TOP_K, tm), per_k),
            pl.BlockSpec((tm, d), tok),
            pl.BlockSpec(memory_space=pl.ANY),
        ],
        out_specs=pl.BlockSpec((tm, d), tok),
        scratch_shapes=[pltpu.VMEM((2, cap, d // 2), jnp.uint32), pltpu.SemaphoreType.DMA((2,))],
    )
    return pl.pallas_call(
        functools.partial(_combine_kernel, tm=tm, max_pieces=max_pieces),
        grid_spec=grid_spec,
        out_shape=jax.ShapeDtypeStruct((t, d), F32),
        compiler_params=_params(1),
        name="combine",
    )(src, npieces, shift, eid, rank, gates, xn, y_rows)


def _tile(n, want):
    t = min(n, want)
    assert n % t == 0, (n, t)
    return t


def kernel(x, meta_tokens, norm1_g, w_in, q_norm_g, k_norm_g, conv_w, conv_b, conv_ln_g, conv_ln_b,
           attn_out_g, w_out, norm2_g, w_router, b_router, w_up, b_up, w_down, b_down):
    assert norm1_g.shape[0] == 1, "single layer: meta-token rows are only keys/values and conv context"
    b, s, d = x.shape
    t = b * s
    cw = conv_w.shape[-1]

    g1 = norm1_g[0][None, :]
    w_in_bf = w_in[0].astype(BF16)
    qg = jnp.tile(q_norm_g[0], SB_HEADS)[None, :]
    kg = jnp.tile(k_norm_g[0], SB_HEADS)[None, :]

    q, k, v, hcv = _inproj(x.reshape(t, d), g1, w_in_bf, qg, kg, _tile(t, INPROJ_ROWS))
    _, km, vm, hm = _inproj(meta_tokens, g1, w_in_bf, qg, kg, N_META)
    pad = ((0, LANES - N_META), (0, 0))
    tq = _tile(s, ATTN_TILE)
    sb = _attention(q.reshape(b, s, SB_WIDTH), k.reshape(b, s, SB_WIDTH), v.reshape(b, s, SB_WIDTH),
                    jnp.pad(km, pad), jnp.pad(vm, pad), attn_out_g[0].reshape(1, SB_WIDTH), tq,
                    _tile(s, ATTN_TILES_PER_STEP * tq) // tq)

    mhalo = jnp.concatenate([jnp.zeros((HALO - N_META, cw), F32), hm], axis=0)
    tm = _tile(s, MIX_ROWS)
    wr_t = w_router[0].T
    wr_hi = wr_t.astype(BF16)
    wr_lo = (wr_t - wr_hi.astype(F32)).astype(BF16)
    cap = -(-t // EXPERT_ROWS) * EXPERT_ROWS
    xn, eid, gates, rank, before, counts, x_rows = _mix(
        hcv.reshape(b, s, cw), mhalo, sb, x, conv_w[0], conv_b[0][None, :], conv_ln_g[0][None, :],
        conv_ln_b[0][None, :], w_out[0].astype(BF16), norm2_g[0][None, :],
        jnp.concatenate([wr_hi, wr_lo], axis=0), b_router[0][:, None], tm, _tile(tm, CONV_CHUNK), cap)

    counts = counts[:, 0].astype(jnp.int32)
    padded = (counts + EXPERT_ROWS - 1) // EXPERT_ROWS * EXPERT_ROWS
    pad_end = jnp.cumsum(padded)
    pad_start = pad_end - padded
    n_blocks = t * TOP_K // EXPERT_ROWS + N_EXPERTS
    block_row = jnp.arange(n_blocks, dtype=jnp.int32) * EXPERT_ROWS
    block_expert = jnp.minimum(jnp.sum(block_row[:, None] >= pad_end[None, :], axis=1), N_EXPERTS - 1).astype(jnp.int32)
    mine = block_expert[:, None] == jnp.arange(N_EXPERTS, dtype=jnp.int32)
    lookup = lambda table: jnp.sum(jnp.where(mine, table[None, :], 0), axis=1)
    block_off = block_row - lookup(pad_start)
    block_src = (block_expert * cap + jnp.maximum(block_off, 0)) // EXPERT_ROWS
    block_valid = jnp.clip(lookup(counts) - block_off, 0, EXPERT_ROWS)
    block_valid = jnp.where(block_row < pad_end[-1], block_valid, 0)
    n_used = (pad_end[-1:] // EXPERT_ROWS).astype(jnp.int32)
    seg = jnp.arange(N_EXPERTS, dtype=jnp.int32) * cap
    x_rows = _padfill(seg + counts, (-counts) % EXPERT_QUARTER, x_rows)
    y_rows = _experts(block_expert, block_src.astype(jnp.int32), block_valid.astype(jnp.int32), n_used, x_rows,
                      w_up[0], b_up[0], w_down[0], b_down[0])
    tc = _tile(t, COMBINE_ROWS)
    run_start = pad_start[:, None] + before[:, ::tc]
    run_end = jnp.concatenate([run_start[:, 1:], (pad_start + counts)[:, None]], axis=1)
    out = _combine(run_start, run_end - run_start, pad_start, eid, rank, gates, xn.reshape(t, d), y_rows, tc)
    return out.reshape(b, s, d)
```

```python
import functools

import jax
import jax.numpy as jnp
from jax import lax
from jax.experimental import pallas as pl
from jax.experimental.pallas import tpu as pltpu

N_META = 16
SB_HEADS = 8
SB_HEAD_DIM = 64
SB_WIDTH = SB_HEADS * SB_HEAD_DIM
CONV_KERNEL = 31
N_EXPERTS = 32
TOP_K = 4
SWIGLU_LIMIT = 7.0
SWIGLU_ALPHA = 1.702
EPS = 1e-6
F32_EXP2_UNDERFLOW = -150.0
LOG2_E = 1.4426950408889634

LANES = 128
SUBLANES = 8
INPROJ_ROWS = 1024
ATTN_TILE = 256
ATTN_TILES_PER_STEP = 16
MIX_ROWS = 1024
CONV_CHUNK = 64
COMBINE_ROWS = 256
WEIGHT_CAST_ROWS = 64
PIECE = 16
COMBINE_CHUNK = 512
COMBINE_STATIC_CHUNKS = 3
HALO = 32
EXPERT_HALF = 512
EXPERT_QUARTER = 256
EXPERT_ROWS = 2 * EXPERT_HALF
VMEM_LIMIT = 56 * 1024 * 1024

F32 = jnp.float32
BF16 = jnp.bfloat16


def _params(n_axes, vmem=VMEM_LIMIT):
    return pltpu.CompilerParams(dimension_semantics=("arbitrary",) * n_axes, vmem_limit_bytes=vmem)


def _inproj_kernel(x_ref, g1_ref, w_ref, qg_ref, kg_ref, q_ref, k_ref, v_ref, h_ref):
    x = x_ref[...]
    ms = jnp.mean(x * x, axis=-1, keepdims=True)
    n = (x * lax.rsqrt(ms + EPS) * g1_ref[...]).astype(BF16)
    lo = lax.broadcasted_iota(jnp.int32, (1, LANES), 1) < SB_HEAD_DIM

    def proj(c0, c1):
        return jnp.dot(n, w_ref[:, c0:c1], preferred_element_type=F32)

    def head_norm(acc, g_ref, out_ref, scale):
        for c in range(SB_WIDTH // LANES):
            sl = slice(c * LANES, (c + 1) * LANES)
            a = acc[:, sl]
            sq = a * a
            s_lo = jnp.sum(jnp.where(lo, sq, 0.0), axis=-1, keepdims=True)
            s_hi = jnp.sum(jnp.where(lo, 0.0, sq), axis=-1, keepdims=True)
            r = lax.rsqrt(jnp.where(lo, s_lo, s_hi) * (1.0 / SB_HEAD_DIM) + EPS)
            out_ref[:, sl] = (a * r * (g_ref[:, sl] * scale)).astype(BF16)

    head_norm(proj(0, SB_WIDTH), qg_ref, q_ref, LOG2_E * SB_HEAD_DIM ** -0.5)
    head_norm(proj(SB_WIDTH, 2 * SB_WIDTH), kg_ref, k_ref, 1.0)
    v_ref[...] = proj(2 * SB_WIDTH, 3 * SB_WIDTH).astype(BF16)
    cw = (w_ref.shape[1] - 3 * SB_WIDTH) // 2
    val = proj(3 * SB_WIDTH, 3 * SB_WIDTH + cw)
    gate = proj(3 * SB_WIDTH + cw, 3 * SB_WIDTH + 2 * cw)
    h_ref[...] = val * jax.nn.sigmoid(gate)


def _inproj(x2, g1, w_in_bf, qg, kg, tm):
    t, d = x2.shape
    cols = w_in_bf.shape[1]
    cw = (cols - 3 * SB_WIDTH) // 2
    row = lambda i: (i, 0)
    fixed = lambda i: (0, 0)
    return pl.pallas_call(
        _inproj_kernel,
        grid=(t // tm,),
        in_specs=[
            pl.BlockSpec((tm, d), row),
            pl.BlockSpec((1, d), fixed),
            pl.BlockSpec((d, cols), fixed),
            pl.BlockSpec((1, SB_WIDTH), fixed),
            pl.BlockSpec((1, SB_WIDTH), fixed),
        ],
        out_specs=[
            pl.BlockSpec((tm, SB_WIDTH), row),
            pl.BlockSpec((tm, SB_WIDTH), row),
            pl.BlockSpec((tm, SB_WIDTH), row),
            pl.BlockSpec((tm, cw), row),
        ],
        out_shape=[
            jax.ShapeDtypeStruct((t, SB_WIDTH), BF16),
            jax.ShapeDtypeStruct((t, SB_WIDTH), BF16),
            jax.ShapeDtypeStruct((t, SB_WIDTH), BF16),
            jax.ShapeDtypeStruct((t, cw), F32),
        ],
        compiler_params=_params(1),
        name="inproj",
    )(x2, g1, w_in_bf, qg, kg)


def _attn_kernel(q_ref, k_ref, v_ref, km_ref, vm_ref, g_ref, o_ref, *, tq, nsub):
    i = pl.program_id(2)
    lane = lax.broadcasted_iota(jnp.int32, (1, LANES), 1)
    lo = lane < SB_HEAD_DIM
    zero_bf = jnp.zeros((), BF16)

    def stacked_q(sub):
        q2 = q_ref[0, sub * tq:(sub + 1) * tq, :]
        return jnp.concatenate([jnp.where(lo, q2, zero_bf), jnp.where(lo, zero_bf, q2)], axis=0)

    qss = [stacked_q(sub) for sub in range(nsub)]
    row = lax.broadcasted_iota(jnp.int32, (2 * tq, tq), 0)
    col = lax.broadcasted_iota(jnp.int32, (2 * tq, tq), 1)
    causal = col < jnp.where(row >= tq, row - tq, row)
    srow = lax.broadcasted_iota(jnp.int32, (tq, tq), 0)
    scol = lax.broadcasted_iota(jnp.int32, (tq, tq), 1)
    neg_suffix = jnp.where(srow > scol, -1.0, 0.0).astype(BF16)

    def block(qs, kb, vb, acc, r, mask, neg_suffix_m):
        z = lax.dot_general(qs, kb, (((1,), (1,)), ((), ())), preferred_element_type=F32)
        m = jnp.minimum(z, 0.0)
        p = jnp.maximum(z, 0.0)
        l1p = jnp.log2(1.0 + jnp.exp2(m - p))
        log_beta = m - l1p
        neg_keep = p + l1p
        if mask is not None:
            neg_keep = jnp.where(mask, neg_keep, 0.0)
        later = jnp.dot(neg_keep.astype(BF16), neg_suffix_m, preferred_element_type=F32)
        a = jnp.exp2(log_beta + later + r)
        if mask is not None:
            a = jnp.where(mask, a, 0.0)
        res = jnp.dot(a.astype(BF16), vb, preferred_element_type=F32)
        acc = acc + jnp.where(lo, res[:tq], res[tq:])
        return acc, r - jnp.sum(neg_keep, axis=-1, keepdims=True)

    def kv_block(j):
        start = pl.multiple_of(j * tq, tq)
        return k_ref[0, pl.ds(start, tq), :], v_ref[0, pl.ds(start, tq), :]

    def live(r):
        return jnp.max(r) > F32_EXP2_UNDERFLOW

    acc0 = jnp.zeros((tq, LANES), F32)
    r0 = jnp.zeros((2 * tq, 1), F32)
    first = i * nsub

    def guaranteed(first_has_previous):
        state = []
        for sub in range(nsub):
            acc, r = block(qss[sub], *kv_block(first + sub), acc0, r0, causal, neg_suffix)
            if sub > 0 or first_has_previous:
                acc, r = block(qss[sub], *kv_block(first + sub - 1), acc, r, None, neg_suffix)
            state += [acc, r]
        return tuple(state)

    state = lax.cond(i > 0, lambda _: guaranteed(True), lambda _: guaranteed(False), 0)

    def finish(sub, acc, r):
        j = first + sub
        qs = qss[sub]

        def cond(c):
            return jnp.logical_and(c[0] <= j, c[3])

        def body(c):
            acc, r = block(qs, *kv_block(j - c[0]), c[1], c[2], None, neg_suffix)
            return c[0] + 1, acc, r, live(r)

        _, acc, r, alive = lax.while_loop(cond, body, (jnp.int32(2), acc, r, live(r)))

        def meta_block(acc):
            return block(qs, km_ref[...], vm_ref[...], acc, r, lane < N_META, neg_suffix[:LANES, :LANES])[0]

        return lax.cond(alive, meta_block, lambda acc: acc, acc)

    accs = [state[2 * sub] for sub in range(nsub)]
    rs = [state[2 * sub + 1] for sub in range(nsub)]
    solo = min(2, nsub)
    for sub in range(solo):
        accs[sub] = finish(sub, accs[sub], rs[sub])
    if nsub > solo:
        others = list(range(solo, nsub))
        joint = functools.reduce(jnp.maximum, [rs[sub] for sub in others])
        accs[solo:] = lax.cond(live(joint),
                               lambda a: tuple(finish(sub, a[n], rs[sub]) for n, sub in enumerate(others)),
                               lambda a: a, tuple(accs[solo:]))

    for sub in range(nsub):
        acc = accs[sub]
        sq = acc * acc
        s_lo = jnp.sum(jnp.where(lo, sq, 0.0), axis=-1, keepdims=True)
        s_hi = jnp.sum(jnp.where(lo, 0.0, sq), axis=-1, keepdims=True)
        rn = lax.rsqrt(jnp.where(lo, s_lo, s_hi) * (1.0 / SB_HEAD_DIM) + EPS)
        o_ref[0, sub * tq:(sub + 1) * tq, :] = (acc * rn * g_ref[...]).astype(BF16)


def _attention(q, k, v, km, vm, og, tq, nsub):
    b, s, _ = q.shape
    n_pairs = SB_WIDTH // LANES
    step = tq * nsub
    return pl.pallas_call(
        functools.partial(_attn_kernel, tq=tq, nsub=nsub),
        grid=(b, n_pairs, s // step),
        in_specs=[
            pl.BlockSpec((1, step, LANES), lambda b, p, i: (b, i, p)),
            pl.BlockSpec((1, s, LANES), lambda b, p, i: (b, 0, p)),
            pl.BlockSpec((1, s, LANES), lambda b, p, i: (b, 0, p)),
            pl.BlockSpec((LANES, LANES), lambda b, p, i: (0, p)),
            pl.BlockSpec((LANES, LANES), lambda b, p, i: (0, p)),
            pl.BlockSpec((1, LANES), lambda b, p, i: (0, p)),
        ],
        out_specs=pl.BlockSpec((1, step, LANES), lambda b, p, i: (b, i, p)),
        out_shape=jax.ShapeDtypeStruct((b, s, SB_WIDTH), BF16),
        compiler_params=_params(3),
        name="attention",
    )(q, k, v, km, vm, og)


def _mix_kernel(h_ref, halo_ref, mhalo_ref, sb_ref, x_ref, cw_ref, cb_ref, lg_ref, lb_ref, wo_ref,
                g2_ref, wr_ref, br_ref,
                xn_ref, eid_ref, gate_ref, rank_ref, before_ref, cnt_ref, rows_ref,
                win_ref, conv_ref, cv_ref, carry_ref, hn3_ref, posv_ref, pos_ref, dsem, psem, *, tm, chunk, cap):
    b = pl.program_id(0)
    i = pl.program_id(1)
    step = b * pl.num_programs(1) + i

    def pos_copy():
        return pltpu.make_async_copy(posv_ref, pos_ref, psem)

    def row_copy(t, k):
        return pltpu.make_async_copy(_row_tiles(hn3_ref, t), _row_tiles(rows_ref, pos_ref[0, k * tm + t]), dsem)

    @pl.when(step == 0)
    def _():
        carry_ref[...] = jnp.zeros_like(carry_ref)
        hn3_ref[...] = jnp.zeros_like(hn3_ref)
        posv_ref[...] = N_EXPERTS * cap + lax.broadcasted_iota(jnp.int32, posv_ref.shape, 1)
        pos_copy().start()

    @pl.when(i == 0)
    def _():
        win_ref[0:HALO, :] = mhalo_ref[...]

    @pl.when(i > 0)
    def _():
        win_ref[0:HALO, :] = halo_ref[0]

    win_ref[HALO:, :] = h_ref[0]
    pos_copy().wait()

    first_tap = HALO - (CONV_KERNEL - 1)

    def conv_chunk(c, _):
        r0 = pl.multiple_of(c * chunk, chunk)
        rows = chunk + HALO
        for lt in range(cb_ref.shape[1] // LANES):
            sl = slice(lt * LANES, (lt + 1) * LANES)
            window = win_ref[pl.ds(r0, rows), sl]
            acc = jnp.broadcast_to(cb_ref[:, sl], (chunk, LANES))
            for b in range(SUBLANES):
                shifted = window if b == 0 else pltpu.roll(window, shift=rows - b, axis=0)
                for a in range(HALO // SUBLANES + 1):
                    j = SUBLANES * a + b - first_tap
                    if 0 <= j < CONV_KERNEL:
                        acc = acc + cw_ref[j:j + 1, sl] * shifted[SUBLANES * a:SUBLANES * a + chunk, :]
            conv_ref[:, sl] = acc
        for tt in range(chunk):
            for k in range(TOP_K):
                row_copy(r0 + tt, k).start(priority=k % 2)
        acc = conv_ref[...]
        mu = jnp.mean(acc, axis=-1, keepdims=True)
        cen = acc - mu
        var = jnp.mean(cen * cen, axis=-1, keepdims=True)
        y = cen * lax.rsqrt(var + EPS) * lg_ref[...] + lb_ref[...]
        cv_ref[pl.ds(r0, chunk), :] = (y * jax.nn.sigmoid(y)).astype(BF16)
        return 0

    lax.fori_loop(0, tm // chunk, conv_chunk, 0)

    mixed = jnp.dot(sb_ref[0], wo_ref[0:SB_WIDTH, :], preferred_element_type=F32)
    mixed = mixed + jnp.dot(cv_ref[...], wo_ref[SB_WIDTH:, :], preferred_element_type=F32)
    xn = x_ref[0] + mixed
    xn_ref[0] = xn
    ms = jnp.mean(xn * xn, axis=-1, keepdims=True)
    hn = xn * lax.rsqrt(ms + EPS) * g2_ref[...]

    nt = (((1,), (1,)), ((), ()))
    hn_hi = hn.astype(BF16)
    hn_lo = (hn - hn_hi.astype(F32)).astype(BF16)
    both = lax.dot_general(wr_ref[...], hn_hi, nt, preferred_element_type=F32)
    logits = (both[:N_EXPERTS] + both[N_EXPERTS:]
              + lax.dot_general(wr_ref[0:N_EXPERTS, :], hn_lo, nt, preferred_element_type=F32) + br_ref[...])
    eidx = lax.broadcasted_iota(jnp.int32, logits.shape, 0)
    vals, idxs = [], []
    for _ in range(TOP_K):
        m = jnp.max(logits, axis=0, keepdims=True)
        sel = jnp.min(jnp.where(logits == m, eidx, N_EXPERTS), axis=0, keepdims=True)
        vals.append(m)
        idxs.append(sel)
        logits = jnp.where(eidx == sel, -jnp.inf, logits)
    exps = [jnp.exp(v - vals[0]) for v in vals]
    denom = exps[0] + exps[1] + exps[2] + exps[3]
    gate_ref[...] = jnp.concatenate([e / denom for e in exps], axis=0)
    eid_ref[...] = jnp.concatenate(idxs, axis=0)

    onehots = [(eidx == s).astype(F32) for s in idxs]
    chosen = onehots[0] + onehots[1] + onehots[2] + onehots[3]
    tr = lax.broadcasted_iota(jnp.int32, (tm, tm), 0)
    tc = lax.broadcasted_iota(jnp.int32, (tm, tm), 1)
    before = (tr < tc).astype(BF16)
    prefix = jnp.dot(chosen.astype(BF16), before, preferred_element_type=F32) + carry_ref[...]
    ranks = [jnp.sum(o * prefix, axis=0, keepdims=True).astype(jnp.int32) for o in onehots]
    rank_ref[...] = jnp.concatenate(ranks, axis=0)
    before_ref[...] = prefix.astype(jnp.int32)
    carry_ref[...] = carry_ref[...] + jnp.sum(chosen, axis=1, keepdims=True)
    cnt_ref[...] = jnp.broadcast_to(carry_ref[...], cnt_ref.shape)

    for k in range(TOP_K):
        posv_ref[:, k * tm:(k + 1) * tm] = idxs[k] * cap + ranks[k]
    pos_copy().start()
    _wait_rows(rows_ref, TOP_K * tm, dsem)
    groups = hn.shape[1] // LANES
    for c in range(groups):
        hn3_ref[pl.ds(c, tm, stride=groups), :] = hn[:, c * LANES:(c + 1) * LANES]

    @pl.when(step == pl.num_programs(0) * pl.num_programs(1) - 1)
    def _():
        pos_copy().wait()

        def issue(t, _):
            for k in range(TOP_K):
                row_copy(t, k).start(priority=k % 2)
            return 0

        lax.fori_loop(0, tm, issue, 0, unroll=2)
        _wait_rows(rows_ref, TOP_K * tm, dsem)


def _mix(hcv, mhalo, sb, x, conv_w, conv_b, ln_g, ln_b, w_out_bf, g2, wr_t, br, tm, chunk, cap):
    b, s, d = x.shape
    cw = hcv.shape[-1]
    t = b * s
    per = s // tm
    assert d == SUBLANES * LANES, d
    tile = lambda b, i: (b, i, 0)
    fixed = lambda b, i: (0, 0)
    tok = lambda b, i: (0, b * per + i)
    x_rows = N_EXPERTS * cap + TOP_K * tm
    return pl.pallas_call(
        functools.partial(_mix_kernel, tm=tm, chunk=chunk, cap=cap),
        grid=(b, per),
        in_specs=[
            pl.BlockSpec((1, tm, cw), tile),
            pl.BlockSpec((1, HALO, cw), lambda b, i: (b, jnp.maximum(i * (tm // HALO) - 1, 0), 0)),
            pl.BlockSpec((HALO, cw), fixed),
            pl.BlockSpec((1, tm, SB_WIDTH), tile),
            pl.BlockSpec((1, tm, d), tile),
            pl.BlockSpec((CONV_KERNEL, cw), fixed),
            pl.BlockSpec((1, cw), fixed),
            pl.BlockSpec((1, cw), fixed),
            pl.BlockSpec((1, cw), fixed),
            pl.BlockSpec((SB_WIDTH + cw, d), fixed),
            pl.BlockSpec((1, d), fixed),
            pl.BlockSpec((2 * N_EXPERTS, d), fixed),
            pl.BlockSpec((N_EXPERTS, 1), fixed),
        ],
        out_specs=[
            pl.BlockSpec((1, tm, d), tile),
            pl.BlockSpec((TOP_K, tm), tok),
            pl.BlockSpec((TOP_K, tm), tok),
            pl.BlockSpec((TOP_K, tm), tok),
            pl.BlockSpec((N_EXPERTS, tm), tok),
            pl.BlockSpec((N_EXPERTS, LANES), fixed),
            pl.BlockSpec(memory_space=pl.ANY),
        ],
        out_shape=[
            jax.ShapeDtypeStruct((b, s, d), F32),
            jax.ShapeDtypeStruct((TOP_K, t), jnp.int32),
            jax.ShapeDtypeStruct((TOP_K, t), F32),
            jax.ShapeDtypeStruct((TOP_K, t), jnp.int32),
            jax.ShapeDtypeStruct((N_EXPERTS, t), jnp.int32),
            jax.ShapeDtypeStruct((N_EXPERTS, LANES), F32),
            jax.ShapeDtypeStruct((x_rows * SUBLANES, LANES), F32),
        ],
        scratch_shapes=[
            pltpu.VMEM((tm + HALO, cw), F32),
            pltpu.VMEM((chunk, cw), F32),
            pltpu.VMEM((tm, cw), BF16),
            pltpu.VMEM((N_EXPERTS, 1), F32),
            pltpu.VMEM((tm * SUBLANES, LANES), F32),
            pltpu.VMEM((1, TOP_K * tm), jnp.int32),
            pltpu.SMEM((1, TOP_K * tm), jnp.int32),
            pltpu.SemaphoreType.DMA(()),
            pltpu.SemaphoreType.DMA(()),
        ],
        compiler_params=_params(2),
        name="mix_router",
    )(hcv, hcv, mhalo, sb, x, conv_w, conv_b, ln_g, ln_b, w_out_bf, g2, wr_t, br)


def _row_tiles(ref, row, n=1):
    size = n * SUBLANES if isinstance(n, int) else pl.multiple_of(n * SUBLANES, SUBLANES)
    return ref.at[pl.ds(pl.multiple_of(row * SUBLANES, SUBLANES), size)]


def _wait_rows(rows_ref, n, sem):
    pltpu.make_async_copy(_row_tiles(rows_ref, 0, n), _row_tiles(rows_ref, 0, n), sem).wait()


def _padfill_kernel(padlo_ref, padlen_ref, rows_in_ref, rows_ref, zeros_ref, zsem):
    del rows_in_ref
    zeros_ref[...] = jnp.zeros_like(zeros_ref)
    total = jnp.int32(0)
    for e in range(N_EXPERTS):
        lo, ln = padlo_ref[e], padlen_ref[e]
        total = total + ln
        for bit in range(EXPERT_HALF.bit_length() - 1):
            size = 1 << bit

            @pl.when((ln >> bit) & 1 == 1)
            def _():
                start = lo + ((ln >> (bit + 1)) << (bit + 1))
                pltpu.make_async_copy(_row_tiles(zeros_ref, 0, size), _row_tiles(rows_ref, start, size),
                                      zsem).start()

    @pl.when(total > 0)
    def _():
        _wait_rows(rows_ref, total, zsem)


def _padfill(pad_lo, pad_len, x_rows):
    smem = pl.BlockSpec(memory_space=pltpu.SMEM)
    return pl.pallas_call(
        _padfill_kernel,
        in_specs=[smem, smem, pl.BlockSpec(memory_space=pl.ANY)],
        out_specs=pl.BlockSpec(memory_space=pl.ANY),
        out_shape=jax.ShapeDtypeStruct(x_rows.shape, x_rows.dtype),
        scratch_shapes=[pltpu.VMEM((EXPERT_HALF * SUBLANES // 2, LANES), x_rows.dtype), pltpu.SemaphoreType.DMA(())],
        input_output_aliases={2: 0},
        compiler_params=pltpu.CompilerParams(vmem_limit_bytes=VMEM_LIMIT),
        name="padfill",
    )(pad_lo, pad_len, x_rows)


def _expert_kernel(be_ref, bsrc_ref, valid_ref, nused_ref, next_ref, slot_ref,
                   x_ref, wu_hbm, bu_ref, wd_hbm, bd_ref, y_ref,
                   wu_f32, wd_f32, wu_bf, wd_bf, wsem, *, cast_rows):
    del bsrc_ref, nused_ref
    j = pl.program_id(0)
    de = wd_hbm.shape[1]
    valid = valid_ref[j]

    def weight_copies(e, s):
        return (pltpu.make_async_copy(wu_hbm.at[e], wu_f32.at[s], wsem.at[0, s]),
                pltpu.make_async_copy(wd_hbm.at[e], wd_f32.at[s], wsem.at[1, s]))

    def chain(row0, nrows):
        x = jnp.concatenate([x_ref[pl.ds(row0 * SUBLANES + c, nrows, stride=SUBLANES), :]
                             for c in range(SUBLANES)], axis=1)
        up = jnp.dot(x.astype(BF16), wu_bf[...], preferred_element_type=F32) + bu_ref[0]
        glu = jnp.minimum(up[:, :de], SWIGLU_LIMIT)
        lin = jnp.clip(up[:, de:], -SWIGLU_LIMIT, SWIGLU_LIMIT)
        act = glu * jax.nn.sigmoid(SWIGLU_ALPHA * glu) * (lin + 1.0)
        y = jnp.dot(act.astype(BF16), wd_bf[...], preferred_element_type=F32) + bd_ref[0]
        y_ref[row0:row0 + nrows, :] = _pack_halves(y)

    def zero_from(row0):
        y_ref[row0:, :] = jnp.zeros((EXPERT_ROWS - row0, y_ref.shape[1]), y_ref.dtype)

    @pl.when(valid == 0)
    def _():
        y_ref[...] = jnp.zeros_like(y_ref)

    @pl.when(valid > 0)
    def _():
        prev = be_ref[jnp.maximum(j - 1, 0)]

        @pl.when(jnp.logical_or(j == 0, be_ref[j] != prev))
        def _():
            s = slot_ref[j]

            @pl.when(j == 0)
            def _():
                for c in weight_copies(be_ref[0], s):
                    c.start()

            for c in weight_copies(be_ref[j], s):
                c.wait()

            def cast(c, _):
                r0 = pl.multiple_of(c * cast_rows, cast_rows)
                wu_bf[pl.ds(r0, cast_rows), :] = wu_f32[s, pl.ds(r0, cast_rows), :].astype(BF16)
                wd_bf[pl.ds(r0, cast_rows), :] = wd_f32[s, pl.ds(r0, cast_rows), :].astype(BF16)
                return 0

            lax.fori_loop(0, wu_f32.shape[1] // cast_rows, cast, 0)

            @pl.when(next_ref[j] >= 0)
            def _():
                for c in weight_copies(next_ref[j], 1 - s):
                    c.start()

    hb, qb = EXPERT_HALF, EXPERT_QUARTER

    @pl.when(valid > hb + qb)
    def _():
        chain(0, hb)
        chain(hb, hb)

    @pl.when(jnp.logical_and(valid > hb, valid <= hb + qb))
    def _():
        chain(0, hb)
        chain(hb, qb)
        zero_from(hb + qb)

    @pl.when(jnp.logical_and(valid > qb, valid <= hb))
    def _():
        chain(0, hb)
        zero_from(hb)

    @pl.when(jnp.logical_and(valid > 0, valid <= qb))
    def _():
        chain(0, qb)
        zero_from(qb)


def _pack_halves(y):
    half = y.shape[1] // 2
    bits = lambda v: lax.bitcast_convert_type(v.astype(BF16).astype(F32), jnp.uint32)
    return (bits(y[:, half:]) & jnp.uint32(0xFFFF0000)) | (bits(y[:, :half]) >> 16)


def _unpack_halves(w):
    lo = lax.bitcast_convert_type(w << 16, F32).astype(BF16)
    hi = lax.bitcast_convert_type(w & jnp.uint32(0xFFFF0000), F32).astype(BF16)
    return lo, hi


def _experts(block_expert, block_src, block_valid, n_used, x_rows, w_up, b_up, w_down, b_down):
    n_blocks = block_expert.shape[0] + 1
    block_valid = jnp.concatenate([block_valid, jnp.zeros((1,), jnp.int32)])
    ne, d, up_cols = w_up.shape
    de = w_down.shape[1]
    steps = jnp.arange(n_blocks - 1, dtype=jnp.int32)
    used = block_valid[:-1] > 0
    first = jnp.logical_and(used, jnp.concatenate([jnp.ones((1,), bool), block_expert[1:] != block_expert[:-1]]))
    slot = (jnp.cumsum(first.astype(jnp.int32)) - 1) % 2
    later_first = jnp.logical_and(first[None, :], steps[None, :] > steps[:, None])
    next_pos = jnp.min(jnp.where(later_first, steps[None, :], n_blocks), axis=1)
    next_expert = jnp.sum(jnp.where(steps[None, :] == next_pos[:, None], block_expert[None, :], 0), axis=1)
    next_expert = jnp.where(next_pos < n_blocks, next_expert, -1).astype(jnp.int32)
    blk = lambda j, be, bs, bv, nu, nx, sl: (bs[jnp.minimum(j, nu[0] - 1)], 0)
    wsel = lambda j, be, bs, bv, nu, nx, sl: (be[jnp.minimum(j, nu[0] - 1)], 0, 0)
    grid_spec = pltpu.PrefetchScalarGridSpec(
        num_scalar_prefetch=6,
        grid=(n_blocks,),
        in_specs=[
            pl.BlockSpec((EXPERT_ROWS * SUBLANES, LANES), blk),
            pl.BlockSpec(memory_space=pl.ANY),
            pl.BlockSpec((1, 1, up_cols), wsel),
            pl.BlockSpec(memory_space=pl.ANY),
            pl.BlockSpec((1, 1, d), wsel),
        ],
        out_specs=pl.BlockSpec((EXPERT_ROWS, d // 2), lambda j, be, bs, bv, nu, nx, sl: (j, 0)),
        scratch_shapes=[pltpu.VMEM((2, d, up_cols), F32), pltpu.VMEM((2, de, d), F32),
                        pltpu.VMEM((d, up_cols), BF16), pltpu.VMEM((de, d), BF16),
                        pltpu.SemaphoreType.DMA((2, 2))],
    )
    return pl.pallas_call(
        functools.partial(_expert_kernel, cast_rows=WEIGHT_CAST_ROWS),
        grid_spec=grid_spec,
        out_shape=jax.ShapeDtypeStruct((n_blocks * EXPERT_ROWS, d // 2), jnp.uint32),
        compiler_params=_params(1),
        name="experts",
    )(block_expert, block_src, block_valid, n_used, jnp.concatenate([next_expert, -jnp.ones((1,), jnp.int32)]),
      jnp.concatenate([slot, jnp.zeros((1,), slot.dtype)]).astype(jnp.int32), x_rows, w_up,
      b_up.reshape(ne, 1, up_cols), w_down, b_down.reshape(ne, 1, d))


def _combine_plan(run_start, run_cnt, seg_start, max_pieces):
    first = run_start // SUBLANES * SUBLANES
    pieces = jnp.where(run_cnt > 0, (run_start - first + run_cnt + PIECE - 1) // PIECE, 0)
    piece_end = jnp.cumsum(pieces, axis=0)
    piece_off = piece_end - pieces
    p = jnp.arange(max_pieces, dtype=jnp.int32)
    mine = jnp.logical_and(p >= piece_off[:, :, None], p < piece_end[:, :, None])
    src = jnp.sum(jnp.where(mine, (first - piece_off * PIECE)[:, :, None], 0), axis=0) + p * PIECE
    shift = (seg_start[:, None] + piece_off * PIECE - first).T
    return (src.reshape(-1).astype(jnp.int32), piece_end[-1].astype(jnp.int32),
            shift.reshape(-1).astype(jnp.int32))


def _to_columns(x):
    pad = jnp.zeros((LANES - x.shape[0], x.shape[1]), x.dtype)
    return jnp.concatenate([x, pad], axis=0).T


def _combine_kernel(src_ref, npieces_ref, shift_ref, eid_ref, rank_ref, gate_ref, xn_ref, rows_ref, o_ref, buf,
                    sems, *, tm, max_pieces):
    i = pl.program_id(0)
    n = pl.num_programs(0)
    slot = lax.rem(i, 2)
    half = o_ref.shape[1] // 2

    def fetch(tile, s):
        def body(p, _):
            src = rows_ref.at[pl.ds(pl.multiple_of(src_ref[tile * max_pieces + p], SUBLANES), PIECE)]
            dst = buf.at[s, pl.ds(pl.multiple_of(p * PIECE, PIECE), PIECE)]
            pltpu.make_async_copy(src, dst, sems.at[s]).start()
            return 0

        lax.fori_loop(0, npieces_ref[tile], body, 0)

    @pl.when(i == 0)
    def _():
        buf[...] = jnp.zeros_like(buf)
        fetch(0, 0)

    @pl.when(i + 1 < n)
    def _():
        fetch(i + 1, 1 - slot)

    total = npieces_ref[i] * PIECE
    eid = eid_ref[...]
    shift = jnp.zeros_like(eid)
    for e in range(N_EXPERTS):
        shift = jnp.where(eid == e, shift_ref[i * N_EXPERTS + e], shift)
    staged = _to_columns((rank_ref[...] + shift).astype(F32)).astype(jnp.int32)
    gates = _to_columns(gate_ref[...])

    @pl.when(total > 0)
    def _():
        rows = pl.ds(0, pl.multiple_of(total, PIECE))
        pltpu.make_async_copy(rows_ref.at[rows], buf.at[slot, rows], sems.at[slot]).wait()

    def chunk_products(c0):
        col = lax.broadcasted_iota(jnp.int32, (tm, COMBINE_CHUNK), 1) + c0
        sel = jnp.zeros((tm, COMBINE_CHUNK), F32)
        for k in range(TOP_K):
            sel = jnp.where(staged[:, k:k + 1] == col, gates[:, k:k + 1], sel)
        sel = sel.astype(BF16)
        lo, hi = _unpack_halves(buf[slot, pl.ds(c0, COMBINE_CHUNK), :])
        return jnp.dot(sel, lo, preferred_element_type=F32), jnp.dot(sel, hi, preferred_element_type=F32)

    xn = xn_ref[...]
    acc_lo, acc_hi = xn[:, :half], xn[:, half:]
    for c in range(COMBINE_STATIC_CHUNKS):
        d_lo, d_hi = chunk_products(c * COMBINE_CHUNK)
        acc_lo, acc_hi = acc_lo + d_lo, acc_hi + d_hi
    o_ref[:, :half] = acc_lo
    o_ref[:, half:] = acc_hi

    def chunk(c, _):
        d_lo, d_hi = chunk_products(pl.multiple_of(c * COMBINE_CHUNK, COMBINE_CHUNK))
        o_ref[:, :half] += d_lo
        o_ref[:, half:] += d_hi
        return 0

    lax.fori_loop(COMBINE_STATIC_CHUNKS, (total + COMBINE_CHUNK - 1) // COMBINE_CHUNK, chunk, 0)


def _combine(run_start, run_cnt, seg_start, eid, rank, gates, xn, y_rows, tm):
    t, d = xn.shape
    n = t // tm
    cap = -(-(TOP_K * tm + 2 * PIECE * N_EXPERTS) // COMBINE_CHUNK) * COMBINE_CHUNK
    max_pieces = cap // PIECE
    src, npieces, shift = _combine_plan(run_start, run_cnt, seg_start, max_pieces)
    tok = lambda i, *_: (i, 0)
    per_k = lambda i, *_: (0, i)
    grid_spec = pltpu.PrefetchScalarGridSpec(
        num_scalar_prefetch=3,
        grid=(n,),
        in_specs=[
            pl.BlockSpec((TOP_K, tm), per_k),
            pl.BlockSpec((TOP_K, tm), per_k),
            pl.BlockSpec((TOP_K, tm), per_k),
            pl.BlockSpec((tm, d), tok),
            pl.BlockSpec(memory_space=pl.ANY),
        ],
        out_specs=pl.BlockSpec((tm, d), tok),
        scratch_shapes=[pltpu.VMEM((2, cap, d // 2), jnp.uint32), pltpu.SemaphoreType.DMA((2,))],
    )
    return pl.pallas_call(
        functools.partial(_combine_kernel, tm=tm, max_pieces=max_pieces),
        grid_spec=grid_spec,
        out_shape=jax.ShapeDtypeStruct((t, d), F32),
        compiler_params=_params(1),
        name="combine",
    )(src, npieces, shift, eid, rank, gates, xn, y_rows)


def _tile(n, want):
    t = min(n, want)
    assert n % t == 0, (n, t)
    return t


def kernel(x, meta_tokens, norm1_g, w_in, q_norm_g, k_norm_g, conv_w, conv_b, conv_ln_g, conv_ln_b,
           attn_out_g, w_out, norm2_g, w_router, b_router, w_up, b_up, w_down, b_down):
    assert norm1_g.shape[0] == 1, "single layer: meta-token rows are only keys/values and conv context"
    b, s, d = x.shape
    t = b * s
    cw = conv_w.shape[-1]

    g1 = norm1_g[0][None, :]
    w_in_bf = w_in[0].astype(BF16)
    qg = jnp.tile(q_norm_g[0], SB_HEADS)[None, :]
    kg = jnp.tile(k_norm_g[0], SB_HEADS)[None, :]

    q, k, v, hcv = _inproj(x.reshape(t, d), g1, w_in_bf, qg, kg, _tile(t, INPROJ_ROWS))
    _, km, vm, hm = _inproj(meta_tokens, g1, w_in_bf, qg, kg, N_META)
    pad = ((0, LANES - N_META), (0, 0))
    tq = _tile(s, ATTN_TILE)
    sb = _attention(q.reshape(b, s, SB_WIDTH), k.reshape(b, s, SB_WIDTH), v.reshape(b, s, SB_WIDTH),
                    jnp.pad(km, pad), jnp.pad(vm, pad), attn_out_g[0].reshape(1, SB_WIDTH), tq,
                    _tile(s, ATTN_TILES_PER_STEP * tq) // tq)

    mhalo = jnp.concatenate([jnp.zeros((HALO - N_META, cw), F32), hm], axis=0)
    tm = _tile(s, MIX_ROWS)
    wr_t = w_router[0].T
    wr_hi = wr_t.astype(BF16)
    wr_lo = (wr_t - wr_hi.astype(F32)).astype(BF16)
    cap = -(-t // EXPERT_ROWS) * EXPERT_ROWS
    xn, eid, gates, rank, before, counts, x_rows = _mix(
        hcv.reshape(b, s, cw), mhalo, sb, x, conv_w[0], conv_b[0][None, :], conv_ln_g[0][None, :],
        conv_ln_b[0][None, :], w_out[0].astype(BF16), norm2_g[0][None, :],
        jnp.concatenate([wr_hi, wr_lo], axis=0), b_router[0][:, None], tm, _tile(tm, CONV_CHUNK), cap)

    counts = counts[:, 0].astype(jnp.int32)
    padded = (counts + EXPERT_ROWS - 1) // EXPERT_ROWS * EXPERT_ROWS
    pad_end = jnp.cumsum(padded)
    pad_start = pad_end - padded
    n_blocks = t * TOP_K // EXPERT_ROWS + N_EXPERTS
    block_row = jnp.arange(n_blocks, dtype=jnp.int32) * EXPERT_ROWS
    block_expert = jnp.minimum(jnp.sum(block_row[:, None] >= pad_end[None, :], axis=1), N_EXPERTS - 1).astype(jnp.int32)
    mine = block_expert[:, None] == jnp.arange(N_EXPERTS, dtype=jnp.int32)
    lookup = lambda table: jnp.sum(jnp.where(mine, table[None, :], 0), axis=1)
    block_off = block_row - lookup(pad_start)
    block_src = (block_expert * cap + jnp.maximum(block_off, 0)) // EXPERT_ROWS
    block_valid = jnp.clip(lookup(counts) - block_off, 0, EXPERT_ROWS)
    block_valid = jnp.where(block_row < pad_end[-1], block_valid, 0)
    n_used = (pad_end[-1:] // EXPERT_ROWS).astype(jnp.int32)
    seg = jnp.arange(N_EXPERTS, dtype=jnp.int32) * cap
    x_rows = _padfill(seg + counts, (-counts) % EXPERT_QUARTER, x_rows)
    y_rows = _experts(block_expert, block_src.astype(jnp.int32), block_valid.astype(jnp.int32), n_used, x_rows,
                      w_up[0], b_up[0], w_down[0], b_down[0])
    tc = _tile(t, COMBINE_ROWS)
    run_start = pad_start[:, None] + before[:, ::tc]
    run_end = jnp.concatenate([run_start[:, 1:], (pad_start + counts)[:, None]], axis=1)
    out = _combine(run_start, run_end - run_start, pad_start, eid, rank, gates, xn.reshape(t, d), y_rows, tc)
    return out.reshape(b, s, d)
```
